```python
import math
import jax, jax.numpy as jnp
from jax import lax
import numpy as np

D_MODEL = 1024
BATCH = 8
SEQ = 4096
DEPTH = 2

N_A_LAYERS = DEPTH // 2
N_B_LAYERS = DEPTH - N_A_LAYERS
CONV_WIDTH = 31
N_HEADS = 16
N_KV_HEADS = 4
HEAD_DIM = 64
Q_PER_KV = N_HEADS // N_KV_HEADS
WINDOW = 128
BLOCK = 128
ROPE_DIM = HEAD_DIM // 4
ROPE_THETA = 500000.0
D_FF = 4 * D_MODEL
PLE_DIM = 256
DEEPNORM_ALPHA = (2 * DEPTH) ** 0.25
DEEPNORM_BETA = (8 * DEPTH) ** -0.25
LN_EPS = 1e-5

kernel_name = "yoco_conformer_swa_sink_deepnorm"


def layer_norm(x, g, b):
    xf = x.astype(jnp.float32)
    mu = jnp.mean(xf, axis=-1, keepdims=True)
    var = jnp.mean(jnp.square(xf - mu), axis=-1, keepdims=True)
    y = (xf - mu) * lax.rsqrt(var + LN_EPS)
    return (y * g.astype(jnp.float32) + b.astype(jnp.float32)).astype(x.dtype)


def rope_tables(seq_len):
    pos = jnp.arange(seq_len, dtype=jnp.float32)
    inv_freq = ROPE_THETA ** (-jnp.arange(0, ROPE_DIM, 2, dtype=jnp.float32) / ROPE_DIM)
    ang = pos[:, None] * inv_freq[None, :]
    return jnp.cos(ang)[:, None, :], jnp.sin(ang)[:, None, :]


def partial_rope(t, cos, sin):
    half = ROPE_DIM // 2
    x1 = t[..., :half].astype(jnp.float32)
    x2 = t[..., half:ROPE_DIM].astype(jnp.float32)
    rot = jnp.concatenate([x1 * cos - x2 * sin, x2 * cos + x1 * sin], axis=-1).astype(t.dtype)
    return jnp.concatenate([rot, t[..., ROPE_DIM:]], axis=-1)


def conformer_conv(x, w_in, b_in, w_dw, b_dw, ln_g, ln_b, w_out, b_out):
    h = x @ w_in + b_in
    a, gate = jnp.split(h, 2, axis=-1)
    h = a * jax.nn.sigmoid(gate)
    h = lax.conv_general_dilated(
        h, w_dw[:, None, :], window_strides=(1,), padding=[(CONV_WIDTH - 1, 0)],
        dimension_numbers=("NWC", "WIO", "NWC"), feature_group_count=D_MODEL) + b_dw
    h = jax.nn.silu(layer_norm(h, ln_g, ln_b))
    return h @ w_out + b_out


def shared_banded_kv(x, w_k, w_v, cos, sin):
    B, T, _ = x.shape
    nb = T // BLOCK
    k = partial_rope((x @ w_k).reshape(B, T, N_KV_HEADS, HEAD_DIM), cos, sin)
    v = (x @ w_v).reshape(B, T, N_KV_HEADS, HEAD_DIM)

    def band(t):
        tb = t.reshape(B, nb, BLOCK, N_KV_HEADS, HEAD_DIM)
        prev = jnp.pad(tb, ((0, 0), (1, 0), (0, 0), (0, 0), (0, 0)))[:, :-1]
        return jnp.concatenate([prev, tb], axis=2)

    return band(k), band(v)


def band_mask(nb):
    n = jnp.arange(nb)[:, None, None]
    a = jnp.arange(BLOCK)[None, :, None]
    s = jnp.arange(2 * BLOCK)[None, None, :]
    qpos = n * BLOCK + a
    kpos = (n - 1) * BLOCK + s
    rel = qpos - kpos
    return (kpos >= 0) & (rel >= 0) & (rel < WINDOW)


def swa_sink_attention(x, w_q, sinks, w_o, kk, vv, cos, sin):
    B, T, _ = x.shape
    nb = T // BLOCK
    q = partial_rope((x @ w_q).reshape(B, T, N_HEADS, HEAD_DIM), cos, sin)
    q = q.reshape(B, nb, BLOCK, N_KV_HEADS, Q_PER_KV, HEAD_DIM)
    scores = jnp.einsum("bnqkgd,bnskd->bnkgqs", q, kk,
                        preferred_element_type=jnp.float32) * (1.0 / math.sqrt(HEAD_DIM))
    mask = band_mask(nb)[None, :, None, None]
    scores = jnp.where(mask, scores, -jnp.inf)
    sink = sinks.astype(jnp.float32).reshape(1, 1, N_KV_HEADS, Q_PER_KV, 1, 1)
    lse = jnp.logaddexp(jax.nn.logsumexp(scores, axis=-1, keepdims=True), sink)
    probs = jnp.exp(scores - lse).astype(vv.dtype)
    out = jnp.einsum("bnkgqs,bnskd->bnqkgd", probs, vv)
    return out.reshape(B, T, N_HEADS * HEAD_DIM) @ w_o


def sq_relu_mlp(x, w_up, w_down):
    return jnp.square(jax.nn.relu(x @ w_up)) @ w_down


def _fwd_setup_inputs(seed: int = 0) -> dict:
    key = jax.random.key(seed)
    ks = jax.random.split(key, 32)
    f32 = jnp.float32

    def nrm(k, shape, scale):
        return jax.random.normal(k, shape, f32) * scale

    def gain(k, shape):
        return 1.0 + 0.02 * jax.random.normal(k, shape, f32)

    D = D_MODEL
    HD = N_HEADS * HEAD_DIM
    KVD = N_KV_HEADS * HEAD_DIM
    return {
        "x": nrm(ks[0], (BATCH, SEQ, D), 1.0),
        "p": nrm(ks[1], (DEPTH, BATCH, SEQ, PLE_DIM), 1.0),
        "conv_w_in": nrm(ks[2], (N_A_LAYERS, D, 2 * D), D ** -0.5),
        "conv_b_in": nrm(ks[3], (N_A_LAYERS, 2 * D), 0.02),
        "conv_w_dw": nrm(ks[4], (N_A_LAYERS, CONV_WIDTH, D), CONV_WIDTH ** -0.5),
        "conv_b_dw": nrm(ks[5], (N_A_LAYERS, D), 0.02),
        "conv_ln_g": gain(ks[6], (N_A_LAYERS, D)),
        "conv_ln_b": nrm(ks[7], (N_A_LAYERS, D), 0.02),
        "conv_w_out": nrm(ks[8], (N_A_LAYERS, D, D), D ** -0.5 * DEEPNORM_BETA),
        "conv_b_out": nrm(ks[9], (N_A_LAYERS, D), 0.02),
        "kv_w_k": nrm(ks[10], (D, KVD), D ** -0.5),
        "kv_w_v": nrm(ks[11], (D, KVD), D ** -0.5),
        "attn_w_q": nrm(ks[12], (N_B_LAYERS, D, HD), D ** -0.5),
        "attn_sinks": nrm(ks[13], (N_B_LAYERS, N_HEADS), 0.5),
        "attn_w_o": nrm(ks[14], (N_B_LAYERS, HD, D), HD ** -0.5 * DEEPNORM_BETA),
        "mix_ln_g": gain(ks[15], (DEPTH, D)),
        "mix_ln_b": nrm(ks[16], (DEPTH, D), 0.02),
        "mlp_w_up": nrm(ks[17], (DEPTH, D, D_FF), D ** -0.5),
        "mlp_w_down": nrm(ks[18], (DEPTH, D_FF, D), D_FF ** -0.5 * DEEPNORM_BETA),
        "mlp_ln_g": gain(ks[19], (DEPTH, D)),
        "mlp_ln_b": nrm(ks[20], (DEPTH, D), 0.02),
        "ple_w_proj": nrm(ks[21], (DEPTH, PLE_DIM, D), PLE_DIM ** -0.5),
        "ple_w_gate": nrm(ks[22], (DEPTH, D, D), D ** -0.5),
    }


def _fwd_reference(x, p, conv_w_in, conv_b_in, conv_w_dw, conv_b_dw, conv_ln_g, conv_ln_b,
              conv_w_out, conv_b_out, kv_w_k, kv_w_v, attn_w_q, attn_sinks, attn_w_o,
              mix_ln_g, mix_ln_b, mlp_w_up, mlp_w_down, mlp_ln_g, mlp_ln_b,
              ple_w_proj, ple_w_gate):
    T = x.shape[1]
    cos, sin = rope_tables(T)
    kk = vv = None
    for i in range(DEPTH):
        if i < N_A_LAYERS:
            y = conformer_conv(x, conv_w_in[i], conv_b_in[i], conv_w_dw[i], conv_b_dw[i],
                               conv_ln_g[i], conv_ln_b[i], conv_w_out[i], conv_b_out[i])
        else:
            if i == N_A_LAYERS:
                kk, vv = shared_banded_kv(x, kv_w_k, kv_w_v, cos, sin)
            j = i - N_A_LAYERS
            y = swa_sink_attention(x, attn_w_q[j], attn_sinks[j], attn_w_o[j], kk, vv, cos, sin)
        x = layer_norm(DEEPNORM_ALPHA * x + y, mix_ln_g[i], mix_ln_b[i])
        x = layer_norm(DEEPNORM_ALPHA * x + sq_relu_mlp(x, mlp_w_up[i], mlp_w_down[i]),
                       mlp_ln_g[i], mlp_ln_b[i])
        x = x + (p[i] @ ple_w_proj[i]) * jax.nn.sigmoid(x @ ple_w_gate[i])
    return x


import jax as _jax
import jax.numpy as _jnp

TWIN_FORMAT = 'train_step'
FWD_PARAMS = ['x', 'p', 'conv_w_in', 'conv_b_in', 'conv_w_dw', 'conv_b_dw', 'conv_ln_g', 'conv_ln_b', 'conv_w_out', 'conv_b_out', 'kv_w_k', 'kv_w_v', 'attn_w_q', 'attn_sinks', 'attn_w_o', 'mix_ln_g', 'mix_ln_b', 'mlp_w_up', 'mlp_w_down', 'mlp_ln_g', 'mlp_ln_b', 'ple_w_proj', 'ple_w_gate']
TWIN_WEIGHTS = ['conv_w_in', 'conv_b_in', 'conv_w_dw', 'conv_b_dw', 'conv_ln_g', 'conv_ln_b', 'conv_w_out', 'conv_b_out', 'kv_w_k', 'kv_w_v', 'attn_w_q', 'attn_sinks', 'attn_w_o', 'mix_ln_g', 'mix_ln_b', 'mlp_w_up', 'mlp_w_down', 'mlp_ln_g', 'mlp_ln_b', 'ple_w_proj', 'ple_w_gate']
TWIN_DIFF_INPUT = 'x'
TWIN_INPUTS = ['x', 'p', 'conv_w_in', 'conv_b_in', 'conv_w_dw', 'conv_b_dw', 'conv_ln_g', 'conv_ln_b', 'conv_w_out', 'conv_b_out', 'kv_w_k', 'kv_w_v', 'attn_w_q', 'attn_sinks', 'attn_w_o', 'mix_ln_g', 'mix_ln_b', 'mlp_w_up', 'mlp_w_down', 'mlp_ln_g', 'mlp_ln_b', 'ple_w_proj', 'ple_w_gate', 'loss_target', 'm_conv_w_in', 'm_conv_b_in', 'm_conv_w_dw', 'm_conv_b_dw', 'm_conv_ln_g', 'm_conv_ln_b', 'm_conv_w_out', 'm_conv_b_out', 'm_kv_w_k', 'm_kv_w_v', 'm_attn_w_q', 'm_attn_sinks', 'm_attn_w_o', 'm_mix_ln_g', 'm_mix_ln_b', 'm_mlp_w_up', 'm_mlp_w_down', 'm_mlp_ln_g', 'm_mlp_ln_b', 'm_ple_w_proj', 'm_ple_w_gate', 'v_conv_w_in', 'v_conv_b_in', 'v_conv_w_dw', 'v_conv_b_dw', 'v_conv_ln_g', 'v_conv_ln_b', 'v_conv_w_out', 'v_conv_b_out', 'v_kv_w_k', 'v_kv_w_v', 'v_attn_w_q', 'v_attn_sinks', 'v_attn_w_o', 'v_mix_ln_g', 'v_mix_ln_b', 'v_mlp_w_up', 'v_mlp_w_down', 'v_mlp_ln_g', 'v_mlp_ln_b', 'v_ple_w_proj', 'v_ple_w_gate']
TWIN_OUTPUTS = ['loss', 'grad_x', 'grad_conv_w_in', 'grad_conv_b_in', 'grad_conv_w_dw', 'grad_conv_b_dw', 'grad_conv_ln_g', 'grad_conv_ln_b', 'grad_conv_w_out', 'grad_conv_b_out', 'grad_kv_w_k', 'grad_kv_w_v', 'grad_attn_w_q', 'grad_attn_sinks', 'grad_attn_w_o', 'grad_mix_ln_g', 'grad_mix_ln_b', 'grad_mlp_w_up', 'grad_mlp_w_down', 'grad_mlp_ln_g', 'grad_mlp_ln_b', 'grad_ple_w_proj', 'grad_ple_w_gate', 'delta_conv_w_in', 'delta_conv_b_in', 'delta_conv_w_dw', 'delta_conv_b_dw', 'delta_conv_ln_g', 'delta_conv_ln_b', 'delta_conv_w_out', 'delta_conv_b_out', 'delta_kv_w_k', 'delta_kv_w_v', 'delta_attn_w_q', 'delta_attn_sinks', 'delta_attn_w_o', 'delta_mix_ln_g', 'delta_mix_ln_b', 'delta_mlp_w_up', 'delta_mlp_w_down', 'delta_mlp_ln_g', 'delta_mlp_ln_b', 'delta_ple_w_proj', 'delta_ple_w_gate', 'new_m_conv_w_in', 'new_m_conv_b_in', 'new_m_conv_w_dw', 'new_m_conv_b_dw', 'new_m_conv_ln_g', 'new_m_conv_ln_b', 'new_m_conv_w_out', 'new_m_conv_b_out', 'new_m_kv_w_k', 'new_m_kv_w_v', 'new_m_attn_w_q', 'new_m_attn_sinks', 'new_m_attn_w_o', 'new_m_mix_ln_g', 'new_m_mix_ln_b', 'new_m_mlp_w_up', 'new_m_mlp_w_down', 'new_m_mlp_ln_g', 'new_m_mlp_ln_b', 'new_m_ple_w_proj', 'new_m_ple_w_gate', 'new_v_conv_w_in', 'new_v_conv_b_in', 'new_v_conv_w_dw', 'new_v_conv_b_dw', 'new_v_conv_ln_g', 'new_v_conv_ln_b', 'new_v_conv_w_out', 'new_v_conv_b_out', 'new_v_kv_w_k', 'new_v_kv_w_v', 'new_v_attn_w_q', 'new_v_attn_sinks', 'new_v_attn_w_o', 'new_v_mix_ln_g', 'new_v_mix_ln_b', 'new_v_mlp_w_up', 'new_v_mlp_w_down', 'new_v_mlp_ln_g', 'new_v_mlp_ln_b', 'new_v_ple_w_proj', 'new_v_ple_w_gate']
TWIN_LEAF_KINDS = {'loss': 'loss', 'grad_x': 'grad_x', 'grad_conv_w_in': 'grad_w', 'grad_conv_b_in': 'grad_w', 'grad_conv_w_dw': 'grad_w', 'grad_conv_b_dw': 'grad_w', 'grad_conv_ln_g': 'grad_w', 'grad_conv_ln_b': 'grad_w', 'grad_conv_w_out': 'grad_w', 'grad_conv_b_out': 'grad_w', 'grad_kv_w_k': 'grad_w', 'grad_kv_w_v': 'grad_w', 'grad_attn_w_q': 'grad_w', 'grad_attn_sinks': 'grad_w', 'grad_attn_w_o': 'grad_w', 'grad_mix_ln_g': 'grad_w', 'grad_mix_ln_b': 'grad_w', 'grad_mlp_w_up': 'grad_w', 'grad_mlp_w_down': 'grad_w', 'grad_mlp_ln_g': 'grad_w', 'grad_mlp_ln_b': 'grad_w', 'grad_ple_w_proj': 'grad_w', 'grad_ple_w_gate': 'grad_w', 'delta_conv_w_in': 'delta_w', 'delta_conv_b_in': 'delta_w', 'delta_conv_w_dw': 'delta_w', 'delta_conv_b_dw': 'delta_w', 'delta_conv_ln_g': 'delta_w', 'delta_conv_ln_b': 'delta_w', 'delta_conv_w_out': 'delta_w', 'delta_conv_b_out': 'delta_w', 'delta_kv_w_k': 'delta_w', 'delta_kv_w_v': 'delta_w', 'delta_attn_w_q': 'delta_w', 'delta_attn_sinks': 'delta_w', 'delta_attn_w_o': 'delta_w', 'delta_mix_ln_g': 'delta_w', 'delta_mix_ln_b': 'delta_w', 'delta_mlp_w_up': 'delta_w', 'delta_mlp_w_down': 'delta_w', 'delta_mlp_ln_g': 'delta_w', 'delta_mlp_ln_b': 'delta_w', 'delta_ple_w_proj': 'delta_w', 'delta_ple_w_gate': 'delta_w', 'new_m_conv_w_in': 'new_m', 'new_m_conv_b_in': 'new_m', 'new_m_conv_w_dw': 'new_m', 'new_m_conv_b_dw': 'new_m', 'new_m_conv_ln_g': 'new_m', 'new_m_conv_ln_b': 'new_m', 'new_m_conv_w_out': 'new_m', 'new_m_conv_b_out': 'new_m', 'new_m_kv_w_k': 'new_m', 'new_m_kv_w_v': 'new_m', 'new_m_attn_w_q': 'new_m', 'new_m_attn_sinks': 'new_m', 'new_m_attn_w_o': 'new_m', 'new_m_mix_ln_g': 'new_m', 'new_m_mix_ln_b': 'new_m', 'new_m_mlp_w_up': 'new_m', 'new_m_mlp_w_down': 'new_m', 'new_m_mlp_ln_g': 'new_m', 'new_m_mlp_ln_b': 'new_m', 'new_m_ple_w_proj': 'new_m', 'new_m_ple_w_gate': 'new_m', 'new_v_conv_w_in': 'new_v', 'new_v_conv_b_in': 'new_v', 'new_v_conv_w_dw': 'new_v', 'new_v_conv_b_dw': 'new_v', 'new_v_conv_ln_g': 'new_v', 'new_v_conv_ln_b': 'new_v', 'new_v_conv_w_out': 'new_v', 'new_v_conv_b_out': 'new_v', 'new_v_kv_w_k': 'new_v', 'new_v_kv_w_v': 'new_v', 'new_v_attn_w_q': 'new_v', 'new_v_attn_sinks': 'new_v', 'new_v_attn_w_o': 'new_v', 'new_v_mix_ln_g': 'new_v', 'new_v_mix_ln_b': 'new_v', 'new_v_mlp_w_up': 'new_v', 'new_v_mlp_w_down': 'new_v', 'new_v_mlp_ln_g': 'new_v', 'new_v_mlp_ln_b': 'new_v', 'new_v_ple_w_proj': 'new_v', 'new_v_ple_w_gate': 'new_v'}


def _forward(args):
    return _fwd_reference(*[args[k] for k in FWD_PARAMS])


def _output_shape():
    out = _jax.eval_shape(lambda: _forward(_fwd_setup_inputs(0)))
    return out.shape, out.dtype

N_MICROBATCH = 1
ADAM_LR = 0.001
ADAM_B1 = 0.9
ADAM_B2 = 0.999
ADAM_EPS = 1e-08
ADAM_WD = 0.01
ADAM_STEP = 10
PER_EXAMPLE_BATCH_AXIS = {'x': 0, 'p': 1, 'loss_target': 0}
SHARED_INPUTS = []
_WEIGHT_DTYPES = {'conv_w_in': _jnp.float32, 'conv_b_in': _jnp.float32, 'conv_w_dw': _jnp.float32, 'conv_b_dw': _jnp.float32, 'conv_ln_g': _jnp.float32, 'conv_ln_b': _jnp.float32, 'conv_w_out': _jnp.float32, 'conv_b_out': _jnp.float32, 'kv_w_k': _jnp.float32, 'kv_w_v': _jnp.float32, 'attn_w_q': _jnp.float32, 'attn_sinks': _jnp.float32, 'attn_w_o': _jnp.float32, 'mix_ln_g': _jnp.float32, 'mix_ln_b': _jnp.float32, 'mlp_w_up': _jnp.float32, 'mlp_w_down': _jnp.float32, 'mlp_ln_g': _jnp.float32, 'mlp_ln_b': _jnp.float32, 'ple_w_proj': _jnp.float32, 'ple_w_gate': _jnp.float32}
MOMENT_SCALE = {'conv_w_in': 3.304219e-02, 'conv_b_in': 3.174272e-01, 'conv_w_dw': 5.569767e-02, 'conv_b_dw': 8.118616e-01, 'conv_ln_g': 3.065029e-01, 'conv_ln_b': 4.815984e-01, 'conv_w_out': 3.296857e-01, 'conv_b_out': 1.931180e+00, 'kv_w_k': 3.987897e-02, 'kv_w_v': 2.929378e-01, 'attn_w_q': 1.980564e-02, 'attn_sinks': 1.389930e-02, 'attn_w_o': 2.719597e-01, 'mix_ln_g': 7.559018e-01, 'mix_ln_b': 2.958796e+00, 'mlp_w_up': 5.783181e-02, 'mlp_w_down': 9.003780e-01, 'mlp_ln_g': 2.340815e+01, 'mlp_ln_b': 6.251816e+00, 'ple_w_proj': 2.984339e-01, 'ple_w_gate': 4.947698e-01}


def _to_microbatches(a, axis):
    t = _jnp.moveaxis(a, axis, 0)
    t = t.reshape((N_MICROBATCH, t.shape[0] // N_MICROBATCH) + t.shape[1:])
    return _jnp.moveaxis(t, 1, axis + 1)


def setup_inputs(seed: int = 0) -> dict:
    inp = _fwd_setup_inputs(seed)
    key = _jax.random.fold_in(_jax.random.key(seed), 7919)
    shape, _ = _output_shape()
    out = dict(inp)
    out["loss_target"] = _jax.random.normal(_jax.random.fold_in(key, 0), shape, _jnp.float32)
    for i, name in enumerate(TWIN_WEIGHTS):
        w = inp[name].astype(_jnp.float32)
        if MOMENT_SCALE is None:
            s = _jnp.sqrt(_jnp.mean(_jnp.square(w)) + 1e-30)
        else:
            s = MOMENT_SCALE[name]
        km, kv = _jax.random.split(_jax.random.fold_in(key, i + 1))
        out[name] = w
        out["m_" + name] = s * _jax.random.normal(km, w.shape, _jnp.float32)
        out["v_" + name] = (s * s) * _jax.random.uniform(kv, w.shape, _jnp.float32, 0.5, 1.5)
    if N_MICROBATCH > 1:
        for name, axis in PER_EXAMPLE_BATCH_AXIS.items():
            out[name] = _to_microbatches(out[name], axis)
    return {'x': out['x'], 'p': out['p'], 'conv_w_in': out['conv_w_in'], 'conv_b_in': out['conv_b_in'], 'conv_w_dw': out['conv_w_dw'], 'conv_b_dw': out['conv_b_dw'], 'conv_ln_g': out['conv_ln_g'], 'conv_ln_b': out['conv_ln_b'], 'conv_w_out': out['conv_w_out'], 'conv_b_out': out['conv_b_out'], 'kv_w_k': out['kv_w_k'], 'kv_w_v': out['kv_w_v'], 'attn_w_q': out['attn_w_q'], 'attn_sinks': out['attn_sinks'], 'attn_w_o': out['attn_w_o'], 'mix_ln_g': out['mix_ln_g'], 'mix_ln_b': out['mix_ln_b'], 'mlp_w_up': out['mlp_w_up'], 'mlp_w_down': out['mlp_w_down'], 'mlp_ln_g': out['mlp_ln_g'], 'mlp_ln_b': out['mlp_ln_b'], 'ple_w_proj': out['ple_w_proj'], 'ple_w_gate': out['ple_w_gate'], 'loss_target': out['loss_target'], 'm_conv_w_in': out['m_conv_w_in'], 'm_conv_b_in': out['m_conv_b_in'], 'm_conv_w_dw': out['m_conv_w_dw'], 'm_conv_b_dw': out['m_conv_b_dw'], 'm_conv_ln_g': out['m_conv_ln_g'], 'm_conv_ln_b': out['m_conv_ln_b'], 'm_conv_w_out': out['m_conv_w_out'], 'm_conv_b_out': out['m_conv_b_out'], 'm_kv_w_k': out['m_kv_w_k'], 'm_kv_w_v': out['m_kv_w_v'], 'm_attn_w_q': out['m_attn_w_q'], 'm_attn_sinks': out['m_attn_sinks'], 'm_attn_w_o': out['m_attn_w_o'], 'm_mix_ln_g': out['m_mix_ln_g'], 'm_mix_ln_b': out['m_mix_ln_b'], 'm_mlp_w_up': out['m_mlp_w_up'], 'm_mlp_w_down': out['m_mlp_w_down'], 'm_mlp_ln_g': out['m_mlp_ln_g'], 'm_mlp_ln_b': out['m_mlp_ln_b'], 'm_ple_w_proj': out['m_ple_w_proj'], 'm_ple_w_gate': out['m_ple_w_gate'], 'v_conv_w_in': out['v_conv_w_in'], 'v_conv_b_in': out['v_conv_b_in'], 'v_conv_w_dw': out['v_conv_w_dw'], 'v_conv_b_dw': out['v_conv_b_dw'], 'v_conv_ln_g': out['v_conv_ln_g'], 'v_conv_ln_b': out['v_conv_ln_b'], 'v_conv_w_out': out['v_conv_w_out'], 'v_conv_b_out': out['v_conv_b_out'], 'v_kv_w_k': out['v_kv_w_k'], 'v_kv_w_v': out['v_kv_w_v'], 'v_attn_w_q': out['v_attn_w_q'], 'v_attn_sinks': out['v_attn_sinks'], 'v_attn_w_o': out['v_attn_w_o'], 'v_mix_ln_g': out['v_mix_ln_g'], 'v_mix_ln_b': out['v_mix_ln_b'], 'v_mlp_w_up': out['v_mlp_w_up'], 'v_mlp_w_down': out['v_mlp_w_down'], 'v_mlp_ln_g': out['v_mlp_ln_g'], 'v_mlp_ln_b': out['v_mlp_ln_b'], 'v_ple_w_proj': out['v_ple_w_proj'], 'v_ple_w_gate': out['v_ple_w_gate']}


def _loss(weights, diff, rest, loss_target):
    with _jax.named_scope("forward"):
        args = {**rest, TWIN_DIFF_INPUT: diff, **{k: w.astype(_WEIGHT_DTYPES[k]) for k, w in weights.items()}}
        y = _forward(args)
    with _jax.named_scope("loss_head"):
        err = _jnp.square(y.astype(_jnp.float32) - loss_target)
        return 0.5 * _jnp.sum(_jnp.mean(err, axis=-1)) if err.ndim else 0.5 * err


def _adamw(w, g, m, v):
    m = ADAM_B1 * m + (1.0 - ADAM_B1) * g
    v = ADAM_B2 * v + (1.0 - ADAM_B2) * _jnp.square(g)
    m_hat = m / (1.0 - ADAM_B1 ** ADAM_STEP)
    v_hat = v / (1.0 - ADAM_B2 ** ADAM_STEP)
    delta = -ADAM_LR * (m_hat / (_jnp.sqrt(v_hat) + ADAM_EPS) + ADAM_WD * w)
    return delta, m, v


def reference(x, p, conv_w_in, conv_b_in, conv_w_dw, conv_b_dw, conv_ln_g, conv_ln_b, conv_w_out, conv_b_out, kv_w_k, kv_w_v, attn_w_q, attn_sinks, attn_w_o, mix_ln_g, mix_ln_b, mlp_w_up, mlp_w_down, mlp_ln_g, mlp_ln_b, ple_w_proj, ple_w_gate, loss_target, m_conv_w_in, m_conv_b_in, m_conv_w_dw, m_conv_b_dw, m_conv_ln_g, m_conv_ln_b, m_conv_w_out, m_conv_b_out, m_kv_w_k, m_kv_w_v, m_attn_w_q, m_attn_sinks, m_attn_w_o, m_mix_ln_g, m_mix_ln_b, m_mlp_w_up, m_mlp_w_down, m_mlp_ln_g, m_mlp_ln_b, m_ple_w_proj, m_ple_w_gate, v_conv_w_in, v_conv_b_in, v_conv_w_dw, v_conv_b_dw, v_conv_ln_g, v_conv_ln_b, v_conv_w_out, v_conv_b_out, v_kv_w_k, v_kv_w_v, v_attn_w_q, v_attn_sinks, v_attn_w_o, v_mix_ln_g, v_mix_ln_b, v_mlp_w_up, v_mlp_w_down, v_mlp_ln_g, v_mlp_ln_b, v_ple_w_proj, v_ple_w_gate):
    given = dict(x=x, p=p, conv_w_in=conv_w_in, conv_b_in=conv_b_in, conv_w_dw=conv_w_dw, conv_b_dw=conv_b_dw, conv_ln_g=conv_ln_g, conv_ln_b=conv_ln_b, conv_w_out=conv_w_out, conv_b_out=conv_b_out, kv_w_k=kv_w_k, kv_w_v=kv_w_v, attn_w_q=attn_w_q, attn_sinks=attn_sinks, attn_w_o=attn_w_o, mix_ln_g=mix_ln_g, mix_ln_b=mix_ln_b, mlp_w_up=mlp_w_up, mlp_w_down=mlp_w_down, mlp_ln_g=mlp_ln_g, mlp_ln_b=mlp_ln_b, ple_w_proj=ple_w_proj, ple_w_gate=ple_w_gate, loss_target=loss_target, m_conv_w_in=m_conv_w_in, m_conv_b_in=m_conv_b_in, m_conv_w_dw=m_conv_w_dw, m_conv_b_dw=m_conv_b_dw, m_conv_ln_g=m_conv_ln_g, m_conv_ln_b=m_conv_ln_b, m_conv_w_out=m_conv_w_out, m_conv_b_out=m_conv_b_out, m_kv_w_k=m_kv_w_k, m_kv_w_v=m_kv_w_v, m_attn_w_q=m_attn_w_q, m_attn_sinks=m_attn_sinks, m_attn_w_o=m_attn_w_o, m_mix_ln_g=m_mix_ln_g, m_mix_ln_b=m_mix_ln_b, m_mlp_w_up=m_mlp_w_up, m_mlp_w_down=m_mlp_w_down, m_mlp_ln_g=m_mlp_ln_g, m_mlp_ln_b=m_mlp_ln_b, m_ple_w_proj=m_ple_w_proj, m_ple_w_gate=m_ple_w_gate, v_conv_w_in=v_conv_w_in, v_conv_b_in=v_conv_b_in, v_conv_w_dw=v_conv_w_dw, v_conv_b_dw=v_conv_b_dw, v_conv_ln_g=v_conv_ln_g, v_conv_ln_b=v_conv_ln_b, v_conv_w_out=v_conv_w_out, v_conv_b_out=v_conv_b_out, v_kv_w_k=v_kv_w_k, v_kv_w_v=v_kv_w_v, v_attn_w_q=v_attn_w_q, v_attn_sinks=v_attn_sinks, v_attn_w_o=v_attn_w_o, v_mix_ln_g=v_mix_ln_g, v_mix_ln_b=v_mix_ln_b, v_mlp_w_up=v_mlp_w_up, v_mlp_w_down=v_mlp_w_down, v_mlp_ln_g=v_mlp_ln_g, v_mlp_ln_b=v_mlp_ln_b, v_ple_w_proj=v_ple_w_proj, v_ple_w_gate=v_ple_w_gate)
    weights = {n: given[n] for n in TWIN_WEIGHTS}
    shared = {n: given[n] for n in SHARED_INPUTS}
    per_example = {n: given[n] for n in ['x', 'p']}
    grad_fn = _jax.value_and_grad(_loss, argnums=(0, 1))

    def one_microbatch(ex, loss_target):
        ex = dict(ex)
        diff = ex.pop(TWIN_DIFF_INPUT)
        return grad_fn(weights, diff, {**shared, **ex}, loss_target)

    if N_MICROBATCH == 1:
        loss, (grad_w, grad_x) = one_microbatch(per_example, given["loss_target"])
    else:
        def body(carry, xs):
            loss_sum, grad_sum = carry
            l_k, (gw_k, gx_k) = one_microbatch(xs[0], xs[1])
            with _jax.named_scope("update"):
                return (loss_sum + l_k, _jax.tree.map(_jnp.add, grad_sum, gw_k)), gx_k

        init = (_jnp.zeros((), _jnp.float32), _jax.tree.map(_jnp.zeros_like, weights))
        (loss, grad_w), grad_x = _jax.lax.scan(body, init, (per_example, given["loss_target"]))
    with _jax.named_scope("update"):
        delta_w, new_m, new_v = {}, {}, {}
        for n in TWIN_WEIGHTS:
            delta_w[n], new_m[n], new_v[n] = _adamw(weights[n], grad_w[n], given["m_" + n], given["v_" + n])
    return (loss, grad_x, *[grad_w[n] for n in TWIN_WEIGHTS], *[delta_w[n] for n in TWIN_WEIGHTS],
            *[new_m[n] for n in TWIN_WEIGHTS], *[new_v[n] for n in TWIN_WEIGHTS])
```

```python
import functools

import jax
import jax.numpy as jnp
from jax import lax
from jax.experimental import pallas as pl
from jax.experimental.pallas import tpu as pltpu

F32 = jnp.float32
BF16 = jnp.bfloat16
NS = 4
HEAD = 64
BLK = 128
ROPE = 16
ROPE_THETA = 500000.0
LN_EPS = 1e-5
NEG = -1e30
HALO = 32
ADAM_LR, ADAM_B1, ADAM_B2, ADAM_EPS, ADAM_WD, ADAM_STEP = 0.001, 0.9, 0.999, 1e-08, 0.01, 10
MESH = pl.DeviceIdType.MESH
ANY = pl.BlockSpec(memory_space=pl.ANY)
NT = (((1,), (1,)), ((), ()))
TN = (((0,), (0,)), ((), ()))


def _pc(body, name, grid, in_specs, out_specs, out_shape, scratch=(), sem=None, vmem=56, **kw):
    return pl.pallas_call(
        body, name=name, grid=grid, in_specs=in_specs, out_specs=out_specs, out_shape=out_shape,
        scratch_shapes=list(scratch),
        compiler_params=pltpu.CompilerParams(dimension_semantics=sem, vmem_limit_bytes=vmem * 2 ** 20), **kw)


def _rows(tm, n):
    return pl.BlockSpec((tm, n), lambda i: (i, 0))


def _const(shape):
    return pl.BlockSpec(shape, lambda *_: (0,) * len(shape))


def _wspec(w):
    buf, off, rows = w
    assert off % rows == 0
    return pl.BlockSpec((NS, rows, buf.shape[2]), lambda *_: (0, off // rows, 0))


def _sds(shape, dtype):
    return jax.ShapeDtypeStruct(shape, dtype)


def _tile(t):
    return min(256, t)


def _sigmoid(x):
    return 1.0 / (1.0 + jnp.exp(-x))


def _ln_stats(w):
    mu = jnp.mean(w, axis=-1, keepdims=True)
    xc = w - mu
    var = jnp.mean(xc * xc, axis=-1, keepdims=True)
    rstd = lax.rsqrt(var + LN_EPS)
    return xc * rstd, rstd


def _ln_bwd(dy, w, g):
    xhat, rstd = _ln_stats(w)
    dxhat = dy * g
    m1 = jnp.mean(dxhat, axis=-1, keepdims=True)
    m2 = jnp.mean(dxhat * xhat, axis=-1, keepdims=True)
    dw = rstd * (dxhat - m1 - xhat * m2)
    return dw, jnp.sum(dy * xhat, axis=0, keepdims=True), jnp.sum(dy, axis=0, keepdims=True)


def _acc_rows(ref, val, first):
    @pl.when(first)
    def _():
        ref[...] = val

    @pl.when(jnp.logical_not(first))
    def _():
        ref[...] += val


def conv_in_fwd(xb, w_in, b_in):
    T, D = xb.shape
    nw = w_in[0].shape[2]
    tm = _tile(T)

    def body(x_ref, w_ref, b_ref, h_ref):
        x = x_ref[...]
        for j in range(NS):
            sl = slice(j * nw, (j + 1) * nw)
            h_ref[:, sl] = (jnp.dot(x, w_ref[j], preferred_element_type=F32) + b_ref[:, sl]).astype(BF16)

    return _pc(body, "conv_in_fwd", (T // tm,), [_rows(tm, D), _wspec(w_in), _const((1, NS * nw))],
               _rows(tm, NS * nw), _sds((T, NS * nw), BF16), sem=("parallel",))(xb, w_in[0], b_in)


def dwconv_fwd(h, w_dw, b_dw, ln_g, ln_b, taps):
    T = h.shape[0]
    C = h.shape[1] // 2
    tq = _tile(T)
    nh = tq // HALO
    off = HALO - (taps - 1)

    def body(a_ref, g_ref, ap_ref, gp_ref, w_ref, bdw_ref, lg_ref, lb_ref, cv_ref, s_ref, scr):
        i = pl.program_id(0)
        scr[HALO:HALO + tq, :] = a_ref[...].astype(F32) * _sigmoid(g_ref[...].astype(F32))
        up = ap_ref[...].astype(F32) * _sigmoid(gp_ref[...].astype(F32))
        scr[0:HALO, :] = jnp.where(i > 0, up, 0.0)
        cv_ref[...] = jnp.broadcast_to(bdw_ref[...], (tq, C))
        for j in range(taps):
            cv_ref[...] += w_ref[j:j + 1, :] * scr[off + j:off + j + tq, :]
        xhat, _ = _ln_stats(cv_ref[...])
        ln = xhat * lg_ref[...] + lb_ref[...]
        s_ref[...] = (ln * _sigmoid(ln)).astype(BF16)

    prev = lambda col: pl.BlockSpec((HALO, C), lambda i: (jnp.maximum(i * nh - 1, 0), col))
    cur = lambda col: pl.BlockSpec((tq, C), lambda i: (i, col))
    return _pc(body, "dwconv_fwd", (T // tq,),
               [cur(0), cur(1), prev(0), prev(1), _const((HALO, C)), _const((1, C)), _const((1, C)), _const((1, C))],
               [_rows(tq, C), _rows(tq, C)], [_sds((T, C), F32), _sds((T, C), BF16)],
               scratch=[pltpu.VMEM((HALO + tq, C), F32)], sem=("parallel",))(h, h, h, h, w_dw, b_dw, ln_g, ln_b)


def mm_res_ln(a, w, res, g, b, alpha, bias, name):
    T, K = a.shape
    ks = K // NS
    D = res.shape[1]
    tm = _tile(T)

    def body(*refs):
        a_ref, w_ref, res_ref, g_ref, b_ref = refs[:5]
        n = 5
        if bias is not None:
            bias_ref = refs[5]
            n = 6
        pre_ref, xo_ref, xb_ref = refs[n:n + 3]
        acc = jnp.dot(a_ref[:, 0:ks], w_ref[0], preferred_element_type=F32)
        for j in range(1, NS):
            acc = acc + jnp.dot(a_ref[:, j * ks:(j + 1) * ks], w_ref[j], preferred_element_type=F32)
        if bias is not None:
            acc = acc + bias_ref[...]
        pre = alpha * res_ref[...] + acc
        xhat, _ = _ln_stats(pre)
        xo = xhat * g_ref[...] + b_ref[...]
        pre_ref[...] = pre
        xo_ref[...] = xo
        xb_ref[...] = xo.astype(BF16)

    ins = [_rows(tm, K), _wspec(w), _rows(tm, D), _const((1, D)), _const((1, D))]
    args = [a, w[0], res, g, b]
    if bias is not None:
        ins.append(_const((1, D)))
        args.append(bias)
    return _pc(body, name, (T // tm,), ins, [_rows(tm, D)] * 3, [_sds((T, D), F32), _sds((T, D), F32), _sds((T, D), BF16)],
               sem=("parallel",))(*args)


def mlp_up_fwd(xb, w_up, name):
    T, D = xb.shape
    fs = w_up[0].shape[2]
    tm = _tile(T)

    def body(x_ref, w_ref, r_ref):
        x = x_ref[...]
        for j in range(NS):
            m = jnp.maximum(jnp.dot(x, w_ref[j], preferred_element_type=F32), 0.0)
            r_ref[:, j * fs:(j + 1) * fs] = (m * m).astype(BF16)

    return _pc(body, name, (T // tm,), [_rows(tm, D), _wspec(w_up)], _rows(tm, NS * fs), _sds((T, NS * fs), BF16),
               sem=("parallel",))(xb, w_up[0])


def ple_fwd(x, xb, p, layer, w_proj, w_gate, target, name):
    T, D = x.shape
    P = p.shape[2]
    ds = D // NS
    tm = _tile(T)
    last = target is not None

    def body(*refs):
        x_ref, xb_ref, p_ref, wp_ref, wg_ref = refs[:5]
        n = 5
        if last:
            t_ref = refs[5]
            n = 6
        o_ref, o2_ref, pp_ref, gl_ref = refs[n:n + 4]
        gl = jnp.dot(xb_ref[:, 0:ds], wg_ref[0], preferred_element_type=F32)
        for j in range(1, NS):
            gl = gl + jnp.dot(xb_ref[:, j * ds:(j + 1) * ds], wg_ref[j], preferred_element_type=F32)
        gl_ref[...] = gl.astype(BF16)
        sg = _sigmoid(gl)
        pb = p_ref[...].astype(BF16)
        sq = jnp.zeros((1, 1), F32)
        for j in range(NS):
            sl = slice(j * ds, (j + 1) * ds)
            pp = jnp.dot(pb, wp_ref[j], preferred_element_type=F32)
            pp_ref[:, sl] = pp.astype(BF16)
            out = x_ref[:, sl] + pp * sg[:, sl]
            if last:
                err = out - t_ref[:, sl]
                o_ref[:, sl] = err * (1.0 / D)
                e2 = jnp.sum(err * err, axis=0, keepdims=True)
                sq = sq + jnp.sum(e2, axis=1, keepdims=True)
            else:
                o_ref[:, sl] = out
                o2_ref[:, sl] = out.astype(BF16)
        if last:
            _acc_rows(o2_ref, jnp.broadcast_to(sq * (0.5 / D), (8, 128)), pl.program_id(0) == 0)

    ins = [_rows(tm, D), _rows(tm, D), pl.BlockSpec((None, tm, P), lambda i: (layer, i, 0)), _wspec(w_proj), _wspec(w_gate)]
    args = [x, xb, p, w_proj[0], w_gate[0]]
    if last:
        ins.append(_rows(tm, D))
        args.append(target)
        outs = [_rows(tm, D), _const((8, 128)), _rows(tm, D), _rows(tm, D)]
        shapes = [_sds((T, D), F32), _sds((8, 128), F32), _sds((T, D), BF16), _sds((T, D), BF16)]
    else:
        outs = [_rows(tm, D)] * 4
        shapes = [_sds((T, D), F32), _sds((T, D), BF16), _sds((T, D), BF16), _sds((T, D), BF16)]
    return _pc(body, name, (T // tm,), ins, outs, shapes, sem=("arbitrary",) if last else ("parallel",))(*args)


def _rope(x, cs_ref, sign):
    c = cs_ref[0]
    s = cs_ref[1] * sign
    lane = lax.broadcasted_iota(jnp.int32, c.shape, 1)
    first = (lane % HEAD) < (ROPE // 2)
    outs = []
    for gq in range(x.shape[1] // 128):
        xg = x[:, gq * 128:(gq + 1) * 128]
        sw = jnp.where(first, pltpu.roll(xg, 128 - ROPE // 2, 1), pltpu.roll(xg, ROPE // 2, 1))
        outs.append(xg * c + sw * s)
    return outs


def qkv_fwd(xb, w_q, w_k, w_v, cs):
    T, D = xb.shape
    ds = D // NS
    HD, KVD = w_q[0].shape[2], w_k[0].shape[2]
    tm = _tile(T)
    scale = 1.0 / (HEAD ** 0.5)

    def body(x_ref, wq_ref, wk_ref, wv_ref, cs_ref, q_ref, k_ref, v_ref):
        def proj(w_ref):
            acc = jnp.dot(x_ref[:, 0:ds], w_ref[0], preferred_element_type=F32)
            for j in range(1, NS):
                acc = acc + jnp.dot(x_ref[:, j * ds:(j + 1) * ds], w_ref[j], preferred_element_type=F32)
            return acc

        for gq, val in enumerate(_rope(proj(wq_ref), cs_ref, 1.0)):
            q_ref[:, gq * 128:(gq + 1) * 128] = (val * scale).astype(BF16)
        for gq, val in enumerate(_rope(proj(wk_ref), cs_ref, 1.0)):
            k_ref[:, gq * 128:(gq + 1) * 128] = val.astype(BF16)
        v_ref[...] = proj(wv_ref).astype(BF16)

    cs_spec = pl.BlockSpec((2, tm, 128), lambda i: (0, i, 0))
    return _pc(body, "qkv_fwd", (T // tm,), [_rows(tm, D), _wspec(w_q), _wspec(w_k), _wspec(w_v), cs_spec],
               [_rows(tm, HD), _rows(tm, KVD), _rows(tm, KVD)],
               [_sds((T, HD), BF16), _sds((T, KVD), BF16), _sds((T, KVD), BF16)], sem=("parallel",))(
                   xb, w_q[0], w_k[0], w_v[0], cs)


def _band_mask(n):
    row = lax.broadcasted_iota(jnp.int32, (BLK, 2 * BLK), 0)
    col = lax.broadcasted_iota(jnp.int32, (BLK, 2 * BLK), 1)
    return (col > row) & (col <= row + BLK) & ((col >= BLK) | (n > 0))


def _probs(q, k2, valid, sink):
    s = jnp.where(valid, lax.dot_general(q, k2, NT, preferred_element_type=F32), NEG)
    m = jnp.maximum(jnp.max(s, axis=-1, keepdims=True), sink)
    e = jnp.exp(s - m)
    es = jnp.exp(sink - m)
    den = jnp.sum(e, axis=-1, keepdims=True) + es
    return e / den, es / den


def attn_fwd(q, k, v, sinks):
    NH, T, _ = q.shape
    NKV = k.shape[0]
    G = NH // NKV

    def body(s_ref, q_ref, kc_ref, kp_ref, vc_ref, vp_ref, o_ref):
        valid = _band_mask(pl.program_id(0))
        for kh in range(NKV):
            k2 = jnp.concatenate([kp_ref[kh], kc_ref[kh]], axis=0)
            v2 = jnp.concatenate([vp_ref[kh], vc_ref[kh]], axis=0)
            for gq in range(G):
                hh = kh * G + gq
                p, _ = _probs(q_ref[hh], k2, valid, s_ref[0, hh])
                o_ref[hh] = jnp.dot(p.astype(BF16), v2, preferred_element_type=F32).astype(BF16)

    cur = lambda nh: pl.BlockSpec((nh, BLK, HEAD), lambda n: (0, n, 0))
    prev = lambda nh: pl.BlockSpec((nh, BLK, HEAD), lambda n: (0, jnp.maximum(n - 1, 0), 0))
    return _pc(body, "attn_fwd", (T // BLK,),
               [pl.BlockSpec(memory_space=pltpu.SMEM), cur(NH), cur(NKV), prev(NKV), cur(NKV), prev(NKV)],
               cur(NH), _sds((NH, T, HEAD), BF16), sem=("parallel",))(sinks, q, k, k, v, v)


def ple_bwd(dxo, pp, gl, w_gate, name):
    T, D = dxo.shape
    ds = D // NS
    tm = _tile(T)

    def body(d_ref, pp_ref, gl_ref, wg_ref, dpp_ref, dgl_ref, dx_ref):
        d = d_ref[...]
        sg = _sigmoid(gl_ref[...].astype(F32))
        dpp_ref[...] = (d * sg).astype(BF16)
        dgl = (d * pp_ref[...].astype(F32) * sg * (1.0 - sg)).astype(BF16)
        dgl_ref[...] = dgl
        for j in range(NS):
            sl = slice(j * ds, (j + 1) * ds)
            dx_ref[:, sl] = d_ref[:, sl] + lax.dot_general(dgl, wg_ref[j], NT, preferred_element_type=F32)

    return _pc(body, name, (T // tm,), [_rows(tm, D)] * 3 + [_wspec(w_gate)], [_rows(tm, D)] * 3,
               [_sds((T, D), BF16), _sds((T, D), BF16), _sds((T, D), F32)], sem=("parallel",))(dxo, pp, gl, w_gate[0])


def mlp_bwd1(dy, pre, g, r, w_down, name):
    T, D = dy.shape
    fs = w_down[2]
    tm = _tile(T)

    def body(dy_ref, pre_ref, g_ref, r_ref, w_ref, dw_ref, dwb_ref, dm_ref, dg_ref, db_ref):
        dw, dg, db = _ln_bwd(dy_ref[...], pre_ref[...], g_ref[...])
        first = pl.program_id(0) == 0
        _acc_rows(dg_ref, dg, first)
        _acc_rows(db_ref, db, first)
        dwb = dw.astype(BF16)
        dw_ref[...] = dw
        dwb_ref[...] = dwb
        for j in range(NS):
            sl = slice(j * fs, (j + 1) * fs)
            dr = lax.dot_general(dwb, w_ref[j], NT, preferred_element_type=F32)
            dm_ref[:, sl] = (dr * (2.0 * jnp.sqrt(r_ref[:, sl].astype(F32)))).astype(BF16)

    return _pc(body, name, (T // tm,), [_rows(tm, D), _rows(tm, D), _const((1, D)), _rows(tm, NS * fs), _wspec(w_down)],
               [_rows(tm, D), _rows(tm, D), _rows(tm, NS * fs), _const((1, D)), _const((1, D))],
               [_sds((T, D), F32), _sds((T, D), BF16), _sds((T, NS * fs), BF16), _sds((1, D), F32), _sds((1, D), F32)],
               sem=("arbitrary",))(dy, pre, g, r, w_down[0])


def mlp_bwd2(dpre, dm, w_up, alpha, pre_mix, g_mix, w_mix, name):
    T, D = dpre.shape
    fs = w_up[0].shape[2]
    ms = w_mix[2]
    tm = _tile(T)

    def body(dp_ref, dm_ref, wu_ref, pre_ref, g_ref, wm_ref, dw_ref, dwb_ref, do_ref, dg_ref, db_ref, dc_ref):
        dy = alpha * dp_ref[...]
        for j in range(NS):
            dy = dy + lax.dot_general(dm_ref[:, j * fs:(j + 1) * fs], wu_ref[j], NT, preferred_element_type=F32)
        dw, dg, db = _ln_bwd(dy, pre_ref[...], g_ref[...])
        first = pl.program_id(0) == 0
        _acc_rows(dg_ref, dg, first)
        _acc_rows(db_ref, db, first)
        _acc_rows(dc_ref, jnp.sum(dw, axis=0, keepdims=True), first)
        dwb = dw.astype(BF16)
        dw_ref[...] = dw
        dwb_ref[...] = dwb
        for j in range(NS):
            do_ref[:, j * ms:(j + 1) * ms] = lax.dot_general(dwb, wm_ref[j], NT, preferred_element_type=F32).astype(BF16)

    return _pc(body, name, (T // tm,),
               [_rows(tm, D), _rows(tm, NS * fs), _wspec(w_up), _rows(tm, D), _const((1, D)), _wspec(w_mix)],
               [_rows(tm, D), _rows(tm, D), _rows(tm, NS * ms), _const((1, D)), _const((1, D)), _const((1, D))],
               [_sds((T, D), F32), _sds((T, D), BF16), _sds((T, NS * ms), BF16)] + [_sds((1, D), F32)] * 3,
               sem=("arbitrary",))(dpre, dm, w_up[0], pre_mix, g_mix, w_mix[0])


def attn_bwd(q, k, v, do, sinks):
    NH, T, _ = q.shape
    NKV = k.shape[0]
    G = NH // NKV
    nb = T // BLK

    def body(s_ref, q_ref, do_ref, kc_ref, kp_ref, vc_ref, vp_ref, dq_ref, dk_ref, dv_ref, ds_ref, ck, cv):
        n = pl.program_id(0)

        @pl.when(n == 0)
        def _():
            ck[...] = jnp.zeros_like(ck)
            cv[...] = jnp.zeros_like(cv)
            ds_ref[...] = jnp.zeros_like(ds_ref)

        @pl.when(n < nb)
        def _():
            valid = _band_mask(n)
            for kh in range(NKV):
                k2 = jnp.concatenate([kp_ref[kh], kc_ref[kh]], axis=0)
                v2 = jnp.concatenate([vp_ref[kh], vc_ref[kh]], axis=0)
                dk2 = jnp.zeros((2 * BLK, HEAD), F32)
                dv2 = jnp.zeros((2 * BLK, HEAD), F32)
                for gq in range(G):
                    hh = kh * G + gq
                    qh = q_ref[hh]
                    doh = do_ref[hh]
                    p, ps = _probs(qh, k2, valid, s_ref[0, hh])
                    dp = lax.dot_general(doh, v2, NT, preferred_element_type=F32)
                    delta = jnp.sum(p * dp, axis=-1, keepdims=True)
                    dsb = (p * (dp - delta)).astype(BF16)
                    ds_ref[hh:hh + 1, :] += jnp.broadcast_to(-jnp.sum(ps * delta, axis=0, keepdims=True), (1, 128))
                    dq_ref[hh] = jnp.dot(dsb, k2, preferred_element_type=F32)
                    dk2 = dk2 + lax.dot_general(dsb, qh, TN, preferred_element_type=F32)
                    dv2 = dv2 + lax.dot_general(p.astype(BF16), doh, TN, preferred_element_type=F32)
                dk_ref[kh] = ck[kh] + dk2[0:BLK]
                dv_ref[kh] = cv[kh] + dv2[0:BLK]
                ck[kh] = dk2[BLK:2 * BLK]
                cv[kh] = dv2[BLK:2 * BLK]

        @pl.when(n == nb)
        def _():
            dk_ref[...] = ck[...]
            dv_ref[...] = cv[...]

    qcur = pl.BlockSpec((NH, BLK, HEAD), lambda n: (0, jnp.minimum(n, nb - 1), 0))
    kcur = pl.BlockSpec((NKV, BLK, HEAD), lambda n: (0, jnp.minimum(n, nb - 1), 0))
    kprev = pl.BlockSpec((NKV, BLK, HEAD), lambda n: (0, jnp.maximum(n - 1, 0), 0))
    return _pc(body, "attn_bwd", (nb + 1,),
               [pl.BlockSpec(memory_space=pltpu.SMEM), qcur, qcur, kcur, kprev, kcur, kprev],
               [qcur, kprev, kprev, _const((NH, 128))],
               [_sds((NH, T, HEAD), F32), _sds((NKV, T, HEAD), F32), _sds((NKV, T, HEAD), F32), _sds((NH, 128), F32)],
               scratch=[pltpu.VMEM((NKV, BLK, HEAD), F32), pltpu.VMEM((NKV, BLK, HEAD), F32)],
               sem=("arbitrary",))(sinks, q, do, k, k, v, v)


def qkv_bwd(dq, dk, dv, dpre_mix, w_q, w_k, w_v, cs, alpha):
    T, HD = dq.shape
    KVD = dk.shape[1]
    D = dpre_mix.shape[1]
    ds = D // NS
    tm = _tile(T)
    scale = 1.0 / (HEAD ** 0.5)

    def body(dq_ref, dk_ref, dv_ref, dp_ref, wq_ref, wk_ref, wv_ref, cs_ref, dqb_ref, dkb_ref, dvb_ref, dx_ref):
        for gq, val in enumerate(_rope(dq_ref[...], cs_ref, -1.0)):
            dqb_ref[:, gq * 128:(gq + 1) * 128] = (val * scale).astype(BF16)
        for gq, val in enumerate(_rope(dk_ref[...], cs_ref, -1.0)):
            dkb_ref[:, gq * 128:(gq + 1) * 128] = val.astype(BF16)
        dvb_ref[...] = dv_ref[...].astype(BF16)
        dqb, dkb, dvb = dqb_ref[...], dkb_ref[...], dvb_ref[...]
        for j in range(NS):
            sl = slice(j * ds, (j + 1) * ds)
            dx_ref[:, sl] = (alpha * dp_ref[:, sl]
                             + lax.dot_general(dqb, wq_ref[j], NT, preferred_element_type=F32)
                             + lax.dot_general(dkb, wk_ref[j], NT, preferred_element_type=F32)
                             + lax.dot_general(dvb, wv_ref[j], NT, preferred_element_type=F32))

    cs_spec = pl.BlockSpec((2, tm, 128), lambda i: (0, i, 0))
    return _pc(body, "qkv_bwd", (T // tm,),
               [_rows(tm, HD), _rows(tm, KVD), _rows(tm, KVD), _rows(tm, D), _wspec(w_q), _wspec(w_k), _wspec(w_v), cs_spec],
               [_rows(tm, HD), _rows(tm, KVD), _rows(tm, KVD), _rows(tm, D)],
               [_sds((T, HD), BF16), _sds((T, KVD), BF16), _sds((T, KVD), BF16), _sds((T, D), F32)],
               sem=("parallel",))(dq, dk, dv, dpre_mix, w_q[0], w_k[0], w_v[0], cs)


def conv_mid_bwd(ds, cv, ln_g, ln_b):
    T, C = cv.shape
    tm = _tile(T)

    def body(ds_ref, cv_ref, g_ref, b_ref, dcv_ref, dg_ref, db_ref, dc_ref):
        xhat, _ = _ln_stats(cv_ref[...])
        ln = xhat * g_ref[...] + b_ref[...]
        sg = _sigmoid(ln)
        dl = ds_ref[...].astype(F32) * (sg * (1.0 + ln * (1.0 - sg)))
        dcv, dg, db = _ln_bwd(dl, cv_ref[...], g_ref[...])
        first = pl.program_id(0) == 0
        _acc_rows(dg_ref, dg, first)
        _acc_rows(db_ref, db, first)
        _acc_rows(dc_ref, jnp.sum(dcv, axis=0, keepdims=True), first)
        dcv_ref[...] = dcv

    return _pc(body, "conv_mid_bwd", (T // tm,), [_rows(tm, C), _rows(tm, C), _const((1, C)), _const((1, C))],
               [_rows(tm, C), _const((1, C)), _const((1, C)), _const((1, C))],
               [_sds((T, C), F32)] + [_sds((1, C), F32)] * 3, sem=("arbitrary",))(ds, cv, ln_g, ln_b)


def dwconv_bwd(dcv, h, w_dw, taps):
    T, C = dcv.shape
    tq = _tile(T)
    nh = tq // HALO
    nblk = T // tq
    off = HALO - (taps - 1)

    def body(d_ref, dn_ref, a_ref, g_ref, ap_ref, gp_ref, w_ref, dh_ref, dw_ref, dbi_ref, su, sd, du):
        i = pl.program_id(0)
        a = a_ref[...].astype(F32)
        sg = _sigmoid(g_ref[...].astype(F32))
        su[HALO:HALO + tq, :] = a * sg
        up = ap_ref[...].astype(F32) * _sigmoid(gp_ref[...].astype(F32))
        su[0:HALO, :] = jnp.where(i > 0, up, 0.0)
        sd[0:tq, :] = d_ref[...]
        sd[tq:tq + HALO, :] = jnp.where(i < nblk - 1, dn_ref[...], 0.0)

        @pl.when(i == 0)
        def _():
            dw_ref[...] = jnp.zeros_like(dw_ref)

        du[...] = jnp.zeros_like(du)
        for j in range(taps):
            du[...] += w_ref[j:j + 1, :] * sd[taps - 1 - j:taps - 1 - j + tq, :]
            dw_ref[j:j + 1, :] += jnp.sum(d_ref[...] * su[off + j:off + j + tq, :], axis=0, keepdims=True)
        da = du[...] * sg
        dgt = du[...] * a * sg * (1.0 - sg)
        dh_ref[:, 0:C] = da.astype(BF16)
        dh_ref[:, C:2 * C] = dgt.astype(BF16)
        first = i == 0
        _acc_rows(dbi_ref.at[:, 0:C], jnp.sum(da, axis=0, keepdims=True), first)
        _acc_rows(dbi_ref.at[:, C:2 * C], jnp.sum(dgt, axis=0, keepdims=True), first)

    prev = lambda col: pl.BlockSpec((HALO, C), lambda i: (jnp.maximum(i * nh - 1, 0), col))
    nxt = pl.BlockSpec((HALO, C), lambda i: (jnp.minimum((i + 1) * nh, T // HALO - 1), 0))
    cur = lambda col: pl.BlockSpec((tq, C), lambda i: (i, col))
    return _pc(body, "dwconv_bwd", (nblk,),
               [cur(0), nxt, cur(0), cur(1), prev(0), prev(1), _const((HALO, C))],
               [_rows(tq, 2 * C), _const((HALO, C)), _const((1, 2 * C))],
               [_sds((T, 2 * C), BF16), _sds((HALO, C), F32), _sds((1, 2 * C), F32)],
               scratch=[pltpu.VMEM((HALO + tq, C), F32), pltpu.VMEM((HALO + tq, C), F32), pltpu.VMEM((tq, C), F32)],
               sem=("arbitrary",))(dcv, dcv, h, h, h, h, w_dw)


def conv_in_bwd(dh, dpre_mix, w_in, alpha):
    T, D = dpre_mix.shape
    nw = w_in[0].shape[2]
    tm = _tile(T)

    def body(dh_ref, dp_ref, w_ref, dx_ref):
        acc = alpha * dp_ref[...]
        for j in range(NS):
            acc = acc + lax.dot_general(dh_ref[:, j * nw:(j + 1) * nw], w_ref[j], NT, preferred_element_type=F32)
        dx_ref[...] = acc

    return _pc(body, "conv_in_bwd", (T // tm,), [_rows(tm, NS * nw), _rows(tm, D), _wspec(w_in)], _rows(tm, D),
               _sds((T, D), F32), sem=("parallel",))(dh, dpre_mix, w_in[0])


def wgrad(a, b, row_sharded, name):
    T, Ka = a.shape
    Nb = b.shape[1]
    tt = min(512, T)
    nt = T // tt
    if row_sharded:
        ka, tn = Ka // NS, min(Nb, 1024)
        grid = (NS, Nb // tn, nt)
        out_shape = (NS, ka, Nb)
        out_map = lambda i, j, t: (i, 0, j)
    else:
        ka, tn = min(Ka, 1024), Nb // NS
        grid = (Ka // ka, NS, nt)
        out_shape = (NS, Ka, tn)
        out_map = lambda i, j, t: (j, i, 0)

    def body(a_ref, b_ref, o_ref, acc):
        t = pl.program_id(2)
        av = a_ref[...]
        if av.dtype != BF16:
            av = av.astype(BF16)
        d = lax.dot_general(av, b_ref[...], TN, preferred_element_type=F32)

        @pl.when(t == 0)
        def _():
            acc[...] = d

        @pl.when(t > 0)
        def _():
            acc[...] += d

        @pl.when(t == nt - 1)
        def _():
            o_ref[...] = acc[...].astype(BF16)

    return _pc(body, name, grid,
               [pl.BlockSpec((tt, ka), lambda i, j, t: (t, i)), pl.BlockSpec((tt, tn), lambda i, j, t: (t, j))],
               pl.BlockSpec((None, ka, tn), out_map), _sds(out_shape, BF16),
               scratch=[pltpu.VMEM((ka, tn), F32)], sem=("parallel", "parallel", "arbitrary"))(a, b)


def adamw(w, g, m, v, name):
    R, W = w.shape
    tr = R
    for cand in (512, 256, 128, 64, 32, 16, 8):
        if R % cand == 0:
            tr = cand
            break
    c1 = 1.0 - ADAM_B1 ** ADAM_STEP
    c2 = 1.0 - ADAM_B2 ** ADAM_STEP

    def body(w_ref, g_ref, m_ref, v_ref, d_ref, mo_ref, vo_ref):
        gv = g_ref[...]
        mn = ADAM_B1 * m_ref[...] + (1.0 - ADAM_B1) * gv
        vn = ADAM_B2 * v_ref[...] + (1.0 - ADAM_B2) * (gv * gv)
        mo_ref[...] = mn
        vo_ref[...] = vn
        d_ref[...] = -ADAM_LR * ((mn / c1) / (jnp.sqrt(vn / c2) + ADAM_EPS) + ADAM_WD * w_ref[...])

    return _pc(body, name, (R // tr,), [_rows(tr, W)] * 4, [_rows(tr, W)] * 3, [_sds((R, W), F32)] * 3,
               sem=("parallel",))(w, g, m, v)


def _to_heads(t):
    T, n = t.shape
    return t.reshape(T, n // HEAD, HEAD).transpose(1, 0, 2)


def _from_heads(t):
    nh, T, _ = t.shape
    return t.transpose(1, 0, 2).reshape(T, nh * HEAD)


def _rope_tables(T):
    pos = jnp.arange(T, dtype=F32)
    inv_freq = ROPE_THETA ** (-jnp.arange(0, ROPE, 2, dtype=F32) / ROPE)
    ang = pos[:, None] * inv_freq[None, :]
    cos, sin = jnp.cos(ang), jnp.sin(ang)
    pad = HEAD - ROPE
    c = jnp.concatenate([cos, cos, jnp.ones((T, pad), F32)], axis=1)
    s = jnp.concatenate([-sin, sin, jnp.zeros((T, pad), F32)], axis=1)
    return jnp.stack([jnp.tile(c, (1, 128 // HEAD)), jnp.tile(s, (1, 128 // HEAD))])


def _local_step(x, p, target, W, small):
    T, D = x.shape
    depth = small["mix_ln_g"].shape[0]
    alpha = float((2 * depth) ** 0.25)
    taps = small["taps"]
    row = lambda a, i: a[i:i + 1]
    cs = _rope_tables(T)

    x0b = x.astype(BF16)
    h = conv_in_fwd(x0b, W["conv_w_in"], small["conv_b_in"])
    cv, s = dwconv_fwd(h, small["conv_w_dw"], small["conv_b_dw"], small["conv_ln_g"], small["conv_ln_b"], taps)
    pre_mix0, x1, x1b = mm_res_ln(s, W["conv_w_out"], x, row(small["mix_ln_g"], 0), row(small["mix_ln_b"], 0), alpha,
                                  small["conv_b_out"], "conv_out_fwd")
    r0 = mlp_up_fwd(x1b, W["mlp_w_up0"], "mlp_up_fwd0")
    pre_mlp0, x2, x2b = mm_res_ln(r0, W["mlp_w_down0"], x1, row(small["mlp_ln_g"], 0), row(small["mlp_ln_b"], 0), alpha,
                                  None, "mlp_down_fwd0")
    x3, x3b, pp0, gl0 = ple_fwd(x2, x2b, p, 0, W["ple_w_proj0"], W["ple_w_gate0"], None, "ple_fwd0")

    q, k, v = qkv_fwd(x3b, W["attn_w_q"], W["kv_w_k"], W["kv_w_v"], cs)
    qh, kh, vh = _to_heads(q), _to_heads(k), _to_heads(v)
    o = _from_heads(attn_fwd(qh, kh, vh, small["attn_sinks"]))
    pre_mix1, x4, x4b = mm_res_ln(o, W["attn_w_o"], x3, row(small["mix_ln_g"], 1), row(small["mix_ln_b"], 1), alpha,
                                  None, "attn_out_fwd")
    r1 = mlp_up_fwd(x4b, W["mlp_w_up1"], "mlp_up_fwd1")
    pre_mlp1, x5, x5b = mm_res_ln(r1, W["mlp_w_down1"], x4, row(small["mlp_ln_g"], 1), row(small["mlp_ln_b"], 1), alpha,
                                  None, "mlp_down_fwd1")
    dx6, loss, pp1, gl1 = ple_fwd(x5, x5b, p, 1, W["ple_w_proj1"], W["ple_w_gate1"], target, "ple_fwd1")

    G, sg = {}, {}
    dpp1, dgl1, dx5 = ple_bwd(dx6, pp1, gl1, W["ple_w_gate1"], "ple_bwd1")
    G["ple_w_proj1"] = wgrad(p[1], dpp1, False, "wg_ple_proj1")
    G["ple_w_gate1"] = wgrad(x5b, dgl1, True, "wg_ple_gate1")
    dpre_mlp1, dpre_mlp1b, dm1, g_mlp_g1, g_mlp_b1 = mlp_bwd1(dx5, pre_mlp1, row(small["mlp_ln_g"], 1), r1,
                                                              W["mlp_w_down1"], "mlp_bwd1_1")
    G["mlp_w_down1"] = wgrad(r1, dpre_mlp1b, True, "wg_mlp_down1")
    G["mlp_w_up1"] = wgrad(x4b, dm1, False, "wg_mlp_up1")
    dpre_mix1, dpre_mix1b, do, g_mix_g1, g_mix_b1, _ = mlp_bwd2(dpre_mlp1, dm1, W["mlp_w_up1"], alpha, pre_mix1,
                                                                row(small["mix_ln_g"], 1), W["attn_w_o"], "mlp_bwd2_1")
    G["attn_w_o"] = wgrad(o, dpre_mix1b, True, "wg_attn_o")
    dqh, dkh, dvh, dsinks = attn_bwd(qh, kh, vh, _to_heads(do), small["attn_sinks"])
    dqb, dkb, dvb, dx3 = qkv_bwd(_from_heads(dqh), _from_heads(dkh), _from_heads(dvh), dpre_mix1,
                                 W["attn_w_q"], W["kv_w_k"], W["kv_w_v"], cs, alpha)
    G["attn_w_q"] = wgrad(x3b, dqb, True, "wg_attn_q")
    G["kv_w_k"] = wgrad(x3b, dkb, True, "wg_kv_k")
    G["kv_w_v"] = wgrad(x3b, dvb, True, "wg_kv_v")

    dpp0, dgl0, dx2 = ple_bwd(dx3, pp0, gl0, W["ple_w_gate0"], "ple_bwd0")
    G["ple_w_proj0"] = wgrad(p[0], dpp0, False, "wg_ple_proj0")
    G["ple_w_gate0"] = wgrad(x2b, dgl0, True, "wg_ple_gate0")
    dpre_mlp0, dpre_mlp0b, dm0, g_mlp_g0, g_mlp_b0 = mlp_bwd1(dx2, pre_mlp0, row(small["mlp_ln_g"], 0), r0,
                                                              W["mlp_w_down0"], "mlp_bwd1_0")
    G["mlp_w_down0"] = wgrad(r0, dpre_mlp0b, True, "wg_mlp_down0")
    G["mlp_w_up0"] = wgrad(x1b, dm0, False, "wg_mlp_up0")
    dpre_mix0, dpre_mix0b, dsw, g_mix_g0, g_mix_b0, g_b_out = mlp_bwd2(dpre_mlp0, dm0, W["mlp_w_up0"], alpha, pre_mix0,
                                                                      row(small["mix_ln_g"], 0), W["conv_w_out"],
                                                                      "mlp_bwd2_0")
    G["conv_w_out"] = wgrad(s, dpre_mix0b, True, "wg_conv_out")
    dcv, g_cln_g, g_cln_b, g_b_dw = conv_mid_bwd(dsw, cv, small["conv_ln_g"], small["conv_ln_b"])
    dh, g_w_dw, g_b_in = dwconv_bwd(dcv, h, small["conv_w_dw"], taps)
    G["conv_w_in"] = wgrad(x0b, dh, False, "wg_conv_in")
    grad_x = conv_in_bwd(dh, dpre_mix0, W["conv_w_in"], alpha)

    sg["conv_b_in"] = g_b_in
    sg["conv_w_dw"] = g_w_dw
    sg["conv_b_dw"], sg["conv_ln_g"], sg["conv_ln_b"], sg["conv_b_out"] = g_b_dw, g_cln_g, g_cln_b, g_b_out
    sg["mix_ln_g"] = jnp.concatenate([g_mix_g0, g_mix_g1], axis=0)
    sg["mix_ln_b"] = jnp.concatenate([g_mix_b0, g_mix_b1], axis=0)
    sg["mlp_ln_g"] = jnp.concatenate([g_mlp_g0, g_mlp_g1], axis=0)
    sg["mlp_ln_b"] = jnp.concatenate([g_mlp_b0, g_mlp_b1], axis=0)
    sg["attn_sinks"] = dsinks[:, 0][None, :]
    return loss, grad_x, G, sg


BUF_A = ("mlp_w_up0", "mlp_w_up1", "mlp_w_down0", "mlp_w_down1", "ple_w_gate0", "ple_w_gate1", "conv_w_out", "attn_w_q",
         "attn_w_o")
BUF_B = ("conv_w_in",)
BUF_C = ("kv_w_k", "kv_w_v", "ple_w_proj0", "ple_w_proj1")
ROW_SHARDED = {"mlp_w_down0", "mlp_w_down1", "ple_w_gate0", "ple_w_gate1", "conv_w_out", "attn_w_q", "attn_w_o", "kv_w_k",
               "kv_w_v"}


def _split_layers(weights):
    out = {"conv_w_in": weights["conv_w_in"][0], "conv_w_out": weights["conv_w_out"][0],
           "attn_w_q": weights["attn_w_q"][0], "attn_w_o": weights["attn_w_o"][0],
           "kv_w_k": weights["kv_w_k"], "kv_w_v": weights["kv_w_v"]}
    for n in ("mlp_w_up", "mlp_w_down", "ple_w_proj", "ple_w_gate"):
        for i in range(weights[n].shape[0]):
            out[n + str(i)] = weights[n][i]
    return out


def _layout(shards):
    lay = {}
    for key, names in (("a", BUF_A), ("b", BUF_B), ("c", BUF_C)):
        off, rows = 0, []
        for n in names:
            rows.append((n, off, shards[n].shape[0]))
            off += shards[n].shape[0]
        lay[key] = rows
    return lay


def _place():
    return lax.axis_index("x"), lax.axis_index("y"), lax.axis_index("c")


def _flip(v, f):
    return (v + f) % 2 if f else v


CHIP_FLIPS = ((1, 0), (0, 1), (1, 1))


def gather_weights(bufs, small):
    nb = len(bufs)

    def body(*refs):
        ins, outs = refs[:nb + 1], refs[nb + 1:2 * nb + 2]
        send, recv, fsend, frecv, lsem = refs[2 * nb + 2:]
        x, y, c = _place()
        me = 2 * x + y
        local = [pltpu.make_async_copy(ins[k], outs[k].at[me], lsem.at[k]) for k in range(nb + 1)]
        for cp in local:
            cp.start()

        def half(ref, k):
            hrows = bufs[k].shape[0] // 2
            return ref.at[pl.ds(pl.multiple_of(c * hrows, 16), hrows), :]

        sends, fwd = [], []
        for d, (fx, fy) in enumerate(CHIP_FLIPS):
            to = (_flip(x, fx), _flip(y, fy), c)
            frm = 2 * _flip(x, fx) + _flip(y, fy)
            for k in range(nb):
                sends.append(pltpu.make_async_remote_copy(half(ins[k], k), half(outs[k].at[me], k), send.at[d * (nb + 1) + k],
                                                          recv.at[d * (nb + 1) + k], device_id=to, device_id_type=MESH))
                fwd.append(pltpu.make_async_remote_copy(half(outs[k].at[frm], k), half(outs[k].at[frm], k),
                                                        fsend.at[d * nb + k], frecv.at[d * nb + k],
                                                        device_id=(x, y, 1 - c), device_id_type=MESH))
            sends.append(pltpu.make_async_remote_copy(ins[nb], outs[nb].at[me], send.at[d * (nb + 1) + nb],
                                                      recv.at[d * (nb + 1) + nb], device_id=to, device_id_type=MESH))
        for cp in sends:
            cp.start()
        for d in range(len(CHIP_FLIPS)):
            for k in range(nb):
                sends[d * (nb + 1) + k].wait_recv()
                fwd[d * nb + k].start()
            sends[d * (nb + 1) + nb].wait_recv()
        for cp in fwd:
            cp.wait_recv()
        for cp in sends + fwd:
            cp.wait_send()
        for cp in local:
            cp.wait()

    arrs = list(bufs) + [small]
    nd = len(CHIP_FLIPS)
    return pl.pallas_call(
        body, name="gather_weights", in_specs=[ANY] * (nb + 1), out_specs=[ANY] * (nb + 1),
        out_shape=[_sds((NS,) + a.shape, a.dtype) for a in arrs],
        scratch_shapes=[pltpu.SemaphoreType.DMA((nd * (nb + 1),)), pltpu.SemaphoreType.DMA((nd * (nb + 1),)),
                        pltpu.SemaphoreType.DMA((nd * nb,)), pltpu.SemaphoreType.DMA((nd * nb,)),
                        pltpu.SemaphoreType.DMA((nb + 1,))])(*arrs)


def sibling_exchange(grads, small):
    nb = len(grads)

    def body(*refs):
        ins, outs = refs[:nb + 1], refs[nb + 1:2 * nb + 2]
        send, recv, lsem = refs[2 * nb + 2:]
        x, y, c = _place()
        me = 4 * x + 2 * y + c
        cps = []
        for k in range(nb):
            hrows = grads[k].shape[1] // 2
            src = ins[k].at[:, pl.ds(pl.multiple_of((1 - c) * hrows, 16), hrows), :]
            cps.append(pltpu.make_async_remote_copy(src, outs[k], send.at[k], recv.at[k], device_id=(x, y, 1 - c),
                                                    device_id_type=MESH))
        n = nb
        for fx in (0, 1):
            for fy in (0, 1):
                for fc in (0, 1):
                    if fx or fy or fc:
                        cps.append(pltpu.make_async_remote_copy(
                            ins[nb], outs[nb].at[me], send.at[n], recv.at[n],
                            device_id=(_flip(x, fx), _flip(y, fy), _flip(c, fc)), device_id_type=MESH))
                        n += 1
        own = pltpu.make_async_copy(ins[nb], outs[nb].at[me], lsem)
        own.start()
        for cp in cps:
            cp.start()
        for cp in cps:
            cp.wait()
        own.wait()

    shapes = [_sds((NS, g.shape[1] // 2, g.shape[2]), g.dtype) for g in grads] + [_sds((8,) + small.shape, small.dtype)]
    return pl.pallas_call(
        body, name="sibling_exchange", in_specs=[ANY] * (nb + 1), out_specs=[ANY] * (nb + 1), out_shape=shapes,
        scratch_shapes=[pltpu.SemaphoreType.DMA((nb + 7,)), pltpu.SemaphoreType.DMA((nb + 7,)), pltpu.SemaphoreType.DMA(())])(
            *grads, small)


def chip_exchange(sums):
    nb = len(sums)

    def body(*refs):
        ins, outs = refs[:nb], refs[nb:2 * nb]
        send, recv = refs[2 * nb:]
        x, y, c = _place()
        cps = []
        for d, (fx, fy) in enumerate(CHIP_FLIPS):
            tx, ty = _flip(x, fx), _flip(y, fy)
            for k in range(nb):
                cps.append(pltpu.make_async_remote_copy(ins[k].at[2 * tx + ty], outs[k].at[d], send.at[d * nb + k],
                                                        recv.at[d * nb + k], device_id=(tx, ty, c), device_id_type=MESH))
        for cp in cps:
            cp.start()
        for cp in cps:
            cp.wait()

    nd = len(CHIP_FLIPS)
    return pl.pallas_call(
        body, name="chip_exchange", in_specs=[ANY] * nb, out_specs=[ANY] * nb,
        out_shape=[_sds((nd,) + s.shape[1:], s.dtype) for s in sums],
        scratch_shapes=[pltpu.SemaphoreType.DMA((nd * nb,)), pltpu.SemaphoreType.DMA((nd * nb,))])(*sums)


def sibling_share(halves):
    nb = len(halves)

    def body(*refs):
        ins, outs = refs[:nb], refs[nb:2 * nb]
        send, recv, lsem = refs[2 * nb:]
        x, y, c = _place()
        cps, loc = [], []
        for k in range(nb):
            hrows = halves[k].shape[0]
            dst = outs[k].at[pl.ds(pl.multiple_of(c * hrows, 8), hrows), :]
            loc.append(pltpu.make_async_copy(ins[k], dst, lsem.at[k]))
            cps.append(pltpu.make_async_remote_copy(ins[k], dst, send.at[k], recv.at[k], device_id=(x, y, 1 - c),
                                                    device_id_type=MESH))
        for cp in loc + cps:
            cp.start()
        for k in range(nb):
            cps[k].wait()
            loc[k].wait()

    return pl.pallas_call(
        body, name="sibling_share", in_specs=[ANY] * nb, out_specs=[ANY] * nb,
        out_shape=[_sds((2 * h.shape[0], h.shape[1]), h.dtype) for h in halves],
        scratch_shapes=[pltpu.SemaphoreType.DMA((nb,)), pltpu.SemaphoreType.DMA((nb,)), pltpu.SemaphoreType.DMA((nb,))])(*halves)


def _row_tile(rows):
    for cand in (512, 384, 256, 128, 64, 32, 16):
        if rows % cand == 0:
            return cand
    return rows


def pair_sum(g, r, idx, name):
    _, hrows, W = r.shape
    tr = _row_tile(hrows)
    nrb = hrows // tr

    def body(idx_ref, g_ref, r_ref, o_ref):
        o_ref[...] = (g_ref[...].astype(F32) + r_ref[...].astype(F32)).astype(BF16)

    gs = pltpu.PrefetchScalarGridSpec(
        num_scalar_prefetch=1, grid=(NS, nrb),
        in_specs=[pl.BlockSpec((None, tr, W), lambda j, i, s: (j, s[1] * nrb + i, 0)),
                  pl.BlockSpec((None, tr, W), lambda j, i, s: (j, i, 0))],
        out_specs=pl.BlockSpec((None, tr, W), lambda j, i, s: (j, i, 0)))
    return pl.pallas_call(body, name=name, grid_spec=gs, out_shape=_sds(r.shape, BF16),
                          compiler_params=pltpu.CompilerParams(dimension_semantics=("parallel", "parallel")))(idx, g, r)


def chip_sum(s, t, idx, name):
    _, hrows, W = s.shape
    tr = _row_tile(hrows)

    def body(idx_ref, s_ref, t_ref, o_ref):
        acc = s_ref[...].astype(F32)
        for d in range(t.shape[0]):
            acc = acc + t_ref[d].astype(F32)
        o_ref[...] = acc

    gs = pltpu.PrefetchScalarGridSpec(
        num_scalar_prefetch=1, grid=(hrows // tr,),
        in_specs=[pl.BlockSpec((None, tr, W), lambda i, sc: (sc[0], i, 0)),
                  pl.BlockSpec((t.shape[0], tr, W), lambda i, sc: (0, i, 0))],
        out_specs=pl.BlockSpec((tr, W), lambda i, sc: (i, 0)))
    return pl.pallas_call(body, name=name, grid_spec=gs, out_shape=_sds((hrows, W), F32),
                          compiler_params=pltpu.CompilerParams(dimension_semantics=("parallel",)))(idx, s, t)


def small_sum(packs):
    n, R, W = packs.shape

    def body(p_ref, o_ref):
        acc = p_ref[0]
        for d in range(1, n):
            acc = acc + p_ref[d]
        o_ref[...] = acc

    return pl.pallas_call(body, name="small_sum", out_shape=_sds((R, W), F32))(packs)


WEIGHTS = ["conv_w_in", "conv_b_in", "conv_w_dw", "conv_b_dw", "conv_ln_g", "conv_ln_b", "conv_w_out", "conv_b_out", "kv_w_k",
           "kv_w_v", "attn_w_q", "attn_sinks", "attn_w_o", "mix_ln_g", "mix_ln_b", "mlp_w_up", "mlp_w_down", "mlp_ln_g",
           "mlp_ln_b", "ple_w_proj", "ple_w_gate"]
BIG = ["conv_w_in", "conv_w_out", "kv_w_k", "kv_w_v", "attn_w_q", "attn_w_o", "mlp_w_up", "mlp_w_down", "ple_w_proj",
       "ple_w_gate"]
SMALL = [n for n in WEIGHTS if n not in BIG]
SHARDED_SMALL = ["conv_b_in", "conv_w_dw", "conv_b_dw", "conv_ln_g", "conv_ln_b", "conv_b_out"]


def _flat128(a):
    f = a.reshape(-1)
    pad = (-f.shape[0]) % 128
    if pad:
        f = jnp.concatenate([f, jnp.zeros((pad,), f.dtype)])
    return f


def _step(x, p, target, w, m, v):
    D = x.shape[-1]
    ds = D // NS
    xq, yq, cq = _place()
    chip = 2 * xq + yq
    idx = jnp.stack([chip, cq]).astype(jnp.int32)

    shards = _split_layers(w)
    lay = _layout(shards)
    packed = [jnp.concatenate([shards[n].astype(BF16) for n, _, _ in lay[key]], axis=0) for key in ("a", "b", "c")]
    taps = w["conv_w_dw"].shape[1]
    small_loc = jnp.concatenate(
        [w["conv_w_dw"][0], jnp.zeros((HALO - taps, ds), F32), w["conv_b_dw"], w["conv_ln_g"], w["conv_ln_b"], w["conv_b_out"],
         w["conv_b_in"].reshape(2, ds), jnp.zeros((2, ds), F32)], axis=0)
    ga, gb, gc, gs = gather_weights(packed, small_loc)
    W = {}
    for key, buf in (("a", ga), ("b", gb), ("c", gc)):
        for n, off, rows in lay[key]:
            W[n] = (buf, off, rows)
    across = lambda rows: gs[:, rows, :].transpose(1, 0, 2).reshape(rows.stop - rows.start, D)
    small = {"taps": taps, "conv_w_dw": across(slice(0, HALO)), "conv_b_dw": across(slice(HALO, HALO + 1)),
             "conv_ln_g": across(slice(HALO + 1, HALO + 2)), "conv_ln_b": across(slice(HALO + 2, HALO + 3)),
             "conv_b_out": across(slice(HALO + 3, HALO + 4)), "conv_b_in": gs[:, HALO + 4:HALO + 6, :].reshape(1, 2 * D),
             "attn_sinks": w["attn_sinks"], "mix_ln_g": w["mix_ln_g"], "mix_ln_b": w["mix_ln_b"],
             "mlp_ln_g": w["mlp_ln_g"], "mlp_ln_b": w["mlp_ln_b"]}

    loss, grad_x, G, sg = _local_step(x[0], p[:, 0], target[0], W, small)

    parts = [jnp.concatenate([G[n] for n, _, _ in lay[key]], axis=1) for key in ("a", "b", "c")]
    nsink = sg["attn_sinks"].shape[1]
    pack = jnp.concatenate(
        [sg["conv_b_in"].reshape(2, D), sg["conv_w_dw"], sg["conv_b_dw"], sg["conv_ln_g"], sg["conv_ln_b"], sg["conv_b_out"],
         sg["mix_ln_g"], sg["mix_ln_b"], sg["mlp_ln_g"], sg["mlp_ln_b"],
         jnp.concatenate([sg["attn_sinks"], jnp.zeros((1, D - nsink), F32)], axis=1), jnp.zeros((1, D), F32)], axis=0)
    *from_sibling, packs = sibling_exchange(parts, pack)
    sums = [pair_sum(g, r, idx, "pair_sum_" + key) for g, r, key in zip(parts, from_sibling, "abc")]
    from_chips = chip_exchange(sums)
    halves = [chip_sum(s, t, idx, "chip_sum_" + key) for s, t, key in zip(sums, from_chips, "abc")]
    full = sibling_share(halves)
    tot = small_sum(packs)

    grads = {}
    for key, buf in zip(("a", "b", "c"), full):
        for n, off, rows in lay[key]:
            grads[n] = buf[off:off + rows]
    for n in ("mlp_w_up", "mlp_w_down", "ple_w_proj", "ple_w_gate"):
        grads[n] = jnp.stack([grads.pop(n + str(i)) for i in range(w[n].shape[0])])
    for n in ("conv_w_in", "conv_w_out", "attn_w_q", "attn_w_o"):
        grads[n] = grads[n][None]
    cols = lambda rows: lax.dynamic_slice(rows, (0, chip * ds), (rows.shape[0], ds))
    grads["conv_b_in"] = lax.dynamic_slice(tot[0:2].reshape(1, 2 * D), (0, chip * 2 * ds), (1, 2 * ds))
    grads["conv_w_dw"] = cols(tot[2:2 + taps])[None]
    r0 = 2 + HALO
    for i, n in enumerate(("conv_b_dw", "conv_ln_g", "conv_ln_b", "conv_b_out")):
        grads[n] = cols(tot[r0 + i:r0 + i + 1])
    r0 += 4
    for i, n in enumerate(("mix_ln_g", "mix_ln_b", "mlp_ln_g", "mlp_ln_b")):
        grads[n] = tot[r0 + 2 * i:r0 + 2 * i + 2]
    grads["attn_sinks"] = tot[r0 + 8:r0 + 9, 0:nsink]

    delta, new_m, new_v = {}, {}, {}
    for n in BIG:
        shp = w[n].shape
        two = lambda a: a.reshape(-1, shp[-1])
        d_, m_, v_ = adamw(two(w[n]), two(grads[n]), two(m[n]), two(v[n]), "adamw_" + n)
        delta[n], new_m[n], new_v[n] = d_.reshape(shp), m_.reshape(shp), v_.reshape(shp)
    flat = lambda t: jnp.concatenate([_flat128(t[n]) for n in SMALL])
    fw, fg, fm, fv = flat(w), flat(grads), flat(m), flat(v)
    pad = (-fw.shape[0]) % 1024
    rs = lambda f: jnp.concatenate([f, jnp.ones((pad,), F32)]).reshape(-1, 128)
    d_, m_, v_ = adamw(rs(fw), rs(fg), rs(fm), rs(fv), "adamw_small")
    pos = 0
    for n in SMALL:
        size = w[n].size
        take = lambda a: a.reshape(-1)[pos:pos + size].reshape(w[n].shape)
        delta[n], new_m[n], new_v[n] = take(d_), take(m_), take(v_)
        pos += size + (-size) % 128

    total = lax.psum(loss[0, 0], ("x", "y", "c"))
    return (total, grad_x[None], *[grads[n] for n in WEIGHTS], *[delta[n] for n in WEIGHTS], *[new_m[n] for n in WEIGHTS],
            *[new_v[n] for n in WEIGHTS])


def kernel(x, p, conv_w_in, conv_b_in, conv_w_dw, conv_b_dw, conv_ln_g, conv_ln_b, conv_w_out, conv_b_out, kv_w_k, kv_w_v, attn_w_q, attn_sinks, attn_w_o, mix_ln_g, mix_ln_b, mlp_w_up, mlp_w_down, mlp_ln_g, mlp_ln_b, ple_w_proj, ple_w_gate, loss_target, m_conv_w_in, m_conv_b_in, m_conv_w_dw, m_conv_b_dw, m_conv_ln_g, m_conv_ln_b, m_conv_w_out, m_conv_b_out, m_kv_w_k, m_kv_w_v, m_attn_w_q, m_attn_sinks, m_attn_w_o, m_mix_ln_g, m_mix_ln_b, m_mlp_w_up, m_mlp_w_down, m_mlp_ln_g, m_mlp_ln_b, m_ple_w_proj, m_ple_w_gate, v_conv_w_in, v_conv_b_in, v_conv_w_dw, v_conv_b_dw, v_conv_ln_g, v_conv_ln_b, v_conv_w_out, v_conv_b_out, v_kv_w_k, v_kv_w_v, v_attn_w_q, v_attn_sinks, v_attn_w_o, v_mix_ln_g, v_mix_ln_b, v_mlp_w_up, v_mlp_w_down, v_mlp_ln_g, v_mlp_ln_b, v_ple_w_proj, v_ple_w_gate):
    w = dict(zip(WEIGHTS, (conv_w_in, conv_b_in, conv_w_dw, conv_b_dw, conv_ln_g, conv_ln_b, conv_w_out, conv_b_out, kv_w_k,
                           kv_w_v, attn_w_q, attn_sinks, attn_w_o, mix_ln_g, mix_ln_b, mlp_w_up, mlp_w_down, mlp_ln_g, mlp_ln_b,
                           ple_w_proj, ple_w_gate)))
    m = dict(zip(WEIGHTS, (m_conv_w_in, m_conv_b_in, m_conv_w_dw, m_conv_b_dw, m_conv_ln_g, m_conv_ln_b, m_conv_w_out,
                           m_conv_b_out, m_kv_w_k, m_kv_w_v, m_attn_w_q, m_attn_sinks, m_attn_w_o, m_mix_ln_g, m_mix_ln_b,
                           m_mlp_w_up, m_mlp_w_down, m_mlp_ln_g, m_mlp_ln_b, m_ple_w_proj, m_ple_w_gate)))
    v = dict(zip(WEIGHTS, (v_conv_w_in, v_conv_b_in, v_conv_w_dw, v_conv_b_dw, v_conv_ln_g, v_conv_ln_b, v_conv_w_out,
                           v_conv_b_out, v_kv_w_k, v_kv_w_v, v_attn_w_q, v_attn_sinks, v_attn_w_o, v_mix_ln_g, v_mix_ln_b,
                           v_mlp_w_up, v_mlp_w_down, v_mlp_ln_g, v_mlp_ln_b, v_ple_w_proj, v_ple_w_gate)))
    return _step(x, p, loss_target, w, m, v)
```

```python
import functools

import jax
import jax.numpy as jnp
from jax import lax
from jax.experimental import pallas as pl
from jax.experimental.pallas import tpu as pltpu

F32 = jnp.float32
BF16 = jnp.bfloat16
NS = 4
HEAD = 64
BLK = 128
ROPE = 16
ROPE_THETA = 500000.0
LN_EPS = 1e-5
NEG = -1e30
HALO = 32
ADAM_LR, ADAM_B1, ADAM_B2, ADAM_EPS, ADAM_WD, ADAM_STEP = 0.001, 0.9, 0.999, 1e-08, 0.01, 10
MESH = pl.DeviceIdType.MESH
ANY = pl.BlockSpec(memory_space=pl.ANY)
NT = (((1,), (1,)), ((), ()))
TN = (((0,), (0,)), ((), ()))


def _pc(body, name, grid, in_specs, out_specs, out_shape, scratch=(), sem=None, vmem=56, **kw):
    return pl.pallas_call(
        body, name=name, grid=grid, in_specs=in_specs, out_specs=out_specs, out_shape=out_shape,
        scratch_shapes=list(scratch),
        compiler_params=pltpu.CompilerParams(dimension_semantics=sem, vmem_limit_bytes=vmem * 2 ** 20), **kw)


def _rows(tm, n):
    return pl.BlockSpec((tm, n), lambda i: (i, 0))


def _const(shape):
    return pl.BlockSpec(shape, lambda *_: (0,) * len(shape))


def _wspec(w):
    buf, off, rows = w
    assert off % rows == 0
    return pl.BlockSpec((NS, rows, buf.shape[2]), lambda *_: (0, off // rows, 0))


def _sds(shape, dtype):
    return jax.ShapeDtypeStruct(shape, dtype)


def _tile(t):
    return min(256, t)


def _sigmoid(x):
    return 1.0 / (1.0 + jnp.exp(-x))


def _ln_stats(w):
    mu = jnp.mean(w, axis=-1, keepdims=True)
    xc = w - mu
    var = jnp.mean(xc * xc, axis=-1, keepdims=True)
    rstd = lax.rsqrt(var + LN_EPS)
    return xc * rstd, rstd


def _ln_bwd(dy, w, g):
    xhat, rstd = _ln_stats(w)
    dxhat = dy * g
    m1 = jnp.mean(dxhat, axis=-1, keepdims=True)
    m2 = jnp.mean(dxhat * xhat, axis=-1, keepdims=True)
    dw = rstd * (dxhat - m1 - xhat * m2)
    return dw, jnp.sum(dy * xhat, axis=0, keepdims=True), jnp.sum(dy, axis=0, keepdims=True)


def _acc_rows(ref, val, first):
    @pl.when(first)
    def _():
        ref[...] = val

    @pl.when(jnp.logical_not(first))
    def _():
        ref[...] += val


def conv_in_fwd(xb, w_in, b_in):
    T, D = xb.shape
    nw = w_in[0].shape[2]
    tm = _tile(T)

    def body(x_ref, w_ref, b_ref, h_ref):
        x = x_ref[...]
        for j in range(NS):
            sl = slice(j * nw, (j + 1) * nw)
            h_ref[:, sl] = (jnp.dot(x, w_ref[j], preferred_element_type=F32) + b_ref[:, sl]).astype(BF16)

    return _pc(body, "conv_in_fwd", (T // tm,), [_rows(tm, D), _wspec(w_in), _const((1, NS * nw))],
               _rows(tm, NS * nw), _sds((T, NS * nw), BF16), sem=("parallel",))(xb, w_in[0], b_in)


def dwconv_fwd(h, w_dw, b_dw, ln_g, ln_b, taps):
    T = h.shape[0]
    C = h.shape[1] // 2
    tq = _tile(T)
    nh = tq // HALO
    off = HALO - (taps - 1)

    def body(a_ref, g_ref, ap_ref, gp_ref, w_ref, bdw_ref, lg_ref, lb_ref, cv_ref, s_ref, scr):
        i = pl.program_id(0)
        scr[HALO:HALO + tq, :] = a_ref[...].astype(F32) * _sigmoid(g_ref[...].astype(F32))
        up = ap_ref[...].astype(F32) * _sigmoid(gp_ref[...].astype(F32))
        scr[0:HALO, :] = jnp.where(i > 0, up, 0.0)
        cv_ref[...] = jnp.broadcast_to(bdw_ref[...], (tq, C))
        for j in range(taps):
            cv_ref[...] += w_ref[j:j + 1, :] * scr[off + j:off + j + tq, :]
        xhat, _ = _ln_stats(cv_ref[...])
        ln = xhat * lg_ref[...] + lb_ref[...]
        s_ref[...] = (ln * _sigmoid(ln)).astype(BF16)

    prev = lambda col: pl.BlockSpec((HALO, C), lambda i: (jnp.maximum(i * nh - 1, 0), col))
    cur = lambda col: pl.BlockSpec((tq, C), lambda i: (i, col))
    return _pc(body, "dwconv_fwd", (T // tq,),
               [cur(0), cur(1), prev(0), prev(1), _const((HALO, C)), _const((1, C)), _const((1, C)), _const((1, C))],
               [_rows(tq, C), _rows(tq, C)], [_sds((T, C), F32), _sds((T, C), BF16)],
               scratch=[pltpu.VMEM((HALO + tq, C), F32)], sem=("parallel",))(h, h, h, h, w_dw, b_dw, ln_g, ln_b)


def mm_res_ln(a, w, res, g, b, alpha, bias, name):
    T, K = a.shape
    ks = K // NS
    D = res.shape[1]
    tm = _tile(T)

    def body(*refs):
        a_ref, w_ref, res_ref, g_ref, b_ref = refs[:5]
        n = 5
        if bias is not None:
            bias_ref = refs[5]
            n = 6
        pre_ref, xo_ref, xb_ref = refs[n:n + 3]
        acc = jnp.dot(a_ref[:, 0:ks], w_ref[0], preferred_element_type=F32)
        for j in range(1, NS):
            acc = acc + jnp.dot(a_ref[:, j * ks:(j + 1) * ks], w_ref[j], preferred_element_type=F32)
        if bias is not None:
            acc = acc + bias_ref[...]
        pre = alpha * res_ref[...] + acc
        xhat, _ = _ln_stats(pre)
        xo = xhat * g_ref[...] + b_ref[...]
        pre_ref[...] = pre
        xo_ref[...] = xo
        xb_ref[...] = xo.astype(BF16)

    ins = [_rows(tm, K), _wspec(w), _rows(tm, D), _const((1, D)), _const((1, D))]
    args = [a, w[0], res, g, b]
    if bias is not None:
        ins.append(_const((1, D)))
        args.append(bias)
    return _pc(body, name, (T // tm,), ins, [_rows(tm, D)] * 3, [_sds((T, D), F32), _sds((T, D), F32), _sds((T, D), BF16)],
               sem=("parallel",))(*args)


def mlp_up_fwd(xb, w_up, name):
    T, D = xb.shape
    fs = w_up[0].shape[2]
    tm = _tile(T)

    def body(x_ref, w_ref, r_ref):
        x = x_ref[...]
        for j in range(NS):
            m = jnp.maximum(jnp.dot(x, w_ref[j], preferred_element_type=F32), 0.0)
            r_ref[:, j * fs:(j + 1) * fs] = (m * m).astype(BF16)

    return _pc(body, name, (T // tm,), [_rows(tm, D), _wspec(w_up)], _rows(tm, NS * fs), _sds((T, NS * fs), BF16),
               sem=("parallel",))(xb, w_up[0])


def ple_fwd(x, xb, p, layer, w_proj, w_gate, target, name):
    T, D = x.shape
    P = p.shape[2]
    ds = D // NS
    tm = _tile(T)
    last = target is not None

    def body(*refs):
        x_ref, xb_ref, p_ref, wp_ref, wg_ref = refs[:5]
        n = 5
        if last:
            t_ref = refs[5]
            n = 6
        o_ref, o2_ref, pp_ref, gl_ref = refs[n:n + 4]
        gl = jnp.dot(xb_ref[:, 0:ds], wg_ref[0], preferred_element_type=F32)
        for j in range(1, NS):
            gl = gl + jnp.dot(xb_ref[:, j * ds:(j + 1) * ds], wg_ref[j], preferred_element_type=F32)
        gl_ref[...] = gl.astype(BF16)
        sg = _sigmoid(gl)
        pb = p_ref[...].astype(BF16)
        sq = jnp.zeros((1, 1), F32)
        for j in range(NS):
            sl = slice(j * ds, (j + 1) * ds)
            pp = jnp.dot(pb, wp_ref[j], preferred_element_type=F32)
            pp_ref[:, sl] = pp.astype(BF16)
            out = x_ref[:, sl] + pp * sg[:, sl]
            if last:
                err = out - t_ref[:, sl]
                o_ref[:, sl] = err * (1.0 / D)
                e2 = jnp.sum(err * err, axis=0, keepdims=True)
                sq = sq + jnp.sum(e2, axis=1, keepdims=True)
            else:
                o_ref[:, sl] = out
                o2_ref[:, sl] = out.astype(BF16)
        if last:
            _acc_rows(o2_ref, jnp.broadcast_to(sq * (0.5 / D), (8, 128)), pl.program_id(0) == 0)

    ins = [_rows(tm, D), _rows(tm, D), pl.BlockSpec((None, tm, P), lambda i: (layer, i, 0)), _wspec(w_proj), _wspec(w_gate)]
    args = [x, xb, p, w_proj[0], w_gate[0]]
    if last:
        ins.append(_rows(tm, D))
        args.append(target)
        outs = [_rows(tm, D), _const((8, 128)), _rows(tm, D), _rows(tm, D)]
        shapes = [_sds((T, D), F32), _sds((8, 128), F32), _sds((T, D), BF16), _sds((T, D), BF16)]
    else:
        outs = [_rows(tm, D)] * 4
        shapes = [_sds((T, D), F32), _sds((T, D), BF16), _sds((T, D), BF16), _sds((T, D), BF16)]
    return _pc(body, name, (T // tm,), ins, outs, shapes, sem=("arbitrary",) if last else ("parallel",))(*args)


def _rope(x, cs_ref, sign):
    c = cs_ref[0]
    s = cs_ref[1] * sign
    lane = lax.broadcasted_iota(jnp.int32, c.shape, 1)
    first = (lane % HEAD) < (ROPE // 2)
    outs = []
    for gq in range(x.shape[1] // 128):
        xg = x[:, gq * 128:(gq + 1) * 128]
        sw = jnp.where(first, pltpu.roll(xg, 128 - ROPE // 2, 1), pltpu.roll(xg, ROPE // 2, 1))
        outs.append(xg * c + sw * s)
    return outs


def qkv_fwd(xb, w_q, w_k, w_v, cs):
    T, D = xb.shape
    ds = D // NS
    HD, KVD = w_q[0].shape[2], w_k[0].shape[2]
    tm = _tile(T)
    scale = 1.0 / (HEAD ** 0.5)

    def body(x_ref, wq_ref, wk_ref, wv_ref, cs_ref, q_ref, k_ref, v_ref):
        def proj(w_ref):
            acc = jnp.dot(x_ref[:, 0:ds], w_ref[0], preferred_element_type=F32)
            for j in range(1, NS):
                acc = acc + jnp.dot(x_ref[:, j * ds:(j + 1) * ds], w_ref[j], preferred_element_type=F32)
            return acc

        for gq, val in enumerate(_rope(proj(wq_ref), cs_ref, 1.0)):
            q_ref[:, gq * 128:(gq + 1) * 128] = (val * scale).astype(BF16)
        for gq, val in enumerate(_rope(proj(wk_ref), cs_ref, 1.0)):
            k_ref[:, gq * 128:(gq + 1) * 128] = val.astype(BF16)
        v_ref[...] = proj(wv_ref).astype(BF16)

    cs_spec = pl.BlockSpec((2, tm, 128), lambda i: (0, i, 0))
    return _pc(body, "qkv_fwd", (T // tm,), [_rows(tm, D), _wspec(w_q), _wspec(w_k), _wspec(w_v), cs_spec],
               [_rows(tm, HD), _rows(tm, KVD), _rows(tm, KVD)],
               [_sds((T, HD), BF16), _sds((T, KVD), BF16), _sds((T, KVD), BF16)], sem=("parallel",))(
                   xb, w_q[0], w_k[0], w_v[0], cs)


def _band_mask(n):
    row = lax.broadcasted_iota(jnp.int32, (BLK, 2 * BLK), 0)
    col = lax.broadcasted_iota(jnp.int32, (BLK, 2 * BLK), 1)
    return (col > row) & (col <= row + BLK) & ((col >= BLK) | (n > 0))


def _probs(q, k2, valid, sink):
    s = jnp.where(valid, lax.dot_general(q, k2, NT, preferred_element_type=F32), NEG)
    m = jnp.maximum(jnp.max(s, axis=-1, keepdims=True), sink)
    e = jnp.exp(s - m)
    es = jnp.exp(sink - m)
    den = jnp.sum(e, axis=-1, keepdims=True) + es
    return e / den, es / den


def attn_fwd(q, k, v, sinks):
    NH, T, _ = q.shape
    NKV = k.shape[0]
    G = NH // NKV

    def body(s_ref, q_ref, kc_ref, kp_ref, vc_ref, vp_ref, o_ref):
        valid = _band_mask(pl.program_id(0))
        for kh in range(NKV):
            k2 = jnp.concatenate([kp_ref[kh], kc_ref[kh]], axis=0)
            v2 = jnp.concatenate([vp_ref[kh], vc_ref[kh]], axis=0)
            for gq in range(G):
                hh = kh * G + gq
                p, _ = _probs(q_ref[hh], k2, valid, s_ref[0, hh])
                o_ref[hh] = jnp.dot(p.astype(BF16), v2, preferred_element_type=F32).astype(BF16)

    cur = lambda nh: pl.BlockSpec((nh, BLK, HEAD), lambda n: (0, n, 0))
    prev = lambda nh: pl.BlockSpec((nh, BLK, HEAD), lambda n: (0, jnp.maximum(n - 1, 0), 0))
    return _pc(body, "attn_fwd", (T // BLK,),
               [pl.BlockSpec(memory_space=pltpu.SMEM), cur(NH), cur(NKV), prev(NKV), cur(NKV), prev(NKV)],
               cur(NH), _sds((NH, T, HEAD), BF16), sem=("parallel",))(sinks, q, k, k, v, v)


def ple_bwd(dxo, pp, gl, w_gate, name):
    T, D = dxo.shape
    ds = D // NS
    tm = _tile(T)

    def body(d_ref, pp_ref, gl_ref, wg_ref, dpp_ref, dgl_ref, dx_ref):
        d = d_ref[...]
        sg = _sigmoid(gl_ref[...].astype(F32))
        dpp_ref[...] = (d * sg).astype(BF16)
        dgl = (d * pp_ref[...].astype(F32) * sg * (1.0 - sg)).astype(BF16)
        dgl_ref[...] = dgl
        for j in range(NS):
            sl = slice(j * ds, (j + 1) * ds)
            dx_ref[:, sl] = d_ref[:, sl] + lax.dot_general(dgl, wg_ref[j], NT, preferred_element_type=F32)

    return _pc(body, name, (T // tm,), [_rows(tm, D)] * 3 + [_wspec(w_gate)], [_rows(tm, D)] * 3,
               [_sds((T, D), BF16), _sds((T, D), BF16), _sds((T, D), F32)], sem=("parallel",))(dxo, pp, gl, w_gate[0])


def mlp_bwd1(dy, pre, g, r, w_down, name):
    T, D = dy.shape
    fs = w_down[2]
    tm = _tile(T)

    def body(dy_ref, pre_ref, g_ref, r_ref, w_ref, dw_ref, dwb_ref, dm_ref, dg_ref, db_ref):
        dw, dg, db = _ln_bwd(dy_ref[...], pre_ref[...], g_ref[...])
        first = pl.program_id(0) == 0
        _acc_rows(dg_ref, dg, first)
        _acc_rows(db_ref, db, first)
        dwb = dw.astype(BF16)
        dw_ref[...] = dw
        dwb_ref[...] = dwb
        for j in range(NS):
            sl = slice(j * fs, (j + 1) * fs)
            dr = lax.dot_general(dwb, w_ref[j], NT, preferred_element_type=F32)
            dm_ref[:, sl] = (dr * (2.0 * jnp.sqrt(r_ref[:, sl].astype(F32)))).astype(BF16)

    return _pc(body, name, (T // tm,), [_rows(tm, D), _rows(tm, D), _const((1, D)), _rows(tm, NS * fs), _wspec(w_down)],
               [_rows(tm, D), _rows(tm, D), _rows(tm, NS * fs), _const((1, D)), _const((1, D))],
               [_sds((T, D), F32), _sds((T, D), BF16), _sds((T, NS * fs), BF16), _sds((1, D), F32), _sds((1, D), F32)],
               sem=("arbitrary",))(dy, pre, g, r, w_down[0])


def mlp_bwd2(dpre, dm, w_up, alpha, pre_mix, g_mix, w_mix, name):
    T, D = dpre.shape
    fs = w_up[0].shape[2]
    ms = w_mix[2]
    tm = _tile(T)

    def body(dp_ref, dm_ref, wu_ref, pre_ref, g_ref, wm_ref, dw_ref, dwb_ref, do_ref, dg_ref, db_ref, dc_ref):
        dy = alpha * dp_ref[...]
        for j in range(NS):
            dy = dy + lax.dot_general(dm_ref[:, j * fs:(j + 1) * fs], wu_ref[j], NT, preferred_element_type=F32)
        dw, dg, db = _ln_bwd(dy, pre_ref[...], g_ref[...])
        first = pl.program_id(0) == 0
        _acc_rows(dg_ref, dg, first)
        _acc_rows(db_ref, db, first)
        _acc_rows(dc_ref, jnp.sum(dw, axis=0, keepdims=True), first)
        dwb = dw.astype(BF16)
        dw_ref[...] = dw
        dwb_ref[...] = dwb
        for j in range(NS):
            do_ref[:, j * ms:(j + 1) * ms] = lax.dot_general(dwb, wm_ref[j], NT, preferred_element_type=F32).astype(BF16)

    return _pc(body, name, (T // tm,),
               [_rows(tm, D), _rows(tm, NS * fs), _wspec(w_up), _rows(tm, D), _const((1, D)), _wspec(w_mix)],
               [_rows(tm, D), _rows(tm, D), _rows(tm, NS * ms), _const((1, D)), _const((1, D)), _const((1, D))],
               [_sds((T, D), F32), _sds((T, D), BF16), _sds((T, NS * ms), BF16)] + [_sds((1, D), F32)] * 3,
               sem=("arbitrary",))(dpre, dm, w_up[0], pre_mix, g_mix, w_mix[0])


def attn_bwd(q, k, v, do, sinks):
    NH, T, _ = q.shape
    NKV = k.shape[0]
    G = NH // NKV
    nb = T // BLK

    def body(s_ref, q_ref, do_ref, kc_ref, kp_ref, vc_ref, vp_ref, dq_ref, dk_ref, dv_ref, ds_ref, ck, cv):
        n = pl.program_id(0)

        @pl.when(n == 0)
        def _():
            ck[...] = jnp.zeros_like(ck)
            cv[...] = jnp.zeros_like(cv)
            ds_ref[...] = jnp.zeros_like(ds_ref)

        @pl.when(n < nb)
        def _():
            valid = _band_mask(n)
            for kh in range(NKV):
                k2 = jnp.concatenate([kp_ref[kh], kc_ref[kh]], axis=0)
                v2 = jnp.concatenate([vp_ref[kh], vc_ref[kh]], axis=0)
                dk2 = jnp.zeros((2 * BLK, HEAD), F32)
                dv2 = jnp.zeros((2 * BLK, HEAD), F32)
                for gq in range(G):
                    hh = kh * G + gq
                    qh = q_ref[hh]
                    doh = do_ref[hh]
                    p, ps = _probs(qh, k2, valid, s_ref[0, hh])
                    dp = lax.dot_general(doh, v2, NT, preferred_element_type=F32)
                    delta = jnp.sum(p * dp, axis=-1, keepdims=True)
                    dsb = (p * (dp - delta)).astype(BF16)
                    ds_ref[hh:hh + 1, :] += jnp.broadcast_to(-jnp.sum(ps * delta, axis=0, keepdims=True), (1, 128))
                    dq_ref[hh] = jnp.dot(dsb, k2, preferred_element_type=F32)
                    dk2 = dk2 + lax.dot_general(dsb, qh, TN, preferred_element_type=F32)
                    dv2 = dv2 + lax.dot_general(p.astype(BF16), doh, TN, preferred_element_type=F32)
                dk_ref[kh] = ck[kh] + dk2[0:BLK]
                dv_ref[kh] = cv[kh] + dv2[0:BLK]
                ck[kh] = dk2[BLK:2 * BLK]
                cv[kh] = dv2[BLK:2 * BLK]

        @pl.when(n == nb)
        def _():
            dk_ref[...] = ck[...]
            dv_ref[...] = cv[...]

    qcur = pl.BlockSpec((NH, BLK, HEAD), lambda n: (0, jnp.minimum(n, nb - 1), 0))
    kcur = pl.BlockSpec((NKV, BLK, HEAD), lambda n: (0, jnp.minimum(n, nb - 1), 0))
    kprev = pl.BlockSpec((NKV, BLK, HEAD), lambda n: (0, jnp.maximum(n - 1, 0), 0))
    return _pc(body, "attn_bwd", (nb + 1,),
               [pl.BlockSpec(memory_space=pltpu.SMEM), qcur, qcur, kcur, kprev, kcur, kprev],
               [qcur, kprev, kprev, _const((NH, 128))],
               [_sds((NH, T, HEAD), F32), _sds((NKV, T, HEAD), F32), _sds((NKV, T, HEAD), F32), _sds((NH, 128), F32)],
               scratch=[pltpu.VMEM((NKV, BLK, HEAD), F32), pltpu.VMEM((NKV, BLK, HEAD), F32)],
               sem=("arbitrary",))(sinks, q, do, k, k, v, v)


def qkv_bwd(dq, dk, dv, dpre_mix, w_q, w_k, w_v, cs, alpha):
    T, HD = dq.shape
    KVD = dk.shape[1]
    D = dpre_mix.shape[1]
    ds = D // NS
    tm = _tile(T)
    scale = 1.0 / (HEAD ** 0.5)

    def body(dq_ref, dk_ref, dv_ref, dp_ref, wq_ref, wk_ref, wv_ref, cs_ref, dqb_ref, dkb_ref, dvb_ref, dx_ref):
        for gq, val in enumerate(_rope(dq_ref[...], cs_ref, -1.0)):
            dqb_ref[:, gq * 128:(gq + 1) * 128] = (val * scale).astype(BF16)
        for gq, val in enumerate(_rope(dk_ref[...], cs_ref, -1.0)):
            dkb_ref[:, gq * 128:(gq + 1) * 128] = val.astype(BF16)
        dvb_ref[...] = dv_ref[...].astype(BF16)
        dqb, dkb, dvb = dqb_ref[...], dkb_ref[...], dvb_ref[...]
        for j in range(NS):
            sl = slice(j * ds, (j + 1) * ds)
            dx_ref[:, sl] = (alpha * dp_ref[:, sl]
                             + lax.dot_general(dqb, wq_ref[j], NT, preferred_element_type=F32)
                             + lax.dot_general(dkb, wk_ref[j], NT, preferred_element_type=F32)
                             + lax.dot_general(dvb, wv_ref[j], NT, preferred_element_type=F32))

    cs_spec = pl.BlockSpec((2, tm, 128), lambda i: (0, i, 0))
    return _pc(body, "qkv_bwd", (T // tm,),
               [_rows(tm, HD), _rows(tm, KVD), _rows(tm, KVD), _rows(tm, D), _wspec(w_q), _wspec(w_k), _wspec(w_v), cs_spec],
               [_rows(tm, HD), _rows(tm, KVD), _rows(tm, KVD), _rows(tm, D)],
               [_sds((T, HD), BF16), _sds((T, KVD), BF16), _sds((T, KVD), BF16), _sds((T, D), F32)],
               sem=("parallel",))(dq, dk, dv, dpre_mix, w_q[0], w_k[0], w_v[0], cs)


def conv_mid_bwd(ds, cv, ln_g, ln_b):
    T, C = cv.shape
    tm = _tile(T)

    def body(ds_ref, cv_ref, g_ref, b_ref, dcv_ref, dg_ref, db_ref, dc_ref):
        xhat, _ = _ln_stats(cv_ref[...])
        ln = xhat * g_ref[...] + b_ref[...]
        sg = _sigmoid(ln)
        dl = ds_ref[...].astype(F32) * (sg * (1.0 + ln * (1.0 - sg)))
        dcv, dg, db = _ln_bwd(dl, cv_ref[...], g_ref[...])
        first = pl.program_id(0) == 0
        _acc_rows(dg_ref, dg, first)
        _acc_rows(db_ref, db, first)
        _acc_rows(dc_ref, jnp.sum(dcv, axis=0, keepdims=True), first)
        dcv_ref[...] = dcv

    return _pc(body, "conv_mid_bwd", (T // tm,), [_rows(tm, C), _rows(tm, C), _const((1, C)), _const((1, C))],
               [_rows(tm, C), _const((1, C)), _const((1, C)), _const((1, C))],
               [_sds((T, C), F32)] + [_sds((1, C), F32)] * 3, sem=("arbitrary",))(ds, cv, ln_g, ln_b)


def dwconv_bwd(dcv, h, w_dw, taps):
    T, C = dcv.shape
    tq = _tile(T)
    nh = tq // HALO
    nblk = T // tq
    off = HALO - (taps - 1)

    def body(d_ref, dn_ref, a_ref, g_ref, ap_ref, gp_ref, w_ref, dh_ref, dw_ref, dbi_ref, su, sd, du):
        i = pl.program_id(0)
        a = a_ref[...].astype(F32)
        sg = _sigmoid(g_ref[...].astype(F32))
        su[HALO:HALO + tq, :] = a * sg
        up = ap_ref[...].astype(F32) * _sigmoid(gp_ref[...].astype(F32))
        su[0:HALO, :] = jnp.where(i > 0, up, 0.0)
        sd[0:tq, :] = d_ref[...]
        sd[tq:tq + HALO, :] = jnp.where(i < nblk - 1, dn_ref[...], 0.0)

        @pl.when(i == 0)
        def _():
            dw_ref[...] = jnp.zeros_like(dw_ref)

        du[...] = jnp.zeros_like(du)
        for j in range(taps):
            du[...] += w_ref[j:j + 1, :] * sd[taps - 1 - j:taps - 1 - j + tq, :]
            dw_ref[j:j + 1, :] += jnp.sum(d_ref[...] * su[off + j:off + j + tq, :], axis=0, keepdims=True)
        da = du[...] * sg
        dgt = du[...] * a * sg * (1.0 - sg)
        dh_ref[:, 0:C] = da.astype(BF16)
        dh_ref[:, C:2 * C] = dgt.astype(BF16)
        first = i == 0
        _acc_rows(dbi_ref.at[:, 0:C], jnp.sum(da, axis=0, keepdims=True), first)
        _acc_rows(dbi_ref.at[:, C:2 * C], jnp.sum(dgt, axis=0, keepdims=True), first)

    prev = lambda col: pl.BlockSpec((HALO, C), lambda i: (jnp.maximum(i * nh - 1, 0), col))
    nxt = pl.BlockSpec((HALO, C), lambda i: (jnp.minimum((i + 1) * nh, T // HALO - 1), 0))
    cur = lambda col: pl.BlockSpec((tq, C), lambda i: (i, col))
    return _pc(body, "dwconv_bwd", (nblk,),
               [cur(0), nxt, cur(0), cur(1), prev(0), prev(1), _const((HALO, C))],
               [_rows(tq, 2 * C), _const((HALO, C)), _const((1, 2 * C))],
               [_sds((T, 2 * C), BF16), _sds((HALO, C), F32), _sds((1, 2 * C), F32)],
               scratch=[pltpu.VMEM((HALO + tq, C), F32), pltpu.VMEM((HALO + tq, C), F32), pltpu.VMEM((tq, C), F32)],
               sem=("arbitrary",))(dcv, dcv, h, h, h, h, w_dw)


def conv_in_bwd(dh, dpre_mix, w_in, alpha):
    T, D = dpre_mix.shape
    nw = w_in[0].shape[2]
    tm = _tile(T)

    def body(dh_ref, dp_ref, w_ref, dx_ref):
        acc = alpha * dp_ref[...]
        for j in range(NS):
            acc = acc + lax.dot_general(dh_ref[:, j * nw:(j + 1) * nw], w_ref[j], NT, preferred_element_type=F32)
        dx_ref[...] = acc

    return _pc(body, "conv_in_bwd", (T // tm,), [_rows(tm, NS * nw), _rows(tm, D), _wspec(w_in)], _rows(tm, D),
               _sds((T, D), F32), sem=("parallel",))(dh, dpre_mix, w_in[0])


def wgrad(a, b, row_sharded, name):
    T, Ka = a.shape
    Nb = b.shape[1]
    tt = min(512, T)
    nt = T // tt
    ka, tn = min(Ka, 1024), min(Nb, 1024)
    if row_sharded:
        sr = Ka // NS
        spb = max(ka // sr, 1)
        out_shape = (NS, sr, Nb)
        out_spec = pl.BlockSpec((spb, ka // spb, tn), lambda i, j, t: (i, 0, j))
    else:
        sc = Nb // NS
        spb = max(tn // sc, 1)
        out_shape = (NS, Ka, sc)
        out_spec = pl.BlockSpec((spb, ka, tn // spb), lambda i, j, t: (j, i, 0))

    def body(a_ref, b_ref, o_ref, acc):
        t = pl.program_id(2)
        av = a_ref[...]
        if av.dtype != BF16:
            av = av.astype(BF16)
        d = lax.dot_general(av, b_ref[...], TN, preferred_element_type=F32)

        @pl.when(t == 0)
        def _():
            acc[...] = d

        @pl.when(t > 0)
        def _():
            acc[...] += d

        @pl.when(t == nt - 1)
        def _():
            for s in range(spb):
                if row_sharded:
                    o_ref[s] = acc[s * (ka // spb):(s + 1) * (ka // spb), :].astype(BF16)
                else:
                    o_ref[s] = acc[:, s * (tn // spb):(s + 1) * (tn // spb)].astype(BF16)

    return _pc(body, name, (Ka // ka, Nb // tn, nt),
               [pl.BlockSpec((tt, ka), lambda i, j, t: (t, i)), pl.BlockSpec((tt, tn), lambda i, j, t: (t, j))],
               out_spec, _sds(out_shape, BF16),
               scratch=[pltpu.VMEM((ka, tn), F32)], sem=("parallel", "parallel", "arbitrary"))(a, b)


def adamw(w, g, m, v, name):
    R, W = w.shape
    tr = R
    for cand in (512, 256, 128, 64, 32, 16, 8):
        if R % cand == 0:
            tr = cand
            break
    c1 = 1.0 - ADAM_B1 ** ADAM_STEP
    c2 = 1.0 - ADAM_B2 ** ADAM_STEP

    def body(w_ref, g_ref, m_ref, v_ref, d_ref, mo_ref, vo_ref):
        gv = g_ref[...]
        mn = ADAM_B1 * m_ref[...] + (1.0 - ADAM_B1) * gv
        vn = ADAM_B2 * v_ref[...] + (1.0 - ADAM_B2) * (gv * gv)
        mo_ref[...] = mn
        vo_ref[...] = vn
        d_ref[...] = -ADAM_LR * ((mn / c1) / (jnp.sqrt(vn / c2) + ADAM_EPS) + ADAM_WD * w_ref[...])

    return _pc(body, name, (R // tr,), [_rows(tr, W)] * 4, [_rows(tr, W)] * 3, [_sds((R, W), F32)] * 3,
               sem=("parallel",))(w, g, m, v)


def _to_heads(t):
    T, n = t.shape
    return t.reshape(T, n // HEAD, HEAD).transpose(1, 0, 2)


def _from_heads(t):
    nh, T, _ = t.shape
    return t.transpose(1, 0, 2).reshape(T, nh * HEAD)


def _rope_tables(T):
    pos = jnp.arange(T, dtype=F32)
    inv_freq = ROPE_THETA ** (-jnp.arange(0, ROPE, 2, dtype=F32) / ROPE)
    ang = pos[:, None] * inv_freq[None, :]
    cos, sin = jnp.cos(ang), jnp.sin(ang)
    pad = HEAD - ROPE
    c = jnp.concatenate([cos, cos, jnp.ones((T, pad), F32)], axis=1)
    s = jnp.concatenate([-sin, sin, jnp.zeros((T, pad), F32)], axis=1)
    return jnp.stack([jnp.tile(c, (1, 128 // HEAD)), jnp.tile(s, (1, 128 // HEAD))])


def _local_step(x, p, target, W, small):
    T, D = x.shape
    depth = small["mix_ln_g"].shape[0]
    alpha = float((2 * depth) ** 0.25)
    taps = small["taps"]
    row = lambda a, i: a[i:i + 1]
    cs = _rope_tables(T)

    x0b = x.astype(BF16)
    h = conv_in_fwd(x0b, W["conv_w_in"], small["conv_b_in"])
    cv, s = dwconv_fwd(h, small["conv_w_dw"], small["conv_b_dw"], small["conv_ln_g"], small["conv_ln_b"], taps)
    pre_mix0, x1, x1b = mm_res_ln(s, W["conv_w_out"], x, row(small["mix_ln_g"], 0), row(small["mix_ln_b"], 0), alpha,
                                  small["conv_b_out"], "conv_out_fwd")
    r0 = mlp_up_fwd(x1b, W["mlp_w_up0"], "mlp_up_fwd0")
    pre_mlp0, x2, x2b = mm_res_ln(r0, W["mlp_w_down0"], x1, row(small["mlp_ln_g"], 0), row(small["mlp_ln_b"], 0), alpha,
                                  None, "mlp_down_fwd0")
    x3, x3b, pp0, gl0 = ple_fwd(x2, x2b, p, 0, W["ple_w_proj0"], W["ple_w_gate0"], None, "ple_fwd0")

    q, k, v = qkv_fwd(x3b, W["attn_w_q"], W["kv_w_k"], W["kv_w_v"], cs)
    qh, kh, vh = _to_heads(q), _to_heads(k), _to_heads(v)
    o = _from_heads(attn_fwd(qh, kh, vh, small["attn_sinks"]))
    pre_mix1, x4, x4b = mm_res_ln(o, W["attn_w_o"], x3, row(small["mix_ln_g"], 1), row(small["mix_ln_b"], 1), alpha,
                                  None, "attn_out_fwd")
    r1 = mlp_up_fwd(x4b, W["mlp_w_up1"], "mlp_up_fwd1")
    pre_mlp1, x5, x5b = mm_res_ln(r1, W["mlp_w_down1"], x4, row(small["mlp_ln_g"], 1), row(small["mlp_ln_b"], 1), alpha,
                                  None, "mlp_down_fwd1")
    dx6, loss, pp1, gl1 = ple_fwd(x5, x5b, p, 1, W["ple_w_proj1"], W["ple_w_gate1"], target, "ple_fwd1")

    G, sg = {}, {}
    dpp1, dgl1, dx5 = ple_bwd(dx6, pp1, gl1, W["ple_w_gate1"], "ple_bwd1")
    G["ple_w_proj1"] = wgrad(p[1], dpp1, False, "wg_ple_proj1")
    G["ple_w_gate1"] = wgrad(x5b, dgl1, True, "wg_ple_gate1")
    dpre_mlp1, dpre_mlp1b, dm1, g_mlp_g1, g_mlp_b1 = mlp_bwd1(dx5, pre_mlp1, row(small["mlp_ln_g"], 1), r1,
                                                              W["mlp_w_down1"], "mlp_bwd1_1")
    G["mlp_w_down1"] = wgrad(r1, dpre_mlp1b, True, "wg_mlp_down1")
    G["mlp_w_up1"] = wgrad(x4b, dm1, False, "wg_mlp_up1")
    dpre_mix1, dpre_mix1b, do, g_mix_g1, g_mix_b1, _ = mlp_bwd2(dpre_mlp1, dm1, W["mlp_w_up1"], alpha, pre_mix1,
                                                                row(small["mix_ln_g"], 1), W["attn_w_o"], "mlp_bwd2_1")
    G["attn_w_o"] = wgrad(o, dpre_mix1b, True, "wg_attn_o")
    dqh, dkh, dvh, dsinks = attn_bwd(qh, kh, vh, _to_heads(do), small["attn_sinks"])
    dqb, dkb, dvb, dx3 = qkv_bwd(_from_heads(dqh), _from_heads(dkh), _from_heads(dvh), dpre_mix1,
                                 W["attn_w_q"], W["kv_w_k"], W["kv_w_v"], cs, alpha)
    G["attn_w_q"] = wgrad(x3b, dqb, True, "wg_attn_q")
    G["kv_w_k"] = wgrad(x3b, dkb, True, "wg_kv_k")
    G["kv_w_v"] = wgrad(x3b, dvb, True, "wg_kv_v")

    dpp0, dgl0, dx2 = ple_bwd(dx3, pp0, gl0, W["ple_w_gate0"], "ple_bwd0")
    G["ple_w_proj0"] = wgrad(p[0], dpp0, False, "wg_ple_proj0")
    G["ple_w_gate0"] = wgrad(x2b, dgl0, True, "wg_ple_gate0")
    dpre_mlp0, dpre_mlp0b, dm0, g_mlp_g0, g_mlp_b0 = mlp_bwd1(dx2, pre_mlp0, row(small["mlp_ln_g"], 0), r0,
                                                              W["mlp_w_down0"], "mlp_bwd1_0")
    G["mlp_w_down0"] = wgrad(r0, dpre_mlp0b, True, "wg_mlp_down0")
    G["mlp_w_up0"] = wgrad(x1b, dm0, False, "wg_mlp_up0")
    dpre_mix0, dpre_mix0b, dsw, g_mix_g0, g_mix_b0, g_b_out = mlp_bwd2(dpre_mlp0, dm0, W["mlp_w_up0"], alpha, pre_mix0,
                                                                      row(small["mix_ln_g"], 0), W["conv_w_out"],
                                                                      "mlp_bwd2_0")
    G["conv_w_out"] = wgrad(s, dpre_mix0b, True, "wg_conv_out")
    dcv, g_cln_g, g_cln_b, g_b_dw = conv_mid_bwd(dsw, cv, small["conv_ln_g"], small["conv_ln_b"])
    dh, g_w_dw, g_b_in = dwconv_bwd(dcv, h, small["conv_w_dw"], taps)
    G["conv_w_in"] = wgrad(x0b, dh, False, "wg_conv_in")
    grad_x = conv_in_bwd(dh, dpre_mix0, W["conv_w_in"], alpha)

    sg["conv_b_in"] = g_b_in
    sg["conv_w_dw"] = g_w_dw
    sg["conv_b_dw"], sg["conv_ln_g"], sg["conv_ln_b"], sg["conv_b_out"] = g_b_dw, g_cln_g, g_cln_b, g_b_out
    sg["mix_ln_g"] = jnp.concatenate([g_mix_g0, g_mix_g1], axis=0)
    sg["mix_ln_b"] = jnp.concatenate([g_mix_b0, g_mix_b1], axis=0)
    sg["mlp_ln_g"] = jnp.concatenate([g_mlp_g0, g_mlp_g1], axis=0)
    sg["mlp_ln_b"] = jnp.concatenate([g_mlp_b0, g_mlp_b1], axis=0)
    sg["attn_sinks"] = dsinks[:, 0][None, :]
    return loss, grad_x, G, sg


BUF_A = ("mlp_w_up0", "mlp_w_up1", "mlp_w_down0", "mlp_w_down1", "ple_w_gate0", "ple_w_gate1", "conv_w_out", "attn_w_q",
         "attn_w_o")
BUF_B = ("conv_w_in",)
BUF_C = ("kv_w_k", "kv_w_v", "ple_w_proj0", "ple_w_proj1")
ROW_SHARDED = {"mlp_w_down0", "mlp_w_down1", "ple_w_gate0", "ple_w_gate1", "conv_w_out", "attn_w_q", "attn_w_o", "kv_w_k",
               "kv_w_v"}


def _split_layers(weights):
    out = {"conv_w_in": weights["conv_w_in"][0], "conv_w_out": weights["conv_w_out"][0],
           "attn_w_q": weights["attn_w_q"][0], "attn_w_o": weights["attn_w_o"][0],
           "kv_w_k": weights["kv_w_k"], "kv_w_v": weights["kv_w_v"]}
    for n in ("mlp_w_up", "mlp_w_down", "ple_w_proj", "ple_w_gate"):
        for i in range(weights[n].shape[0]):
            out[n + str(i)] = weights[n][i]
    return out


def _layout(shards):
    lay = {}
    for key, names in (("a", BUF_A), ("b", BUF_B), ("c", BUF_C)):
        off, rows = 0, []
        for n in names:
            rows.append((n, off, shards[n].shape[0]))
            off += shards[n].shape[0]
        lay[key] = rows
    return lay


def _place():
    return lax.axis_index("x"), lax.axis_index("y"), lax.axis_index("c")


def _flip(v, f):
    return (v + f) % 2 if f else v


CHIP_FLIPS = ((1, 0), (0, 1), (1, 1))


def gather_weights(bufs, small):
    nb = len(bufs)

    def body(*refs):
        outs = refs[nb + 1:2 * nb + 2]
        send, recv, fsend, frecv = refs[2 * nb + 2:]
        x, y, c = _place()
        me = 2 * x + y

        def half(ref, k):
            hrows = bufs[k].shape[1] // 2
            return ref.at[pl.ds(pl.multiple_of(c * hrows, 16), hrows), :]

        sends, fwd = [], []
        for d, (fx, fy) in enumerate(CHIP_FLIPS):
            to = (_flip(x, fx), _flip(y, fy), c)
            frm = 2 * _flip(x, fx) + _flip(y, fy)
            for k in range(nb):
                mine = half(outs[k].at[me], k)
                sends.append(pltpu.make_async_remote_copy(mine, mine, send.at[d * (nb + 1) + k],
                                                          recv.at[d * (nb + 1) + k], device_id=to, device_id_type=MESH))
                theirs = half(outs[k].at[frm], k)
                fwd.append(pltpu.make_async_remote_copy(theirs, theirs, fsend.at[d * nb + k], frecv.at[d * nb + k],
                                                        device_id=(x, y, 1 - c), device_id_type=MESH))
            sends.append(pltpu.make_async_remote_copy(outs[nb].at[me], outs[nb].at[me], send.at[d * (nb + 1) + nb],
                                                      recv.at[d * (nb + 1) + nb], device_id=to, device_id_type=MESH))
        for cp in sends:
            cp.start()
        for d in range(len(CHIP_FLIPS)):
            for k in range(nb):
                sends[d * (nb + 1) + k].wait_recv()
                fwd[d * nb + k].start()
            sends[d * (nb + 1) + nb].wait_recv()
        for cp in fwd:
            cp.wait_recv()
        for cp in sends + fwd:
            cp.wait_send()

    arrs = list(bufs) + [small]
    nd = len(CHIP_FLIPS)
    return pl.pallas_call(
        body, name="gather_weights", in_specs=[ANY] * (nb + 1), out_specs=[ANY] * (nb + 1),
        out_shape=[_sds(a.shape, a.dtype) for a in arrs], input_output_aliases={k: k for k in range(nb + 1)},
        scratch_shapes=[pltpu.SemaphoreType.DMA((nd * (nb + 1),)), pltpu.SemaphoreType.DMA((nd * (nb + 1),)),
                        pltpu.SemaphoreType.DMA((nd * nb,)), pltpu.SemaphoreType.DMA((nd * nb,))])(*arrs)


def sibling_exchange(grads, small):
    nb = len(grads)

    def body(*refs):
        ins, outs = refs[:nb + 1], refs[nb + 1:2 * nb + 2]
        send, recv, lsem = refs[2 * nb + 2:]
        x, y, c = _place()
        me = 4 * x + 2 * y + c
        cps = []
        for k in range(nb):
            hrows = grads[k].shape[1] // 2
            src = ins[k].at[:, pl.ds(pl.multiple_of((1 - c) * hrows, 16), hrows), :]
            cps.append(pltpu.make_async_remote_copy(src, outs[k], send.at[k], recv.at[k], device_id=(x, y, 1 - c),
                                                    device_id_type=MESH))
        n = nb
        for fx in (0, 1):
            for fy in (0, 1):
                for fc in (0, 1):
                    if fx or fy or fc:
                        cps.append(pltpu.make_async_remote_copy(
                            ins[nb], outs[nb].at[me], send.at[n], recv.at[n],
                            device_id=(_flip(x, fx), _flip(y, fy), _flip(c, fc)), device_id_type=MESH))
                        n += 1
        own = pltpu.make_async_copy(ins[nb], outs[nb].at[me], lsem)
        own.start()
        for cp in cps:
            cp.start()
        for cp in cps:
            cp.wait()
        own.wait()

    shapes = [_sds((NS, g.shape[1] // 2, g.shape[2]), g.dtype) for g in grads] + [_sds((8,) + small.shape, small.dtype)]
    return pl.pallas_call(
        body, name="sibling_exchange", in_specs=[ANY] * (nb + 1), out_specs=[ANY] * (nb + 1), out_shape=shapes,
        scratch_shapes=[pltpu.SemaphoreType.DMA((nb + 7,)), pltpu.SemaphoreType.DMA((nb + 7,)), pltpu.SemaphoreType.DMA(())])(
            *grads, small)


def chip_exchange(sums):
    nb = len(sums)

    def body(*refs):
        ins, outs = refs[:nb], refs[nb:2 * nb]
        send, recv = refs[2 * nb:]
        x, y, c = _place()
        cps = []
        for d, (fx, fy) in enumerate(CHIP_FLIPS):
            tx, ty = _flip(x, fx), _flip(y, fy)
            for k in range(nb):
                cps.append(pltpu.make_async_remote_copy(ins[k].at[2 * tx + ty], outs[k].at[d], send.at[d * nb + k],
                                                        recv.at[d * nb + k], device_id=(tx, ty, c), device_id_type=MESH))
        for cp in cps:
            cp.start()
        for cp in cps:
            cp.wait()

    nd = len(CHIP_FLIPS)
    return pl.pallas_call(
        body, name="chip_exchange", in_specs=[ANY] * nb, out_specs=[ANY] * nb,
        out_shape=[_sds((nd,) + s.shape[1:], s.dtype) for s in sums],
        scratch_shapes=[pltpu.SemaphoreType.DMA((nd * nb,)), pltpu.SemaphoreType.DMA((nd * nb,))])(*sums)


def sibling_share(halves):
    nb = len(halves)

    def body(*refs):
        outs = refs[nb:2 * nb]
        send, recv = refs[2 * nb:]
        x, y, c = _place()
        cps = []
        for k in range(nb):
            hrows = halves[k].shape[0] // 2
            mine = outs[k].at[pl.ds(pl.multiple_of(c * hrows, 8), hrows), :]
            cps.append(pltpu.make_async_remote_copy(mine, mine, send.at[k], recv.at[k], device_id=(x, y, 1 - c),
                                                    device_id_type=MESH))
        for cp in cps:
            cp.start()
        for cp in cps:
            cp.wait()

    return pl.pallas_call(
        body, name="sibling_share", in_specs=[ANY] * nb, out_specs=[ANY] * nb,
        out_shape=[_sds(h.shape, h.dtype) for h in halves], input_output_aliases={k: k for k in range(nb)},
        scratch_shapes=[pltpu.SemaphoreType.DMA((nb,)), pltpu.SemaphoreType.DMA((nb,))])(*halves)


def _row_tile(rows):
    for cand in (512, 384, 256, 128, 64, 32, 16):
        if rows % cand == 0:
            return cand
    return rows


def pair_sum(g, r, idx, name):
    _, hrows, W = r.shape
    tr = _row_tile(hrows)
    nrb = hrows // tr

    def body(idx_ref, g_ref, r_ref, o_ref):
        o_ref[...] = (g_ref[...].astype(F32) + r_ref[...].astype(F32)).astype(BF16)

    gs = pltpu.PrefetchScalarGridSpec(
        num_scalar_prefetch=1, grid=(NS, nrb),
        in_specs=[pl.BlockSpec((None, tr, W), lambda j, i, s: (j, s[1] * nrb + i, 0)),
                  pl.BlockSpec((None, tr, W), lambda j, i, s: (j, i, 0))],
        out_specs=pl.BlockSpec((None, tr, W), lambda j, i, s: (j, i, 0)))
    return pl.pallas_call(body, name=name, grid_spec=gs, out_shape=_sds(r.shape, BF16),
                          compiler_params=pltpu.CompilerParams(dimension_semantics=("parallel", "parallel")))(idx, g, r)


def chip_sum(s, t, idx, name):
    _, hrows, W = s.shape
    tr = _row_tile(hrows)
    nrb = hrows // tr

    def body(idx_ref, s_ref, t_ref, o_ref):
        acc = s_ref[...].astype(F32)
        for d in range(t.shape[0]):
            acc = acc + t_ref[d].astype(F32)
        o_ref[...] = acc

    gs = pltpu.PrefetchScalarGridSpec(
        num_scalar_prefetch=1, grid=(nrb,),
        in_specs=[pl.BlockSpec((None, tr, W), lambda i, sc: (sc[0], i, 0)),
                  pl.BlockSpec((t.shape[0], tr, W), lambda i, sc: (0, i, 0))],
        out_specs=pl.BlockSpec((tr, W), lambda i, sc: (sc[1] * nrb + i, 0)))
    return pl.pallas_call(body, name=name, grid_spec=gs, out_shape=_sds((2 * hrows, W), F32),
                          compiler_params=pltpu.CompilerParams(dimension_semantics=("parallel",)))(idx, s, t)


def small_sum(packs):
    n, R, W = packs.shape

    def body(p_ref, o_ref):
        acc = p_ref[0]
        for d in range(1, n):
            acc = acc + p_ref[d]
        o_ref[...] = acc

    return pl.pallas_call(body, name="small_sum", out_shape=_sds((R, W), F32))(packs)


WEIGHTS = ["conv_w_in", "conv_b_in", "conv_w_dw", "conv_b_dw", "conv_ln_g", "conv_ln_b", "conv_w_out", "conv_b_out", "kv_w_k",
           "kv_w_v", "attn_w_q", "attn_sinks", "attn_w_o", "mix_ln_g", "mix_ln_b", "mlp_w_up", "mlp_w_down", "mlp_ln_g",
           "mlp_ln_b", "ple_w_proj", "ple_w_gate"]
BIG = ["conv_w_in", "conv_w_out", "kv_w_k", "kv_w_v", "attn_w_q", "attn_w_o", "mlp_w_up", "mlp_w_down", "ple_w_proj",
       "ple_w_gate"]
SMALL = [n for n in WEIGHTS if n not in BIG]
SHARDED_SMALL = ["conv_b_in", "conv_w_dw", "conv_b_dw", "conv_ln_g", "conv_ln_b", "conv_b_out"]


def _flat128(a):
    f = a.reshape(-1)
    pad = (-f.shape[0]) % 128
    if pad:
        f = jnp.concatenate([f, jnp.zeros((pad,), f.dtype)])
    return f


def _step(x, p, target, w, m, v):
    D = x.shape[-1]
    ds = D // NS
    xq, yq, cq = _place()
    chip = 2 * xq + yq
    idx = jnp.stack([chip, cq]).astype(jnp.int32)

    shards = _split_layers(w)
    lay = _layout(shards)
    packed = [jnp.concatenate([shards[n].astype(BF16) for n, _, _ in lay[key]], axis=0) for key in ("a", "b", "c")]
    taps = w["conv_w_dw"].shape[1]
    small_loc = jnp.concatenate(
        [w["conv_w_dw"][0], jnp.zeros((HALO - taps, ds), F32), w["conv_b_dw"], w["conv_ln_g"], w["conv_ln_b"], w["conv_b_out"],
         w["conv_b_in"].reshape(2, ds), jnp.zeros((2, ds), F32)], axis=0)
    slot = lambda a: lax.dynamic_update_slice(jnp.zeros((NS,) + a.shape, a.dtype), a[None], (chip, 0, 0))
    ga, gb, gc, gs = gather_weights([slot(a) for a in packed], slot(small_loc))
    W = {}
    for key, buf in (("a", ga), ("b", gb), ("c", gc)):
        for n, off, rows in lay[key]:
            W[n] = (buf, off, rows)
    across = lambda rows: gs[:, rows, :].transpose(1, 0, 2).reshape(rows.stop - rows.start, D)
    small = {"taps": taps, "conv_w_dw": across(slice(0, HALO)), "conv_b_dw": across(slice(HALO, HALO + 1)),
             "conv_ln_g": across(slice(HALO + 1, HALO + 2)), "conv_ln_b": across(slice(HALO + 2, HALO + 3)),
             "conv_b_out": across(slice(HALO + 3, HALO + 4)), "conv_b_in": gs[:, HALO + 4:HALO + 6, :].reshape(1, 2 * D),
             "attn_sinks": w["attn_sinks"], "mix_ln_g": w["mix_ln_g"], "mix_ln_b": w["mix_ln_b"],
             "mlp_ln_g": w["mlp_ln_g"], "mlp_ln_b": w["mlp_ln_b"]}

    loss, grad_x, G, sg = _local_step(x[0], p[:, 0], target[0], W, small)

    parts = [jnp.concatenate([G[n] for n, _, _ in lay[key]], axis=1) for key in ("a", "b", "c")]
    nsink = sg["attn_sinks"].shape[1]
    pack = jnp.concatenate(
        [sg["conv_b_in"].reshape(2, D), sg["conv_w_dw"], sg["conv_b_dw"], sg["conv_ln_g"], sg["conv_ln_b"], sg["conv_b_out"],
         sg["mix_ln_g"], sg["mix_ln_b"], sg["mlp_ln_g"], sg["mlp_ln_b"],
         jnp.concatenate([sg["attn_sinks"], jnp.zeros((1, D - nsink), F32)], axis=1),
         jnp.concatenate([loss[0:1], jnp.zeros((1, D - loss.shape[1]), F32)], axis=1)], axis=0)
    *from_sibling, packs = sibling_exchange(parts, pack)
    sums = [pair_sum(g, r, idx, "pair_sum_" + key) for g, r, key in zip(parts, from_sibling, "abc")]
    from_chips = chip_exchange(sums)
    halves = [chip_sum(s, t, idx, "chip_sum_" + key) for s, t, key in zip(sums, from_chips, "abc")]
    full = sibling_share(halves)
    tot = small_sum(packs)

    grads = {}
    for key, buf in zip(("a", "b", "c"), full):
        for n, off, rows in lay[key]:
            grads[n] = buf[off:off + rows]
    for n in ("mlp_w_up", "mlp_w_down", "ple_w_proj", "ple_w_gate"):
        grads[n] = jnp.stack([grads.pop(n + str(i)) for i in range(w[n].shape[0])])
    for n in ("conv_w_in", "conv_w_out", "attn_w_q", "attn_w_o"):
        grads[n] = grads[n][None]
    cols = lambda rows: lax.dynamic_slice(rows, (0, chip * ds), (rows.shape[0], ds))
    grads["conv_b_in"] = lax.dynamic_slice(tot[0:2].reshape(1, 2 * D), (0, chip * 2 * ds), (1, 2 * ds))
    grads["conv_w_dw"] = cols(tot[2:2 + taps])[None]
    r0 = 2 + HALO
    for i, n in enumerate(("conv_b_dw", "conv_ln_g", "conv_ln_b", "conv_b_out")):
        grads[n] = cols(tot[r0 + i:r0 + i + 1])
    r0 += 4
    for i, n in enumerate(("mix_ln_g", "mix_ln_b", "mlp_ln_g", "mlp_ln_b")):
        grads[n] = tot[r0 + 2 * i:r0 + 2 * i + 2]
    grads["attn_sinks"] = tot[r0 + 8:r0 + 9, 0:nsink]

    delta, new_m, new_v = {}, {}, {}
    for n in BIG:
        shp = w[n].shape
        two = lambda a: a.reshape(-1, shp[-1])
        d_, m_, v_ = adamw(two(w[n]), two(grads[n]), two(m[n]), two(v[n]), "adamw_" + n)
        delta[n], new_m[n], new_v[n] = d_.reshape(shp), m_.reshape(shp), v_.reshape(shp)
    flat = lambda t: jnp.concatenate([_flat128(t[n]) for n in SMALL])
    fw, fg, fm, fv = flat(w), flat(grads), flat(m), flat(v)
    pad = (-fw.shape[0]) % 1024
    rs = lambda f: jnp.concatenate([f, jnp.ones((pad,), F32)]).reshape(-1, 128)
    d_, m_, v_ = adamw(rs(fw), rs(fg), rs(fm), rs(fv), "adamw_small")
    pos = 0
    for n in SMALL:
        size = w[n].size
        take = lambda a: a.reshape(-1)[pos:pos + size].reshape(w[n].shape)
        delta[n], new_m[n], new_v[n] = take(d_), take(m_), take(v_)
        pos += size + (-size) % 128

    total = tot[r0 + 9, 0]
    return (total, grad_x[None], *[grads[n] for n in WEIGHTS], *[delta[n] for n in WEIGHTS], *[new_m[n] for n in WEIGHTS],
            *[new_v[n] for n in WEIGHTS])


def kernel(x, p, conv_w_in, conv_b_in, conv_w_dw, conv_b_dw, conv_ln_g, conv_ln_b, conv_w_out, conv_b_out, kv_w_k, kv_w_v, attn_w_q, attn_sinks, attn_w_o, mix_ln_g, mix_ln_b, mlp_w_up, mlp_w_down, mlp_ln_g, mlp_ln_b, ple_w_proj, ple_w_gate, loss_target, m_conv_w_in, m_conv_b_in, m_conv_w_dw, m_conv_b_dw, m_conv_ln_g, m_conv_ln_b, m_conv_w_out, m_conv_b_out, m_kv_w_k, m_kv_w_v, m_attn_w_q, m_attn_sinks, m_attn_w_o, m_mix_ln_g, m_mix_ln_b, m_mlp_w_up, m_mlp_w_down, m_mlp_ln_g, m_mlp_ln_b, m_ple_w_proj, m_ple_w_gate, v_conv_w_in, v_conv_b_in, v_conv_w_dw, v_conv_b_dw, v_conv_ln_g, v_conv_ln_b, v_conv_w_out, v_conv_b_out, v_kv_w_k, v_kv_w_v, v_attn_w_q, v_attn_sinks, v_attn_w_o, v_mix_ln_g, v_mix_ln_b, v_mlp_w_up, v_mlp_w_down, v_mlp_ln_g, v_mlp_ln_b, v_ple_w_proj, v_ple_w_gate):
    w = dict(zip(WEIGHTS, (conv_w_in, conv_b_in, conv_w_dw, conv_b_dw, conv_ln_g, conv_ln_b, conv_w_out, conv_b_out, kv_w_k,
                           kv_w_v, attn_w_q, attn_sinks, attn_w_o, mix_ln_g, mix_ln_b, mlp_w_up, mlp_w_down, mlp_ln_g, mlp_ln_b,
                           ple_w_proj, ple_w_gate)))
    m = dict(zip(WEIGHTS, (m_conv_w_in, m_conv_b_in, m_conv_w_dw, m_conv_b_dw, m_conv_ln_g, m_conv_ln_b, m_conv_w_out,
                           m_conv_b_out, m_kv_w_k, m_kv_w_v, m_attn_w_q, m_attn_sinks, m_attn_w_o, m_mix_ln_g, m_mix_ln_b,
                           m_mlp_w_up, m_mlp_w_down, m_mlp_ln_g, m_mlp_ln_b, m_ple_w_proj, m_ple_w_gate)))
    v = dict(zip(WEIGHTS, (v_conv_w_in, v_conv_b_in, v_conv_w_dw, v_conv_b_dw, v_conv_ln_g, v_conv_ln_b, v_conv_w_out,
                           v_conv_b_out, v_kv_w_k, v_kv_w_v, v_attn_w_q, v_attn_sinks, v_attn_w_o, v_mix_ln_g, v_mix_ln_b,
                           v_mlp_w_up, v_mlp_w_down, v_mlp_ln_g, v_mlp_ln_b, v_ple_w_proj, v_ple_w_gate)))
    return _step(x, p, loss_target, w, m, v)
```

```python
import functools

import jax
import jax.numpy as jnp
from jax import lax
from jax.experimental import pallas as pl
from jax.experimental.pallas import tpu as pltpu

F32 = jnp.float32
BF16 = jnp.bfloat16
NS = 4
HEAD = 64
BLK = 128
ROPE = 16
ROPE_THETA = 500000.0
LN_EPS = 1e-5
NEG = -1e30
HALO = 32
ADAM_LR, ADAM_B1, ADAM_B2, ADAM_EPS, ADAM_WD, ADAM_STEP = 0.001, 0.9, 0.999, 1e-08, 0.01, 10
MESH = pl.DeviceIdType.MESH
ANY = pl.BlockSpec(memory_space=pl.ANY)
NT = (((1,), (1,)), ((), ()))
TN = (((0,), (0,)), ((), ()))


def _pc(body, name, grid, in_specs, out_specs, out_shape, scratch=(), sem=None, vmem=56, **kw):
    return pl.pallas_call(
        body, name=name, grid=grid, in_specs=in_specs, out_specs=out_specs, out_shape=out_shape,
        scratch_shapes=list(scratch),
        compiler_params=pltpu.CompilerParams(dimension_semantics=sem, vmem_limit_bytes=vmem * 2 ** 20), **kw)


def _rows(tm, n):
    return pl.BlockSpec((tm, n), lambda i: (i, 0))


def _const(shape):
    return pl.BlockSpec(shape, lambda *_: (0,) * len(shape))


def _wspec(w):
    buf, off, rows = w
    assert off % rows == 0
    return pl.BlockSpec((NS, rows, buf.shape[2]), lambda *_: (0, off // rows, 0))


def _sds(shape, dtype):
    return jax.ShapeDtypeStruct(shape, dtype)


def _tile(t):
    return min(256, t)


def _sigmoid(x):
    return 1.0 / (1.0 + jnp.exp(-x))


def _ln_stats(w):
    mu = jnp.mean(w, axis=-1, keepdims=True)
    xc = w - mu
    var = jnp.mean(xc * xc, axis=-1, keepdims=True)
    rstd = lax.rsqrt(var + LN_EPS)
    return xc * rstd, rstd


def _ln_bwd(dy, w, g):
    xhat, rstd = _ln_stats(w)
    dxhat = dy * g
    m1 = jnp.mean(dxhat, axis=-1, keepdims=True)
    m2 = jnp.mean(dxhat * xhat, axis=-1, keepdims=True)
    dw = rstd * (dxhat - m1 - xhat * m2)
    return dw, jnp.sum(dy * xhat, axis=0, keepdims=True), jnp.sum(dy, axis=0, keepdims=True)


def _acc_rows(ref, val, first):
    @pl.when(first)
    def _():
        ref[...] = val

    @pl.when(jnp.logical_not(first))
    def _():
        ref[...] += val


def conv_in_fwd(xb, w_in, b_in):
    T, D = xb.shape
    nw = w_in[0].shape[2]
    tm = _tile(T)

    def body(x_ref, w_ref, b_ref, h_ref):
        x = x_ref[...]
        for j in range(NS):
            sl = slice(j * nw, (j + 1) * nw)
            h_ref[:, sl] = (jnp.dot(x, w_ref[j], preferred_element_type=F32) + b_ref[:, sl]).astype(BF16)

    return _pc(body, "conv_in_fwd", (T // tm,), [_rows(tm, D), _wspec(w_in), _const((1, NS * nw))],
               _rows(tm, NS * nw), _sds((T, NS * nw), BF16), sem=("parallel",))(xb, w_in[0], b_in)


CONV_ROWS = 16


def _phases(scr, sh):
    n = scr.shape[0] - 8
    for b in range(1, 8):
        sh[b - 1, 0:n, :] = scr[b:b + n, :]


def _spread(w_ref, wb, taps):
    for j in range(taps):
        wb[j] = jnp.broadcast_to(w_ref[j:j + 1, :], wb.shape[1:])


def _tap(scr, sh, o, n):
    b = o % 8
    return scr[o:o + n, :] if b == 0 else sh[b - 1, o - b:o - b + n, :]


def dwconv_fwd(h, w_dw, b_dw, ln_g, ln_b, taps):
    T = h.shape[0]
    C = h.shape[1] // 2
    tq = _tile(T)
    nh = tq // HALO
    off = HALO - (taps - 1)

    def body(a_ref, g_ref, ap_ref, gp_ref, w_ref, bdw_ref, lg_ref, lb_ref, cv_ref, s_ref, scr, sh, wb):
        i = pl.program_id(0)
        scr[HALO:HALO + tq, :] = a_ref[...].astype(F32) * _sigmoid(g_ref[...].astype(F32))
        up = ap_ref[...].astype(F32) * _sigmoid(gp_ref[...].astype(F32))
        scr[0:HALO, :] = jnp.where(i > 0, up, 0.0)
        _phases(scr, sh)
        _spread(w_ref, wb, taps)
        bias = jnp.broadcast_to(bdw_ref[...], (8, C))
        for r in range(tq // CONV_ROWS):
            accs = [bias] * (CONV_ROWS // 8)
            for j in range(taps):
                wj = wb[j]
                accs = [acc + wj * _tap(scr, sh, off + j + r * CONV_ROWS + 8 * k, 8) for k, acc in enumerate(accs)]
            for k, acc in enumerate(accs):
                cv_ref[r * CONV_ROWS + 8 * k:r * CONV_ROWS + 8 * k + 8, :] = acc
        xhat, _ = _ln_stats(cv_ref[...])
        ln = xhat * lg_ref[...] + lb_ref[...]
        s_ref[...] = (ln * _sigmoid(ln)).astype(BF16)

    prev = lambda col: pl.BlockSpec((HALO, C), lambda i: (jnp.maximum(i * nh - 1, 0), col))
    cur = lambda col: pl.BlockSpec((tq, C), lambda i: (i, col))
    return _pc(body, "dwconv_fwd", (T // tq,),
               [cur(0), cur(1), prev(0), prev(1), _const((HALO, C)), _const((1, C)), _const((1, C)), _const((1, C))],
               [_rows(tq, C), _rows(tq, C)], [_sds((T, C), F32), _sds((T, C), BF16)],
               scratch=[pltpu.VMEM((HALO + tq, C), F32), pltpu.VMEM((7, HALO + tq, C), F32), pltpu.VMEM((taps, 8, C), F32)],
               sem=("parallel",))(h, h, h, h, w_dw, b_dw, ln_g, ln_b)


def mm_res_ln(a, w, res, g, b, alpha, bias, name):
    T, K = a.shape
    ks = K // NS
    D = res.shape[1]
    tm = _tile(T)

    def body(*refs):
        a_ref, w_ref, res_ref, g_ref, b_ref = refs[:5]
        n = 5
        if bias is not None:
            bias_ref = refs[5]
            n = 6
        pre_ref, xo_ref, xb_ref = refs[n:n + 3]
        acc = jnp.dot(a_ref[:, 0:ks], w_ref[0], preferred_element_type=F32)
        for j in range(1, NS):
            acc = acc + jnp.dot(a_ref[:, j * ks:(j + 1) * ks], w_ref[j], preferred_element_type=F32)
        if bias is not None:
            acc = acc + bias_ref[...]
        pre = alpha * res_ref[...] + acc
        xhat, _ = _ln_stats(pre)
        xo = xhat * g_ref[...] + b_ref[...]
        pre_ref[...] = pre
        xo_ref[...] = xo
        xb_ref[...] = xo.astype(BF16)

    ins = [_rows(tm, K), _wspec(w), _rows(tm, D), _const((1, D)), _const((1, D))]
    args = [a, w[0], res, g, b]
    if bias is not None:
        ins.append(_const((1, D)))
        args.append(bias)
    return _pc(body, name, (T // tm,), ins, [_rows(tm, D)] * 3, [_sds((T, D), F32), _sds((T, D), F32), _sds((T, D), BF16)],
               sem=("parallel",))(*args)


def mlp_up_fwd(xb, w_up, name):
    T, D = xb.shape
    fs = w_up[0].shape[2]
    tm = _tile(T)

    def body(x_ref, w_ref, r_ref):
        x = x_ref[...]
        for j in range(NS):
            m = jnp.maximum(jnp.dot(x, w_ref[j], preferred_element_type=F32), 0.0)
            r_ref[:, j * fs:(j + 1) * fs] = (m * m).astype(BF16)

    return _pc(body, name, (T // tm,), [_rows(tm, D), _wspec(w_up)], _rows(tm, NS * fs), _sds((T, NS * fs), BF16),
               sem=("parallel",))(xb, w_up[0])


def ple_fwd(x, xb, p, layer, w_proj, w_gate, target, name):
    T, D = x.shape
    P = p.shape[2]
    ds = D // NS
    tm = _tile(T)
    last = target is not None

    def body(*refs):
        x_ref, xb_ref, p_ref, wp_ref, wg_ref = refs[:5]
        n = 5
        if last:
            t_ref = refs[5]
            n = 6
        o_ref, o2_ref, pp_ref, gl_ref = refs[n:n + 4]
        gl = jnp.dot(xb_ref[:, 0:ds], wg_ref[0], preferred_element_type=F32)
        for j in range(1, NS):
            gl = gl + jnp.dot(xb_ref[:, j * ds:(j + 1) * ds], wg_ref[j], preferred_element_type=F32)
        gl_ref[...] = gl.astype(BF16)
        sg = _sigmoid(gl)
        pb = p_ref[...].astype(BF16)
        sq = jnp.zeros((1, 1), F32)
        for j in range(NS):
            sl = slice(j * ds, (j + 1) * ds)
            pp = jnp.dot(pb, wp_ref[j], preferred_element_type=F32)
            pp_ref[:, sl] = pp.astype(BF16)
            out = x_ref[:, sl] + pp * sg[:, sl]
            if last:
                err = out - t_ref[:, sl]
                o_ref[:, sl] = err * (1.0 / D)
                e2 = jnp.sum(err * err, axis=0, keepdims=True)
                sq = sq + jnp.sum(e2, axis=1, keepdims=True)
            else:
                o_ref[:, sl] = out
                o2_ref[:, sl] = out.astype(BF16)
        if last:
            _acc_rows(o2_ref, jnp.broadcast_to(sq * (0.5 / D), (8, 128)), pl.program_id(0) == 0)

    ins = [_rows(tm, D), _rows(tm, D), pl.BlockSpec((None, tm, P), lambda i: (layer, i, 0)), _wspec(w_proj), _wspec(w_gate)]
    args = [x, xb, p, w_proj[0], w_gate[0]]
    if last:
        ins.append(_rows(tm, D))
        args.append(target)
        outs = [_rows(tm, D), _const((8, 128)), _rows(tm, D), _rows(tm, D)]
        shapes = [_sds((T, D), F32), _sds((8, 128), F32), _sds((T, D), BF16), _sds((T, D), BF16)]
    else:
        outs = [_rows(tm, D)] * 4
        shapes = [_sds((T, D), F32), _sds((T, D), BF16), _sds((T, D), BF16), _sds((T, D), BF16)]
    return _pc(body, name, (T // tm,), ins, outs, shapes, sem=("arbitrary",) if last else ("parallel",))(*args)


def _rope(x, cs_ref, sign):
    c = cs_ref[0]
    s = cs_ref[1] * sign
    lane = lax.broadcasted_iota(jnp.int32, c.shape, 1)
    first = (lane % HEAD) < (ROPE // 2)
    outs = []
    for gq in range(x.shape[1] // 128):
        xg = x[:, gq * 128:(gq + 1) * 128]
        sw = jnp.where(first, pltpu.roll(xg, 128 - ROPE // 2, 1), pltpu.roll(xg, ROPE // 2, 1))
        outs.append(xg * c + sw * s)
    return outs


def qkv_fwd(xb, w_q, w_k, w_v, cs):
    T, D = xb.shape
    ds = D // NS
    HD, KVD = w_q[0].shape[2], w_k[0].shape[2]
    tm = _tile(T)
    scale = 1.0 / (HEAD ** 0.5)

    def body(x_ref, wq_ref, wk_ref, wv_ref, cs_ref, q_ref, k_ref, v_ref):
        def proj(w_ref):
            acc = jnp.dot(x_ref[:, 0:ds], w_ref[0], preferred_element_type=F32)
            for j in range(1, NS):
                acc = acc + jnp.dot(x_ref[:, j * ds:(j + 1) * ds], w_ref[j], preferred_element_type=F32)
            return acc

        for gq, val in enumerate(_rope(proj(wq_ref), cs_ref, 1.0)):
            q_ref[:, gq * 128:(gq + 1) * 128] = (val * scale).astype(BF16)
        for gq, val in enumerate(_rope(proj(wk_ref), cs_ref, 1.0)):
            k_ref[:, gq * 128:(gq + 1) * 128] = val.astype(BF16)
        v_ref[...] = proj(wv_ref).astype(BF16)

    cs_spec = pl.BlockSpec((2, tm, 128), lambda i: (0, i, 0))
    return _pc(body, "qkv_fwd", (T // tm,), [_rows(tm, D), _wspec(w_q), _wspec(w_k), _wspec(w_v), cs_spec],
               [_rows(tm, HD), _rows(tm, KVD), _rows(tm, KVD)],
               [_sds((T, HD), BF16), _sds((T, KVD), BF16), _sds((T, KVD), BF16)], sem=("parallel",))(
                   xb, w_q[0], w_k[0], w_v[0], cs)


def _band_mask(n):
    row = lax.broadcasted_iota(jnp.int32, (BLK, 2 * BLK), 0)
    col = lax.broadcasted_iota(jnp.int32, (BLK, 2 * BLK), 1)
    return (col > row) & (col <= row + BLK) & ((col >= BLK) | (n > 0))


def _softmax_sink(s, sink):
    m = jnp.maximum(jnp.max(s, axis=-1, keepdims=True), sink)
    e = jnp.exp(s - m)
    es = jnp.exp(sink - m)
    den = jnp.sum(e, axis=-1, keepdims=True) + es
    return e / den, es / den


def attn_fwd(q, k, v, sinks):
    NH, T, _ = q.shape
    NKV = k.shape[0]
    G = NH // NKV

    def body(s_ref, q_ref, kc_ref, kp_ref, vc_ref, vp_ref, o_ref):
        valid = _band_mask(pl.program_id(0))
        for kh in range(NKV):
            k2 = jnp.concatenate([kp_ref[kh], kc_ref[kh]], axis=0)
            v2 = jnp.concatenate([vp_ref[kh], vc_ref[kh]], axis=0)
            hs = [kh * G + gq for gq in range(G)]
            sc = [lax.dot_general(q_ref[hh], k2, NT, preferred_element_type=F32) for hh in hs]
            pb = [_softmax_sink(jnp.where(valid, s, NEG), s_ref[0, hh])[0].astype(BF16) for s, hh in zip(sc, hs)]
            for p, hh in zip(pb, hs):
                o_ref[hh] = jnp.dot(p, v2, preferred_element_type=F32).astype(BF16)

    cur = lambda nh: pl.BlockSpec((nh, BLK, HEAD), lambda n: (0, n, 0))
    prev = lambda nh: pl.BlockSpec((nh, BLK, HEAD), lambda n: (0, jnp.maximum(n - 1, 0), 0))
    return _pc(body, "attn_fwd", (T // BLK,),
               [pl.BlockSpec(memory_space=pltpu.SMEM), cur(NH), cur(NKV), prev(NKV), cur(NKV), prev(NKV)],
               cur(NH), _sds((NH, T, HEAD), BF16), sem=("parallel",))(sinks, q, k, k, v, v)


def ple_bwd(dxo, pp, gl, w_gate, name):
    T, D = dxo.shape
    ds = D // NS
    tm = _tile(T)

    def body(d_ref, pp_ref, gl_ref, wg_ref, dpp_ref, dgl_ref, dx_ref):
        d = d_ref[...]
        sg = _sigmoid(gl_ref[...].astype(F32))
        dpp_ref[...] = (d * sg).astype(BF16)
        dgl = (d * pp_ref[...].astype(F32) * sg * (1.0 - sg)).astype(BF16)
        dgl_ref[...] = dgl
        for j in range(NS):
            sl = slice(j * ds, (j + 1) * ds)
            dx_ref[:, sl] = d_ref[:, sl] + lax.dot_general(dgl, wg_ref[j], NT, preferred_element_type=F32)

    return _pc(body, name, (T // tm,), [_rows(tm, D)] * 3 + [_wspec(w_gate)], [_rows(tm, D)] * 3,
               [_sds((T, D), BF16), _sds((T, D), BF16), _sds((T, D), F32)], sem=("parallel",))(dxo, pp, gl, w_gate[0])


def mlp_bwd1(dy, pre, g, r, w_down, name):
    T, D = dy.shape
    fs = w_down[2]
    tm = _tile(T)

    def body(dy_ref, pre_ref, g_ref, r_ref, w_ref, dw_ref, dwb_ref, dm_ref, dg_ref, db_ref):
        dw, dg, db = _ln_bwd(dy_ref[...], pre_ref[...], g_ref[...])
        first = pl.program_id(0) == 0
        _acc_rows(dg_ref, dg, first)
        _acc_rows(db_ref, db, first)
        dwb = dw.astype(BF16)
        dw_ref[...] = dw
        dwb_ref[...] = dwb
        for j in range(NS):
            sl = slice(j * fs, (j + 1) * fs)
            dr = lax.dot_general(dwb, w_ref[j], NT, preferred_element_type=F32)
            dm_ref[:, sl] = (dr * (2.0 * jnp.sqrt(r_ref[:, sl].astype(F32)))).astype(BF16)

    return _pc(body, name, (T // tm,), [_rows(tm, D), _rows(tm, D), _const((1, D)), _rows(tm, NS * fs), _wspec(w_down)],
               [_rows(tm, D), _rows(tm, D), _rows(tm, NS * fs), _const((1, D)), _const((1, D))],
               [_sds((T, D), F32), _sds((T, D), BF16), _sds((T, NS * fs), BF16), _sds((1, D), F32), _sds((1, D), F32)],
               sem=("arbitrary",))(dy, pre, g, r, w_down[0])


def mlp_bwd2(dpre, dm, w_up, alpha, pre_mix, g_mix, w_mix, name):
    T, D = dpre.shape
    fs = w_up[0].shape[2]
    ms = w_mix[2]
    tm = _tile(T)

    def body(dp_ref, dm_ref, wu_ref, pre_ref, g_ref, wm_ref, dw_ref, dwb_ref, do_ref, dg_ref, db_ref, dc_ref):
        dy = alpha * dp_ref[...]
        for j in range(NS):
            dy = dy + lax.dot_general(dm_ref[:, j * fs:(j + 1) * fs], wu_ref[j], NT, preferred_element_type=F32)
        dw, dg, db = _ln_bwd(dy, pre_ref[...], g_ref[...])
        first = pl.program_id(0) == 0
        _acc_rows(dg_ref, dg, first)
        _acc_rows(db_ref, db, first)
        _acc_rows(dc_ref, jnp.sum(dw, axis=0, keepdims=True), first)
        dwb = dw.astype(BF16)
        dw_ref[...] = dw
        dwb_ref[...] = dwb
        for j in range(NS):
            do_ref[:, j * ms:(j + 1) * ms] = lax.dot_general(dwb, wm_ref[j], NT, preferred_element_type=F32).astype(BF16)

    return _pc(body, name, (T // tm,),
               [_rows(tm, D), _rows(tm, NS * fs), _wspec(w_up), _rows(tm, D), _const((1, D)), _wspec(w_mix)],
               [_rows(tm, D), _rows(tm, D), _rows(tm, NS * ms), _const((1, D)), _const((1, D)), _const((1, D))],
               [_sds((T, D), F32), _sds((T, D), BF16), _sds((T, NS * ms), BF16)] + [_sds((1, D), F32)] * 3,
               sem=("arbitrary",))(dpre, dm, w_up[0], pre_mix, g_mix, w_mix[0])


def attn_bwd(q, k, v, do, sinks):
    NH, T, _ = q.shape
    NKV = k.shape[0]
    G = NH // NKV
    nb = T // BLK

    def body(s_ref, q_ref, do_ref, kc_ref, kp_ref, vc_ref, vp_ref, dq_ref, dk_ref, dv_ref, ds_ref, ck, cv):
        n = pl.program_id(0)

        @pl.when(n == 0)
        def _():
            ck[...] = jnp.zeros_like(ck)
            cv[...] = jnp.zeros_like(cv)
            ds_ref[...] = jnp.zeros_like(ds_ref)

        @pl.when(n < nb)
        def _():
            valid = _band_mask(n)
            for kh in range(NKV):
                k2 = jnp.concatenate([kp_ref[kh], kc_ref[kh]], axis=0)
                v2 = jnp.concatenate([vp_ref[kh], vc_ref[kh]], axis=0)
                hs = [kh * G + gq for gq in range(G)]
                sc = [lax.dot_general(q_ref[hh], k2, NT, preferred_element_type=F32) for hh in hs]
                dp = [lax.dot_general(do_ref[hh], v2, NT, preferred_element_type=F32) for hh in hs]
                pr = [_softmax_sink(jnp.where(valid, s, NEG), s_ref[0, hh]) for s, hh in zip(sc, hs)]
                delta = [jnp.sum(p * d, axis=-1, keepdims=True) for (p, _), d in zip(pr, dp)]
                dsb = [(p * (d - dl)).astype(BF16) for (p, _), d, dl in zip(pr, dp, delta)]
                pb = [p.astype(BF16) for p, _ in pr]
                for (_, ps), dl, hh in zip(pr, delta, hs):
                    ds_ref[hh:hh + 1, :] += jnp.broadcast_to(-jnp.sum(ps * dl, axis=0, keepdims=True), (1, 128))
                for d, hh in zip(dsb, hs):
                    dq_ref[hh] = jnp.dot(d, k2, preferred_element_type=F32)
                dk2 = lax.dot_general(jnp.concatenate(dsb, axis=0), q_ref[kh * G:(kh + 1) * G].reshape(G * BLK, HEAD), TN,
                                      preferred_element_type=F32)
                dv2 = lax.dot_general(jnp.concatenate(pb, axis=0), do_ref[kh * G:(kh + 1) * G].reshape(G * BLK, HEAD), TN,
                                      preferred_element_type=F32)
                dk_ref[kh] = ck[kh] + dk2[0:BLK]
                dv_ref[kh] = cv[kh] + dv2[0:BLK]
                ck[kh] = dk2[BLK:2 * BLK]
                cv[kh] = dv2[BLK:2 * BLK]

        @pl.when(n == nb)
        def _():
            dk_ref[...] = ck[...]
            dv_ref[...] = cv[...]

    qcur = pl.BlockSpec((NH, BLK, HEAD), lambda n: (0, jnp.minimum(n, nb - 1), 0))
    kcur = pl.BlockSpec((NKV, BLK, HEAD), lambda n: (0, jnp.minimum(n, nb - 1), 0))
    kprev = pl.BlockSpec((NKV, BLK, HEAD), lambda n: (0, jnp.maximum(n - 1, 0), 0))
    return _pc(body, "attn_bwd", (nb + 1,),
               [pl.BlockSpec(memory_space=pltpu.SMEM), qcur, qcur, kcur, kprev, kcur, kprev],
               [qcur, kprev, kprev, _const((NH, 128))],
               [_sds((NH, T, HEAD), F32), _sds((NKV, T, HEAD), F32), _sds((NKV, T, HEAD), F32), _sds((NH, 128), F32)],
               scratch=[pltpu.VMEM((NKV, BLK, HEAD), F32), pltpu.VMEM((NKV, BLK, HEAD), F32)],
               sem=("arbitrary",))(sinks, q, do, k, k, v, v)


def qkv_bwd(dq, dk, dv, dpre_mix, w_q, w_k, w_v, cs, alpha):
    T, HD = dq.shape
    KVD = dk.shape[1]
    D = dpre_mix.shape[1]
    ds = D // NS
    tm = _tile(T)
    scale = 1.0 / (HEAD ** 0.5)

    def body(dq_ref, dk_ref, dv_ref, dp_ref, wq_ref, wk_ref, wv_ref, cs_ref, dqb_ref, dkb_ref, dvb_ref, dx_ref):
        for gq, val in enumerate(_rope(dq_ref[...], cs_ref, -1.0)):
            dqb_ref[:, gq * 128:(gq + 1) * 128] = (val * scale).astype(BF16)
        for gq, val in enumerate(_rope(dk_ref[...], cs_ref, -1.0)):
            dkb_ref[:, gq * 128:(gq + 1) * 128] = val.astype(BF16)
        dvb_ref[...] = dv_ref[...].astype(BF16)
        dqb, dkb, dvb = dqb_ref[...], dkb_ref[...], dvb_ref[...]
        for j in range(NS):
            sl = slice(j * ds, (j + 1) * ds)
            dx_ref[:, sl] = (alpha * dp_ref[:, sl]
                             + lax.dot_general(dqb, wq_ref[j], NT, preferred_element_type=F32)
                             + lax.dot_general(dkb, wk_ref[j], NT, preferred_element_type=F32)
                             + lax.dot_general(dvb, wv_ref[j], NT, preferred_element_type=F32))

    cs_spec = pl.BlockSpec((2, tm, 128), lambda i: (0, i, 0))
    return _pc(body, "qkv_bwd", (T // tm,),
               [_rows(tm, HD), _rows(tm, KVD), _rows(tm, KVD), _rows(tm, D), _wspec(w_q), _wspec(w_k), _wspec(w_v), cs_spec],
               [_rows(tm, HD), _rows(tm, KVD), _rows(tm, KVD), _rows(tm, D)],
               [_sds((T, HD), BF16), _sds((T, KVD), BF16), _sds((T, KVD), BF16), _sds((T, D), F32)],
               sem=("parallel",))(dq, dk, dv, dpre_mix, w_q[0], w_k[0], w_v[0], cs)


def conv_mid_bwd(ds, cv, ln_g, ln_b):
    T, C = cv.shape
    tm = _tile(T)

    def body(ds_ref, cv_ref, g_ref, b_ref, dcv_ref, dg_ref, db_ref, dc_ref):
        xhat, _ = _ln_stats(cv_ref[...])
        ln = xhat * g_ref[...] + b_ref[...]
        sg = _sigmoid(ln)
        dl = ds_ref[...].astype(F32) * (sg * (1.0 + ln * (1.0 - sg)))
        dcv, dg, db = _ln_bwd(dl, cv_ref[...], g_ref[...])
        first = pl.program_id(0) == 0
        _acc_rows(dg_ref, dg, first)
        _acc_rows(db_ref, db, first)
        _acc_rows(dc_ref, jnp.sum(dcv, axis=0, keepdims=True), first)
        dcv_ref[...] = dcv

    return _pc(body, "conv_mid_bwd", (T // tm,), [_rows(tm, C), _rows(tm, C), _const((1, C)), _const((1, C))],
               [_rows(tm, C), _const((1, C)), _const((1, C)), _const((1, C))],
               [_sds((T, C), F32)] + [_sds((1, C), F32)] * 3, sem=("arbitrary",))(ds, cv, ln_g, ln_b)


def dwconv_bwd(dcv, h, w_dw, taps):
    T, C = dcv.shape
    tq = _tile(T)
    nh = tq // HALO
    nblk = T // tq
    off = HALO - (taps - 1)

    def body(d_ref, dn_ref, a_ref, g_ref, ap_ref, gp_ref, w_ref, dh_ref, dw_ref, dbi_ref, su, sus, sd, sds, wb):
        i = pl.program_id(0)
        su[HALO:HALO + tq, :] = a_ref[...].astype(F32) * _sigmoid(g_ref[...].astype(F32))
        up = ap_ref[...].astype(F32) * _sigmoid(gp_ref[...].astype(F32))
        su[0:HALO, :] = jnp.where(i > 0, up, 0.0)
        sd[0:tq, :] = d_ref[...]
        sd[tq:tq + HALO, :] = jnp.where(i < nblk - 1, dn_ref[...], 0.0)
        _phases(su, sus)
        _phases(sd, sds)

        @pl.when(i == 0)
        def _():
            dw_ref[...] = jnp.zeros_like(dw_ref)

        for j in range(taps):
            dw_ref[j:j + 1, :] += jnp.sum(d_ref[...] * _tap(su, sus, off + j, tq), axis=0, keepdims=True)
        sa = jnp.zeros((1, C), F32)
        sb = jnp.zeros((1, C), F32)
        _spread(w_ref, wb, taps)
        for r in range(tq // CONV_ROWS):
            rows = slice(r * CONV_ROWS, (r + 1) * CONV_ROWS)
            dus = [wb[0] * _tap(sd, sds, taps - 1 + r * CONV_ROWS + 8 * k, 8) for k in range(CONV_ROWS // 8)]
            for j in range(1, taps):
                wj = wb[j]
                dus = [acc + wj * _tap(sd, sds, taps - 1 - j + r * CONV_ROWS + 8 * k, 8) for k, acc in enumerate(dus)]
            du = jnp.concatenate(dus, axis=0)
            a = a_ref[rows, :].astype(F32)
            sg = _sigmoid(g_ref[rows, :].astype(F32))
            da = du * sg
            dgt = du * a * sg * (1.0 - sg)
            dh_ref[rows, 0:C] = da.astype(BF16)
            dh_ref[rows, C:2 * C] = dgt.astype(BF16)
            sa = sa + jnp.sum(da, axis=0, keepdims=True)
            sb = sb + jnp.sum(dgt, axis=0, keepdims=True)
        first = i == 0
        _acc_rows(dbi_ref.at[:, 0:C], sa, first)
        _acc_rows(dbi_ref.at[:, C:2 * C], sb, first)

    prev = lambda col: pl.BlockSpec((HALO, C), lambda i: (jnp.maximum(i * nh - 1, 0), col))
    nxt = pl.BlockSpec((HALO, C), lambda i: (jnp.minimum((i + 1) * nh, T // HALO - 1), 0))
    cur = lambda col: pl.BlockSpec((tq, C), lambda i: (i, col))
    return _pc(body, "dwconv_bwd", (nblk,),
               [cur(0), nxt, cur(0), cur(1), prev(0), prev(1), _const((HALO, C))],
               [_rows(tq, 2 * C), _const((HALO, C)), _const((1, 2 * C))],
               [_sds((T, 2 * C), BF16), _sds((HALO, C), F32), _sds((1, 2 * C), F32)],
               scratch=[pltpu.VMEM((HALO + tq, C), F32), pltpu.VMEM((7, HALO + tq, C), F32),
                        pltpu.VMEM((HALO + tq, C), F32), pltpu.VMEM((7, HALO + tq, C), F32), pltpu.VMEM((taps, 8, C), F32)],
               sem=("arbitrary",))(dcv, dcv, h, h, h, h, w_dw)


def conv_in_bwd(dh, dpre_mix, w_in, alpha):
    T, D = dpre_mix.shape
    nw = w_in[0].shape[2]
    tm = _tile(T)

    def body(dh_ref, dp_ref, w_ref, dx_ref):
        acc = alpha * dp_ref[...]
        for j in range(NS):
            acc = acc + lax.dot_general(dh_ref[:, j * nw:(j + 1) * nw], w_ref[j], NT, preferred_element_type=F32)
        dx_ref[...] = acc

    return _pc(body, "conv_in_bwd", (T // tm,), [_rows(tm, NS * nw), _rows(tm, D), _wspec(w_in)], _rows(tm, D),
               _sds((T, D), F32), sem=("parallel",))(dh, dpre_mix, w_in[0])


def wgrad(a, b, row_sharded, name):
    T, Ka = a.shape
    Nb = b.shape[1]
    tt = min(512, T)
    nt = T // tt
    ka, tn = min(Ka, 1024), min(Nb, 1024)
    if row_sharded:
        sr = Ka // NS
        spb = max(ka // sr, 1)
        out_shape = (NS, sr, Nb)
        out_spec = pl.BlockSpec((spb, ka // spb, tn), lambda i, j, t: (i, 0, j))
    else:
        sc = Nb // NS
        spb = max(tn // sc, 1)
        out_shape = (NS, Ka, sc)
        out_spec = pl.BlockSpec((spb, ka, tn // spb), lambda i, j, t: (j, i, 0))

    def body(a_ref, b_ref, o_ref, acc):
        t = pl.program_id(2)
        av = a_ref[...]
        if av.dtype != BF16:
            av = av.astype(BF16)
        d = lax.dot_general(av, b_ref[...], TN, preferred_element_type=F32)

        @pl.when(t == 0)
        def _():
            acc[...] = d

        @pl.when(t > 0)
        def _():
            acc[...] += d

        @pl.when(t == nt - 1)
        def _():
            for s in range(spb):
                if row_sharded:
                    o_ref[s] = acc[s * (ka // spb):(s + 1) * (ka // spb), :].astype(BF16)
                else:
                    o_ref[s] = acc[:, s * (tn // spb):(s + 1) * (tn // spb)].astype(BF16)

    return _pc(body, name, (Ka // ka, Nb // tn, nt),
               [pl.BlockSpec((tt, ka), lambda i, j, t: (t, i)), pl.BlockSpec((tt, tn), lambda i, j, t: (t, j))],
               out_spec, _sds(out_shape, BF16),
               scratch=[pltpu.VMEM((ka, tn), F32)], sem=("parallel", "parallel", "arbitrary"))(a, b)


def adamw(w, g, m, v, name):
    R, W = w.shape
    tr = R
    for cand in (512, 256, 128, 64, 32, 16, 8):
        if R % cand == 0:
            tr = cand
            break
    c1 = 1.0 - ADAM_B1 ** ADAM_STEP
    c2 = 1.0 - ADAM_B2 ** ADAM_STEP

    def body(w_ref, g_ref, m_ref, v_ref, d_ref, mo_ref, vo_ref):
        gv = g_ref[...]
        mn = ADAM_B1 * m_ref[...] + (1.0 - ADAM_B1) * gv
        vn = ADAM_B2 * v_ref[...] + (1.0 - ADAM_B2) * (gv * gv)
        mo_ref[...] = mn
        vo_ref[...] = vn
        d_ref[...] = -ADAM_LR * ((mn / c1) / (jnp.sqrt(vn / c2) + ADAM_EPS) + ADAM_WD * w_ref[...])

    return _pc(body, name, (R // tr,), [_rows(tr, W)] * 4, [_rows(tr, W)] * 3, [_sds((R, W), F32)] * 3,
               sem=("parallel",))(w, g, m, v)


def _to_heads(t):
    T, n = t.shape
    return t.reshape(T, n // HEAD, HEAD).transpose(1, 0, 2)


def _from_heads(t):
    nh, T, _ = t.shape
    return t.transpose(1, 0, 2).reshape(T, nh * HEAD)


def _rope_tables(T):
    pos = jnp.arange(T, dtype=F32)
    inv_freq = ROPE_THETA ** (-jnp.arange(0, ROPE, 2, dtype=F32) / ROPE)
    ang = pos[:, None] * inv_freq[None, :]
    cos, sin = jnp.cos(ang), jnp.sin(ang)
    pad = HEAD - ROPE
    c = jnp.concatenate([cos, cos, jnp.ones((T, pad), F32)], axis=1)
    s = jnp.concatenate([-sin, sin, jnp.zeros((T, pad), F32)], axis=1)
    return jnp.stack([jnp.tile(c, (1, 128 // HEAD)), jnp.tile(s, (1, 128 // HEAD))])


def _local_step(x, p, target, W, small):
    T, D = x.shape
    depth = small["mix_ln_g"].shape[0]
    alpha = float((2 * depth) ** 0.25)
    taps = small["taps"]
    row = lambda a, i: a[i:i + 1]
    cs = _rope_tables(T)

    x0b = x.astype(BF16)
    h = conv_in_fwd(x0b, W["conv_w_in"], small["conv_b_in"])
    cv, s = dwconv_fwd(h, small["conv_w_dw"], small["conv_b_dw"], small["conv_ln_g"], small["conv_ln_b"], taps)
    pre_mix0, x1, x1b = mm_res_ln(s, W["conv_w_out"], x, row(small["mix_ln_g"], 0), row(small["mix_ln_b"], 0), alpha,
                                  small["conv_b_out"], "conv_out_fwd")
    r0 = mlp_up_fwd(x1b, W["mlp_w_up0"], "mlp_up_fwd0")
    pre_mlp0, x2, x2b = mm_res_ln(r0, W["mlp_w_down0"], x1, row(small["mlp_ln_g"], 0), row(small["mlp_ln_b"], 0), alpha,
                                  None, "mlp_down_fwd0")
    x3, x3b, pp0, gl0 = ple_fwd(x2, x2b, p, 0, W["ple_w_proj0"], W["ple_w_gate0"], None, "ple_fwd0")

    q, k, v = qkv_fwd(x3b, W["attn_w_q"], W["kv_w_k"], W["kv_w_v"], cs)
    qh, kh, vh = _to_heads(q), _to_heads(k), _to_heads(v)
    o = _from_heads(attn_fwd(qh, kh, vh, small["attn_sinks"]))
    pre_mix1, x4, x4b = mm_res_ln(o, W["attn_w_o"], x3, row(small["mix_ln_g"], 1), row(small["mix_ln_b"], 1), alpha,
                                  None, "attn_out_fwd")
    r1 = mlp_up_fwd(x4b, W["mlp_w_up1"], "mlp_up_fwd1")
    pre_mlp1, x5, x5b = mm_res_ln(r1, W["mlp_w_down1"], x4, row(small["mlp_ln_g"], 1), row(small["mlp_ln_b"], 1), alpha,
                                  None, "mlp_down_fwd1")
    dx6, loss, pp1, gl1 = ple_fwd(x5, x5b, p, 1, W["ple_w_proj1"], W["ple_w_gate1"], target, "ple_fwd1")

    G, sg = {}, {}
    dpp1, dgl1, dx5 = ple_bwd(dx6, pp1, gl1, W["ple_w_gate1"], "ple_bwd1")
    G["ple_w_proj1"] = wgrad(p[1], dpp1, False, "wg_ple_proj1")
    G["ple_w_gate1"] = wgrad(x5b, dgl1, True, "wg_ple_gate1")
    dpre_mlp1, dpre_mlp1b, dm1, g_mlp_g1, g_mlp_b1 = mlp_bwd1(dx5, pre_mlp1, row(small["mlp_ln_g"], 1), r1,
                                                              W["mlp_w_down1"], "mlp_bwd1_1")
    G["mlp_w_down1"] = wgrad(r1, dpre_mlp1b, True, "wg_mlp_down1")
    G["mlp_w_up1"] = wgrad(x4b, dm1, False, "wg_mlp_up1")
    dpre_mix1, dpre_mix1b, do, g_mix_g1, g_mix_b1, _ = mlp_bwd2(dpre_mlp1, dm1, W["mlp_w_up1"], alpha, pre_mix1,
                                                                row(small["mix_ln_g"], 1), W["attn_w_o"], "mlp_bwd2_1")
    G["attn_w_o"] = wgrad(o, dpre_mix1b, True, "wg_attn_o")
    dqh, dkh, dvh, dsinks = attn_bwd(qh, kh, vh, _to_heads(do), small["attn_sinks"])
    dqb, dkb, dvb, dx3 = qkv_bwd(_from_heads(dqh), _from_heads(dkh), _from_heads(dvh), dpre_mix1,
                                 W["attn_w_q"], W["kv_w_k"], W["kv_w_v"], cs, alpha)
    G["attn_w_q"] = wgrad(x3b, dqb, True, "wg_attn_q")
    G["kv_w_k"] = wgrad(x3b, dkb, True, "wg_kv_k")
    G["kv_w_v"] = wgrad(x3b, dvb, True, "wg_kv_v")

    dpp0, dgl0, dx2 = ple_bwd(dx3, pp0, gl0, W["ple_w_gate0"], "ple_bwd0")
    G["ple_w_proj0"] = wgrad(p[0], dpp0, False, "wg_ple_proj0")
    G["ple_w_gate0"] = wgrad(x2b, dgl0, True, "wg_ple_gate0")
    dpre_mlp0, dpre_mlp0b, dm0, g_mlp_g0, g_mlp_b0 = mlp_bwd1(dx2, pre_mlp0, row(small["mlp_ln_g"], 0), r0,
                                                              W["mlp_w_down0"], "mlp_bwd1_0")
    G["mlp_w_down0"] = wgrad(r0, dpre_mlp0b, True, "wg_mlp_down0")
    G["mlp_w_up0"] = wgrad(x1b, dm0, False, "wg_mlp_up0")
    dpre_mix0, dpre_mix0b, dsw, g_mix_g0, g_mix_b0, g_b_out = mlp_bwd2(dpre_mlp0, dm0, W["mlp_w_up0"], alpha, pre_mix0,
                                                                      row(small["mix_ln_g"], 0), W["conv_w_out"],
                                                                      "mlp_bwd2_0")
    G["conv_w_out"] = wgrad(s, dpre_mix0b, True, "wg_conv_out")
    dcv, g_cln_g, g_cln_b, g_b_dw = conv_mid_bwd(dsw, cv, small["conv_ln_g"], small["conv_ln_b"])
    dh, g_w_dw, g_b_in = dwconv_bwd(dcv, h, small["conv_w_dw"], taps)
    G["conv_w_in"] = wgrad(x0b, dh, False, "wg_conv_in")
    grad_x = conv_in_bwd(dh, dpre_mix0, W["conv_w_in"], alpha)

    sg["conv_b_in"] = g_b_in
    sg["conv_w_dw"] = g_w_dw
    sg["conv_b_dw"], sg["conv_ln_g"], sg["conv_ln_b"], sg["conv_b_out"] = g_b_dw, g_cln_g, g_cln_b, g_b_out
    sg["mix_ln_g"] = jnp.concatenate([g_mix_g0, g_mix_g1], axis=0)
    sg["mix_ln_b"] = jnp.concatenate([g_mix_b0, g_mix_b1], axis=0)
    sg["mlp_ln_g"] = jnp.concatenate([g_mlp_g0, g_mlp_g1], axis=0)
    sg["mlp_ln_b"] = jnp.concatenate([g_mlp_b0, g_mlp_b1], axis=0)
    sg["attn_sinks"] = dsinks[:, 0][None, :]
    return loss, grad_x, G, sg


BUF_A = ("mlp_w_up0", "mlp_w_up1", "mlp_w_down0", "mlp_w_down1", "ple_w_gate0", "ple_w_gate1", "conv_w_out", "attn_w_q",
         "attn_w_o")
BUF_B = ("conv_w_in",)
BUF_C = ("kv_w_k", "kv_w_v", "ple_w_proj0", "ple_w_proj1")
ROW_SHARDED = {"mlp_w_down0", "mlp_w_down1", "ple_w_gate0", "ple_w_gate1", "conv_w_out", "attn_w_q", "attn_w_o", "kv_w_k",
               "kv_w_v"}


def _split_layers(weights):
    out = {"conv_w_in": weights["conv_w_in"][0], "conv_w_out": weights["conv_w_out"][0],
           "attn_w_q": weights["attn_w_q"][0], "attn_w_o": weights["attn_w_o"][0],
           "kv_w_k": weights["kv_w_k"], "kv_w_v": weights["kv_w_v"]}
    for n in ("mlp_w_up", "mlp_w_down", "ple_w_proj", "ple_w_gate"):
        for i in range(weights[n].shape[0]):
            out[n + str(i)] = weights[n][i]
    return out


def _layout(shards):
    lay = {}
    for key, names in (("a", BUF_A), ("b", BUF_B), ("c", BUF_C)):
        off, rows = 0, []
        for n in names:
            rows.append((n, off, shards[n].shape[0]))
            off += shards[n].shape[0]
        lay[key] = rows
    return lay


def _place():
    return lax.axis_index("x"), lax.axis_index("y"), lax.axis_index("c")


def _flip(v, f):
    return (v + f) % 2 if f else v


CHIP_FLIPS = ((1, 0), (0, 1), (1, 1))


def gather_weights(bufs, small):
    nb = len(bufs)

    def body(*refs):
        outs = refs[nb + 1:2 * nb + 2]
        send, recv, fsend, frecv = refs[2 * nb + 2:]
        x, y, c = _place()
        me = 2 * x + y

        def half(ref, k):
            hrows = bufs[k].shape[1] // 2
            return ref.at[pl.ds(pl.multiple_of(c * hrows, 16), hrows), :]

        sends, fwd = [], []
        for d, (fx, fy) in enumerate(CHIP_FLIPS):
            to = (_flip(x, fx), _flip(y, fy), c)
            frm = 2 * _flip(x, fx) + _flip(y, fy)
            for k in range(nb):
                mine = half(outs[k].at[me], k)
                sends.append(pltpu.make_async_remote_copy(mine, mine, send.at[d * (nb + 1) + k],
                                                          recv.at[d * (nb + 1) + k], device_id=to, device_id_type=MESH))
                theirs = half(outs[k].at[frm], k)
                fwd.append(pltpu.make_async_remote_copy(theirs, theirs, fsend.at[d * nb + k], frecv.at[d * nb + k],
                                                        device_id=(x, y, 1 - c), device_id_type=MESH))
            sends.append(pltpu.make_async_remote_copy(outs[nb].at[me], outs[nb].at[me], send.at[d * (nb + 1) + nb],
                                                      recv.at[d * (nb + 1) + nb], device_id=to, device_id_type=MESH))
        for cp in sends:
            cp.start()
        for d in range(len(CHIP_FLIPS)):
            for k in range(nb):
                sends[d * (nb + 1) + k].wait_recv()
                fwd[d * nb + k].start()
            sends[d * (nb + 1) + nb].wait_recv()
        for cp in fwd:
            cp.wait_recv()
        for cp in sends + fwd:
            cp.wait_send()

    arrs = list(bufs) + [small]
    nd = len(CHIP_FLIPS)
    return pl.pallas_call(
        body, name="gather_weights", in_specs=[ANY] * (nb + 1), out_specs=[ANY] * (nb + 1),
        out_shape=[_sds(a.shape, a.dtype) for a in arrs], input_output_aliases={k: k for k in range(nb + 1)},
        scratch_shapes=[pltpu.SemaphoreType.DMA((nd * (nb + 1),)), pltpu.SemaphoreType.DMA((nd * (nb + 1),)),
                        pltpu.SemaphoreType.DMA((nd * nb,)), pltpu.SemaphoreType.DMA((nd * nb,))])(*arrs)


def sibling_exchange(grads, small):
    nb = len(grads)

    def body(*refs):
        ins, outs = refs[:nb + 1], refs[nb + 1:2 * nb + 2]
        send, recv, lsem = refs[2 * nb + 2:]
        x, y, c = _place()
        me = 4 * x + 2 * y + c
        cps = []
        for k in range(nb):
            hrows = grads[k].shape[1] // 2
            src = ins[k].at[:, pl.ds(pl.multiple_of((1 - c) * hrows, 16), hrows), :]
            cps.append(pltpu.make_async_remote_copy(src, outs[k], send.at[k], recv.at[k], device_id=(x, y, 1 - c),
                                                    device_id_type=MESH))
        n = nb
        for fx in (0, 1):
            for fy in (0, 1):
                for fc in (0, 1):
                    if fx or fy or fc:
                        cps.append(pltpu.make_async_remote_copy(
                            ins[nb], outs[nb].at[me], send.at[n], recv.at[n],
                            device_id=(_flip(x, fx), _flip(y, fy), _flip(c, fc)), device_id_type=MESH))
                        n += 1
        own = pltpu.make_async_copy(ins[nb], outs[nb].at[me], lsem)
        own.start()
        for cp in cps:
            cp.start()
        for cp in cps:
            cp.wait()
        own.wait()

    shapes = [_sds((NS, g.shape[1] // 2, g.shape[2]), g.dtype) for g in grads] + [_sds((8,) + small.shape, small.dtype)]
    return pl.pallas_call(
        body, name="sibling_exchange", in_specs=[ANY] * (nb + 1), out_specs=[ANY] * (nb + 1), out_shape=shapes,
        scratch_shapes=[pltpu.SemaphoreType.DMA((nb + 7,)), pltpu.SemaphoreType.DMA((nb + 7,)), pltpu.SemaphoreType.DMA(())])(
            *grads, small)


def chip_exchange(sums):
    nb = len(sums)

    def body(*refs):
        ins, outs = refs[:nb], refs[nb:2 * nb]
        send, recv = refs[2 * nb:]
        x, y, c = _place()
        cps = []
        for d, (fx, fy) in enumerate(CHIP_FLIPS):
            tx, ty = _flip(x, fx), _flip(y, fy)
            for k in range(nb):
                cps.append(pltpu.make_async_remote_copy(ins[k].at[2 * tx + ty], outs[k].at[d], send.at[d * nb + k],
                                                        recv.at[d * nb + k], device_id=(tx, ty, c), device_id_type=MESH))
        for cp in cps:
            cp.start()
        for cp in cps:
            cp.wait()

    nd = len(CHIP_FLIPS)
    return pl.pallas_call(
        body, name="chip_exchange", in_specs=[ANY] * nb, out_specs=[ANY] * nb,
        out_shape=[_sds((nd,) + s.shape[1:], s.dtype) for s in sums],
        scratch_shapes=[pltpu.SemaphoreType.DMA((nd * nb,)), pltpu.SemaphoreType.DMA((nd * nb,))])(*sums)


def sibling_share(halves):
    nb = len(halves)

    def body(*refs):
        outs = refs[nb:2 * nb]
        send, recv = refs[2 * nb:]
        x, y, c = _place()
        cps = []
        for k in range(nb):
            hrows = halves[k].shape[0] // 2
            mine = outs[k].at[pl.ds(pl.multiple_of(c * hrows, 8), hrows), :]
            cps.append(pltpu.make_async_remote_copy(mine, mine, send.at[k], recv.at[k], device_id=(x, y, 1 - c),
                                                    device_id_type=MESH))
        for cp in cps:
            cp.start()
        for cp in cps:
            cp.wait()

    return pl.pallas_call(
        body, name="sibling_share", in_specs=[ANY] * nb, out_specs=[ANY] * nb,
        out_shape=[_sds(h.shape, h.dtype) for h in halves], input_output_aliases={k: k for k in range(nb)},
        scratch_shapes=[pltpu.SemaphoreType.DMA((nb,)), pltpu.SemaphoreType.DMA((nb,))])(*halves)


def _row_tile(rows):
    for cand in (512, 384, 256, 128, 64, 32, 16):
        if rows % cand == 0:
            return cand
    return rows


def pair_sum(g, r, idx, name):
    _, hrows, W = r.shape
    tr = _row_tile(hrows)
    nrb = hrows // tr

    def body(idx_ref, g_ref, r_ref, o_ref):
        o_ref[...] = (g_ref[...].astype(F32) + r_ref[...].astype(F32)).astype(BF16)

    gs = pltpu.PrefetchScalarGridSpec(
        num_scalar_prefetch=1, grid=(NS, nrb),
        in_specs=[pl.BlockSpec((None, tr, W), lambda j, i, s: (j, s[1] * nrb + i, 0)),
                  pl.BlockSpec((None, tr, W), lambda j, i, s: (j, i, 0))],
        out_specs=pl.BlockSpec((None, tr, W), lambda j, i, s: (j, i, 0)))
    return pl.pallas_call(body, name=name, grid_spec=gs, out_shape=_sds(r.shape, BF16),
                          compiler_params=pltpu.CompilerParams(dimension_semantics=("parallel", "parallel")))(idx, g, r)


def chip_sum(s, t, idx, name):
    _, hrows, W = s.shape
    tr = _row_tile(hrows)
    nrb = hrows // tr

    def body(idx_ref, s_ref, t_ref, o_ref):
        acc = s_ref[...].astype(F32)
        for d in range(t.shape[0]):
            acc = acc + t_ref[d].astype(F32)
        o_ref[...] = acc

    gs = pltpu.PrefetchScalarGridSpec(
        num_scalar_prefetch=1, grid=(nrb,),
        in_specs=[pl.BlockSpec((None, tr, W), lambda i, sc: (sc[0], i, 0)),
                  pl.BlockSpec((t.shape[0], tr, W), lambda i, sc: (0, i, 0))],
        out_specs=pl.BlockSpec((tr, W), lambda i, sc: (sc[1] * nrb + i, 0)))
    return pl.pallas_call(body, name=name, grid_spec=gs, out_shape=_sds((2 * hrows, W), F32),
                          compiler_params=pltpu.CompilerParams(dimension_semantics=("parallel",)))(idx, s, t)


def small_sum(packs):
    n, R, W = packs.shape

    def body(p_ref, o_ref):
        acc = p_ref[0]
        for d in range(1, n):
            acc = acc + p_ref[d]
        o_ref[...] = acc

    return pl.pallas_call(body, name="small_sum", out_shape=_sds((R, W), F32))(packs)


WEIGHTS = ["conv_w_in", "conv_b_in", "conv_w_dw", "conv_b_dw", "conv_ln_g", "conv_ln_b", "conv_w_out", "conv_b_out", "kv_w_k",
           "kv_w_v", "attn_w_q", "attn_sinks", "attn_w_o", "mix_ln_g", "mix_ln_b", "mlp_w_up", "mlp_w_down", "mlp_ln_g",
           "mlp_ln_b", "ple_w_proj", "ple_w_gate"]
BIG = ["conv_w_in", "conv_w_out", "kv_w_k", "kv_w_v", "attn_w_q", "attn_w_o", "mlp_w_up", "mlp_w_down", "ple_w_proj",
       "ple_w_gate"]
SMALL = [n for n in WEIGHTS if n not in BIG]
SHARDED_SMALL = ["conv_b_in", "conv_w_dw", "conv_b_dw", "conv_ln_g", "conv_ln_b", "conv_b_out"]


FLAT = 1024


def _flat_tiles(a):
    f = a.reshape(-1)
    pad = (-f.shape[0]) % FLAT
    return jnp.pad(f, (0, pad)) if pad else f


def _step(x, p, target, w, m, v):
    D = x.shape[-1]
    ds = D // NS
    xq, yq, cq = _place()
    chip = 2 * xq + yq
    idx = jnp.stack([chip, cq]).astype(jnp.int32)

    shards = _split_layers(w)
    lay = _layout(shards)
    packed = [jnp.concatenate([shards[n].astype(BF16) for n, _, _ in lay[key]], axis=0) for key in ("a", "b", "c")]
    taps = w["conv_w_dw"].shape[1]
    small_loc = jnp.concatenate(
        [w["conv_w_dw"][0], jnp.zeros((HALO - taps, ds), F32), w["conv_b_dw"], w["conv_ln_g"], w["conv_ln_b"], w["conv_b_out"],
         w["conv_b_in"].reshape(2, ds), jnp.zeros((2, ds), F32)], axis=0)
    slot = lambda a: lax.dynamic_update_slice(jnp.zeros((NS,) + a.shape, a.dtype), a[None], (chip, 0, 0))
    ga, gb, gc, gs = gather_weights([slot(a) for a in packed], slot(small_loc))
    W = {}
    for key, buf in (("a", ga), ("b", gb), ("c", gc)):
        for n, off, rows in lay[key]:
            W[n] = (buf, off, rows)
    across = lambda rows: gs[:, rows, :].transpose(1, 0, 2).reshape(rows.stop - rows.start, D)
    small = {"taps": taps, "conv_w_dw": across(slice(0, HALO)), "conv_b_dw": across(slice(HALO, HALO + 1)),
             "conv_ln_g": across(slice(HALO + 1, HALO + 2)), "conv_ln_b": across(slice(HALO + 2, HALO + 3)),
             "conv_b_out": across(slice(HALO + 3, HALO + 4)), "conv_b_in": gs[:, HALO + 4:HALO + 6, :].reshape(1, 2 * D),
             "attn_sinks": w["attn_sinks"], "mix_ln_g": w["mix_ln_g"], "mix_ln_b": w["mix_ln_b"],
             "mlp_ln_g": w["mlp_ln_g"], "mlp_ln_b": w["mlp_ln_b"]}

    loss, grad_x, G, sg = _local_step(x[0], p[:, 0], target[0], W, small)

    parts = [jnp.concatenate([G[n] for n, _, _ in lay[key]], axis=1) for key in ("a", "b", "c")]
    nsink = sg["attn_sinks"].shape[1]
    pack = jnp.concatenate(
        [sg["conv_b_in"].reshape(2, D), sg["conv_w_dw"], sg["conv_b_dw"], sg["conv_ln_g"], sg["conv_ln_b"], sg["conv_b_out"],
         sg["mix_ln_g"], sg["mix_ln_b"], sg["mlp_ln_g"], sg["mlp_ln_b"],
         jnp.concatenate([sg["attn_sinks"], jnp.zeros((1, D - nsink), F32)], axis=1),
         jnp.concatenate([loss[0:1], jnp.zeros((1, D - loss.shape[1]), F32)], axis=1)], axis=0)
    *from_sibling, packs = sibling_exchange(parts, pack)
    sums = [pair_sum(g, r, idx, "pair_sum_" + key) for g, r, key in zip(parts, from_sibling, "abc")]
    from_chips = chip_exchange(sums)
    halves = [chip_sum(s, t, idx, "chip_sum_" + key) for s, t, key in zip(sums, from_chips, "abc")]
    full = sibling_share(halves)
    tot = small_sum(packs)

    grads = {}
    for key, buf in zip(("a", "b", "c"), full):
        for n, off, rows in lay[key]:
            grads[n] = buf[off:off + rows]
    for n in ("mlp_w_up", "mlp_w_down", "ple_w_proj", "ple_w_gate"):
        grads[n] = jnp.stack([grads.pop(n + str(i)) for i in range(w[n].shape[0])])
    for n in ("conv_w_in", "conv_w_out", "attn_w_q", "attn_w_o"):
        grads[n] = grads[n][None]
    cols = lambda rows: lax.dynamic_slice(rows, (0, chip * ds), (rows.shape[0], ds))
    grads["conv_b_in"] = lax.dynamic_slice(tot[0:2].reshape(1, 2 * D), (0, chip * 2 * ds), (1, 2 * ds))
    grads["conv_w_dw"] = cols(tot[2:2 + taps])[None]
    r0 = 2 + HALO
    for i, n in enumerate(("conv_b_dw", "conv_ln_g", "conv_ln_b", "conv_b_out")):
        grads[n] = cols(tot[r0 + i:r0 + i + 1])
    r0 += 4
    for i, n in enumerate(("mix_ln_g", "mix_ln_b", "mlp_ln_g", "mlp_ln_b")):
        grads[n] = tot[r0 + 2 * i:r0 + 2 * i + 2]
    grads["attn_sinks"] = tot[r0 + 8:r0 + 9, 0:nsink]

    delta, new_m, new_v = {}, {}, {}
    for n in BIG:
        shp = w[n].shape
        two = lambda a: a.reshape(-1, shp[-1])
        d_, m_, v_ = adamw(two(w[n]), two(grads[n]), two(m[n]), two(v[n]), "adamw_" + n)
        delta[n], new_m[n], new_v[n] = d_.reshape(shp), m_.reshape(shp), v_.reshape(shp)
    flat = lambda t: jnp.concatenate([_flat_tiles(t[n]).reshape(-1, 128) for n in SMALL], axis=0)
    d_, m_, v_ = adamw(flat(w), flat(grads), flat(m), flat(v), "adamw_small")
    pos = 0
    for n in SMALL:
        size = w[n].size
        take = lambda a: a.reshape(-1)[pos:pos + size].reshape(w[n].shape)
        delta[n], new_m[n], new_v[n] = take(d_), take(m_), take(v_)
        pos += size + (-size) % FLAT

    total = tot[r0 + 9, 0]
    return (total, grad_x[None], *[grads[n] for n in WEIGHTS], *[delta[n] for n in WEIGHTS], *[new_m[n] for n in WEIGHTS],
            *[new_v[n] for n in WEIGHTS])


def kernel(x, p, conv_w_in, conv_b_in, conv_w_dw, conv_b_dw, conv_ln_g, conv_ln_b, conv_w_out, conv_b_out, kv_w_k, kv_w_v, attn_w_q, attn_sinks, attn_w_o, mix_ln_g, mix_ln_b, mlp_w_up, mlp_w_down, mlp_ln_g, mlp_ln_b, ple_w_proj, ple_w_gate, loss_target, m_conv_w_in, m_conv_b_in, m_conv_w_dw, m_conv_b_dw, m_conv_ln_g, m_conv_ln_b, m_conv_w_out, m_conv_b_out, m_kv_w_k, m_kv_w_v, m_attn_w_q, m_attn_sinks, m_attn_w_o, m_mix_ln_g, m_mix_ln_b, m_mlp_w_up, m_mlp_w_down, m_mlp_ln_g, m_mlp_ln_b, m_ple_w_proj, m_ple_w_gate, v_conv_w_in, v_conv_b_in, v_conv_w_dw, v_conv_b_dw, v_conv_ln_g, v_conv_ln_b, v_conv_w_out, v_conv_b_out, v_kv_w_k, v_kv_w_v, v_attn_w_q, v_attn_sinks, v_attn_w_o, v_mix_ln_g, v_mix_ln_b, v_mlp_w_up, v_mlp_w_down, v_mlp_ln_g, v_mlp_ln_b, v_ple_w_proj, v_ple_w_gate):
    w = dict(zip(WEIGHTS, (conv_w_in, conv_b_in, conv_w_dw, conv_b_dw, conv_ln_g, conv_ln_b, conv_w_out, conv_b_out, kv_w_k,
                           kv_w_v, attn_w_q, attn_sinks, attn_w_o, mix_ln_g, mix_ln_b, mlp_w_up, mlp_w_down, mlp_ln_g, mlp_ln_b,
                           ple_w_proj, ple_w_gate)))
    m = dict(zip(WEIGHTS, (m_conv_w_in, m_conv_b_in, m_conv_w_dw, m_conv_b_dw, m_conv_ln_g, m_conv_ln_b, m_conv_w_out,
                           m_conv_b_out, m_kv_w_k, m_kv_w_v, m_attn_w_q, m_attn_sinks, m_attn_w_o, m_mix_ln_g, m_mix_ln_b,
                           m_mlp_w_up, m_mlp_w_down, m_mlp_ln_g, m_mlp_ln_b, m_ple_w_proj, m_ple_w_gate)))
    v = dict(zip(WEIGHTS, (v_conv_w_in, v_conv_b_in, v_conv_w_dw, v_conv_b_dw, v_conv_ln_g, v_conv_ln_b, v_conv_w_out,
                           v_conv_b_out, v_kv_w_k, v_kv_w_v, v_attn_w_q, v_attn_sinks, v_attn_w_o, v_mix_ln_g, v_mix_ln_b,
                           v_mlp_w_up, v_mlp_w_down, v_mlp_ln_g, v_mlp_ln_b, v_ple_w_proj, v_ple_w_gate)))
    return _step(x, p, loss_target, w, m, v)
```

```python
import functools

import jax
import jax.numpy as jnp
from jax import lax
from jax.experimental import pallas as pl
from jax.experimental.pallas import tpu as pltpu

F32 = jnp.float32
BF16 = jnp.bfloat16
NS = 4
HEAD = 64
BLK = 128
ROPE = 16
ROPE_THETA = 500000.0
LN_EPS = 1e-5
NEG = -1e30
HALO = 32
ADAM_LR, ADAM_B1, ADAM_B2, ADAM_EPS, ADAM_WD, ADAM_STEP = 0.001, 0.9, 0.999, 1e-08, 0.01, 10
MESH = pl.DeviceIdType.MESH
ANY = pl.BlockSpec(memory_space=pl.ANY)
NT = (((1,), (1,)), ((), ()))
TN = (((0,), (0,)), ((), ()))


_FOLLOW = []


def _pc(body, name, grid, in_specs, out_specs, out_shape, scratch=(), sem=None, vmem=56, **kw):
    call = lambda fn, ins: pl.pallas_call(
        fn, name=name, grid=grid, in_specs=ins, out_specs=out_specs, out_shape=out_shape,
        scratch_shapes=list(scratch),
        compiler_params=pltpu.CompilerParams(dimension_semantics=sem, vmem_limit_bytes=vmem * 2 ** 20), **kw)
    if not _FOLLOW:
        return call(body, in_specs)
    extra = list(_FOLLOW)
    _FOLLOW.clear()
    n_in = len(in_specs)

    def ordered(*refs):
        return body(*refs[:n_in], *refs[n_in + len(extra):])

    run = call(ordered, list(in_specs) + [ANY] * len(extra))
    return lambda *args: run(*args, *extra)


def _rows(tm, n):
    return pl.BlockSpec((tm, n), lambda i: (i, 0))


def _const(shape):
    return pl.BlockSpec(shape, lambda *_: (0,) * len(shape))


def _wspec(w):
    buf, off, rows = w
    assert off % rows == 0
    return pl.BlockSpec((NS, rows, buf.shape[2]), lambda *_: (0, off // rows, 0))


def _sds(shape, dtype):
    return jax.ShapeDtypeStruct(shape, dtype)


def _tile(t):
    return min(256, t)


def _sigmoid(x):
    return 1.0 / (1.0 + jnp.exp(-x))


def _ln_stats(w):
    mu = jnp.mean(w, axis=-1, keepdims=True)
    xc = w - mu
    var = jnp.mean(xc * xc, axis=-1, keepdims=True)
    rstd = lax.rsqrt(var + LN_EPS)
    return xc * rstd, rstd


def _ln_bwd(dy, w, g):
    xhat, rstd = _ln_stats(w)
    dxhat = dy * g
    m1 = jnp.mean(dxhat, axis=-1, keepdims=True)
    m2 = jnp.mean(dxhat * xhat, axis=-1, keepdims=True)
    dw = rstd * (dxhat - m1 - xhat * m2)
    return dw, jnp.sum(dy * xhat, axis=0, keepdims=True), jnp.sum(dy, axis=0, keepdims=True)


def _acc_rows(ref, val, first):
    @pl.when(first)
    def _():
        ref[...] = val

    @pl.when(jnp.logical_not(first))
    def _():
        ref[...] += val


def conv_in_fwd(xb, w_in, b_in):
    T, D = xb.shape
    nw = w_in[0].shape[2]
    tm = _tile(T)

    def body(x_ref, w_ref, b_ref, h_ref):
        x = x_ref[...]
        for j in range(NS):
            sl = slice(j * nw, (j + 1) * nw)
            h_ref[:, sl] = (jnp.dot(x, w_ref[j], preferred_element_type=F32) + b_ref[:, sl]).astype(BF16)

    return _pc(body, "conv_in_fwd", (T // tm,), [_rows(tm, D), _wspec(w_in), _const((1, NS * nw))],
               _rows(tm, NS * nw), _sds((T, NS * nw), BF16), sem=("parallel",))(xb, w_in[0], b_in)


CONV_ROWS = 16


def _phases(scr, sh):
    n = scr.shape[0] - 8
    for b in range(1, 8):
        sh[b - 1, 0:n, :] = scr[b:b + n, :]


def _spread(w_ref, wb, taps):
    for j in range(taps):
        wb[j] = jnp.broadcast_to(w_ref[j:j + 1, :], wb.shape[1:])


def _tap(scr, sh, o, n):
    b = o % 8
    return scr[o:o + n, :] if b == 0 else sh[b - 1, o - b:o - b + n, :]


def dwconv_fwd(h, w_dw, b_dw, ln_g, ln_b, taps):
    T = h.shape[0]
    C = h.shape[1] // 2
    tq = _tile(T)
    nh = tq // HALO
    off = HALO - (taps - 1)

    def body(a_ref, g_ref, ap_ref, gp_ref, w_ref, bdw_ref, lg_ref, lb_ref, cv_ref, s_ref, scr, sh, wb):
        i = pl.program_id(0)
        scr[HALO:HALO + tq, :] = a_ref[...].astype(F32) * _sigmoid(g_ref[...].astype(F32))
        up = ap_ref[...].astype(F32) * _sigmoid(gp_ref[...].astype(F32))
        scr[0:HALO, :] = jnp.where(i > 0, up, 0.0)
        _phases(scr, sh)
        _spread(w_ref, wb, taps)
        bias = jnp.broadcast_to(bdw_ref[...], (8, C))
        for r in range(tq // CONV_ROWS):
            accs = [bias] * (CONV_ROWS // 8)
            for j in range(taps):
                wj = wb[j]
                accs = [acc + wj * _tap(scr, sh, off + j + r * CONV_ROWS + 8 * k, 8) for k, acc in enumerate(accs)]
            for k, acc in enumerate(accs):
                cv_ref[r * CONV_ROWS + 8 * k:r * CONV_ROWS + 8 * k + 8, :] = acc
        xhat, _ = _ln_stats(cv_ref[...])
        ln = xhat * lg_ref[...] + lb_ref[...]
        s_ref[...] = (ln * _sigmoid(ln)).astype(BF16)

    prev = lambda col: pl.BlockSpec((HALO, C), lambda i: (jnp.maximum(i * nh - 1, 0), col))
    cur = lambda col: pl.BlockSpec((tq, C), lambda i: (i, col))
    return _pc(body, "dwconv_fwd", (T // tq,),
               [cur(0), cur(1), prev(0), prev(1), _const((HALO, C)), _const((1, C)), _const((1, C)), _const((1, C))],
               [_rows(tq, C), _rows(tq, C)], [_sds((T, C), F32), _sds((T, C), BF16)],
               scratch=[pltpu.VMEM((HALO + tq, C), F32), pltpu.VMEM((7, HALO + tq, C), F32), pltpu.VMEM((taps, 8, C), F32)],
               sem=("parallel",))(h, h, h, h, w_dw, b_dw, ln_g, ln_b)


def mm_res_ln(a, w, res, g, b, alpha, bias, name):
    T, K = a.shape
    ks = K // NS
    D = res.shape[1]
    tm = _tile(T)

    def body(*refs):
        a_ref, w_ref, res_ref, g_ref, b_ref = refs[:5]
        n = 5
        if bias is not None:
            bias_ref = refs[5]
            n = 6
        pre_ref, xo_ref, xb_ref = refs[n:n + 3]
        acc = jnp.dot(a_ref[:, 0:ks], w_ref[0], preferred_element_type=F32)
        for j in range(1, NS):
            acc = acc + jnp.dot(a_ref[:, j * ks:(j + 1) * ks], w_ref[j], preferred_element_type=F32)
        if bias is not None:
            acc = acc + bias_ref[...]
        pre = alpha * res_ref[...] + acc
        xhat, _ = _ln_stats(pre)
        xo = xhat * g_ref[...] + b_ref[...]
        pre_ref[...] = pre
        xo_ref[...] = xo
        xb_ref[...] = xo.astype(BF16)

    ins = [_rows(tm, K), _wspec(w), _rows(tm, D), _const((1, D)), _const((1, D))]
    args = [a, w[0], res, g, b]
    if bias is not None:
        ins.append(_const((1, D)))
        args.append(bias)
    return _pc(body, name, (T // tm,), ins, [_rows(tm, D)] * 3, [_sds((T, D), F32), _sds((T, D), F32), _sds((T, D), BF16)],
               sem=("parallel",))(*args)


def mlp_up_fwd(xb, w_up, name):
    T, D = xb.shape
    fs = w_up[0].shape[2]
    tm = _tile(T)

    def body(x_ref, w_ref, r_ref):
        x = x_ref[...]
        for j in range(NS):
            m = jnp.maximum(jnp.dot(x, w_ref[j], preferred_element_type=F32), 0.0)
            r_ref[:, j * fs:(j + 1) * fs] = (m * m).astype(BF16)

    return _pc(body, name, (T // tm,), [_rows(tm, D), _wspec(w_up)], _rows(tm, NS * fs), _sds((T, NS * fs), BF16),
               sem=("parallel",))(xb, w_up[0])


def ple_fwd(x, xb, p, layer, w_proj, w_gate, target, name):
    T, D = x.shape
    P = p.shape[2]
    ds = D // NS
    tm = _tile(T)
    last = target is not None

    def body(*refs):
        x_ref, xb_ref, p_ref, wp_ref, wg_ref = refs[:5]
        n = 5
        if last:
            t_ref = refs[5]
            n = 6
        o_ref, o2_ref, pp_ref, gl_ref = refs[n:n + 4]
        gl = jnp.dot(xb_ref[:, 0:ds], wg_ref[0], preferred_element_type=F32)
        for j in range(1, NS):
            gl = gl + jnp.dot(xb_ref[:, j * ds:(j + 1) * ds], wg_ref[j], preferred_element_type=F32)
        gl_ref[...] = gl.astype(BF16)
        sg = _sigmoid(gl)
        pb = p_ref[...].astype(BF16)
        sq = jnp.zeros((1, 1), F32)
        for j in range(NS):
            sl = slice(j * ds, (j + 1) * ds)
            pp = jnp.dot(pb, wp_ref[j], preferred_element_type=F32)
            pp_ref[:, sl] = pp.astype(BF16)
            out = x_ref[:, sl] + pp * sg[:, sl]
            if last:
                err = out - t_ref[:, sl]
                o_ref[:, sl] = err * (1.0 / D)
                e2 = jnp.sum(err * err, axis=0, keepdims=True)
                sq = sq + jnp.sum(e2, axis=1, keepdims=True)
            else:
                o_ref[:, sl] = out
                o2_ref[:, sl] = out.astype(BF16)
        if last:
            _acc_rows(o2_ref, jnp.broadcast_to(sq * (0.5 / D), (8, 128)), pl.program_id(0) == 0)

    ins = [_rows(tm, D), _rows(tm, D), pl.BlockSpec((None, tm, P), lambda i: (layer, i, 0)), _wspec(w_proj), _wspec(w_gate)]
    args = [x, xb, p, w_proj[0], w_gate[0]]
    if last:
        ins.append(_rows(tm, D))
        args.append(target)
        outs = [_rows(tm, D), _const((8, 128)), _rows(tm, D), _rows(tm, D)]
        shapes = [_sds((T, D), F32), _sds((8, 128), F32), _sds((T, D), BF16), _sds((T, D), BF16)]
    else:
        outs = [_rows(tm, D)] * 4
        shapes = [_sds((T, D), F32), _sds((T, D), BF16), _sds((T, D), BF16), _sds((T, D), BF16)]
    return _pc(body, name, (T // tm,), ins, outs, shapes, sem=("arbitrary",) if last else ("parallel",))(*args)


def _rope(x, cs_ref, sign):
    c = cs_ref[0]
    s = cs_ref[1] * sign
    lane = lax.broadcasted_iota(jnp.int32, c.shape, 1)
    first = (lane % HEAD) < (ROPE // 2)
    outs = []
    for gq in range(x.shape[1] // 128):
        xg = x[:, gq * 128:(gq + 1) * 128]
        sw = jnp.where(first, pltpu.roll(xg, 128 - ROPE // 2, 1), pltpu.roll(xg, ROPE // 2, 1))
        outs.append(xg * c + sw * s)
    return outs


def qkv_fwd(xb, w_q, w_k, w_v, cs):
    T, D = xb.shape
    ds = D // NS
    HD, KVD = w_q[0].shape[2], w_k[0].shape[2]
    tm = _tile(T)
    scale = 1.0 / (HEAD ** 0.5)

    def body(x_ref, wq_ref, wk_ref, wv_ref, cs_ref, q_ref, k_ref, v_ref):
        def proj(w_ref):
            acc = jnp.dot(x_ref[:, 0:ds], w_ref[0], preferred_element_type=F32)
            for j in range(1, NS):
                acc = acc + jnp.dot(x_ref[:, j * ds:(j + 1) * ds], w_ref[j], preferred_element_type=F32)
            return acc

        for gq, val in enumerate(_rope(proj(wq_ref), cs_ref, 1.0)):
            q_ref[:, gq * 128:(gq + 1) * 128] = (val * scale).astype(BF16)
        for gq, val in enumerate(_rope(proj(wk_ref), cs_ref, 1.0)):
            k_ref[:, gq * 128:(gq + 1) * 128] = val.astype(BF16)
        v_ref[...] = proj(wv_ref).astype(BF16)

    cs_spec = pl.BlockSpec((2, tm, 128), lambda i: (0, i, 0))
    return _pc(body, "qkv_fwd", (T // tm,), [_rows(tm, D), _wspec(w_q), _wspec(w_k), _wspec(w_v), cs_spec],
               [_rows(tm, HD), _rows(tm, KVD), _rows(tm, KVD)],
               [_sds((T, HD), BF16), _sds((T, KVD), BF16), _sds((T, KVD), BF16)], sem=("parallel",))(
                   xb, w_q[0], w_k[0], w_v[0], cs)


def _band_mask(n):
    row = lax.broadcasted_iota(jnp.int32, (BLK, 2 * BLK), 0)
    col = lax.broadcasted_iota(jnp.int32, (BLK, 2 * BLK), 1)
    return (col > row) & (col <= row + BLK) & ((col >= BLK) | (n > 0))


def _softmax_sink(s, sink):
    m = jnp.maximum(jnp.max(s, axis=-1, keepdims=True), sink)
    e = jnp.exp(s - m)
    es = jnp.exp(sink - m)
    den = jnp.sum(e, axis=-1, keepdims=True) + es
    return e / den, es / den


def attn_fwd(q, k, v, sinks):
    NH, T, _ = q.shape
    NKV = k.shape[0]
    G = NH // NKV

    def body(s_ref, q_ref, kc_ref, kp_ref, vc_ref, vp_ref, o_ref):
        valid = _band_mask(pl.program_id(0))
        for kh in range(NKV):
            k2 = jnp.concatenate([kp_ref[kh], kc_ref[kh]], axis=0)
            v2 = jnp.concatenate([vp_ref[kh], vc_ref[kh]], axis=0)
            hs = [kh * G + gq for gq in range(G)]
            sc = [lax.dot_general(q_ref[hh], k2, NT, preferred_element_type=F32) for hh in hs]
            pb = [_softmax_sink(jnp.where(valid, s, NEG), s_ref[0, hh])[0].astype(BF16) for s, hh in zip(sc, hs)]
            for p, hh in zip(pb, hs):
                o_ref[hh] = jnp.dot(p, v2, preferred_element_type=F32).astype(BF16)

    cur = lambda nh: pl.BlockSpec((nh, BLK, HEAD), lambda n: (0, n, 0))
    prev = lambda nh: pl.BlockSpec((nh, BLK, HEAD), lambda n: (0, jnp.maximum(n - 1, 0), 0))
    return _pc(body, "attn_fwd", (T // BLK,),
               [pl.BlockSpec(memory_space=pltpu.SMEM), cur(NH), cur(NKV), prev(NKV), cur(NKV), prev(NKV)],
               cur(NH), _sds((NH, T, HEAD), BF16), sem=("parallel",))(sinks, q, k, k, v, v)


def ple_bwd(dxo, pp, gl, w_gate, name):
    T, D = dxo.shape
    ds = D // NS
    tm = _tile(T)

    def body(d_ref, pp_ref, gl_ref, wg_ref, dpp_ref, dgl_ref, dx_ref):
        d = d_ref[...]
        sg = _sigmoid(gl_ref[...].astype(F32))
        dpp_ref[...] = (d * sg).astype(BF16)
        dgl = (d * pp_ref[...].astype(F32) * sg * (1.0 - sg)).astype(BF16)
        dgl_ref[...] = dgl
        for j in range(NS):
            sl = slice(j * ds, (j + 1) * ds)
            dx_ref[:, sl] = d_ref[:, sl] + lax.dot_general(dgl, wg_ref[j], NT, preferred_element_type=F32)

    return _pc(body, name, (T // tm,), [_rows(tm, D)] * 3 + [_wspec(w_gate)], [_rows(tm, D)] * 3,
               [_sds((T, D), BF16), _sds((T, D), BF16), _sds((T, D), F32)], sem=("parallel",))(dxo, pp, gl, w_gate[0])


def mlp_bwd1(dy, pre, g, r, w_down, name):
    T, D = dy.shape
    fs = w_down[2]
    tm = _tile(T)

    def body(dy_ref, pre_ref, g_ref, r_ref, w_ref, dw_ref, dwb_ref, dm_ref, dg_ref, db_ref):
        dw, dg, db = _ln_bwd(dy_ref[...], pre_ref[...], g_ref[...])
        first = pl.program_id(0) == 0
        _acc_rows(dg_ref, dg, first)
        _acc_rows(db_ref, db, first)
        dwb = dw.astype(BF16)
        dw_ref[...] = dw
        dwb_ref[...] = dwb
        for j in range(NS):
            sl = slice(j * fs, (j + 1) * fs)
            dr = lax.dot_general(dwb, w_ref[j], NT, preferred_element_type=F32)
            dm_ref[:, sl] = (dr * (2.0 * jnp.sqrt(r_ref[:, sl].astype(F32)))).astype(BF16)

    return _pc(body, name, (T // tm,), [_rows(tm, D), _rows(tm, D), _const((1, D)), _rows(tm, NS * fs), _wspec(w_down)],
               [_rows(tm, D), _rows(tm, D), _rows(tm, NS * fs), _const((1, D)), _const((1, D))],
               [_sds((T, D), F32), _sds((T, D), BF16), _sds((T, NS * fs), BF16), _sds((1, D), F32), _sds((1, D), F32)],
               sem=("arbitrary",))(dy, pre, g, r, w_down[0])


def mlp_bwd2(dpre, dm, w_up, alpha, pre_mix, g_mix, w_mix, name):
    T, D = dpre.shape
    fs = w_up[0].shape[2]
    ms = w_mix[2]
    tm = _tile(T)

    def body(dp_ref, dm_ref, wu_ref, pre_ref, g_ref, wm_ref, dw_ref, dwb_ref, do_ref, dg_ref, db_ref, dc_ref):
        dy = alpha * dp_ref[...]
        for j in range(NS):
            dy = dy + lax.dot_general(dm_ref[:, j * fs:(j + 1) * fs], wu_ref[j], NT, preferred_element_type=F32)
        dw, dg, db = _ln_bwd(dy, pre_ref[...], g_ref[...])
        first = pl.program_id(0) == 0
        _acc_rows(dg_ref, dg, first)
        _acc_rows(db_ref, db, first)
        _acc_rows(dc_ref, jnp.sum(dw, axis=0, keepdims=True), first)
        dwb = dw.astype(BF16)
        dw_ref[...] = dw
        dwb_ref[...] = dwb
        for j in range(NS):
            do_ref[:, j * ms:(j + 1) * ms] = lax.dot_general(dwb, wm_ref[j], NT, preferred_element_type=F32).astype(BF16)

    return _pc(body, name, (T // tm,),
               [_rows(tm, D), _rows(tm, NS * fs), _wspec(w_up), _rows(tm, D), _const((1, D)), _wspec(w_mix)],
               [_rows(tm, D), _rows(tm, D), _rows(tm, NS * ms), _const((1, D)), _const((1, D)), _const((1, D))],
               [_sds((T, D), F32), _sds((T, D), BF16), _sds((T, NS * ms), BF16)] + [_sds((1, D), F32)] * 3,
               sem=("arbitrary",))(dpre, dm, w_up[0], pre_mix, g_mix, w_mix[0])


def attn_bwd(q, k, v, do, sinks):
    NH, T, _ = q.shape
    NKV = k.shape[0]
    G = NH // NKV
    nb = T // BLK

    def body(s_ref, q_ref, do_ref, kc_ref, kp_ref, vc_ref, vp_ref, dq_ref, dk_ref, dv_ref, ds_ref, ck, cv):
        n = pl.program_id(0)

        @pl.when(n == 0)
        def _():
            ck[...] = jnp.zeros_like(ck)
            cv[...] = jnp.zeros_like(cv)
            ds_ref[...] = jnp.zeros_like(ds_ref)

        @pl.when(n < nb)
        def _():
            valid = _band_mask(n)
            for kh in range(NKV):
                k2 = jnp.concatenate([kp_ref[kh], kc_ref[kh]], axis=0)
                v2 = jnp.concatenate([vp_ref[kh], vc_ref[kh]], axis=0)
                hs = [kh * G + gq for gq in range(G)]
                sc = [lax.dot_general(q_ref[hh], k2, NT, preferred_element_type=F32) for hh in hs]
                dp = [lax.dot_general(do_ref[hh], v2, NT, preferred_element_type=F32) for hh in hs]
                pr = [_softmax_sink(jnp.where(valid, s, NEG), s_ref[0, hh]) for s, hh in zip(sc, hs)]
                delta = [jnp.sum(p * d, axis=-1, keepdims=True) for (p, _), d in zip(pr, dp)]
                dsb = [(p * (d - dl)).astype(BF16) for (p, _), d, dl in zip(pr, dp, delta)]
                pb = [p.astype(BF16) for p, _ in pr]
                for (_, ps), dl, hh in zip(pr, delta, hs):
                    ds_ref[hh:hh + 1, :] += jnp.broadcast_to(-jnp.sum(ps * dl, axis=0, keepdims=True), (1, 128))
                for d, hh in zip(dsb, hs):
                    dq_ref[hh] = jnp.dot(d, k2, preferred_element_type=F32)
                dk2 = lax.dot_general(jnp.concatenate(dsb, axis=0), q_ref[kh * G:(kh + 1) * G].reshape(G * BLK, HEAD), TN,
                                      preferred_element_type=F32)
                dv2 = lax.dot_general(jnp.concatenate(pb, axis=0), do_ref[kh * G:(kh + 1) * G].reshape(G * BLK, HEAD), TN,
                                      preferred_element_type=F32)
                dk_ref[kh] = ck[kh] + dk2[0:BLK]
                dv_ref[kh] = cv[kh] + dv2[0:BLK]
                ck[kh] = dk2[BLK:2 * BLK]
                cv[kh] = dv2[BLK:2 * BLK]

        @pl.when(n == nb)
        def _():
            dk_ref[...] = ck[...]
            dv_ref[...] = cv[...]

    qcur = pl.BlockSpec((NH, BLK, HEAD), lambda n: (0, jnp.minimum(n, nb - 1), 0))
    kcur = pl.BlockSpec((NKV, BLK, HEAD), lambda n: (0, jnp.minimum(n, nb - 1), 0))
    kprev = pl.BlockSpec((NKV, BLK, HEAD), lambda n: (0, jnp.maximum(n - 1, 0), 0))
    return _pc(body, "attn_bwd", (nb + 1,),
               [pl.BlockSpec(memory_space=pltpu.SMEM), qcur, qcur, kcur, kprev, kcur, kprev],
               [qcur, kprev, kprev, _const((NH, 128))],
               [_sds((NH, T, HEAD), F32), _sds((NKV, T, HEAD), F32), _sds((NKV, T, HEAD), F32), _sds((NH, 128), F32)],
               scratch=[pltpu.VMEM((NKV, BLK, HEAD), F32), pltpu.VMEM((NKV, BLK, HEAD), F32)],
               sem=("arbitrary",))(sinks, q, do, k, k, v, v)


def qkv_bwd(dq, dk, dv, dpre_mix, w_q, w_k, w_v, cs, alpha):
    T, HD = dq.shape
    KVD = dk.shape[1]
    D = dpre_mix.shape[1]
    ds = D // NS
    tm = _tile(T)
    scale = 1.0 / (HEAD ** 0.5)

    def body(dq_ref, dk_ref, dv_ref, dp_ref, wq_ref, wk_ref, wv_ref, cs_ref, dqb_ref, dkb_ref, dvb_ref, dx_ref):
        for gq, val in enumerate(_rope(dq_ref[...], cs_ref, -1.0)):
            dqb_ref[:, gq * 128:(gq + 1) * 128] = (val * scale).astype(BF16)
        for gq, val in enumerate(_rope(dk_ref[...], cs_ref, -1.0)):
            dkb_ref[:, gq * 128:(gq + 1) * 128] = val.astype(BF16)
        dvb_ref[...] = dv_ref[...].astype(BF16)
        dqb, dkb, dvb = dqb_ref[...], dkb_ref[...], dvb_ref[...]
        for j in range(NS):
            sl = slice(j * ds, (j + 1) * ds)
            dx_ref[:, sl] = (alpha * dp_ref[:, sl]
                             + lax.dot_general(dqb, wq_ref[j], NT, preferred_element_type=F32)
                             + lax.dot_general(dkb, wk_ref[j], NT, preferred_element_type=F32)
                             + lax.dot_general(dvb, wv_ref[j], NT, preferred_element_type=F32))

    cs_spec = pl.BlockSpec((2, tm, 128), lambda i: (0, i, 0))
    return _pc(body, "qkv_bwd", (T // tm,),
               [_rows(tm, HD), _rows(tm, KVD), _rows(tm, KVD), _rows(tm, D), _wspec(w_q), _wspec(w_k), _wspec(w_v), cs_spec],
               [_rows(tm, HD), _rows(tm, KVD), _rows(tm, KVD), _rows(tm, D)],
               [_sds((T, HD), BF16), _sds((T, KVD), BF16), _sds((T, KVD), BF16), _sds((T, D), F32)],
               sem=("parallel",))(dq, dk, dv, dpre_mix, w_q[0], w_k[0], w_v[0], cs)


def conv_mid_bwd(ds, cv, ln_g, ln_b):
    T, C = cv.shape
    tm = _tile(T)

    def body(ds_ref, cv_ref, g_ref, b_ref, dcv_ref, dg_ref, db_ref, dc_ref):
        xhat, _ = _ln_stats(cv_ref[...])
        ln = xhat * g_ref[...] + b_ref[...]
        sg = _sigmoid(ln)
        dl = ds_ref[...].astype(F32) * (sg * (1.0 + ln * (1.0 - sg)))
        dcv, dg, db = _ln_bwd(dl, cv_ref[...], g_ref[...])
        first = pl.program_id(0) == 0
        _acc_rows(dg_ref, dg, first)
        _acc_rows(db_ref, db, first)
        _acc_rows(dc_ref, jnp.sum(dcv, axis=0, keepdims=True), first)
        dcv_ref[...] = dcv

    return _pc(body, "conv_mid_bwd", (T // tm,), [_rows(tm, C), _rows(tm, C), _const((1, C)), _const((1, C))],
               [_rows(tm, C), _const((1, C)), _const((1, C)), _const((1, C))],
               [_sds((T, C), F32)] + [_sds((1, C), F32)] * 3, sem=("arbitrary",))(ds, cv, ln_g, ln_b)


def dwconv_bwd(dcv, h, w_dw, taps):
    T, C = dcv.shape
    tq = _tile(T)
    nh = tq // HALO
    nblk = T // tq
    off = HALO - (taps - 1)

    def body(d_ref, dn_ref, a_ref, g_ref, ap_ref, gp_ref, w_ref, dh_ref, dw_ref, dbi_ref, su, sus, sd, sds, wb):
        i = pl.program_id(0)
        su[HALO:HALO + tq, :] = a_ref[...].astype(F32) * _sigmoid(g_ref[...].astype(F32))
        up = ap_ref[...].astype(F32) * _sigmoid(gp_ref[...].astype(F32))
        su[0:HALO, :] = jnp.where(i > 0, up, 0.0)
        sd[0:tq, :] = d_ref[...]
        sd[tq:tq + HALO, :] = jnp.where(i < nblk - 1, dn_ref[...], 0.0)
        _phases(su, sus)
        _phases(sd, sds)

        @pl.when(i == 0)
        def _():
            dw_ref[...] = jnp.zeros_like(dw_ref)

        for j in range(taps):
            dw_ref[j:j + 1, :] += jnp.sum(d_ref[...] * _tap(su, sus, off + j, tq), axis=0, keepdims=True)
        sa = jnp.zeros((1, C), F32)
        sb = jnp.zeros((1, C), F32)
        _spread(w_ref, wb, taps)
        for r in range(tq // CONV_ROWS):
            rows = slice(r * CONV_ROWS, (r + 1) * CONV_ROWS)
            dus = [wb[0] * _tap(sd, sds, taps - 1 + r * CONV_ROWS + 8 * k, 8) for k in range(CONV_ROWS // 8)]
            for j in range(1, taps):
                wj = wb[j]
                dus = [acc + wj * _tap(sd, sds, taps - 1 - j + r * CONV_ROWS + 8 * k, 8) for k, acc in enumerate(dus)]
            du = jnp.concatenate(dus, axis=0)
            a = a_ref[rows, :].astype(F32)
            sg = _sigmoid(g_ref[rows, :].astype(F32))
            da = du * sg
            dgt = du * a * sg * (1.0 - sg)
            dh_ref[rows, 0:C] = da.astype(BF16)
            dh_ref[rows, C:2 * C] = dgt.astype(BF16)
            sa = sa + jnp.sum(da, axis=0, keepdims=True)
            sb = sb + jnp.sum(dgt, axis=0, keepdims=True)
        first = i == 0
        _acc_rows(dbi_ref.at[:, 0:C], sa, first)
        _acc_rows(dbi_ref.at[:, C:2 * C], sb, first)

    prev = lambda col: pl.BlockSpec((HALO, C), lambda i: (jnp.maximum(i * nh - 1, 0), col))
    nxt = pl.BlockSpec((HALO, C), lambda i: (jnp.minimum((i + 1) * nh, T // HALO - 1), 0))
    cur = lambda col: pl.BlockSpec((tq, C), lambda i: (i, col))
    return _pc(body, "dwconv_bwd", (nblk,),
               [cur(0), nxt, cur(0), cur(1), prev(0), prev(1), _const((HALO, C))],
               [_rows(tq, 2 * C), _const((HALO, C)), _const((1, 2 * C))],
               [_sds((T, 2 * C), BF16), _sds((HALO, C), F32), _sds((1, 2 * C), F32)],
               scratch=[pltpu.VMEM((HALO + tq, C), F32), pltpu.VMEM((7, HALO + tq, C), F32),
                        pltpu.VMEM((HALO + tq, C), F32), pltpu.VMEM((7, HALO + tq, C), F32), pltpu.VMEM((taps, 8, C), F32)],
               sem=("arbitrary",))(dcv, dcv, h, h, h, h, w_dw)


def conv_in_bwd(dh, dpre_mix, w_in, alpha):
    T, D = dpre_mix.shape
    nw = w_in[0].shape[2]
    tm = _tile(T)

    def body(dh_ref, dp_ref, w_ref, dx_ref):
        acc = alpha * dp_ref[...]
        for j in range(NS):
            acc = acc + lax.dot_general(dh_ref[:, j * nw:(j + 1) * nw], w_ref[j], NT, preferred_element_type=F32)
        dx_ref[...] = acc

    return _pc(body, "conv_in_bwd", (T // tm,), [_rows(tm, NS * nw), _rows(tm, D), _wspec(w_in)], _rows(tm, D),
               _sds((T, D), F32), sem=("parallel",))(dh, dpre_mix, w_in[0])


def wgrad(a, b, row_sharded, name):
    T, Ka = a.shape
    Nb = b.shape[1]
    tt = min(512, T)
    nt = T // tt
    ka, tn = min(Ka, 1024), min(Nb, 1024)
    if row_sharded:
        sr = Ka // NS
        spb = max(ka // sr, 1)
        out_shape = (NS, sr, Nb)
        out_spec = pl.BlockSpec((spb, ka // spb, tn), lambda i, j, t: (i, 0, j))
    else:
        sc = Nb // NS
        spb = max(tn // sc, 1)
        out_shape = (NS, Ka, sc)
        out_spec = pl.BlockSpec((spb, ka, tn // spb), lambda i, j, t: (j, i, 0))

    def body(a_ref, b_ref, o_ref, acc):
        t = pl.program_id(2)
        av = a_ref[...]
        if av.dtype != BF16:
            av = av.astype(BF16)
        d = lax.dot_general(av, b_ref[...], TN, preferred_element_type=F32)

        @pl.when(t == 0)
        def _():
            acc[...] = d

        @pl.when(t > 0)
        def _():
            acc[...] += d

        @pl.when(t == nt - 1)
        def _():
            for s in range(spb):
                if row_sharded:
                    o_ref[s] = acc[s * (ka // spb):(s + 1) * (ka // spb), :].astype(BF16)
                else:
                    o_ref[s] = acc[:, s * (tn // spb):(s + 1) * (tn // spb)].astype(BF16)

    return _pc(body, name, (Ka // ka, Nb // tn, nt),
               [pl.BlockSpec((tt, ka), lambda i, j, t: (t, i)), pl.BlockSpec((tt, tn), lambda i, j, t: (t, j))],
               out_spec, _sds(out_shape, BF16),
               scratch=[pltpu.VMEM((ka, tn), F32)], sem=("parallel", "parallel", "arbitrary"))(a, b)


def adamw(w, g, m, v, name):
    R, W = w.shape
    tr = R
    for cand in (512, 256, 128, 64, 32, 16, 8):
        if R % cand == 0:
            tr = cand
            break
    c1 = 1.0 - ADAM_B1 ** ADAM_STEP
    c2 = 1.0 - ADAM_B2 ** ADAM_STEP

    def body(w_ref, g_ref, m_ref, v_ref, d_ref, mo_ref, vo_ref):
        gv = g_ref[...]
        mn = ADAM_B1 * m_ref[...] + (1.0 - ADAM_B1) * gv
        vn = ADAM_B2 * v_ref[...] + (1.0 - ADAM_B2) * (gv * gv)
        mo_ref[...] = mn
        vo_ref[...] = vn
        d_ref[...] = -ADAM_LR * ((mn / c1) / (jnp.sqrt(vn / c2) + ADAM_EPS) + ADAM_WD * w_ref[...])

    return _pc(body, name, (R // tr,), [_rows(tr, W)] * 4, [_rows(tr, W)] * 3, [_sds((R, W), F32)] * 3,
               sem=("parallel",))(w, g, m, v)


def _to_heads(t):
    T, n = t.shape
    return t.reshape(T, n // HEAD, HEAD).transpose(1, 0, 2)


def _from_heads(t):
    nh, T, _ = t.shape
    return t.transpose(1, 0, 2).reshape(T, nh * HEAD)


def _rope_tables(T):
    pos = jnp.arange(T, dtype=F32)
    inv_freq = ROPE_THETA ** (-jnp.arange(0, ROPE, 2, dtype=F32) / ROPE)
    ang = pos[:, None] * inv_freq[None, :]
    cos, sin = jnp.cos(ang), jnp.sin(ang)
    pad = HEAD - ROPE
    c = jnp.concatenate([cos, cos, jnp.ones((T, pad), F32)], axis=1)
    s = jnp.concatenate([-sin, sin, jnp.zeros((T, pad), F32)], axis=1)
    return jnp.stack([jnp.tile(c, (1, 128 // HEAD)), jnp.tile(s, (1, 128 // HEAD))])


def _local_step(x, p, target, W, small, hook=None):
    if hook is None:
        hook = lambda stage, after, G, sg=None: None
    T, D = x.shape
    depth = small["mix_ln_g"].shape[0]
    alpha = float((2 * depth) ** 0.25)
    taps = small["taps"]
    row = lambda a, i: a[i:i + 1]
    cs = _rope_tables(T)

    x0b = x.astype(BF16)
    h = conv_in_fwd(x0b, W["conv_w_in"], small["conv_b_in"])
    cv, s = dwconv_fwd(h, small["conv_w_dw"], small["conv_b_dw"], small["conv_ln_g"], small["conv_ln_b"], taps)
    pre_mix0, x1, x1b = mm_res_ln(s, W["conv_w_out"], x, row(small["mix_ln_g"], 0), row(small["mix_ln_b"], 0), alpha,
                                  small["conv_b_out"], "conv_out_fwd")
    hook("weights1", x1b, None)
    r0 = mlp_up_fwd(x1b, W["mlp_w_up0"], "mlp_up_fwd0")
    pre_mlp0, x2, x2b = mm_res_ln(r0, W["mlp_w_down0"], x1, row(small["mlp_ln_g"], 0), row(small["mlp_ln_b"], 0), alpha,
                                  None, "mlp_down_fwd0")
    x3, x3b, pp0, gl0 = ple_fwd(x2, x2b, p, 0, W["ple_w_proj0"], W["ple_w_gate0"], None, "ple_fwd0")

    hook("weights2", x3b, None)
    q, k, v = qkv_fwd(x3b, W["attn_w_q"], W["kv_w_k"], W["kv_w_v"], cs)
    qh, kh, vh = _to_heads(q), _to_heads(k), _to_heads(v)
    o = _from_heads(attn_fwd(qh, kh, vh, small["attn_sinks"]))
    pre_mix1, x4, x4b = mm_res_ln(o, W["attn_w_o"], x3, row(small["mix_ln_g"], 1), row(small["mix_ln_b"], 1), alpha,
                                  None, "attn_out_fwd")
    r1 = mlp_up_fwd(x4b, W["mlp_w_up1"], "mlp_up_fwd1")
    pre_mlp1, x5, x5b = mm_res_ln(r1, W["mlp_w_down1"], x4, row(small["mlp_ln_g"], 1), row(small["mlp_ln_b"], 1), alpha,
                                  None, "mlp_down_fwd1")
    dx6, loss, pp1, gl1 = ple_fwd(x5, x5b, p, 1, W["ple_w_proj1"], W["ple_w_gate1"], target, "ple_fwd1")

    G, sg = {}, {}
    dpp1, dgl1, dx5 = ple_bwd(dx6, pp1, gl1, W["ple_w_gate1"], "ple_bwd1")
    G["ple_w_proj1"] = wgrad(p[1], dpp1, False, "wg_ple_proj1")
    G["ple_w_gate1"] = wgrad(x5b, dgl1, True, "wg_ple_gate1")
    dpre_mlp1, dpre_mlp1b, dm1, g_mlp_g1, g_mlp_b1 = mlp_bwd1(dx5, pre_mlp1, row(small["mlp_ln_g"], 1), r1,
                                                              W["mlp_w_down1"], "mlp_bwd1_1")
    G["mlp_w_down1"] = wgrad(r1, dpre_mlp1b, True, "wg_mlp_down1")
    G["mlp_w_up1"] = wgrad(x4b, dm1, False, "wg_mlp_up1")
    dpre_mix1, dpre_mix1b, do, g_mix_g1, g_mix_b1, _ = mlp_bwd2(dpre_mlp1, dm1, W["mlp_w_up1"], alpha, pre_mix1,
                                                                row(small["mix_ln_g"], 1), W["attn_w_o"], "mlp_bwd2_1")
    G["attn_w_o"] = wgrad(o, dpre_mix1b, True, "wg_attn_o")
    dqh, dkh, dvh, dsinks = attn_bwd(qh, kh, vh, _to_heads(do), small["attn_sinks"])
    dqb, dkb, dvb, dx3 = qkv_bwd(_from_heads(dqh), _from_heads(dkh), _from_heads(dvh), dpre_mix1,
                                 W["attn_w_q"], W["kv_w_k"], W["kv_w_v"], cs, alpha)
    G["attn_w_q"] = wgrad(x3b, dqb, True, "wg_attn_q")
    G["kv_w_k"] = wgrad(x3b, dkb, True, "wg_kv_k")
    G["kv_w_v"] = wgrad(x3b, dvb, True, "wg_kv_v")
    hook("grads2", None, G)

    dpp0, dgl0, dx2 = ple_bwd(dx3, pp0, gl0, W["ple_w_gate0"], "ple_bwd0")
    G["ple_w_proj0"] = wgrad(p[0], dpp0, False, "wg_ple_proj0")
    G["ple_w_gate0"] = wgrad(x2b, dgl0, True, "wg_ple_gate0")
    dpre_mlp0, dpre_mlp0b, dm0, g_mlp_g0, g_mlp_b0 = mlp_bwd1(dx2, pre_mlp0, row(small["mlp_ln_g"], 0), r0,
                                                              W["mlp_w_down0"], "mlp_bwd1_0")
    G["mlp_w_down0"] = wgrad(r0, dpre_mlp0b, True, "wg_mlp_down0")
    G["mlp_w_up0"] = wgrad(x1b, dm0, False, "wg_mlp_up0")
    dpre_mix0, dpre_mix0b, dsw, g_mix_g0, g_mix_b0, g_b_out = mlp_bwd2(dpre_mlp0, dm0, W["mlp_w_up0"], alpha, pre_mix0,
                                                                      row(small["mix_ln_g"], 0), W["conv_w_out"],
                                                                      "mlp_bwd2_0")
    hook("grads1", None, G)
    G["conv_w_out"] = wgrad(s, dpre_mix0b, True, "wg_conv_out")
    dcv, g_cln_g, g_cln_b, g_b_dw = conv_mid_bwd(dsw, cv, small["conv_ln_g"], small["conv_ln_b"])
    dh, g_w_dw, g_b_in = dwconv_bwd(dcv, h, small["conv_w_dw"], taps)
    G["conv_w_in"] = wgrad(x0b, dh, False, "wg_conv_in")

    sg["conv_b_in"] = g_b_in
    sg["conv_w_dw"] = g_w_dw
    sg["conv_b_dw"], sg["conv_ln_g"], sg["conv_ln_b"], sg["conv_b_out"] = g_b_dw, g_cln_g, g_cln_b, g_b_out
    sg["mix_ln_g"] = [g_mix_g0, g_mix_g1]
    sg["mix_ln_b"] = [g_mix_b0, g_mix_b1]
    sg["mlp_ln_g"] = [g_mlp_g0, g_mlp_g1]
    sg["mlp_ln_b"] = [g_mlp_b0, g_mlp_b1]
    sg["attn_sinks"] = dsinks[:, 0][None, :]
    sg["loss"] = loss
    hook("grads0", None, G, sg)
    grad_x = conv_in_bwd(dh, dpre_mix0, W["conv_w_in"], alpha)
    return loss, grad_x, G, sg


BUFFERS = (("b0", ("conv_w_in",)), ("a0", ("conv_w_out",)),
           ("a1", ("mlp_w_up0", "mlp_w_down0", "ple_w_gate0")), ("c1", ("ple_w_proj0",)),
           ("a2", ("mlp_w_up1", "mlp_w_down1", "ple_w_gate1", "attn_w_q", "attn_w_o")),
           ("c2", ("kv_w_k", "kv_w_v", "ple_w_proj1")))
GROUPS = (("b0", "a0"), ("a1", "c1"), ("a2", "c2"))
ROW_SHARDED = {"mlp_w_down0", "mlp_w_down1", "ple_w_gate0", "ple_w_gate1", "conv_w_out", "attn_w_q", "attn_w_o", "kv_w_k",
               "kv_w_v"}


def _split_layers(weights):
    out = {"conv_w_in": weights["conv_w_in"][0], "conv_w_out": weights["conv_w_out"][0],
           "attn_w_q": weights["attn_w_q"][0], "attn_w_o": weights["attn_w_o"][0],
           "kv_w_k": weights["kv_w_k"], "kv_w_v": weights["kv_w_v"]}
    for n in ("mlp_w_up", "mlp_w_down", "ple_w_proj", "ple_w_gate"):
        for i in range(weights[n].shape[0]):
            out[n + str(i)] = weights[n][i]
    return out


def _layout(shards):
    lay = {}
    for key, names in BUFFERS:
        off, rows = 0, []
        for n in names:
            rows.append((n, off, shards[n].shape[0]))
            off += shards[n].shape[0]
        lay[key] = rows
    return lay


def _place():
    return lax.axis_index("x"), lax.axis_index("y"), lax.axis_index("c")


def _flip(v, f):
    return (v + f) % 2 if f else v


CHIP_FLIPS = ((1, 0), (0, 1), (1, 1))


HBM = pl.BlockSpec(memory_space=pltpu.HBM)
SEM = pl.BlockSpec(memory_space=pltpu.SEMAPHORE)
EFFECT = pltpu.SideEffectType.DATAFLOW_SIDE_EFFECTING


def _half(ref, rows, c):
    return ref.at[pl.ds(pl.multiple_of(c * (rows // 2), 16), rows // 2), :]


def _gather_copies(refs, shapes, whole, send, recv):
    x, y, c = _place()
    me = 2 * x + y
    na = len(refs)
    cps = []
    for d, (fx, fy) in enumerate(CHIP_FLIPS):
        to = (_flip(x, fx), _flip(y, fy), c)
        for k in range(na):
            mine = refs[k].at[me] if k >= na - whole else _half(refs[k].at[me], shapes[k][1], c)
            cps.append(pltpu.make_async_remote_copy(mine, mine, send.at[d * na + k], recv.at[d * na + k], device_id=to,
                                                    device_id_type=MESH))
    return cps


def gather_start(bufs, whole, after, name):
    na = len(bufs)
    shapes = [b.shape for b in bufs]
    nsem = len(CHIP_FLIPS) * na

    def body(*refs):
        ins = refs[:na]
        send, recv = refs[-(na + 3)], refs[-(na + 2)]
        token = refs[-1]
        for cp in _gather_copies(ins, shapes, whole, send, recv):
            cp.start()
        token[...] = jnp.zeros_like(token)

    args = [pltpu.with_memory_space_constraint(b, pltpu.HBM) for b in bufs]
    ins = [HBM] * na
    if after is not None:
        args.append(after)
        ins.append(ANY)
    return pl.pallas_call(
        body, name=name, in_specs=ins,
        out_specs=[SEM, SEM] + [HBM] * na + [pl.BlockSpec(memory_space=pltpu.VMEM)],
        out_shape=[pltpu.SemaphoreType.DMA((nsem,)), pltpu.SemaphoreType.DMA((nsem,))]
        + [pltpu.HBM(b.shape, b.dtype) for b in bufs] + [_sds((8, 128), F32)],
        input_output_aliases={k: k + 2 for k in range(na)},
        compiler_params=pltpu.CompilerParams(has_side_effects=EFFECT))(*args)


def gather_wait(send, recv, bufs, whole, after, name):
    na = len(bufs)
    shapes = [b.shape for b in bufs]

    def body(*refs):
        ins = refs[:na]
        send_ref, recv_ref = refs[na], refs[na + 1]
        for cp in _gather_copies(ins, shapes, whole, send_ref, recv_ref):
            cp.wait_send()
            cp.wait_recv()

    return pl.pallas_call(
        body, name=name, in_specs=[HBM] * na + [SEM, SEM, ANY], out_specs=[HBM] * na,
        out_shape=[pltpu.HBM(b.shape, b.dtype) for b in bufs], input_output_aliases={k: k for k in range(na)},
        compiler_params=pltpu.CompilerParams(has_side_effects=EFFECT))(*bufs, send, recv, after)


def sibling_forward(bufs, name):
    nb = len(bufs)

    def body(*refs):
        outs = refs[nb:2 * nb]
        send, recv = refs[2 * nb:]
        x, y, c = _place()
        cps = []
        for d, (fx, fy) in enumerate(CHIP_FLIPS):
            frm = 2 * _flip(x, fx) + _flip(y, fy)
            for k in range(nb):
                theirs = _half(outs[k].at[frm], bufs[k].shape[1], c)
                cps.append(pltpu.make_async_remote_copy(theirs, theirs, send.at[d * nb + k], recv.at[d * nb + k],
                                                        device_id=(x, y, 1 - c), device_id_type=MESH))
        for cp in cps:
            cp.start()
        for cp in cps:
            cp.wait()

    nsem = len(CHIP_FLIPS) * nb
    return pl.pallas_call(
        body, name=name, in_specs=[ANY] * nb, out_specs=[ANY] * nb, out_shape=[_sds(b.shape, b.dtype) for b in bufs],
        input_output_aliases={k: k for k in range(nb)},
        scratch_shapes=[pltpu.SemaphoreType.DMA((nsem,)), pltpu.SemaphoreType.DMA((nsem,))])(*bufs)


def pack_rows(pieces, rows, width, name):
    def body(*refs):
        o_ref = refs[-1]
        o_ref[...] = jnp.zeros_like(o_ref)
        for ref, (a, off) in zip(refs[:-1], pieces):
            o_ref[off:off + a.shape[0], 0:a.shape[1]] = ref[...]

    return pl.pallas_call(body, name=name, out_shape=_sds((rows, width), F32))(*[a for a, _ in pieces])


def sibling_exchange(grads, small, name):
    nb = len(grads)
    ns = 0 if small is None else 1

    def body(*refs):
        ins, outs = refs[:nb + ns], refs[nb + ns:2 * (nb + ns)]
        send, recv, lsem = refs[2 * (nb + ns):]
        x, y, c = _place()
        me = 4 * x + 2 * y + c
        cps = []
        for k in range(nb):
            hrows = grads[k].shape[1] // 2
            src = ins[k].at[:, pl.ds(pl.multiple_of((1 - c) * hrows, 16), hrows), :]
            cps.append(pltpu.make_async_remote_copy(src, outs[k], send.at[k], recv.at[k], device_id=(x, y, 1 - c),
                                                    device_id_type=MESH))
        if ns:
            n = nb
            for fx in (0, 1):
                for fy in (0, 1):
                    for fc in (0, 1):
                        if fx or fy or fc:
                            cps.append(pltpu.make_async_remote_copy(
                                ins[nb], outs[nb].at[me], send.at[n], recv.at[n],
                                device_id=(_flip(x, fx), _flip(y, fy), _flip(c, fc)), device_id_type=MESH))
                            n += 1
            own = pltpu.make_async_copy(ins[nb], outs[nb].at[me], lsem)
            own.start()
        for cp in cps:
            cp.start()
        for cp in cps:
            cp.wait()
        if ns:
            own.wait()

    shapes = [_sds((NS, g.shape[1] // 2, g.shape[2]), g.dtype) for g in grads]
    args = list(grads)
    if ns:
        shapes.append(_sds((8,) + small.shape, small.dtype))
        args.append(small)
    nsem = nb + 7 * ns
    return pl.pallas_call(
        body, name=name, in_specs=[ANY] * (nb + ns), out_specs=[ANY] * (nb + ns), out_shape=shapes,
        scratch_shapes=[pltpu.SemaphoreType.DMA((nsem,)), pltpu.SemaphoreType.DMA((nsem,)), pltpu.SemaphoreType.DMA(())])(*args)


def _chip_copies(sums, lands, send, recv):
    x, y, c = _place()
    nb = len(sums)
    cps = []
    for d, (fx, fy) in enumerate(CHIP_FLIPS):
        tx, ty = _flip(x, fx), _flip(y, fy)
        for k in range(nb):
            cps.append(pltpu.make_async_remote_copy(sums[k].at[2 * tx + ty], lands[k].at[d], send.at[d * nb + k],
                                                    recv.at[d * nb + k], device_id=(tx, ty, c), device_id_type=MESH))
    return cps


def chip_start(sums, name):
    nb = len(sums)
    nsem = len(CHIP_FLIPS) * nb

    def body(*refs):
        ins, lands = refs[:nb], refs[nb:2 * nb]
        send, recv = refs[2 * nb], refs[2 * nb + 1]
        for cp in _chip_copies(ins, lands, send, recv):
            cp.start()
        refs[-1][...] = jnp.zeros_like(refs[-1])

    zones = [lax.empty((len(CHIP_FLIPS),) + s.shape[1:], s.dtype) for s in sums]
    args = [pltpu.with_memory_space_constraint(a, pltpu.HBM) for a in list(sums) + zones]
    return pl.pallas_call(
        body, name=name, in_specs=[HBM] * (2 * nb),
        out_specs=[SEM, SEM] + [HBM] * (2 * nb) + [pl.BlockSpec(memory_space=pltpu.VMEM)],
        out_shape=[pltpu.SemaphoreType.DMA((nsem,)), pltpu.SemaphoreType.DMA((nsem,))]
        + [pltpu.HBM(a.shape, a.dtype) for a in list(sums) + zones] + [_sds((8, 128), F32)],
        input_output_aliases={k: k + 2 for k in range(2 * nb)},
        compiler_params=pltpu.CompilerParams(has_side_effects=EFFECT))(*args)


def chip_wait(send, recv, sums, lands, after, name):
    nb = len(sums)

    def body(*refs):
        ins, zones = refs[:nb], refs[nb:2 * nb]
        for cp in _chip_copies(ins, zones, refs[2 * nb], refs[2 * nb + 1]):
            cp.wait_send()
            cp.wait_recv()

    arrs = list(sums) + list(lands)
    return pl.pallas_call(
        body, name=name, in_specs=[HBM] * (2 * nb) + [SEM, SEM, ANY], out_specs=[HBM] * (2 * nb),
        out_shape=[pltpu.HBM(a.shape, a.dtype) for a in arrs], input_output_aliases={k: k for k in range(2 * nb)},
        compiler_params=pltpu.CompilerParams(has_side_effects=EFFECT))(*arrs, send, recv, after)


def sibling_share(halves):
    nb = len(halves)

    def body(*refs):
        outs = refs[nb:2 * nb]
        send, recv = refs[2 * nb:]
        x, y, c = _place()
        cps = []
        for k in range(nb):
            hrows = halves[k].shape[0] // 2
            mine = outs[k].at[pl.ds(pl.multiple_of(c * hrows, 8), hrows), :]
            cps.append(pltpu.make_async_remote_copy(mine, mine, send.at[k], recv.at[k], device_id=(x, y, 1 - c),
                                                    device_id_type=MESH))
        for cp in cps:
            cp.start()
        for cp in cps:
            cp.wait()

    return pl.pallas_call(
        body, name="sibling_share", in_specs=[ANY] * nb, out_specs=[ANY] * nb,
        out_shape=[_sds(h.shape, h.dtype) for h in halves], input_output_aliases={k: k for k in range(nb)},
        scratch_shapes=[pltpu.SemaphoreType.DMA((nb,)), pltpu.SemaphoreType.DMA((nb,))])(*halves)


def _row_tile(rows):
    for cand in (512, 384, 256, 128, 64, 32, 16):
        if rows % cand == 0:
            return cand
    return rows


def pair_sum(g, r, idx, name):
    _, hrows, W = r.shape
    tr = _row_tile(hrows)
    nrb = hrows // tr

    def body(idx_ref, g_ref, r_ref, o_ref):
        o_ref[...] = (g_ref[...].astype(F32) + r_ref[...].astype(F32)).astype(BF16)

    gs = pltpu.PrefetchScalarGridSpec(
        num_scalar_prefetch=1, grid=(NS, nrb),
        in_specs=[pl.BlockSpec((None, tr, W), lambda j, i, s: (j, s[1] * nrb + i, 0)),
                  pl.BlockSpec((None, tr, W), lambda j, i, s: (j, i, 0))],
        out_specs=pl.BlockSpec((None, tr, W), lambda j, i, s: (j, i, 0)))
    return pl.pallas_call(body, name=name, grid_spec=gs, out_shape=_sds(r.shape, BF16),
                          compiler_params=pltpu.CompilerParams(dimension_semantics=("parallel", "parallel")))(idx, g, r)


def chip_sum(s, t, idx, name):
    _, hrows, W = s.shape
    tr = _row_tile(hrows)
    nrb = hrows // tr

    def body(idx_ref, s_ref, t_ref, o_ref):
        acc = s_ref[...].astype(F32)
        for d in range(t.shape[0]):
            acc = acc + t_ref[d].astype(F32)
        o_ref[...] = acc

    gs = pltpu.PrefetchScalarGridSpec(
        num_scalar_prefetch=1, grid=(nrb,),
        in_specs=[pl.BlockSpec((None, tr, W), lambda i, sc: (sc[0], i, 0)),
                  pl.BlockSpec((t.shape[0], tr, W), lambda i, sc: (0, i, 0))],
        out_specs=pl.BlockSpec((tr, W), lambda i, sc: (sc[1] * nrb + i, 0)))
    return pl.pallas_call(body, name=name, grid_spec=gs, out_shape=_sds((2 * hrows, W), F32),
                          compiler_params=pltpu.CompilerParams(dimension_semantics=("parallel",)))(idx, s, t)


def small_sum(packs):
    n, R, W = packs.shape

    def body(p_ref, o_ref):
        acc = p_ref[0]
        for d in range(1, n):
            acc = acc + p_ref[d]
        o_ref[...] = acc

    return pl.pallas_call(body, name="small_sum", out_shape=_sds((R, W), F32))(packs)


WEIGHTS = ["conv_w_in", "conv_b_in", "conv_w_dw", "conv_b_dw", "conv_ln_g", "conv_ln_b", "conv_w_out", "conv_b_out", "kv_w_k",
           "kv_w_v", "attn_w_q", "attn_sinks", "attn_w_o", "mix_ln_g", "mix_ln_b", "mlp_w_up", "mlp_w_down", "mlp_ln_g",
           "mlp_ln_b", "ple_w_proj", "ple_w_gate"]
BIG = ["conv_w_in", "conv_w_out", "kv_w_k", "kv_w_v", "attn_w_q", "attn_w_o", "mlp_w_up", "mlp_w_down", "ple_w_proj",
       "ple_w_gate"]
SMALL = [n for n in WEIGHTS if n not in BIG]
SHARDED_SMALL = ["conv_b_in", "conv_w_dw", "conv_b_dw", "conv_ln_g", "conv_ln_b", "conv_b_out"]


FLAT = 1024


def _flat_tiles(a):
    f = a.reshape(-1)
    pad = (-f.shape[0]) % FLAT
    return jnp.pad(f, (0, pad)) if pad else f


def _step(x, p, target, w, m, v):
    D = x.shape[-1]
    ds = D // NS
    xq, yq, cq = _place()
    chip = 2 * xq + yq
    idx = jnp.stack([chip, cq]).astype(jnp.int32)

    shards = _split_layers(w)
    lay = _layout(shards)
    taps = w["conv_w_dw"].shape[1]
    small_loc = pack_rows([(w["conv_w_dw"][0], 0), (w["conv_b_dw"], HALO), (w["conv_ln_g"], HALO + 1), (w["conv_ln_b"], HALO + 2),
                           (w["conv_b_out"], HALO + 3), (w["conv_b_in"].reshape(2, ds), HALO + 4)], HALO + 8, ds, "pack_small")
    slot = lambda a: lax.dynamic_update_slice(jnp.zeros((NS,) + a.shape, a.dtype), a[None], (chip, 0, 0))
    started, token = [], None
    for gi, keys in enumerate(GROUPS):
        bufs = [slot(jnp.concatenate([shards[n].astype(BF16) for n, _, _ in lay[key]], axis=0)) for key in keys]
        if gi == 0:
            bufs.append(slot(small_loc))
        send, recv, *thru, token = gather_start(bufs, 1 if gi == 0 else 0, token, "gather_start%d" % gi)
        started.append((send, recv, thru))
    W = {}

    def arrive(gi, after):
        send, recv, thru = started[gi]
        whole = 1 if gi == 0 else 0
        got = gather_wait(send, recv, thru, whole, after, "gather_wait%d" % gi)
        nk = len(GROUPS[gi])
        for key, buf in zip(GROUPS[gi], sibling_forward(got[:nk], "sibling_forward%d" % gi)):
            for n, off, rows in lay[key]:
                W[n] = (buf, off, rows)
        return got[nk:]

    gs, = arrive(0, token)
    across = lambda rows: gs[:, rows, :].transpose(1, 0, 2).reshape(rows.stop - rows.start, D)
    small = {"taps": taps, "conv_w_dw": across(slice(0, HALO)), "conv_b_dw": across(slice(HALO, HALO + 1)),
             "conv_ln_g": across(slice(HALO + 1, HALO + 2)), "conv_ln_b": across(slice(HALO + 2, HALO + 3)),
             "conv_b_out": across(slice(HALO + 3, HALO + 4)), "conv_b_in": gs[:, HALO + 4:HALO + 6, :].reshape(1, 2 * D),
             "attn_sinks": w["attn_sinks"], "mix_ln_g": w["mix_ln_g"], "mix_ln_b": w["mix_ln_b"],
             "mlp_ln_g": w["mlp_ln_g"], "mlp_ln_b": w["mlp_ln_b"]}

    reducing = {}

    def reduce_start(gi, G, pack):
        keys = GROUPS[gi]
        parts = [jnp.concatenate([G[n] for n, _, _ in lay[key]], axis=1) for key in keys]
        got = sibling_exchange(parts, pack, "sibling_exchange%d" % gi)
        sums = [pair_sum(g, r, idx, "pair_sum_" + key) for g, r, key in zip(parts, got, keys)]
        send, recv, *thru, token = chip_start(sums, "chip_start%d" % gi)
        reducing[gi] = (send, recv, thru[:len(keys)], thru[len(keys):])
        _FOLLOW.append(token)
        return got[len(keys):]

    def small_pack(sg):
        pieces = [(sg["conv_b_in"].reshape(2, D), 0), (sg["conv_w_dw"], 2)]
        r0 = 2 + HALO
        for i, n in enumerate(("conv_b_dw", "conv_ln_g", "conv_ln_b", "conv_b_out")):
            pieces.append((sg[n], r0 + i))
        r0 += 4
        for i, n in enumerate(("mix_ln_g", "mix_ln_b", "mlp_ln_g", "mlp_ln_b")):
            pieces += [(sg[n][0], r0 + 2 * i), (sg[n][1], r0 + 2 * i + 1)]
        pieces += [(sg["attn_sinks"], r0 + 8), (sg["loss"][0:1], r0 + 9)]
        return pack_rows(pieces, r0 + 10, D, "pack_small_grads")

    def hook(stage, after, G, sg=None):
        if stage == "weights1":
            arrive(1, after)
        elif stage == "weights2":
            arrive(2, after)
        elif stage == "grads2":
            reduce_start(2, G, None)
        elif stage == "grads1":
            reduce_start(1, G, None)
        elif stage == "grads0":
            reducing["packs"], = reduce_start(0, G, small_pack(sg))

    loss, grad_x, G, sg = _local_step(x[0], p[:, 0], target[0], W, small, hook)
    _FOLLOW.clear()
    nsink = w["attn_sinks"].shape[1]
    tot = small_sum(reducing["packs"])

    halves = {}
    for gi in (2, 1, 0):
        send, recv, sums, lands = reducing[gi]
        done = chip_wait(send, recv, sums, lands, grad_x, "chip_wait%d" % gi)
        nk = len(GROUPS[gi])
        for key, s_, t_ in zip(GROUPS[gi], done[:nk], done[nk:]):
            halves[key] = chip_sum(s_, t_, idx, "chip_sum_" + key)
    order = [key for key, _ in BUFFERS]
    full = sibling_share([halves[key] for key in order])

    grads = {}
    for key, buf in zip(order, full):
        for n, off, rows in lay[key]:
            grads[n] = buf[off:off + rows]
    for n in ("mlp_w_up", "mlp_w_down", "ple_w_proj", "ple_w_gate"):
        grads[n] = jnp.stack([grads.pop(n + str(i)) for i in range(w[n].shape[0])])
    for n in ("conv_w_in", "conv_w_out", "attn_w_q", "attn_w_o"):
        grads[n] = grads[n][None]
    cols = lambda rows: lax.dynamic_slice(rows, (0, chip * ds), (rows.shape[0], ds))
    grads["conv_b_in"] = lax.dynamic_slice(tot[0:2].reshape(1, 2 * D), (0, chip * 2 * ds), (1, 2 * ds))
    grads["conv_w_dw"] = cols(tot[2:2 + taps])[None]
    r0 = 2 + HALO
    for i, n in enumerate(("conv_b_dw", "conv_ln_g", "conv_ln_b", "conv_b_out")):
        grads[n] = cols(tot[r0 + i:r0 + i + 1])
    r0 += 4
    for i, n in enumerate(("mix_ln_g", "mix_ln_b", "mlp_ln_g", "mlp_ln_b")):
        grads[n] = tot[r0 + 2 * i:r0 + 2 * i + 2]
    grads["attn_sinks"] = tot[r0 + 8:r0 + 9, 0:nsink]

    delta, new_m, new_v = {}, {}, {}
    for n in BIG:
        shp = w[n].shape
        two = lambda a: a.reshape(-1, shp[-1])
        d_, m_, v_ = adamw(two(w[n]), two(grads[n]), two(m[n]), two(v[n]), "adamw_" + n)
        delta[n], new_m[n], new_v[n] = d_.reshape(shp), m_.reshape(shp), v_.reshape(shp)
    flat = lambda t: jnp.concatenate([_flat_tiles(t[n]).reshape(-1, 128) for n in SMALL], axis=0)
    d_, m_, v_ = adamw(flat(w), flat(grads), flat(m), flat(v), "adamw_small")
    pos = 0
    for n in SMALL:
        size = w[n].size
        take = lambda a: a.reshape(-1)[pos:pos + size].reshape(w[n].shape)
        delta[n], new_m[n], new_v[n] = take(d_), take(m_), take(v_)
        pos += size + (-size) % FLAT

    total = tot[r0 + 9, 0]
    return (total, grad_x[None], *[grads[n] for n in WEIGHTS], *[delta[n] for n in WEIGHTS], *[new_m[n] for n in WEIGHTS],
            *[new_v[n] for n in WEIGHTS])


def kernel(x, p, conv_w_in, conv_b_in, conv_w_dw, conv_b_dw, conv_ln_g, conv_ln_b, conv_w_out, conv_b_out, kv_w_k, kv_w_v, attn_w_q, attn_sinks, attn_w_o, mix_ln_g, mix_ln_b, mlp_w_up, mlp_w_down, mlp_ln_g, mlp_ln_b, ple_w_proj, ple_w_gate, loss_target, m_conv_w_in, m_conv_b_in, m_conv_w_dw, m_conv_b_dw, m_conv_ln_g, m_conv_ln_b, m_conv_w_out, m_conv_b_out, m_kv_w_k, m_kv_w_v, m_attn_w_q, m_attn_sinks, m_attn_w_o, m_mix_ln_g, m_mix_ln_b, m_mlp_w_up, m_mlp_w_down, m_mlp_ln_g, m_mlp_ln_b, m_ple_w_proj, m_ple_w_gate, v_conv_w_in, v_conv_b_in, v_conv_w_dw, v_conv_b_dw, v_conv_ln_g, v_conv_ln_b, v_conv_w_out, v_conv_b_out, v_kv_w_k, v_kv_w_v, v_attn_w_q, v_attn_sinks, v_attn_w_o, v_mix_ln_g, v_mix_ln_b, v_mlp_w_up, v_mlp_w_down, v_mlp_ln_g, v_mlp_ln_b, v_ple_w_proj, v_ple_w_gate):
    w = dict(zip(WEIGHTS, (conv_w_in, conv_b_in, conv_w_dw, conv_b_dw, conv_ln_g, conv_ln_b, conv_w_out, conv_b_out, kv_w_k,
                           kv_w_v, attn_w_q, attn_sinks, attn_w_o, mix_ln_g, mix_ln_b, mlp_w_up, mlp_w_down, mlp_ln_g, mlp_ln_b,
                           ple_w_proj, ple_w_gate)))
    m = dict(zip(WEIGHTS, (m_conv_w_in, m_conv_b_in, m_conv_w_dw, m_conv_b_dw, m_conv_ln_g, m_conv_ln_b, m_conv_w_out,
                           m_conv_b_out, m_kv_w_k, m_kv_w_v, m_attn_w_q, m_attn_sinks, m_attn_w_o, m_mix_ln_g, m_mix_ln_b,
                           m_mlp_w_up, m_mlp_w_down, m_mlp_ln_g, m_mlp_ln_b, m_ple_w_proj, m_ple_w_gate)))
    v = dict(zip(WEIGHTS, (v_conv_w_in, v_conv_b_in, v_conv_w_dw, v_conv_b_dw, v_conv_ln_g, v_conv_ln_b, v_conv_w_out,
                           v_conv_b_out, v_kv_w_k, v_kv_w_v, v_attn_w_q, v_attn_sinks, v_attn_w_o, v_mix_ln_g, v_mix_ln_b,
                           v_mlp_w_up, v_mlp_w_down, v_mlp_ln_g, v_mlp_ln_b, v_ple_w_proj, v_ple_w_gate)))
    return _step(x, p, loss_target, w, m, v)
```

```python
import functools

import jax
import jax.numpy as jnp
from jax import lax
from jax.experimental import pallas as pl
from jax.experimental.pallas import tpu as pltpu

F32 = jnp.float32
BF16 = jnp.bfloat16
NS = 4
HEAD = 64
BLK = 128
ROPE = 16
ROPE_THETA = 500000.0
LN_EPS = 1e-5
NEG = -1e30
HALO = 32
ADAM_LR, ADAM_B1, ADAM_B2, ADAM_EPS, ADAM_WD, ADAM_STEP = 0.001, 0.9, 0.999, 1e-08, 0.01, 10
MESH = pl.DeviceIdType.MESH
ANY = pl.BlockSpec(memory_space=pl.ANY)
NT = (((1,), (1,)), ((), ()))
TN = (((0,), (0,)), ((), ()))


_FOLLOW = []


def _pc(body, name, grid, in_specs, out_specs, out_shape, scratch=(), sem=None, vmem=56, **kw):
    call = lambda fn, ins: pl.pallas_call(
        fn, name=name, grid=grid, in_specs=ins, out_specs=out_specs, out_shape=out_shape,
        scratch_shapes=list(scratch),
        compiler_params=pltpu.CompilerParams(dimension_semantics=sem, vmem_limit_bytes=vmem * 2 ** 20), **kw)
    if not _FOLLOW:
        return call(body, in_specs)
    extra = list(_FOLLOW)
    _FOLLOW.clear()
    n_in = len(in_specs)

    def ordered(*refs):
        return body(*refs[:n_in], *refs[n_in + len(extra):])

    run = call(ordered, list(in_specs) + [ANY] * len(extra))
    return lambda *args: run(*args, *extra)


def _rows(tm, n):
    return pl.BlockSpec((tm, n), lambda i: (i, 0))


def _const(shape):
    return pl.BlockSpec(shape, lambda *_: (0,) * len(shape))


def _wspec(w):
    buf, off, rows = w
    assert off % rows == 0
    return pl.BlockSpec((NS, rows, buf.shape[2]), lambda *_: (0, off // rows, 0))


def _sds(shape, dtype):
    return jax.ShapeDtypeStruct(shape, dtype)


def _tile(t):
    return min(256, t)


def _sigmoid(x):
    return 1.0 / (1.0 + jnp.exp(-x))


def _ln_stats(w):
    mu = jnp.mean(w, axis=-1, keepdims=True)
    xc = w - mu
    var = jnp.mean(xc * xc, axis=-1, keepdims=True)
    rstd = lax.rsqrt(var + LN_EPS)
    return xc * rstd, rstd


def _ln_bwd(dy, w, g):
    xhat, rstd = _ln_stats(w)
    dxhat = dy * g
    m1 = jnp.mean(dxhat, axis=-1, keepdims=True)
    m2 = jnp.mean(dxhat * xhat, axis=-1, keepdims=True)
    dw = rstd * (dxhat - m1 - xhat * m2)
    return dw, jnp.sum(dy * xhat, axis=0, keepdims=True), jnp.sum(dy, axis=0, keepdims=True)


def _acc_rows(ref, val, first):
    @pl.when(first)
    def _():
        ref[...] = val

    @pl.when(jnp.logical_not(first))
    def _():
        ref[...] += val


def conv_in_fwd(xb, w_in, b_in):
    T, D = xb.shape
    nw = w_in[0].shape[2]
    tm = _tile(T)

    def body(x_ref, w_ref, b_ref, h_ref):
        x = x_ref[...]
        for j in range(NS):
            sl = slice(j * nw, (j + 1) * nw)
            h_ref[:, sl] = (jnp.dot(x, w_ref[j], preferred_element_type=F32) + b_ref[:, sl]).astype(BF16)

    return _pc(body, "conv_in_fwd", (T // tm,), [_rows(tm, D), _wspec(w_in), _const((1, NS * nw))],
               _rows(tm, NS * nw), _sds((T, NS * nw), BF16), sem=("parallel",))(xb, w_in[0], b_in)


CONV_ROWS = 16


def _phases(scr, sh):
    n = scr.shape[0] - 8
    for b in range(1, 8):
        sh[b - 1, 0:n, :] = scr[b:b + n, :]


def _spread(w_ref, wb, taps):
    for j in range(taps):
        wb[j] = jnp.broadcast_to(w_ref[j:j + 1, :], wb.shape[1:])


def _tap(scr, sh, o, n):
    b = o % 8
    return scr[o:o + n, :] if b == 0 else sh[b - 1, o - b:o - b + n, :]


def dwconv_fwd(h, w_dw, b_dw, ln_g, ln_b, taps):
    T = h.shape[0]
    C = h.shape[1] // 2
    tq = _tile(T)
    nh = tq // HALO
    off = HALO - (taps - 1)

    def body(a_ref, g_ref, ap_ref, gp_ref, w_ref, bdw_ref, lg_ref, lb_ref, cv_ref, s_ref, scr, sh, wb):
        i = pl.program_id(0)
        scr[HALO:HALO + tq, :] = a_ref[...].astype(F32) * _sigmoid(g_ref[...].astype(F32))
        up = ap_ref[...].astype(F32) * _sigmoid(gp_ref[...].astype(F32))
        scr[0:HALO, :] = jnp.where(i > 0, up, 0.0)
        _phases(scr, sh)
        _spread(w_ref, wb, taps)
        bias = jnp.broadcast_to(bdw_ref[...], (8, C))
        for r in range(tq // CONV_ROWS):
            accs = [bias] * (CONV_ROWS // 8)
            for j in range(taps):
                wj = wb[j]
                accs = [acc + wj * _tap(scr, sh, off + j + r * CONV_ROWS + 8 * k, 8) for k, acc in enumerate(accs)]
            for k, acc in enumerate(accs):
                cv_ref[r * CONV_ROWS + 8 * k:r * CONV_ROWS + 8 * k + 8, :] = acc
        xhat, _ = _ln_stats(cv_ref[...])
        ln = xhat * lg_ref[...] + lb_ref[...]
        s_ref[...] = (ln * _sigmoid(ln)).astype(BF16)

    prev = lambda col: pl.BlockSpec((HALO, C), lambda i: (jnp.maximum(i * nh - 1, 0), col))
    cur = lambda col: pl.BlockSpec((tq, C), lambda i: (i, col))
    return _pc(body, "dwconv_fwd", (T // tq,),
               [cur(0), cur(1), prev(0), prev(1), _const((HALO, C)), _const((1, C)), _const((1, C)), _const((1, C))],
               [_rows(tq, C), _rows(tq, C)], [_sds((T, C), F32), _sds((T, C), BF16)],
               scratch=[pltpu.VMEM((HALO + tq, C), F32), pltpu.VMEM((7, HALO + tq, C), F32), pltpu.VMEM((taps, 8, C), F32)],
               sem=("parallel",))(h, h, h, h, w_dw, b_dw, ln_g, ln_b)


def mm_res_ln(a, w, res, g, b, alpha, bias, name):
    T, K = a.shape
    ks = K // NS
    D = res.shape[1]
    tm = _tile(T)

    def body(*refs):
        a_ref, w_ref, res_ref, g_ref, b_ref = refs[:5]
        n = 5
        if bias is not None:
            bias_ref = refs[5]
            n = 6
        pre_ref, xo_ref, xb_ref = refs[n:n + 3]
        acc = jnp.dot(a_ref[:, 0:ks], w_ref[0], preferred_element_type=F32)
        for j in range(1, NS):
            acc = acc + jnp.dot(a_ref[:, j * ks:(j + 1) * ks], w_ref[j], preferred_element_type=F32)
        if bias is not None:
            acc = acc + bias_ref[...]
        pre = alpha * res_ref[...] + acc
        xhat, _ = _ln_stats(pre)
        xo = xhat * g_ref[...] + b_ref[...]
        pre_ref[...] = pre
        xo_ref[...] = xo
        xb_ref[...] = xo.astype(BF16)

    ins = [_rows(tm, K), _wspec(w), _rows(tm, D), _const((1, D)), _const((1, D))]
    args = [a, w[0], res, g, b]
    if bias is not None:
        ins.append(_const((1, D)))
        args.append(bias)
    return _pc(body, name, (T // tm,), ins, [_rows(tm, D)] * 3, [_sds((T, D), F32), _sds((T, D), F32), _sds((T, D), BF16)],
               sem=("parallel",))(*args)


def mlp_up_fwd(xb, w_up, name):
    T, D = xb.shape
    fs = w_up[0].shape[2]
    tm = _tile(T)

    def body(x_ref, w_ref, r_ref):
        x = x_ref[...]
        for j in range(NS):
            m = jnp.maximum(jnp.dot(x, w_ref[j], preferred_element_type=F32), 0.0)
            r_ref[:, j * fs:(j + 1) * fs] = (m * m).astype(BF16)

    return _pc(body, name, (T // tm,), [_rows(tm, D), _wspec(w_up)], _rows(tm, NS * fs), _sds((T, NS * fs), BF16),
               sem=("parallel",))(xb, w_up[0])


def ple_fwd(x, xb, p, layer, w_proj, w_gate, target, name):
    T, D = x.shape
    P = p.shape[2]
    ds = D // NS
    tm = _tile(T)
    last = target is not None

    def body(*refs):
        x_ref, xb_ref, p_ref, wp_ref, wg_ref = refs[:5]
        n = 5
        if last:
            t_ref = refs[5]
            n = 6
        o_ref, o2_ref, pp_ref, gl_ref = refs[n:n + 4]
        gl = jnp.dot(xb_ref[:, 0:ds], wg_ref[0], preferred_element_type=F32)
        for j in range(1, NS):
            gl = gl + jnp.dot(xb_ref[:, j * ds:(j + 1) * ds], wg_ref[j], preferred_element_type=F32)
        gl_ref[...] = gl.astype(BF16)
        sg = _sigmoid(gl)
        pb = p_ref[...].astype(BF16)
        sq = jnp.zeros((1, 1), F32)
        for j in range(NS):
            sl = slice(j * ds, (j + 1) * ds)
            pp = jnp.dot(pb, wp_ref[j], preferred_element_type=F32)
            pp_ref[:, sl] = pp.astype(BF16)
            out = x_ref[:, sl] + pp * sg[:, sl]
            if last:
                err = out - t_ref[:, sl]
                o_ref[:, sl] = err * (1.0 / D)
                e2 = jnp.sum(err * err, axis=0, keepdims=True)
                sq = sq + jnp.sum(e2, axis=1, keepdims=True)
            else:
                o_ref[:, sl] = out
                o2_ref[:, sl] = out.astype(BF16)
        if last:
            _acc_rows(o2_ref, jnp.broadcast_to(sq * (0.5 / D), (8, 128)), pl.program_id(0) == 0)

    ins = [_rows(tm, D), _rows(tm, D), pl.BlockSpec((None, tm, P), lambda i: (layer, i, 0)), _wspec(w_proj), _wspec(w_gate)]
    args = [x, xb, p, w_proj[0], w_gate[0]]
    if last:
        ins.append(_rows(tm, D))
        args.append(target)
        outs = [_rows(tm, D), _const((8, 128)), _rows(tm, D), _rows(tm, D)]
        shapes = [_sds((T, D), F32), _sds((8, 128), F32), _sds((T, D), BF16), _sds((T, D), BF16)]
    else:
        outs = [_rows(tm, D)] * 4
        shapes = [_sds((T, D), F32), _sds((T, D), BF16), _sds((T, D), BF16), _sds((T, D), BF16)]
    return _pc(body, name, (T // tm,), ins, outs, shapes, sem=("arbitrary",) if last else ("parallel",))(*args)


def _rope(x, cs_ref, sign):
    c = cs_ref[0]
    s = cs_ref[1] * sign
    lane = lax.broadcasted_iota(jnp.int32, c.shape, 1)
    first = (lane % HEAD) < (ROPE // 2)
    outs = []
    for gq in range(x.shape[1] // 128):
        xg = x[:, gq * 128:(gq + 1) * 128]
        sw = jnp.where(first, pltpu.roll(xg, 128 - ROPE // 2, 1), pltpu.roll(xg, ROPE // 2, 1))
        outs.append(xg * c + sw * s)
    return outs


def qkv_fwd(xb, w_q, w_k, w_v, cs):
    T, D = xb.shape
    ds = D // NS
    HD, KVD = w_q[0].shape[2], w_k[0].shape[2]
    tm = _tile(T)
    scale = 1.0 / (HEAD ** 0.5)

    def body(x_ref, wq_ref, wk_ref, wv_ref, cs_ref, q_ref, k_ref, v_ref):
        def proj(w_ref):
            acc = jnp.dot(x_ref[:, 0:ds], w_ref[0], preferred_element_type=F32)
            for j in range(1, NS):
                acc = acc + jnp.dot(x_ref[:, j * ds:(j + 1) * ds], w_ref[j], preferred_element_type=F32)
            return acc

        for gq, val in enumerate(_rope(proj(wq_ref), cs_ref, 1.0)):
            q_ref[:, gq * 128:(gq + 1) * 128] = (val * scale).astype(BF16)
        for gq, val in enumerate(_rope(proj(wk_ref), cs_ref, 1.0)):
            k_ref[:, gq * 128:(gq + 1) * 128] = val.astype(BF16)
        v_ref[...] = proj(wv_ref).astype(BF16)

    cs_spec = pl.BlockSpec((2, tm, 128), lambda i: (0, i, 0))
    return _pc(body, "qkv_fwd", (T // tm,), [_rows(tm, D), _wspec(w_q), _wspec(w_k), _wspec(w_v), cs_spec],
               [_rows(tm, HD), _rows(tm, KVD), _rows(tm, KVD)],
               [_sds((T, HD), BF16), _sds((T, KVD), BF16), _sds((T, KVD), BF16)], sem=("parallel",))(
                   xb, w_q[0], w_k[0], w_v[0], cs)


def _band_mask(n):
    row = lax.broadcasted_iota(jnp.int32, (BLK, 2 * BLK), 0)
    col = lax.broadcasted_iota(jnp.int32, (BLK, 2 * BLK), 1)
    return (col > row) & (col <= row + BLK) & ((col >= BLK) | (n > 0))


def _head(h):
    return slice(h * HEAD, (h + 1) * HEAD)


def _softmax_sink(s, sink):
    m = jnp.maximum(jnp.max(s, axis=-1, keepdims=True), sink)
    e = jnp.exp(s - m)
    es = jnp.exp(sink - m)
    den = jnp.sum(e, axis=-1, keepdims=True) + es
    return e / den, es / den


def attn_fwd(q, k, v, sinks):
    T, HD = q.shape
    KVD = k.shape[1]
    NKV = KVD // HEAD
    G = HD // KVD

    def body(s_ref, q_ref, kc_ref, kp_ref, vc_ref, vp_ref, o_ref):
        valid = _band_mask(pl.program_id(0))
        for kh in range(NKV):
            k2 = jnp.concatenate([kp_ref[:, _head(kh)], kc_ref[:, _head(kh)]], axis=0)
            v2 = jnp.concatenate([vp_ref[:, _head(kh)], vc_ref[:, _head(kh)]], axis=0)
            hs = [kh * G + gq for gq in range(G)]
            sc = [lax.dot_general(q_ref[:, _head(hh)], k2, NT, preferred_element_type=F32) for hh in hs]
            pb = [_softmax_sink(jnp.where(valid, s, NEG), s_ref[0, hh])[0].astype(BF16) for s, hh in zip(sc, hs)]
            for p, hh in zip(pb, hs):
                o_ref[:, _head(hh)] = jnp.dot(p, v2, preferred_element_type=F32).astype(BF16)

    cur = lambda n_: pl.BlockSpec((BLK, n_), lambda n: (n, 0))
    prev = lambda n_: pl.BlockSpec((BLK, n_), lambda n: (jnp.maximum(n - 1, 0), 0))
    return _pc(body, "attn_fwd", (T // BLK,),
               [pl.BlockSpec(memory_space=pltpu.SMEM), cur(HD), cur(KVD), prev(KVD), cur(KVD), prev(KVD)],
               cur(HD), _sds((T, HD), BF16), sem=("parallel",))(sinks, q, k, k, v, v)


def ple_bwd(dxo, pp, gl, w_gate, name):
    T, D = dxo.shape
    ds = D // NS
    tm = _tile(T)

    def body(d_ref, pp_ref, gl_ref, wg_ref, dpp_ref, dgl_ref, dx_ref):
        d = d_ref[...]
        sg = _sigmoid(gl_ref[...].astype(F32))
        dpp_ref[...] = (d * sg).astype(BF16)
        dgl = (d * pp_ref[...].astype(F32) * sg * (1.0 - sg)).astype(BF16)
        dgl_ref[...] = dgl
        for j in range(NS):
            sl = slice(j * ds, (j + 1) * ds)
            dx_ref[:, sl] = d_ref[:, sl] + lax.dot_general(dgl, wg_ref[j], NT, preferred_element_type=F32)

    return _pc(body, name, (T // tm,), [_rows(tm, D)] * 3 + [_wspec(w_gate)], [_rows(tm, D)] * 3,
               [_sds((T, D), BF16), _sds((T, D), BF16), _sds((T, D), F32)], sem=("parallel",))(dxo, pp, gl, w_gate[0])


def mlp_bwd1(dy, pre, g, r, w_down, name):
    T, D = dy.shape
    fs = w_down[2]
    tm = _tile(T)

    def body(dy_ref, pre_ref, g_ref, r_ref, w_ref, dw_ref, dwb_ref, dm_ref, dg_ref, db_ref):
        dw, dg, db = _ln_bwd(dy_ref[...], pre_ref[...], g_ref[...])
        first = pl.program_id(0) == 0
        _acc_rows(dg_ref, dg, first)
        _acc_rows(db_ref, db, first)
        dwb = dw.astype(BF16)
        dw_ref[...] = dw
        dwb_ref[...] = dwb
        for j in range(NS):
            sl = slice(j * fs, (j + 1) * fs)
            dr = lax.dot_general(dwb, w_ref[j], NT, preferred_element_type=F32)
            dm_ref[:, sl] = (dr * (2.0 * jnp.sqrt(r_ref[:, sl].astype(F32)))).astype(BF16)

    return _pc(body, name, (T // tm,), [_rows(tm, D), _rows(tm, D), _const((1, D)), _rows(tm, NS * fs), _wspec(w_down)],
               [_rows(tm, D), _rows(tm, D), _rows(tm, NS * fs), _const((1, D)), _const((1, D))],
               [_sds((T, D), F32), _sds((T, D), BF16), _sds((T, NS * fs), BF16), _sds((1, D), F32), _sds((1, D), F32)],
               sem=("arbitrary",))(dy, pre, g, r, w_down[0])


def mlp_bwd2(dpre, dm, w_up, alpha, pre_mix, g_mix, w_mix, name):
    T, D = dpre.shape
    fs = w_up[0].shape[2]
    ms = w_mix[2]
    tm = _tile(T)

    def body(dp_ref, dm_ref, wu_ref, pre_ref, g_ref, wm_ref, dw_ref, dwb_ref, do_ref, dg_ref, db_ref, dc_ref):
        dy = alpha * dp_ref[...]
        for j in range(NS):
            dy = dy + lax.dot_general(dm_ref[:, j * fs:(j + 1) * fs], wu_ref[j], NT, preferred_element_type=F32)
        dw, dg, db = _ln_bwd(dy, pre_ref[...], g_ref[...])
        first = pl.program_id(0) == 0
        _acc_rows(dg_ref, dg, first)
        _acc_rows(db_ref, db, first)
        _acc_rows(dc_ref, jnp.sum(dw, axis=0, keepdims=True), first)
        dwb = dw.astype(BF16)
        dw_ref[...] = dw
        dwb_ref[...] = dwb
        for j in range(NS):
            do_ref[:, j * ms:(j + 1) * ms] = lax.dot_general(dwb, wm_ref[j], NT, preferred_element_type=F32).astype(BF16)

    return _pc(body, name, (T // tm,),
               [_rows(tm, D), _rows(tm, NS * fs), _wspec(w_up), _rows(tm, D), _const((1, D)), _wspec(w_mix)],
               [_rows(tm, D), _rows(tm, D), _rows(tm, NS * ms), _const((1, D)), _const((1, D)), _const((1, D))],
               [_sds((T, D), F32), _sds((T, D), BF16), _sds((T, NS * ms), BF16)] + [_sds((1, D), F32)] * 3,
               sem=("arbitrary",))(dpre, dm, w_up[0], pre_mix, g_mix, w_mix[0])


def attn_bwd(q, k, v, do, sinks):
    T, HD = q.shape
    KVD = k.shape[1]
    NH, NKV = HD // HEAD, KVD // HEAD
    G = NH // NKV
    nb = T // BLK

    def body(s_ref, q_ref, do_ref, kc_ref, kp_ref, vc_ref, vp_ref, dq_ref, dk_ref, dv_ref, ds_ref, ck, cv):
        n = pl.program_id(0)

        @pl.when(n == 0)
        def _():
            ck[...] = jnp.zeros_like(ck)
            cv[...] = jnp.zeros_like(cv)
            ds_ref[...] = jnp.zeros_like(ds_ref)

        @pl.when(n < nb)
        def _():
            valid = _band_mask(n)
            for kh in range(NKV):
                kv = _head(kh)
                k2 = jnp.concatenate([kp_ref[:, kv], kc_ref[:, kv]], axis=0)
                v2 = jnp.concatenate([vp_ref[:, kv], vc_ref[:, kv]], axis=0)
                hs = [kh * G + gq for gq in range(G)]
                qs = [q_ref[:, _head(hh)] for hh in hs]
                dos = [do_ref[:, _head(hh)] for hh in hs]
                sc = [lax.dot_general(qh, k2, NT, preferred_element_type=F32) for qh in qs]
                dp = [lax.dot_general(doh, v2, NT, preferred_element_type=F32) for doh in dos]
                pr = [_softmax_sink(jnp.where(valid, s, NEG), s_ref[0, hh]) for s, hh in zip(sc, hs)]
                delta = [jnp.sum(p * d, axis=-1, keepdims=True) for (p, _), d in zip(pr, dp)]
                dsb = [(p * (d - dl)).astype(BF16) for (p, _), d, dl in zip(pr, dp, delta)]
                pb = [p.astype(BF16) for p, _ in pr]
                for (_, ps), dl, hh in zip(pr, delta, hs):
                    ds_ref[hh:hh + 1, :] += jnp.broadcast_to(-jnp.sum(ps * dl, axis=0, keepdims=True), (1, 128))
                for d, hh in zip(dsb, hs):
                    dq_ref[:, _head(hh)] = jnp.dot(d, k2, preferred_element_type=F32)
                dk2 = lax.dot_general(jnp.concatenate(dsb, axis=0), jnp.concatenate(qs, axis=0), TN,
                                      preferred_element_type=F32)
                dv2 = lax.dot_general(jnp.concatenate(pb, axis=0), jnp.concatenate(dos, axis=0), TN,
                                      preferred_element_type=F32)
                dk_ref[:, kv] = ck[:, kv] + dk2[0:BLK]
                dv_ref[:, kv] = cv[:, kv] + dv2[0:BLK]
                ck[:, kv] = dk2[BLK:2 * BLK]
                cv[:, kv] = dv2[BLK:2 * BLK]

        @pl.when(n == nb)
        def _():
            dk_ref[...] = ck[...]
            dv_ref[...] = cv[...]

    qcur = pl.BlockSpec((BLK, HD), lambda n: (jnp.minimum(n, nb - 1), 0))
    kcur = pl.BlockSpec((BLK, KVD), lambda n: (jnp.minimum(n, nb - 1), 0))
    kprev = pl.BlockSpec((BLK, KVD), lambda n: (jnp.maximum(n - 1, 0), 0))
    return _pc(body, "attn_bwd", (nb + 1,),
               [pl.BlockSpec(memory_space=pltpu.SMEM), qcur, qcur, kcur, kprev, kcur, kprev],
               [qcur, kprev, kprev, _const((NH, 128))],
               [_sds((T, HD), F32), _sds((T, KVD), F32), _sds((T, KVD), F32), _sds((NH, 128), F32)],
               scratch=[pltpu.VMEM((BLK, KVD), F32), pltpu.VMEM((BLK, KVD), F32)],
               sem=("arbitrary",))(sinks, q, do, k, k, v, v)


def qkv_bwd(dq, dk, dv, dpre_mix, w_q, w_k, w_v, cs, alpha):
    T, HD = dq.shape
    KVD = dk.shape[1]
    D = dpre_mix.shape[1]
    ds = D // NS
    tm = _tile(T)
    scale = 1.0 / (HEAD ** 0.5)

    def body(dq_ref, dk_ref, dv_ref, dp_ref, wq_ref, wk_ref, wv_ref, cs_ref, dqb_ref, dkb_ref, dvb_ref, dx_ref):
        for gq, val in enumerate(_rope(dq_ref[...], cs_ref, -1.0)):
            dqb_ref[:, gq * 128:(gq + 1) * 128] = (val * scale).astype(BF16)
        for gq, val in enumerate(_rope(dk_ref[...], cs_ref, -1.0)):
            dkb_ref[:, gq * 128:(gq + 1) * 128] = val.astype(BF16)
        dvb_ref[...] = dv_ref[...].astype(BF16)
        dqb, dkb, dvb = dqb_ref[...], dkb_ref[...], dvb_ref[...]
        for j in range(NS):
            sl = slice(j * ds, (j + 1) * ds)
            dx_ref[:, sl] = (alpha * dp_ref[:, sl]
                             + lax.dot_general(dqb, wq_ref[j], NT, preferred_element_type=F32)
                             + lax.dot_general(dkb, wk_ref[j], NT, preferred_element_type=F32)
                             + lax.dot_general(dvb, wv_ref[j], NT, preferred_element_type=F32))

    cs_spec = pl.BlockSpec((2, tm, 128), lambda i: (0, i, 0))
    return _pc(body, "qkv_bwd", (T // tm,),
               [_rows(tm, HD), _rows(tm, KVD), _rows(tm, KVD), _rows(tm, D), _wspec(w_q), _wspec(w_k), _wspec(w_v), cs_spec],
               [_rows(tm, HD), _rows(tm, KVD), _rows(tm, KVD), _rows(tm, D)],
               [_sds((T, HD), BF16), _sds((T, KVD), BF16), _sds((T, KVD), BF16), _sds((T, D), F32)],
               sem=("parallel",))(dq, dk, dv, dpre_mix, w_q[0], w_k[0], w_v[0], cs)


def conv_mid_bwd(ds, cv, ln_g, ln_b):
    T, C = cv.shape
    tm = _tile(T)

    def body(ds_ref, cv_ref, g_ref, b_ref, dcv_ref, dg_ref, db_ref, dc_ref):
        xhat, _ = _ln_stats(cv_ref[...])
        ln = xhat * g_ref[...] + b_ref[...]
        sg = _sigmoid(ln)
        dl = ds_ref[...].astype(F32) * (sg * (1.0 + ln * (1.0 - sg)))
        dcv, dg, db = _ln_bwd(dl, cv_ref[...], g_ref[...])
        first = pl.program_id(0) == 0
        _acc_rows(dg_ref, dg, first)
        _acc_rows(db_ref, db, first)
        _acc_rows(dc_ref, jnp.sum(dcv, axis=0, keepdims=True), first)
        dcv_ref[...] = dcv

    return _pc(body, "conv_mid_bwd", (T // tm,), [_rows(tm, C), _rows(tm, C), _const((1, C)), _const((1, C))],
               [_rows(tm, C), _const((1, C)), _const((1, C)), _const((1, C))],
               [_sds((T, C), F32)] + [_sds((1, C), F32)] * 3, sem=("arbitrary",))(ds, cv, ln_g, ln_b)


def dwconv_bwd(dcv, h, w_dw, taps):
    T, C = dcv.shape
    tq = _tile(T)
    nh = tq // HALO
    nblk = T // tq
    off = HALO - (taps - 1)

    def body(d_ref, dn_ref, a_ref, g_ref, ap_ref, gp_ref, w_ref, dh_ref, dw_ref, dbi_ref, su, sus, sd, sds, wb):
        i = pl.program_id(0)
        su[HALO:HALO + tq, :] = a_ref[...].astype(F32) * _sigmoid(g_ref[...].astype(F32))
        up = ap_ref[...].astype(F32) * _sigmoid(gp_ref[...].astype(F32))
        su[0:HALO, :] = jnp.where(i > 0, up, 0.0)
        sd[0:tq, :] = d_ref[...]
        sd[tq:tq + HALO, :] = jnp.where(i < nblk - 1, dn_ref[...], 0.0)
        _phases(su, sus)
        _phases(sd, sds)

        @pl.when(i == 0)
        def _():
            dw_ref[...] = jnp.zeros_like(dw_ref)

        for j in range(taps):
            dw_ref[j:j + 1, :] += jnp.sum(d_ref[...] * _tap(su, sus, off + j, tq), axis=0, keepdims=True)
        sa = jnp.zeros((1, C), F32)
        sb = jnp.zeros((1, C), F32)
        _spread(w_ref, wb, taps)
        for r in range(tq // CONV_ROWS):
            rows = slice(r * CONV_ROWS, (r + 1) * CONV_ROWS)
            dus = [wb[0] * _tap(sd, sds, taps - 1 + r * CONV_ROWS + 8 * k, 8) for k in range(CONV_ROWS // 8)]
            for j in range(1, taps):
                wj = wb[j]
                dus = [acc + wj * _tap(sd, sds, taps - 1 - j + r * CONV_ROWS + 8 * k, 8) for k, acc in enumerate(dus)]
            du = jnp.concatenate(dus, axis=0)
            a = a_ref[rows, :].astype(F32)
            sg = _sigmoid(g_ref[rows, :].astype(F32))
            da = du * sg
            dgt = du * a * sg * (1.0 - sg)
            dh_ref[rows, 0:C] = da.astype(BF16)
            dh_ref[rows, C:2 * C] = dgt.astype(BF16)
            sa = sa + jnp.sum(da, axis=0, keepdims=True)
            sb = sb + jnp.sum(dgt, axis=0, keepdims=True)
        first = i == 0
        _acc_rows(dbi_ref.at[:, 0:C], sa, first)
        _acc_rows(dbi_ref.at[:, C:2 * C], sb, first)

    prev = lambda col: pl.BlockSpec((HALO, C), lambda i: (jnp.maximum(i * nh - 1, 0), col))
    nxt = pl.BlockSpec((HALO, C), lambda i: (jnp.minimum((i + 1) * nh, T // HALO - 1), 0))
    cur = lambda col: pl.BlockSpec((tq, C), lambda i: (i, col))
    return _pc(body, "dwconv_bwd", (nblk,),
               [cur(0), nxt, cur(0), cur(1), prev(0), prev(1), _const((HALO, C))],
               [_rows(tq, 2 * C), _const((HALO, C)), _const((1, 2 * C))],
               [_sds((T, 2 * C), BF16), _sds((HALO, C), F32), _sds((1, 2 * C), F32)],
               scratch=[pltpu.VMEM((HALO + tq, C), F32), pltpu.VMEM((7, HALO + tq, C), F32),
                        pltpu.VMEM((HALO + tq, C), F32), pltpu.VMEM((7, HALO + tq, C), F32), pltpu.VMEM((taps, 8, C), F32)],
               sem=("arbitrary",))(dcv, dcv, h, h, h, h, w_dw)


def conv_in_bwd(dh, dpre_mix, w_in, alpha):
    T, D = dpre_mix.shape
    nw = w_in[0].shape[2]
    tm = _tile(T)

    def body(dh_ref, dp_ref, w_ref, dx_ref):
        acc = alpha * dp_ref[...]
        for j in range(NS):
            acc = acc + lax.dot_general(dh_ref[:, j * nw:(j + 1) * nw], w_ref[j], NT, preferred_element_type=F32)
        dx_ref[...] = acc

    return _pc(body, "conv_in_bwd", (T // tm,), [_rows(tm, NS * nw), _rows(tm, D), _wspec(w_in)], _rows(tm, D),
               _sds((T, D), F32), sem=("parallel",))(dh, dpre_mix, w_in[0])


def wgrad(a, b, row_sharded, name):
    T, Ka = a.shape
    Nb = b.shape[1]
    tt = min(512, T)
    nt = T // tt
    ka, tn = min(Ka, 1024), min(Nb, 1024)
    if row_sharded:
        sr = Ka // NS
        spb = max(ka // sr, 1)
        out_shape = (NS, sr, Nb)
        out_spec = pl.BlockSpec((spb, ka // spb, tn), lambda i, j, t: (i, 0, j))
    else:
        sc = Nb // NS
        spb = max(tn // sc, 1)
        out_shape = (NS, Ka, sc)
        out_spec = pl.BlockSpec((spb, ka, tn // spb), lambda i, j, t: (j, i, 0))

    def body(a_ref, b_ref, o_ref, acc):
        t = pl.program_id(2)
        av = a_ref[...]
        if av.dtype != BF16:
            av = av.astype(BF16)
        d = lax.dot_general(av, b_ref[...], TN, preferred_element_type=F32)

        @pl.when(t == 0)
        def _():
            acc[...] = d

        @pl.when(t > 0)
        def _():
            acc[...] += d

        @pl.when(t == nt - 1)
        def _():
            for s in range(spb):
                if row_sharded:
                    o_ref[s] = acc[s * (ka // spb):(s + 1) * (ka // spb), :].astype(BF16)
                else:
                    o_ref[s] = acc[:, s * (tn // spb):(s + 1) * (tn // spb)].astype(BF16)

    return _pc(body, name, (Ka // ka, Nb // tn, nt),
               [pl.BlockSpec((tt, ka), lambda i, j, t: (t, i)), pl.BlockSpec((tt, tn), lambda i, j, t: (t, j))],
               out_spec, _sds(out_shape, BF16),
               scratch=[pltpu.VMEM((ka, tn), F32)], sem=("parallel", "parallel", "arbitrary"))(a, b)


def adamw(w, g, m, v, name):
    R, W = w.shape
    tr = R
    for cand in (512, 256, 128, 64, 32, 16, 8):
        if R % cand == 0:
            tr = cand
            break
    c1 = 1.0 - ADAM_B1 ** ADAM_STEP
    c2 = 1.0 - ADAM_B2 ** ADAM_STEP

    def body(w_ref, g_ref, m_ref, v_ref, d_ref, mo_ref, vo_ref):
        gv = g_ref[...]
        mn = ADAM_B1 * m_ref[...] + (1.0 - ADAM_B1) * gv
        vn = ADAM_B2 * v_ref[...] + (1.0 - ADAM_B2) * (gv * gv)
        mo_ref[...] = mn
        vo_ref[...] = vn
        d_ref[...] = -ADAM_LR * ((mn / c1) / (jnp.sqrt(vn / c2) + ADAM_EPS) + ADAM_WD * w_ref[...])

    return _pc(body, name, (R // tr,), [_rows(tr, W)] * 4, [_rows(tr, W)] * 3, [_sds((R, W), F32)] * 3,
               sem=("parallel",))(w, g, m, v)


def _rope_tables(T):
    pos = jnp.arange(T, dtype=F32)
    inv_freq = ROPE_THETA ** (-jnp.arange(0, ROPE, 2, dtype=F32) / ROPE)
    ang = pos[:, None] * inv_freq[None, :]
    cos, sin = jnp.cos(ang), jnp.sin(ang)
    pad = HEAD - ROPE
    c = jnp.concatenate([cos, cos, jnp.ones((T, pad), F32)], axis=1)
    s = jnp.concatenate([-sin, sin, jnp.zeros((T, pad), F32)], axis=1)
    return jnp.stack([jnp.tile(c, (1, 128 // HEAD)), jnp.tile(s, (1, 128 // HEAD))])


def _local_step(x, p, target, W, small, hook=None):
    if hook is None:
        hook = lambda stage, after, G, sg=None: None
    T, D = x.shape
    depth = small["mix_ln_g"].shape[0]
    alpha = float((2 * depth) ** 0.25)
    taps = small["taps"]
    row = lambda a, i: a[i:i + 1]
    cs = _rope_tables(T)

    x0b = x.astype(BF16)
    h = conv_in_fwd(x0b, W["conv_w_in"], small["conv_b_in"])
    cv, s = dwconv_fwd(h, small["conv_w_dw"], small["conv_b_dw"], small["conv_ln_g"], small["conv_ln_b"], taps)
    pre_mix0, x1, x1b = mm_res_ln(s, W["conv_w_out"], x, row(small["mix_ln_g"], 0), row(small["mix_ln_b"], 0), alpha,
                                  small["conv_b_out"], "conv_out_fwd")
    hook("weights1", x1b, None)
    r0 = mlp_up_fwd(x1b, W["mlp_w_up0"], "mlp_up_fwd0")
    pre_mlp0, x2, x2b = mm_res_ln(r0, W["mlp_w_down0"], x1, row(small["mlp_ln_g"], 0), row(small["mlp_ln_b"], 0), alpha,
                                  None, "mlp_down_fwd0")
    x3, x3b, pp0, gl0 = ple_fwd(x2, x2b, p, 0, W["ple_w_proj0"], W["ple_w_gate0"], None, "ple_fwd0")

    hook("weights2", x3b, None)
    q, k, v = qkv_fwd(x3b, W["attn_w_q"], W["kv_w_k"], W["kv_w_v"], cs)
    o = attn_fwd(q, k, v, small["attn_sinks"])
    pre_mix1, x4, x4b = mm_res_ln(o, W["attn_w_o"], x3, row(small["mix_ln_g"], 1), row(small["mix_ln_b"], 1), alpha,
                                  None, "attn_out_fwd")
    r1 = mlp_up_fwd(x4b, W["mlp_w_up1"], "mlp_up_fwd1")
    pre_mlp1, x5, x5b = mm_res_ln(r1, W["mlp_w_down1"], x4, row(small["mlp_ln_g"], 1), row(small["mlp_ln_b"], 1), alpha,
                                  None, "mlp_down_fwd1")
    dx6, loss, pp1, gl1 = ple_fwd(x5, x5b, p, 1, W["ple_w_proj1"], W["ple_w_gate1"], target, "ple_fwd1")

    G, sg = {}, {}
    dpp1, dgl1, dx5 = ple_bwd(dx6, pp1, gl1, W["ple_w_gate1"], "ple_bwd1")
    G["ple_w_proj1"] = wgrad(p[1], dpp1, False, "wg_ple_proj1")
    G["ple_w_gate1"] = wgrad(x5b, dgl1, True, "wg_ple_gate1")
    dpre_mlp1, dpre_mlp1b, dm1, g_mlp_g1, g_mlp_b1 = mlp_bwd1(dx5, pre_mlp1, row(small["mlp_ln_g"], 1), r1,
                                                              W["mlp_w_down1"], "mlp_bwd1_1")
    G["mlp_w_down1"] = wgrad(r1, dpre_mlp1b, True, "wg_mlp_down1")
    G["mlp_w_up1"] = wgrad(x4b, dm1, False, "wg_mlp_up1")
    dpre_mix1, dpre_mix1b, do, g_mix_g1, g_mix_b1, _ = mlp_bwd2(dpre_mlp1, dm1, W["mlp_w_up1"], alpha, pre_mix1,
                                                                row(small["mix_ln_g"], 1), W["attn_w_o"], "mlp_bwd2_1")
    G["attn_w_o"] = wgrad(o, dpre_mix1b, True, "wg_attn_o")
    dq, dk, dv, dsinks = attn_bwd(q, k, v, do, small["attn_sinks"])
    dqb, dkb, dvb, dx3 = qkv_bwd(dq, dk, dv, dpre_mix1,
                                 W["attn_w_q"], W["kv_w_k"], W["kv_w_v"], cs, alpha)
    G["attn_w_q"] = wgrad(x3b, dqb, True, "wg_attn_q")
    G["kv_w_k"] = wgrad(x3b, dkb, True, "wg_kv_k")
    G["kv_w_v"] = wgrad(x3b, dvb, True, "wg_kv_v")
    hook("grads2", None, G)

    dpp0, dgl0, dx2 = ple_bwd(dx3, pp0, gl0, W["ple_w_gate0"], "ple_bwd0")
    G["ple_w_proj0"] = wgrad(p[0], dpp0, False, "wg_ple_proj0")
    G["ple_w_gate0"] = wgrad(x2b, dgl0, True, "wg_ple_gate0")
    dpre_mlp0, dpre_mlp0b, dm0, g_mlp_g0, g_mlp_b0 = mlp_bwd1(dx2, pre_mlp0, row(small["mlp_ln_g"], 0), r0,
                                                              W["mlp_w_down0"], "mlp_bwd1_0")
    G["mlp_w_down0"] = wgrad(r0, dpre_mlp0b, True, "wg_mlp_down0")
    G["mlp_w_up0"] = wgrad(x1b, dm0, False, "wg_mlp_up0")
    dpre_mix0, dpre_mix0b, dsw, g_mix_g0, g_mix_b0, g_b_out = mlp_bwd2(dpre_mlp0, dm0, W["mlp_w_up0"], alpha, pre_mix0,
                                                                      row(small["mix_ln_g"], 0), W["conv_w_out"],
                                                                      "mlp_bwd2_0")
    hook("grads1", None, G)
    dcv, g_cln_g, g_cln_b, g_b_dw = conv_mid_bwd(dsw, cv, small["conv_ln_g"], small["conv_ln_b"])
    dh, g_w_dw, g_b_in = dwconv_bwd(dcv, h, small["conv_w_dw"], taps)
    G["conv_w_out"] = wgrad(s, dpre_mix0b, True, "wg_conv_out")
    G["conv_w_in"] = wgrad(x0b, dh, False, "wg_conv_in")

    sg["conv_b_in"] = g_b_in
    sg["conv_w_dw"] = g_w_dw
    sg["conv_b_dw"], sg["conv_ln_g"], sg["conv_ln_b"], sg["conv_b_out"] = g_b_dw, g_cln_g, g_cln_b, g_b_out
    sg["mix_ln_g"] = [g_mix_g0, g_mix_g1]
    sg["mix_ln_b"] = [g_mix_b0, g_mix_b1]
    sg["mlp_ln_g"] = [g_mlp_g0, g_mlp_g1]
    sg["mlp_ln_b"] = [g_mlp_b0, g_mlp_b1]
    sg["attn_sinks"] = dsinks[:, 0][None, :]
    sg["loss"] = loss
    hook("grads0", None, G, sg)
    grad_x = conv_in_bwd(dh, dpre_mix0, W["conv_w_in"], alpha)
    return loss, grad_x, G, sg


BUFFERS = (("b0", ("conv_w_in",)), ("a0", ("conv_w_out",)),
           ("a1", ("mlp_w_up0", "mlp_w_down0", "ple_w_gate0")), ("c1", ("ple_w_proj0",)),
           ("a2", ("mlp_w_up1", "mlp_w_down1", "ple_w_gate1", "attn_w_q", "attn_w_o")),
           ("c2", ("kv_w_k", "kv_w_v", "ple_w_proj1")))
GROUPS = (("b0", "a0"), ("a1", "c1"), ("a2", "c2"))
ROW_SHARDED = {"mlp_w_down0", "mlp_w_down1", "ple_w_gate0", "ple_w_gate1", "conv_w_out", "attn_w_q", "attn_w_o", "kv_w_k",
               "kv_w_v"}


def _split_layers(weights):
    out = {"conv_w_in": weights["conv_w_in"][0], "conv_w_out": weights["conv_w_out"][0],
           "attn_w_q": weights["attn_w_q"][0], "attn_w_o": weights["attn_w_o"][0],
           "kv_w_k": weights["kv_w_k"], "kv_w_v": weights["kv_w_v"]}
    for n in ("mlp_w_up", "mlp_w_down", "ple_w_proj", "ple_w_gate"):
        for i in range(weights[n].shape[0]):
            out[n + str(i)] = weights[n][i]
    return out


def _layout(shards):
    lay = {}
    for key, names in BUFFERS:
        off, rows = 0, []
        for n in names:
            rows.append((n, off, shards[n].shape[0]))
            off += shards[n].shape[0]
        lay[key] = rows
    return lay


def _place():
    return lax.axis_index("x"), lax.axis_index("y"), lax.axis_index("c")


def _flip(v, f):
    return (v + f) % 2 if f else v


CHIP_FLIPS = ((1, 0), (0, 1), (1, 1))


HBM = pl.BlockSpec(memory_space=pltpu.HBM)
SEM = pl.BlockSpec(memory_space=pltpu.SEMAPHORE)
EFFECT = pltpu.SideEffectType.DATAFLOW_SIDE_EFFECTING


def _half(ref, rows, c):
    return ref.at[pl.ds(pl.multiple_of(c * (rows // 2), 16), rows // 2), :]


def _gather_copies(refs, shapes, whole, send, recv):
    x, y, c = _place()
    me = 2 * x + y
    na = len(refs)
    cps = []
    for d, (fx, fy) in enumerate(CHIP_FLIPS):
        to = (_flip(x, fx), _flip(y, fy), c)
        for k in range(na):
            mine = refs[k].at[me] if k >= na - whole else _half(refs[k].at[me], shapes[k][1], c)
            cps.append(pltpu.make_async_remote_copy(mine, mine, send.at[d * na + k], recv.at[d * na + k], device_id=to,
                                                    device_id_type=MESH))
    return cps


def gather_start(bufs, whole, after, name):
    na = len(bufs)
    shapes = [b.shape for b in bufs]
    nsem = len(CHIP_FLIPS) * na

    def body(*refs):
        ins = refs[:na]
        send, recv = refs[-(na + 3)], refs[-(na + 2)]
        token = refs[-1]
        for cp in _gather_copies(ins, shapes, whole, send, recv):
            cp.start()
        token[...] = jnp.zeros_like(token)

    args = [pltpu.with_memory_space_constraint(b, pltpu.HBM) for b in bufs]
    ins = [HBM] * na
    if after is not None:
        args.append(after)
        ins.append(ANY)
    return pl.pallas_call(
        body, name=name, in_specs=ins,
        out_specs=[SEM, SEM] + [HBM] * na + [pl.BlockSpec(memory_space=pltpu.VMEM)],
        out_shape=[pltpu.SemaphoreType.DMA((nsem,)), pltpu.SemaphoreType.DMA((nsem,))]
        + [pltpu.HBM(b.shape, b.dtype) for b in bufs] + [_sds((8, 128), F32)],
        input_output_aliases={k: k + 2 for k in range(na)},
        compiler_params=pltpu.CompilerParams(has_side_effects=EFFECT))(*args)


def gather_wait(send, recv, bufs, whole, after, name):
    na = len(bufs)
    shapes = [b.shape for b in bufs]

    def body(*refs):
        ins = refs[:na]
        send_ref, recv_ref = refs[na], refs[na + 1]
        for cp in _gather_copies(ins, shapes, whole, send_ref, recv_ref):
            cp.wait_send()
            cp.wait_recv()

    return pl.pallas_call(
        body, name=name, in_specs=[HBM] * na + [SEM, SEM, ANY], out_specs=[HBM] * na,
        out_shape=[pltpu.HBM(b.shape, b.dtype) for b in bufs], input_output_aliases={k: k for k in range(na)},
        compiler_params=pltpu.CompilerParams(has_side_effects=EFFECT))(*bufs, send, recv, after)


def sibling_forward(bufs, name):
    nb = len(bufs)

    def body(*refs):
        outs = refs[nb:2 * nb]
        send, recv = refs[2 * nb:]
        x, y, c = _place()
        cps = []
        for d, (fx, fy) in enumerate(CHIP_FLIPS):
            frm = 2 * _flip(x, fx) + _flip(y, fy)
            for k in range(nb):
                theirs = _half(outs[k].at[frm], bufs[k].shape[1], c)
                cps.append(pltpu.make_async_remote_copy(theirs, theirs, send.at[d * nb + k], recv.at[d * nb + k],
                                                        device_id=(x, y, 1 - c), device_id_type=MESH))
        for cp in cps:
            cp.start()
        for cp in cps:
            cp.wait()

    nsem = len(CHIP_FLIPS) * nb
    return pl.pallas_call(
        body, name=name, in_specs=[ANY] * nb, out_specs=[ANY] * nb, out_shape=[_sds(b.shape, b.dtype) for b in bufs],
        input_output_aliases={k: k for k in range(nb)},
        scratch_shapes=[pltpu.SemaphoreType.DMA((nsem,)), pltpu.SemaphoreType.DMA((nsem,))])(*bufs)


def pack_rows(pieces, rows, width, name):
    def body(*refs):
        o_ref = refs[-1]
        o_ref[...] = jnp.zeros_like(o_ref)
        for ref, (a, off) in zip(refs[:-1], pieces):
            o_ref[off:off + a.shape[0], 0:a.shape[1]] = ref[...]

    return pl.pallas_call(body, name=name, out_shape=_sds((rows, width), F32))(*[a for a, _ in pieces])


def sibling_exchange(grads, small, name):
    nb = len(grads)
    ns = 0 if small is None else 1

    def body(*refs):
        ins, outs = refs[:nb + ns], refs[nb + ns:2 * (nb + ns)]
        send, recv, lsem = refs[2 * (nb + ns):]
        x, y, c = _place()
        me = 4 * x + 2 * y + c
        cps = []
        for k in range(nb):
            hrows = grads[k].shape[1] // 2
            src = ins[k].at[:, pl.ds(pl.multiple_of((1 - c) * hrows, 16), hrows), :]
            cps.append(pltpu.make_async_remote_copy(src, outs[k], send.at[k], recv.at[k], device_id=(x, y, 1 - c),
                                                    device_id_type=MESH))
        if ns:
            n = nb
            for fx in (0, 1):
                for fy in (0, 1):
                    for fc in (0, 1):
                        if fx or fy or fc:
                            cps.append(pltpu.make_async_remote_copy(
                                ins[nb], outs[nb].at[me], send.at[n], recv.at[n],
                                device_id=(_flip(x, fx), _flip(y, fy), _flip(c, fc)), device_id_type=MESH))
                            n += 1
            own = pltpu.make_async_copy(ins[nb], outs[nb].at[me], lsem)
            own.start()
        for cp in cps:
            cp.start()
        for cp in cps:
            cp.wait()
        if ns:
            own.wait()

    shapes = [_sds((NS, g.shape[1] // 2, g.shape[2]), g.dtype) for g in grads]
    args = list(grads)
    if ns:
        shapes.append(_sds((8,) + small.shape, small.dtype))
        args.append(small)
    nsem = nb + 7 * ns
    return pl.pallas_call(
        body, name=name, in_specs=[ANY] * (nb + ns), out_specs=[ANY] * (nb + ns), out_shape=shapes,
        scratch_shapes=[pltpu.SemaphoreType.DMA((nsem,)), pltpu.SemaphoreType.DMA((nsem,)), pltpu.SemaphoreType.DMA(())])(*args)


def _chip_copies(sums, lands, send, recv):
    x, y, c = _place()
    nb = len(sums)
    cps = []
    for d, (fx, fy) in enumerate(CHIP_FLIPS):
        tx, ty = _flip(x, fx), _flip(y, fy)
        for k in range(nb):
            cps.append(pltpu.make_async_remote_copy(sums[k].at[2 * tx + ty], lands[k].at[d], send.at[d * nb + k],
                                                    recv.at[d * nb + k], device_id=(tx, ty, c), device_id_type=MESH))
    return cps


def chip_start(sums, name):
    nb = len(sums)
    nsem = len(CHIP_FLIPS) * nb

    def body(*refs):
        ins, lands = refs[:nb], refs[nb:2 * nb]
        send, recv = refs[2 * nb], refs[2 * nb + 1]
        for cp in _chip_copies(ins, lands, send, recv):
            cp.start()
        refs[-1][...] = jnp.zeros_like(refs[-1])

    zones = [lax.empty((len(CHIP_FLIPS),) + s.shape[1:], s.dtype) for s in sums]
    args = [pltpu.with_memory_space_constraint(a, pltpu.HBM) for a in list(sums) + zones]
    return pl.pallas_call(
        body, name=name, in_specs=[HBM] * (2 * nb),
        out_specs=[SEM, SEM] + [HBM] * (2 * nb) + [pl.BlockSpec(memory_space=pltpu.VMEM)],
        out_shape=[pltpu.SemaphoreType.DMA((nsem,)), pltpu.SemaphoreType.DMA((nsem,))]
        + [pltpu.HBM(a.shape, a.dtype) for a in list(sums) + zones] + [_sds((8, 128), F32)],
        input_output_aliases={k: k + 2 for k in range(2 * nb)},
        compiler_params=pltpu.CompilerParams(has_side_effects=EFFECT))(*args)


def chip_wait(send, recv, sums, lands, after, name):
    nb = len(sums)

    def body(*refs):
        ins, zones = refs[:nb], refs[nb:2 * nb]
        for cp in _chip_copies(ins, zones, refs[2 * nb], refs[2 * nb + 1]):
            cp.wait_send()
            cp.wait_recv()

    arrs = list(sums) + list(lands)
    return pl.pallas_call(
        body, name=name, in_specs=[HBM] * (2 * nb) + [SEM, SEM, ANY], out_specs=[HBM] * (2 * nb),
        out_shape=[pltpu.HBM(a.shape, a.dtype) for a in arrs], input_output_aliases={k: k for k in range(2 * nb)},
        compiler_params=pltpu.CompilerParams(has_side_effects=EFFECT))(*arrs, send, recv, after)


def sibling_share(halves):
    nb = len(halves)

    def body(*refs):
        outs = refs[nb:2 * nb]
        send, recv = refs[2 * nb:]
        x, y, c = _place()
        cps = []
        for k in range(nb):
            hrows = halves[k].shape[0] // 2
            mine = outs[k].at[pl.ds(pl.multiple_of(c * hrows, 8), hrows), :]
            cps.append(pltpu.make_async_remote_copy(mine, mine, send.at[k], recv.at[k], device_id=(x, y, 1 - c),
                                                    device_id_type=MESH))
        for cp in cps:
            cp.start()
        for cp in cps:
            cp.wait()

    return pl.pallas_call(
        body, name="sibling_share", in_specs=[ANY] * nb, out_specs=[ANY] * nb,
        out_shape=[_sds(h.shape, h.dtype) for h in halves], input_output_aliases={k: k for k in range(nb)},
        scratch_shapes=[pltpu.SemaphoreType.DMA((nb,)), pltpu.SemaphoreType.DMA((nb,))])(*halves)


def _row_tile(rows):
    for cand in (512, 384, 256, 128, 64, 32, 16):
        if rows % cand == 0:
            return cand
    return rows


def pair_sum(g, r, idx, name):
    _, hrows, W = r.shape
    tr = _row_tile(hrows)
    nrb = hrows // tr

    def body(idx_ref, g_ref, r_ref, o_ref):
        o_ref[...] = (g_ref[...].astype(F32) + r_ref[...].astype(F32)).astype(BF16)

    gs = pltpu.PrefetchScalarGridSpec(
        num_scalar_prefetch=1, grid=(NS, nrb),
        in_specs=[pl.BlockSpec((None, tr, W), lambda j, i, s: (j, s[1] * nrb + i, 0)),
                  pl.BlockSpec((None, tr, W), lambda j, i, s: (j, i, 0))],
        out_specs=pl.BlockSpec((None, tr, W), lambda j, i, s: (j, i, 0)))
    return pl.pallas_call(body, name=name, grid_spec=gs, out_shape=_sds(r.shape, BF16),
                          compiler_params=pltpu.CompilerParams(dimension_semantics=("parallel", "parallel")))(idx, g, r)


def chip_sum(s, t, idx, name):
    _, hrows, W = s.shape
    tr = _row_tile(hrows)
    nrb = hrows // tr

    def body(idx_ref, s_ref, t_ref, o_ref):
        acc = s_ref[...].astype(F32)
        for d in range(t.shape[0]):
            acc = acc + t_ref[d].astype(F32)
        o_ref[...] = acc

    gs = pltpu.PrefetchScalarGridSpec(
        num_scalar_prefetch=1, grid=(nrb,),
        in_specs=[pl.BlockSpec((None, tr, W), lambda i, sc: (sc[0], i, 0)),
                  pl.BlockSpec((t.shape[0], tr, W), lambda i, sc: (0, i, 0))],
        out_specs=pl.BlockSpec((tr, W), lambda i, sc: (sc[1] * nrb + i, 0)))
    return pl.pallas_call(body, name=name, grid_spec=gs, out_shape=_sds((2 * hrows, W), F32),
                          compiler_params=pltpu.CompilerParams(dimension_semantics=("parallel",)))(idx, s, t)


def small_sum(packs):
    n, R, W = packs.shape

    def body(p_ref, o_ref):
        acc = p_ref[0]
        for d in range(1, n):
            acc = acc + p_ref[d]
        o_ref[...] = acc

    return pl.pallas_call(body, name="small_sum", out_shape=_sds((R, W), F32))(packs)


WEIGHTS = ["conv_w_in", "conv_b_in", "conv_w_dw", "conv_b_dw", "conv_ln_g", "conv_ln_b", "conv_w_out", "conv_b_out", "kv_w_k",
           "kv_w_v", "attn_w_q", "attn_sinks", "attn_w_o", "mix_ln_g", "mix_ln_b", "mlp_w_up", "mlp_w_down", "mlp_ln_g",
           "mlp_ln_b", "ple_w_proj", "ple_w_gate"]
BIG = ["conv_w_in", "conv_w_out", "kv_w_k", "kv_w_v", "attn_w_q", "attn_w_o", "mlp_w_up", "mlp_w_down", "ple_w_proj",
       "ple_w_gate"]
SMALL = [n for n in WEIGHTS if n not in BIG]
SHARDED_SMALL = ["conv_b_in", "conv_w_dw", "conv_b_dw", "conv_ln_g", "conv_ln_b", "conv_b_out"]


FLAT = 1024


def _flat_tiles(a):
    f = a.reshape(-1)
    pad = (-f.shape[0]) % FLAT
    return jnp.pad(f, (0, pad)) if pad else f


def _step(x, p, target, w, m, v):
    D = x.shape[-1]
    ds = D // NS
    xq, yq, cq = _place()
    chip = 2 * xq + yq
    idx = jnp.stack([chip, cq]).astype(jnp.int32)

    shards = _split_layers(w)
    lay = _layout(shards)
    taps = w["conv_w_dw"].shape[1]
    small_loc = pack_rows([(w["conv_w_dw"][0], 0), (w["conv_b_dw"], HALO), (w["conv_ln_g"], HALO + 1), (w["conv_ln_b"], HALO + 2),
                           (w["conv_b_out"], HALO + 3), (w["conv_b_in"].reshape(2, ds), HALO + 4)], HALO + 8, ds, "pack_small")
    slot = lambda a: lax.dynamic_update_slice(lax.empty((NS,) + a.shape, a.dtype), a[None], (chip, 0, 0))
    started, token = [], None
    for gi, keys in enumerate(GROUPS):
        bufs = [slot(jnp.concatenate([shards[n].astype(BF16) for n, _, _ in lay[key]], axis=0)) for key in keys]
        if gi == 0:
            bufs.append(slot(small_loc))
        send, recv, *thru, token = gather_start(bufs, 1 if gi == 0 else 0, token, "gather_start%d" % gi)
        started.append((send, recv, thru))
    W = {}

    def arrive(gi, after):
        send, recv, thru = started[gi]
        whole = 1 if gi == 0 else 0
        got = gather_wait(send, recv, thru, whole, after, "gather_wait%d" % gi)
        nk = len(GROUPS[gi])
        for key, buf in zip(GROUPS[gi], sibling_forward(got[:nk], "sibling_forward%d" % gi)):
            for n, off, rows in lay[key]:
                W[n] = (buf, off, rows)
        return got[nk:]

    gs, = arrive(0, token)
    across = lambda rows: gs[:, rows, :].transpose(1, 0, 2).reshape(rows.stop - rows.start, D)
    small = {"taps": taps, "conv_w_dw": across(slice(0, HALO)), "conv_b_dw": across(slice(HALO, HALO + 1)),
             "conv_ln_g": across(slice(HALO + 1, HALO + 2)), "conv_ln_b": across(slice(HALO + 2, HALO + 3)),
             "conv_b_out": across(slice(HALO + 3, HALO + 4)), "conv_b_in": gs[:, HALO + 4:HALO + 6, :].reshape(1, 2 * D),
             "attn_sinks": w["attn_sinks"], "mix_ln_g": w["mix_ln_g"], "mix_ln_b": w["mix_ln_b"],
             "mlp_ln_g": w["mlp_ln_g"], "mlp_ln_b": w["mlp_ln_b"]}

    reducing = {}

    def reduce_start(gi, G, pack):
        keys = GROUPS[gi]
        parts = [jnp.concatenate([G[n] for n, _, _ in lay[key]], axis=1) for key in keys]
        got = sibling_exchange(parts, pack, "sibling_exchange%d" % gi)
        sums = [pair_sum(g, r, idx, "pair_sum_" + key) for g, r, key in zip(parts, got, keys)]
        send, recv, *thru, token = chip_start(sums, "chip_start%d" % gi)
        reducing[gi] = (send, recv, thru[:len(keys)], thru[len(keys):])
        _FOLLOW.append(token)
        return got[len(keys):]

    def small_pack(sg):
        pieces = [(sg["conv_b_in"].reshape(2, D), 0), (sg["conv_w_dw"], 2)]
        r0 = 2 + HALO
        for i, n in enumerate(("conv_b_dw", "conv_ln_g", "conv_ln_b", "conv_b_out")):
            pieces.append((sg[n], r0 + i))
        r0 += 4
        for i, n in enumerate(("mix_ln_g", "mix_ln_b", "mlp_ln_g", "mlp_ln_b")):
            pieces += [(sg[n][0], r0 + 2 * i), (sg[n][1], r0 + 2 * i + 1)]
        pieces += [(sg["attn_sinks"], r0 + 8), (sg["loss"][0:1], r0 + 9)]
        return pack_rows(pieces, r0 + 10, D, "pack_small_grads")

    def hook(stage, after, G, sg=None):
        if stage == "weights1":
            arrive(1, after)
        elif stage == "weights2":
            arrive(2, after)
        elif stage == "grads2":
            reduce_start(2, G, None)
        elif stage == "grads1":
            reduce_start(1, G, None)
        elif stage == "grads0":
            reducing["packs"], = reduce_start(0, G, small_pack(sg))

    loss, grad_x, G, sg = _local_step(x[0], p[:, 0], target[0], W, small, hook)
    _FOLLOW.clear()
    nsink = w["attn_sinks"].shape[1]
    tot = small_sum(reducing["packs"])

    halves = {}
    for gi in (2, 1, 0):
        send, recv, sums, lands = reducing[gi]
        done = chip_wait(send, recv, sums, lands, grad_x, "chip_wait%d" % gi)
        nk = len(GROUPS[gi])
        for key, s_, t_ in zip(GROUPS[gi], done[:nk], done[nk:]):
            halves[key] = chip_sum(s_, t_, idx, "chip_sum_" + key)
    order = [key for key, _ in BUFFERS]
    full = sibling_share([halves[key] for key in order])

    grads = {}
    for key, buf in zip(order, full):
        for n, off, rows in lay[key]:
            grads[n] = buf[off:off + rows]
    for n in ("mlp_w_up", "mlp_w_down", "ple_w_proj", "ple_w_gate"):
        grads[n] = jnp.stack([grads.pop(n + str(i)) for i in range(w[n].shape[0])])
    for n in ("conv_w_in", "conv_w_out", "attn_w_q", "attn_w_o"):
        grads[n] = grads[n][None]
    cols = lambda rows: lax.dynamic_slice(rows, (0, chip * ds), (rows.shape[0], ds))
    grads["conv_b_in"] = lax.dynamic_slice(tot[0:2].reshape(1, 2 * D), (0, chip * 2 * ds), (1, 2 * ds))
    grads["conv_w_dw"] = cols(tot[2:2 + taps])[None]
    r0 = 2 + HALO
    for i, n in enumerate(("conv_b_dw", "conv_ln_g", "conv_ln_b", "conv_b_out")):
        grads[n] = cols(tot[r0 + i:r0 + i + 1])
    r0 += 4
    for i, n in enumerate(("mix_ln_g", "mix_ln_b", "mlp_ln_g", "mlp_ln_b")):
        grads[n] = tot[r0 + 2 * i:r0 + 2 * i + 2]
    grads["attn_sinks"] = tot[r0 + 8:r0 + 9, 0:nsink]

    delta, new_m, new_v = {}, {}, {}
    for n in BIG:
        shp = w[n].shape
        two = lambda a: a.reshape(-1, shp[-1])
        d_, m_, v_ = adamw(two(w[n]), two(grads[n]), two(m[n]), two(v[n]), "adamw_" + n)
        delta[n], new_m[n], new_v[n] = d_.reshape(shp), m_.reshape(shp), v_.reshape(shp)
    flat = lambda t: jnp.concatenate([_flat_tiles(t[n]).reshape(-1, 128) for n in SMALL], axis=0)
    d_, m_, v_ = adamw(flat(w), flat(grads), flat(m), flat(v), "adamw_small")
    pos = 0
    for n in SMALL:
        size = w[n].size
        take = lambda a: a.reshape(-1)[pos:pos + size].reshape(w[n].shape)
        delta[n], new_m[n], new_v[n] = take(d_), take(m_), take(v_)
        pos += size + (-size) % FLAT

    total = tot[r0 + 9, 0]
    return (total, grad_x[None], *[grads[n] for n in WEIGHTS], *[delta[n] for n in WEIGHTS], *[new_m[n] for n in WEIGHTS],
            *[new_v[n] for n in WEIGHTS])


def kernel(x, p, conv_w_in, conv_b_in, conv_w_dw, conv_b_dw, conv_ln_g, conv_ln_b, conv_w_out, conv_b_out, kv_w_k, kv_w_v, attn_w_q, attn_sinks, attn_w_o, mix_ln_g, mix_ln_b, mlp_w_up, mlp_w_down, mlp_ln_g, mlp_ln_b, ple_w_proj, ple_w_gate, loss_target, m_conv_w_in, m_conv_b_in, m_conv_w_dw, m_conv_b_dw, m_conv_ln_g, m_conv_ln_b, m_conv_w_out, m_conv_b_out, m_kv_w_k, m_kv_w_v, m_attn_w_q, m_attn_sinks, m_attn_w_o, m_mix_ln_g, m_mix_ln_b, m_mlp_w_up, m_mlp_w_down, m_mlp_ln_g, m_mlp_ln_b, m_ple_w_proj, m_ple_w_gate, v_conv_w_in, v_conv_b_in, v_conv_w_dw, v_conv_b_dw, v_conv_ln_g, v_conv_ln_b, v_conv_w_out, v_conv_b_out, v_kv_w_k, v_kv_w_v, v_attn_w_q, v_attn_sinks, v_attn_w_o, v_mix_ln_g, v_mix_ln_b, v_mlp_w_up, v_mlp_w_down, v_mlp_ln_g, v_mlp_ln_b, v_ple_w_proj, v_ple_w_gate):
    w = dict(zip(WEIGHTS, (conv_w_in, conv_b_in, conv_w_dw, conv_b_dw, conv_ln_g, conv_ln_b, conv_w_out, conv_b_out, kv_w_k,
                           kv_w_v, attn_w_q, attn_sinks, attn_w_o, mix_ln_g, mix_ln_b, mlp_w_up, mlp_w_down, mlp_ln_g, mlp_ln_b,
                           ple_w_proj, ple_w_gate)))
    m = dict(zip(WEIGHTS, (m_conv_w_in, m_conv_b_in, m_conv_w_dw, m_conv_b_dw, m_conv_ln_g, m_conv_ln_b, m_conv_w_out,
                           m_conv_b_out, m_kv_w_k, m_kv_w_v, m_attn_w_q, m_attn_sinks, m_attn_w_o, m_mix_ln_g, m_mix_ln_b,
                           m_mlp_w_up, m_mlp_w_down, m_mlp_ln_g, m_mlp_ln_b, m_ple_w_proj, m_ple_w_gate)))
    v = dict(zip(WEIGHTS, (v_conv_w_in, v_conv_b_in, v_conv_w_dw, v_conv_b_dw, v_conv_ln_g, v_conv_ln_b, v_conv_w_out,
                           v_conv_b_out, v_kv_w_k, v_kv_w_v, v_attn_w_q, v_attn_sinks, v_attn_w_o, v_mix_ln_g, v_mix_ln_b,
                           v_mlp_w_up, v_mlp_w_down, v_mlp_ln_g, v_mlp_ln_b, v_ple_w_proj, v_ple_w_gate)))
    return _step(x, p, loss_target, w, m, v)
```

```python
import functools

import jax
import jax.numpy as jnp
from jax import lax
from jax.experimental import pallas as pl
from jax.experimental.pallas import tpu as pltpu

F32 = jnp.float32
BF16 = jnp.bfloat16
NS = 4
HEAD = 64
BLK = 128
ROPE = 16
ROPE_THETA = 500000.0
LN_EPS = 1e-5
NEG = -1e30
HALO = 32
ADAM_LR, ADAM_B1, ADAM_B2, ADAM_EPS, ADAM_WD, ADAM_STEP = 0.001, 0.9, 0.999, 1e-08, 0.01, 10
MESH = pl.DeviceIdType.MESH
ANY = pl.BlockSpec(memory_space=pl.ANY)
NT = (((1,), (1,)), ((), ()))
TN = (((0,), (0,)), ((), ()))


_FOLLOW = []


def _pc(body, name, grid, in_specs, out_specs, out_shape, scratch=(), sem=None, vmem=56, **kw):
    call = lambda fn, ins: pl.pallas_call(
        fn, name=name, grid=grid, in_specs=ins, out_specs=out_specs, out_shape=out_shape,
        scratch_shapes=list(scratch),
        compiler_params=pltpu.CompilerParams(dimension_semantics=sem, vmem_limit_bytes=vmem * 2 ** 20), **kw)
    if not _FOLLOW:
        return call(body, in_specs)
    extra = list(_FOLLOW)
    _FOLLOW.clear()
    n_in = len(in_specs)

    def ordered(*refs):
        return body(*refs[:n_in], *refs[n_in + len(extra):])

    run = call(ordered, list(in_specs) + [ANY] * len(extra))
    return lambda *args: run(*args, *extra)


def _rows(tm, n):
    return pl.BlockSpec((tm, n), lambda i: (i, 0))


def _const(shape):
    return pl.BlockSpec(shape, lambda *_: (0,) * len(shape))


def _wspec(w):
    buf, off, rows = w
    assert off % rows == 0
    return pl.BlockSpec((NS, rows, buf.shape[2]), lambda *_: (0, off // rows, 0))


def _rows_joined(w_ref):
    n, r, c = w_ref.shape
    return w_ref[...].reshape(n * r, c)


def _sds(shape, dtype):
    return jax.ShapeDtypeStruct(shape, dtype)


def _tile(t):
    return min(256, t)


def _sigmoid(x):
    return 1.0 / (1.0 + jnp.exp(-x))


def _ln_stats(w):
    mu = jnp.mean(w, axis=-1, keepdims=True)
    xc = w - mu
    var = jnp.mean(xc * xc, axis=-1, keepdims=True)
    rstd = lax.rsqrt(var + LN_EPS)
    return xc * rstd, rstd


def _ln_bwd(dy, w, g):
    xhat, rstd = _ln_stats(w)
    dxhat = dy * g
    m1 = jnp.mean(dxhat, axis=-1, keepdims=True)
    m2 = jnp.mean(dxhat * xhat, axis=-1, keepdims=True)
    dw = rstd * (dxhat - m1 - xhat * m2)
    return dw, jnp.sum(dy * xhat, axis=0, keepdims=True), jnp.sum(dy, axis=0, keepdims=True)


def _acc_rows(ref, val, first):
    @pl.when(first)
    def _():
        ref[...] = val

    @pl.when(jnp.logical_not(first))
    def _():
        ref[...] += val


def conv_in_fwd(xb, w_in, b_in):
    T, D = xb.shape
    nw = w_in[0].shape[2]
    tm = _tile(T)

    def body(x_ref, w_ref, b_ref, h_ref):
        x = x_ref[...]
        for j in range(NS):
            sl = slice(j * nw, (j + 1) * nw)
            h_ref[:, sl] = (jnp.dot(x, w_ref[j], preferred_element_type=F32) + b_ref[:, sl]).astype(BF16)

    return _pc(body, "conv_in_fwd", (T // tm,), [_rows(tm, D), _wspec(w_in), _const((1, NS * nw))],
               _rows(tm, NS * nw), _sds((T, NS * nw), BF16), sem=("parallel",))(xb, w_in[0], b_in)


CONV_ROWS = 16


def _phases(scr, sh):
    n = scr.shape[0] - 8
    for b in range(1, 8):
        sh[b - 1, 0:n, :] = scr[b:b + n, :]


def _spread(w_ref, wb, taps):
    for j in range(taps):
        wb[j] = jnp.broadcast_to(w_ref[j:j + 1, :], wb.shape[1:])


def _tap(scr, sh, o, n):
    b = o % 8
    return scr[o:o + n, :] if b == 0 else sh[b - 1, o - b:o - b + n, :]


def dwconv_fwd(h, w_dw, b_dw, ln_g, ln_b, taps):
    T = h.shape[0]
    C = h.shape[1] // 2
    tq = _tile(T)
    nh = tq // HALO
    off = HALO - (taps - 1)

    def body(a_ref, g_ref, ap_ref, gp_ref, w_ref, bdw_ref, lg_ref, lb_ref, cv_ref, s_ref, scr, sh, wb):
        i = pl.program_id(0)
        scr[HALO:HALO + tq, :] = a_ref[...].astype(F32) * _sigmoid(g_ref[...].astype(F32))
        up = ap_ref[...].astype(F32) * _sigmoid(gp_ref[...].astype(F32))
        scr[0:HALO, :] = jnp.where(i > 0, up, 0.0)
        _phases(scr, sh)
        _spread(w_ref, wb, taps)
        bias = jnp.broadcast_to(bdw_ref[...], (8, C))
        for r in range(tq // CONV_ROWS):
            accs = [bias] * (CONV_ROWS // 8)
            for j in range(taps):
                wj = wb[j]
                accs = [acc + wj * _tap(scr, sh, off + j + r * CONV_ROWS + 8 * k, 8) for k, acc in enumerate(accs)]
            for k, acc in enumerate(accs):
                cv_ref[r * CONV_ROWS + 8 * k:r * CONV_ROWS + 8 * k + 8, :] = acc
        xhat, _ = _ln_stats(cv_ref[...])
        ln = xhat * lg_ref[...] + lb_ref[...]
        s_ref[...] = (ln * _sigmoid(ln)).astype(BF16)

    prev = lambda col: pl.BlockSpec((HALO, C), lambda i: (jnp.maximum(i * nh - 1, 0), col))
    cur = lambda col: pl.BlockSpec((tq, C), lambda i: (i, col))
    return _pc(body, "dwconv_fwd", (T // tq,),
               [cur(0), cur(1), prev(0), prev(1), _const((HALO, C)), _const((1, C)), _const((1, C)), _const((1, C))],
               [_rows(tq, C), _rows(tq, C)], [_sds((T, C), F32), _sds((T, C), BF16)],
               scratch=[pltpu.VMEM((HALO + tq, C), F32), pltpu.VMEM((7, HALO + tq, C), F32), pltpu.VMEM((taps, 8, C), F32)],
               sem=("parallel",))(h, h, h, h, w_dw, b_dw, ln_g, ln_b)


def mm_res_ln(a, w, res, g, b, alpha, bias, name):
    T, K = a.shape
    ks = K // NS
    D = res.shape[1]
    tm = _tile(T)

    def body(*refs):
        a_ref, w_ref, res_ref, g_ref, b_ref = refs[:5]
        n = 5
        if bias is not None:
            bias_ref = refs[5]
            n = 6
        pre_ref, xo_ref, xb_ref = refs[n:n + 3]
        acc = jnp.dot(a_ref[...], _rows_joined(w_ref), preferred_element_type=F32)
        if bias is not None:
            acc = acc + bias_ref[...]
        pre = alpha * res_ref[...] + acc
        xhat, _ = _ln_stats(pre)
        xo = xhat * g_ref[...] + b_ref[...]
        pre_ref[...] = pre
        xo_ref[...] = xo
        xb_ref[...] = xo.astype(BF16)

    ins = [_rows(tm, K), _wspec(w), _rows(tm, D), _const((1, D)), _const((1, D))]
    args = [a, w[0], res, g, b]
    if bias is not None:
        ins.append(_const((1, D)))
        args.append(bias)
    return _pc(body, name, (T // tm,), ins, [_rows(tm, D)] * 3, [_sds((T, D), F32), _sds((T, D), F32), _sds((T, D), BF16)],
               sem=("parallel",))(*args)


def mlp_up_fwd(xb, w_up, name):
    T, D = xb.shape
    fs = w_up[0].shape[2]
    tm = _tile(T)

    def body(x_ref, w_ref, r_ref):
        x = x_ref[...]
        for j in range(NS):
            m = jnp.maximum(jnp.dot(x, w_ref[j], preferred_element_type=F32), 0.0)
            r_ref[:, j * fs:(j + 1) * fs] = (m * m).astype(BF16)

    return _pc(body, name, (T // tm,), [_rows(tm, D), _wspec(w_up)], _rows(tm, NS * fs), _sds((T, NS * fs), BF16),
               sem=("parallel",))(xb, w_up[0])


def ple_fwd(x, xb, p, layer, w_proj, w_gate, target, name):
    T, D = x.shape
    P = p.shape[2]
    ds = D // NS
    tm = _tile(T)
    last = target is not None

    def body(*refs):
        x_ref, xb_ref, p_ref, wp_ref, wg_ref = refs[:5]
        n = 5
        if last:
            t_ref = refs[5]
            n = 6
        o_ref, o2_ref, pp_ref, gl_ref = refs[n:n + 4]
        gl = jnp.dot(xb_ref[...], _rows_joined(wg_ref), preferred_element_type=F32)
        gl_ref[...] = gl.astype(BF16)
        sg = _sigmoid(gl)
        pb = p_ref[...].astype(BF16)
        sq = jnp.zeros((1, 1), F32)
        for j in range(NS):
            sl = slice(j * ds, (j + 1) * ds)
            pp = jnp.dot(pb, wp_ref[j], preferred_element_type=F32)
            pp_ref[:, sl] = pp.astype(BF16)
            out = x_ref[:, sl] + pp * sg[:, sl]
            if last:
                err = out - t_ref[:, sl]
                o_ref[:, sl] = err * (1.0 / D)
                e2 = jnp.sum(err * err, axis=0, keepdims=True)
                sq = sq + jnp.sum(e2, axis=1, keepdims=True)
            else:
                o_ref[:, sl] = out
                o2_ref[:, sl] = out.astype(BF16)
        if last:
            _acc_rows(o2_ref, jnp.broadcast_to(sq * (0.5 / D), (8, 128)), pl.program_id(0) == 0)

    ins = [_rows(tm, D), _rows(tm, D), pl.BlockSpec((None, tm, P), lambda i: (layer, i, 0)), _wspec(w_proj), _wspec(w_gate)]
    args = [x, xb, p, w_proj[0], w_gate[0]]
    if last:
        ins.append(_rows(tm, D))
        args.append(target)
        outs = [_rows(tm, D), _const((8, 128)), _rows(tm, D), _rows(tm, D)]
        shapes = [_sds((T, D), F32), _sds((8, 128), F32), _sds((T, D), BF16), _sds((T, D), BF16)]
    else:
        outs = [_rows(tm, D)] * 4
        shapes = [_sds((T, D), F32), _sds((T, D), BF16), _sds((T, D), BF16), _sds((T, D), BF16)]
    return _pc(body, name, (T // tm,), ins, outs, shapes, sem=("arbitrary",) if last else ("parallel",))(*args)


def _rope(x, cs_ref, sign):
    c = cs_ref[0]
    s = cs_ref[1] * sign
    lane = lax.broadcasted_iota(jnp.int32, c.shape, 1)
    first = (lane % HEAD) < (ROPE // 2)
    outs = []
    for gq in range(x.shape[1] // 128):
        xg = x[:, gq * 128:(gq + 1) * 128]
        sw = jnp.where(first, pltpu.roll(xg, 128 - ROPE // 2, 1), pltpu.roll(xg, ROPE // 2, 1))
        outs.append(xg * c + sw * s)
    return outs


def qkv_fwd(xb, w_q, w_k, w_v, cs):
    T, D = xb.shape
    ds = D // NS
    HD, KVD = w_q[0].shape[2], w_k[0].shape[2]
    tm = _tile(T)
    scale = 1.0 / (HEAD ** 0.5)

    def body(x_ref, wq_ref, wk_ref, wv_ref, cs_ref, q_ref, k_ref, v_ref):
        def proj(w_ref):
            return jnp.dot(x_ref[...], _rows_joined(w_ref), preferred_element_type=F32)

        for gq, val in enumerate(_rope(proj(wq_ref), cs_ref, 1.0)):
            q_ref[:, gq * 128:(gq + 1) * 128] = (val * scale).astype(BF16)
        for gq, val in enumerate(_rope(proj(wk_ref), cs_ref, 1.0)):
            k_ref[:, gq * 128:(gq + 1) * 128] = val.astype(BF16)
        v_ref[...] = proj(wv_ref).astype(BF16)

    cs_spec = pl.BlockSpec((2, tm, 128), lambda i: (0, i, 0))
    return _pc(body, "qkv_fwd", (T // tm,), [_rows(tm, D), _wspec(w_q), _wspec(w_k), _wspec(w_v), cs_spec],
               [_rows(tm, HD), _rows(tm, KVD), _rows(tm, KVD)],
               [_sds((T, HD), BF16), _sds((T, KVD), BF16), _sds((T, KVD), BF16)], sem=("parallel",))(
                   xb, w_q[0], w_k[0], w_v[0], cs)


def _band_mask(n):
    row = lax.broadcasted_iota(jnp.int32, (BLK, 2 * BLK), 0)
    col = lax.broadcasted_iota(jnp.int32, (BLK, 2 * BLK), 1)
    return (col > row) & (col <= row + BLK) & ((col >= BLK) | (n > 0))


def _head(h):
    return slice(h * HEAD, (h + 1) * HEAD)


def _softmax_sink(s, sink):
    m = jnp.maximum(jnp.max(s, axis=-1, keepdims=True), sink)
    e = jnp.exp(s - m)
    es = jnp.exp(sink - m)
    den = jnp.sum(e, axis=-1, keepdims=True) + es
    return e / den, es / den


def attn_fwd(q, k, v, sinks):
    T, HD = q.shape
    KVD = k.shape[1]
    NKV = KVD // HEAD
    G = HD // KVD

    def body(s_ref, q_ref, kc_ref, kp_ref, vc_ref, vp_ref, o_ref):
        valid = _band_mask(pl.program_id(0))
        for kh in range(NKV):
            k2 = jnp.concatenate([kp_ref[:, _head(kh)], kc_ref[:, _head(kh)]], axis=0)
            v2 = jnp.concatenate([vp_ref[:, _head(kh)], vc_ref[:, _head(kh)]], axis=0)
            hs = [kh * G + gq for gq in range(G)]
            sc = [lax.dot_general(q_ref[:, _head(hh)], k2, NT, preferred_element_type=F32) for hh in hs]
            pb = [_softmax_sink(jnp.where(valid, s, NEG), s_ref[0, hh])[0].astype(BF16) for s, hh in zip(sc, hs)]
            for p, hh in zip(pb, hs):
                o_ref[:, _head(hh)] = jnp.dot(p, v2, preferred_element_type=F32).astype(BF16)

    cur = lambda n_: pl.BlockSpec((BLK, n_), lambda n: (n, 0))
    prev = lambda n_: pl.BlockSpec((BLK, n_), lambda n: (jnp.maximum(n - 1, 0), 0))
    return _pc(body, "attn_fwd", (T // BLK,),
               [pl.BlockSpec(memory_space=pltpu.SMEM), cur(HD), cur(KVD), prev(KVD), cur(KVD), prev(KVD)],
               cur(HD), _sds((T, HD), BF16), sem=("parallel",))(sinks, q, k, k, v, v)


def ple_bwd(dxo, pp, gl, w_gate, name):
    T, D = dxo.shape
    ds = D // NS
    tm = _tile(T)

    def body(d_ref, pp_ref, gl_ref, wg_ref, dpp_ref, dgl_ref, dx_ref):
        d = d_ref[...]
        sg = _sigmoid(gl_ref[...].astype(F32))
        dpp_ref[...] = (d * sg).astype(BF16)
        dgl = (d * pp_ref[...].astype(F32) * sg * (1.0 - sg)).astype(BF16)
        dgl_ref[...] = dgl
        dx_ref[...] = d + lax.dot_general(dgl, _rows_joined(wg_ref), NT, preferred_element_type=F32)

    return _pc(body, name, (T // tm,), [_rows(tm, D)] * 3 + [_wspec(w_gate)], [_rows(tm, D)] * 3,
               [_sds((T, D), BF16), _sds((T, D), BF16), _sds((T, D), F32)], sem=("parallel",))(dxo, pp, gl, w_gate[0])


def mlp_bwd1(dy, pre, g, r, w_down, name):
    T, D = dy.shape
    fs = w_down[2]
    tm = _tile(T)

    def body(dy_ref, pre_ref, g_ref, r_ref, w_ref, dw_ref, dwb_ref, dm_ref, dg_ref, db_ref):
        dw, dg, db = _ln_bwd(dy_ref[...], pre_ref[...], g_ref[...])
        first = pl.program_id(0) == 0
        _acc_rows(dg_ref, dg, first)
        _acc_rows(db_ref, db, first)
        dwb = dw.astype(BF16)
        dw_ref[...] = dw
        dwb_ref[...] = dwb
        for j in range(NS):
            sl = slice(j * fs, (j + 1) * fs)
            dr = lax.dot_general(dwb, w_ref[j], NT, preferred_element_type=F32)
            dm_ref[:, sl] = (dr * (2.0 * jnp.sqrt(r_ref[:, sl].astype(F32)))).astype(BF16)

    return _pc(body, name, (T // tm,), [_rows(tm, D), _rows(tm, D), _const((1, D)), _rows(tm, NS * fs), _wspec(w_down)],
               [_rows(tm, D), _rows(tm, D), _rows(tm, NS * fs), _const((1, D)), _const((1, D))],
               [_sds((T, D), F32), _sds((T, D), BF16), _sds((T, NS * fs), BF16), _sds((1, D), F32), _sds((1, D), F32)],
               sem=("arbitrary",))(dy, pre, g, r, w_down[0])


def mlp_bwd2(dpre, dm, w_up, alpha, pre_mix, g_mix, w_mix, name):
    T, D = dpre.shape
    fs = w_up[0].shape[2]
    ms = w_mix[2]
    tm = _tile(T)

    def body(dp_ref, dm_ref, wu_ref, pre_ref, g_ref, wm_ref, dw_ref, dwb_ref, do_ref, dg_ref, db_ref, dc_ref):
        dy = alpha * dp_ref[...]
        for j in range(NS):
            dy = dy + lax.dot_general(dm_ref[:, j * fs:(j + 1) * fs], wu_ref[j], NT, preferred_element_type=F32)
        dw, dg, db = _ln_bwd(dy, pre_ref[...], g_ref[...])
        first = pl.program_id(0) == 0
        _acc_rows(dg_ref, dg, first)
        _acc_rows(db_ref, db, first)
        _acc_rows(dc_ref, jnp.sum(dw, axis=0, keepdims=True), first)
        dwb = dw.astype(BF16)
        dw_ref[...] = dw
        dwb_ref[...] = dwb
        do_ref[...] = lax.dot_general(dwb, _rows_joined(wm_ref), NT, preferred_element_type=F32).astype(BF16)

    return _pc(body, name, (T // tm,),
               [_rows(tm, D), _rows(tm, NS * fs), _wspec(w_up), _rows(tm, D), _const((1, D)), _wspec(w_mix)],
               [_rows(tm, D), _rows(tm, D), _rows(tm, NS * ms), _const((1, D)), _const((1, D)), _const((1, D))],
               [_sds((T, D), F32), _sds((T, D), BF16), _sds((T, NS * ms), BF16)] + [_sds((1, D), F32)] * 3,
               sem=("arbitrary",))(dpre, dm, w_up[0], pre_mix, g_mix, w_mix[0])


def attn_bwd(q, k, v, do, sinks):
    T, HD = q.shape
    KVD = k.shape[1]
    NH, NKV = HD // HEAD, KVD // HEAD
    G = NH // NKV
    nb = T // BLK

    def body(s_ref, q_ref, do_ref, kc_ref, kp_ref, vc_ref, vp_ref, dq_ref, dk_ref, dv_ref, ds_ref, ck, cv):
        n = pl.program_id(0)

        @pl.when(n == 0)
        def _():
            ck[...] = jnp.zeros_like(ck)
            cv[...] = jnp.zeros_like(cv)
            ds_ref[...] = jnp.zeros_like(ds_ref)

        @pl.when(n < nb)
        def _():
            valid = _band_mask(n)
            for kh in range(NKV):
                kv = _head(kh)
                k2 = jnp.concatenate([kp_ref[:, kv], kc_ref[:, kv]], axis=0)
                v2 = jnp.concatenate([vp_ref[:, kv], vc_ref[:, kv]], axis=0)
                hs = [kh * G + gq for gq in range(G)]
                qs = [q_ref[:, _head(hh)] for hh in hs]
                dos = [do_ref[:, _head(hh)] for hh in hs]
                sc = [lax.dot_general(qh, k2, NT, preferred_element_type=F32) for qh in qs]
                dp = [lax.dot_general(doh, v2, NT, preferred_element_type=F32) for doh in dos]
                pr = [_softmax_sink(jnp.where(valid, s, NEG), s_ref[0, hh]) for s, hh in zip(sc, hs)]
                delta = [jnp.sum(p * d, axis=-1, keepdims=True) for (p, _), d in zip(pr, dp)]
                dsb = [(p * (d - dl)).astype(BF16) for (p, _), d, dl in zip(pr, dp, delta)]
                pb = [p.astype(BF16) for p, _ in pr]
                for (_, ps), dl, hh in zip(pr, delta, hs):
                    ds_ref[hh:hh + 1, :] += jnp.broadcast_to(-jnp.sum(ps * dl, axis=0, keepdims=True), (1, 128))
                for d, hh in zip(dsb, hs):
                    dq_ref[:, _head(hh)] = jnp.dot(d, k2, preferred_element_type=F32)
                dk2 = lax.dot_general(jnp.concatenate(dsb, axis=0), jnp.concatenate(qs, axis=0), TN,
                                      preferred_element_type=F32)
                dv2 = lax.dot_general(jnp.concatenate(pb, axis=0), jnp.concatenate(dos, axis=0), TN,
                                      preferred_element_type=F32)
                dk_ref[:, kv] = ck[:, kv] + dk2[0:BLK]
                dv_ref[:, kv] = cv[:, kv] + dv2[0:BLK]
                ck[:, kv] = dk2[BLK:2 * BLK]
                cv[:, kv] = dv2[BLK:2 * BLK]

        @pl.when(n == nb)
        def _():
            dk_ref[...] = ck[...]
            dv_ref[...] = cv[...]

    qcur = pl.BlockSpec((BLK, HD), lambda n: (jnp.minimum(n, nb - 1), 0))
    kcur = pl.BlockSpec((BLK, KVD), lambda n: (jnp.minimum(n, nb - 1), 0))
    kprev = pl.BlockSpec((BLK, KVD), lambda n: (jnp.maximum(n - 1, 0), 0))
    return _pc(body, "attn_bwd", (nb + 1,),
               [pl.BlockSpec(memory_space=pltpu.SMEM), qcur, qcur, kcur, kprev, kcur, kprev],
               [qcur, kprev, kprev, _const((NH, 128))],
               [_sds((T, HD), F32), _sds((T, KVD), F32), _sds((T, KVD), F32), _sds((NH, 128), F32)],
               scratch=[pltpu.VMEM((BLK, KVD), F32), pltpu.VMEM((BLK, KVD), F32)],
               sem=("arbitrary",))(sinks, q, do, k, k, v, v)


def qkv_bwd(dq, dk, dv, dpre_mix, w_q, w_k, w_v, cs, alpha):
    T, HD = dq.shape
    KVD = dk.shape[1]
    D = dpre_mix.shape[1]
    ds = D // NS
    tm = _tile(T)
    scale = 1.0 / (HEAD ** 0.5)

    def body(dq_ref, dk_ref, dv_ref, dp_ref, wq_ref, wk_ref, wv_ref, cs_ref, dqb_ref, dkb_ref, dvb_ref, dx_ref):
        for gq, val in enumerate(_rope(dq_ref[...], cs_ref, -1.0)):
            dqb_ref[:, gq * 128:(gq + 1) * 128] = (val * scale).astype(BF16)
        for gq, val in enumerate(_rope(dk_ref[...], cs_ref, -1.0)):
            dkb_ref[:, gq * 128:(gq + 1) * 128] = val.astype(BF16)
        dvb_ref[...] = dv_ref[...].astype(BF16)
        dqb, dkb, dvb = dqb_ref[...], dkb_ref[...], dvb_ref[...]
        dx_ref[...] = (alpha * dp_ref[...]
                       + lax.dot_general(dqb, _rows_joined(wq_ref), NT, preferred_element_type=F32)
                       + lax.dot_general(dkb, _rows_joined(wk_ref), NT, preferred_element_type=F32)
                       + lax.dot_general(dvb, _rows_joined(wv_ref), NT, preferred_element_type=F32))

    cs_spec = pl.BlockSpec((2, tm, 128), lambda i: (0, i, 0))
    return _pc(body, "qkv_bwd", (T // tm,),
               [_rows(tm, HD), _rows(tm, KVD), _rows(tm, KVD), _rows(tm, D), _wspec(w_q), _wspec(w_k), _wspec(w_v), cs_spec],
               [_rows(tm, HD), _rows(tm, KVD), _rows(tm, KVD), _rows(tm, D)],
               [_sds((T, HD), BF16), _sds((T, KVD), BF16), _sds((T, KVD), BF16), _sds((T, D), F32)],
               sem=("parallel",))(dq, dk, dv, dpre_mix, w_q[0], w_k[0], w_v[0], cs)


def conv_mid_bwd(ds, cv, ln_g, ln_b):
    T, C = cv.shape
    tm = _tile(T)

    def body(ds_ref, cv_ref, g_ref, b_ref, dcv_ref, dg_ref, db_ref, dc_ref):
        xhat, _ = _ln_stats(cv_ref[...])
        ln = xhat * g_ref[...] + b_ref[...]
        sg = _sigmoid(ln)
        dl = ds_ref[...].astype(F32) * (sg * (1.0 + ln * (1.0 - sg)))
        dcv, dg, db = _ln_bwd(dl, cv_ref[...], g_ref[...])
        first = pl.program_id(0) == 0
        _acc_rows(dg_ref, dg, first)
        _acc_rows(db_ref, db, first)
        _acc_rows(dc_ref, jnp.sum(dcv, axis=0, keepdims=True), first)
        dcv_ref[...] = dcv

    return _pc(body, "conv_mid_bwd", (T // tm,), [_rows(tm, C), _rows(tm, C), _const((1, C)), _const((1, C))],
               [_rows(tm, C), _const((1, C)), _const((1, C)), _const((1, C))],
               [_sds((T, C), F32)] + [_sds((1, C), F32)] * 3, sem=("arbitrary",))(ds, cv, ln_g, ln_b)


def dwconv_bwd(dcv, h, w_dw, taps):
    T, C = dcv.shape
    tq = _tile(T)
    nh = tq // HALO
    nblk = T // tq
    off = HALO - (taps - 1)

    def body(d_ref, dn_ref, a_ref, g_ref, ap_ref, gp_ref, w_ref, dh_ref, dw_ref, dbi_ref, su, sus, sd, sds, wb):
        i = pl.program_id(0)
        su[HALO:HALO + tq, :] = a_ref[...].astype(F32) * _sigmoid(g_ref[...].astype(F32))
        up = ap_ref[...].astype(F32) * _sigmoid(gp_ref[...].astype(F32))
        su[0:HALO, :] = jnp.where(i > 0, up, 0.0)
        sd[0:tq, :] = d_ref[...]
        sd[tq:tq + HALO, :] = jnp.where(i < nblk - 1, dn_ref[...], 0.0)
        _phases(su, sus)
        _phases(sd, sds)

        @pl.when(i == 0)
        def _():
            dw_ref[...] = jnp.zeros_like(dw_ref)

        for j in range(taps):
            dw_ref[j:j + 1, :] += jnp.sum(d_ref[...] * _tap(su, sus, off + j, tq), axis=0, keepdims=True)
        sa = jnp.zeros((1, C), F32)
        sb = jnp.zeros((1, C), F32)
        _spread(w_ref, wb, taps)
        for r in range(tq // CONV_ROWS):
            rows = slice(r * CONV_ROWS, (r + 1) * CONV_ROWS)
            dus = [wb[0] * _tap(sd, sds, taps - 1 + r * CONV_ROWS + 8 * k, 8) for k in range(CONV_ROWS // 8)]
            for j in range(1, taps):
                wj = wb[j]
                dus = [acc + wj * _tap(sd, sds, taps - 1 - j + r * CONV_ROWS + 8 * k, 8) for k, acc in enumerate(dus)]
            du = jnp.concatenate(dus, axis=0)
            a = a_ref[rows, :].astype(F32)
            sg = _sigmoid(g_ref[rows, :].astype(F32))
            da = du * sg
            dgt = du * a * sg * (1.0 - sg)
            dh_ref[rows, 0:C] = da.astype(BF16)
            dh_ref[rows, C:2 * C] = dgt.astype(BF16)
            sa = sa + jnp.sum(da, axis=0, keepdims=True)
            sb = sb + jnp.sum(dgt, axis=0, keepdims=True)
        first = i == 0
        _acc_rows(dbi_ref.at[:, 0:C], sa, first)
        _acc_rows(dbi_ref.at[:, C:2 * C], sb, first)

    prev = lambda col: pl.BlockSpec((HALO, C), lambda i: (jnp.maximum(i * nh - 1, 0), col))
    nxt = pl.BlockSpec((HALO, C), lambda i: (jnp.minimum((i + 1) * nh, T // HALO - 1), 0))
    cur = lambda col: pl.BlockSpec((tq, C), lambda i: (i, col))
    return _pc(body, "dwconv_bwd", (nblk,),
               [cur(0), nxt, cur(0), cur(1), prev(0), prev(1), _const((HALO, C))],
               [_rows(tq, 2 * C), _const((HALO, C)), _const((1, 2 * C))],
               [_sds((T, 2 * C), BF16), _sds((HALO, C), F32), _sds((1, 2 * C), F32)],
               scratch=[pltpu.VMEM((HALO + tq, C), F32), pltpu.VMEM((7, HALO + tq, C), F32),
                        pltpu.VMEM((HALO + tq, C), F32), pltpu.VMEM((7, HALO + tq, C), F32), pltpu.VMEM((taps, 8, C), F32)],
               sem=("arbitrary",))(dcv, dcv, h, h, h, h, w_dw)


def conv_in_bwd(dh, dpre_mix, w_in, alpha):
    T, D = dpre_mix.shape
    nw = w_in[0].shape[2]
    tm = _tile(T)

    def body(dh_ref, dp_ref, w_ref, dx_ref):
        acc = alpha * dp_ref[...]
        for j in range(NS):
            acc = acc + lax.dot_general(dh_ref[:, j * nw:(j + 1) * nw], w_ref[j], NT, preferred_element_type=F32)
        dx_ref[...] = acc

    return _pc(body, "conv_in_bwd", (T // tm,), [_rows(tm, NS * nw), _rows(tm, D), _wspec(w_in)], _rows(tm, D),
               _sds((T, D), F32), sem=("parallel",))(dh, dpre_mix, w_in[0])


def wgrad(a, b, row_sharded, name):
    T, Ka = a.shape
    Nb = b.shape[1]
    tt = min(1024, T)
    nt = T // tt
    ka, tn = min(Ka, 1024), min(Nb, 1024)
    if row_sharded:
        sr = Ka // NS
        spb = max(ka // sr, 1)
        out_shape = (NS, sr, Nb)
        out_spec = pl.BlockSpec((spb, ka // spb, tn), lambda i, j, t: (i, 0, j))
    else:
        sc = Nb // NS
        spb = max(tn // sc, 1)
        out_shape = (NS, Ka, sc)
        out_spec = pl.BlockSpec((spb, ka, tn // spb), lambda i, j, t: (j, i, 0))

    def body(a_ref, b_ref, o_ref, acc):
        t = pl.program_id(2)
        av = a_ref[...]
        if av.dtype != BF16:
            av = av.astype(BF16)
        d = lax.dot_general(av, b_ref[...], TN, preferred_element_type=F32)

        @pl.when(t == 0)
        def _():
            acc[...] = d

        @pl.when(t > 0)
        def _():
            acc[...] += d

        @pl.when(t == nt - 1)
        def _():
            for s in range(spb):
                if row_sharded:
                    o_ref[s] = acc[s * (ka // spb):(s + 1) * (ka // spb), :].astype(BF16)
                else:
                    o_ref[s] = acc[:, s * (tn // spb):(s + 1) * (tn // spb)].astype(BF16)

    return _pc(body, name, (Ka // ka, Nb // tn, nt),
               [pl.BlockSpec((tt, ka), lambda i, j, t: (t, i)), pl.BlockSpec((tt, tn), lambda i, j, t: (t, j))],
               out_spec, _sds(out_shape, BF16),
               scratch=[pltpu.VMEM((ka, tn), F32)], sem=("parallel", "parallel", "arbitrary"))(a, b)


def adamw(w, g, m, v, name):
    R, W = w.shape
    tr = R
    for cand in (512, 256, 128, 64, 32, 16, 8):
        if R % cand == 0:
            tr = cand
            break
    c1 = 1.0 - ADAM_B1 ** ADAM_STEP
    c2 = 1.0 - ADAM_B2 ** ADAM_STEP

    def body(w_ref, g_ref, m_ref, v_ref, d_ref, mo_ref, vo_ref):
        gv = g_ref[...]
        mn = ADAM_B1 * m_ref[...] + (1.0 - ADAM_B1) * gv
        vn = ADAM_B2 * v_ref[...] + (1.0 - ADAM_B2) * (gv * gv)
        mo_ref[...] = mn
        vo_ref[...] = vn
        d_ref[...] = -ADAM_LR * ((mn / c1) / (jnp.sqrt(vn / c2) + ADAM_EPS) + ADAM_WD * w_ref[...])

    return _pc(body, name, (R // tr,), [_rows(tr, W)] * 4, [_rows(tr, W)] * 3, [_sds((R, W), F32)] * 3,
               sem=("parallel",))(w, g, m, v)


def _rope_tables(T):
    pos = jnp.arange(T, dtype=F32)
    inv_freq = ROPE_THETA ** (-jnp.arange(0, ROPE, 2, dtype=F32) / ROPE)
    ang = pos[:, None] * inv_freq[None, :]
    cos, sin = jnp.cos(ang), jnp.sin(ang)
    pad = HEAD - ROPE
    c = jnp.concatenate([cos, cos, jnp.ones((T, pad), F32)], axis=1)
    s = jnp.concatenate([-sin, sin, jnp.zeros((T, pad), F32)], axis=1)
    return jnp.stack([jnp.tile(c, (1, 128 // HEAD)), jnp.tile(s, (1, 128 // HEAD))])


def _local_step(x, p, target, W, small, hook=None):
    if hook is None:
        hook = lambda stage, after, G, sg=None: None
    T, D = x.shape
    depth = small["mix_ln_g"].shape[0]
    alpha = float((2 * depth) ** 0.25)
    taps = small["taps"]
    row = lambda a, i: a[i:i + 1]
    cs = _rope_tables(T)

    x0b = x.astype(BF16)
    h = conv_in_fwd(x0b, W["conv_w_in"], small["conv_b_in"])
    cv, s = dwconv_fwd(h, small["conv_w_dw"], small["conv_b_dw"], small["conv_ln_g"], small["conv_ln_b"], taps)
    pre_mix0, x1, x1b = mm_res_ln(s, W["conv_w_out"], x, row(small["mix_ln_g"], 0), row(small["mix_ln_b"], 0), alpha,
                                  small["conv_b_out"], "conv_out_fwd")
    hook("weights1", x1b, None)
    r0 = mlp_up_fwd(x1b, W["mlp_w_up0"], "mlp_up_fwd0")
    pre_mlp0, x2, x2b = mm_res_ln(r0, W["mlp_w_down0"], x1, row(small["mlp_ln_g"], 0), row(small["mlp_ln_b"], 0), alpha,
                                  None, "mlp_down_fwd0")
    x3, x3b, pp0, gl0 = ple_fwd(x2, x2b, p, 0, W["ple_w_proj0"], W["ple_w_gate0"], None, "ple_fwd0")

    hook("weights2", x3b, None)
    q, k, v = qkv_fwd(x3b, W["attn_w_q"], W["kv_w_k"], W["kv_w_v"], cs)
    o = attn_fwd(q, k, v, small["attn_sinks"])
    pre_mix1, x4, x4b = mm_res_ln(o, W["attn_w_o"], x3, row(small["mix_ln_g"], 1), row(small["mix_ln_b"], 1), alpha,
                                  None, "attn_out_fwd")
    r1 = mlp_up_fwd(x4b, W["mlp_w_up1"], "mlp_up_fwd1")
    pre_mlp1, x5, x5b = mm_res_ln(r1, W["mlp_w_down1"], x4, row(small["mlp_ln_g"], 1), row(small["mlp_ln_b"], 1), alpha,
                                  None, "mlp_down_fwd1")
    dx6, loss, pp1, gl1 = ple_fwd(x5, x5b, p, 1, W["ple_w_proj1"], W["ple_w_gate1"], target, "ple_fwd1")

    G, sg = {}, {}
    dpp1, dgl1, dx5 = ple_bwd(dx6, pp1, gl1, W["ple_w_gate1"], "ple_bwd1")
    G["ple_w_proj1"] = wgrad(p[1], dpp1, False, "wg_ple_proj1")
    G["ple_w_gate1"] = wgrad(x5b, dgl1, True, "wg_ple_gate1")
    dpre_mlp1, dpre_mlp1b, dm1, g_mlp_g1, g_mlp_b1 = mlp_bwd1(dx5, pre_mlp1, row(small["mlp_ln_g"], 1), r1,
                                                              W["mlp_w_down1"], "mlp_bwd1_1")
    G["mlp_w_down1"] = wgrad(r1, dpre_mlp1b, True, "wg_mlp_down1")
    G["mlp_w_up1"] = wgrad(x4b, dm1, False, "wg_mlp_up1")
    dpre_mix1, dpre_mix1b, do, g_mix_g1, g_mix_b1, _ = mlp_bwd2(dpre_mlp1, dm1, W["mlp_w_up1"], alpha, pre_mix1,
                                                                row(small["mix_ln_g"], 1), W["attn_w_o"], "mlp_bwd2_1")
    G["attn_w_o"] = wgrad(o, dpre_mix1b, True, "wg_attn_o")
    dq, dk, dv, dsinks = attn_bwd(q, k, v, do, small["attn_sinks"])
    dqb, dkb, dvb, dx3 = qkv_bwd(dq, dk, dv, dpre_mix1,
                                 W["attn_w_q"], W["kv_w_k"], W["kv_w_v"], cs, alpha)
    G["attn_w_q"] = wgrad(x3b, dqb, True, "wg_attn_q")
    G["kv_w_k"] = wgrad(x3b, dkb, True, "wg_kv_k")
    G["kv_w_v"] = wgrad(x3b, dvb, True, "wg_kv_v")
    hook("grads2", None, G)

    dpp0, dgl0, dx2 = ple_bwd(dx3, pp0, gl0, W["ple_w_gate0"], "ple_bwd0")
    G["ple_w_proj0"] = wgrad(p[0], dpp0, False, "wg_ple_proj0")
    G["ple_w_gate0"] = wgrad(x2b, dgl0, True, "wg_ple_gate0")
    dpre_mlp0, dpre_mlp0b, dm0, g_mlp_g0, g_mlp_b0 = mlp_bwd1(dx2, pre_mlp0, row(small["mlp_ln_g"], 0), r0,
                                                              W["mlp_w_down0"], "mlp_bwd1_0")
    G["mlp_w_down0"] = wgrad(r0, dpre_mlp0b, True, "wg_mlp_down0")
    G["mlp_w_up0"] = wgrad(x1b, dm0, False, "wg_mlp_up0")
    dpre_mix0, dpre_mix0b, dsw, g_mix_g0, g_mix_b0, g_b_out = mlp_bwd2(dpre_mlp0, dm0, W["mlp_w_up0"], alpha, pre_mix0,
                                                                      row(small["mix_ln_g"], 0), W["conv_w_out"],
                                                                      "mlp_bwd2_0")
    hook("grads1", None, G)
    dcv, g_cln_g, g_cln_b, g_b_dw = conv_mid_bwd(dsw, cv, small["conv_ln_g"], small["conv_ln_b"])
    dh, g_w_dw, g_b_in = dwconv_bwd(dcv, h, small["conv_w_dw"], taps)
    G["conv_w_out"] = wgrad(s, dpre_mix0b, True, "wg_conv_out")
    G["conv_w_in"] = wgrad(x0b, dh, False, "wg_conv_in")

    sg["conv_b_in"] = g_b_in
    sg["conv_w_dw"] = g_w_dw
    sg["conv_b_dw"], sg["conv_ln_g"], sg["conv_ln_b"], sg["conv_b_out"] = g_b_dw, g_cln_g, g_cln_b, g_b_out
    sg["mix_ln_g"] = [g_mix_g0, g_mix_g1]
    sg["mix_ln_b"] = [g_mix_b0, g_mix_b1]
    sg["mlp_ln_g"] = [g_mlp_g0, g_mlp_g1]
    sg["mlp_ln_b"] = [g_mlp_b0, g_mlp_b1]
    sg["attn_sinks"] = dsinks[:, 0][None, :]
    sg["loss"] = loss
    hook("grads0", None, G, sg)
    grad_x = conv_in_bwd(dh, dpre_mix0, W["conv_w_in"], alpha)
    return loss, grad_x, G, sg


BUFFERS = (("b0", ("conv_w_in",)), ("a0", ("conv_w_out",)),
           ("a1", ("mlp_w_up0", "mlp_w_down0", "ple_w_gate0")), ("c1", ("ple_w_proj0",)),
           ("a2", ("mlp_w_up1", "mlp_w_down1", "ple_w_gate1", "attn_w_q", "attn_w_o")),
           ("c2", ("kv_w_k", "kv_w_v", "ple_w_proj1")))
GROUPS = (("b0", "a0"), ("a1", "c1"), ("a2", "c2"))
ROW_SHARDED = {"mlp_w_down0", "mlp_w_down1", "ple_w_gate0", "ple_w_gate1", "conv_w_out", "attn_w_q", "attn_w_o", "kv_w_k",
               "kv_w_v"}


def _split_layers(weights):
    out = {"conv_w_in": weights["conv_w_in"][0], "conv_w_out": weights["conv_w_out"][0],
           "attn_w_q": weights["attn_w_q"][0], "attn_w_o": weights["attn_w_o"][0],
           "kv_w_k": weights["kv_w_k"], "kv_w_v": weights["kv_w_v"]}
    for n in ("mlp_w_up", "mlp_w_down", "ple_w_proj", "ple_w_gate"):
        for i in range(weights[n].shape[0]):
            out[n + str(i)] = weights[n][i]
    return out


def _layout(shards):
    lay = {}
    for key, names in BUFFERS:
        off, rows = 0, []
        for n in names:
            rows.append((n, off, shards[n].shape[0]))
            off += shards[n].shape[0]
        lay[key] = rows
    return lay


def _place():
    return lax.axis_index("x"), lax.axis_index("y"), lax.axis_index("c")


def _flip(v, f):
    return (v + f) % 2 if f else v


CHIP_FLIPS = ((1, 0), (0, 1), (1, 1))


HBM = pl.BlockSpec(memory_space=pltpu.HBM)
SEM = pl.BlockSpec(memory_space=pltpu.SEMAPHORE)
EFFECT = pltpu.SideEffectType.DATAFLOW_SIDE_EFFECTING


def _half(ref, rows, c):
    return ref.at[pl.ds(pl.multiple_of(c * (rows // 2), 16), rows // 2), :]


def _gather_copies(refs, shapes, whole, send, recv):
    x, y, c = _place()
    me = 2 * x + y
    na = len(refs)
    cps = []
    for d, (fx, fy) in enumerate(CHIP_FLIPS):
        to = (_flip(x, fx), _flip(y, fy), c)
        for k in range(na):
            mine = refs[k].at[me] if k >= na - whole else _half(refs[k].at[me], shapes[k][1], c)
            cps.append(pltpu.make_async_remote_copy(mine, mine, send.at[d * na + k], recv.at[d * na + k], device_id=to,
                                                    device_id_type=MESH))
    return cps


def gather_start(bufs, whole, after, name):
    na = len(bufs)
    shapes = [b.shape for b in bufs]
    nsem = len(CHIP_FLIPS) * na

    def body(*refs):
        ins = refs[:na]
        send, recv = refs[-(na + 3)], refs[-(na + 2)]
        token = refs[-1]
        for cp in _gather_copies(ins, shapes, whole, send, recv):
            cp.start()
        token[...] = jnp.zeros_like(token)

    args = [pltpu.with_memory_space_constraint(b, pltpu.HBM) for b in bufs]
    ins = [HBM] * na
    if after is not None:
        args.append(after)
        ins.append(ANY)
    return pl.pallas_call(
        body, name=name, in_specs=ins,
        out_specs=[SEM, SEM] + [HBM] * na + [pl.BlockSpec(memory_space=pltpu.VMEM)],
        out_shape=[pltpu.SemaphoreType.DMA((nsem,)), pltpu.SemaphoreType.DMA((nsem,))]
        + [pltpu.HBM(b.shape, b.dtype) for b in bufs] + [_sds((8, 128), F32)],
        input_output_aliases={k: k + 2 for k in range(na)},
        compiler_params=pltpu.CompilerParams(has_side_effects=EFFECT))(*args)


def gather_wait(send, recv, bufs, whole, after, name):
    na = len(bufs)
    shapes = [b.shape for b in bufs]

    def body(*refs):
        ins = refs[:na]
        send_ref, recv_ref = refs[na], refs[na + 1]
        for cp in _gather_copies(ins, shapes, whole, send_ref, recv_ref):
            cp.wait_send()
            cp.wait_recv()

    return pl.pallas_call(
        body, name=name, in_specs=[HBM] * na + [SEM, SEM, ANY], out_specs=[HBM] * na,
        out_shape=[pltpu.HBM(b.shape, b.dtype) for b in bufs], input_output_aliases={k: k for k in range(na)},
        compiler_params=pltpu.CompilerParams(has_side_effects=EFFECT))(*bufs, send, recv, after)


def sibling_forward(bufs, name):
    nb = len(bufs)

    def body(*refs):
        outs = refs[nb:2 * nb]
        send, recv = refs[2 * nb:]
        x, y, c = _place()
        cps = []
        for d, (fx, fy) in enumerate(CHIP_FLIPS):
            frm = 2 * _flip(x, fx) + _flip(y, fy)
            for k in range(nb):
                theirs = _half(outs[k].at[frm], bufs[k].shape[1], c)
                cps.append(pltpu.make_async_remote_copy(theirs, theirs, send.at[d * nb + k], recv.at[d * nb + k],
                                                        device_id=(x, y, 1 - c), device_id_type=MESH))
        for cp in cps:
            cp.start()
        for cp in cps:
            cp.wait()

    nsem = len(CHIP_FLIPS) * nb
    return pl.pallas_call(
        body, name=name, in_specs=[ANY] * nb, out_specs=[ANY] * nb, out_shape=[_sds(b.shape, b.dtype) for b in bufs],
        input_output_aliases={k: k for k in range(nb)},
        scratch_shapes=[pltpu.SemaphoreType.DMA((nsem,)), pltpu.SemaphoreType.DMA((nsem,))])(*bufs)


def pack_rows(pieces, rows, width, name):
    def body(*refs):
        o_ref = refs[-1]
        o_ref[...] = jnp.zeros_like(o_ref)
        for ref, (a, off) in zip(refs[:-1], pieces):
            o_ref[off:off + a.shape[0], 0:a.shape[1]] = ref[...]

    return pl.pallas_call(body, name=name, out_shape=_sds((rows, width), F32))(*[a for a, _ in pieces])


def sibling_exchange(grads, small, name):
    nb = len(grads)
    ns = 0 if small is None else 1

    def body(*refs):
        ins, outs = refs[:nb + ns], refs[nb + ns:2 * (nb + ns)]
        send, recv, lsem = refs[2 * (nb + ns):]
        x, y, c = _place()
        me = 4 * x + 2 * y + c
        cps = []
        for k in range(nb):
            hrows = grads[k].shape[1] // 2
            src = ins[k].at[:, pl.ds(pl.multiple_of((1 - c) * hrows, 16), hrows), :]
            cps.append(pltpu.make_async_remote_copy(src, outs[k], send.at[k], recv.at[k], device_id=(x, y, 1 - c),
                                                    device_id_type=MESH))
        if ns:
            n = nb
            for fx in (0, 1):
                for fy in (0, 1):
                    for fc in (0, 1):
                        if fx or fy or fc:
                            cps.append(pltpu.make_async_remote_copy(
                                ins[nb], outs[nb].at[me], send.at[n], recv.at[n],
                                device_id=(_flip(x, fx), _flip(y, fy), _flip(c, fc)), device_id_type=MESH))
                            n += 1
            own = pltpu.make_async_copy(ins[nb], outs[nb].at[me], lsem)
            own.start()
        for cp in cps:
            cp.start()
        for cp in cps:
            cp.wait()
        if ns:
            own.wait()

    shapes = [_sds((NS, g.shape[1] // 2, g.shape[2]), g.dtype) for g in grads]
    args = list(grads)
    if ns:
        shapes.append(_sds((8,) + small.shape, small.dtype))
        args.append(small)
    nsem = nb + 7 * ns
    return pl.pallas_call(
        body, name=name, in_specs=[ANY] * (nb + ns), out_specs=[ANY] * (nb + ns), out_shape=shapes,
        scratch_shapes=[pltpu.SemaphoreType.DMA((nsem,)), pltpu.SemaphoreType.DMA((nsem,)), pltpu.SemaphoreType.DMA(())])(*args)


def _chip_copies(sums, lands, send, recv):
    x, y, c = _place()
    nb = len(sums)
    cps = []
    for d, (fx, fy) in enumerate(CHIP_FLIPS):
        tx, ty = _flip(x, fx), _flip(y, fy)
        for k in range(nb):
            cps.append(pltpu.make_async_remote_copy(sums[k].at[2 * tx + ty], lands[k].at[d], send.at[d * nb + k],
                                                    recv.at[d * nb + k], device_id=(tx, ty, c), device_id_type=MESH))
    return cps


def chip_start(sums, name):
    nb = len(sums)
    nsem = len(CHIP_FLIPS) * nb

    def body(*refs):
        ins, lands = refs[:nb], refs[nb:2 * nb]
        send, recv = refs[2 * nb], refs[2 * nb + 1]
        for cp in _chip_copies(ins, lands, send, recv):
            cp.start()
        refs[-1][...] = jnp.zeros_like(refs[-1])

    zones = [lax.empty((len(CHIP_FLIPS),) + s.shape[1:], s.dtype) for s in sums]
    args = [pltpu.with_memory_space_constraint(a, pltpu.HBM) for a in list(sums) + zones]
    return pl.pallas_call(
        body, name=name, in_specs=[HBM] * (2 * nb),
        out_specs=[SEM, SEM] + [HBM] * (2 * nb) + [pl.BlockSpec(memory_space=pltpu.VMEM)],
        out_shape=[pltpu.SemaphoreType.DMA((nsem,)), pltpu.SemaphoreType.DMA((nsem,))]
        + [pltpu.HBM(a.shape, a.dtype) for a in list(sums) + zones] + [_sds((8, 128), F32)],
        input_output_aliases={k: k + 2 for k in range(2 * nb)},
        compiler_params=pltpu.CompilerParams(has_side_effects=EFFECT))(*args)


def chip_wait(send, recv, sums, lands, after, name):
    nb = len(sums)

    def body(*refs):
        ins, zones = refs[:nb], refs[nb:2 * nb]
        for cp in _chip_copies(ins, zones, refs[2 * nb], refs[2 * nb + 1]):
            cp.wait_send()
            cp.wait_recv()

    arrs = list(sums) + list(lands)
    return pl.pallas_call(
        body, name=name, in_specs=[HBM] * (2 * nb) + [SEM, SEM, ANY], out_specs=[HBM] * (2 * nb),
        out_shape=[pltpu.HBM(a.shape, a.dtype) for a in arrs], input_output_aliases={k: k for k in range(2 * nb)},
        compiler_params=pltpu.CompilerParams(has_side_effects=EFFECT))(*arrs, send, recv, after)


def sibling_share(halves):
    nb = len(halves)

    def body(*refs):
        outs = refs[nb:2 * nb]
        send, recv = refs[2 * nb:]
        x, y, c = _place()
        cps = []
        for k in range(nb):
            hrows = halves[k].shape[0] // 2
            mine = outs[k].at[pl.ds(pl.multiple_of(c * hrows, 8), hrows), :]
            cps.append(pltpu.make_async_remote_copy(mine, mine, send.at[k], recv.at[k], device_id=(x, y, 1 - c),
                                                    device_id_type=MESH))
        for cp in cps:
            cp.start()
        for cp in cps:
            cp.wait()

    return pl.pallas_call(
        body, name="sibling_share", in_specs=[ANY] * nb, out_specs=[ANY] * nb,
        out_shape=[_sds(h.shape, h.dtype) for h in halves], input_output_aliases={k: k for k in range(nb)},
        scratch_shapes=[pltpu.SemaphoreType.DMA((nb,)), pltpu.SemaphoreType.DMA((nb,))])(*halves)


def _row_tile(rows):
    for cand in (512, 384, 256, 128, 64, 32, 16):
        if rows % cand == 0:
            return cand
    return rows


def pair_sum(g, r, idx, name):
    _, hrows, W = r.shape
    tr = _row_tile(hrows)
    nrb = hrows // tr

    def body(idx_ref, g_ref, r_ref, o_ref):
        o_ref[...] = (g_ref[...].astype(F32) + r_ref[...].astype(F32)).astype(BF16)

    gs = pltpu.PrefetchScalarGridSpec(
        num_scalar_prefetch=1, grid=(NS, nrb),
        in_specs=[pl.BlockSpec((None, tr, W), lambda j, i, s: (j, s[1] * nrb + i, 0)),
                  pl.BlockSpec((None, tr, W), lambda j, i, s: (j, i, 0))],
        out_specs=pl.BlockSpec((None, tr, W), lambda j, i, s: (j, i, 0)))
    return pl.pallas_call(body, name=name, grid_spec=gs, out_shape=_sds(r.shape, BF16),
                          compiler_params=pltpu.CompilerParams(dimension_semantics=("parallel", "parallel")))(idx, g, r)


def chip_sum(s, t, idx, name):
    _, hrows, W = s.shape
    tr = _row_tile(hrows)
    nrb = hrows // tr

    def body(idx_ref, s_ref, t_ref, o_ref):
        acc = s_ref[...].astype(F32)
        for d in range(t.shape[0]):
            acc = acc + t_ref[d].astype(F32)
        o_ref[...] = acc

    gs = pltpu.PrefetchScalarGridSpec(
        num_scalar_prefetch=1, grid=(nrb,),
        in_specs=[pl.BlockSpec((None, tr, W), lambda i, sc: (sc[0], i, 0)),
                  pl.BlockSpec((t.shape[0], tr, W), lambda i, sc: (0, i, 0))],
        out_specs=pl.BlockSpec((tr, W), lambda i, sc: (sc[1] * nrb + i, 0)))
    return pl.pallas_call(body, name=name, grid_spec=gs, out_shape=_sds((2 * hrows, W), F32),
                          compiler_params=pltpu.CompilerParams(dimension_semantics=("parallel",)))(idx, s, t)


def small_sum(packs):
    n, R, W = packs.shape

    def body(p_ref, o_ref):
        acc = p_ref[0]
        for d in range(1, n):
            acc = acc + p_ref[d]
        o_ref[...] = acc

    return pl.pallas_call(body, name="small_sum", out_shape=_sds((R, W), F32))(packs)


WEIGHTS = ["conv_w_in", "conv_b_in", "conv_w_dw", "conv_b_dw", "conv_ln_g", "conv_ln_b", "conv_w_out", "conv_b_out", "kv_w_k",
           "kv_w_v", "attn_w_q", "attn_sinks", "attn_w_o", "mix_ln_g", "mix_ln_b", "mlp_w_up", "mlp_w_down", "mlp_ln_g",
           "mlp_ln_b", "ple_w_proj", "ple_w_gate"]
BIG = ["conv_w_in", "conv_w_out", "kv_w_k", "kv_w_v", "attn_w_q", "attn_w_o", "mlp_w_up", "mlp_w_down", "ple_w_proj",
       "ple_w_gate"]
SMALL = [n for n in WEIGHTS if n not in BIG]
SHARDED_SMALL = ["conv_b_in", "conv_w_dw", "conv_b_dw", "conv_ln_g", "conv_ln_b", "conv_b_out"]


FLAT = 1024


def _flat_tiles(a):
    f = a.reshape(-1)
    pad = (-f.shape[0]) % FLAT
    return jnp.pad(f, (0, pad)) if pad else f


def _step(x, p, target, w, m, v):
    D = x.shape[-1]
    ds = D // NS
    xq, yq, cq = _place()
    chip = 2 * xq + yq
    idx = jnp.stack([chip, cq]).astype(jnp.int32)

    shards = _split_layers(w)
    lay = _layout(shards)
    taps = w["conv_w_dw"].shape[1]
    small_loc = pack_rows([(w["conv_w_dw"][0], 0), (w["conv_b_dw"], HALO), (w["conv_ln_g"], HALO + 1), (w["conv_ln_b"], HALO + 2),
                           (w["conv_b_out"], HALO + 3), (w["conv_b_in"].reshape(2, ds), HALO + 4)], HALO + 8, ds, "pack_small")
    slot = lambda a: lax.dynamic_update_slice(lax.empty((NS,) + a.shape, a.dtype), a[None], (chip, 0, 0))
    started, token = [], None
    for gi, keys in enumerate(GROUPS):
        bufs = [slot(jnp.concatenate([shards[n].astype(BF16) for n, _, _ in lay[key]], axis=0)) for key in keys]
        if gi == 0:
            bufs.append(slot(small_loc))
        send, recv, *thru, token = gather_start(bufs, 1 if gi == 0 else 0, token, "gather_start%d" % gi)
        started.append((send, recv, thru))
    W = {}

    def arrive(gi, after):
        send, recv, thru = started[gi]
        whole = 1 if gi == 0 else 0
        got = gather_wait(send, recv, thru, whole, after, "gather_wait%d" % gi)
        nk = len(GROUPS[gi])
        for key, buf in zip(GROUPS[gi], sibling_forward(got[:nk], "sibling_forward%d" % gi)):
            for n, off, rows in lay[key]:
                W[n] = (buf, off, rows)
        return got[nk:]

    gs, = arrive(0, token)
    across = lambda rows: gs[:, rows, :].transpose(1, 0, 2).reshape(rows.stop - rows.start, D)
    small = {"taps": taps, "conv_w_dw": across(slice(0, HALO)), "conv_b_dw": across(slice(HALO, HALO + 1)),
             "conv_ln_g": across(slice(HALO + 1, HALO + 2)), "conv_ln_b": across(slice(HALO + 2, HALO + 3)),
             "conv_b_out": across(slice(HALO + 3, HALO + 4)), "conv_b_in": gs[:, HALO + 4:HALO + 6, :].reshape(1, 2 * D),
             "attn_sinks": w["attn_sinks"], "mix_ln_g": w["mix_ln_g"], "mix_ln_b": w["mix_ln_b"],
             "mlp_ln_g": w["mlp_ln_g"], "mlp_ln_b": w["mlp_ln_b"]}

    reducing = {}

    def reduce_start(gi, G, pack):
        keys = GROUPS[gi]
        parts = [jnp.concatenate([G[n] for n, _, _ in lay[key]], axis=1) for key in keys]
        got = sibling_exchange(parts, pack, "sibling_exchange%d" % gi)
        sums = [pair_sum(g, r, idx, "pair_sum_" + key) for g, r, key in zip(parts, got, keys)]
        send, recv, *thru, token = chip_start(sums, "chip_start%d" % gi)
        reducing[gi] = (send, recv, thru[:len(keys)], thru[len(keys):])
        _FOLLOW.append(token)
        return got[len(keys):]

    def small_pack(sg):
        pieces = [(sg["conv_b_in"].reshape(2, D), 0), (sg["conv_w_dw"], 2)]
        r0 = 2 + HALO
        for i, n in enumerate(("conv_b_dw", "conv_ln_g", "conv_ln_b", "conv_b_out")):
            pieces.append((sg[n], r0 + i))
        r0 += 4
        for i, n in enumerate(("mix_ln_g", "mix_ln_b", "mlp_ln_g", "mlp_ln_b")):
            pieces += [(sg[n][0], r0 + 2 * i), (sg[n][1], r0 + 2 * i + 1)]
        pieces += [(sg["attn_sinks"], r0 + 8), (sg["loss"][0:1], r0 + 9)]
        return pack_rows(pieces, r0 + 10, D, "pack_small_grads")

    def hook(stage, after, G, sg=None):
        if stage == "weights1":
            arrive(1, after)
        elif stage == "weights2":
            arrive(2, after)
        elif stage == "grads2":
            reduce_start(2, G, None)
        elif stage == "grads1":
            reduce_start(1, G, None)
        elif stage == "grads0":
            reducing["packs"], = reduce_start(0, G, small_pack(sg))

    loss, grad_x, G, sg = _local_step(x[0], p[:, 0], target[0], W, small, hook)
    _FOLLOW.clear()
    nsink = w["attn_sinks"].shape[1]
    tot = small_sum(reducing["packs"])

    halves = {}
    for gi in (2, 1, 0):
        send, recv, sums, lands = reducing[gi]
        done = chip_wait(send, recv, sums, lands, grad_x, "chip_wait%d" % gi)
        nk = len(GROUPS[gi])
        for key, s_, t_ in zip(GROUPS[gi], done[:nk], done[nk:]):
            halves[key] = chip_sum(s_, t_, idx, "chip_sum_" + key)
    order = [key for key, _ in BUFFERS]
    full = sibling_share([halves[key] for key in order])

    grads = {}
    for key, buf in zip(order, full):
        for n, off, rows in lay[key]:
            grads[n] = buf[off:off + rows]
    for n in ("mlp_w_up", "mlp_w_down", "ple_w_proj", "ple_w_gate"):
        grads[n] = jnp.stack([grads.pop(n + str(i)) for i in range(w[n].shape[0])])
    for n in ("conv_w_in", "conv_w_out", "attn_w_q", "attn_w_o"):
        grads[n] = grads[n][None]
    cols = lambda rows: lax.dynamic_slice(rows, (0, chip * ds), (rows.shape[0], ds))
    grads["conv_b_in"] = lax.dynamic_slice(tot[0:2].reshape(1, 2 * D), (0, chip * 2 * ds), (1, 2 * ds))
    grads["conv_w_dw"] = cols(tot[2:2 + taps])[None]
    r0 = 2 + HALO
    for i, n in enumerate(("conv_b_dw", "conv_ln_g", "conv_ln_b", "conv_b_out")):
        grads[n] = cols(tot[r0 + i:r0 + i + 1])
    r0 += 4
    for i, n in enumerate(("mix_ln_g", "mix_ln_b", "mlp_ln_g", "mlp_ln_b")):
        grads[n] = tot[r0 + 2 * i:r0 + 2 * i + 2]
    grads["attn_sinks"] = tot[r0 + 8:r0 + 9, 0:nsink]

    delta, new_m, new_v = {}, {}, {}
    for n in BIG:
        shp = w[n].shape
        two = lambda a: a.reshape(-1, shp[-1])
        d_, m_, v_ = adamw(two(w[n]), two(grads[n]), two(m[n]), two(v[n]), "adamw_" + n)
        delta[n], new_m[n], new_v[n] = d_.reshape(shp), m_.reshape(shp), v_.reshape(shp)
    flat = lambda t: jnp.concatenate([_flat_tiles(t[n]).reshape(-1, 128) for n in SMALL], axis=0)
    d_, m_, v_ = adamw(flat(w), flat(grads), flat(m), flat(v), "adamw_small")
    pos = 0
    for n in SMALL:
        size = w[n].size
        take = lambda a: a.reshape(-1)[pos:pos + size].reshape(w[n].shape)
        delta[n], new_m[n], new_v[n] = take(d_), take(m_), take(v_)
        pos += size + (-size) % FLAT

    total = tot[r0 + 9, 0]
    return (total, grad_x[None], *[grads[n] for n in WEIGHTS], *[delta[n] for n in WEIGHTS], *[new_m[n] for n in WEIGHTS],
            *[new_v[n] for n in WEIGHTS])


def kernel(x, p, conv_w_in, conv_b_in, conv_w_dw, conv_b_dw, conv_ln_g, conv_ln_b, conv_w_out, conv_b_out, kv_w_k, kv_w_v, attn_w_q, attn_sinks, attn_w_o, mix_ln_g, mix_ln_b, mlp_w_up, mlp_w_down, mlp_ln_g, mlp_ln_b, ple_w_proj, ple_w_gate, loss_target, m_conv_w_in, m_conv_b_in, m_conv_w_dw, m_conv_b_dw, m_conv_ln_g, m_conv_ln_b, m_conv_w_out, m_conv_b_out, m_kv_w_k, m_kv_w_v, m_attn_w_q, m_attn_sinks, m_attn_w_o, m_mix_ln_g, m_mix_ln_b, m_mlp_w_up, m_mlp_w_down, m_mlp_ln_g, m_mlp_ln_b, m_ple_w_proj, m_ple_w_gate, v_conv_w_in, v_conv_b_in, v_conv_w_dw, v_conv_b_dw, v_conv_ln_g, v_conv_ln_b, v_conv_w_out, v_conv_b_out, v_kv_w_k, v_kv_w_v, v_attn_w_q, v_attn_sinks, v_attn_w_o, v_mix_ln_g, v_mix_ln_b, v_mlp_w_up, v_mlp_w_down, v_mlp_ln_g, v_mlp_ln_b, v_ple_w_proj, v_ple_w_gate):
    w = dict(zip(WEIGHTS, (conv_w_in, conv_b_in, conv_w_dw, conv_b_dw, conv_ln_g, conv_ln_b, conv_w_out, conv_b_out, kv_w_k,
                           kv_w_v, attn_w_q, attn_sinks, attn_w_o, mix_ln_g, mix_ln_b, mlp_w_up, mlp_w_down, mlp_ln_g, mlp_ln_b,
                           ple_w_proj, ple_w_gate)))
    m = dict(zip(WEIGHTS, (m_conv_w_in, m_conv_b_in, m_conv_w_dw, m_conv_b_dw, m_conv_ln_g, m_conv_ln_b, m_conv_w_out,
                           m_conv_b_out, m_kv_w_k, m_kv_w_v, m_attn_w_q, m_attn_sinks, m_attn_w_o, m_mix_ln_g, m_mix_ln_b,
                           m_mlp_w_up, m_mlp_w_down, m_mlp_ln_g, m_mlp_ln_b, m_ple_w_proj, m_ple_w_gate)))
    v = dict(zip(WEIGHTS, (v_conv_w_in, v_conv_b_in, v_conv_w_dw, v_conv_b_dw, v_conv_ln_g, v_conv_ln_b, v_conv_w_out,
                           v_conv_b_out, v_kv_w_k, v_kv_w_v, v_attn_w_q, v_attn_sinks, v_attn_w_o, v_mix_ln_g, v_mix_ln_b,
                           v_mlp_w_up, v_mlp_w_down, v_mlp_ln_g, v_mlp_ln_b, v_ple_w_proj, v_ple_w_gate)))
    return _step(x, p, loss_target, w, m, v)
```

```python
import functools

import jax
import jax.numpy as jnp
from jax import lax
from jax.experimental import pallas as pl
from jax.experimental.pallas import tpu as pltpu

F32 = jnp.float32
BF16 = jnp.bfloat16
NS = 4
HEAD = 64
BLK = 128
ROPE = 16
ROPE_THETA = 500000.0
LN_EPS = 1e-5
NEG = -1e30
HALO = 32
ADAM_LR, ADAM_B1, ADAM_B2, ADAM_EPS, ADAM_WD, ADAM_STEP = 0.001, 0.9, 0.999, 1e-08, 0.01, 10
MESH = pl.DeviceIdType.MESH
ANY = pl.BlockSpec(memory_space=pl.ANY)
NT = (((1,), (1,)), ((), ()))
TN = (((0,), (0,)), ((), ()))


_FOLLOW = []


def _pc(body, name, grid, in_specs, out_specs, out_shape, scratch=(), sem=None, vmem=56, **kw):
    call = lambda fn, ins: pl.pallas_call(
        fn, name=name, grid=grid, in_specs=ins, out_specs=out_specs, out_shape=out_shape,
        scratch_shapes=list(scratch),
        compiler_params=pltpu.CompilerParams(dimension_semantics=sem, vmem_limit_bytes=vmem * 2 ** 20), **kw)
    if not _FOLLOW:
        return call(body, in_specs)
    extra = list(_FOLLOW)
    _FOLLOW.clear()
    n_in = len(in_specs)

    def ordered(*refs):
        return body(*refs[:n_in], *refs[n_in + len(extra):])

    run = call(ordered, list(in_specs) + [ANY] * len(extra))
    return lambda *args: run(*args, *extra)


def _rows(tm, n):
    return pl.BlockSpec((tm, n), lambda i: (i, 0))


def _const(shape):
    return pl.BlockSpec(shape, lambda *_: (0,) * len(shape))


def _wspec(w):
    buf, off, rows = w
    assert off % rows == 0
    return pl.BlockSpec((NS, rows, buf.shape[2]), lambda *_: (0, off // rows, 0))


def _rows_joined(w_ref):
    n, r, c = w_ref.shape
    return w_ref[...].reshape(n * r, c)


def _sds(shape, dtype):
    return jax.ShapeDtypeStruct(shape, dtype)


def _tile(t):
    return min(256, t)


def _sigmoid(x):
    return 1.0 / (1.0 + jnp.exp(-x))


def _ln_stats(w):
    mu = jnp.mean(w, axis=-1, keepdims=True)
    xc = w - mu
    var = jnp.mean(xc * xc, axis=-1, keepdims=True)
    rstd = lax.rsqrt(var + LN_EPS)
    return xc * rstd, rstd


def _ln_bwd(dy, w, g):
    xhat, rstd = _ln_stats(w)
    dxhat = dy * g
    m1 = jnp.mean(dxhat, axis=-1, keepdims=True)
    m2 = jnp.mean(dxhat * xhat, axis=-1, keepdims=True)
    dw = rstd * (dxhat - m1 - xhat * m2)
    return dw, jnp.sum(dy * xhat, axis=0, keepdims=True), jnp.sum(dy, axis=0, keepdims=True)


def _acc_rows(ref, val, first):
    @pl.when(first)
    def _():
        ref[...] = val

    @pl.when(jnp.logical_not(first))
    def _():
        ref[...] += val


def conv_in_fwd(xb, w_in, b_in):
    T, D = xb.shape
    nw = w_in[0].shape[2]
    tm = _tile(T)

    def body(x_ref, w_ref, b_ref, h_ref):
        x = x_ref[...]
        for j in range(NS):
            sl = slice(j * nw, (j + 1) * nw)
            h_ref[:, sl] = (jnp.dot(x, w_ref[j], preferred_element_type=F32) + b_ref[:, sl]).astype(BF16)

    return _pc(body, "conv_in_fwd", (T // tm,), [_rows(tm, D), _wspec(w_in), _const((1, NS * nw))],
               _rows(tm, NS * nw), _sds((T, NS * nw), BF16), sem=("parallel",))(xb, w_in[0], b_in)


CONV_ROWS = 16


def _phases(scr, sh):
    n = scr.shape[0] - 8
    for b in range(1, 8):
        sh[b - 1, 0:n, :] = scr[b:b + n, :]


def _spread(w_ref, wb, taps):
    for j in range(taps):
        wb[j] = jnp.broadcast_to(w_ref[j:j + 1, :], wb.shape[1:])


def _tap(scr, sh, o, n):
    b = o % 8
    return scr[o:o + n, :] if b == 0 else sh[b - 1, o - b:o - b + n, :]


def dwconv_fwd(h, w_dw, b_dw, ln_g, ln_b, taps):
    T = h.shape[0]
    C = h.shape[1] // 2
    tq = _tile(T)
    nh = tq // HALO
    off = HALO - (taps - 1)

    def body(a_ref, g_ref, ap_ref, gp_ref, w_ref, bdw_ref, lg_ref, lb_ref, cv_ref, s_ref, scr, sh, wb):
        i = pl.program_id(0)
        scr[HALO:HALO + tq, :] = a_ref[...].astype(F32) * _sigmoid(g_ref[...].astype(F32))
        up = ap_ref[...].astype(F32) * _sigmoid(gp_ref[...].astype(F32))
        scr[0:HALO, :] = jnp.where(i > 0, up, 0.0)
        _phases(scr, sh)
        _spread(w_ref, wb, taps)
        bias = jnp.broadcast_to(bdw_ref[...], (8, C))
        for r in range(tq // CONV_ROWS):
            accs = [bias] * (CONV_ROWS // 8)
            for j in range(taps):
                wj = wb[j]
                accs = [acc + wj * _tap(scr, sh, off + j + r * CONV_ROWS + 8 * k, 8) for k, acc in enumerate(accs)]
            for k, acc in enumerate(accs):
                cv_ref[r * CONV_ROWS + 8 * k:r * CONV_ROWS + 8 * k + 8, :] = acc
        xhat, _ = _ln_stats(cv_ref[...])
        ln = xhat * lg_ref[...] + lb_ref[...]
        s_ref[...] = (ln * _sigmoid(ln)).astype(BF16)

    prev = lambda col: pl.BlockSpec((HALO, C), lambda i: (jnp.maximum(i * nh - 1, 0), col))
    cur = lambda col: pl.BlockSpec((tq, C), lambda i: (i, col))
    return _pc(body, "dwconv_fwd", (T // tq,),
               [cur(0), cur(1), prev(0), prev(1), _const((HALO, C)), _const((1, C)), _const((1, C)), _const((1, C))],
               [_rows(tq, C), _rows(tq, C)], [_sds((T, C), F32), _sds((T, C), BF16)],
               scratch=[pltpu.VMEM((HALO + tq, C), F32), pltpu.VMEM((7, HALO + tq, C), F32), pltpu.VMEM((taps, 8, C), F32)],
               sem=("parallel",))(h, h, h, h, w_dw, b_dw, ln_g, ln_b)


def mm_res_ln(a, w, res, g, b, alpha, bias, name):
    T, K = a.shape
    ks = K // NS
    D = res.shape[1]
    tm = _tile(T)

    def body(*refs):
        a_ref, w_ref, res_ref, g_ref, b_ref = refs[:5]
        n = 5
        if bias is not None:
            bias_ref = refs[5]
            n = 6
        pre_ref, xo_ref, xb_ref = refs[n:n + 3]
        acc = jnp.dot(a_ref[...], _rows_joined(w_ref), preferred_element_type=F32)
        if bias is not None:
            acc = acc + bias_ref[...]
        pre = alpha * res_ref[...] + acc
        xhat, _ = _ln_stats(pre)
        xo = xhat * g_ref[...] + b_ref[...]
        pre_ref[...] = pre
        xo_ref[...] = xo
        xb_ref[...] = xo.astype(BF16)

    ins = [_rows(tm, K), _wspec(w), _rows(tm, D), _const((1, D)), _const((1, D))]
    args = [a, w[0], res, g, b]
    if bias is not None:
        ins.append(_const((1, D)))
        args.append(bias)
    return _pc(body, name, (T // tm,), ins, [_rows(tm, D)] * 3, [_sds((T, D), F32), _sds((T, D), F32), _sds((T, D), BF16)],
               sem=("parallel",))(*args)


def mlp_up_fwd(xb, w_up, name):
    T, D = xb.shape
    fs = w_up[0].shape[2]
    tm = _tile(T)

    def body(x_ref, w_ref, r_ref):
        x = x_ref[...]
        for j in range(NS):
            m = jnp.maximum(jnp.dot(x, w_ref[j], preferred_element_type=F32), 0.0)
            r_ref[:, j * fs:(j + 1) * fs] = (m * m).astype(BF16)

    return _pc(body, name, (T // tm,), [_rows(tm, D), _wspec(w_up)], _rows(tm, NS * fs), _sds((T, NS * fs), BF16),
               sem=("parallel",))(xb, w_up[0])


def ple_fwd(x, xb, p, layer, w_proj, w_gate, target, name):
    T, D = x.shape
    P = p.shape[2]
    ds = D // NS
    tm = _tile(T)
    last = target is not None

    def body(*refs):
        x_ref, xb_ref, p_ref, wp_ref, wg_ref = refs[:5]
        n = 5
        if last:
            t_ref = refs[5]
            n = 6
        o_ref, o2_ref, pp_ref, gl_ref = refs[n:n + 4]
        gl = jnp.dot(xb_ref[...], _rows_joined(wg_ref), preferred_element_type=F32)
        gl_ref[...] = gl.astype(BF16)
        sg = _sigmoid(gl)
        pb = p_ref[...].astype(BF16)
        sq = jnp.zeros((1, 1), F32)
        for j in range(NS):
            sl = slice(j * ds, (j + 1) * ds)
            pp = jnp.dot(pb, wp_ref[j], preferred_element_type=F32)
            pp_ref[:, sl] = pp.astype(BF16)
            out = x_ref[:, sl] + pp * sg[:, sl]
            if last:
                err = out - t_ref[:, sl]
                o_ref[:, sl] = err * (1.0 / D)
                e2 = jnp.sum(err * err, axis=0, keepdims=True)
                sq = sq + jnp.sum(e2, axis=1, keepdims=True)
            else:
                o_ref[:, sl] = out
                o2_ref[:, sl] = out.astype(BF16)
        if last:
            _acc_rows(o2_ref, jnp.broadcast_to(sq * (0.5 / D), (8, 128)), pl.program_id(0) == 0)

    ins = [_rows(tm, D), _rows(tm, D), pl.BlockSpec((None, tm, P), lambda i: (layer, i, 0)), _wspec(w_proj), _wspec(w_gate)]
    args = [x, xb, p, w_proj[0], w_gate[0]]
    if last:
        ins.append(_rows(tm, D))
        args.append(target)
        outs = [_rows(tm, D), _const((8, 128)), _rows(tm, D), _rows(tm, D)]
        shapes = [_sds((T, D), F32), _sds((8, 128), F32), _sds((T, D), BF16), _sds((T, D), BF16)]
    else:
        outs = [_rows(tm, D)] * 4
        shapes = [_sds((T, D), F32), _sds((T, D), BF16), _sds((T, D), BF16), _sds((T, D), BF16)]
    return _pc(body, name, (T // tm,), ins, outs, shapes, sem=("arbitrary",) if last else ("parallel",))(*args)


def _rope(x, cs_ref, sign):
    c = cs_ref[0]
    s = cs_ref[1] * sign
    lane = lax.broadcasted_iota(jnp.int32, c.shape, 1)
    first = (lane % HEAD) < (ROPE // 2)
    outs = []
    for gq in range(x.shape[1] // 128):
        xg = x[:, gq * 128:(gq + 1) * 128]
        sw = jnp.where(first, pltpu.roll(xg, 128 - ROPE // 2, 1), pltpu.roll(xg, ROPE // 2, 1))
        outs.append(xg * c + sw * s)
    return outs


def qkv_fwd(xb, w_q, w_k, w_v, cs):
    T, D = xb.shape
    ds = D // NS
    HD, KVD = w_q[0].shape[2], w_k[0].shape[2]
    tm = _tile(T)
    scale = 1.0 / (HEAD ** 0.5)

    def body(x_ref, wq_ref, wk_ref, wv_ref, cs_ref, q_ref, k_ref, v_ref):
        def proj(w_ref):
            return jnp.dot(x_ref[...], _rows_joined(w_ref), preferred_element_type=F32)

        for gq, val in enumerate(_rope(proj(wq_ref), cs_ref, 1.0)):
            q_ref[:, gq * 128:(gq + 1) * 128] = (val * scale).astype(BF16)
        for gq, val in enumerate(_rope(proj(wk_ref), cs_ref, 1.0)):
            k_ref[:, gq * 128:(gq + 1) * 128] = val.astype(BF16)
        v_ref[...] = proj(wv_ref).astype(BF16)

    cs_spec = pl.BlockSpec((2, tm, 128), lambda i: (0, i, 0))
    return _pc(body, "qkv_fwd", (T // tm,), [_rows(tm, D), _wspec(w_q), _wspec(w_k), _wspec(w_v), cs_spec],
               [_rows(tm, HD), _rows(tm, KVD), _rows(tm, KVD)],
               [_sds((T, HD), BF16), _sds((T, KVD), BF16), _sds((T, KVD), BF16)], sem=("parallel",))(
                   xb, w_q[0], w_k[0], w_v[0], cs)


def _band_mask(n):
    row = lax.broadcasted_iota(jnp.int32, (BLK, 2 * BLK), 0)
    col = lax.broadcasted_iota(jnp.int32, (BLK, 2 * BLK), 1)
    return (col > row) & (col <= row + BLK) & ((col >= BLK) | (n > 0))


def _head(h):
    return slice(h * HEAD, (h + 1) * HEAD)


def _softmax_sink(s, sink):
    m = jnp.maximum(jnp.max(s, axis=-1, keepdims=True), sink)
    e = jnp.exp(s - m)
    es = jnp.exp(sink - m)
    den = jnp.sum(e, axis=-1, keepdims=True) + es
    return e / den, es / den


def attn_fwd(q, k, v, sinks):
    T, HD = q.shape
    KVD = k.shape[1]
    NKV = KVD // HEAD
    G = HD // KVD

    def body(s_ref, q_ref, kc_ref, kp_ref, vc_ref, vp_ref, o_ref):
        valid = _band_mask(pl.program_id(0))
        for kh in range(NKV):
            k2 = jnp.concatenate([kp_ref[:, _head(kh)], kc_ref[:, _head(kh)]], axis=0)
            v2 = jnp.concatenate([vp_ref[:, _head(kh)], vc_ref[:, _head(kh)]], axis=0)
            hs = [kh * G + gq for gq in range(G)]
            sc = [lax.dot_general(q_ref[:, _head(hh)], k2, NT, preferred_element_type=F32) for hh in hs]
            pb = [_softmax_sink(jnp.where(valid, s, NEG), s_ref[0, hh])[0].astype(BF16) for s, hh in zip(sc, hs)]
            for p, hh in zip(pb, hs):
                o_ref[:, _head(hh)] = jnp.dot(p, v2, preferred_element_type=F32).astype(BF16)

    cur = lambda n_: pl.BlockSpec((BLK, n_), lambda n: (n, 0))
    prev = lambda n_: pl.BlockSpec((BLK, n_), lambda n: (jnp.maximum(n - 1, 0), 0))
    return _pc(body, "attn_fwd", (T // BLK,),
               [pl.BlockSpec(memory_space=pltpu.SMEM), cur(HD), cur(KVD), prev(KVD), cur(KVD), prev(KVD)],
               cur(HD), _sds((T, HD), BF16), sem=("parallel",))(sinks, q, k, k, v, v)


def ple_bwd(dxo, pp, gl, w_gate, name):
    T, D = dxo.shape
    ds = D // NS
    tm = _tile(T)

    def body(d_ref, pp_ref, gl_ref, wg_ref, dpp_ref, dgl_ref, dx_ref):
        d = d_ref[...]
        sg = _sigmoid(gl_ref[...].astype(F32))
        dpp_ref[...] = (d * sg).astype(BF16)
        dgl = (d * pp_ref[...].astype(F32) * sg * (1.0 - sg)).astype(BF16)
        dgl_ref[...] = dgl
        dx_ref[...] = d + lax.dot_general(dgl, _rows_joined(wg_ref), NT, preferred_element_type=F32)

    return _pc(body, name, (T // tm,), [_rows(tm, D)] * 3 + [_wspec(w_gate)], [_rows(tm, D)] * 3,
               [_sds((T, D), BF16), _sds((T, D), BF16), _sds((T, D), F32)], sem=("parallel",))(dxo, pp, gl, w_gate[0])


def mlp_bwd1(dy, pre, g, r, w_down, name):
    T, D = dy.shape
    fs = w_down[2]
    tm = _tile(T)

    def body(dy_ref, pre_ref, g_ref, r_ref, w_ref, dw_ref, dwb_ref, dm_ref, dg_ref, db_ref):
        dw, dg, db = _ln_bwd(dy_ref[...], pre_ref[...], g_ref[...])
        first = pl.program_id(0) == 0
        _acc_rows(dg_ref, dg, first)
        _acc_rows(db_ref, db, first)
        dwb = dw.astype(BF16)
        dw_ref[...] = dw
        dwb_ref[...] = dwb
        for j in range(NS):
            sl = slice(j * fs, (j + 1) * fs)
            dr = lax.dot_general(dwb, w_ref[j], NT, preferred_element_type=F32)
            dm_ref[:, sl] = (dr * (2.0 * jnp.sqrt(r_ref[:, sl].astype(F32)))).astype(BF16)

    return _pc(body, name, (T // tm,), [_rows(tm, D), _rows(tm, D), _const((1, D)), _rows(tm, NS * fs), _wspec(w_down)],
               [_rows(tm, D), _rows(tm, D), _rows(tm, NS * fs), _const((1, D)), _const((1, D))],
               [_sds((T, D), F32), _sds((T, D), BF16), _sds((T, NS * fs), BF16), _sds((1, D), F32), _sds((1, D), F32)],
               sem=("arbitrary",))(dy, pre, g, r, w_down[0])


def mlp_bwd2(dpre, dm, w_up, alpha, pre_mix, g_mix, w_mix, name):
    T, D = dpre.shape
    fs = w_up[0].shape[2]
    ms = w_mix[2]
    tm = _tile(T)

    def body(dp_ref, dm_ref, wu_ref, pre_ref, g_ref, wm_ref, dw_ref, dwb_ref, do_ref, dg_ref, db_ref, dc_ref):
        dy = alpha * dp_ref[...]
        for j in range(NS):
            dy = dy + lax.dot_general(dm_ref[:, j * fs:(j + 1) * fs], wu_ref[j], NT, preferred_element_type=F32)
        dw, dg, db = _ln_bwd(dy, pre_ref[...], g_ref[...])
        first = pl.program_id(0) == 0
        _acc_rows(dg_ref, dg, first)
        _acc_rows(db_ref, db, first)
        _acc_rows(dc_ref, jnp.sum(dw, axis=0, keepdims=True), first)
        dwb = dw.astype(BF16)
        dw_ref[...] = dw
        dwb_ref[...] = dwb
        do_ref[...] = lax.dot_general(dwb, _rows_joined(wm_ref), NT, preferred_element_type=F32).astype(BF16)

    return _pc(body, name, (T // tm,),
               [_rows(tm, D), _rows(tm, NS * fs), _wspec(w_up), _rows(tm, D), _const((1, D)), _wspec(w_mix)],
               [_rows(tm, D), _rows(tm, D), _rows(tm, NS * ms), _const((1, D)), _const((1, D)), _const((1, D))],
               [_sds((T, D), F32), _sds((T, D), BF16), _sds((T, NS * ms), BF16)] + [_sds((1, D), F32)] * 3,
               sem=("arbitrary",))(dpre, dm, w_up[0], pre_mix, g_mix, w_mix[0])


def attn_bwd(q, k, v, do, sinks):
    T, HD = q.shape
    KVD = k.shape[1]
    NH, NKV = HD // HEAD, KVD // HEAD
    G = NH // NKV
    nb = T // BLK

    def body(s_ref, q_ref, do_ref, kc_ref, kp_ref, vc_ref, vp_ref, dq_ref, dk_ref, dv_ref, ds_ref, ck, cv):
        n = pl.program_id(0)

        @pl.when(n == 0)
        def _():
            ck[...] = jnp.zeros_like(ck)
            cv[...] = jnp.zeros_like(cv)
            ds_ref[...] = jnp.zeros_like(ds_ref)

        @pl.when(n < nb)
        def _():
            valid = _band_mask(n)
            for kh in range(NKV):
                kv = _head(kh)
                k2 = jnp.concatenate([kp_ref[:, kv], kc_ref[:, kv]], axis=0)
                v2 = jnp.concatenate([vp_ref[:, kv], vc_ref[:, kv]], axis=0)
                hs = [kh * G + gq for gq in range(G)]
                qs = [q_ref[:, _head(hh)] for hh in hs]
                dos = [do_ref[:, _head(hh)] for hh in hs]
                sc = [lax.dot_general(qh, k2, NT, preferred_element_type=F32) for qh in qs]
                dp = [lax.dot_general(doh, v2, NT, preferred_element_type=F32) for doh in dos]
                pr = [_softmax_sink(jnp.where(valid, s, NEG), s_ref[0, hh]) for s, hh in zip(sc, hs)]
                delta = [jnp.sum(p * d, axis=-1, keepdims=True) for (p, _), d in zip(pr, dp)]
                dsb = [(p * (d - dl)).astype(BF16) for (p, _), d, dl in zip(pr, dp, delta)]
                pb = [p.astype(BF16) for p, _ in pr]
                for (_, ps), dl, hh in zip(pr, delta, hs):
                    ds_ref[hh:hh + 1, :] += jnp.broadcast_to(-jnp.sum(ps * dl, axis=0, keepdims=True), (1, 128))
                for d, hh in zip(dsb, hs):
                    dq_ref[:, _head(hh)] = jnp.dot(d, k2, preferred_element_type=F32)
                dk2 = lax.dot_general(jnp.concatenate(dsb, axis=0), jnp.concatenate(qs, axis=0), TN,
                                      preferred_element_type=F32)
                dv2 = lax.dot_general(jnp.concatenate(pb, axis=0), jnp.concatenate(dos, axis=0), TN,
                                      preferred_element_type=F32)
                dk_ref[:, kv] = ck[:, kv] + dk2[0:BLK]
                dv_ref[:, kv] = cv[:, kv] + dv2[0:BLK]
                ck[:, kv] = dk2[BLK:2 * BLK]
                cv[:, kv] = dv2[BLK:2 * BLK]

        @pl.when(n == nb)
        def _():
            dk_ref[...] = ck[...]
            dv_ref[...] = cv[...]

    qcur = pl.BlockSpec((BLK, HD), lambda n: (jnp.minimum(n, nb - 1), 0))
    kcur = pl.BlockSpec((BLK, KVD), lambda n: (jnp.minimum(n, nb - 1), 0))
    kprev = pl.BlockSpec((BLK, KVD), lambda n: (jnp.maximum(n - 1, 0), 0))
    return _pc(body, "attn_bwd", (nb + 1,),
               [pl.BlockSpec(memory_space=pltpu.SMEM), qcur, qcur, kcur, kprev, kcur, kprev],
               [qcur, kprev, kprev, _const((NH, 128))],
               [_sds((T, HD), F32), _sds((T, KVD), F32), _sds((T, KVD), F32), _sds((NH, 128), F32)],
               scratch=[pltpu.VMEM((BLK, KVD), F32), pltpu.VMEM((BLK, KVD), F32)],
               sem=("arbitrary",))(sinks, q, do, k, k, v, v)


def qkv_bwd(dq, dk, dv, dpre_mix, w_q, w_k, w_v, cs, alpha):
    T, HD = dq.shape
    KVD = dk.shape[1]
    D = dpre_mix.shape[1]
    ds = D // NS
    tm = _tile(T)
    scale = 1.0 / (HEAD ** 0.5)

    def body(dq_ref, dk_ref, dv_ref, dp_ref, wq_ref, wk_ref, wv_ref, cs_ref, dqb_ref, dkb_ref, dvb_ref, dx_ref):
        for gq, val in enumerate(_rope(dq_ref[...], cs_ref, -1.0)):
            dqb_ref[:, gq * 128:(gq + 1) * 128] = (val * scale).astype(BF16)
        for gq, val in enumerate(_rope(dk_ref[...], cs_ref, -1.0)):
            dkb_ref[:, gq * 128:(gq + 1) * 128] = val.astype(BF16)
        dvb_ref[...] = dv_ref[...].astype(BF16)
        dqb, dkb, dvb = dqb_ref[...], dkb_ref[...], dvb_ref[...]
        dx_ref[...] = (alpha * dp_ref[...]
                       + lax.dot_general(dqb, _rows_joined(wq_ref), NT, preferred_element_type=F32)
                       + lax.dot_general(dkb, _rows_joined(wk_ref), NT, preferred_element_type=F32)
                       + lax.dot_general(dvb, _rows_joined(wv_ref), NT, preferred_element_type=F32))

    cs_spec = pl.BlockSpec((2, tm, 128), lambda i: (0, i, 0))
    return _pc(body, "qkv_bwd", (T // tm,),
               [_rows(tm, HD), _rows(tm, KVD), _rows(tm, KVD), _rows(tm, D), _wspec(w_q), _wspec(w_k), _wspec(w_v), cs_spec],
               [_rows(tm, HD), _rows(tm, KVD), _rows(tm, KVD), _rows(tm, D)],
               [_sds((T, HD), BF16), _sds((T, KVD), BF16), _sds((T, KVD), BF16), _sds((T, D), F32)],
               sem=("parallel",))(dq, dk, dv, dpre_mix, w_q[0], w_k[0], w_v[0], cs)


def conv_mid_bwd(ds, cv, ln_g, ln_b):
    T, C = cv.shape
    tm = _tile(T)

    def body(ds_ref, cv_ref, g_ref, b_ref, dcv_ref, dg_ref, db_ref, dc_ref):
        xhat, _ = _ln_stats(cv_ref[...])
        ln = xhat * g_ref[...] + b_ref[...]
        sg = _sigmoid(ln)
        dl = ds_ref[...].astype(F32) * (sg * (1.0 + ln * (1.0 - sg)))
        dcv, dg, db = _ln_bwd(dl, cv_ref[...], g_ref[...])
        first = pl.program_id(0) == 0
        _acc_rows(dg_ref, dg, first)
        _acc_rows(db_ref, db, first)
        _acc_rows(dc_ref, jnp.sum(dcv, axis=0, keepdims=True), first)
        dcv_ref[...] = dcv

    return _pc(body, "conv_mid_bwd", (T // tm,), [_rows(tm, C), _rows(tm, C), _const((1, C)), _const((1, C))],
               [_rows(tm, C), _const((1, C)), _const((1, C)), _const((1, C))],
               [_sds((T, C), F32)] + [_sds((1, C), F32)] * 3, sem=("arbitrary",))(ds, cv, ln_g, ln_b)


def dwconv_bwd(dcv, h, w_dw, taps):
    T, C = dcv.shape
    tq = _tile(T)
    nh = tq // HALO
    nblk = T // tq
    off = HALO - (taps - 1)

    def body(d_ref, dn_ref, a_ref, g_ref, ap_ref, gp_ref, w_ref, dh_ref, dw_ref, dbi_ref, su, sus, sd, sds, wb):
        i = pl.program_id(0)
        su[HALO:HALO + tq, :] = a_ref[...].astype(F32) * _sigmoid(g_ref[...].astype(F32))
        up = ap_ref[...].astype(F32) * _sigmoid(gp_ref[...].astype(F32))
        su[0:HALO, :] = jnp.where(i > 0, up, 0.0)
        sd[0:tq, :] = d_ref[...]
        sd[tq:tq + HALO, :] = jnp.where(i < nblk - 1, dn_ref[...], 0.0)
        _phases(su, sus)
        _phases(sd, sds)

        @pl.when(i == 0)
        def _():
            dw_ref[...] = jnp.zeros_like(dw_ref)

        for j in range(taps):
            dw_ref[j:j + 1, :] += jnp.sum(d_ref[...] * _tap(su, sus, off + j, tq), axis=0, keepdims=True)
        sa = jnp.zeros((1, C), F32)
        sb = jnp.zeros((1, C), F32)
        _spread(w_ref, wb, taps)
        for r in range(tq // CONV_ROWS):
            rows = slice(r * CONV_ROWS, (r + 1) * CONV_ROWS)
            dus = [wb[0] * _tap(sd, sds, taps - 1 + r * CONV_ROWS + 8 * k, 8) for k in range(CONV_ROWS // 8)]
            for j in range(1, taps):
                wj = wb[j]
                dus = [acc + wj * _tap(sd, sds, taps - 1 - j + r * CONV_ROWS + 8 * k, 8) for k, acc in enumerate(dus)]
            du = jnp.concatenate(dus, axis=0)
            a = a_ref[rows, :].astype(F32)
            sg = _sigmoid(g_ref[rows, :].astype(F32))
            da = du * sg
            dgt = du * a * sg * (1.0 - sg)
            dh_ref[rows, 0:C] = da.astype(BF16)
            dh_ref[rows, C:2 * C] = dgt.astype(BF16)
            sa = sa + jnp.sum(da, axis=0, keepdims=True)
            sb = sb + jnp.sum(dgt, axis=0, keepdims=True)
        first = i == 0
        _acc_rows(dbi_ref.at[:, 0:C], sa, first)
        _acc_rows(dbi_ref.at[:, C:2 * C], sb, first)

    prev = lambda col: pl.BlockSpec((HALO, C), lambda i: (jnp.maximum(i * nh - 1, 0), col))
    nxt = pl.BlockSpec((HALO, C), lambda i: (jnp.minimum((i + 1) * nh, T // HALO - 1), 0))
    cur = lambda col: pl.BlockSpec((tq, C), lambda i: (i, col))
    return _pc(body, "dwconv_bwd", (nblk,),
               [cur(0), nxt, cur(0), cur(1), prev(0), prev(1), _const((HALO, C))],
               [_rows(tq, 2 * C), _const((HALO, C)), _const((1, 2 * C))],
               [_sds((T, 2 * C), BF16), _sds((HALO, C), F32), _sds((1, 2 * C), F32)],
               scratch=[pltpu.VMEM((HALO + tq, C), F32), pltpu.VMEM((7, HALO + tq, C), F32),
                        pltpu.VMEM((HALO + tq, C), F32), pltpu.VMEM((7, HALO + tq, C), F32), pltpu.VMEM((taps, 8, C), F32)],
               sem=("arbitrary",))(dcv, dcv, h, h, h, h, w_dw)


def conv_in_bwd(dh, dpre_mix, w_in, alpha):
    T, D = dpre_mix.shape
    nw = w_in[0].shape[2]
    tm = _tile(T)

    def body(dh_ref, dp_ref, w_ref, dx_ref):
        acc = alpha * dp_ref[...]
        for j in range(NS):
            acc = acc + lax.dot_general(dh_ref[:, j * nw:(j + 1) * nw], w_ref[j], NT, preferred_element_type=F32)
        dx_ref[...] = acc

    return _pc(body, "conv_in_bwd", (T // tm,), [_rows(tm, NS * nw), _rows(tm, D), _wspec(w_in)], _rows(tm, D),
               _sds((T, D), F32), sem=("parallel",))(dh, dpre_mix, w_in[0])


def wgrad(a, b, row_sharded, name, into):
    prev, out_shape, off = into
    T, Ka = a.shape
    Nb = b.shape[1]
    tt = min(1024, T)
    nt = T // tt
    ka, tn = min(Ka, 1024), min(Nb, 1024)
    if row_sharded:
        sr = Ka // NS
        spb = max(ka // sr, 1)
        rb = ka // spb
        assert out_shape[2] == Nb and off % rb == 0
        out_spec = pl.BlockSpec((spb, rb, tn), lambda i, j, t: (i, off // rb, j))
    else:
        sc = Nb // NS
        spb = max(tn // sc, 1)
        rb = ka
        assert out_shape[2] == sc and off % ka == 0
        out_spec = pl.BlockSpec((spb, ka, tn // spb), lambda i, j, t: (j, off // ka + i, 0))

    def body(a_ref, b_ref, *rest):
        o_ref, acc = rest[-2:]
        t = pl.program_id(2)
        av = a_ref[...]
        if av.dtype != BF16:
            av = av.astype(BF16)
        d = lax.dot_general(av, b_ref[...], TN, preferred_element_type=F32)

        @pl.when(t == 0)
        def _():
            acc[...] = d

        @pl.when(t > 0)
        def _():
            acc[...] += d

        @pl.when(t == nt - 1)
        def _():
            for s in range(spb):
                if row_sharded:
                    o_ref[s] = acc[s * rb:(s + 1) * rb, :].astype(BF16)
                else:
                    o_ref[s] = acc[:, s * (tn // spb):(s + 1) * (tn // spb)].astype(BF16)

    ins = [pl.BlockSpec((tt, ka), lambda i, j, t: (t, i)), pl.BlockSpec((tt, tn), lambda i, j, t: (t, j))]
    args = [a, b]
    kw = {}
    if prev is not None:
        ins.append(ANY)
        args.append(prev)
        kw["input_output_aliases"] = {2: 0}
    return _pc(body, name, (Ka // ka, Nb // tn, nt), ins, out_spec, _sds(out_shape, BF16),
               scratch=[pltpu.VMEM((ka, tn), F32)], sem=("parallel", "parallel", "arbitrary"), **kw)(*args)


def _adamw_math(w, g, m, v):
    c1 = 1.0 - ADAM_B1 ** ADAM_STEP
    c2 = 1.0 - ADAM_B2 ** ADAM_STEP
    mn = ADAM_B1 * m + (1.0 - ADAM_B1) * g
    vn = ADAM_B2 * v + (1.0 - ADAM_B2) * (g * g)
    return -ADAM_LR * ((mn / c1) / (jnp.sqrt(vn / c2) + ADAM_EPS) + ADAM_WD * w), mn, vn


def adamw_layer(w, m, v, layer, gbuf, off, prev, name):
    L, R, W = w.shape
    tr = 256
    assert R % tr == 0 and off % tr == 0

    def body(w_ref, g_ref, m_ref, v_ref, *rest):
        go_ref, d_ref, mo_ref, vo_ref = rest[-4:]
        g = g_ref[...]
        go_ref[...] = g
        d_ref[...], mo_ref[...], vo_ref[...] = _adamw_math(w_ref[...], g, m_ref[...], v_ref[...])

    lay = pl.BlockSpec((None, tr, W), lambda r: (layer, r, 0))
    ins = [lay, pl.BlockSpec((tr, W), lambda r: (off // tr + r, 0)), lay, lay]
    args = [w, gbuf, m, v]
    kw = {}
    if prev is not None:
        ins += [ANY] * 4
        args += list(prev)
        kw["input_output_aliases"] = {4 + k: k for k in range(4)}
    return _pc(body, name, (R // tr,), ins, [lay] * 4, [_sds((L, R, W), F32)] * 4, sem=("parallel",), **kw)(*args)


def adamw_many(ws, gs, ms, vs):
    n = len(ws)

    def body(*refs):
        for k in range(n):
            d, mn, vn = _adamw_math(refs[k][...], refs[n + k][...], refs[2 * n + k][...], refs[3 * n + k][...])
            refs[4 * n + k][...] = d
            refs[5 * n + k][...] = mn
            refs[6 * n + k][...] = vn

    outs = pl.pallas_call(body, name="adamw_small", out_shape=[_sds(a.shape, F32) for a in ws] * 3)(*ws, *gs, *ms, *vs)
    return outs[:n], outs[n:2 * n], outs[2 * n:]


def _rope_tables(T):
    pos = jnp.arange(T, dtype=F32)
    inv_freq = ROPE_THETA ** (-jnp.arange(0, ROPE, 2, dtype=F32) / ROPE)
    ang = pos[:, None] * inv_freq[None, :]
    cos, sin = jnp.cos(ang), jnp.sin(ang)
    pad = HEAD - ROPE
    c = jnp.concatenate([cos, cos, jnp.ones((T, pad), F32)], axis=1)
    s = jnp.concatenate([-sin, sin, jnp.zeros((T, pad), F32)], axis=1)
    return jnp.stack([jnp.tile(c, (1, 128 // HEAD)), jnp.tile(s, (1, 128 // HEAD))])


def _local_step(x, p, target, W, small, lay, hook=None):
    if hook is None:
        hook = lambda stage, after, G, sg=None: None
    T, D = x.shape
    depth = small["mix_ln_g"].shape[0]
    alpha = float((2 * depth) ** 0.25)
    taps = small["taps"]
    row = lambda a, i: a[i:i + 1]
    cs = _rope_tables(T)

    x0b = x.astype(BF16)
    h = conv_in_fwd(x0b, W["conv_w_in"], small["conv_b_in"])
    cv, s = dwconv_fwd(h, small["conv_w_dw"], small["conv_b_dw"], small["conv_ln_g"], small["conv_ln_b"], taps)
    pre_mix0, x1, x1b = mm_res_ln(s, W["conv_w_out"], x, row(small["mix_ln_g"], 0), row(small["mix_ln_b"], 0), alpha,
                                  small["conv_b_out"], "conv_out_fwd")
    hook("weights1", x1b, None)
    r0 = mlp_up_fwd(x1b, W["mlp_w_up0"], "mlp_up_fwd0")
    pre_mlp0, x2, x2b = mm_res_ln(r0, W["mlp_w_down0"], x1, row(small["mlp_ln_g"], 0), row(small["mlp_ln_b"], 0), alpha,
                                  None, "mlp_down_fwd0")
    x3, x3b, pp0, gl0 = ple_fwd(x2, x2b, p, 0, W["ple_w_proj0"], W["ple_w_gate0"], None, "ple_fwd0")

    hook("weights2", x3b, None)
    q, k, v = qkv_fwd(x3b, W["attn_w_q"], W["kv_w_k"], W["kv_w_v"], cs)
    o = attn_fwd(q, k, v, small["attn_sinks"])
    pre_mix1, x4, x4b = mm_res_ln(o, W["attn_w_o"], x3, row(small["mix_ln_g"], 1), row(small["mix_ln_b"], 1), alpha,
                                  None, "attn_out_fwd")
    r1 = mlp_up_fwd(x4b, W["mlp_w_up1"], "mlp_up_fwd1")
    pre_mlp1, x5, x5b = mm_res_ln(r1, W["mlp_w_down1"], x4, row(small["mlp_ln_g"], 1), row(small["mlp_ln_b"], 1), alpha,
                                  None, "mlp_down_fwd1")
    dx6, loss, pp1, gl1 = ple_fwd(x5, x5b, p, 1, W["ple_w_proj1"], W["ple_w_gate1"], target, "ple_fwd1")

    G, sg = {}, {}
    where = {n: (key, off) for key in lay for n, off, _ in lay[key]}
    rows_of = {key: sum(r for _, _, r in lay[key]) for key in lay}

    def wg(name, a, b, row_sharded):
        key, off = where[name]
        shape = (NS, rows_of[key], W[name][0].shape[2])
        G[key] = wgrad(a, b, row_sharded, "wg_" + name, (G.get(key), shape, off))

    dpp1, dgl1, dx5 = ple_bwd(dx6, pp1, gl1, W["ple_w_gate1"], "ple_bwd1")
    wg("ple_w_proj1", p[1], dpp1, False)
    wg("ple_w_gate1", x5b, dgl1, True)
    dpre_mlp1, dpre_mlp1b, dm1, g_mlp_g1, g_mlp_b1 = mlp_bwd1(dx5, pre_mlp1, row(small["mlp_ln_g"], 1), r1,
                                                              W["mlp_w_down1"], "mlp_bwd1_1")
    wg("mlp_w_down1", r1, dpre_mlp1b, True)
    wg("mlp_w_up1", x4b, dm1, False)
    dpre_mix1, dpre_mix1b, do, g_mix_g1, g_mix_b1, _ = mlp_bwd2(dpre_mlp1, dm1, W["mlp_w_up1"], alpha, pre_mix1,
                                                                row(small["mix_ln_g"], 1), W["attn_w_o"], "mlp_bwd2_1")
    wg("attn_w_o", o, dpre_mix1b, True)
    dq, dk, dv, dsinks = attn_bwd(q, k, v, do, small["attn_sinks"])
    dqb, dkb, dvb, dx3 = qkv_bwd(dq, dk, dv, dpre_mix1,
                                 W["attn_w_q"], W["kv_w_k"], W["kv_w_v"], cs, alpha)
    wg("attn_w_q", x3b, dqb, True)
    wg("kv_w_k", x3b, dkb, True)
    wg("kv_w_v", x3b, dvb, True)
    hook("grads2", None, G)

    dpp0, dgl0, dx2 = ple_bwd(dx3, pp0, gl0, W["ple_w_gate0"], "ple_bwd0")
    wg("ple_w_proj0", p[0], dpp0, False)
    wg("ple_w_gate0", x2b, dgl0, True)
    dpre_mlp0, dpre_mlp0b, dm0, g_mlp_g0, g_mlp_b0 = mlp_bwd1(dx2, pre_mlp0, row(small["mlp_ln_g"], 0), r0,
                                                              W["mlp_w_down0"], "mlp_bwd1_0")
    wg("mlp_w_down0", r0, dpre_mlp0b, True)
    wg("mlp_w_up0", x1b, dm0, False)
    dpre_mix0, dpre_mix0b, dsw, g_mix_g0, g_mix_b0, g_b_out = mlp_bwd2(dpre_mlp0, dm0, W["mlp_w_up0"], alpha, pre_mix0,
                                                                      row(small["mix_ln_g"], 0), W["conv_w_out"],
                                                                      "mlp_bwd2_0")
    hook("grads1", None, G)
    dcv, g_cln_g, g_cln_b, g_b_dw = conv_mid_bwd(dsw, cv, small["conv_ln_g"], small["conv_ln_b"])
    dh, g_w_dw, g_b_in = dwconv_bwd(dcv, h, small["conv_w_dw"], taps)
    wg("conv_w_out", s, dpre_mix0b, True)
    wg("conv_w_in", x0b, dh, False)

    sg["conv_b_in"] = g_b_in
    sg["conv_w_dw"] = g_w_dw
    sg["conv_b_dw"], sg["conv_ln_g"], sg["conv_ln_b"], sg["conv_b_out"] = g_b_dw, g_cln_g, g_cln_b, g_b_out
    sg["mix_ln_g"] = [g_mix_g0, g_mix_g1]
    sg["mix_ln_b"] = [g_mix_b0, g_mix_b1]
    sg["mlp_ln_g"] = [g_mlp_g0, g_mlp_g1]
    sg["mlp_ln_b"] = [g_mlp_b0, g_mlp_b1]
    sg["attn_sinks"] = dsinks[:, 0][None, :]
    sg["loss"] = loss
    hook("grads0", None, G, sg)
    grad_x = conv_in_bwd(dh, dpre_mix0, W["conv_w_in"], alpha)
    return loss, grad_x, G, sg


BUFFERS = (("b0", ("conv_w_in",)), ("a0", ("conv_w_out",)),
           ("a1", ("mlp_w_up0", "mlp_w_down0", "ple_w_gate0")), ("c1", ("ple_w_proj0",)),
           ("a2", ("mlp_w_up1", "mlp_w_down1", "ple_w_gate1", "attn_w_q", "attn_w_o")),
           ("c2", ("kv_w_k", "kv_w_v", "ple_w_proj1")))
GROUPS = (("b0", "a0"), ("a1", "c1"), ("a2", "c2"))
ROW_SHARDED = {"mlp_w_down0", "mlp_w_down1", "ple_w_gate0", "ple_w_gate1", "conv_w_out", "attn_w_q", "attn_w_o", "kv_w_k",
               "kv_w_v"}


def _split_layers(weights):
    out = {"conv_w_in": weights["conv_w_in"][0], "conv_w_out": weights["conv_w_out"][0],
           "attn_w_q": weights["attn_w_q"][0], "attn_w_o": weights["attn_w_o"][0],
           "kv_w_k": weights["kv_w_k"], "kv_w_v": weights["kv_w_v"]}
    for n in ("mlp_w_up", "mlp_w_down", "ple_w_proj", "ple_w_gate"):
        for i in range(weights[n].shape[0]):
            out[n + str(i)] = weights[n][i]
    return out


def _layout(shards):
    lay = {}
    for key, names in BUFFERS:
        off, rows = 0, []
        for n in names:
            rows.append((n, off, shards[n].shape[0]))
            off += shards[n].shape[0]
        lay[key] = rows
    return lay


def _place():
    return lax.axis_index("x"), lax.axis_index("y"), lax.axis_index("c")


def _flip(v, f):
    return (v + f) % 2 if f else v


CHIP_FLIPS = ((1, 0), (0, 1), (1, 1))


HBM = pl.BlockSpec(memory_space=pltpu.HBM)
SEM = pl.BlockSpec(memory_space=pltpu.SEMAPHORE)
EFFECT = pltpu.SideEffectType.DATAFLOW_SIDE_EFFECTING


def _half(ref, rows, c):
    return ref.at[pl.ds(pl.multiple_of(c * (rows // 2), 16), rows // 2), :]


def _gather_copies(refs, shapes, whole, send, recv):
    x, y, c = _place()
    me = 2 * x + y
    na = len(refs)
    cps = []
    for d, (fx, fy) in enumerate(CHIP_FLIPS):
        to = (_flip(x, fx), _flip(y, fy), c)
        for k in range(na):
            mine = refs[k].at[me] if k >= na - whole else _half(refs[k].at[me], shapes[k][1], c)
            cps.append(pltpu.make_async_remote_copy(mine, mine, send.at[d * na + k], recv.at[d * na + k], device_id=to,
                                                    device_id_type=MESH))
    return cps


def gather_start(bufs, whole, after, name):
    na = len(bufs)
    shapes = [b.shape for b in bufs]
    nsem = len(CHIP_FLIPS) * na

    def body(*refs):
        ins = refs[:na]
        send, recv = refs[-(na + 3)], refs[-(na + 2)]
        token = refs[-1]
        for cp in _gather_copies(ins, shapes, whole, send, recv):
            cp.start()
        token[...] = jnp.zeros_like(token)

    args = [pltpu.with_memory_space_constraint(b, pltpu.HBM) for b in bufs]
    ins = [HBM] * na
    if after is not None:
        args.append(after)
        ins.append(ANY)
    return pl.pallas_call(
        body, name=name, in_specs=ins,
        out_specs=[SEM, SEM] + [HBM] * na + [pl.BlockSpec(memory_space=pltpu.VMEM)],
        out_shape=[pltpu.SemaphoreType.DMA((nsem,)), pltpu.SemaphoreType.DMA((nsem,))]
        + [pltpu.HBM(b.shape, b.dtype) for b in bufs] + [_sds((8, 128), F32)],
        input_output_aliases={k: k + 2 for k in range(na)},
        compiler_params=pltpu.CompilerParams(has_side_effects=EFFECT))(*args)


def gather_wait(send, recv, bufs, whole, after, name):
    na = len(bufs)
    shapes = [b.shape for b in bufs]

    def body(*refs):
        ins = refs[:na]
        send_ref, recv_ref = refs[na], refs[na + 1]
        for cp in _gather_copies(ins, shapes, whole, send_ref, recv_ref):
            cp.wait_send()
            cp.wait_recv()

    return pl.pallas_call(
        body, name=name, in_specs=[HBM] * na + [SEM, SEM, ANY], out_specs=[HBM] * na,
        out_shape=[pltpu.HBM(b.shape, b.dtype) for b in bufs], input_output_aliases={k: k for k in range(na)},
        compiler_params=pltpu.CompilerParams(has_side_effects=EFFECT))(*bufs, send, recv, after)


def sibling_forward(bufs, name):
    nb = len(bufs)

    def body(*refs):
        outs = refs[nb:2 * nb]
        send, recv = refs[2 * nb:]
        x, y, c = _place()
        cps = []
        for d, (fx, fy) in enumerate(CHIP_FLIPS):
            frm = 2 * _flip(x, fx) + _flip(y, fy)
            for k in range(nb):
                theirs = _half(outs[k].at[frm], bufs[k].shape[1], c)
                cps.append(pltpu.make_async_remote_copy(theirs, theirs, send.at[d * nb + k], recv.at[d * nb + k],
                                                        device_id=(x, y, 1 - c), device_id_type=MESH))
        for cp in cps:
            cp.start()
        for cp in cps:
            cp.wait()

    nsem = len(CHIP_FLIPS) * nb
    return pl.pallas_call(
        body, name=name, in_specs=[ANY] * nb, out_specs=[ANY] * nb, out_shape=[_sds(b.shape, b.dtype) for b in bufs],
        input_output_aliases={k: k for k in range(nb)},
        scratch_shapes=[pltpu.SemaphoreType.DMA((nsem,)), pltpu.SemaphoreType.DMA((nsem,))])(*bufs)


def pack_rows(pieces, rows, width, name):
    def body(*refs):
        o_ref = refs[-1]
        o_ref[...] = jnp.zeros_like(o_ref)
        for ref, (a, off) in zip(refs[:-1], pieces):
            o_ref[off:off + a.shape[0], 0:a.shape[1]] = ref[...]

    return pl.pallas_call(body, name=name, out_shape=_sds((rows, width), F32))(*[a for a, _ in pieces])


def sibling_exchange(grads, small, name):
    nb = len(grads)
    ns = 0 if small is None else 1

    def body(*refs):
        ins, outs = refs[:nb + ns], refs[nb + ns:2 * (nb + ns)]
        send, recv, lsem = refs[2 * (nb + ns):]
        x, y, c = _place()
        me = 4 * x + 2 * y + c
        cps = []
        for k in range(nb):
            hrows = grads[k].shape[1] // 2
            src = ins[k].at[:, pl.ds(pl.multiple_of((1 - c) * hrows, 16), hrows), :]
            cps.append(pltpu.make_async_remote_copy(src, outs[k], send.at[k], recv.at[k], device_id=(x, y, 1 - c),
                                                    device_id_type=MESH))
        if ns:
            n = nb
            for fx in (0, 1):
                for fy in (0, 1):
                    for fc in (0, 1):
                        if fx or fy or fc:
                            cps.append(pltpu.make_async_remote_copy(
                                ins[nb], outs[nb].at[me], send.at[n], recv.at[n],
                                device_id=(_flip(x, fx), _flip(y, fy), _flip(c, fc)), device_id_type=MESH))
                            n += 1
            own = pltpu.make_async_copy(ins[nb], outs[nb].at[me], lsem)
            own.start()
        for cp in cps:
            cp.start()
        for cp in cps:
            cp.wait()
        if ns:
            own.wait()

    shapes = [_sds((NS, g.shape[1] // 2, g.shape[2]), g.dtype) for g in grads]
    args = list(grads)
    if ns:
        shapes.append(_sds((8,) + small.shape, small.dtype))
        args.append(small)
    nsem = nb + 7 * ns
    return pl.pallas_call(
        body, name=name, in_specs=[ANY] * (nb + ns), out_specs=[ANY] * (nb + ns), out_shape=shapes,
        scratch_shapes=[pltpu.SemaphoreType.DMA((nsem,)), pltpu.SemaphoreType.DMA((nsem,)), pltpu.SemaphoreType.DMA(())])(*args)


def _chip_copies(sums, lands, send, recv):
    x, y, c = _place()
    nb = len(sums)
    cps = []
    for d, (fx, fy) in enumerate(CHIP_FLIPS):
        tx, ty = _flip(x, fx), _flip(y, fy)
        for k in range(nb):
            cps.append(pltpu.make_async_remote_copy(sums[k].at[2 * tx + ty], lands[k].at[d], send.at[d * nb + k],
                                                    recv.at[d * nb + k], device_id=(tx, ty, c), device_id_type=MESH))
    return cps


def chip_start(sums, name):
    nb = len(sums)
    nsem = len(CHIP_FLIPS) * nb

    def body(*refs):
        ins, lands = refs[:nb], refs[nb:2 * nb]
        send, recv = refs[2 * nb], refs[2 * nb + 1]
        for cp in _chip_copies(ins, lands, send, recv):
            cp.start()
        refs[-1][...] = jnp.zeros_like(refs[-1])

    zones = [lax.empty((len(CHIP_FLIPS),) + s.shape[1:], s.dtype) for s in sums]
    args = [pltpu.with_memory_space_constraint(a, pltpu.HBM) for a in list(sums) + zones]
    return pl.pallas_call(
        body, name=name, in_specs=[HBM] * (2 * nb),
        out_specs=[SEM, SEM] + [HBM] * (2 * nb) + [pl.BlockSpec(memory_space=pltpu.VMEM)],
        out_shape=[pltpu.SemaphoreType.DMA((nsem,)), pltpu.SemaphoreType.DMA((nsem,))]
        + [pltpu.HBM(a.shape, a.dtype) for a in list(sums) + zones] + [_sds((8, 128), F32)],
        input_output_aliases={k: k + 2 for k in range(2 * nb)},
        compiler_params=pltpu.CompilerParams(has_side_effects=EFFECT))(*args)


def chip_wait(send, recv, sums, lands, after, name):
    nb = len(sums)

    def body(*refs):
        ins, zones = refs[:nb], refs[nb:2 * nb]
        for cp in _chip_copies(ins, zones, refs[2 * nb], refs[2 * nb + 1]):
            cp.wait_send()
            cp.wait_recv()

    arrs = list(sums) + list(lands)
    return pl.pallas_call(
        body, name=name, in_specs=[HBM] * (2 * nb) + [SEM, SEM, ANY], out_specs=[HBM] * (2 * nb),
        out_shape=[pltpu.HBM(a.shape, a.dtype) for a in arrs], input_output_aliases={k: k for k in range(2 * nb)},
        compiler_params=pltpu.CompilerParams(has_side_effects=EFFECT))(*arrs, send, recv, after)


def sibling_share(halves):
    nb = len(halves)

    def body(*refs):
        outs = refs[nb:2 * nb]
        send, recv = refs[2 * nb:]
        x, y, c = _place()
        cps = []
        for k in range(nb):
            hrows = halves[k].shape[0] // 2
            mine = outs[k].at[pl.ds(pl.multiple_of(c * hrows, 8), hrows), :]
            cps.append(pltpu.make_async_remote_copy(mine, mine, send.at[k], recv.at[k], device_id=(x, y, 1 - c),
                                                    device_id_type=MESH))
        for cp in cps:
            cp.start()
        for cp in cps:
            cp.wait()

    return pl.pallas_call(
        body, name="sibling_share", in_specs=[ANY] * nb, out_specs=[ANY] * nb,
        out_shape=[_sds(h.shape, h.dtype) for h in halves], input_output_aliases={k: k for k in range(nb)},
        scratch_shapes=[pltpu.SemaphoreType.DMA((nb,)), pltpu.SemaphoreType.DMA((nb,))])(*halves)


def _row_tile(rows):
    for cand in (512, 384, 256, 128, 64, 32, 16):
        if rows % cand == 0:
            return cand
    return rows


def pair_sum(g, r, idx, name):
    _, hrows, W = r.shape
    tr = _row_tile(hrows)
    nrb = hrows // tr

    def body(idx_ref, g_ref, r_ref, o_ref):
        o_ref[...] = (g_ref[...].astype(F32) + r_ref[...].astype(F32)).astype(BF16)

    gs = pltpu.PrefetchScalarGridSpec(
        num_scalar_prefetch=1, grid=(NS, nrb),
        in_specs=[pl.BlockSpec((None, tr, W), lambda j, i, s: (j, s[1] * nrb + i, 0)),
                  pl.BlockSpec((None, tr, W), lambda j, i, s: (j, i, 0))],
        out_specs=pl.BlockSpec((None, tr, W), lambda j, i, s: (j, i, 0)))
    return pl.pallas_call(body, name=name, grid_spec=gs, out_shape=_sds(r.shape, BF16),
                          compiler_params=pltpu.CompilerParams(dimension_semantics=("parallel", "parallel")))(idx, g, r)


def chip_sum(s, t, idx, name):
    _, hrows, W = s.shape
    tr = _row_tile(hrows)
    nrb = hrows // tr

    def body(idx_ref, s_ref, t_ref, o_ref):
        acc = s_ref[...].astype(F32)
        for d in range(t.shape[0]):
            acc = acc + t_ref[d].astype(F32)
        o_ref[...] = acc

    gs = pltpu.PrefetchScalarGridSpec(
        num_scalar_prefetch=1, grid=(nrb,),
        in_specs=[pl.BlockSpec((None, tr, W), lambda i, sc: (sc[0], i, 0)),
                  pl.BlockSpec((t.shape[0], tr, W), lambda i, sc: (0, i, 0))],
        out_specs=pl.BlockSpec((tr, W), lambda i, sc: (sc[1] * nrb + i, 0)))
    return pl.pallas_call(body, name=name, grid_spec=gs, out_shape=_sds((2 * hrows, W), F32),
                          compiler_params=pltpu.CompilerParams(dimension_semantics=("parallel",)))(idx, s, t)


def small_sum(packs):
    n, R, W = packs.shape

    def body(p_ref, o_ref):
        acc = p_ref[0]
        for d in range(1, n):
            acc = acc + p_ref[d]
        o_ref[...] = acc

    return pl.pallas_call(body, name="small_sum", out_shape=_sds((R, W), F32))(packs)


WEIGHTS = ["conv_w_in", "conv_b_in", "conv_w_dw", "conv_b_dw", "conv_ln_g", "conv_ln_b", "conv_w_out", "conv_b_out", "kv_w_k",
           "kv_w_v", "attn_w_q", "attn_sinks", "attn_w_o", "mix_ln_g", "mix_ln_b", "mlp_w_up", "mlp_w_down", "mlp_ln_g",
           "mlp_ln_b", "ple_w_proj", "ple_w_gate"]
BIG = ["conv_w_in", "conv_w_out", "kv_w_k", "kv_w_v", "attn_w_q", "attn_w_o", "mlp_w_up", "mlp_w_down", "ple_w_proj",
       "ple_w_gate"]
SMALL = [n for n in WEIGHTS if n not in BIG]


def _step(x, p, target, w, m, v):
    D = x.shape[-1]
    ds = D // NS
    xq, yq, cq = _place()
    chip = 2 * xq + yq
    idx = jnp.stack([chip, cq]).astype(jnp.int32)

    shards = _split_layers(w)
    lay = _layout(shards)
    taps = w["conv_w_dw"].shape[1]
    small_loc = pack_rows([(w["conv_w_dw"][0], 0), (w["conv_b_dw"], HALO), (w["conv_ln_g"], HALO + 1), (w["conv_ln_b"], HALO + 2),
                           (w["conv_b_out"], HALO + 3), (w["conv_b_in"].reshape(2, ds), HALO + 4)], HALO + 8, ds, "pack_small")
    slot = lambda a: lax.dynamic_update_slice(lax.empty((NS,) + a.shape, a.dtype), a[None], (chip, 0, 0))
    started, token = [], None
    for gi, keys in enumerate(GROUPS):
        bufs = [slot(jnp.concatenate([shards[n].astype(BF16) for n, _, _ in lay[key]], axis=0)) for key in keys]
        if gi == 0:
            bufs.append(slot(small_loc))
        send, recv, *thru, token = gather_start(bufs, 1 if gi == 0 else 0, token, "gather_start%d" % gi)
        started.append((send, recv, thru))
    W = {}

    def arrive(gi, after):
        send, recv, thru = started[gi]
        whole = 1 if gi == 0 else 0
        got = gather_wait(send, recv, thru, whole, after, "gather_wait%d" % gi)
        nk = len(GROUPS[gi])
        for key, buf in zip(GROUPS[gi], sibling_forward(got[:nk], "sibling_forward%d" % gi)):
            for n, off, rows in lay[key]:
                W[n] = (buf, off, rows)
        return got[nk:]

    gs, = arrive(0, token)
    across = lambda rows: gs[:, rows, :].transpose(1, 0, 2).reshape(rows.stop - rows.start, D)
    small = {"taps": taps, "conv_w_dw": across(slice(0, HALO)), "conv_b_dw": across(slice(HALO, HALO + 1)),
             "conv_ln_g": across(slice(HALO + 1, HALO + 2)), "conv_ln_b": across(slice(HALO + 2, HALO + 3)),
             "conv_b_out": across(slice(HALO + 3, HALO + 4)), "conv_b_in": gs[:, HALO + 4:HALO + 6, :].reshape(1, 2 * D),
             "attn_sinks": w["attn_sinks"], "mix_ln_g": w["mix_ln_g"], "mix_ln_b": w["mix_ln_b"],
             "mlp_ln_g": w["mlp_ln_g"], "mlp_ln_b": w["mlp_ln_b"]}

    reducing = {}

    def reduce_start(gi, G, pack):
        keys = GROUPS[gi]
        parts = [G[key] for key in keys]
        got = sibling_exchange(parts, pack, "sibling_exchange%d" % gi)
        sums = [pair_sum(g, r, idx, "pair_sum_" + key) for g, r, key in zip(parts, got, keys)]
        send, recv, *thru, token = chip_start(sums, "chip_start%d" % gi)
        reducing[gi] = (send, recv, thru[:len(keys)], thru[len(keys):])
        _FOLLOW.append(token)
        return got[len(keys):]

    def small_pack(sg):
        pieces = [(sg["conv_b_in"].reshape(2, D), 0), (sg["conv_w_dw"], 2)]
        r0 = 2 + HALO
        for i, n in enumerate(("conv_b_dw", "conv_ln_g", "conv_ln_b", "conv_b_out")):
            pieces.append((sg[n], r0 + i))
        r0 += 4
        for i, n in enumerate(("mix_ln_g", "mix_ln_b", "mlp_ln_g", "mlp_ln_b")):
            pieces += [(sg[n][0], r0 + 2 * i), (sg[n][1], r0 + 2 * i + 1)]
        pieces += [(sg["attn_sinks"], r0 + 8), (sg["loss"][0:1], r0 + 9)]
        return pack_rows(pieces, r0 + 10, D, "pack_small_grads")

    def hook(stage, after, G, sg=None):
        if stage == "weights1":
            arrive(1, after)
        elif stage == "weights2":
            arrive(2, after)
        elif stage == "grads2":
            reduce_start(2, G, None)
        elif stage == "grads1":
            reduce_start(1, G, None)
        elif stage == "grads0":
            reducing["packs"], = reduce_start(0, G, small_pack(sg))

    loss, grad_x, G, sg = _local_step(x[0], p[:, 0], target[0], W, small, lay, hook)
    _FOLLOW.clear()
    nsink = w["attn_sinks"].shape[1]
    tot = small_sum(reducing["packs"])

    halves = {}
    for gi in (2, 1, 0):
        send, recv, sums, lands = reducing[gi]
        done = chip_wait(send, recv, sums, lands, grad_x, "chip_wait%d" % gi)
        nk = len(GROUPS[gi])
        for key, s_, t_ in zip(GROUPS[gi], done[:nk], done[nk:]):
            halves[key] = chip_sum(s_, t_, idx, "chip_sum_" + key)
    order = [key for key, _ in BUFFERS]
    full = sibling_share([halves[key] for key in order])

    grads, delta, new_m, new_v = {}, {}, {}, {}
    found = {n: (buf, off) for key, buf in zip(order, full) for n, off, _ in lay[key]}
    for n in BIG:
        three = lambda a: a.reshape((-1,) + a.shape[-2:])
        w3, m3, v3 = three(w[n]), three(m[n]), three(v[n])
        outs = None
        for i in range(w3.shape[0]):
            buf, off = found[n + str(i)] if n + str(i) in found else found[n]
            outs = adamw_layer(w3, m3, v3, i, buf, off, outs, "adamw_%s%d" % (n, i))
        grads[n], delta[n], new_m[n], new_v[n] = [a.reshape(w[n].shape) for a in outs]
    cols = lambda rows: lax.dynamic_slice(rows, (0, chip * ds), (rows.shape[0], ds))
    grads["conv_b_in"] = lax.dynamic_slice(tot[0:2].reshape(1, 2 * D), (0, chip * 2 * ds), (1, 2 * ds))
    grads["conv_w_dw"] = cols(tot[2:2 + taps])[None]
    r0 = 2 + HALO
    for i, n in enumerate(("conv_b_dw", "conv_ln_g", "conv_ln_b", "conv_b_out")):
        grads[n] = cols(tot[r0 + i:r0 + i + 1])
    r0 += 4
    for i, n in enumerate(("mix_ln_g", "mix_ln_b", "mlp_ln_g", "mlp_ln_b")):
        grads[n] = tot[r0 + 2 * i:r0 + 2 * i + 2]
    grads["attn_sinks"] = tot[r0 + 8:r0 + 9, 0:nsink]

    ds_, ms_, vs_ = adamw_many([w[n] for n in SMALL], [grads[n] for n in SMALL], [m[n] for n in SMALL], [v[n] for n in SMALL])
    for n, d_, m_, v_ in zip(SMALL, ds_, ms_, vs_):
        delta[n], new_m[n], new_v[n] = d_, m_, v_

    total = tot[r0 + 9, 0]
    return (total, grad_x[None], *[grads[n] for n in WEIGHTS], *[delta[n] for n in WEIGHTS], *[new_m[n] for n in WEIGHTS],
            *[new_v[n] for n in WEIGHTS])


def kernel(x, p, conv_w_in, conv_b_in, conv_w_dw, conv_b_dw, conv_ln_g, conv_ln_b, conv_w_out, conv_b_out, kv_w_k, kv_w_v, attn_w_q, attn_sinks, attn_w_o, mix_ln_g, mix_ln_b, mlp_w_up, mlp_w_down, mlp_ln_g, mlp_ln_b, ple_w_proj, ple_w_gate, loss_target, m_conv_w_in, m_conv_b_in, m_conv_w_dw, m_conv_b_dw, m_conv_ln_g, m_conv_ln_b, m_conv_w_out, m_conv_b_out, m_kv_w_k, m_kv_w_v, m_attn_w_q, m_attn_sinks, m_attn_w_o, m_mix_ln_g, m_mix_ln_b, m_mlp_w_up, m_mlp_w_down, m_mlp_ln_g, m_mlp_ln_b, m_ple_w_proj, m_ple_w_gate, v_conv_w_in, v_conv_b_in, v_conv_w_dw, v_conv_b_dw, v_conv_ln_g, v_conv_ln_b, v_conv_w_out, v_conv_b_out, v_kv_w_k, v_kv_w_v, v_attn_w_q, v_attn_sinks, v_attn_w_o, v_mix_ln_g, v_mix_ln_b, v_mlp_w_up, v_mlp_w_down, v_mlp_ln_g, v_mlp_ln_b, v_ple_w_proj, v_ple_w_gate):
    w = dict(zip(WEIGHTS, (conv_w_in, conv_b_in, conv_w_dw, conv_b_dw, conv_ln_g, conv_ln_b, conv_w_out, conv_b_out, kv_w_k,
                           kv_w_v, attn_w_q, attn_sinks, attn_w_o, mix_ln_g, mix_ln_b, mlp_w_up, mlp_w_down, mlp_ln_g, mlp_ln_b,
                           ple_w_proj, ple_w_gate)))
    m = dict(zip(WEIGHTS, (m_conv_w_in, m_conv_b_in, m_conv_w_dw, m_conv_b_dw, m_conv_ln_g, m_conv_ln_b, m_conv_w_out,
                           m_conv_b_out, m_kv_w_k, m_kv_w_v, m_attn_w_q, m_attn_sinks, m_attn_w_o, m_mix_ln_g, m_mix_ln_b,
                           m_mlp_w_up, m_mlp_w_down, m_mlp_ln_g, m_mlp_ln_b, m_ple_w_proj, m_ple_w_gate)))
    v = dict(zip(WEIGHTS, (v_conv_w_in, v_conv_b_in, v_conv_w_dw, v_conv_b_dw, v_conv_ln_g, v_conv_ln_b, v_conv_w_out,
                           v_conv_b_out, v_kv_w_k, v_kv_w_v, v_attn_w_q, v_attn_sinks, v_attn_w_o, v_mix_ln_g, v_mix_ln_b,
                           v_mlp_w_up, v_mlp_w_down, v_mlp_ln_g, v_mlp_ln_b, v_ple_w_proj, v_ple_w_gate)))
    return _step(x, p, loss_target, w, m, v)
```

```python
import functools

import jax
import jax.numpy as jnp
from jax import lax
from jax.experimental import pallas as pl
from jax.experimental.pallas import tpu as pltpu

F32 = jnp.float32
BF16 = jnp.bfloat16
NS = 4
HEAD = 64
BLK = 128
ROPE = 16
ROPE_THETA = 500000.0
LN_EPS = 1e-5
NEG = -1e30
HALO = 32
ADAM_LR, ADAM_B1, ADAM_B2, ADAM_EPS, ADAM_WD, ADAM_STEP = 0.001, 0.9, 0.999, 1e-08, 0.01, 10
MESH = pl.DeviceIdType.MESH
ANY = pl.BlockSpec(memory_space=pl.ANY)
NT = (((1,), (1,)), ((), ()))
TN = (((0,), (0,)), ((), ()))


_FOLLOW = []


def _pc(body, name, grid, in_specs, out_specs, out_shape, scratch=(), sem=None, vmem=56, **kw):
    call = lambda fn, ins: pl.pallas_call(
        fn, name=name, grid=grid, in_specs=ins, out_specs=out_specs, out_shape=out_shape,
        scratch_shapes=list(scratch),
        compiler_params=pltpu.CompilerParams(dimension_semantics=sem, vmem_limit_bytes=vmem * 2 ** 20), **kw)
    if not _FOLLOW:
        return call(body, in_specs)
    extra = list(_FOLLOW)
    _FOLLOW.clear()
    n_in = len(in_specs)

    def ordered(*refs):
        return body(*refs[:n_in], *refs[n_in + len(extra):])

    run = call(ordered, list(in_specs) + [ANY] * len(extra))
    return lambda *args: run(*args, *extra)


def _rows(tm, n):
    return pl.BlockSpec((tm, n), lambda i: (i, 0))


def _const(shape):
    return pl.BlockSpec(shape, lambda *_: (0,) * len(shape))


def _wspec(w):
    buf, off, rows = w
    assert off % rows == 0
    return pl.BlockSpec((NS, rows, buf.shape[2]), lambda *_: (0, off // rows, 0))


def _rows_joined(w_ref):
    n, r, c = w_ref.shape
    return w_ref[...].reshape(n * r, c)


def _sds(shape, dtype):
    return jax.ShapeDtypeStruct(shape, dtype)


def _tile(t):
    return min(256, t)


def _sigmoid(x):
    return 1.0 / (1.0 + jnp.exp(-x))


def _ln_stats(w):
    mu = jnp.mean(w, axis=-1, keepdims=True)
    xc = w - mu
    var = jnp.mean(xc * xc, axis=-1, keepdims=True)
    rstd = lax.rsqrt(var + LN_EPS)
    return xc * rstd, rstd


def _ln_bwd(dy, w, g):
    xhat, rstd = _ln_stats(w)
    dxhat = dy * g
    m1 = jnp.mean(dxhat, axis=-1, keepdims=True)
    m2 = jnp.mean(dxhat * xhat, axis=-1, keepdims=True)
    dw = rstd * (dxhat - m1 - xhat * m2)
    return dw, jnp.sum(dy * xhat, axis=0, keepdims=True), jnp.sum(dy, axis=0, keepdims=True)


def _acc_rows(ref, val, first):
    @pl.when(first)
    def _():
        ref[...] = val

    @pl.when(jnp.logical_not(first))
    def _():
        ref[...] += val


def conv_in_fwd(xb, w_in, b_in):
    T, D = xb.shape
    nw = w_in[0].shape[2]
    tm = _tile(T)

    def body(x_ref, w_ref, b_ref, h_ref):
        x = x_ref[...]
        for j in range(NS):
            sl = slice(j * nw, (j + 1) * nw)
            h_ref[:, sl] = (jnp.dot(x, w_ref[j], preferred_element_type=F32) + b_ref[:, sl]).astype(BF16)

    return _pc(body, "conv_in_fwd", (T // tm,), [_rows(tm, D), _wspec(w_in), _const((1, NS * nw))],
               _rows(tm, NS * nw), _sds((T, NS * nw), BF16), sem=("parallel",))(xb, w_in[0], b_in)


CONV_ROWS = 16


def _phases(scr, sh):
    n = scr.shape[0] - 8
    for b in range(1, 8):
        sh[b - 1, 0:n, :] = scr[b:b + n, :]


def _spread(w_ref, wb, taps):
    for j in range(taps):
        wb[j] = jnp.broadcast_to(w_ref[j:j + 1, :], wb.shape[1:])


def _tap(scr, sh, o, n):
    b = o % 8
    return scr[o:o + n, :] if b == 0 else sh[b - 1, o - b:o - b + n, :]


def dwconv_fwd(h, w_dw, b_dw, ln_g, ln_b, taps):
    T = h.shape[0]
    C = h.shape[1] // 2
    tq = _tile(T)
    nh = tq // HALO
    off = HALO - (taps - 1)

    def body(a_ref, g_ref, ap_ref, gp_ref, w_ref, bdw_ref, lg_ref, lb_ref, cv_ref, s_ref, scr, sh, wb):
        i = pl.program_id(0)
        scr[HALO:HALO + tq, :] = a_ref[...].astype(F32) * _sigmoid(g_ref[...].astype(F32))
        up = ap_ref[...].astype(F32) * _sigmoid(gp_ref[...].astype(F32))
        scr[0:HALO, :] = jnp.where(i > 0, up, 0.0)
        _phases(scr, sh)
        _spread(w_ref, wb, taps)
        bias = jnp.broadcast_to(bdw_ref[...], (8, C))
        for r in range(tq // CONV_ROWS):
            accs = [bias] * (CONV_ROWS // 8)
            for j in range(taps):
                wj = wb[j]
                accs = [acc + wj * _tap(scr, sh, off + j + r * CONV_ROWS + 8 * k, 8) for k, acc in enumerate(accs)]
            for k, acc in enumerate(accs):
                cv_ref[r * CONV_ROWS + 8 * k:r * CONV_ROWS + 8 * k + 8, :] = acc
        xhat, _ = _ln_stats(cv_ref[...])
        ln = xhat * lg_ref[...] + lb_ref[...]
        s_ref[...] = (ln * _sigmoid(ln)).astype(BF16)

    prev = lambda col: pl.BlockSpec((HALO, C), lambda i: (jnp.maximum(i * nh - 1, 0), col))
    cur = lambda col: pl.BlockSpec((tq, C), lambda i: (i, col))
    return _pc(body, "dwconv_fwd", (T // tq,),
               [cur(0), cur(1), prev(0), prev(1), _const((HALO, C)), _const((1, C)), _const((1, C)), _const((1, C))],
               [_rows(tq, C), _rows(tq, C)], [_sds((T, C), F32), _sds((T, C), BF16)],
               scratch=[pltpu.VMEM((HALO + tq, C), F32), pltpu.VMEM((7, HALO + tq, C), F32), pltpu.VMEM((taps, 8, C), F32)],
               sem=("parallel",))(h, h, h, h, w_dw, b_dw, ln_g, ln_b)


def mm_res_ln(a, w, res, g, b, alpha, bias, name):
    T, K = a.shape
    ks = K // NS
    D = res.shape[1]
    tm = _tile(T)

    def body(*refs):
        a_ref, w_ref, res_ref, g_ref, b_ref = refs[:5]
        n = 5
        if bias is not None:
            bias_ref = refs[5]
            n = 6
        pre_ref, xo_ref, xb_ref = refs[n:n + 3]
        acc = jnp.dot(a_ref[...], _rows_joined(w_ref), preferred_element_type=F32)
        if bias is not None:
            acc = acc + bias_ref[...]
        pre = alpha * res_ref[...] + acc
        xhat, _ = _ln_stats(pre)
        xo = xhat * g_ref[...] + b_ref[...]
        pre_ref[...] = pre
        xo_ref[...] = xo
        xb_ref[...] = xo.astype(BF16)

    ins = [_rows(tm, K), _wspec(w), _rows(tm, D), _const((1, D)), _const((1, D))]
    args = [a, w[0], res, g, b]
    if bias is not None:
        ins.append(_const((1, D)))
        args.append(bias)
    return _pc(body, name, (T // tm,), ins, [_rows(tm, D)] * 3, [_sds((T, D), F32), _sds((T, D), F32), _sds((T, D), BF16)],
               sem=("parallel",))(*args)


def mlp_up_fwd(xb, w_up, name):
    T, D = xb.shape
    fs = w_up[0].shape[2]
    tm = _tile(T)

    def body(x_ref, w_ref, r_ref):
        x = x_ref[...]
        for j in range(NS):
            m = jnp.maximum(jnp.dot(x, w_ref[j], preferred_element_type=F32), 0.0)
            r_ref[:, j * fs:(j + 1) * fs] = (m * m).astype(BF16)

    return _pc(body, name, (T // tm,), [_rows(tm, D), _wspec(w_up)], _rows(tm, NS * fs), _sds((T, NS * fs), BF16),
               sem=("parallel",))(xb, w_up[0])


def ple_fwd(x, xb, p, layer, w_proj, w_gate, target, name):
    T, D = x.shape
    P = p.shape[2]
    ds = D // NS
    tm = _tile(T)
    last = target is not None

    def body(*refs):
        x_ref, xb_ref, p_ref, wp_ref, wg_ref = refs[:5]
        n = 5
        if last:
            t_ref = refs[5]
            n = 6
        o_ref, o2_ref, pp_ref, gl_ref = refs[n:n + 4]
        gl = jnp.dot(xb_ref[...], _rows_joined(wg_ref), preferred_element_type=F32)
        gl_ref[...] = gl.astype(BF16)
        sg = _sigmoid(gl)
        pb = p_ref[...].astype(BF16)
        sq = jnp.zeros((1, 1), F32)
        for j in range(NS):
            sl = slice(j * ds, (j + 1) * ds)
            pp = jnp.dot(pb, wp_ref[j], preferred_element_type=F32)
            pp_ref[:, sl] = pp.astype(BF16)
            out = x_ref[:, sl] + pp * sg[:, sl]
            if last:
                err = out - t_ref[:, sl]
                o_ref[:, sl] = err * (1.0 / D)
                e2 = jnp.sum(err * err, axis=0, keepdims=True)
                sq = sq + jnp.sum(e2, axis=1, keepdims=True)
            else:
                o_ref[:, sl] = out
                o2_ref[:, sl] = out.astype(BF16)
        if last:
            _acc_rows(o2_ref, jnp.broadcast_to(sq * (0.5 / D), (8, 128)), pl.program_id(0) == 0)

    ins = [_rows(tm, D), _rows(tm, D), pl.BlockSpec((None, tm, P), lambda i: (layer, i, 0)), _wspec(w_proj), _wspec(w_gate)]
    args = [x, xb, p, w_proj[0], w_gate[0]]
    if last:
        ins.append(_rows(tm, D))
        args.append(target)
        outs = [_rows(tm, D), _const((8, 128)), _rows(tm, D), _rows(tm, D)]
        shapes = [_sds((T, D), F32), _sds((8, 128), F32), _sds((T, D), BF16), _sds((T, D), BF16)]
    else:
        outs = [_rows(tm, D)] * 4
        shapes = [_sds((T, D), F32), _sds((T, D), BF16), _sds((T, D), BF16), _sds((T, D), BF16)]
    return _pc(body, name, (T // tm,), ins, outs, shapes, sem=("arbitrary",) if last else ("parallel",))(*args)


def _rope(x, cs_ref, sign):
    c = cs_ref[0]
    s = cs_ref[1] * sign
    lane = lax.broadcasted_iota(jnp.int32, c.shape, 1)
    first = (lane % HEAD) < (ROPE // 2)
    outs = []
    for gq in range(x.shape[1] // 128):
        xg = x[:, gq * 128:(gq + 1) * 128]
        sw = jnp.where(first, pltpu.roll(xg, 128 - ROPE // 2, 1), pltpu.roll(xg, ROPE // 2, 1))
        outs.append(xg * c + sw * s)
    return outs


def qkv_fwd(xb, w_q, w_k, w_v, cs):
    T, D = xb.shape
    ds = D // NS
    HD, KVD = w_q[0].shape[2], w_k[0].shape[2]
    tm = _tile(T)
    scale = 1.0 / (HEAD ** 0.5)

    def body(x_ref, wq_ref, wk_ref, wv_ref, cs_ref, q_ref, k_ref, v_ref):
        def proj(w_ref):
            return jnp.dot(x_ref[...], _rows_joined(w_ref), preferred_element_type=F32)

        for gq, val in enumerate(_rope(proj(wq_ref), cs_ref, 1.0)):
            q_ref[:, gq * 128:(gq + 1) * 128] = (val * scale).astype(BF16)
        for gq, val in enumerate(_rope(proj(wk_ref), cs_ref, 1.0)):
            k_ref[:, gq * 128:(gq + 1) * 128] = val.astype(BF16)
        v_ref[...] = proj(wv_ref).astype(BF16)

    cs_spec = pl.BlockSpec((2, tm, 128), lambda i: (0, i, 0))
    return _pc(body, "qkv_fwd", (T // tm,), [_rows(tm, D), _wspec(w_q), _wspec(w_k), _wspec(w_v), cs_spec],
               [_rows(tm, HD), _rows(tm, KVD), _rows(tm, KVD)],
               [_sds((T, HD), BF16), _sds((T, KVD), BF16), _sds((T, KVD), BF16)], sem=("parallel",))(
                   xb, w_q[0], w_k[0], w_v[0], cs)


def _band_mask(n):
    row = lax.broadcasted_iota(jnp.int32, (BLK, 2 * BLK), 0)
    col = lax.broadcasted_iota(jnp.int32, (BLK, 2 * BLK), 1)
    return (col > row) & (col <= row + BLK) & ((col >= BLK) | (n > 0))


def _head(h):
    return slice(h * HEAD, (h + 1) * HEAD)


def _softmax_sink(s, sink):
    m = jnp.maximum(jnp.max(s, axis=-1, keepdims=True), sink)
    e = jnp.exp(s - m)
    es = jnp.exp(sink - m)
    den = jnp.sum(e, axis=-1, keepdims=True) + es
    return e / den, es / den


def attn_fwd(q, k, v, sinks):
    T, HD = q.shape
    KVD = k.shape[1]
    NKV = KVD // HEAD
    G = HD // KVD

    def body(s_ref, q_ref, kc_ref, kp_ref, vc_ref, vp_ref, o_ref):
        valid = _band_mask(pl.program_id(0))
        for kh in range(NKV):
            k2 = jnp.concatenate([kp_ref[:, _head(kh)], kc_ref[:, _head(kh)]], axis=0)
            v2 = jnp.concatenate([vp_ref[:, _head(kh)], vc_ref[:, _head(kh)]], axis=0)
            hs = [kh * G + gq for gq in range(G)]
            sc = [lax.dot_general(q_ref[:, _head(hh)], k2, NT, preferred_element_type=F32) for hh in hs]
            pb = [_softmax_sink(jnp.where(valid, s, NEG), s_ref[0, hh])[0].astype(BF16) for s, hh in zip(sc, hs)]
            for p, hh in zip(pb, hs):
                o_ref[:, _head(hh)] = jnp.dot(p, v2, preferred_element_type=F32).astype(BF16)

    cur = lambda n_: pl.BlockSpec((BLK, n_), lambda n: (n, 0))
    prev = lambda n_: pl.BlockSpec((BLK, n_), lambda n: (jnp.maximum(n - 1, 0), 0))
    return _pc(body, "attn_fwd", (T // BLK,),
               [pl.BlockSpec(memory_space=pltpu.SMEM), cur(HD), cur(KVD), prev(KVD), cur(KVD), prev(KVD)],
               cur(HD), _sds((T, HD), BF16), sem=("parallel",))(sinks, q, k, k, v, v)


def ple_bwd(dxo, pp, gl, w_gate, name):
    T, D = dxo.shape
    ds = D // NS
    tm = _tile(T)

    def body(d_ref, pp_ref, gl_ref, wg_ref, dpp_ref, dgl_ref, dx_ref):
        d = d_ref[...]
        sg = _sigmoid(gl_ref[...].astype(F32))
        dpp_ref[...] = (d * sg).astype(BF16)
        dgl = (d * pp_ref[...].astype(F32) * sg * (1.0 - sg)).astype(BF16)
        dgl_ref[...] = dgl
        dx_ref[...] = d + lax.dot_general(dgl, _rows_joined(wg_ref), NT, preferred_element_type=F32)

    return _pc(body, name, (T // tm,), [_rows(tm, D)] * 3 + [_wspec(w_gate)], [_rows(tm, D)] * 3,
               [_sds((T, D), BF16), _sds((T, D), BF16), _sds((T, D), F32)], sem=("parallel",))(dxo, pp, gl, w_gate[0])


def mlp_bwd1(dy, pre, g, r, w_down, name):
    T, D = dy.shape
    fs = w_down[2]
    tm = _tile(T)

    def body(dy_ref, pre_ref, g_ref, r_ref, w_ref, dw_ref, dwb_ref, dm_ref, dg_ref, db_ref):
        dw, dg, db = _ln_bwd(dy_ref[...], pre_ref[...], g_ref[...])
        first = pl.program_id(0) == 0
        _acc_rows(dg_ref, dg, first)
        _acc_rows(db_ref, db, first)
        dwb = dw.astype(BF16)
        dw_ref[...] = dw
        dwb_ref[...] = dwb
        for j in range(NS):
            sl = slice(j * fs, (j + 1) * fs)
            dr = lax.dot_general(dwb, w_ref[j], NT, preferred_element_type=F32)
            dm_ref[:, sl] = (dr * (2.0 * jnp.sqrt(r_ref[:, sl].astype(F32)))).astype(BF16)

    return _pc(body, name, (T // tm,), [_rows(tm, D), _rows(tm, D), _const((1, D)), _rows(tm, NS * fs), _wspec(w_down)],
               [_rows(tm, D), _rows(tm, D), _rows(tm, NS * fs), _const((1, D)), _const((1, D))],
               [_sds((T, D), F32), _sds((T, D), BF16), _sds((T, NS * fs), BF16), _sds((1, D), F32), _sds((1, D), F32)],
               sem=("arbitrary",))(dy, pre, g, r, w_down[0])


def mlp_bwd2(dpre, dm, w_up, alpha, pre_mix, g_mix, w_mix, name):
    T, D = dpre.shape
    fs = w_up[0].shape[2]
    ms = w_mix[2]
    tm = _tile(T)

    def body(dp_ref, dm_ref, wu_ref, pre_ref, g_ref, wm_ref, dw_ref, dwb_ref, do_ref, dg_ref, db_ref, dc_ref):
        dy = alpha * dp_ref[...]
        for j in range(NS):
            dy = dy + lax.dot_general(dm_ref[:, j * fs:(j + 1) * fs], wu_ref[j], NT, preferred_element_type=F32)
        dw, dg, db = _ln_bwd(dy, pre_ref[...], g_ref[...])
        first = pl.program_id(0) == 0
        _acc_rows(dg_ref, dg, first)
        _acc_rows(db_ref, db, first)
        _acc_rows(dc_ref, jnp.sum(dw, axis=0, keepdims=True), first)
        dwb = dw.astype(BF16)
        dw_ref[...] = dw
        dwb_ref[...] = dwb
        do_ref[...] = lax.dot_general(dwb, _rows_joined(wm_ref), NT, preferred_element_type=F32).astype(BF16)

    return _pc(body, name, (T // tm,),
               [_rows(tm, D), _rows(tm, NS * fs), _wspec(w_up), _rows(tm, D), _const((1, D)), _wspec(w_mix)],
               [_rows(tm, D), _rows(tm, D), _rows(tm, NS * ms), _const((1, D)), _const((1, D)), _const((1, D))],
               [_sds((T, D), F32), _sds((T, D), BF16), _sds((T, NS * ms), BF16)] + [_sds((1, D), F32)] * 3,
               sem=("arbitrary",))(dpre, dm, w_up[0], pre_mix, g_mix, w_mix[0])


def attn_bwd(q, k, v, do, sinks):
    T, HD = q.shape
    KVD = k.shape[1]
    NH, NKV = HD // HEAD, KVD // HEAD
    G = NH // NKV
    nb = T // BLK

    def body(s_ref, q_ref, do_ref, kc_ref, kp_ref, vc_ref, vp_ref, dq_ref, dk_ref, dv_ref, ds_ref, ck, cv):
        n = pl.program_id(0)

        @pl.when(n == 0)
        def _():
            ck[...] = jnp.zeros_like(ck)
            cv[...] = jnp.zeros_like(cv)
            ds_ref[...] = jnp.zeros_like(ds_ref)

        @pl.when(n < nb)
        def _():
            valid = _band_mask(n)
            for kh in range(NKV):
                kv = _head(kh)
                k2 = jnp.concatenate([kp_ref[:, kv], kc_ref[:, kv]], axis=0)
                v2 = jnp.concatenate([vp_ref[:, kv], vc_ref[:, kv]], axis=0)
                hs = [kh * G + gq for gq in range(G)]
                qs = [q_ref[:, _head(hh)] for hh in hs]
                dos = [do_ref[:, _head(hh)] for hh in hs]
                sc = [lax.dot_general(qh, k2, NT, preferred_element_type=F32) for qh in qs]
                dp = [lax.dot_general(doh, v2, NT, preferred_element_type=F32) for doh in dos]
                pr = [_softmax_sink(jnp.where(valid, s, NEG), s_ref[0, hh]) for s, hh in zip(sc, hs)]
                delta = [jnp.sum(p * d, axis=-1, keepdims=True) for (p, _), d in zip(pr, dp)]
                dsb = [(p * (d - dl)).astype(BF16) for (p, _), d, dl in zip(pr, dp, delta)]
                pb = [p.astype(BF16) for p, _ in pr]
                for (_, ps), dl, hh in zip(pr, delta, hs):
                    ds_ref[hh:hh + 1, :] += jnp.broadcast_to(-jnp.sum(ps * dl, axis=0, keepdims=True), (1, 128))
                for d, hh in zip(dsb, hs):
                    dq_ref[:, _head(hh)] = jnp.dot(d, k2, preferred_element_type=F32)
                dk2 = lax.dot_general(jnp.concatenate(dsb, axis=0), jnp.concatenate(qs, axis=0), TN,
                                      preferred_element_type=F32)
                dv2 = lax.dot_general(jnp.concatenate(pb, axis=0), jnp.concatenate(dos, axis=0), TN,
                                      preferred_element_type=F32)
                dk_ref[:, kv] = ck[:, kv] + dk2[0:BLK]
                dv_ref[:, kv] = cv[:, kv] + dv2[0:BLK]
                ck[:, kv] = dk2[BLK:2 * BLK]
                cv[:, kv] = dv2[BLK:2 * BLK]

        @pl.when(n == nb)
        def _():
            dk_ref[...] = ck[...]
            dv_ref[...] = cv[...]

    qcur = pl.BlockSpec((BLK, HD), lambda n: (jnp.minimum(n, nb - 1), 0))
    kcur = pl.BlockSpec((BLK, KVD), lambda n: (jnp.minimum(n, nb - 1), 0))
    kprev = pl.BlockSpec((BLK, KVD), lambda n: (jnp.maximum(n - 1, 0), 0))
    return _pc(body, "attn_bwd", (nb + 1,),
               [pl.BlockSpec(memory_space=pltpu.SMEM), qcur, qcur, kcur, kprev, kcur, kprev],
               [qcur, kprev, kprev, _const((NH, 128))],
               [_sds((T, HD), F32), _sds((T, KVD), F32), _sds((T, KVD), F32), _sds((NH, 128), F32)],
               scratch=[pltpu.VMEM((BLK, KVD), F32), pltpu.VMEM((BLK, KVD), F32)],
               sem=("arbitrary",))(sinks, q, do, k, k, v, v)


def qkv_bwd(dq, dk, dv, dpre_mix, w_q, w_k, w_v, cs, alpha):
    T, HD = dq.shape
    KVD = dk.shape[1]
    D = dpre_mix.shape[1]
    ds = D // NS
    tm = _tile(T)
    scale = 1.0 / (HEAD ** 0.5)

    def body(dq_ref, dk_ref, dv_ref, dp_ref, wq_ref, wk_ref, wv_ref, cs_ref, dqb_ref, dkb_ref, dvb_ref, dx_ref):
        for gq, val in enumerate(_rope(dq_ref[...], cs_ref, -1.0)):
            dqb_ref[:, gq * 128:(gq + 1) * 128] = (val * scale).astype(BF16)
        for gq, val in enumerate(_rope(dk_ref[...], cs_ref, -1.0)):
            dkb_ref[:, gq * 128:(gq + 1) * 128] = val.astype(BF16)
        dvb_ref[...] = dv_ref[...].astype(BF16)
        dqb, dkb, dvb = dqb_ref[...], dkb_ref[...], dvb_ref[...]
        dx_ref[...] = (alpha * dp_ref[...]
                       + lax.dot_general(dqb, _rows_joined(wq_ref), NT, preferred_element_type=F32)
                       + lax.dot_general(dkb, _rows_joined(wk_ref), NT, preferred_element_type=F32)
                       + lax.dot_general(dvb, _rows_joined(wv_ref), NT, preferred_element_type=F32))

    cs_spec = pl.BlockSpec((2, tm, 128), lambda i: (0, i, 0))
    return _pc(body, "qkv_bwd", (T // tm,),
               [_rows(tm, HD), _rows(tm, KVD), _rows(tm, KVD), _rows(tm, D), _wspec(w_q), _wspec(w_k), _wspec(w_v), cs_spec],
               [_rows(tm, HD), _rows(tm, KVD), _rows(tm, KVD), _rows(tm, D)],
               [_sds((T, HD), BF16), _sds((T, KVD), BF16), _sds((T, KVD), BF16), _sds((T, D), F32)],
               sem=("parallel",))(dq, dk, dv, dpre_mix, w_q[0], w_k[0], w_v[0], cs)


def conv_mid_bwd(ds, cv, ln_g, ln_b):
    T, C = cv.shape
    tm = _tile(T)

    def body(ds_ref, cv_ref, g_ref, b_ref, dcv_ref, dg_ref, db_ref, dc_ref):
        xhat, _ = _ln_stats(cv_ref[...])
        ln = xhat * g_ref[...] + b_ref[...]
        sg = _sigmoid(ln)
        dl = ds_ref[...].astype(F32) * (sg * (1.0 + ln * (1.0 - sg)))
        dcv, dg, db = _ln_bwd(dl, cv_ref[...], g_ref[...])
        first = pl.program_id(0) == 0
        _acc_rows(dg_ref, dg, first)
        _acc_rows(db_ref, db, first)
        _acc_rows(dc_ref, jnp.sum(dcv, axis=0, keepdims=True), first)
        dcv_ref[...] = dcv

    return _pc(body, "conv_mid_bwd", (T // tm,), [_rows(tm, C), _rows(tm, C), _const((1, C)), _const((1, C))],
               [_rows(tm, C), _const((1, C)), _const((1, C)), _const((1, C))],
               [_sds((T, C), F32)] + [_sds((1, C), F32)] * 3, sem=("arbitrary",))(ds, cv, ln_g, ln_b)


def dwconv_bwd(dcv, h, w_dw, taps):
    T, C = dcv.shape
    tq = _tile(T)
    nh = tq // HALO
    nblk = T // tq
    off = HALO - (taps - 1)

    def body(d_ref, dn_ref, a_ref, g_ref, ap_ref, gp_ref, w_ref, dh_ref, dw_ref, dbi_ref, su, sus, sd, sds, wb):
        i = pl.program_id(0)
        su[HALO:HALO + tq, :] = a_ref[...].astype(F32) * _sigmoid(g_ref[...].astype(F32))
        up = ap_ref[...].astype(F32) * _sigmoid(gp_ref[...].astype(F32))
        su[0:HALO, :] = jnp.where(i > 0, up, 0.0)
        sd[0:tq, :] = d_ref[...]
        sd[tq:tq + HALO, :] = jnp.where(i < nblk - 1, dn_ref[...], 0.0)
        _phases(su, sus)
        _phases(sd, sds)

        @pl.when(i == 0)
        def _():
            dw_ref[...] = jnp.zeros_like(dw_ref)

        for j in range(taps):
            dw_ref[j:j + 1, :] += jnp.sum(d_ref[...] * _tap(su, sus, off + j, tq), axis=0, keepdims=True)
        sa = jnp.zeros((1, C), F32)
        sb = jnp.zeros((1, C), F32)
        _spread(w_ref, wb, taps)
        for r in range(tq // CONV_ROWS):
            rows = slice(r * CONV_ROWS, (r + 1) * CONV_ROWS)
            dus = [wb[0] * _tap(sd, sds, taps - 1 + r * CONV_ROWS + 8 * k, 8) for k in range(CONV_ROWS // 8)]
            for j in range(1, taps):
                wj = wb[j]
                dus = [acc + wj * _tap(sd, sds, taps - 1 - j + r * CONV_ROWS + 8 * k, 8) for k, acc in enumerate(dus)]
            du = jnp.concatenate(dus, axis=0)
            a = a_ref[rows, :].astype(F32)
            sg = _sigmoid(g_ref[rows, :].astype(F32))
            da = du * sg
            dgt = du * a * sg * (1.0 - sg)
            dh_ref[rows, 0:C] = da.astype(BF16)
            dh_ref[rows, C:2 * C] = dgt.astype(BF16)
            sa = sa + jnp.sum(da, axis=0, keepdims=True)
            sb = sb + jnp.sum(dgt, axis=0, keepdims=True)
        first = i == 0
        _acc_rows(dbi_ref.at[:, 0:C], sa, first)
        _acc_rows(dbi_ref.at[:, C:2 * C], sb, first)

    prev = lambda col: pl.BlockSpec((HALO, C), lambda i: (jnp.maximum(i * nh - 1, 0), col))
    nxt = pl.BlockSpec((HALO, C), lambda i: (jnp.minimum((i + 1) * nh, T // HALO - 1), 0))
    cur = lambda col: pl.BlockSpec((tq, C), lambda i: (i, col))
    return _pc(body, "dwconv_bwd", (nblk,),
               [cur(0), nxt, cur(0), cur(1), prev(0), prev(1), _const((HALO, C))],
               [_rows(tq, 2 * C), _const((HALO, C)), _const((1, 2 * C))],
               [_sds((T, 2 * C), BF16), _sds((HALO, C), F32), _sds((1, 2 * C), F32)],
               scratch=[pltpu.VMEM((HALO + tq, C), F32), pltpu.VMEM((7, HALO + tq, C), F32),
                        pltpu.VMEM((HALO + tq, C), F32), pltpu.VMEM((7, HALO + tq, C), F32), pltpu.VMEM((taps, 8, C), F32)],
               sem=("arbitrary",))(dcv, dcv, h, h, h, h, w_dw)


def conv_in_bwd(dh, dpre_mix, w_in, alpha):
    T, D = dpre_mix.shape
    nw = w_in[0].shape[2]
    tm = _tile(T)

    def body(dh_ref, dp_ref, w_ref, dx_ref):
        acc = alpha * dp_ref[...]
        for j in range(NS):
            acc = acc + lax.dot_general(dh_ref[:, j * nw:(j + 1) * nw], w_ref[j], NT, preferred_element_type=F32)
        dx_ref[...] = acc

    return _pc(body, "conv_in_bwd", (T // tm,), [_rows(tm, NS * nw), _rows(tm, D), _wspec(w_in)], _rows(tm, D),
               _sds((T, D), F32), sem=("parallel",))(dh, dpre_mix, w_in[0])


def wgrad(a, b, row_sharded, name, into):
    prev, out_shape, off = into
    T, Ka = a.shape
    Nb = b.shape[1]
    tt = min(1024, T)
    nt = T // tt
    ka, tn = min(Ka, 1024), min(Nb, 1024)
    if row_sharded:
        sr = Ka // NS
        spb = max(ka // sr, 1)
        rb = ka // spb
        assert out_shape[2] == Nb and off % rb == 0
        out_spec = pl.BlockSpec((spb, rb, tn), lambda i, j, t: (i, off // rb, j))
    else:
        sc = Nb // NS
        spb = max(tn // sc, 1)
        rb = ka
        assert out_shape[2] == sc and off % ka == 0
        out_spec = pl.BlockSpec((spb, ka, tn // spb), lambda i, j, t: (j, off // ka + i, 0))

    def body(a_ref, b_ref, *rest):
        o_ref, acc = rest[-2:]
        t = pl.program_id(2)
        av = a_ref[...]
        if av.dtype != BF16:
            av = av.astype(BF16)
        d = lax.dot_general(av, b_ref[...], TN, preferred_element_type=F32)

        @pl.when(t == 0)
        def _():
            acc[...] = d

        @pl.when(t > 0)
        def _():
            acc[...] += d

        @pl.when(t == nt - 1)
        def _():
            for s in range(spb):
                if row_sharded:
                    o_ref[s] = acc[s * rb:(s + 1) * rb, :].astype(BF16)
                else:
                    o_ref[s] = acc[:, s * (tn // spb):(s + 1) * (tn // spb)].astype(BF16)

    ins = [pl.BlockSpec((tt, ka), lambda i, j, t: (t, i)), pl.BlockSpec((tt, tn), lambda i, j, t: (t, j))]
    args = [a, b]
    kw = {}
    if prev is not None:
        ins.append(ANY)
        args.append(prev)
        kw["input_output_aliases"] = {2: 0}
    return _pc(body, name, (Ka // ka, Nb // tn, nt), ins, out_spec, _sds(out_shape, BF16),
               scratch=[pltpu.VMEM((ka, tn), F32)], sem=("parallel", "parallel", "arbitrary"), **kw)(*args)


def _adamw_math(w, g, m, v):
    c1 = 1.0 - ADAM_B1 ** ADAM_STEP
    c2 = 1.0 - ADAM_B2 ** ADAM_STEP
    mn = ADAM_B1 * m + (1.0 - ADAM_B1) * g
    vn = ADAM_B2 * v + (1.0 - ADAM_B2) * (g * g)
    return -ADAM_LR * ((mn / c1) / (jnp.sqrt(vn / c2) + ADAM_EPS) + ADAM_WD * w), mn, vn


def adamw_layer(w, m, v, layer, gbuf, off, prev, name):
    L, R, W = w.shape
    tr = 256
    assert R % tr == 0 and off % tr == 0

    def body(w_ref, g_ref, m_ref, v_ref, *rest):
        go_ref, d_ref, mo_ref, vo_ref = rest[-4:]
        g = g_ref[...]
        go_ref[...] = g
        d_ref[...], mo_ref[...], vo_ref[...] = _adamw_math(w_ref[...], g, m_ref[...], v_ref[...])

    lay = pl.BlockSpec((None, tr, W), lambda r: (layer, r, 0))
    ins = [lay, pl.BlockSpec((tr, W), lambda r: (off // tr + r, 0)), lay, lay]
    args = [w, gbuf, m, v]
    kw = {}
    if prev is not None:
        ins += [ANY] * 4
        args += list(prev)
        kw["input_output_aliases"] = {4 + k: k for k in range(4)}
    return _pc(body, name, (R // tr,), ins, [lay] * 4, [_sds((L, R, W), F32)] * 4, sem=("parallel",), **kw)(*args)


def adamw_many(ws, gs, ms, vs):
    n = len(ws)

    def body(*refs):
        for k in range(n):
            d, mn, vn = _adamw_math(refs[k][...], refs[n + k][...], refs[2 * n + k][...], refs[3 * n + k][...])
            refs[4 * n + k][...] = d
            refs[5 * n + k][...] = mn
            refs[6 * n + k][...] = vn

    outs = pl.pallas_call(body, name="adamw_small", out_shape=[_sds(a.shape, F32) for a in ws] * 3)(*ws, *gs, *ms, *vs)
    return outs[:n], outs[n:2 * n], outs[2 * n:]


def _rope_tables(T):
    pos = jnp.arange(T, dtype=F32)
    inv_freq = ROPE_THETA ** (-jnp.arange(0, ROPE, 2, dtype=F32) / ROPE)
    ang = pos[:, None] * inv_freq[None, :]
    cos, sin = jnp.cos(ang), jnp.sin(ang)
    pad = HEAD - ROPE
    c = jnp.concatenate([cos, cos, jnp.ones((T, pad), F32)], axis=1)
    s = jnp.concatenate([-sin, sin, jnp.zeros((T, pad), F32)], axis=1)
    return jnp.stack([jnp.tile(c, (1, 128 // HEAD)), jnp.tile(s, (1, 128 // HEAD))])


def _local_step(x, p, target, W, small, lay, hook=None):
    if hook is None:
        hook = lambda stage, after, G, sg=None: None
    T, D = x.shape
    depth = small["mix_ln_g"].shape[0]
    alpha = float((2 * depth) ** 0.25)
    taps = small["taps"]
    row = lambda a, i: a[i:i + 1]
    cs = _rope_tables(T)

    x0b = x.astype(BF16)
    h = conv_in_fwd(x0b, W["conv_w_in"], small["conv_b_in"])
    cv, s = dwconv_fwd(h, small["conv_w_dw"], small["conv_b_dw"], small["conv_ln_g"], small["conv_ln_b"], taps)
    pre_mix0, x1, x1b = mm_res_ln(s, W["conv_w_out"], x, row(small["mix_ln_g"], 0), row(small["mix_ln_b"], 0), alpha,
                                  small["conv_b_out"], "conv_out_fwd")
    hook("weights1", x1b, None)
    r0 = mlp_up_fwd(x1b, W["mlp_w_up0"], "mlp_up_fwd0")
    pre_mlp0, x2, x2b = mm_res_ln(r0, W["mlp_w_down0"], x1, row(small["mlp_ln_g"], 0), row(small["mlp_ln_b"], 0), alpha,
                                  None, "mlp_down_fwd0")
    x3, x3b, pp0, gl0 = ple_fwd(x2, x2b, p, 0, W["ple_w_proj0"], W["ple_w_gate0"], None, "ple_fwd0")

    hook("weights2", x3b, None)
    q, k, v = qkv_fwd(x3b, W["attn_w_q"], W["kv_w_k"], W["kv_w_v"], cs)
    o = attn_fwd(q, k, v, small["attn_sinks"])
    pre_mix1, x4, x4b = mm_res_ln(o, W["attn_w_o"], x3, row(small["mix_ln_g"], 1), row(small["mix_ln_b"], 1), alpha,
                                  None, "attn_out_fwd")
    r1 = mlp_up_fwd(x4b, W["mlp_w_up1"], "mlp_up_fwd1")
    pre_mlp1, x5, x5b = mm_res_ln(r1, W["mlp_w_down1"], x4, row(small["mlp_ln_g"], 1), row(small["mlp_ln_b"], 1), alpha,
                                  None, "mlp_down_fwd1")
    dx6, loss, pp1, gl1 = ple_fwd(x5, x5b, p, 1, W["ple_w_proj1"], W["ple_w_gate1"], target, "ple_fwd1")

    G, sg = {}, {}
    where = {n: (key, off) for key in lay for n, off, _ in lay[key]}
    rows_of = {key: sum(r for _, _, r in lay[key]) for key in lay}

    def wg(name, a, b, row_sharded):
        key, off = where[name]
        shape = (NS, rows_of[key], W[name][0].shape[2])
        G[key] = wgrad(a, b, row_sharded, "wg_" + name, (G.get(key), shape, off))

    dpp1, dgl1, dx5 = ple_bwd(dx6, pp1, gl1, W["ple_w_gate1"], "ple_bwd1")
    wg("ple_w_proj1", p[1], dpp1, False)
    wg("ple_w_gate1", x5b, dgl1, True)
    dpre_mlp1, dpre_mlp1b, dm1, g_mlp_g1, g_mlp_b1 = mlp_bwd1(dx5, pre_mlp1, row(small["mlp_ln_g"], 1), r1,
                                                              W["mlp_w_down1"], "mlp_bwd1_1")
    wg("mlp_w_down1", r1, dpre_mlp1b, True)
    wg("mlp_w_up1", x4b, dm1, False)
    dpre_mix1, dpre_mix1b, do, g_mix_g1, g_mix_b1, _ = mlp_bwd2(dpre_mlp1, dm1, W["mlp_w_up1"], alpha, pre_mix1,
                                                                row(small["mix_ln_g"], 1), W["attn_w_o"], "mlp_bwd2_1")
    wg("attn_w_o", o, dpre_mix1b, True)
    dq, dk, dv, dsinks = attn_bwd(q, k, v, do, small["attn_sinks"])
    dqb, dkb, dvb, dx3 = qkv_bwd(dq, dk, dv, dpre_mix1,
                                 W["attn_w_q"], W["kv_w_k"], W["kv_w_v"], cs, alpha)
    wg("attn_w_q", x3b, dqb, True)
    wg("kv_w_k", x3b, dkb, True)
    wg("kv_w_v", x3b, dvb, True)
    hook("grads2", None, G)

    dpp0, dgl0, dx2 = ple_bwd(dx3, pp0, gl0, W["ple_w_gate0"], "ple_bwd0")
    wg("ple_w_proj0", p[0], dpp0, False)
    wg("ple_w_gate0", x2b, dgl0, True)
    dpre_mlp0, dpre_mlp0b, dm0, g_mlp_g0, g_mlp_b0 = mlp_bwd1(dx2, pre_mlp0, row(small["mlp_ln_g"], 0), r0,
                                                              W["mlp_w_down0"], "mlp_bwd1_0")
    wg("mlp_w_down0", r0, dpre_mlp0b, True)
    wg("mlp_w_up0", x1b, dm0, False)
    dpre_mix0, dpre_mix0b, dsw, g_mix_g0, g_mix_b0, g_b_out = mlp_bwd2(dpre_mlp0, dm0, W["mlp_w_up0"], alpha, pre_mix0,
                                                                      row(small["mix_ln_g"], 0), W["conv_w_out"],
                                                                      "mlp_bwd2_0")
    wg("conv_w_out", s, dpre_mix0b, True)
    hook("grads1", None, G)
    dcv, g_cln_g, g_cln_b, g_b_dw = conv_mid_bwd(dsw, cv, small["conv_ln_g"], small["conv_ln_b"])
    dh, g_w_dw, g_b_in = dwconv_bwd(dcv, h, small["conv_w_dw"], taps)
    wg("conv_w_in", x0b, dh, False)

    sg["conv_b_in"] = g_b_in
    sg["conv_w_dw"] = g_w_dw
    sg["conv_b_dw"], sg["conv_ln_g"], sg["conv_ln_b"], sg["conv_b_out"] = g_b_dw, g_cln_g, g_cln_b, g_b_out
    sg["mix_ln_g"] = [g_mix_g0, g_mix_g1]
    sg["mix_ln_b"] = [g_mix_b0, g_mix_b1]
    sg["mlp_ln_g"] = [g_mlp_g0, g_mlp_g1]
    sg["mlp_ln_b"] = [g_mlp_b0, g_mlp_b1]
    sg["attn_sinks"] = dsinks[:, 0][None, :]
    sg["loss"] = loss
    hook("grads0", None, G, sg)
    grad_x = conv_in_bwd(dh, dpre_mix0, W["conv_w_in"], alpha)
    return loss, grad_x, G, sg


BUFFERS = (("b0", ("conv_w_in",)), ("a0", ("conv_w_out",)),
           ("a1", ("mlp_w_up0", "mlp_w_down0", "ple_w_gate0")), ("c1", ("ple_w_proj0",)),
           ("a2", ("mlp_w_up1", "mlp_w_down1", "ple_w_gate1", "attn_w_q", "attn_w_o")),
           ("c2", ("kv_w_k", "kv_w_v", "ple_w_proj1")))
GROUPS = (("b0", "a0"), ("a1", "c1"), ("a2", "c2"))
REDUCED = (("b0",), ("a0", "a1", "c1"), ("a2", "c2"))
ROW_SHARDED = {"mlp_w_down0", "mlp_w_down1", "ple_w_gate0", "ple_w_gate1", "conv_w_out", "attn_w_q", "attn_w_o", "kv_w_k",
               "kv_w_v"}


def _split_layers(weights):
    out = {"conv_w_in": weights["conv_w_in"][0], "conv_w_out": weights["conv_w_out"][0],
           "attn_w_q": weights["attn_w_q"][0], "attn_w_o": weights["attn_w_o"][0],
           "kv_w_k": weights["kv_w_k"], "kv_w_v": weights["kv_w_v"]}
    for n in ("mlp_w_up", "mlp_w_down", "ple_w_proj", "ple_w_gate"):
        for i in range(weights[n].shape[0]):
            out[n + str(i)] = weights[n][i]
    return out


def _layout(shards):
    lay = {}
    for key, names in BUFFERS:
        off, rows = 0, []
        for n in names:
            rows.append((n, off, shards[n].shape[0]))
            off += shards[n].shape[0]
        lay[key] = rows
    return lay


def _place():
    return lax.axis_index("x"), lax.axis_index("y"), lax.axis_index("c")


def _flip(v, f):
    return (v + f) % 2 if f else v


CHIP_FLIPS = ((1, 0), (0, 1), (1, 1))


HBM = pl.BlockSpec(memory_space=pltpu.HBM)
SEM = pl.BlockSpec(memory_space=pltpu.SEMAPHORE)
EFFECT = pltpu.SideEffectType.DATAFLOW_SIDE_EFFECTING


def _half(ref, rows, c):
    return ref.at[pl.ds(pl.multiple_of(c * (rows // 2), 16), rows // 2), :]


def _gather_copies(refs, shapes, whole, send, recv):
    x, y, c = _place()
    me = 2 * x + y
    na = len(refs)
    cps = []
    for d, (fx, fy) in enumerate(CHIP_FLIPS):
        to = (_flip(x, fx), _flip(y, fy), c)
        for k in range(na):
            mine = refs[k].at[me] if k >= na - whole else _half(refs[k].at[me], shapes[k][1], c)
            cps.append(pltpu.make_async_remote_copy(mine, mine, send.at[d * na + k], recv.at[d * na + k], device_id=to,
                                                    device_id_type=MESH))
    return cps


def gather_start(bufs, whole, after, name):
    na = len(bufs)
    shapes = [b.shape for b in bufs]
    nsem = len(CHIP_FLIPS) * na

    def body(*refs):
        ins = refs[:na]
        send, recv = refs[-(na + 3)], refs[-(na + 2)]
        token = refs[-1]
        for cp in _gather_copies(ins, shapes, whole, send, recv):
            cp.start()
        token[...] = jnp.zeros_like(token)

    args = [pltpu.with_memory_space_constraint(b, pltpu.HBM) for b in bufs]
    ins = [HBM] * na
    if after is not None:
        args.append(after)
        ins.append(ANY)
    return pl.pallas_call(
        body, name=name, in_specs=ins,
        out_specs=[SEM, SEM] + [HBM] * na + [pl.BlockSpec(memory_space=pltpu.VMEM)],
        out_shape=[pltpu.SemaphoreType.DMA((nsem,)), pltpu.SemaphoreType.DMA((nsem,))]
        + [pltpu.HBM(b.shape, b.dtype) for b in bufs] + [_sds((8, 128), F32)],
        input_output_aliases={k: k + 2 for k in range(na)},
        compiler_params=pltpu.CompilerParams(has_side_effects=EFFECT))(*args)


def gather_wait(send, recv, bufs, whole, after, name):
    na = len(bufs)
    shapes = [b.shape for b in bufs]

    def body(*refs):
        ins = refs[:na]
        send_ref, recv_ref = refs[na], refs[na + 1]
        for cp in _gather_copies(ins, shapes, whole, send_ref, recv_ref):
            cp.wait_send()
            cp.wait_recv()

    return pl.pallas_call(
        body, name=name, in_specs=[HBM] * na + [SEM, SEM, ANY], out_specs=[HBM] * na,
        out_shape=[pltpu.HBM(b.shape, b.dtype) for b in bufs], input_output_aliases={k: k for k in range(na)},
        compiler_params=pltpu.CompilerParams(has_side_effects=EFFECT))(*bufs, send, recv, after)


def sibling_forward(bufs, name):
    nb = len(bufs)

    def body(*refs):
        outs = refs[nb:2 * nb]
        send, recv = refs[2 * nb:]
        x, y, c = _place()
        cps = []
        for d, (fx, fy) in enumerate(CHIP_FLIPS):
            frm = 2 * _flip(x, fx) + _flip(y, fy)
            for k in range(nb):
                theirs = _half(outs[k].at[frm], bufs[k].shape[1], c)
                cps.append(pltpu.make_async_remote_copy(theirs, theirs, send.at[d * nb + k], recv.at[d * nb + k],
                                                        device_id=(x, y, 1 - c), device_id_type=MESH))
        for cp in cps:
            cp.start()
        for cp in cps:
            cp.wait()

    nsem = len(CHIP_FLIPS) * nb
    return pl.pallas_call(
        body, name=name, in_specs=[ANY] * nb, out_specs=[ANY] * nb, out_shape=[_sds(b.shape, b.dtype) for b in bufs],
        input_output_aliases={k: k for k in range(nb)},
        scratch_shapes=[pltpu.SemaphoreType.DMA((nsem,)), pltpu.SemaphoreType.DMA((nsem,))])(*bufs)


def pack_rows(pieces, rows, width, name):
    def body(*refs):
        o_ref = refs[-1]
        o_ref[...] = jnp.zeros_like(o_ref)
        for ref, (a, off) in zip(refs[:-1], pieces):
            o_ref[off:off + a.shape[0], 0:a.shape[1]] = ref[...]

    return pl.pallas_call(body, name=name, out_shape=_sds((rows, width), F32))(*[a for a, _ in pieces])


PEER_FLIPS = tuple((fx, fy, fc) for fx in (0, 1) for fy in (0, 1) for fc in (0, 1) if fx or fy or fc)


def _reduce_copies(parts, zones, pack, send, recv):
    x, y, c = _place()
    nb = len(parts)
    na = nb + (1 if pack is not None else 0)
    cps = []
    for f, (fx, fy, fc) in enumerate(PEER_FLIPS):
        tx, ty, tc = _flip(x, fx), _flip(y, fy), _flip(c, fc)
        for k in range(nb):
            hrows = parts[k].shape[1] // 2
            piece = parts[k].at[2 * tx + ty, pl.ds(pl.multiple_of(tc * hrows, 16), hrows), :]
            cps.append(pltpu.make_async_remote_copy(piece, zones[k].at[f], send.at[f * na + k], recv.at[f * na + k],
                                                    device_id=(tx, ty, tc), device_id_type=MESH))
        if pack is not None:
            mine = pack.at[4 * x + 2 * y + c]
            cps.append(pltpu.make_async_remote_copy(mine, mine, send.at[f * na + nb], recv.at[f * na + nb],
                                                    device_id=(tx, ty, tc), device_id_type=MESH))
    return cps


def reduce_begin(parts, pack, name):
    nb = len(parts)
    zones = [lax.empty((len(PEER_FLIPS), g.shape[1] // 2, g.shape[2]), g.dtype) for g in parts]
    arrs = list(parts) + zones + ([pack] if pack is not None else [])
    na = len(arrs)
    nsem = len(PEER_FLIPS) * (nb + (1 if pack is not None else 0))

    def body(*refs):
        ins = refs[:na]
        send, recv = refs[na], refs[na + 1]
        for cp in _reduce_copies(ins[:nb], ins[nb:2 * nb], ins[2 * nb] if pack is not None else None, send, recv):
            cp.start()
        refs[-1][...] = jnp.zeros_like(refs[-1])

    return pl.pallas_call(
        body, name=name, in_specs=[HBM] * na,
        out_specs=[SEM, SEM] + [HBM] * na + [pl.BlockSpec(memory_space=pltpu.VMEM)],
        out_shape=[pltpu.SemaphoreType.DMA((nsem,)), pltpu.SemaphoreType.DMA((nsem,))]
        + [pltpu.HBM(a.shape, a.dtype) for a in arrs] + [_sds((8, 128), F32)],
        input_output_aliases={k: k + 2 for k in range(na)},
        compiler_params=pltpu.CompilerParams(has_side_effects=EFFECT))(
            *[pltpu.with_memory_space_constraint(a, pltpu.HBM) for a in arrs])


def reduce_end(send, recv, parts, zones, pack, after, name):
    nb = len(parts)
    arrs = list(parts) + list(zones) + ([pack] if pack is not None else [])
    na = len(arrs)

    def body(*refs):
        ins = refs[:na]
        for cp in _reduce_copies(ins[:nb], ins[nb:2 * nb], ins[2 * nb] if pack is not None else None, refs[na], refs[na + 1]):
            cp.wait_send()
            cp.wait_recv()

    return pl.pallas_call(
        body, name=name, in_specs=[HBM] * na + [SEM, SEM, ANY], out_specs=[HBM] * na,
        out_shape=[pltpu.HBM(a.shape, a.dtype) for a in arrs], input_output_aliases={k: k for k in range(na)},
        compiler_params=pltpu.CompilerParams(has_side_effects=EFFECT))(*arrs, send, recv, after)


def sibling_share(halves):
    nb = len(halves)

    def body(*refs):
        outs = refs[nb:2 * nb]
        send, recv = refs[2 * nb:]
        x, y, c = _place()
        cps = []
        for k in range(nb):
            hrows = halves[k].shape[0] // 2
            mine = outs[k].at[pl.ds(pl.multiple_of(c * hrows, 8), hrows), :]
            cps.append(pltpu.make_async_remote_copy(mine, mine, send.at[k], recv.at[k], device_id=(x, y, 1 - c),
                                                    device_id_type=MESH))
        for cp in cps:
            cp.start()
        for cp in cps:
            cp.wait()

    return pl.pallas_call(
        body, name="sibling_share", in_specs=[ANY] * nb, out_specs=[ANY] * nb,
        out_shape=[_sds(h.shape, h.dtype) for h in halves], input_output_aliases={k: k for k in range(nb)},
        scratch_shapes=[pltpu.SemaphoreType.DMA((nb,)), pltpu.SemaphoreType.DMA((nb,))])(*halves)


def _row_tile(rows):
    for cand in (512, 384, 256, 128, 64, 32, 16):
        if rows % cand == 0:
            return cand
    return rows


def piece_sum(g, z, idx, name):
    _, hrows, W = z.shape
    tr = _row_tile(hrows)
    nrb = hrows // tr

    def body(idx_ref, g_ref, z_ref, o_ref):
        acc = g_ref[...].astype(F32)
        for d in range(z.shape[0]):
            acc = acc + z_ref[d].astype(F32)
        o_ref[...] = acc

    gs = pltpu.PrefetchScalarGridSpec(
        num_scalar_prefetch=1, grid=(nrb,),
        in_specs=[pl.BlockSpec((None, tr, W), lambda i, sc: (sc[0], sc[1] * nrb + i, 0)),
                  pl.BlockSpec((z.shape[0], tr, W), lambda i, sc: (0, i, 0))],
        out_specs=pl.BlockSpec((tr, W), lambda i, sc: (sc[1] * nrb + i, 0)))
    return pl.pallas_call(body, name=name, grid_spec=gs, out_shape=_sds((2 * hrows, W), F32),
                          compiler_params=pltpu.CompilerParams(dimension_semantics=("parallel",),
                                                               vmem_limit_bytes=48 * 2 ** 20))(idx, g, z)


def small_sum(packs):
    n, R, W = packs.shape

    def body(p_ref, o_ref):
        acc = p_ref[0]
        for d in range(1, n):
            acc = acc + p_ref[d]
        o_ref[...] = acc

    return pl.pallas_call(body, name="small_sum", out_shape=_sds((R, W), F32))(packs)


WEIGHTS = ["conv_w_in", "conv_b_in", "conv_w_dw", "conv_b_dw", "conv_ln_g", "conv_ln_b", "conv_w_out", "conv_b_out", "kv_w_k",
           "kv_w_v", "attn_w_q", "attn_sinks", "attn_w_o", "mix_ln_g", "mix_ln_b", "mlp_w_up", "mlp_w_down", "mlp_ln_g",
           "mlp_ln_b", "ple_w_proj", "ple_w_gate"]
BIG = ["conv_w_in", "conv_w_out", "kv_w_k", "kv_w_v", "attn_w_q", "attn_w_o", "mlp_w_up", "mlp_w_down", "ple_w_proj",
       "ple_w_gate"]
SMALL = [n for n in WEIGHTS if n not in BIG]


def _step(x, p, target, w, m, v):
    D = x.shape[-1]
    ds = D // NS
    xq, yq, cq = _place()
    chip = 2 * xq + yq
    idx = jnp.stack([chip, cq]).astype(jnp.int32)

    shards = _split_layers(w)
    lay = _layout(shards)
    taps = w["conv_w_dw"].shape[1]
    small_loc = pack_rows([(w["conv_w_dw"][0], 0), (w["conv_b_dw"], HALO), (w["conv_ln_g"], HALO + 1), (w["conv_ln_b"], HALO + 2),
                           (w["conv_b_out"], HALO + 3), (w["conv_b_in"].reshape(2, ds), HALO + 4)], HALO + 8, ds, "pack_small")
    slot = lambda a: lax.dynamic_update_slice(lax.empty((NS,) + a.shape, a.dtype), a[None], (chip, 0, 0))
    started, token = [], None
    for gi, keys in enumerate(GROUPS):
        bufs = [slot(jnp.concatenate([shards[n].astype(BF16) for n, _, _ in lay[key]], axis=0)) for key in keys]
        if gi == 0:
            bufs.append(slot(small_loc))
        send, recv, *thru, token = gather_start(bufs, 1 if gi == 0 else 0, token, "gather_start%d" % gi)
        started.append((send, recv, thru))
    W = {}

    def arrive(gi, after):
        send, recv, thru = started[gi]
        whole = 1 if gi == 0 else 0
        got = gather_wait(send, recv, thru, whole, after, "gather_wait%d" % gi)
        nk = len(GROUPS[gi])
        for key, buf in zip(GROUPS[gi], sibling_forward(got[:nk], "sibling_forward%d" % gi)):
            for n, off, rows in lay[key]:
                W[n] = (buf, off, rows)
        return got[nk:]

    gs, = arrive(0, token)
    across = lambda rows: gs[:, rows, :].transpose(1, 0, 2).reshape(rows.stop - rows.start, D)
    small = {"taps": taps, "conv_w_dw": across(slice(0, HALO)), "conv_b_dw": across(slice(HALO, HALO + 1)),
             "conv_ln_g": across(slice(HALO + 1, HALO + 2)), "conv_ln_b": across(slice(HALO + 2, HALO + 3)),
             "conv_b_out": across(slice(HALO + 3, HALO + 4)), "conv_b_in": gs[:, HALO + 4:HALO + 6, :].reshape(1, 2 * D),
             "attn_sinks": w["attn_sinks"], "mix_ln_g": w["mix_ln_g"], "mix_ln_b": w["mix_ln_b"],
             "mlp_ln_g": w["mlp_ln_g"], "mlp_ln_b": w["mlp_ln_b"]}

    reducing = {}

    def reduce_start(gi, G, pack):
        nk = len(REDUCED[gi])
        send, recv, *thru, token = reduce_begin([G[key] for key in REDUCED[gi]], pack, "reduce_begin%d" % gi)
        reducing[gi] = (send, recv, thru[:nk], thru[nk:2 * nk], thru[2 * nk] if pack is not None else None)
        _FOLLOW.append(token)

    def small_pack(sg):
        pieces = [(sg["conv_b_in"].reshape(2, D), 0), (sg["conv_w_dw"], 2)]
        r0 = 2 + HALO
        for i, n in enumerate(("conv_b_dw", "conv_ln_g", "conv_ln_b", "conv_b_out")):
            pieces.append((sg[n], r0 + i))
        r0 += 4
        for i, n in enumerate(("mix_ln_g", "mix_ln_b", "mlp_ln_g", "mlp_ln_b")):
            pieces += [(sg[n][0], r0 + 2 * i), (sg[n][1], r0 + 2 * i + 1)]
        pieces += [(sg["attn_sinks"], r0 + 8), (sg["loss"][0:1], r0 + 9)]
        mine = pack_rows(pieces, r0 + 10, D, "pack_small_grads")
        return lax.dynamic_update_slice(lax.empty((8,) + mine.shape, F32), mine[None], (4 * xq + 2 * yq + cq, 0, 0))

    def hook(stage, after, G, sg=None):
        if stage == "weights1":
            arrive(1, after)
        elif stage == "weights2":
            arrive(2, after)
        elif stage == "grads2":
            reduce_start(2, G, None)
        elif stage == "grads1":
            reduce_start(1, G, None)
        elif stage == "grads0":
            reduce_start(0, G, small_pack(sg))

    loss, grad_x, G, sg = _local_step(x[0], p[:, 0], target[0], W, small, lay, hook)
    _FOLLOW.clear()
    nsink = w["attn_sinks"].shape[1]

    halves = {}
    for gi in (2, 1, 0):
        send, recv, parts, zones, pack = reducing[gi]
        done = reduce_end(send, recv, parts, zones, pack, grad_x, "reduce_end%d" % gi)
        nk = len(REDUCED[gi])
        for key, g_, z_ in zip(REDUCED[gi], done[:nk], done[nk:2 * nk]):
            halves[key] = piece_sum(g_, z_, idx, "piece_sum_" + key)
        if pack is not None:
            tot = small_sum(done[2 * nk])
    order = [key for key, _ in BUFFERS]
    full = sibling_share([halves[key] for key in order])

    grads, delta, new_m, new_v = {}, {}, {}, {}
    found = {n: (buf, off) for key, buf in zip(order, full) for n, off, _ in lay[key]}
    for n in BIG:
        three = lambda a: a.reshape((-1,) + a.shape[-2:])
        w3, m3, v3 = three(w[n]), three(m[n]), three(v[n])
        outs = None
        for i in range(w3.shape[0]):
            buf, off = found[n + str(i)] if n + str(i) in found else found[n]
            outs = adamw_layer(w3, m3, v3, i, buf, off, outs, "adamw_%s%d" % (n, i))
        grads[n], delta[n], new_m[n], new_v[n] = [a.reshape(w[n].shape) for a in outs]
    cols = lambda rows: lax.dynamic_slice(rows, (0, chip * ds), (rows.shape[0], ds))
    grads["conv_b_in"] = lax.dynamic_slice(tot[0:2].reshape(1, 2 * D), (0, chip * 2 * ds), (1, 2 * ds))
    grads["conv_w_dw"] = cols(tot[2:2 + taps])[None]
    r0 = 2 + HALO
    for i, n in enumerate(("conv_b_dw", "conv_ln_g", "conv_ln_b", "conv_b_out")):
        grads[n] = cols(tot[r0 + i:r0 + i + 1])
    r0 += 4
    for i, n in enumerate(("mix_ln_g", "mix_ln_b", "mlp_ln_g", "mlp_ln_b")):
        grads[n] = tot[r0 + 2 * i:r0 + 2 * i + 2]
    grads["attn_sinks"] = tot[r0 + 8:r0 + 9, 0:nsink]

    ds_, ms_, vs_ = adamw_many([w[n] for n in SMALL], [grads[n] for n in SMALL], [m[n] for n in SMALL], [v[n] for n in SMALL])
    for n, d_, m_, v_ in zip(SMALL, ds_, ms_, vs_):
        delta[n], new_m[n], new_v[n] = d_, m_, v_

    total = tot[r0 + 9, 0]
    return (total, grad_x[None], *[grads[n] for n in WEIGHTS], *[delta[n] for n in WEIGHTS], *[new_m[n] for n in WEIGHTS],
            *[new_v[n] for n in WEIGHTS])


def kernel(x, p, conv_w_in, conv_b_in, conv_w_dw, conv_b_dw, conv_ln_g, conv_ln_b, conv_w_out, conv_b_out, kv_w_k, kv_w_v, attn_w_q, attn_sinks, attn_w_o, mix_ln_g, mix_ln_b, mlp_w_up, mlp_w_down, mlp_ln_g, mlp_ln_b, ple_w_proj, ple_w_gate, loss_target, m_conv_w_in, m_conv_b_in, m_conv_w_dw, m_conv_b_dw, m_conv_ln_g, m_conv_ln_b, m_conv_w_out, m_conv_b_out, m_kv_w_k, m_kv_w_v, m_attn_w_q, m_attn_sinks, m_attn_w_o, m_mix_ln_g, m_mix_ln_b, m_mlp_w_up, m_mlp_w_down, m_mlp_ln_g, m_mlp_ln_b, m_ple_w_proj, m_ple_w_gate, v_conv_w_in, v_conv_b_in, v_conv_w_dw, v_conv_b_dw, v_conv_ln_g, v_conv_ln_b, v_conv_w_out, v_conv_b_out, v_kv_w_k, v_kv_w_v, v_attn_w_q, v_attn_sinks, v_attn_w_o, v_mix_ln_g, v_mix_ln_b, v_mlp_w_up, v_mlp_w_down, v_mlp_ln_g, v_mlp_ln_b, v_ple_w_proj, v_ple_w_gate):
    w = dict(zip(WEIGHTS, (conv_w_in, conv_b_in, conv_w_dw, conv_b_dw, conv_ln_g, conv_ln_b, conv_w_out, conv_b_out, kv_w_k,
                           kv_w_v, attn_w_q, attn_sinks, attn_w_o, mix_ln_g, mix_ln_b, mlp_w_up, mlp_w_down, mlp_ln_g, mlp_ln_b,
                           ple_w_proj, ple_w_gate)))
    m = dict(zip(WEIGHTS, (m_conv_w_in, m_conv_b_in, m_conv_w_dw, m_conv_b_dw, m_conv_ln_g, m_conv_ln_b, m_conv_w_out,
                           m_conv_b_out, m_kv_w_k, m_kv_w_v, m_attn_w_q, m_attn_sinks, m_attn_w_o, m_mix_ln_g, m_mix_ln_b,
                           m_mlp_w_up, m_mlp_w_down, m_mlp_ln_g, m_mlp_ln_b, m_ple_w_proj, m_ple_w_gate)))
    v = dict(zip(WEIGHTS, (v_conv_w_in, v_conv_b_in, v_conv_w_dw, v_conv_b_dw, v_conv_ln_g, v_conv_ln_b, v_conv_w_out,
                           v_conv_b_out, v_kv_w_k, v_kv_w_v, v_attn_w_q, v_attn_sinks, v_attn_w_o, v_mix_ln_g, v_mix_ln_b,
                           v_mlp_w_up, v_mlp_w_down, v_mlp_ln_g, v_mlp_ln_b, v_ple_w_proj, v_ple_w_gate)))
    return _step(x, p, loss_target, w, m, v)
```

```python
import functools

import jax
import jax.numpy as jnp
from jax import lax
from jax.experimental import pallas as pl
from jax.experimental.pallas import tpu as pltpu

F32 = jnp.float32
BF16 = jnp.bfloat16
NS = 4
HEAD = 64
BLK = 128
ROPE = 16
ROPE_THETA = 500000.0
LN_EPS = 1e-5
NEG = -1e30
HALO = 32
ADAM_LR, ADAM_B1, ADAM_B2, ADAM_EPS, ADAM_WD, ADAM_STEP = 0.001, 0.9, 0.999, 1e-08, 0.01, 10
MESH = pl.DeviceIdType.MESH
ANY = pl.BlockSpec(memory_space=pl.ANY)
NT = (((1,), (1,)), ((), ()))
TN = (((0,), (0,)), ((), ()))


_FOLLOW = []


def _pc(body, name, grid, in_specs, out_specs, out_shape, scratch=(), sem=None, vmem=56, **kw):
    call = lambda fn, ins: pl.pallas_call(
        fn, name=name, grid=grid, in_specs=ins, out_specs=out_specs, out_shape=out_shape,
        scratch_shapes=list(scratch),
        compiler_params=pltpu.CompilerParams(dimension_semantics=sem, vmem_limit_bytes=vmem * 2 ** 20), **kw)
    if not _FOLLOW:
        return call(body, in_specs)
    extra = list(_FOLLOW)
    _FOLLOW.clear()
    n_in = len(in_specs)

    def ordered(*refs):
        return body(*refs[:n_in], *refs[n_in + len(extra):])

    run = call(ordered, list(in_specs) + [ANY] * len(extra))
    return lambda *args: run(*args, *extra)


def _rows(tm, n):
    return pl.BlockSpec((tm, n), lambda i: (i, 0))


def _const(shape):
    return pl.BlockSpec(shape, lambda *_: (0,) * len(shape))


def _wspec(w):
    buf, off, rows = w
    assert off % rows == 0
    return pl.BlockSpec((NS, rows, buf.shape[2]), lambda *_: (0, off // rows, 0))


def _rows_joined(w_ref):
    n, r, c = w_ref.shape
    return w_ref[...].reshape(n * r, c)


def _sds(shape, dtype):
    return jax.ShapeDtypeStruct(shape, dtype)


def _tile(t, rows=256):
    return min(rows, t)


def _sigmoid(x):
    return 1.0 / (1.0 + jnp.exp(-x))


def _ln_stats(w):
    mu = jnp.mean(w, axis=-1, keepdims=True)
    xc = w - mu
    var = jnp.mean(xc * xc, axis=-1, keepdims=True)
    rstd = lax.rsqrt(var + LN_EPS)
    return xc * rstd, rstd


def _ln_bwd(dy, w, g):
    xhat, rstd = _ln_stats(w)
    dxhat = dy * g
    m1 = jnp.mean(dxhat, axis=-1, keepdims=True)
    m2 = jnp.mean(dxhat * xhat, axis=-1, keepdims=True)
    dw = rstd * (dxhat - m1 - xhat * m2)
    return dw, jnp.sum(dy * xhat, axis=0, keepdims=True), jnp.sum(dy, axis=0, keepdims=True)


def _acc_rows(ref, val, first):
    @pl.when(first)
    def _():
        ref[...] = val

    @pl.when(jnp.logical_not(first))
    def _():
        ref[...] += val


def conv_in_fwd(xb, w_in, b_in):
    T, D = xb.shape
    nw = w_in[0].shape[2]
    tm = _tile(T, 512)

    def body(x_ref, w_ref, b_ref, h_ref):
        x = x_ref[...]
        for j in range(NS):
            sl = slice(j * nw, (j + 1) * nw)
            h_ref[:, sl] = (jnp.dot(x, w_ref[j], preferred_element_type=F32) + b_ref[:, sl]).astype(BF16)

    return _pc(body, "conv_in_fwd", (T // tm,), [_rows(tm, D), _wspec(w_in), _const((1, NS * nw))],
               _rows(tm, NS * nw), _sds((T, NS * nw), BF16), sem=("parallel",))(xb, w_in[0], b_in)


CONV_ROWS = 16


def _phases(scr, sh):
    n = scr.shape[0] - 8
    for b in range(1, 8):
        sh[b - 1, 0:n, :] = scr[b:b + n, :]


def _spread(w_ref, wb, taps):
    for j in range(taps):
        wb[j] = jnp.broadcast_to(w_ref[j:j + 1, :], wb.shape[1:])


def _tap(scr, sh, o, n):
    b = o % 8
    return scr[o:o + n, :] if b == 0 else sh[b - 1, o - b:o - b + n, :]


def dwconv_fwd(h, w_dw, b_dw, ln_g, ln_b, taps):
    T = h.shape[0]
    C = h.shape[1] // 2
    tq = _tile(T)
    nh = tq // HALO
    off = HALO - (taps - 1)

    def body(a_ref, g_ref, ap_ref, gp_ref, w_ref, bdw_ref, lg_ref, lb_ref, cv_ref, s_ref, scr, sh, wb):
        i = pl.program_id(0)
        scr[HALO:HALO + tq, :] = a_ref[...].astype(F32) * _sigmoid(g_ref[...].astype(F32))
        up = ap_ref[...].astype(F32) * _sigmoid(gp_ref[...].astype(F32))
        scr[0:HALO, :] = jnp.where(i > 0, up, 0.0)
        _phases(scr, sh)
        _spread(w_ref, wb, taps)
        bias = jnp.broadcast_to(bdw_ref[...], (8, C))
        for r in range(tq // CONV_ROWS):
            accs = [bias] * (CONV_ROWS // 8)
            for j in range(taps):
                wj = wb[j]
                accs = [acc + wj * _tap(scr, sh, off + j + r * CONV_ROWS + 8 * k, 8) for k, acc in enumerate(accs)]
            for k, acc in enumerate(accs):
                cv_ref[r * CONV_ROWS + 8 * k:r * CONV_ROWS + 8 * k + 8, :] = acc
        xhat, _ = _ln_stats(cv_ref[...])
        ln = xhat * lg_ref[...] + lb_ref[...]
        s_ref[...] = (ln * _sigmoid(ln)).astype(BF16)

    prev = lambda col: pl.BlockSpec((HALO, C), lambda i: (jnp.maximum(i * nh - 1, 0), col))
    cur = lambda col: pl.BlockSpec((tq, C), lambda i: (i, col))
    return _pc(body, "dwconv_fwd", (T // tq,),
               [cur(0), cur(1), prev(0), prev(1), _const((HALO, C)), _const((1, C)), _const((1, C)), _const((1, C))],
               [_rows(tq, C), _rows(tq, C)], [_sds((T, C), F32), _sds((T, C), BF16)],
               scratch=[pltpu.VMEM((HALO + tq, C), F32), pltpu.VMEM((7, HALO + tq, C), F32), pltpu.VMEM((taps, 8, C), F32)],
               sem=("parallel",))(h, h, h, h, w_dw, b_dw, ln_g, ln_b)


def mm_res_ln(a, w, res, g, b, alpha, bias, name):
    T, K = a.shape
    ks = K // NS
    D = res.shape[1]
    tm = _tile(T, 512 if K <= D else 256)

    def body(*refs):
        a_ref, w_ref, res_ref, g_ref, b_ref = refs[:5]
        n = 5
        if bias is not None:
            bias_ref = refs[5]
            n = 6
        pre_ref, xo_ref, xb_ref = refs[n:n + 3]
        acc = jnp.dot(a_ref[...], _rows_joined(w_ref), preferred_element_type=F32)
        if bias is not None:
            acc = acc + bias_ref[...]
        pre = alpha * res_ref[...] + acc
        xhat, _ = _ln_stats(pre)
        xo = xhat * g_ref[...] + b_ref[...]
        pre_ref[...] = pre
        xo_ref[...] = xo
        xb_ref[...] = xo.astype(BF16)

    ins = [_rows(tm, K), _wspec(w), _rows(tm, D), _const((1, D)), _const((1, D))]
    args = [a, w[0], res, g, b]
    if bias is not None:
        ins.append(_const((1, D)))
        args.append(bias)
    return _pc(body, name, (T // tm,), ins, [_rows(tm, D)] * 3, [_sds((T, D), F32), _sds((T, D), F32), _sds((T, D), BF16)],
               sem=("parallel",))(*args)


def mlp_up_fwd(xb, w_up, name):
    T, D = xb.shape
    fs = w_up[0].shape[2]
    tm = _tile(T)

    def body(x_ref, w_ref, r_ref):
        x = x_ref[...]
        for j in range(NS):
            m = jnp.maximum(jnp.dot(x, w_ref[j], preferred_element_type=F32), 0.0)
            r_ref[:, j * fs:(j + 1) * fs] = (m * m).astype(BF16)

    return _pc(body, name, (T // tm,), [_rows(tm, D), _wspec(w_up)], _rows(tm, NS * fs), _sds((T, NS * fs), BF16),
               sem=("parallel",))(xb, w_up[0])


def ple_fwd(x, xb, p, layer, w_proj, w_gate, target, name):
    T, D = x.shape
    P = p.shape[2]
    ds = D // NS
    tm = _tile(T, 512)
    last = target is not None

    def body(*refs):
        x_ref, xb_ref, p_ref, wp_ref, wg_ref = refs[:5]
        n = 5
        if last:
            t_ref = refs[5]
            n = 6
        o_ref, o2_ref, pp_ref, gl_ref = refs[n:n + 4]
        gl = jnp.dot(xb_ref[...], _rows_joined(wg_ref), preferred_element_type=F32)
        gl_ref[...] = gl.astype(BF16)
        sg = _sigmoid(gl)
        pb = p_ref[...].astype(BF16)
        sq = jnp.zeros((1, 1), F32)
        for j in range(NS):
            sl = slice(j * ds, (j + 1) * ds)
            pp = jnp.dot(pb, wp_ref[j], preferred_element_type=F32)
            pp_ref[:, sl] = pp.astype(BF16)
            out = x_ref[:, sl] + pp * sg[:, sl]
            if last:
                err = out - t_ref[:, sl]
                o_ref[:, sl] = err * (1.0 / D)
                e2 = jnp.sum(err * err, axis=0, keepdims=True)
                sq = sq + jnp.sum(e2, axis=1, keepdims=True)
            else:
                o_ref[:, sl] = out
                o2_ref[:, sl] = out.astype(BF16)
        if last:
            _acc_rows(o2_ref, jnp.broadcast_to(sq * (0.5 / D), (8, 128)), pl.program_id(0) == 0)

    ins = [_rows(tm, D), _rows(tm, D), pl.BlockSpec((None, tm, P), lambda i: (layer, i, 0)), _wspec(w_proj), _wspec(w_gate)]
    args = [x, xb, p, w_proj[0], w_gate[0]]
    if last:
        ins.append(_rows(tm, D))
        args.append(target)
        outs = [_rows(tm, D), _const((8, 128)), _rows(tm, D), _rows(tm, D)]
        shapes = [_sds((T, D), F32), _sds((8, 128), F32), _sds((T, D), BF16), _sds((T, D), BF16)]
    else:
        outs = [_rows(tm, D)] * 4
        shapes = [_sds((T, D), F32), _sds((T, D), BF16), _sds((T, D), BF16), _sds((T, D), BF16)]
    return _pc(body, name, (T // tm,), ins, outs, shapes, sem=("arbitrary",) if last else ("parallel",))(*args)


def _rope(x, cs_ref, sign):
    c = cs_ref[0]
    s = cs_ref[1] * sign
    lane = lax.broadcasted_iota(jnp.int32, c.shape, 1)
    first = (lane % HEAD) < (ROPE // 2)
    outs = []
    for gq in range(x.shape[1] // 128):
        xg = x[:, gq * 128:(gq + 1) * 128]
        sw = jnp.where(first, pltpu.roll(xg, 128 - ROPE // 2, 1), pltpu.roll(xg, ROPE // 2, 1))
        outs.append(xg * c + sw * s)
    return outs


def qkv_fwd(xb, w_q, w_k, w_v, cs):
    T, D = xb.shape
    ds = D // NS
    HD, KVD = w_q[0].shape[2], w_k[0].shape[2]
    tm = _tile(T, 512)
    scale = 1.0 / (HEAD ** 0.5)

    def body(x_ref, wq_ref, wk_ref, wv_ref, cs_ref, q_ref, k_ref, v_ref):
        def proj(w_ref):
            return jnp.dot(x_ref[...], _rows_joined(w_ref), preferred_element_type=F32)

        for gq, val in enumerate(_rope(proj(wq_ref), cs_ref, 1.0)):
            q_ref[:, gq * 128:(gq + 1) * 128] = (val * scale).astype(BF16)
        for gq, val in enumerate(_rope(proj(wk_ref), cs_ref, 1.0)):
            k_ref[:, gq * 128:(gq + 1) * 128] = val.astype(BF16)
        v_ref[...] = proj(wv_ref).astype(BF16)

    cs_spec = pl.BlockSpec((2, tm, 128), lambda i: (0, i, 0))
    return _pc(body, "qkv_fwd", (T // tm,), [_rows(tm, D), _wspec(w_q), _wspec(w_k), _wspec(w_v), cs_spec],
               [_rows(tm, HD), _rows(tm, KVD), _rows(tm, KVD)],
               [_sds((T, HD), BF16), _sds((T, KVD), BF16), _sds((T, KVD), BF16)], sem=("parallel",))(
                   xb, w_q[0], w_k[0], w_v[0], cs)


def _band_mask(n):
    row = lax.broadcasted_iota(jnp.int32, (BLK, 2 * BLK), 0)
    col = lax.broadcasted_iota(jnp.int32, (BLK, 2 * BLK), 1)
    return (col > row) & (col <= row + BLK) & ((col >= BLK) | (n > 0))


def _head(h):
    return slice(h * HEAD, (h + 1) * HEAD)


def _softmax_sink(s, sink):
    m = jnp.maximum(jnp.max(s, axis=-1, keepdims=True), sink)
    e = jnp.exp(s - m)
    es = jnp.exp(sink - m)
    den = jnp.sum(e, axis=-1, keepdims=True) + es
    return e / den, es / den


def attn_fwd(q, k, v, sinks):
    T, HD = q.shape
    KVD = k.shape[1]
    NKV = KVD // HEAD
    G = HD // KVD

    def body(s_ref, q_ref, kc_ref, kp_ref, vc_ref, vp_ref, o_ref):
        valid = _band_mask(pl.program_id(0))
        for kh in range(NKV):
            k2 = jnp.concatenate([kp_ref[:, _head(kh)], kc_ref[:, _head(kh)]], axis=0)
            v2 = jnp.concatenate([vp_ref[:, _head(kh)], vc_ref[:, _head(kh)]], axis=0)
            hs = [kh * G + gq for gq in range(G)]
            sc = [lax.dot_general(q_ref[:, _head(hh)], k2, NT, preferred_element_type=F32) for hh in hs]
            pb = [_softmax_sink(jnp.where(valid, s, NEG), s_ref[0, hh])[0].astype(BF16) for s, hh in zip(sc, hs)]
            for p, hh in zip(pb, hs):
                o_ref[:, _head(hh)] = jnp.dot(p, v2, preferred_element_type=F32).astype(BF16)

    cur = lambda n_: pl.BlockSpec((BLK, n_), lambda n: (n, 0))
    prev = lambda n_: pl.BlockSpec((BLK, n_), lambda n: (jnp.maximum(n - 1, 0), 0))
    return _pc(body, "attn_fwd", (T // BLK,),
               [pl.BlockSpec(memory_space=pltpu.SMEM), cur(HD), cur(KVD), prev(KVD), cur(KVD), prev(KVD)],
               cur(HD), _sds((T, HD), BF16), sem=("parallel",))(sinks, q, k, k, v, v)


def ple_bwd(dxo, pp, gl, w_gate, name):
    T, D = dxo.shape
    ds = D // NS
    tm = _tile(T, 512)

    def body(d_ref, pp_ref, gl_ref, wg_ref, dpp_ref, dgl_ref, dx_ref):
        d = d_ref[...]
        sg = _sigmoid(gl_ref[...].astype(F32))
        dpp_ref[...] = (d * sg).astype(BF16)
        dgl = (d * pp_ref[...].astype(F32) * sg * (1.0 - sg)).astype(BF16)
        dgl_ref[...] = dgl
        dx_ref[...] = d + lax.dot_general(dgl, _rows_joined(wg_ref), NT, preferred_element_type=F32)

    return _pc(body, name, (T // tm,), [_rows(tm, D)] * 3 + [_wspec(w_gate)], [_rows(tm, D)] * 3,
               [_sds((T, D), BF16), _sds((T, D), BF16), _sds((T, D), F32)], sem=("parallel",))(dxo, pp, gl, w_gate[0])


def mlp_bwd1(dy, pre, g, r, w_down, name):
    T, D = dy.shape
    fs = w_down[2]
    tm = _tile(T)

    def body(dy_ref, pre_ref, g_ref, r_ref, w_ref, dw_ref, dwb_ref, dm_ref, dg_ref, db_ref):
        dw, dg, db = _ln_bwd(dy_ref[...], pre_ref[...], g_ref[...])
        first = pl.program_id(0) == 0
        _acc_rows(dg_ref, dg, first)
        _acc_rows(db_ref, db, first)
        dwb = dw.astype(BF16)
        dw_ref[...] = dw
        dwb_ref[...] = dwb
        for j in range(NS):
            sl = slice(j * fs, (j + 1) * fs)
            dr = lax.dot_general(dwb, w_ref[j], NT, preferred_element_type=F32)
            dm_ref[:, sl] = (dr * (2.0 * jnp.sqrt(r_ref[:, sl].astype(F32)))).astype(BF16)

    return _pc(body, name, (T // tm,), [_rows(tm, D), _rows(tm, D), _const((1, D)), _rows(tm, NS * fs), _wspec(w_down)],
               [_rows(tm, D), _rows(tm, D), _rows(tm, NS * fs), _const((1, D)), _const((1, D))],
               [_sds((T, D), F32), _sds((T, D), BF16), _sds((T, NS * fs), BF16), _sds((1, D), F32), _sds((1, D), F32)],
               sem=("arbitrary",))(dy, pre, g, r, w_down[0])


def mlp_bwd2(dpre, dm, w_up, alpha, pre_mix, g_mix, w_mix, name):
    T, D = dpre.shape
    fs = w_up[0].shape[2]
    ms = w_mix[2]
    tm = _tile(T)

    def body(dp_ref, dm_ref, wu_ref, pre_ref, g_ref, wm_ref, dw_ref, dwb_ref, do_ref, dg_ref, db_ref, dc_ref):
        dy = alpha * dp_ref[...]
        for j in range(NS):
            dy = dy + lax.dot_general(dm_ref[:, j * fs:(j + 1) * fs], wu_ref[j], NT, preferred_element_type=F32)
        dw, dg, db = _ln_bwd(dy, pre_ref[...], g_ref[...])
        first = pl.program_id(0) == 0
        _acc_rows(dg_ref, dg, first)
        _acc_rows(db_ref, db, first)
        _acc_rows(dc_ref, jnp.sum(dw, axis=0, keepdims=True), first)
        dwb = dw.astype(BF16)
        dw_ref[...] = dw
        dwb_ref[...] = dwb
        do_ref[...] = lax.dot_general(dwb, _rows_joined(wm_ref), NT, preferred_element_type=F32).astype(BF16)

    return _pc(body, name, (T // tm,),
               [_rows(tm, D), _rows(tm, NS * fs), _wspec(w_up), _rows(tm, D), _const((1, D)), _wspec(w_mix)],
               [_rows(tm, D), _rows(tm, D), _rows(tm, NS * ms), _const((1, D)), _const((1, D)), _const((1, D))],
               [_sds((T, D), F32), _sds((T, D), BF16), _sds((T, NS * ms), BF16)] + [_sds((1, D), F32)] * 3,
               sem=("arbitrary",))(dpre, dm, w_up[0], pre_mix, g_mix, w_mix[0])


def attn_bwd(q, k, v, do, sinks):
    T, HD = q.shape
    KVD = k.shape[1]
    NH, NKV = HD // HEAD, KVD // HEAD
    G = NH // NKV
    nb = T // BLK

    def body(s_ref, q_ref, do_ref, kc_ref, kp_ref, vc_ref, vp_ref, dq_ref, dk_ref, dv_ref, ds_ref, ck, cv):
        n = pl.program_id(0)

        @pl.when(n == 0)
        def _():
            ck[...] = jnp.zeros_like(ck)
            cv[...] = jnp.zeros_like(cv)
            ds_ref[...] = jnp.zeros_like(ds_ref)

        @pl.when(n < nb)
        def _():
            valid = _band_mask(n)
            for kh in range(NKV):
                kv = _head(kh)
                k2 = jnp.concatenate([kp_ref[:, kv], kc_ref[:, kv]], axis=0)
                v2 = jnp.concatenate([vp_ref[:, kv], vc_ref[:, kv]], axis=0)
                hs = [kh * G + gq for gq in range(G)]
                qs = [q_ref[:, _head(hh)] for hh in hs]
                dos = [do_ref[:, _head(hh)] for hh in hs]
                sc = [lax.dot_general(qh, k2, NT, preferred_element_type=F32) for qh in qs]
                dp = [lax.dot_general(doh, v2, NT, preferred_element_type=F32) for doh in dos]
                pr = [_softmax_sink(jnp.where(valid, s, NEG), s_ref[0, hh]) for s, hh in zip(sc, hs)]
                delta = [jnp.sum(p * d, axis=-1, keepdims=True) for (p, _), d in zip(pr, dp)]
                dsb = [(p * (d - dl)).astype(BF16) for (p, _), d, dl in zip(pr, dp, delta)]
                pb = [p.astype(BF16) for p, _ in pr]
                for (_, ps), dl, hh in zip(pr, delta, hs):
                    ds_ref[hh:hh + 1, :] += jnp.broadcast_to(-jnp.sum(ps * dl, axis=0, keepdims=True), (1, 128))
                for d, hh in zip(dsb, hs):
                    dq_ref[:, _head(hh)] = jnp.dot(d, k2, preferred_element_type=F32)
                dk2 = lax.dot_general(jnp.concatenate(dsb, axis=0), jnp.concatenate(qs, axis=0), TN,
                                      preferred_element_type=F32)
                dv2 = lax.dot_general(jnp.concatenate(pb, axis=0), jnp.concatenate(dos, axis=0), TN,
                                      preferred_element_type=F32)
                dk_ref[:, kv] = ck[:, kv] + dk2[0:BLK]
                dv_ref[:, kv] = cv[:, kv] + dv2[0:BLK]
                ck[:, kv] = dk2[BLK:2 * BLK]
                cv[:, kv] = dv2[BLK:2 * BLK]

        @pl.when(n == nb)
        def _():
            dk_ref[...] = ck[...]
            dv_ref[...] = cv[...]

    qcur = pl.BlockSpec((BLK, HD), lambda n: (jnp.minimum(n, nb - 1), 0))
    kcur = pl.BlockSpec((BLK, KVD), lambda n: (jnp.minimum(n, nb - 1), 0))
    kprev = pl.BlockSpec((BLK, KVD), lambda n: (jnp.maximum(n - 1, 0), 0))
    return _pc(body, "attn_bwd", (nb + 1,),
               [pl.BlockSpec(memory_space=pltpu.SMEM), qcur, qcur, kcur, kprev, kcur, kprev],
               [qcur, kprev, kprev, _const((NH, 128))],
               [_sds((T, HD), F32), _sds((T, KVD), F32), _sds((T, KVD), F32), _sds((NH, 128), F32)],
               scratch=[pltpu.VMEM((BLK, KVD), F32), pltpu.VMEM((BLK, KVD), F32)],
               sem=("arbitrary",))(sinks, q, do, k, k, v, v)


def qkv_bwd(dq, dk, dv, dpre_mix, w_q, w_k, w_v, cs, alpha):
    T, HD = dq.shape
    KVD = dk.shape[1]
    D = dpre_mix.shape[1]
    ds = D // NS
    tm = _tile(T, 512)
    scale = 1.0 / (HEAD ** 0.5)

    def body(dq_ref, dk_ref, dv_ref, dp_ref, wq_ref, wk_ref, wv_ref, cs_ref, dqb_ref, dkb_ref, dvb_ref, dx_ref):
        for gq, val in enumerate(_rope(dq_ref[...], cs_ref, -1.0)):
            dqb_ref[:, gq * 128:(gq + 1) * 128] = (val * scale).astype(BF16)
        for gq, val in enumerate(_rope(dk_ref[...], cs_ref, -1.0)):
            dkb_ref[:, gq * 128:(gq + 1) * 128] = val.astype(BF16)
        dvb_ref[...] = dv_ref[...].astype(BF16)
        dqb, dkb, dvb = dqb_ref[...], dkb_ref[...], dvb_ref[...]
        dx_ref[...] = (alpha * dp_ref[...]
                       + lax.dot_general(dqb, _rows_joined(wq_ref), NT, preferred_element_type=F32)
                       + lax.dot_general(dkb, _rows_joined(wk_ref), NT, preferred_element_type=F32)
                       + lax.dot_general(dvb, _rows_joined(wv_ref), NT, preferred_element_type=F32))

    cs_spec = pl.BlockSpec((2, tm, 128), lambda i: (0, i, 0))
    return _pc(body, "qkv_bwd", (T // tm,),
               [_rows(tm, HD), _rows(tm, KVD), _rows(tm, KVD), _rows(tm, D), _wspec(w_q), _wspec(w_k), _wspec(w_v), cs_spec],
               [_rows(tm, HD), _rows(tm, KVD), _rows(tm, KVD), _rows(tm, D)],
               [_sds((T, HD), BF16), _sds((T, KVD), BF16), _sds((T, KVD), BF16), _sds((T, D), F32)],
               sem=("parallel",))(dq, dk, dv, dpre_mix, w_q[0], w_k[0], w_v[0], cs)


def conv_mid_bwd(ds, cv, ln_g, ln_b):
    T, C = cv.shape
    tm = _tile(T, 512)

    def body(ds_ref, cv_ref, g_ref, b_ref, dcv_ref, dg_ref, db_ref, dc_ref):
        xhat, _ = _ln_stats(cv_ref[...])
        ln = xhat * g_ref[...] + b_ref[...]
        sg = _sigmoid(ln)
        dl = ds_ref[...].astype(F32) * (sg * (1.0 + ln * (1.0 - sg)))
        dcv, dg, db = _ln_bwd(dl, cv_ref[...], g_ref[...])
        first = pl.program_id(0) == 0
        _acc_rows(dg_ref, dg, first)
        _acc_rows(db_ref, db, first)
        _acc_rows(dc_ref, jnp.sum(dcv, axis=0, keepdims=True), first)
        dcv_ref[...] = dcv

    return _pc(body, "conv_mid_bwd", (T // tm,), [_rows(tm, C), _rows(tm, C), _const((1, C)), _const((1, C))],
               [_rows(tm, C), _const((1, C)), _const((1, C)), _const((1, C))],
               [_sds((T, C), F32)] + [_sds((1, C), F32)] * 3, sem=("arbitrary",))(ds, cv, ln_g, ln_b)


def dwconv_bwd(dcv, h, w_dw, taps):
    T, C = dcv.shape
    tq = _tile(T)
    nh = tq // HALO
    nblk = T // tq
    off = HALO - (taps - 1)

    def body(d_ref, dn_ref, a_ref, g_ref, ap_ref, gp_ref, w_ref, dh_ref, dw_ref, dbi_ref, su, sus, sd, sds, wb):
        i = pl.program_id(0)
        su[HALO:HALO + tq, :] = a_ref[...].astype(F32) * _sigmoid(g_ref[...].astype(F32))
        up = ap_ref[...].astype(F32) * _sigmoid(gp_ref[...].astype(F32))
        su[0:HALO, :] = jnp.where(i > 0, up, 0.0)
        sd[0:tq, :] = d_ref[...]
        sd[tq:tq + HALO, :] = jnp.where(i < nblk - 1, dn_ref[...], 0.0)
        _phases(su, sus)
        _phases(sd, sds)

        @pl.when(i == 0)
        def _():
            dw_ref[...] = jnp.zeros_like(dw_ref)

        for j in range(taps):
            dw_ref[j:j + 1, :] += jnp.sum(d_ref[...] * _tap(su, sus, off + j, tq), axis=0, keepdims=True)
        sa = jnp.zeros((1, C), F32)
        sb = jnp.zeros((1, C), F32)
        _spread(w_ref, wb, taps)
        for r in range(tq // CONV_ROWS):
            rows = slice(r * CONV_ROWS, (r + 1) * CONV_ROWS)
            dus = [wb[0] * _tap(sd, sds, taps - 1 + r * CONV_ROWS + 8 * k, 8) for k in range(CONV_ROWS // 8)]
            for j in range(1, taps):
                wj = wb[j]
                dus = [acc + wj * _tap(sd, sds, taps - 1 - j + r * CONV_ROWS + 8 * k, 8) for k, acc in enumerate(dus)]
            du = jnp.concatenate(dus, axis=0)
            a = a_ref[rows, :].astype(F32)
            sg = _sigmoid(g_ref[rows, :].astype(F32))
            da = du * sg
            dgt = du * a * sg * (1.0 - sg)
            dh_ref[rows, 0:C] = da.astype(BF16)
            dh_ref[rows, C:2 * C] = dgt.astype(BF16)
            sa = sa + jnp.sum(da, axis=0, keepdims=True)
            sb = sb + jnp.sum(dgt, axis=0, keepdims=True)
        first = i == 0
        _acc_rows(dbi_ref.at[:, 0:C], sa, first)
        _acc_rows(dbi_ref.at[:, C:2 * C], sb, first)

    prev = lambda col: pl.BlockSpec((HALO, C), lambda i: (jnp.maximum(i * nh - 1, 0), col))
    nxt = pl.BlockSpec((HALO, C), lambda i: (jnp.minimum((i + 1) * nh, T // HALO - 1), 0))
    cur = lambda col: pl.BlockSpec((tq, C), lambda i: (i, col))
    return _pc(body, "dwconv_bwd", (nblk,),
               [cur(0), nxt, cur(0), cur(1), prev(0), prev(1), _const((HALO, C))],
               [_rows(tq, 2 * C), _const((HALO, C)), _const((1, 2 * C))],
               [_sds((T, 2 * C), BF16), _sds((HALO, C), F32), _sds((1, 2 * C), F32)],
               scratch=[pltpu.VMEM((HALO + tq, C), F32), pltpu.VMEM((7, HALO + tq, C), F32),
                        pltpu.VMEM((HALO + tq, C), F32), pltpu.VMEM((7, HALO + tq, C), F32), pltpu.VMEM((taps, 8, C), F32)],
               sem=("arbitrary",))(dcv, dcv, h, h, h, h, w_dw)


def conv_in_bwd(dh, dpre_mix, w_in, alpha):
    T, D = dpre_mix.shape
    nw = w_in[0].shape[2]
    tm = _tile(T, 512)

    def body(dh_ref, dp_ref, w_ref, dx_ref):
        acc = alpha * dp_ref[...]
        for j in range(NS):
            acc = acc + lax.dot_general(dh_ref[:, j * nw:(j + 1) * nw], w_ref[j], NT, preferred_element_type=F32)
        dx_ref[...] = acc

    return _pc(body, "conv_in_bwd", (T // tm,), [_rows(tm, NS * nw), _rows(tm, D), _wspec(w_in)], _rows(tm, D),
               _sds((T, D), F32), sem=("parallel",))(dh, dpre_mix, w_in[0])


def wgrad(a, b, row_sharded, name, into):
    prev, out_shape, off = into
    T, Ka = a.shape
    Nb = b.shape[1]
    tt = min(1024, T)
    nt = T // tt
    ka, tn = min(Ka, 1024), min(Nb, 1024)
    if row_sharded:
        sr = Ka // NS
        spb = max(ka // sr, 1)
        rb = ka // spb
        assert out_shape[2] == Nb and off % rb == 0
        out_spec = pl.BlockSpec((spb, rb, tn), lambda i, j, t: (i, off // rb, j))
    else:
        sc = Nb // NS
        spb = max(tn // sc, 1)
        rb = ka
        assert out_shape[2] == sc and off % ka == 0
        out_spec = pl.BlockSpec((spb, ka, tn // spb), lambda i, j, t: (j, off // ka + i, 0))

    def body(a_ref, b_ref, *rest):
        o_ref, acc = rest[-2:]
        t = pl.program_id(2)
        av = a_ref[...]
        if av.dtype != BF16:
            av = av.astype(BF16)
        d = lax.dot_general(av, b_ref[...], TN, preferred_element_type=F32)

        @pl.when(t == 0)
        def _():
            acc[...] = d

        @pl.when(t > 0)
        def _():
            acc[...] += d

        @pl.when(t == nt - 1)
        def _():
            for s in range(spb):
                if row_sharded:
                    o_ref[s] = acc[s * rb:(s + 1) * rb, :].astype(BF16)
                else:
                    o_ref[s] = acc[:, s * (tn // spb):(s + 1) * (tn // spb)].astype(BF16)

    ins = [pl.BlockSpec((tt, ka), lambda i, j, t: (t, i)), pl.BlockSpec((tt, tn), lambda i, j, t: (t, j))]
    args = [a, b]
    kw = {}
    if prev is not None:
        ins.append(ANY)
        args.append(prev)
        kw["input_output_aliases"] = {2: 0}
    return _pc(body, name, (Ka // ka, Nb // tn, nt), ins, out_spec, _sds(out_shape, BF16),
               scratch=[pltpu.VMEM((ka, tn), F32)], sem=("parallel", "parallel", "arbitrary"), **kw)(*args)


def _adamw_math(w, g, m, v):
    c1 = 1.0 - ADAM_B1 ** ADAM_STEP
    c2 = 1.0 - ADAM_B2 ** ADAM_STEP
    mn = ADAM_B1 * m + (1.0 - ADAM_B1) * g
    vn = ADAM_B2 * v + (1.0 - ADAM_B2) * (g * g)
    return -ADAM_LR * ((mn / c1) / (jnp.sqrt(vn / c2) + ADAM_EPS) + ADAM_WD * w), mn, vn


def adamw_layer(w, m, v, layer, gbuf, off, prev, name):
    L, R, W = w.shape
    tr = 256
    assert R % tr == 0 and off % tr == 0

    def body(w_ref, g_ref, m_ref, v_ref, *rest):
        go_ref, d_ref, mo_ref, vo_ref = rest[-4:]
        g = g_ref[...]
        go_ref[...] = g
        d_ref[...], mo_ref[...], vo_ref[...] = _adamw_math(w_ref[...], g, m_ref[...], v_ref[...])

    lay = pl.BlockSpec((None, tr, W), lambda r: (layer, r, 0))
    ins = [lay, pl.BlockSpec((tr, W), lambda r: (off // tr + r, 0)), lay, lay]
    args = [w, gbuf, m, v]
    kw = {}
    if prev is not None:
        ins += [ANY] * 4
        args += list(prev)
        kw["input_output_aliases"] = {4 + k: k for k in range(4)}
    return _pc(body, name, (R // tr,), ins, [lay] * 4, [_sds((L, R, W), F32)] * 4, sem=("parallel",), **kw)(*args)


def adamw_many(ws, gs, ms, vs):
    n = len(ws)

    def body(*refs):
        for k in range(n):
            d, mn, vn = _adamw_math(refs[k][...], refs[n + k][...], refs[2 * n + k][...], refs[3 * n + k][...])
            refs[4 * n + k][...] = d
            refs[5 * n + k][...] = mn
            refs[6 * n + k][...] = vn

    outs = pl.pallas_call(body, name="adamw_small", out_shape=[_sds(a.shape, F32) for a in ws] * 3)(*ws, *gs, *ms, *vs)
    return outs[:n], outs[n:2 * n], outs[2 * n:]


def _rope_tables(T):
    pos = jnp.arange(T, dtype=F32)
    inv_freq = ROPE_THETA ** (-jnp.arange(0, ROPE, 2, dtype=F32) / ROPE)
    ang = pos[:, None] * inv_freq[None, :]
    cos, sin = jnp.cos(ang), jnp.sin(ang)
    pad = HEAD - ROPE
    c = jnp.concatenate([cos, cos, jnp.ones((T, pad), F32)], axis=1)
    s = jnp.concatenate([-sin, sin, jnp.zeros((T, pad), F32)], axis=1)
    return jnp.stack([jnp.tile(c, (1, 128 // HEAD)), jnp.tile(s, (1, 128 // HEAD))])


def _local_step(x, p, target, W, small, lay, hook=None):
    if hook is None:
        hook = lambda stage, after, G, sg=None: None
    T, D = x.shape
    depth = small["mix_ln_g"].shape[0]
    alpha = float((2 * depth) ** 0.25)
    taps = small["taps"]
    row = lambda a, i: a[i:i + 1]
    cs = _rope_tables(T)

    x0b = x.astype(BF16)
    h = conv_in_fwd(x0b, W["conv_w_in"], small["conv_b_in"])
    cv, s = dwconv_fwd(h, small["conv_w_dw"], small["conv_b_dw"], small["conv_ln_g"], small["conv_ln_b"], taps)
    pre_mix0, x1, x1b = mm_res_ln(s, W["conv_w_out"], x, row(small["mix_ln_g"], 0), row(small["mix_ln_b"], 0), alpha,
                                  small["conv_b_out"], "conv_out_fwd")
    hook("weights1", x1b, None)
    r0 = mlp_up_fwd(x1b, W["mlp_w_up0"], "mlp_up_fwd0")
    pre_mlp0, x2, x2b = mm_res_ln(r0, W["mlp_w_down0"], x1, row(small["mlp_ln_g"], 0), row(small["mlp_ln_b"], 0), alpha,
                                  None, "mlp_down_fwd0")
    x3, x3b, pp0, gl0 = ple_fwd(x2, x2b, p, 0, W["ple_w_proj0"], W["ple_w_gate0"], None, "ple_fwd0")

    hook("weights2", x3b, None)
    q, k, v = qkv_fwd(x3b, W["attn_w_q"], W["kv_w_k"], W["kv_w_v"], cs)
    o = attn_fwd(q, k, v, small["attn_sinks"])
    pre_mix1, x4, x4b = mm_res_ln(o, W["attn_w_o"], x3, row(small["mix_ln_g"], 1), row(small["mix_ln_b"], 1), alpha,
                                  None, "attn_out_fwd")
    r1 = mlp_up_fwd(x4b, W["mlp_w_up1"], "mlp_up_fwd1")
    pre_mlp1, x5, x5b = mm_res_ln(r1, W["mlp_w_down1"], x4, row(small["mlp_ln_g"], 1), row(small["mlp_ln_b"], 1), alpha,
                                  None, "mlp_down_fwd1")
    dx6, loss, pp1, gl1 = ple_fwd(x5, x5b, p, 1, W["ple_w_proj1"], W["ple_w_gate1"], target, "ple_fwd1")

    G, sg = {}, {}
    where = {n: (key, off) for key in lay for n, off, _ in lay[key]}
    rows_of = {key: sum(r for _, _, r in lay[key]) for key in lay}

    def wg(name, a, b, row_sharded):
        key, off = where[name]
        shape = (NS, rows_of[key], W[name][0].shape[2])
        G[key] = wgrad(a, b, row_sharded, "wg_" + name, (G.get(key), shape, off))

    dpp1, dgl1, dx5 = ple_bwd(dx6, pp1, gl1, W["ple_w_gate1"], "ple_bwd1")
    wg("ple_w_proj1", p[1], dpp1, False)
    wg("ple_w_gate1", x5b, dgl1, True)
    dpre_mlp1, dpre_mlp1b, dm1, g_mlp_g1, g_mlp_b1 = mlp_bwd1(dx5, pre_mlp1, row(small["mlp_ln_g"], 1), r1,
                                                              W["mlp_w_down1"], "mlp_bwd1_1")
    wg("mlp_w_down1", r1, dpre_mlp1b, True)
    wg("mlp_w_up1", x4b, dm1, False)
    dpre_mix1, dpre_mix1b, do, g_mix_g1, g_mix_b1, _ = mlp_bwd2(dpre_mlp1, dm1, W["mlp_w_up1"], alpha, pre_mix1,
                                                                row(small["mix_ln_g"], 1), W["attn_w_o"], "mlp_bwd2_1")
    wg("attn_w_o", o, dpre_mix1b, True)
    dq, dk, dv, dsinks = attn_bwd(q, k, v, do, small["attn_sinks"])
    dqb, dkb, dvb, dx3 = qkv_bwd(dq, dk, dv, dpre_mix1,
                                 W["attn_w_q"], W["kv_w_k"], W["kv_w_v"], cs, alpha)
    wg("attn_w_q", x3b, dqb, True)
    wg("kv_w_k", x3b, dkb, True)
    wg("kv_w_v", x3b, dvb, True)
    hook("grads3", None, G)

    dpp0, dgl0, dx2 = ple_bwd(dx3, pp0, gl0, W["ple_w_gate0"], "ple_bwd0")
    wg("ple_w_proj0", p[0], dpp0, False)
    wg("ple_w_gate0", x2b, dgl0, True)
    dpre_mlp0, dpre_mlp0b, dm0, g_mlp_g0, g_mlp_b0 = mlp_bwd1(dx2, pre_mlp0, row(small["mlp_ln_g"], 0), r0,
                                                              W["mlp_w_down0"], "mlp_bwd1_0")
    wg("mlp_w_down0", r0, dpre_mlp0b, True)
    wg("mlp_w_up0", x1b, dm0, False)
    hook("grads2", None, G)
    dpre_mix0, dpre_mix0b, dsw, g_mix_g0, g_mix_b0, g_b_out = mlp_bwd2(dpre_mlp0, dm0, W["mlp_w_up0"], alpha, pre_mix0,
                                                                      row(small["mix_ln_g"], 0), W["conv_w_out"],
                                                                      "mlp_bwd2_0")
    wg("conv_w_out", s, dpre_mix0b, True)
    hook("grads1", None, G)
    dcv, g_cln_g, g_cln_b, g_b_dw = conv_mid_bwd(dsw, cv, small["conv_ln_g"], small["conv_ln_b"])
    dh, g_w_dw, g_b_in = dwconv_bwd(dcv, h, small["conv_w_dw"], taps)
    wg("conv_w_in", x0b, dh, False)

    sg["conv_b_in"] = g_b_in
    sg["conv_w_dw"] = g_w_dw
    sg["conv_b_dw"], sg["conv_ln_g"], sg["conv_ln_b"], sg["conv_b_out"] = g_b_dw, g_cln_g, g_cln_b, g_b_out
    sg["mix_ln_g"] = [g_mix_g0, g_mix_g1]
    sg["mix_ln_b"] = [g_mix_b0, g_mix_b1]
    sg["mlp_ln_g"] = [g_mlp_g0, g_mlp_g1]
    sg["mlp_ln_b"] = [g_mlp_b0, g_mlp_b1]
    sg["attn_sinks"] = dsinks[:, 0][None, :]
    sg["loss"] = loss
    hook("grads0", None, G, sg)
    grad_x = conv_in_bwd(dh, dpre_mix0, W["conv_w_in"], alpha)
    return loss, grad_x, G, sg


BUFFERS = (("b0", ("conv_w_in",)), ("a0", ("conv_w_out",)),
           ("a1", ("mlp_w_up0", "mlp_w_down0", "ple_w_gate0")), ("c1", ("ple_w_proj0",)),
           ("a2", ("mlp_w_up1", "mlp_w_down1", "ple_w_gate1", "attn_w_q", "attn_w_o")),
           ("c2", ("kv_w_k", "kv_w_v", "ple_w_proj1")))
GROUPS = (("b0", "a0"), ("a1", "c1"), ("a2", "c2"))
REDUCED = (("b0",), ("a0",), ("a1", "c1"), ("a2", "c2"))
ROW_SHARDED = {"mlp_w_down0", "mlp_w_down1", "ple_w_gate0", "ple_w_gate1", "conv_w_out", "attn_w_q", "attn_w_o", "kv_w_k",
               "kv_w_v"}


def _split_layers(weights):
    out = {"conv_w_in": weights["conv_w_in"][0], "conv_w_out": weights["conv_w_out"][0],
           "attn_w_q": weights["attn_w_q"][0], "attn_w_o": weights["attn_w_o"][0],
           "kv_w_k": weights["kv_w_k"], "kv_w_v": weights["kv_w_v"]}
    for n in ("mlp_w_up", "mlp_w_down", "ple_w_proj", "ple_w_gate"):
        for i in range(weights[n].shape[0]):
            out[n + str(i)] = weights[n][i]
    return out


def _layout(shards):
    lay = {}
    for key, names in BUFFERS:
        off, rows = 0, []
        for n in names:
            rows.append((n, off, shards[n].shape[0]))
            off += shards[n].shape[0]
        lay[key] = rows
    return lay


def _place():
    return lax.axis_index("x"), lax.axis_index("y"), lax.axis_index("c")


def _flip(v, f):
    return (v + f) % 2 if f else v


CHIP_FLIPS = ((1, 0), (0, 1), (1, 1))


HBM = pl.BlockSpec(memory_space=pltpu.HBM)
SEM = pl.BlockSpec(memory_space=pltpu.SEMAPHORE)
EFFECT = pltpu.SideEffectType.DATAFLOW_SIDE_EFFECTING


def _half(ref, rows, c):
    return ref.at[pl.ds(pl.multiple_of(c * (rows // 2), 16), rows // 2), :]


def _gather_copies(refs, shapes, whole, send, recv):
    x, y, c = _place()
    me = 2 * x + y
    na = len(refs)
    cps = []
    for d, (fx, fy) in enumerate(CHIP_FLIPS):
        to = (_flip(x, fx), _flip(y, fy), c)
        for k in range(na):
            mine = refs[k].at[me] if k >= na - whole else _half(refs[k].at[me], shapes[k][1], c)
            cps.append(pltpu.make_async_remote_copy(mine, mine, send.at[d * na + k], recv.at[d * na + k], device_id=to,
                                                    device_id_type=MESH))
    return cps


def gather_start(bufs, whole, after, name):
    na = len(bufs)
    shapes = [b.shape for b in bufs]
    nsem = len(CHIP_FLIPS) * na

    def body(*refs):
        ins = refs[:na]
        send, recv = refs[-(na + 3)], refs[-(na + 2)]
        token = refs[-1]
        for cp in _gather_copies(ins, shapes, whole, send, recv):
            cp.start()
        token[...] = jnp.zeros_like(token)

    args = [pltpu.with_memory_space_constraint(b, pltpu.HBM) for b in bufs]
    ins = [HBM] * na
    if after is not None:
        args.append(after)
        ins.append(ANY)
    return pl.pallas_call(
        body, name=name, in_specs=ins,
        out_specs=[SEM, SEM] + [HBM] * na + [pl.BlockSpec(memory_space=pltpu.VMEM)],
        out_shape=[pltpu.SemaphoreType.DMA((nsem,)), pltpu.SemaphoreType.DMA((nsem,))]
        + [pltpu.HBM(b.shape, b.dtype) for b in bufs] + [_sds((8, 128), F32)],
        input_output_aliases={k: k + 2 for k in range(na)},
        compiler_params=pltpu.CompilerParams(has_side_effects=EFFECT))(*args)


def gather_wait(send, recv, bufs, whole, after, name):
    na = len(bufs)
    shapes = [b.shape for b in bufs]

    def body(*refs):
        ins = refs[:na]
        send_ref, recv_ref = refs[na], refs[na + 1]
        for cp in _gather_copies(ins, shapes, whole, send_ref, recv_ref):
            cp.wait_send()
            cp.wait_recv()

    return pl.pallas_call(
        body, name=name, in_specs=[HBM] * na + [SEM, SEM, ANY], out_specs=[HBM] * na,
        out_shape=[pltpu.HBM(b.shape, b.dtype) for b in bufs], input_output_aliases={k: k for k in range(na)},
        compiler_params=pltpu.CompilerParams(has_side_effects=EFFECT))(*bufs, send, recv, after)


def sibling_forward(bufs, name):
    nb = len(bufs)

    def body(*refs):
        outs = refs[nb:2 * nb]
        send, recv = refs[2 * nb:]
        x, y, c = _place()
        cps = []
        for d, (fx, fy) in enumerate(CHIP_FLIPS):
            frm = 2 * _flip(x, fx) + _flip(y, fy)
            for k in range(nb):
                theirs = _half(outs[k].at[frm], bufs[k].shape[1], c)
                cps.append(pltpu.make_async_remote_copy(theirs, theirs, send.at[d * nb + k], recv.at[d * nb + k],
                                                        device_id=(x, y, 1 - c), device_id_type=MESH))
        for cp in cps:
            cp.start()
        for cp in cps:
            cp.wait()

    nsem = len(CHIP_FLIPS) * nb
    return pl.pallas_call(
        body, name=name, in_specs=[ANY] * nb, out_specs=[ANY] * nb, out_shape=[_sds(b.shape, b.dtype) for b in bufs],
        input_output_aliases={k: k for k in range(nb)},
        scratch_shapes=[pltpu.SemaphoreType.DMA((nsem,)), pltpu.SemaphoreType.DMA((nsem,))])(*bufs)


def pack_rows(pieces, rows, width, name):
    def body(*refs):
        o_ref = refs[-1]
        o_ref[...] = jnp.zeros_like(o_ref)
        for ref, (a, off) in zip(refs[:-1], pieces):
            o_ref[off:off + a.shape[0], 0:a.shape[1]] = ref[...]

    return pl.pallas_call(body, name=name, out_shape=_sds((rows, width), F32))(*[a for a, _ in pieces])


PEER_FLIPS = tuple((fx, fy, fc) for fx in (0, 1) for fy in (0, 1) for fc in (0, 1) if fx or fy or fc)


def _reduce_copies(parts, zones, pack, send, recv):
    x, y, c = _place()
    nb = len(parts)
    na = nb + (1 if pack is not None else 0)
    cps = []
    for f, (fx, fy, fc) in enumerate(PEER_FLIPS):
        tx, ty, tc = _flip(x, fx), _flip(y, fy), _flip(c, fc)
        for k in range(nb):
            hrows = parts[k].shape[1] // 2
            piece = parts[k].at[2 * tx + ty, pl.ds(pl.multiple_of(tc * hrows, 16), hrows), :]
            cps.append(pltpu.make_async_remote_copy(piece, zones[k].at[f], send.at[f * na + k], recv.at[f * na + k],
                                                    device_id=(tx, ty, tc), device_id_type=MESH))
        if pack is not None:
            mine = pack.at[4 * x + 2 * y + c]
            cps.append(pltpu.make_async_remote_copy(mine, mine, send.at[f * na + nb], recv.at[f * na + nb],
                                                    device_id=(tx, ty, tc), device_id_type=MESH))
    return cps


def reduce_begin(parts, pack, name):
    nb = len(parts)
    zones = [lax.empty((len(PEER_FLIPS), g.shape[1] // 2, g.shape[2]), g.dtype) for g in parts]
    arrs = list(parts) + zones + ([pack] if pack is not None else [])
    na = len(arrs)
    nsem = len(PEER_FLIPS) * (nb + (1 if pack is not None else 0))

    def body(*refs):
        ins = refs[:na]
        send, recv = refs[na], refs[na + 1]
        for cp in _reduce_copies(ins[:nb], ins[nb:2 * nb], ins[2 * nb] if pack is not None else None, send, recv):
            cp.start()
        refs[-1][...] = jnp.zeros_like(refs[-1])

    return pl.pallas_call(
        body, name=name, in_specs=[HBM] * na,
        out_specs=[SEM, SEM] + [HBM] * na + [pl.BlockSpec(memory_space=pltpu.VMEM)],
        out_shape=[pltpu.SemaphoreType.DMA((nsem,)), pltpu.SemaphoreType.DMA((nsem,))]
        + [pltpu.HBM(a.shape, a.dtype) for a in arrs] + [_sds((8, 128), F32)],
        input_output_aliases={k: k + 2 for k in range(na)},
        compiler_params=pltpu.CompilerParams(has_side_effects=EFFECT))(
            *[pltpu.with_memory_space_constraint(a, pltpu.HBM) for a in arrs])


def reduce_end(send, recv, parts, zones, pack, after, name):
    nb = len(parts)
    arrs = list(parts) + list(zones) + ([pack] if pack is not None else [])
    na = len(arrs)

    def body(*refs):
        ins = refs[:na]
        for cp in _reduce_copies(ins[:nb], ins[nb:2 * nb], ins[2 * nb] if pack is not None else None, refs[na], refs[na + 1]):
            cp.wait_send()
            cp.wait_recv()

    return pl.pallas_call(
        body, name=name, in_specs=[HBM] * na + [SEM, SEM, ANY], out_specs=[HBM] * na,
        out_shape=[pltpu.HBM(a.shape, a.dtype) for a in arrs], input_output_aliases={k: k for k in range(na)},
        compiler_params=pltpu.CompilerParams(has_side_effects=EFFECT))(*arrs, send, recv, after)


def sibling_share(halves):
    nb = len(halves)

    def body(*refs):
        outs = refs[nb:2 * nb]
        send, recv = refs[2 * nb:]
        x, y, c = _place()
        cps = []
        for k in range(nb):
            hrows = halves[k].shape[0] // 2
            mine = outs[k].at[pl.ds(pl.multiple_of(c * hrows, 8), hrows), :]
            cps.append(pltpu.make_async_remote_copy(mine, mine, send.at[k], recv.at[k], device_id=(x, y, 1 - c),
                                                    device_id_type=MESH))
        for cp in cps:
            cp.start()
        for cp in cps:
            cp.wait()

    return pl.pallas_call(
        body, name="sibling_share", in_specs=[ANY] * nb, out_specs=[ANY] * nb,
        out_shape=[_sds(h.shape, h.dtype) for h in halves], input_output_aliases={k: k for k in range(nb)},
        scratch_shapes=[pltpu.SemaphoreType.DMA((nb,)), pltpu.SemaphoreType.DMA((nb,))])(*halves)


def _row_tile(rows):
    for cand in (512, 384, 256, 128, 64, 32, 16):
        if rows % cand == 0:
            return cand
    return rows


def piece_sum(g, z, idx, name):
    _, hrows, W = z.shape
    tr = _row_tile(hrows)
    nrb = hrows // tr

    def body(idx_ref, g_ref, z_ref, o_ref):
        acc = g_ref[...].astype(F32)
        for d in range(z.shape[0]):
            acc = acc + z_ref[d].astype(F32)
        o_ref[...] = acc

    gs = pltpu.PrefetchScalarGridSpec(
        num_scalar_prefetch=1, grid=(nrb,),
        in_specs=[pl.BlockSpec((None, tr, W), lambda i, sc: (sc[0], sc[1] * nrb + i, 0)),
                  pl.BlockSpec((z.shape[0], tr, W), lambda i, sc: (0, i, 0))],
        out_specs=pl.BlockSpec((tr, W), lambda i, sc: (sc[1] * nrb + i, 0)))
    return pl.pallas_call(body, name=name, grid_spec=gs, out_shape=_sds((2 * hrows, W), F32),
                          compiler_params=pltpu.CompilerParams(dimension_semantics=("parallel",),
                                                               vmem_limit_bytes=48 * 2 ** 20))(idx, g, z)


def small_sum(packs):
    n, R, W = packs.shape

    def body(p_ref, o_ref):
        acc = p_ref[0]
        for d in range(1, n):
            acc = acc + p_ref[d]
        o_ref[...] = acc

    return pl.pallas_call(body, name="small_sum", out_shape=_sds((R, W), F32))(packs)


WEIGHTS = ["conv_w_in", "conv_b_in", "conv_w_dw", "conv_b_dw", "conv_ln_g", "conv_ln_b", "conv_w_out", "conv_b_out", "kv_w_k",
           "kv_w_v", "attn_w_q", "attn_sinks", "attn_w_o", "mix_ln_g", "mix_ln_b", "mlp_w_up", "mlp_w_down", "mlp_ln_g",
           "mlp_ln_b", "ple_w_proj", "ple_w_gate"]
BIG = ["conv_w_in", "conv_w_out", "kv_w_k", "kv_w_v", "attn_w_q", "attn_w_o", "mlp_w_up", "mlp_w_down", "ple_w_proj",
       "ple_w_gate"]
SMALL = [n for n in WEIGHTS if n not in BIG]


def _step(x, p, target, w, m, v):
    D = x.shape[-1]
    ds = D // NS
    xq, yq, cq = _place()
    chip = 2 * xq + yq
    idx = jnp.stack([chip, cq]).astype(jnp.int32)

    shards = _split_layers(w)
    lay = _layout(shards)
    taps = w["conv_w_dw"].shape[1]
    small_loc = pack_rows([(w["conv_w_dw"][0], 0), (w["conv_b_dw"], HALO), (w["conv_ln_g"], HALO + 1), (w["conv_ln_b"], HALO + 2),
                           (w["conv_b_out"], HALO + 3), (w["conv_b_in"].reshape(2, ds), HALO + 4)], HALO + 8, ds, "pack_small")
    slot = lambda a: lax.dynamic_update_slice(lax.empty((NS,) + a.shape, a.dtype), a[None], (chip, 0, 0))
    started, token = [], None
    for gi, keys in enumerate(GROUPS):
        bufs = [slot(jnp.concatenate([shards[n].astype(BF16) for n, _, _ in lay[key]], axis=0)) for key in keys]
        if gi == 0:
            bufs.append(slot(small_loc))
        send, recv, *thru, token = gather_start(bufs, 1 if gi == 0 else 0, token, "gather_start%d" % gi)
        started.append((send, recv, thru))
    W = {}

    def arrive(gi, after):
        send, recv, thru = started[gi]
        whole = 1 if gi == 0 else 0
        got = gather_wait(send, recv, thru, whole, after, "gather_wait%d" % gi)
        nk = len(GROUPS[gi])
        for key, buf in zip(GROUPS[gi], sibling_forward(got[:nk], "sibling_forward%d" % gi)):
            for n, off, rows in lay[key]:
                W[n] = (buf, off, rows)
        return got[nk:]

    gs, = arrive(0, token)
    across = lambda rows: gs[:, rows, :].transpose(1, 0, 2).reshape(rows.stop - rows.start, D)
    small = {"taps": taps, "conv_w_dw": across(slice(0, HALO)), "conv_b_dw": across(slice(HALO, HALO + 1)),
             "conv_ln_g": across(slice(HALO + 1, HALO + 2)), "conv_ln_b": across(slice(HALO + 2, HALO + 3)),
             "conv_b_out": across(slice(HALO + 3, HALO + 4)), "conv_b_in": gs[:, HALO + 4:HALO + 6, :].reshape(1, 2 * D),
             "attn_sinks": w["attn_sinks"], "mix_ln_g": w["mix_ln_g"], "mix_ln_b": w["mix_ln_b"],
             "mlp_ln_g": w["mlp_ln_g"], "mlp_ln_b": w["mlp_ln_b"]}

    reducing = {}

    def reduce_start(gi, G, pack):
        nk = len(REDUCED[gi])
        send, recv, *thru, token = reduce_begin([G[key] for key in REDUCED[gi]], pack, "reduce_begin%d" % gi)
        reducing[gi] = (send, recv, thru[:nk], thru[nk:2 * nk], thru[2 * nk] if pack is not None else None)
        _FOLLOW.append(token)

    def small_pack(sg):
        pieces = [(sg["conv_b_in"].reshape(2, D), 0), (sg["conv_w_dw"], 2)]
        r0 = 2 + HALO
        for i, n in enumerate(("conv_b_dw", "conv_ln_g", "conv_ln_b", "conv_b_out")):
            pieces.append((sg[n], r0 + i))
        r0 += 4
        for i, n in enumerate(("mix_ln_g", "mix_ln_b", "mlp_ln_g", "mlp_ln_b")):
            pieces += [(sg[n][0], r0 + 2 * i), (sg[n][1], r0 + 2 * i + 1)]
        pieces += [(sg["attn_sinks"], r0 + 8), (sg["loss"][0:1], r0 + 9)]
        mine = pack_rows(pieces, r0 + 10, D, "pack_small_grads")
        return lax.dynamic_update_slice(lax.empty((8,) + mine.shape, F32), mine[None], (4 * xq + 2 * yq + cq, 0, 0))

    def hook(stage, after, G, sg=None):
        if stage == "weights1":
            arrive(1, after)
        elif stage == "weights2":
            arrive(2, after)
        elif stage == "grads0":
            reduce_start(0, G, small_pack(sg))
        elif stage.startswith("grads"):
            reduce_start(int(stage[5:]), G, None)

    loss, grad_x, G, sg = _local_step(x[0], p[:, 0], target[0], W, small, lay, hook)
    _FOLLOW.clear()
    nsink = w["attn_sinks"].shape[1]

    halves = {}
    for gi in reversed(range(len(REDUCED))):
        send, recv, parts, zones, pack = reducing[gi]
        done = reduce_end(send, recv, parts, zones, pack, grad_x, "reduce_end%d" % gi)
        nk = len(REDUCED[gi])
        for key, g_, z_ in zip(REDUCED[gi], done[:nk], done[nk:2 * nk]):
            halves[key] = piece_sum(g_, z_, idx, "piece_sum_" + key)
        if pack is not None:
            tot = small_sum(done[2 * nk])
    order = [key for key, _ in BUFFERS]
    full = sibling_share([halves[key] for key in order])

    grads, delta, new_m, new_v = {}, {}, {}, {}
    found = {n: (buf, off) for key, buf in zip(order, full) for n, off, _ in lay[key]}
    for n in BIG:
        three = lambda a: a.reshape((-1,) + a.shape[-2:])
        w3, m3, v3 = three(w[n]), three(m[n]), three(v[n])
        outs = None
        for i in range(w3.shape[0]):
            buf, off = found[n + str(i)] if n + str(i) in found else found[n]
            outs = adamw_layer(w3, m3, v3, i, buf, off, outs, "adamw_%s%d" % (n, i))
        grads[n], delta[n], new_m[n], new_v[n] = [a.reshape(w[n].shape) for a in outs]
    cols = lambda rows: lax.dynamic_slice(rows, (0, chip * ds), (rows.shape[0], ds))
    grads["conv_b_in"] = lax.dynamic_slice(tot[0:2].reshape(1, 2 * D), (0, chip * 2 * ds), (1, 2 * ds))
    grads["conv_w_dw"] = cols(tot[2:2 + taps])[None]
    r0 = 2 + HALO
    for i, n in enumerate(("conv_b_dw", "conv_ln_g", "conv_ln_b", "conv_b_out")):
        grads[n] = cols(tot[r0 + i:r0 + i + 1])
    r0 += 4
    for i, n in enumerate(("mix_ln_g", "mix_ln_b", "mlp_ln_g", "mlp_ln_b")):
        grads[n] = tot[r0 + 2 * i:r0 + 2 * i + 2]
    grads["attn_sinks"] = tot[r0 + 8:r0 + 9, 0:nsink]

    ds_, ms_, vs_ = adamw_many([w[n] for n in SMALL], [grads[n] for n in SMALL], [m[n] for n in SMALL], [v[n] for n in SMALL])
    for n, d_, m_, v_ in zip(SMALL, ds_, ms_, vs_):
        delta[n], new_m[n], new_v[n] = d_, m_, v_

    total = tot[r0 + 9, 0]
    return (total, grad_x[None], *[grads[n] for n in WEIGHTS], *[delta[n] for n in WEIGHTS], *[new_m[n] for n in WEIGHTS],
            *[new_v[n] for n in WEIGHTS])


def kernel(x, p, conv_w_in, conv_b_in, conv_w_dw, conv_b_dw, conv_ln_g, conv_ln_b, conv_w_out, conv_b_out, kv_w_k, kv_w_v, attn_w_q, attn_sinks, attn_w_o, mix_ln_g, mix_ln_b, mlp_w_up, mlp_w_down, mlp_ln_g, mlp_ln_b, ple_w_proj, ple_w_gate, loss_target, m_conv_w_in, m_conv_b_in, m_conv_w_dw, m_conv_b_dw, m_conv_ln_g, m_conv_ln_b, m_conv_w_out, m_conv_b_out, m_kv_w_k, m_kv_w_v, m_attn_w_q, m_attn_sinks, m_attn_w_o, m_mix_ln_g, m_mix_ln_b, m_mlp_w_up, m_mlp_w_down, m_mlp_ln_g, m_mlp_ln_b, m_ple_w_proj, m_ple_w_gate, v_conv_w_in, v_conv_b_in, v_conv_w_dw, v_conv_b_dw, v_conv_ln_g, v_conv_ln_b, v_conv_w_out, v_conv_b_out, v_kv_w_k, v_kv_w_v, v_attn_w_q, v_attn_sinks, v_attn_w_o, v_mix_ln_g, v_mix_ln_b, v_mlp_w_up, v_mlp_w_down, v_mlp_ln_g, v_mlp_ln_b, v_ple_w_proj, v_ple_w_gate):
    w = dict(zip(WEIGHTS, (conv_w_in, conv_b_in, conv_w_dw, conv_b_dw, conv_ln_g, conv_ln_b, conv_w_out, conv_b_out, kv_w_k,
                           kv_w_v, attn_w_q, attn_sinks, attn_w_o, mix_ln_g, mix_ln_b, mlp_w_up, mlp_w_down, mlp_ln_g, mlp_ln_b,
                           ple_w_proj, ple_w_gate)))
    m = dict(zip(WEIGHTS, (m_conv_w_in, m_conv_b_in, m_conv_w_dw, m_conv_b_dw, m_conv_ln_g, m_conv_ln_b, m_conv_w_out,
                           m_conv_b_out, m_kv_w_k, m_kv_w_v, m_attn_w_q, m_attn_sinks, m_attn_w_o, m_mix_ln_g, m_mix_ln_b,
                           m_mlp_w_up, m_mlp_w_down, m_mlp_ln_g, m_mlp_ln_b, m_ple_w_proj, m_ple_w_gate)))
    v = dict(zip(WEIGHTS, (v_conv_w_in, v_conv_b_in, v_conv_w_dw, v_conv_b_dw, v_conv_ln_g, v_conv_ln_b, v_conv_w_out,
                           v_conv_b_out, v_kv_w_k, v_kv_w_v, v_attn_w_q, v_attn_sinks, v_attn_w_o, v_mix_ln_g, v_mix_ln_b,
                           v_mlp_w_up, v_mlp_w_down, v_mlp_ln_g, v_mlp_ln_b, v_ple_w_proj, v_ple_w_gate)))
    return _step(x, p, loss_target, w, m, v)
```

```python
import functools

import jax
import jax.numpy as jnp
from jax import lax
from jax.experimental import pallas as pl
from jax.experimental.pallas import tpu as pltpu

F32 = jnp.float32
BF16 = jnp.bfloat16
NS = 4
HEAD = 64
BLK = 128
ROPE = 16
ROPE_THETA = 500000.0
LN_EPS = 1e-5
NEG = -1e30
HALO = 32
ADAM_LR, ADAM_B1, ADAM_B2, ADAM_EPS, ADAM_WD, ADAM_STEP = 0.001, 0.9, 0.999, 1e-08, 0.01, 10
MESH = pl.DeviceIdType.MESH
ANY = pl.BlockSpec(memory_space=pl.ANY)
NT = (((1,), (1,)), ((), ()))
TN = (((0,), (0,)), ((), ()))


_FOLLOW = []


def _pc(body, name, grid, in_specs, out_specs, out_shape, scratch=(), sem=None, vmem=56, **kw):
    call = lambda fn, ins: pl.pallas_call(
        fn, name=name, grid=grid, in_specs=ins, out_specs=out_specs, out_shape=out_shape,
        scratch_shapes=list(scratch),
        compiler_params=pltpu.CompilerParams(dimension_semantics=sem, vmem_limit_bytes=vmem * 2 ** 20), **kw)
    if not _FOLLOW:
        return call(body, in_specs)
    extra = list(_FOLLOW)
    _FOLLOW.clear()
    n_in = len(in_specs)

    def ordered(*refs):
        return body(*refs[:n_in], *refs[n_in + len(extra):])

    run = call(ordered, list(in_specs) + [ANY] * len(extra))
    return lambda *args: run(*args, *extra)


def _rows(tm, n):
    return pl.BlockSpec((tm, n), lambda i: (i, 0))


def _const(shape):
    return pl.BlockSpec(shape, lambda *_: (0,) * len(shape))


def _wspec(w):
    buf, off, rows = w
    assert off % rows == 0
    return pl.BlockSpec((NS, rows, buf.shape[2]), lambda *_: (0, off // rows, 0))


def _rows_joined(w_ref):
    n, r, c = w_ref.shape
    return w_ref[...].reshape(n * r, c)


def _sds(shape, dtype):
    return jax.ShapeDtypeStruct(shape, dtype)


def _tile(t, rows=256):
    return min(rows, t)


def _sigmoid(x):
    return 1.0 / (1.0 + jnp.exp(-x))


def _ln_stats(w):
    mu = jnp.mean(w, axis=-1, keepdims=True)
    xc = w - mu
    var = jnp.mean(xc * xc, axis=-1, keepdims=True)
    rstd = lax.rsqrt(var + LN_EPS)
    return xc * rstd, rstd


def _ln_bwd(dy, w, g):
    xhat, rstd = _ln_stats(w)
    dxhat = dy * g
    m1 = jnp.mean(dxhat, axis=-1, keepdims=True)
    m2 = jnp.mean(dxhat * xhat, axis=-1, keepdims=True)
    dw = rstd * (dxhat - m1 - xhat * m2)
    return dw, jnp.sum(dy * xhat, axis=0, keepdims=True), jnp.sum(dy, axis=0, keepdims=True)


def _acc_rows(ref, val, first):
    @pl.when(first)
    def _():
        ref[...] = val

    @pl.when(jnp.logical_not(first))
    def _():
        ref[...] += val


def conv_in_fwd(xb, w_in, b_in):
    T, D = xb.shape
    nw = w_in[0].shape[2]
    tm = _tile(T, 512)

    def body(x_ref, w_ref, b_ref, h_ref):
        x = x_ref[...]
        for j in range(NS):
            sl = slice(j * nw, (j + 1) * nw)
            h_ref[:, sl] = (jnp.dot(x, w_ref[j], preferred_element_type=F32) + b_ref[:, sl]).astype(BF16)

    return _pc(body, "conv_in_fwd", (T // tm,), [_rows(tm, D), _wspec(w_in), _const((1, NS * nw))],
               _rows(tm, NS * nw), _sds((T, NS * nw), BF16), sem=("parallel",))(xb, w_in[0], b_in)


CONV_ROWS = 16


def _phases(scr, sh):
    n = scr.shape[0] - 8
    for b in range(1, 8):
        sh[b - 1, 0:n, :] = scr[b:b + n, :]


def _spread(w_ref, wb, taps):
    for j in range(taps):
        wb[j] = jnp.broadcast_to(w_ref[j:j + 1, :], wb.shape[1:])


def _tap(scr, sh, o, n):
    b = o % 8
    return scr[o:o + n, :] if b == 0 else sh[b - 1, o - b:o - b + n, :]


def dwconv_fwd(h, w_dw, b_dw, ln_g, ln_b, taps):
    T = h.shape[0]
    C = h.shape[1] // 2
    tq = _tile(T)
    nh = tq // HALO
    off = HALO - (taps - 1)

    def body(a_ref, g_ref, ap_ref, gp_ref, w_ref, bdw_ref, lg_ref, lb_ref, cv_ref, s_ref, scr, sh, wb):
        i = pl.program_id(0)
        scr[HALO:HALO + tq, :] = a_ref[...].astype(F32) * _sigmoid(g_ref[...].astype(F32))
        up = ap_ref[...].astype(F32) * _sigmoid(gp_ref[...].astype(F32))
        scr[0:HALO, :] = jnp.where(i > 0, up, 0.0)
        _phases(scr, sh)
        _spread(w_ref, wb, taps)
        bias = jnp.broadcast_to(bdw_ref[...], (8, C))
        for r in range(tq // CONV_ROWS):
            accs = [bias] * (CONV_ROWS // 8)
            for j in range(taps):
                wj = wb[j]
                accs = [acc + wj * _tap(scr, sh, off + j + r * CONV_ROWS + 8 * k, 8) for k, acc in enumerate(accs)]
            for k, acc in enumerate(accs):
                cv_ref[r * CONV_ROWS + 8 * k:r * CONV_ROWS + 8 * k + 8, :] = acc
        xhat, _ = _ln_stats(cv_ref[...])
        ln = xhat * lg_ref[...] + lb_ref[...]
        s_ref[...] = (ln * _sigmoid(ln)).astype(BF16)

    prev = lambda col: pl.BlockSpec((HALO, C), lambda i: (jnp.maximum(i * nh - 1, 0), col))
    cur = lambda col: pl.BlockSpec((tq, C), lambda i: (i, col))
    return _pc(body, "dwconv_fwd", (T // tq,),
               [cur(0), cur(1), prev(0), prev(1), _const((HALO, C)), _const((1, C)), _const((1, C)), _const((1, C))],
               [_rows(tq, C), _rows(tq, C)], [_sds((T, C), F32), _sds((T, C), BF16)],
               scratch=[pltpu.VMEM((HALO + tq, C), F32), pltpu.VMEM((7, HALO + tq, C), F32), pltpu.VMEM((taps, 8, C), F32)],
               sem=("parallel",))(h, h, h, h, w_dw, b_dw, ln_g, ln_b)


def mm_res_ln(a, w, res, g, b, alpha, bias, name):
    T, K = a.shape
    ks = K // NS
    D = res.shape[1]
    tm = _tile(T, 512 if K <= D else 256)

    def body(*refs):
        a_ref, w_ref, res_ref, g_ref, b_ref = refs[:5]
        n = 5
        if bias is not None:
            bias_ref = refs[5]
            n = 6
        pre_ref, xo_ref, xb_ref = refs[n:n + 3]
        acc = jnp.dot(a_ref[...], _rows_joined(w_ref), preferred_element_type=F32)
        if bias is not None:
            acc = acc + bias_ref[...]
        pre = alpha * res_ref[...] + acc
        xhat, _ = _ln_stats(pre)
        xo = xhat * g_ref[...] + b_ref[...]
        pre_ref[...] = pre
        xo_ref[...] = xo
        xb_ref[...] = xo.astype(BF16)

    ins = [_rows(tm, K), _wspec(w), _rows(tm, D), _const((1, D)), _const((1, D))]
    args = [a, w[0], res, g, b]
    if bias is not None:
        ins.append(_const((1, D)))
        args.append(bias)
    return _pc(body, name, (T // tm,), ins, [_rows(tm, D)] * 3, [_sds((T, D), F32), _sds((T, D), F32), _sds((T, D), BF16)],
               sem=("parallel",))(*args)


def mlp_up_fwd(xb, w_up, name):
    T, D = xb.shape
    fs = w_up[0].shape[2]
    tm = _tile(T)

    def body(x_ref, w_ref, r_ref):
        x = x_ref[...]
        for j in range(NS):
            m = jnp.maximum(jnp.dot(x, w_ref[j], preferred_element_type=F32), 0.0)
            r_ref[:, j * fs:(j + 1) * fs] = (m * m).astype(BF16)

    return _pc(body, name, (T // tm,), [_rows(tm, D), _wspec(w_up)], _rows(tm, NS * fs), _sds((T, NS * fs), BF16),
               sem=("parallel",))(xb, w_up[0])


def ple_fwd(x, xb, p, layer, w_proj, w_gate, target, name):
    T, D = x.shape
    P = p.shape[2]
    ds = D // NS
    tm = _tile(T, 512)
    last = target is not None

    def body(*refs):
        x_ref, xb_ref, p_ref, wp_ref, wg_ref = refs[:5]
        n = 5
        if last:
            t_ref = refs[5]
            n = 6
        o_ref, o2_ref, pp_ref, gl_ref = refs[n:n + 4]
        gl = jnp.dot(xb_ref[...], _rows_joined(wg_ref), preferred_element_type=F32)
        gl_ref[...] = gl.astype(BF16)
        sg = _sigmoid(gl)
        pb = p_ref[...].astype(BF16)
        sq = jnp.zeros((1, 1), F32)
        for j in range(NS):
            sl = slice(j * ds, (j + 1) * ds)
            pp = jnp.dot(pb, wp_ref[j], preferred_element_type=F32)
            pp_ref[:, sl] = pp.astype(BF16)
            out = x_ref[:, sl] + pp * sg[:, sl]
            if last:
                err = out - t_ref[:, sl]
                o_ref[:, sl] = err * (1.0 / D)
                e2 = jnp.sum(err * err, axis=0, keepdims=True)
                sq = sq + jnp.sum(e2, axis=1, keepdims=True)
            else:
                o_ref[:, sl] = out
                o2_ref[:, sl] = out.astype(BF16)
        if last:
            _acc_rows(o2_ref, jnp.broadcast_to(sq * (0.5 / D), (8, 128)), pl.program_id(0) == 0)

    ins = [_rows(tm, D), _rows(tm, D), pl.BlockSpec((None, tm, P), lambda i: (layer, i, 0)), _wspec(w_proj), _wspec(w_gate)]
    args = [x, xb, p, w_proj[0], w_gate[0]]
    if last:
        ins.append(_rows(tm, D))
        args.append(target)
        outs = [_rows(tm, D), _const((8, 128)), _rows(tm, D), _rows(tm, D)]
        shapes = [_sds((T, D), F32), _sds((8, 128), F32), _sds((T, D), BF16), _sds((T, D), BF16)]
    else:
        outs = [_rows(tm, D)] * 4
        shapes = [_sds((T, D), F32), _sds((T, D), BF16), _sds((T, D), BF16), _sds((T, D), BF16)]
    return _pc(body, name, (T // tm,), ins, outs, shapes, sem=("arbitrary",) if last else ("parallel",))(*args)


def _rope(x, cs_ref, sign):
    c = cs_ref[0]
    s = cs_ref[1] * sign
    lane = lax.broadcasted_iota(jnp.int32, c.shape, 1)
    first = (lane % HEAD) < (ROPE // 2)
    outs = []
    for gq in range(x.shape[1] // 128):
        xg = x[:, gq * 128:(gq + 1) * 128]
        sw = jnp.where(first, pltpu.roll(xg, 128 - ROPE // 2, 1), pltpu.roll(xg, ROPE // 2, 1))
        outs.append(xg * c + sw * s)
    return outs


def qkv_fwd(xb, w_q, w_k, w_v, cs):
    T, D = xb.shape
    ds = D // NS
    HD, KVD = w_q[0].shape[2], w_k[0].shape[2]
    tm = _tile(T, 512)
    scale = 1.0 / (HEAD ** 0.5)

    def body(x_ref, wq_ref, wk_ref, wv_ref, cs_ref, q_ref, k_ref, v_ref):
        def proj(w_ref):
            return jnp.dot(x_ref[...], _rows_joined(w_ref), preferred_element_type=F32)

        for gq, val in enumerate(_rope(proj(wq_ref), cs_ref, 1.0)):
            q_ref[:, gq * 128:(gq + 1) * 128] = (val * scale).astype(BF16)
        for gq, val in enumerate(_rope(proj(wk_ref), cs_ref, 1.0)):
            k_ref[:, gq * 128:(gq + 1) * 128] = val.astype(BF16)
        v_ref[...] = proj(wv_ref).astype(BF16)

    cs_spec = pl.BlockSpec((2, tm, 128), lambda i: (0, i, 0))
    return _pc(body, "qkv_fwd", (T // tm,), [_rows(tm, D), _wspec(w_q), _wspec(w_k), _wspec(w_v), cs_spec],
               [_rows(tm, HD), _rows(tm, KVD), _rows(tm, KVD)],
               [_sds((T, HD), BF16), _sds((T, KVD), BF16), _sds((T, KVD), BF16)], sem=("parallel",))(
                   xb, w_q[0], w_k[0], w_v[0], cs)


def _band_mask(n):
    row = lax.broadcasted_iota(jnp.int32, (BLK, 2 * BLK), 0)
    col = lax.broadcasted_iota(jnp.int32, (BLK, 2 * BLK), 1)
    return (col > row) & (col <= row + BLK) & ((col >= BLK) | (n > 0))


def _head(h):
    return slice(h * HEAD, (h + 1) * HEAD)


def _softmax_sink(s, sink):
    m = jnp.maximum(jnp.max(s, axis=-1, keepdims=True), sink)
    e = jnp.exp(s - m)
    es = jnp.exp(sink - m)
    den = jnp.sum(e, axis=-1, keepdims=True) + es
    return e / den, es / den


def attn_fwd(q, k, v, sinks):
    T, HD = q.shape
    KVD = k.shape[1]
    NKV = KVD // HEAD
    G = HD // KVD

    def body(s_ref, q_ref, kc_ref, kp_ref, vc_ref, vp_ref, o_ref):
        valid = _band_mask(pl.program_id(0))
        for kh in range(NKV):
            k2 = jnp.concatenate([kp_ref[:, _head(kh)], kc_ref[:, _head(kh)]], axis=0)
            v2 = jnp.concatenate([vp_ref[:, _head(kh)], vc_ref[:, _head(kh)]], axis=0)
            hs = [kh * G + gq for gq in range(G)]
            sc = [lax.dot_general(q_ref[:, _head(hh)], k2, NT, preferred_element_type=F32) for hh in hs]
            pb = [_softmax_sink(jnp.where(valid, s, NEG), s_ref[0, hh])[0].astype(BF16) for s, hh in zip(sc, hs)]
            for p, hh in zip(pb, hs):
                o_ref[:, _head(hh)] = jnp.dot(p, v2, preferred_element_type=F32).astype(BF16)

    cur = lambda n_: pl.BlockSpec((BLK, n_), lambda n: (n, 0))
    prev = lambda n_: pl.BlockSpec((BLK, n_), lambda n: (jnp.maximum(n - 1, 0), 0))
    return _pc(body, "attn_fwd", (T // BLK,),
               [pl.BlockSpec(memory_space=pltpu.SMEM), cur(HD), cur(KVD), prev(KVD), cur(KVD), prev(KVD)],
               cur(HD), _sds((T, HD), BF16), sem=("parallel",))(sinks, q, k, k, v, v)


def ple_bwd(dxo, pp, gl, w_gate, name):
    T, D = dxo.shape
    ds = D // NS
    tm = _tile(T, 512)

    def body(d_ref, pp_ref, gl_ref, wg_ref, dpp_ref, dgl_ref, dx_ref):
        d = d_ref[...]
        sg = _sigmoid(gl_ref[...].astype(F32))
        dpp_ref[...] = (d * sg).astype(BF16)
        dgl = (d * pp_ref[...].astype(F32) * sg * (1.0 - sg)).astype(BF16)
        dgl_ref[...] = dgl
        dx_ref[...] = d + lax.dot_general(dgl, _rows_joined(wg_ref), NT, preferred_element_type=F32)

    return _pc(body, name, (T // tm,), [_rows(tm, D)] * 3 + [_wspec(w_gate)], [_rows(tm, D)] * 3,
               [_sds((T, D), BF16), _sds((T, D), BF16), _sds((T, D), F32)], sem=("parallel",))(dxo, pp, gl, w_gate[0])


def mlp_bwd1(dy, pre, g, r, w_down, name):
    T, D = dy.shape
    fs = w_down[2]
    tm = _tile(T)

    def body(dy_ref, pre_ref, g_ref, r_ref, w_ref, dw_ref, dwb_ref, dm_ref, dg_ref, db_ref):
        dw, dg, db = _ln_bwd(dy_ref[...], pre_ref[...], g_ref[...])
        first = pl.program_id(0) == 0
        _acc_rows(dg_ref, dg, first)
        _acc_rows(db_ref, db, first)
        dwb = dw.astype(BF16)
        dw_ref[...] = dw
        dwb_ref[...] = dwb
        for j in range(NS):
            sl = slice(j * fs, (j + 1) * fs)
            dr = lax.dot_general(dwb, w_ref[j], NT, preferred_element_type=F32)
            dm_ref[:, sl] = (dr * (2.0 * jnp.sqrt(r_ref[:, sl].astype(F32)))).astype(BF16)

    return _pc(body, name, (T // tm,), [_rows(tm, D), _rows(tm, D), _const((1, D)), _rows(tm, NS * fs), _wspec(w_down)],
               [_rows(tm, D), _rows(tm, D), _rows(tm, NS * fs), _const((1, D)), _const((1, D))],
               [_sds((T, D), F32), _sds((T, D), BF16), _sds((T, NS * fs), BF16), _sds((1, D), F32), _sds((1, D), F32)],
               sem=("arbitrary",))(dy, pre, g, r, w_down[0])


def mlp_bwd2(dpre, dm, w_up, alpha, pre_mix, g_mix, w_mix, name):
    T, D = dpre.shape
    fs = w_up[0].shape[2]
    ms = w_mix[2]
    tm = _tile(T)

    def body(dp_ref, dm_ref, wu_ref, pre_ref, g_ref, wm_ref, dw_ref, dwb_ref, do_ref, dg_ref, db_ref, dc_ref):
        dy = alpha * dp_ref[...]
        for j in range(NS):
            dy = dy + lax.dot_general(dm_ref[:, j * fs:(j + 1) * fs], wu_ref[j], NT, preferred_element_type=F32)
        dw, dg, db = _ln_bwd(dy, pre_ref[...], g_ref[...])
        first = pl.program_id(0) == 0
        _acc_rows(dg_ref, dg, first)
        _acc_rows(db_ref, db, first)
        _acc_rows(dc_ref, jnp.sum(dw, axis=0, keepdims=True), first)
        dwb = dw.astype(BF16)
        dw_ref[...] = dw
        dwb_ref[...] = dwb
        do_ref[...] = lax.dot_general(dwb, _rows_joined(wm_ref), NT, preferred_element_type=F32).astype(BF16)

    return _pc(body, name, (T // tm,),
               [_rows(tm, D), _rows(tm, NS * fs), _wspec(w_up), _rows(tm, D), _const((1, D)), _wspec(w_mix)],
               [_rows(tm, D), _rows(tm, D), _rows(tm, NS * ms), _const((1, D)), _const((1, D)), _const((1, D))],
               [_sds((T, D), F32), _sds((T, D), BF16), _sds((T, NS * ms), BF16)] + [_sds((1, D), F32)] * 3,
               sem=("arbitrary",))(dpre, dm, w_up[0], pre_mix, g_mix, w_mix[0])


def attn_bwd(q, k, v, do, sinks):
    T, HD = q.shape
    KVD = k.shape[1]
    NH, NKV = HD // HEAD, KVD // HEAD
    G = NH // NKV
    nb = T // BLK

    def body(s_ref, q_ref, do_ref, kc_ref, kp_ref, vc_ref, vp_ref, dq_ref, dk_ref, dv_ref, ds_ref, ck, cv):
        n = pl.program_id(0)

        @pl.when(n == 0)
        def _():
            ck[...] = jnp.zeros_like(ck)
            cv[...] = jnp.zeros_like(cv)
            ds_ref[...] = jnp.zeros_like(ds_ref)

        @pl.when(n < nb)
        def _():
            valid = _band_mask(n)
            for kh in range(NKV):
                kv = _head(kh)
                k2 = jnp.concatenate([kp_ref[:, kv], kc_ref[:, kv]], axis=0)
                v2 = jnp.concatenate([vp_ref[:, kv], vc_ref[:, kv]], axis=0)
                hs = [kh * G + gq for gq in range(G)]
                qs = [q_ref[:, _head(hh)] for hh in hs]
                dos = [do_ref[:, _head(hh)] for hh in hs]
                sc = [lax.dot_general(qh, k2, NT, preferred_element_type=F32) for qh in qs]
                dp = [lax.dot_general(doh, v2, NT, preferred_element_type=F32) for doh in dos]
                pr = [_softmax_sink(jnp.where(valid, s, NEG), s_ref[0, hh]) for s, hh in zip(sc, hs)]
                delta = [jnp.sum(p * d, axis=-1, keepdims=True) for (p, _), d in zip(pr, dp)]
                dsb = [(p * (d - dl)).astype(BF16) for (p, _), d, dl in zip(pr, dp, delta)]
                pb = [p.astype(BF16) for p, _ in pr]
                for (_, ps), dl, hh in zip(pr, delta, hs):
                    ds_ref[hh:hh + 1, :] += jnp.broadcast_to(-jnp.sum(ps * dl, axis=0, keepdims=True), (1, 128))
                for d, hh in zip(dsb, hs):
                    dq_ref[:, _head(hh)] = jnp.dot(d, k2, preferred_element_type=F32)
                dk2 = lax.dot_general(jnp.concatenate(dsb, axis=0), jnp.concatenate(qs, axis=0), TN,
                                      preferred_element_type=F32)
                dv2 = lax.dot_general(jnp.concatenate(pb, axis=0), jnp.concatenate(dos, axis=0), TN,
                                      preferred_element_type=F32)
                dk_ref[:, kv] = ck[:, kv] + dk2[0:BLK]
                dv_ref[:, kv] = cv[:, kv] + dv2[0:BLK]
                ck[:, kv] = dk2[BLK:2 * BLK]
                cv[:, kv] = dv2[BLK:2 * BLK]

        @pl.when(n == nb)
        def _():
            dk_ref[...] = ck[...]
            dv_ref[...] = cv[...]

    qcur = pl.BlockSpec((BLK, HD), lambda n: (jnp.minimum(n, nb - 1), 0))
    kcur = pl.BlockSpec((BLK, KVD), lambda n: (jnp.minimum(n, nb - 1), 0))
    kprev = pl.BlockSpec((BLK, KVD), lambda n: (jnp.maximum(n - 1, 0), 0))
    return _pc(body, "attn_bwd", (nb + 1,),
               [pl.BlockSpec(memory_space=pltpu.SMEM), qcur, qcur, kcur, kprev, kcur, kprev],
               [qcur, kprev, kprev, _const((NH, 128))],
               [_sds((T, HD), F32), _sds((T, KVD), F32), _sds((T, KVD), F32), _sds((NH, 128), F32)],
               scratch=[pltpu.VMEM((BLK, KVD), F32), pltpu.VMEM((BLK, KVD), F32)],
               sem=("arbitrary",))(sinks, q, do, k, k, v, v)


def qkv_bwd(dq, dk, dv, dpre_mix, w_q, w_k, w_v, cs, alpha):
    T, HD = dq.shape
    KVD = dk.shape[1]
    D = dpre_mix.shape[1]
    ds = D // NS
    tm = _tile(T, 512)
    scale = 1.0 / (HEAD ** 0.5)

    def body(dq_ref, dk_ref, dv_ref, dp_ref, wq_ref, wk_ref, wv_ref, cs_ref, dqb_ref, dkb_ref, dvb_ref, dx_ref):
        for gq, val in enumerate(_rope(dq_ref[...], cs_ref, -1.0)):
            dqb_ref[:, gq * 128:(gq + 1) * 128] = (val * scale).astype(BF16)
        for gq, val in enumerate(_rope(dk_ref[...], cs_ref, -1.0)):
            dkb_ref[:, gq * 128:(gq + 1) * 128] = val.astype(BF16)
        dvb_ref[...] = dv_ref[...].astype(BF16)
        dqb, dkb, dvb = dqb_ref[...], dkb_ref[...], dvb_ref[...]
        dx_ref[...] = (alpha * dp_ref[...]
                       + lax.dot_general(dqb, _rows_joined(wq_ref), NT, preferred_element_type=F32)
                       + lax.dot_general(dkb, _rows_joined(wk_ref), NT, preferred_element_type=F32)
                       + lax.dot_general(dvb, _rows_joined(wv_ref), NT, preferred_element_type=F32))

    cs_spec = pl.BlockSpec((2, tm, 128), lambda i: (0, i, 0))
    return _pc(body, "qkv_bwd", (T // tm,),
               [_rows(tm, HD), _rows(tm, KVD), _rows(tm, KVD), _rows(tm, D), _wspec(w_q), _wspec(w_k), _wspec(w_v), cs_spec],
               [_rows(tm, HD), _rows(tm, KVD), _rows(tm, KVD), _rows(tm, D)],
               [_sds((T, HD), BF16), _sds((T, KVD), BF16), _sds((T, KVD), BF16), _sds((T, D), F32)],
               sem=("parallel",))(dq, dk, dv, dpre_mix, w_q[0], w_k[0], w_v[0], cs)


def conv_mid_bwd(ds, cv, ln_g, ln_b):
    T, C = cv.shape
    tm = _tile(T, 512)

    def body(ds_ref, cv_ref, g_ref, b_ref, dcv_ref, dg_ref, db_ref, dc_ref):
        xhat, _ = _ln_stats(cv_ref[...])
        ln = xhat * g_ref[...] + b_ref[...]
        sg = _sigmoid(ln)
        dl = ds_ref[...].astype(F32) * (sg * (1.0 + ln * (1.0 - sg)))
        dcv, dg, db = _ln_bwd(dl, cv_ref[...], g_ref[...])
        first = pl.program_id(0) == 0
        _acc_rows(dg_ref, dg, first)
        _acc_rows(db_ref, db, first)
        _acc_rows(dc_ref, jnp.sum(dcv, axis=0, keepdims=True), first)
        dcv_ref[...] = dcv

    return _pc(body, "conv_mid_bwd", (T // tm,), [_rows(tm, C), _rows(tm, C), _const((1, C)), _const((1, C))],
               [_rows(tm, C), _const((1, C)), _const((1, C)), _const((1, C))],
               [_sds((T, C), F32)] + [_sds((1, C), F32)] * 3, sem=("arbitrary",))(ds, cv, ln_g, ln_b)


def dwconv_bwd(dcv, h, w_dw, taps):
    T, C = dcv.shape
    tq = _tile(T)
    nh = tq // HALO
    nblk = T // tq
    off = HALO - (taps - 1)

    def body(d_ref, dn_ref, a_ref, g_ref, ap_ref, gp_ref, w_ref, dh_ref, dw_ref, dbi_ref, su, sus, sd, sds, wb):
        i = pl.program_id(0)
        su[HALO:HALO + tq, :] = a_ref[...].astype(F32) * _sigmoid(g_ref[...].astype(F32))
        up = ap_ref[...].astype(F32) * _sigmoid(gp_ref[...].astype(F32))
        su[0:HALO, :] = jnp.where(i > 0, up, 0.0)
        sd[0:tq, :] = d_ref[...]
        sd[tq:tq + HALO, :] = jnp.where(i < nblk - 1, dn_ref[...], 0.0)
        _phases(su, sus)
        _phases(sd, sds)

        @pl.when(i == 0)
        def _():
            dw_ref[...] = jnp.zeros_like(dw_ref)

        for j in range(taps):
            dw_ref[j:j + 1, :] += jnp.sum(d_ref[...] * _tap(su, sus, off + j, tq), axis=0, keepdims=True)
        sa = jnp.zeros((1, C), F32)
        sb = jnp.zeros((1, C), F32)
        _spread(w_ref, wb, taps)
        for r in range(tq // CONV_ROWS):
            rows = slice(r * CONV_ROWS, (r + 1) * CONV_ROWS)
            dus = [wb[0] * _tap(sd, sds, taps - 1 + r * CONV_ROWS + 8 * k, 8) for k in range(CONV_ROWS // 8)]
            for j in range(1, taps):
                wj = wb[j]
                dus = [acc + wj * _tap(sd, sds, taps - 1 - j + r * CONV_ROWS + 8 * k, 8) for k, acc in enumerate(dus)]
            du = jnp.concatenate(dus, axis=0)
            a = a_ref[rows, :].astype(F32)
            sg = _sigmoid(g_ref[rows, :].astype(F32))
            da = du * sg
            dgt = du * a * sg * (1.0 - sg)
            dh_ref[rows, 0:C] = da.astype(BF16)
            dh_ref[rows, C:2 * C] = dgt.astype(BF16)
            sa = sa + jnp.sum(da, axis=0, keepdims=True)
            sb = sb + jnp.sum(dgt, axis=0, keepdims=True)
        first = i == 0
        _acc_rows(dbi_ref.at[:, 0:C], sa, first)
        _acc_rows(dbi_ref.at[:, C:2 * C], sb, first)

    prev = lambda col: pl.BlockSpec((HALO, C), lambda i: (jnp.maximum(i * nh - 1, 0), col))
    nxt = pl.BlockSpec((HALO, C), lambda i: (jnp.minimum((i + 1) * nh, T // HALO - 1), 0))
    cur = lambda col: pl.BlockSpec((tq, C), lambda i: (i, col))
    return _pc(body, "dwconv_bwd", (nblk,),
               [cur(0), nxt, cur(0), cur(1), prev(0), prev(1), _const((HALO, C))],
               [_rows(tq, 2 * C), _const((HALO, C)), _const((1, 2 * C))],
               [_sds((T, 2 * C), BF16), _sds((HALO, C), F32), _sds((1, 2 * C), F32)],
               scratch=[pltpu.VMEM((HALO + tq, C), F32), pltpu.VMEM((7, HALO + tq, C), F32),
                        pltpu.VMEM((HALO + tq, C), F32), pltpu.VMEM((7, HALO + tq, C), F32), pltpu.VMEM((taps, 8, C), F32)],
               sem=("arbitrary",))(dcv, dcv, h, h, h, h, w_dw)


def conv_in_bwd(dh, dpre_mix, w_in, alpha):
    T, D = dpre_mix.shape
    nw = w_in[0].shape[2]
    tm = _tile(T, 512)

    def body(dh_ref, dp_ref, w_ref, dx_ref):
        acc = alpha * dp_ref[...]
        for j in range(NS):
            acc = acc + lax.dot_general(dh_ref[:, j * nw:(j + 1) * nw], w_ref[j], NT, preferred_element_type=F32)
        dx_ref[...] = acc

    return _pc(body, "conv_in_bwd", (T // tm,), [_rows(tm, NS * nw), _rows(tm, D), _wspec(w_in)], _rows(tm, D),
               _sds((T, D), F32), sem=("parallel",))(dh, dpre_mix, w_in[0])


def wgrad(a, b, row_sharded, name, into):
    prev, out_shape, off = into
    T, Ka = a.shape
    Nb = b.shape[1]
    tt = min(1024, T)
    nt = T // tt
    ka, tn = min(Ka, 1024), min(Nb, 1024)
    if row_sharded:
        sr = Ka // NS
        spb = max(ka // sr, 1)
        rb = ka // spb
        assert out_shape[2] == Nb and off % rb == 0
        out_spec = pl.BlockSpec((spb, rb, tn), lambda i, j, t: (i, off // rb, j))
    else:
        sc = Nb // NS
        spb = max(tn // sc, 1)
        rb = ka
        assert out_shape[2] == sc and off % ka == 0
        out_spec = pl.BlockSpec((spb, ka, tn // spb), lambda i, j, t: (j, off // ka + i, 0))

    def body(a_ref, b_ref, *rest):
        o_ref, acc = rest[-2:]
        t = pl.program_id(2)
        av = a_ref[...]
        if av.dtype != BF16:
            av = av.astype(BF16)
        d = lax.dot_general(av, b_ref[...], TN, preferred_element_type=F32)

        @pl.when(t == 0)
        def _():
            acc[...] = d

        @pl.when(t > 0)
        def _():
            acc[...] += d

        @pl.when(t == nt - 1)
        def _():
            for s in range(spb):
                if row_sharded:
                    o_ref[s] = acc[s * rb:(s + 1) * rb, :].astype(BF16)
                else:
                    o_ref[s] = acc[:, s * (tn // spb):(s + 1) * (tn // spb)].astype(BF16)

    ins = [pl.BlockSpec((tt, ka), lambda i, j, t: (t, i)), pl.BlockSpec((tt, tn), lambda i, j, t: (t, j))]
    args = [a, b]
    kw = {}
    if prev is not None:
        ins.append(ANY)
        args.append(prev)
        kw["input_output_aliases"] = {2: 0}
    return _pc(body, name, (Ka // ka, Nb // tn, nt), ins, out_spec, _sds(out_shape, BF16),
               scratch=[pltpu.VMEM((ka, tn), F32)], sem=("parallel", "parallel", "arbitrary"), **kw)(*args)


def _adamw_math(w, g, m, v):
    c1 = 1.0 - ADAM_B1 ** ADAM_STEP
    c2 = 1.0 - ADAM_B2 ** ADAM_STEP
    mn = ADAM_B1 * m + (1.0 - ADAM_B1) * g
    vn = ADAM_B2 * v + (1.0 - ADAM_B2) * (g * g)
    return -ADAM_LR * ((mn / c1) / (jnp.sqrt(vn / c2) + ADAM_EPS) + ADAM_WD * w), mn, vn


def adamw_layer(w, m, v, layer, gbuf, off, prev, name):
    L, R, W = w.shape
    tr = 256
    assert R % tr == 0 and off % tr == 0

    def body(w_ref, g_ref, m_ref, v_ref, *rest):
        go_ref, d_ref, mo_ref, vo_ref = rest[-4:]
        g = g_ref[...]
        go_ref[...] = g
        d_ref[...], mo_ref[...], vo_ref[...] = _adamw_math(w_ref[...], g, m_ref[...], v_ref[...])

    lay = pl.BlockSpec((None, tr, W), lambda r: (layer, r, 0))
    ins = [lay, pl.BlockSpec((tr, W), lambda r: (off // tr + r, 0)), lay, lay]
    args = [w, gbuf, m, v]
    kw = {}
    if prev is not None:
        ins += [ANY] * 4
        args += list(prev)
        kw["input_output_aliases"] = {4 + k: k for k in range(4)}
    return _pc(body, name, (R // tr,), ins, [lay] * 4, [_sds((L, R, W), F32)] * 4, sem=("parallel",), **kw)(*args)


def adamw_many(ws, gs, ms, vs):
    n = len(ws)

    def body(*refs):
        for k in range(n):
            d, mn, vn = _adamw_math(refs[k][...], refs[n + k][...], refs[2 * n + k][...], refs[3 * n + k][...])
            refs[4 * n + k][...] = d
            refs[5 * n + k][...] = mn
            refs[6 * n + k][...] = vn

    outs = pl.pallas_call(body, name="adamw_small", out_shape=[_sds(a.shape, F32) for a in ws] * 3)(*ws, *gs, *ms, *vs)
    return outs[:n], outs[n:2 * n], outs[2 * n:]


def _rope_tables(T):
    pos = jnp.arange(T, dtype=F32)
    inv_freq = ROPE_THETA ** (-jnp.arange(0, ROPE, 2, dtype=F32) / ROPE)
    ang = pos[:, None] * inv_freq[None, :]
    cos, sin = jnp.cos(ang), jnp.sin(ang)
    pad = HEAD - ROPE
    c = jnp.concatenate([cos, cos, jnp.ones((T, pad), F32)], axis=1)
    s = jnp.concatenate([-sin, sin, jnp.zeros((T, pad), F32)], axis=1)
    return jnp.stack([jnp.tile(c, (1, 128 // HEAD)), jnp.tile(s, (1, 128 // HEAD))])


def _local_step(x, p, target, W, small, lay, hook=None):
    if hook is None:
        hook = lambda stage, after, G, sg=None: None
    T, D = x.shape
    depth = small["mix_ln_g"].shape[0]
    alpha = float((2 * depth) ** 0.25)
    taps = small["taps"]
    row = lambda a, i: a[i:i + 1]
    cs = _rope_tables(T)

    x0b = x.astype(BF16)
    h = conv_in_fwd(x0b, W["conv_w_in"], small["conv_b_in"])
    cv, s = dwconv_fwd(h, small["conv_w_dw"], small["conv_b_dw"], small["conv_ln_g"], small["conv_ln_b"], taps)
    pre_mix0, x1, x1b = mm_res_ln(s, W["conv_w_out"], x, row(small["mix_ln_g"], 0), row(small["mix_ln_b"], 0), alpha,
                                  small["conv_b_out"], "conv_out_fwd")
    hook("weights1", x1b, None)
    r0 = mlp_up_fwd(x1b, W["mlp_w_up0"], "mlp_up_fwd0")
    pre_mlp0, x2, x2b = mm_res_ln(r0, W["mlp_w_down0"], x1, row(small["mlp_ln_g"], 0), row(small["mlp_ln_b"], 0), alpha,
                                  None, "mlp_down_fwd0")
    x3, x3b, pp0, gl0 = ple_fwd(x2, x2b, p, 0, W["ple_w_proj0"], W["ple_w_gate0"], None, "ple_fwd0")

    hook("weights2", x3b, None)
    q, k, v = qkv_fwd(x3b, W["attn_w_q"], W["kv_w_k"], W["kv_w_v"], cs)
    o = attn_fwd(q, k, v, small["attn_sinks"])
    pre_mix1, x4, x4b = mm_res_ln(o, W["attn_w_o"], x3, row(small["mix_ln_g"], 1), row(small["mix_ln_b"], 1), alpha,
                                  None, "attn_out_fwd")
    r1 = mlp_up_fwd(x4b, W["mlp_w_up1"], "mlp_up_fwd1")
    pre_mlp1, x5, x5b = mm_res_ln(r1, W["mlp_w_down1"], x4, row(small["mlp_ln_g"], 1), row(small["mlp_ln_b"], 1), alpha,
                                  None, "mlp_down_fwd1")
    dx6, loss, pp1, gl1 = ple_fwd(x5, x5b, p, 1, W["ple_w_proj1"], W["ple_w_gate1"], target, "ple_fwd1")

    G, sg = {}, {}
    where = {n: (key, off) for key in lay for n, off, _ in lay[key]}
    rows_of = {key: sum(r for _, _, r in lay[key]) for key in lay}

    def wg(name, a, b, row_sharded):
        key, off = where[name]
        shape = (NS, rows_of[key], W[name][0].shape[2])
        G[key] = wgrad(a, b, row_sharded, "wg_" + name, (G.get(key), shape, off))

    dpp1, dgl1, dx5 = ple_bwd(dx6, pp1, gl1, W["ple_w_gate1"], "ple_bwd1")
    wg("ple_w_proj1", p[1], dpp1, False)
    wg("ple_w_gate1", x5b, dgl1, True)
    dpre_mlp1, dpre_mlp1b, dm1, g_mlp_g1, g_mlp_b1 = mlp_bwd1(dx5, pre_mlp1, row(small["mlp_ln_g"], 1), r1,
                                                              W["mlp_w_down1"], "mlp_bwd1_1")
    wg("mlp_w_down1", r1, dpre_mlp1b, True)
    wg("mlp_w_up1", x4b, dm1, False)
    dpre_mix1, dpre_mix1b, do, g_mix_g1, g_mix_b1, _ = mlp_bwd2(dpre_mlp1, dm1, W["mlp_w_up1"], alpha, pre_mix1,
                                                                row(small["mix_ln_g"], 1), W["attn_w_o"], "mlp_bwd2_1")
    wg("attn_w_o", o, dpre_mix1b, True)
    dq, dk, dv, dsinks = attn_bwd(q, k, v, do, small["attn_sinks"])
    dqb, dkb, dvb, dx3 = qkv_bwd(dq, dk, dv, dpre_mix1,
                                 W["attn_w_q"], W["kv_w_k"], W["kv_w_v"], cs, alpha)
    wg("attn_w_q", x3b, dqb, True)
    wg("kv_w_k", x3b, dkb, True)
    wg("kv_w_v", x3b, dvb, True)
    hook("grads3", None, G)

    dpp0, dgl0, dx2 = ple_bwd(dx3, pp0, gl0, W["ple_w_gate0"], "ple_bwd0")
    wg("ple_w_proj0", p[0], dpp0, False)
    wg("ple_w_gate0", x2b, dgl0, True)
    dpre_mlp0, dpre_mlp0b, dm0, g_mlp_g0, g_mlp_b0 = mlp_bwd1(dx2, pre_mlp0, row(small["mlp_ln_g"], 0), r0,
                                                              W["mlp_w_down0"], "mlp_bwd1_0")
    wg("mlp_w_down0", r0, dpre_mlp0b, True)
    wg("mlp_w_up0", x1b, dm0, False)
    hook("grads2", None, G)
    dpre_mix0, dpre_mix0b, dsw, g_mix_g0, g_mix_b0, g_b_out = mlp_bwd2(dpre_mlp0, dm0, W["mlp_w_up0"], alpha, pre_mix0,
                                                                      row(small["mix_ln_g"], 0), W["conv_w_out"],
                                                                      "mlp_bwd2_0")
    wg("conv_w_out", s, dpre_mix0b, True)
    hook("grads1", None, G)
    dcv, g_cln_g, g_cln_b, g_b_dw = conv_mid_bwd(dsw, cv, small["conv_ln_g"], small["conv_ln_b"])
    dh, g_w_dw, g_b_in = dwconv_bwd(dcv, h, small["conv_w_dw"], taps)
    wg("conv_w_in", x0b, dh, False)

    sg["conv_b_in"] = g_b_in
    sg["conv_w_dw"] = g_w_dw
    sg["conv_b_dw"], sg["conv_ln_g"], sg["conv_ln_b"], sg["conv_b_out"] = g_b_dw, g_cln_g, g_cln_b, g_b_out
    sg["mix_ln_g"] = [g_mix_g0, g_mix_g1]
    sg["mix_ln_b"] = [g_mix_b0, g_mix_b1]
    sg["mlp_ln_g"] = [g_mlp_g0, g_mlp_g1]
    sg["mlp_ln_b"] = [g_mlp_b0, g_mlp_b1]
    sg["attn_sinks"] = dsinks[:, 0][None, :]
    sg["loss"] = loss
    hook("grads0", None, G, sg)
    grad_x = conv_in_bwd(dh, dpre_mix0, W["conv_w_in"], alpha)
    return loss, grad_x, G, sg


BUFFERS = (("b0", ("conv_w_in",)), ("a0", ("conv_w_out",)),
           ("a1", ("mlp_w_up0", "mlp_w_down0", "ple_w_gate0")), ("c1", ("ple_w_proj0",)),
           ("a2", ("mlp_w_up1", "mlp_w_down1", "ple_w_gate1", "attn_w_q", "attn_w_o")),
           ("c2", ("kv_w_k", "kv_w_v", "ple_w_proj1")))
GROUPS = (("b0", "a0"), ("a1", "c1"), ("a2", "c2"))
REDUCED = (("b0",), ("a0",), ("a1", "c1"), ("a2", "c2"))
ROW_SHARDED = {"mlp_w_down0", "mlp_w_down1", "ple_w_gate0", "ple_w_gate1", "conv_w_out", "attn_w_q", "attn_w_o", "kv_w_k",
               "kv_w_v"}


def _split_layers(weights):
    out = {"conv_w_in": weights["conv_w_in"][0], "conv_w_out": weights["conv_w_out"][0],
           "attn_w_q": weights["attn_w_q"][0], "attn_w_o": weights["attn_w_o"][0],
           "kv_w_k": weights["kv_w_k"], "kv_w_v": weights["kv_w_v"]}
    for n in ("mlp_w_up", "mlp_w_down", "ple_w_proj", "ple_w_gate"):
        for i in range(weights[n].shape[0]):
            out[n + str(i)] = weights[n][i]
    return out


def _layout(shards):
    lay = {}
    for key, names in BUFFERS:
        off, rows = 0, []
        for n in names:
            rows.append((n, off, shards[n].shape[0]))
            off += shards[n].shape[0]
        lay[key] = rows
    return lay


def _place():
    return lax.axis_index("x"), lax.axis_index("y"), lax.axis_index("c")


def _flip(v, f):
    return (v + f) % 2 if f else v


CHIP_FLIPS = ((1, 0), (0, 1), (1, 1))


HBM = pl.BlockSpec(memory_space=pltpu.HBM)
SEM = pl.BlockSpec(memory_space=pltpu.SEMAPHORE)
EFFECT = pltpu.SideEffectType.DATAFLOW_SIDE_EFFECTING


def _half(ref, rows, c):
    return ref.at[pl.ds(pl.multiple_of(c * (rows // 2), 16), rows // 2), :]


def _gather_copies(refs, shapes, whole, send, recv):
    x, y, c = _place()
    me = 2 * x + y
    na = len(refs)
    cps = []
    for d, (fx, fy) in enumerate(CHIP_FLIPS):
        to = (_flip(x, fx), _flip(y, fy), c)
        for k in range(na):
            mine = refs[k].at[me] if k >= na - whole else _half(refs[k].at[me], shapes[k][1], c)
            cps.append(pltpu.make_async_remote_copy(mine, mine, send.at[d * na + k], recv.at[d * na + k], device_id=to,
                                                    device_id_type=MESH))
    return cps


def gather_start(bufs, whole, after, name):
    na = len(bufs)
    shapes = [b.shape for b in bufs]
    nsem = len(CHIP_FLIPS) * na

    def body(*refs):
        ins = refs[:na]
        send, recv = refs[-(na + 3)], refs[-(na + 2)]
        token = refs[-1]
        for cp in _gather_copies(ins, shapes, whole, send, recv):
            cp.start()
        token[...] = jnp.zeros_like(token)

    args = [pltpu.with_memory_space_constraint(b, pltpu.HBM) for b in bufs]
    ins = [HBM] * na
    if after is not None:
        args.append(after)
        ins.append(ANY)
    return pl.pallas_call(
        body, name=name, in_specs=ins,
        out_specs=[SEM, SEM] + [HBM] * na + [pl.BlockSpec(memory_space=pltpu.VMEM)],
        out_shape=[pltpu.SemaphoreType.DMA((nsem,)), pltpu.SemaphoreType.DMA((nsem,))]
        + [pltpu.HBM(b.shape, b.dtype) for b in bufs] + [_sds((8, 128), F32)],
        input_output_aliases={k: k + 2 for k in range(na)},
        compiler_params=pltpu.CompilerParams(has_side_effects=EFFECT))(*args)


def gather_wait(send, recv, bufs, whole, after, name):
    na = len(bufs)
    shapes = [b.shape for b in bufs]

    def body(*refs):
        ins = refs[:na]
        send_ref, recv_ref = refs[na], refs[na + 1]
        for cp in _gather_copies(ins, shapes, whole, send_ref, recv_ref):
            cp.wait_send()
            cp.wait_recv()

    return pl.pallas_call(
        body, name=name, in_specs=[HBM] * na + [SEM, SEM, ANY], out_specs=[HBM] * na,
        out_shape=[pltpu.HBM(b.shape, b.dtype) for b in bufs], input_output_aliases={k: k for k in range(na)},
        compiler_params=pltpu.CompilerParams(has_side_effects=EFFECT))(*bufs, send, recv, after)


def sibling_forward(bufs, name):
    nb = len(bufs)

    def body(*refs):
        outs = refs[nb:2 * nb]
        send, recv = refs[2 * nb:]
        x, y, c = _place()
        cps = []
        for d, (fx, fy) in enumerate(CHIP_FLIPS):
            frm = 2 * _flip(x, fx) + _flip(y, fy)
            for k in range(nb):
                theirs = _half(outs[k].at[frm], bufs[k].shape[1], c)
                cps.append(pltpu.make_async_remote_copy(theirs, theirs, send.at[d * nb + k], recv.at[d * nb + k],
                                                        device_id=(x, y, 1 - c), device_id_type=MESH))
        for cp in cps:
            cp.start()
        for cp in cps:
            cp.wait()

    nsem = len(CHIP_FLIPS) * nb
    return pl.pallas_call(
        body, name=name, in_specs=[ANY] * nb, out_specs=[ANY] * nb, out_shape=[_sds(b.shape, b.dtype) for b in bufs],
        input_output_aliases={k: k for k in range(nb)},
        scratch_shapes=[pltpu.SemaphoreType.DMA((nsem,)), pltpu.SemaphoreType.DMA((nsem,))])(*bufs)


def pack_rows(pieces, rows, width, name):
    def body(*refs):
        o_ref = refs[-1]
        o_ref[...] = jnp.zeros_like(o_ref)
        for ref, (a, off) in zip(refs[:-1], pieces):
            o_ref[off:off + a.shape[0], 0:a.shape[1]] = ref[...]

    return pl.pallas_call(body, name=name, out_shape=_sds((rows, width), F32))(*[a for a, _ in pieces])


PEER_FLIPS = tuple((fx, fy, fc) for fx in (0, 1) for fy in (0, 1) for fc in (0, 1) if fx or fy or fc)


def _reduce_copies(parts, zones, pack, send, recv):
    x, y, c = _place()
    nb = len(parts)
    na = nb + (1 if pack is not None else 0)
    cps = []
    for f, (fx, fy, fc) in enumerate(PEER_FLIPS):
        tx, ty, tc = _flip(x, fx), _flip(y, fy), _flip(c, fc)
        for k in range(nb):
            hrows = parts[k].shape[1] // 2
            piece = parts[k].at[2 * tx + ty, pl.ds(pl.multiple_of(tc * hrows, 16), hrows), :]
            cps.append(pltpu.make_async_remote_copy(piece, zones[k].at[f], send.at[f * na + k], recv.at[f * na + k],
                                                    device_id=(tx, ty, tc), device_id_type=MESH))
        if pack is not None:
            mine = pack.at[4 * x + 2 * y + c]
            cps.append(pltpu.make_async_remote_copy(mine, mine, send.at[f * na + nb], recv.at[f * na + nb],
                                                    device_id=(tx, ty, tc), device_id_type=MESH))
    return cps


def reduce_begin(parts, pack, name):
    nb = len(parts)
    zones = [lax.empty((len(PEER_FLIPS), g.shape[1] // 2, g.shape[2]), g.dtype) for g in parts]
    arrs = list(parts) + zones + ([pack] if pack is not None else [])
    na = len(arrs)
    nsem = len(PEER_FLIPS) * (nb + (1 if pack is not None else 0))

    def body(*refs):
        ins = refs[:na]
        send, recv = refs[na], refs[na + 1]
        for cp in _reduce_copies(ins[:nb], ins[nb:2 * nb], ins[2 * nb] if pack is not None else None, send, recv):
            cp.start()
        refs[-1][...] = jnp.zeros_like(refs[-1])

    return pl.pallas_call(
        body, name=name, in_specs=[HBM] * na,
        out_specs=[SEM, SEM] + [HBM] * na + [pl.BlockSpec(memory_space=pltpu.VMEM)],
        out_shape=[pltpu.SemaphoreType.DMA((nsem,)), pltpu.SemaphoreType.DMA((nsem,))]
        + [pltpu.HBM(a.shape, a.dtype) for a in arrs] + [_sds((8, 128), F32)],
        input_output_aliases={k: k + 2 for k in range(na)},
        compiler_params=pltpu.CompilerParams(has_side_effects=EFFECT))(
            *[pltpu.with_memory_space_constraint(a, pltpu.HBM) for a in arrs])


def reduce_end(send, recv, parts, zones, pack, after, name):
    nb = len(parts)
    arrs = list(parts) + list(zones) + ([pack] if pack is not None else [])
    na = len(arrs)

    def body(*refs):
        ins = refs[:na]
        for cp in _reduce_copies(ins[:nb], ins[nb:2 * nb], ins[2 * nb] if pack is not None else None, refs[na], refs[na + 1]):
            cp.wait_send()
            cp.wait_recv()

    return pl.pallas_call(
        body, name=name, in_specs=[HBM] * na + [SEM, SEM, ANY], out_specs=[HBM] * na,
        out_shape=[pltpu.HBM(a.shape, a.dtype) for a in arrs], input_output_aliases={k: k for k in range(na)},
        compiler_params=pltpu.CompilerParams(has_side_effects=EFFECT))(*arrs, send, recv, after)


def sibling_share(halves, name):
    nb = len(halves)

    def body(*refs):
        outs = refs[nb:2 * nb]
        send, recv = refs[2 * nb:]
        x, y, c = _place()
        cps = []
        for k in range(nb):
            hrows = halves[k].shape[0] // 2
            mine = outs[k].at[pl.ds(pl.multiple_of(c * hrows, 8), hrows), :]
            cps.append(pltpu.make_async_remote_copy(mine, mine, send.at[k], recv.at[k], device_id=(x, y, 1 - c),
                                                    device_id_type=MESH))
        for cp in cps:
            cp.start()
        for cp in cps:
            cp.wait()

    return pl.pallas_call(
        body, name=name, in_specs=[ANY] * nb, out_specs=[ANY] * nb,
        out_shape=[_sds(h.shape, h.dtype) for h in halves], input_output_aliases={k: k for k in range(nb)},
        scratch_shapes=[pltpu.SemaphoreType.DMA((nb,)), pltpu.SemaphoreType.DMA((nb,))])(*halves)


def _row_tile(rows):
    for cand in (512, 384, 256, 128, 64, 32, 16):
        if rows % cand == 0:
            return cand
    return rows


def piece_sum(g, z, idx, name):
    _, hrows, W = z.shape
    tr = _row_tile(hrows)
    nrb = hrows // tr

    def body(idx_ref, g_ref, z_ref, o_ref):
        acc = g_ref[...].astype(F32)
        for d in range(z.shape[0]):
            acc = acc + z_ref[d].astype(F32)
        o_ref[...] = acc

    gs = pltpu.PrefetchScalarGridSpec(
        num_scalar_prefetch=1, grid=(nrb,),
        in_specs=[pl.BlockSpec((None, tr, W), lambda i, sc: (sc[0], sc[1] * nrb + i, 0)),
                  pl.BlockSpec((z.shape[0], tr, W), lambda i, sc: (0, i, 0))],
        out_specs=pl.BlockSpec((tr, W), lambda i, sc: (sc[1] * nrb + i, 0)))
    return pl.pallas_call(body, name=name, grid_spec=gs, out_shape=_sds((2 * hrows, W), F32),
                          compiler_params=pltpu.CompilerParams(dimension_semantics=("parallel",),
                                                               vmem_limit_bytes=48 * 2 ** 20))(idx, g, z)


def small_sum(packs):
    n, R, W = packs.shape

    def body(p_ref, o_ref):
        acc = p_ref[0]
        for d in range(1, n):
            acc = acc + p_ref[d]
        o_ref[...] = acc

    return pl.pallas_call(body, name="small_sum", out_shape=_sds((R, W), F32))(packs)


WEIGHTS = ["conv_w_in", "conv_b_in", "conv_w_dw", "conv_b_dw", "conv_ln_g", "conv_ln_b", "conv_w_out", "conv_b_out", "kv_w_k",
           "kv_w_v", "attn_w_q", "attn_sinks", "attn_w_o", "mix_ln_g", "mix_ln_b", "mlp_w_up", "mlp_w_down", "mlp_ln_g",
           "mlp_ln_b", "ple_w_proj", "ple_w_gate"]
BIG = ["conv_w_in", "conv_w_out", "kv_w_k", "kv_w_v", "attn_w_q", "attn_w_o", "mlp_w_up", "mlp_w_down", "ple_w_proj",
       "ple_w_gate"]
SMALL = [n for n in WEIGHTS if n not in BIG]


def _step(x, p, target, w, m, v):
    D = x.shape[-1]
    ds = D // NS
    xq, yq, cq = _place()
    chip = 2 * xq + yq
    idx = jnp.stack([chip, cq]).astype(jnp.int32)

    shards = _split_layers(w)
    lay = _layout(shards)
    taps = w["conv_w_dw"].shape[1]
    small_loc = pack_rows([(w["conv_w_dw"][0], 0), (w["conv_b_dw"], HALO), (w["conv_ln_g"], HALO + 1), (w["conv_ln_b"], HALO + 2),
                           (w["conv_b_out"], HALO + 3), (w["conv_b_in"].reshape(2, ds), HALO + 4)], HALO + 8, ds, "pack_small")
    slot = lambda a: lax.dynamic_update_slice(lax.empty((NS,) + a.shape, a.dtype), a[None], (chip, 0, 0))
    started, token = [], None
    for gi, keys in enumerate(GROUPS):
        bufs = [slot(jnp.concatenate([shards[n].astype(BF16) for n, _, _ in lay[key]], axis=0)) for key in keys]
        if gi == 0:
            bufs.append(slot(small_loc))
        send, recv, *thru, token = gather_start(bufs, 1 if gi == 0 else 0, token, "gather_start%d" % gi)
        started.append((send, recv, thru))
    W = {}

    def arrive(gi, after):
        send, recv, thru = started[gi]
        whole = 1 if gi == 0 else 0
        got = gather_wait(send, recv, thru, whole, after, "gather_wait%d" % gi)
        nk = len(GROUPS[gi])
        for key, buf in zip(GROUPS[gi], sibling_forward(got[:nk], "sibling_forward%d" % gi)):
            for n, off, rows in lay[key]:
                W[n] = (buf, off, rows)
        return got[nk:]

    gs, = arrive(0, token)
    across = lambda rows: gs[:, rows, :].transpose(1, 0, 2).reshape(rows.stop - rows.start, D)
    small = {"taps": taps, "conv_w_dw": across(slice(0, HALO)), "conv_b_dw": across(slice(HALO, HALO + 1)),
             "conv_ln_g": across(slice(HALO + 1, HALO + 2)), "conv_ln_b": across(slice(HALO + 2, HALO + 3)),
             "conv_b_out": across(slice(HALO + 3, HALO + 4)), "conv_b_in": gs[:, HALO + 4:HALO + 6, :].reshape(1, 2 * D),
             "attn_sinks": w["attn_sinks"], "mix_ln_g": w["mix_ln_g"], "mix_ln_b": w["mix_ln_b"],
             "mlp_ln_g": w["mlp_ln_g"], "mlp_ln_b": w["mlp_ln_b"]}

    reducing = {}

    def reduce_start(gi, G, pack):
        nk = len(REDUCED[gi])
        send, recv, *thru, token = reduce_begin([G[key] for key in REDUCED[gi]], pack, "reduce_begin%d" % gi)
        reducing[gi] = (send, recv, thru[:nk], thru[nk:2 * nk], thru[2 * nk] if pack is not None else None)
        _FOLLOW.append(token)

    def small_pack(sg):
        pieces = [(sg["conv_b_in"].reshape(2, D), 0), (sg["conv_w_dw"], 2)]
        r0 = 2 + HALO
        for i, n in enumerate(("conv_b_dw", "conv_ln_g", "conv_ln_b", "conv_b_out")):
            pieces.append((sg[n], r0 + i))
        r0 += 4
        for i, n in enumerate(("mix_ln_g", "mix_ln_b", "mlp_ln_g", "mlp_ln_b")):
            pieces += [(sg[n][0], r0 + 2 * i), (sg[n][1], r0 + 2 * i + 1)]
        pieces += [(sg["attn_sinks"], r0 + 8), (sg["loss"][0:1], r0 + 9)]
        mine = pack_rows(pieces, r0 + 10, D, "pack_small_grads")
        return lax.dynamic_update_slice(lax.empty((8,) + mine.shape, F32), mine[None], (4 * xq + 2 * yq + cq, 0, 0))

    def hook(stage, after, G, sg=None):
        if stage == "weights1":
            arrive(1, after)
        elif stage == "weights2":
            arrive(2, after)
        elif stage == "grads0":
            reduce_start(0, G, small_pack(sg))
        elif stage.startswith("grads"):
            reduce_start(int(stage[5:]), G, None)

    loss, grad_x, G, sg = _local_step(x[0], p[:, 0], target[0], W, small, lay, hook)
    _FOLLOW.clear()
    nsink = w["attn_sinks"].shape[1]

    grads, delta, new_m, new_v = {}, {}, {}, {}
    found = {}

    def finish(groups, after, tag):
        keys, halves, tot = [], [], None
        for gi in groups:
            send, recv, parts, zones, pack = reducing[gi]
            done = reduce_end(send, recv, parts, zones, pack, after, "reduce_end%d" % gi)
            nk = len(REDUCED[gi])
            for key, g_, z_ in zip(REDUCED[gi], done[:nk], done[nk:2 * nk]):
                keys.append(key)
                halves.append(piece_sum(g_, z_, idx, "piece_sum_" + key))
            if pack is not None:
                tot = small_sum(done[2 * nk])
        for key, buf in zip(keys, sibling_share(halves, "sibling_share" + tag)):
            for n, off, _ in lay[key]:
                found[n] = (buf, off)
        return tot

    def big_adamw(names):
        for n in names:
            three = lambda a: a.reshape((-1,) + a.shape[-2:])
            w3, m3, v3 = three(w[n]), three(m[n]), three(v[n])
            outs = None
            for i in range(w3.shape[0]):
                buf, off = found[n + str(i)] if n + str(i) in found else found[n]
                outs = adamw_layer(w3, m3, v3, i, buf, off, outs, "adamw_%s%d" % (n, i))
            grads[n], delta[n], new_m[n], new_v[n] = [a.reshape(w[n].shape) for a in outs]

    last = [n for n, _, _ in lay[REDUCED[0][0]]]
    finish(reversed(range(1, len(REDUCED))), grad_x, "1")
    big_adamw([n for n in BIG if n not in last])
    tot = finish([0], new_v["mlp_w_down"], "0")
    big_adamw(last)
    cols = lambda rows: lax.dynamic_slice(rows, (0, chip * ds), (rows.shape[0], ds))
    grads["conv_b_in"] = lax.dynamic_slice(tot[0:2].reshape(1, 2 * D), (0, chip * 2 * ds), (1, 2 * ds))
    grads["conv_w_dw"] = cols(tot[2:2 + taps])[None]
    r0 = 2 + HALO
    for i, n in enumerate(("conv_b_dw", "conv_ln_g", "conv_ln_b", "conv_b_out")):
        grads[n] = cols(tot[r0 + i:r0 + i + 1])
    r0 += 4
    for i, n in enumerate(("mix_ln_g", "mix_ln_b", "mlp_ln_g", "mlp_ln_b")):
        grads[n] = tot[r0 + 2 * i:r0 + 2 * i + 2]
    grads["attn_sinks"] = tot[r0 + 8:r0 + 9, 0:nsink]

    ds_, ms_, vs_ = adamw_many([w[n] for n in SMALL], [grads[n] for n in SMALL], [m[n] for n in SMALL], [v[n] for n in SMALL])
    for n, d_, m_, v_ in zip(SMALL, ds_, ms_, vs_):
        delta[n], new_m[n], new_v[n] = d_, m_, v_

    total = tot[r0 + 9, 0]
    return (total, grad_x[None], *[grads[n] for n in WEIGHTS], *[delta[n] for n in WEIGHTS], *[new_m[n] for n in WEIGHTS],
            *[new_v[n] for n in WEIGHTS])


def kernel(x, p, conv_w_in, conv_b_in, conv_w_dw, conv_b_dw, conv_ln_g, conv_ln_b, conv_w_out, conv_b_out, kv_w_k, kv_w_v, attn_w_q, attn_sinks, attn_w_o, mix_ln_g, mix_ln_b, mlp_w_up, mlp_w_down, mlp_ln_g, mlp_ln_b, ple_w_proj, ple_w_gate, loss_target, m_conv_w_in, m_conv_b_in, m_conv_w_dw, m_conv_b_dw, m_conv_ln_g, m_conv_ln_b, m_conv_w_out, m_conv_b_out, m_kv_w_k, m_kv_w_v, m_attn_w_q, m_attn_sinks, m_attn_w_o, m_mix_ln_g, m_mix_ln_b, m_mlp_w_up, m_mlp_w_down, m_mlp_ln_g, m_mlp_ln_b, m_ple_w_proj, m_ple_w_gate, v_conv_w_in, v_conv_b_in, v_conv_w_dw, v_conv_b_dw, v_conv_ln_g, v_conv_ln_b, v_conv_w_out, v_conv_b_out, v_kv_w_k, v_kv_w_v, v_attn_w_q, v_attn_sinks, v_attn_w_o, v_mix_ln_g, v_mix_ln_b, v_mlp_w_up, v_mlp_w_down, v_mlp_ln_g, v_mlp_ln_b, v_ple_w_proj, v_ple_w_gate):
    w = dict(zip(WEIGHTS, (conv_w_in, conv_b_in, conv_w_dw, conv_b_dw, conv_ln_g, conv_ln_b, conv_w_out, conv_b_out, kv_w_k,
                           kv_w_v, attn_w_q, attn_sinks, attn_w_o, mix_ln_g, mix_ln_b, mlp_w_up, mlp_w_down, mlp_ln_g, mlp_ln_b,
                           ple_w_proj, ple_w_gate)))
    m = dict(zip(WEIGHTS, (m_conv_w_in, m_conv_b_in, m_conv_w_dw, m_conv_b_dw, m_conv_ln_g, m_conv_ln_b, m_conv_w_out,
                           m_conv_b_out, m_kv_w_k, m_kv_w_v, m_attn_w_q, m_attn_sinks, m_attn_w_o, m_mix_ln_g, m_mix_ln_b,
                           m_mlp_w_up, m_mlp_w_down, m_mlp_ln_g, m_mlp_ln_b, m_ple_w_proj, m_ple_w_gate)))
    v = dict(zip(WEIGHTS, (v_conv_w_in, v_conv_b_in, v_conv_w_dw, v_conv_b_dw, v_conv_ln_g, v_conv_ln_b, v_conv_w_out,
                           v_conv_b_out, v_kv_w_k, v_kv_w_v, v_attn_w_q, v_attn_sinks, v_attn_w_o, v_mix_ln_g, v_mix_ln_b,
                           v_mlp_w_up, v_mlp_w_down, v_mlp_ln_g, v_mlp_ln_b, v_ple_w_proj, v_ple_w_gate)))
    return _step(x, p, loss_target, w, m, v)
```

```python
import functools

import jax
import jax.numpy as jnp
from jax import lax
from jax.experimental import pallas as pl
from jax.experimental.pallas import tpu as pltpu

F32 = jnp.float32
BF16 = jnp.bfloat16
NS = 4
HEAD = 64
BLK = 128
ROPE = 16
ROPE_THETA = 500000.0
LN_EPS = 1e-5
NEG = -1e30
HALO = 32
ADAM_LR, ADAM_B1, ADAM_B2, ADAM_EPS, ADAM_WD, ADAM_STEP = 0.001, 0.9, 0.999, 1e-08, 0.01, 10
MESH = pl.DeviceIdType.MESH
ANY = pl.BlockSpec(memory_space=pl.ANY)
NT = (((1,), (1,)), ((), ()))
TN = (((0,), (0,)), ((), ()))


_FOLLOW = []


def _pc(body, name, grid, in_specs, out_specs, out_shape, scratch=(), sem=None, vmem=56, **kw):
    call = lambda fn, ins: pl.pallas_call(
        fn, name=name, grid=grid, in_specs=ins, out_specs=out_specs, out_shape=out_shape,
        scratch_shapes=list(scratch),
        compiler_params=pltpu.CompilerParams(dimension_semantics=sem, vmem_limit_bytes=vmem * 2 ** 20), **kw)
    if not _FOLLOW:
        return call(body, in_specs)
    extra = list(_FOLLOW)
    _FOLLOW.clear()
    n_in = len(in_specs)

    def ordered(*refs):
        return body(*refs[:n_in], *refs[n_in + len(extra):])

    run = call(ordered, list(in_specs) + [ANY] * len(extra))
    return lambda *args: run(*args, *extra)


def _rows(tm, n):
    return pl.BlockSpec((tm, n), lambda i: (i, 0))


def _const(shape):
    return pl.BlockSpec(shape, lambda *_: (0,) * len(shape))


def _wspec(w):
    buf, off, rows = w
    assert off % rows == 0
    return pl.BlockSpec((NS, rows, buf.shape[2]), lambda *_: (0, off // rows, 0))


def _rows_joined(w_ref):
    n, r, c = w_ref.shape
    return w_ref[...].reshape(n * r, c)


def _sds(shape, dtype):
    return jax.ShapeDtypeStruct(shape, dtype)


def _tile(t, rows=256):
    return min(rows, t)


def _sigmoid(x):
    return 0.5 * jnp.tanh(0.5 * x) + 0.5


def _ln_stats(w):
    mu = jnp.mean(w, axis=-1, keepdims=True)
    xc = w - mu
    var = jnp.mean(xc * xc, axis=-1, keepdims=True)
    rstd = lax.rsqrt(var + LN_EPS)
    return xc * rstd, rstd


def _ln_bwd(dy, w, g):
    xhat, rstd = _ln_stats(w)
    dxhat = dy * g
    m1 = jnp.mean(dxhat, axis=-1, keepdims=True)
    m2 = jnp.mean(dxhat * xhat, axis=-1, keepdims=True)
    dw = rstd * (dxhat - m1 - xhat * m2)
    return dw, jnp.sum(dy * xhat, axis=0, keepdims=True), jnp.sum(dy, axis=0, keepdims=True)


def _acc_rows(ref, val, first):
    @pl.when(first)
    def _():
        ref[...] = val

    @pl.when(jnp.logical_not(first))
    def _():
        ref[...] += val


def conv_in_fwd(xb, w_in, b_in):
    T, D = xb.shape
    nw = w_in[0].shape[2]
    tm = _tile(T, 512)

    def body(x_ref, w_ref, b_ref, h_ref):
        x = x_ref[...].astype(BF16)
        for j in range(NS):
            sl = slice(j * nw, (j + 1) * nw)
            h_ref[:, sl] = (jnp.dot(x, w_ref[j], preferred_element_type=F32) + b_ref[:, sl]).astype(BF16)

    return _pc(body, "conv_in_fwd", (T // tm,), [_rows(tm, D), _wspec(w_in), _const((1, NS * nw))],
               _rows(tm, NS * nw), _sds((T, NS * nw), BF16), sem=("parallel",))(xb, w_in[0], b_in)


CONV_ROWS = 16


def _phases(scr, sh):
    n = scr.shape[0] - 8
    for b in range(1, 8):
        sh[b - 1, 0:n, :] = scr[b:b + n, :]


def _spread(w_ref, wb, taps):
    for j in range(taps):
        wb[j] = jnp.broadcast_to(w_ref[j:j + 1, :], wb.shape[1:])


def _tap(scr, sh, o, n):
    b = o % 8
    return scr[o:o + n, :] if b == 0 else sh[b - 1, o - b:o - b + n, :]


def dwconv_fwd(h, w_dw, b_dw, ln_g, ln_b, taps):
    T = h.shape[0]
    C = h.shape[1] // 2
    tq = _tile(T)
    nh = tq // HALO
    off = HALO - (taps - 1)

    def body(a_ref, g_ref, ap_ref, gp_ref, w_ref, bdw_ref, lg_ref, lb_ref, cv_ref, s_ref, scr, sh, wb):
        i = pl.program_id(0)
        scr[HALO:HALO + tq, :] = a_ref[...].astype(F32) * _sigmoid(g_ref[...].astype(F32))
        up = ap_ref[...].astype(F32) * _sigmoid(gp_ref[...].astype(F32))
        scr[0:HALO, :] = jnp.where(i > 0, up, 0.0)
        _phases(scr, sh)
        _spread(w_ref, wb, taps)
        bias = jnp.broadcast_to(bdw_ref[...], (8, C))
        for r in range(tq // CONV_ROWS):
            accs = [bias] * (CONV_ROWS // 8)
            for j in range(taps):
                wj = wb[j]
                accs = [acc + wj * _tap(scr, sh, off + j + r * CONV_ROWS + 8 * k, 8) for k, acc in enumerate(accs)]
            for k, acc in enumerate(accs):
                cv_ref[r * CONV_ROWS + 8 * k:r * CONV_ROWS + 8 * k + 8, :] = acc
        xhat, _ = _ln_stats(cv_ref[...])
        ln = xhat * lg_ref[...] + lb_ref[...]
        s_ref[...] = (ln * _sigmoid(ln)).astype(BF16)

    prev = lambda col: pl.BlockSpec((HALO, C), lambda i: (jnp.maximum(i * nh - 1, 0), col))
    cur = lambda col: pl.BlockSpec((tq, C), lambda i: (i, col))
    return _pc(body, "dwconv_fwd", (T // tq,),
               [cur(0), cur(1), prev(0), prev(1), _const((HALO, C)), _const((1, C)), _const((1, C)), _const((1, C))],
               [_rows(tq, C), _rows(tq, C)], [_sds((T, C), F32), _sds((T, C), BF16)],
               scratch=[pltpu.VMEM((HALO + tq, C), F32), pltpu.VMEM((7, HALO + tq, C), F32), pltpu.VMEM((taps, 8, C), F32)],
               sem=("parallel",))(h, h, h, h, w_dw, b_dw, ln_g, ln_b)


def mm_res_ln(a, w, res, g, b, alpha, bias, name):
    T, K = a.shape
    ks = K // NS
    D = res.shape[1]
    tm = _tile(T, 512 if K <= D else 256)

    def body(*refs):
        a_ref, w_ref, res_ref, g_ref, b_ref = refs[:5]
        n = 5
        if bias is not None:
            bias_ref = refs[5]
            n = 6
        pre_ref, xo_ref, xb_ref = refs[n:n + 3]
        acc = jnp.dot(a_ref[...], _rows_joined(w_ref), preferred_element_type=F32)
        if bias is not None:
            acc = acc + bias_ref[...]
        pre = alpha * res_ref[...] + acc
        xhat, _ = _ln_stats(pre)
        xo = xhat * g_ref[...] + b_ref[...]
        pre_ref[...] = pre
        xo_ref[...] = xo
        xb_ref[...] = xo.astype(BF16)

    ins = [_rows(tm, K), _wspec(w), _rows(tm, D), _const((1, D)), _const((1, D))]
    args = [a, w[0], res, g, b]
    if bias is not None:
        ins.append(_const((1, D)))
        args.append(bias)
    return _pc(body, name, (T // tm,), ins, [_rows(tm, D)] * 3, [_sds((T, D), F32), _sds((T, D), F32), _sds((T, D), BF16)],
               sem=("parallel",))(*args)


def mlp_up_fwd(xb, w_up, name):
    T, D = xb.shape
    fs = w_up[0].shape[2]
    tm = _tile(T)

    def body(x_ref, w_ref, r_ref):
        x = x_ref[...]
        for j in range(NS):
            m = jnp.maximum(jnp.dot(x, w_ref[j], preferred_element_type=F32), 0.0)
            r_ref[:, j * fs:(j + 1) * fs] = (m * m).astype(BF16)

    return _pc(body, name, (T // tm,), [_rows(tm, D), _wspec(w_up)], _rows(tm, NS * fs), _sds((T, NS * fs), BF16),
               sem=("parallel",))(xb, w_up[0])


def ple_fwd(x, xb, p, layer, w_proj, w_gate, target, name):
    T, D = x.shape
    P = p.shape[2]
    ds = D // NS
    tm = _tile(T, 512)
    last = target is not None

    def body(*refs):
        x_ref, xb_ref, p_ref, wp_ref, wg_ref = refs[:5]
        n = 5
        if last:
            t_ref = refs[5]
            n = 6
        o_ref, o2_ref, pp_ref, gl_ref = refs[n:n + 4]
        gl = jnp.dot(xb_ref[...], _rows_joined(wg_ref), preferred_element_type=F32)
        gl_ref[...] = gl.astype(BF16)
        sg = _sigmoid(gl)
        pb = p_ref[...].astype(BF16)
        sq = jnp.zeros((1, 1), F32)
        for j in range(NS):
            sl = slice(j * ds, (j + 1) * ds)
            pp = jnp.dot(pb, wp_ref[j], preferred_element_type=F32)
            pp_ref[:, sl] = pp.astype(BF16)
            out = x_ref[:, sl] + pp * sg[:, sl]
            if last:
                err = out - t_ref[:, sl]
                o_ref[:, sl] = err * (1.0 / D)
                e2 = jnp.sum(err * err, axis=0, keepdims=True)
                sq = sq + jnp.sum(e2, axis=1, keepdims=True)
            else:
                o_ref[:, sl] = out
                o2_ref[:, sl] = out.astype(BF16)
        if last:
            _acc_rows(o2_ref, jnp.broadcast_to(sq * (0.5 / D), (8, 128)), pl.program_id(0) == 0)

    ins = [_rows(tm, D), _rows(tm, D), pl.BlockSpec((None, tm, P), lambda i: (layer, i, 0)), _wspec(w_proj), _wspec(w_gate)]
    args = [x, xb, p, w_proj[0], w_gate[0]]
    if last:
        ins.append(_rows(tm, D))
        args.append(target)
        outs = [_rows(tm, D), _const((8, 128)), _rows(tm, D), _rows(tm, D)]
        shapes = [_sds((T, D), F32), _sds((8, 128), F32), _sds((T, D), BF16), _sds((T, D), BF16)]
    else:
        outs = [_rows(tm, D)] * 4
        shapes = [_sds((T, D), F32), _sds((T, D), BF16), _sds((T, D), BF16), _sds((T, D), BF16)]
    return _pc(body, name, (T // tm,), ins, outs, shapes, sem=("arbitrary",) if last else ("parallel",))(*args)


def _rope(x, cs_ref, sign):
    c = cs_ref[0]
    s = cs_ref[1] * sign
    lane = lax.broadcasted_iota(jnp.int32, c.shape, 1)
    first = (lane % HEAD) < (ROPE // 2)
    outs = []
    for gq in range(x.shape[1] // 128):
        xg = x[:, gq * 128:(gq + 1) * 128]
        sw = jnp.where(first, pltpu.roll(xg, 128 - ROPE // 2, 1), pltpu.roll(xg, ROPE // 2, 1))
        outs.append(xg * c + sw * s)
    return outs


def qkv_fwd(xb, w_q, w_k, w_v, cs):
    T, D = xb.shape
    ds = D // NS
    HD, KVD = w_q[0].shape[2], w_k[0].shape[2]
    tm = _tile(T, 512)
    scale = 1.0 / (HEAD ** 0.5)

    def body(x_ref, wq_ref, wk_ref, wv_ref, cs_ref, q_ref, k_ref, v_ref):
        def proj(w_ref):
            return jnp.dot(x_ref[...], _rows_joined(w_ref), preferred_element_type=F32)

        for gq, val in enumerate(_rope(proj(wq_ref), cs_ref, 1.0)):
            q_ref[:, gq * 128:(gq + 1) * 128] = (val * scale).astype(BF16)
        for gq, val in enumerate(_rope(proj(wk_ref), cs_ref, 1.0)):
            k_ref[:, gq * 128:(gq + 1) * 128] = val.astype(BF16)
        v_ref[...] = proj(wv_ref).astype(BF16)

    cs_spec = pl.BlockSpec((2, tm, 128), lambda i: (0, i, 0))
    return _pc(body, "qkv_fwd", (T // tm,), [_rows(tm, D), _wspec(w_q), _wspec(w_k), _wspec(w_v), cs_spec],
               [_rows(tm, HD), _rows(tm, KVD), _rows(tm, KVD)],
               [_sds((T, HD), BF16), _sds((T, KVD), BF16), _sds((T, KVD), BF16)], sem=("parallel",))(
                   xb, w_q[0], w_k[0], w_v[0], cs)


def _band_mask(n):
    row = lax.broadcasted_iota(jnp.int32, (BLK, 2 * BLK), 0)
    col = lax.broadcasted_iota(jnp.int32, (BLK, 2 * BLK), 1)
    return (col > row) & (col <= row + BLK) & ((col >= BLK) | (n > 0))


def _head(h):
    return slice(h * HEAD, (h + 1) * HEAD)


def _softmax_sink(s, sink):
    m = jnp.maximum(jnp.max(s, axis=-1, keepdims=True), sink)
    e = jnp.exp(s - m)
    es = jnp.exp(sink - m)
    den = jnp.sum(e, axis=-1, keepdims=True) + es
    inv = 1.0 / den
    return e * inv, es * inv


def attn_fwd(q, k, v, sinks):
    T, HD = q.shape
    KVD = k.shape[1]
    NKV = KVD // HEAD
    G = HD // KVD

    def body(s_ref, q_ref, kc_ref, kp_ref, vc_ref, vp_ref, o_ref):
        valid = _band_mask(pl.program_id(0))
        for kh in range(NKV):
            k2 = jnp.concatenate([kp_ref[:, _head(kh)], kc_ref[:, _head(kh)]], axis=0)
            v2 = jnp.concatenate([vp_ref[:, _head(kh)], vc_ref[:, _head(kh)]], axis=0)
            hs = [kh * G + gq for gq in range(G)]
            sc = [lax.dot_general(q_ref[:, _head(hh)], k2, NT, preferred_element_type=F32) for hh in hs]
            pb = [_softmax_sink(jnp.where(valid, s, NEG), s_ref[0, hh])[0].astype(BF16) for s, hh in zip(sc, hs)]
            for p, hh in zip(pb, hs):
                o_ref[:, _head(hh)] = jnp.dot(p, v2, preferred_element_type=F32).astype(BF16)

    cur = lambda n_: pl.BlockSpec((BLK, n_), lambda n: (n, 0))
    prev = lambda n_: pl.BlockSpec((BLK, n_), lambda n: (jnp.maximum(n - 1, 0), 0))
    return _pc(body, "attn_fwd", (T // BLK,),
               [pl.BlockSpec(memory_space=pltpu.SMEM), cur(HD), cur(KVD), prev(KVD), cur(KVD), prev(KVD)],
               cur(HD), _sds((T, HD), BF16), sem=("parallel",))(sinks, q, k, k, v, v)


def ple_bwd(dxo, pp, gl, w_gate, name):
    T, D = dxo.shape
    ds = D // NS
    tm = _tile(T, 512)

    def body(d_ref, pp_ref, gl_ref, wg_ref, dpp_ref, dgl_ref, dx_ref):
        d = d_ref[...]
        sg = _sigmoid(gl_ref[...].astype(F32))
        dpp_ref[...] = (d * sg).astype(BF16)
        dgl = (d * pp_ref[...].astype(F32) * sg * (1.0 - sg)).astype(BF16)
        dgl_ref[...] = dgl
        dx_ref[...] = d + lax.dot_general(dgl, _rows_joined(wg_ref), NT, preferred_element_type=F32)

    return _pc(body, name, (T // tm,), [_rows(tm, D)] * 3 + [_wspec(w_gate)], [_rows(tm, D)] * 3,
               [_sds((T, D), BF16), _sds((T, D), BF16), _sds((T, D), F32)], sem=("parallel",))(dxo, pp, gl, w_gate[0])


def mlp_bwd1(dy, pre, g, r, w_down, name):
    T, D = dy.shape
    fs = w_down[2]
    tm = _tile(T)

    def body(dy_ref, pre_ref, g_ref, r_ref, w_ref, dw_ref, dwb_ref, dm_ref, dg_ref, db_ref):
        dw, dg, db = _ln_bwd(dy_ref[...], pre_ref[...], g_ref[...])
        first = pl.program_id(0) == 0
        _acc_rows(dg_ref, dg, first)
        _acc_rows(db_ref, db, first)
        dwb = dw.astype(BF16)
        dw_ref[...] = dw
        dwb_ref[...] = dwb
        for j in range(NS):
            sl = slice(j * fs, (j + 1) * fs)
            dr = lax.dot_general(dwb, w_ref[j], NT, preferred_element_type=F32)
            dm_ref[:, sl] = (dr * (2.0 * jnp.sqrt(r_ref[:, sl].astype(F32)))).astype(BF16)

    return _pc(body, name, (T // tm,), [_rows(tm, D), _rows(tm, D), _const((1, D)), _rows(tm, NS * fs), _wspec(w_down)],
               [_rows(tm, D), _rows(tm, D), _rows(tm, NS * fs), _const((1, D)), _const((1, D))],
               [_sds((T, D), F32), _sds((T, D), BF16), _sds((T, NS * fs), BF16), _sds((1, D), F32), _sds((1, D), F32)],
               sem=("arbitrary",))(dy, pre, g, r, w_down[0])


def mlp_bwd2(dpre, dm, w_up, alpha, pre_mix, g_mix, w_mix, name):
    T, D = dpre.shape
    fs = w_up[0].shape[2]
    ms = w_mix[2]
    tm = _tile(T)

    def body(dp_ref, dm_ref, wu_ref, pre_ref, g_ref, wm_ref, dw_ref, dwb_ref, do_ref, dg_ref, db_ref, dc_ref):
        dy = alpha * dp_ref[...]
        for j in range(NS):
            dy = dy + lax.dot_general(dm_ref[:, j * fs:(j + 1) * fs], wu_ref[j], NT, preferred_element_type=F32)
        dw, dg, db = _ln_bwd(dy, pre_ref[...], g_ref[...])
        first = pl.program_id(0) == 0
        _acc_rows(dg_ref, dg, first)
        _acc_rows(db_ref, db, first)
        _acc_rows(dc_ref, jnp.sum(dw, axis=0, keepdims=True), first)
        dwb = dw.astype(BF16)
        dw_ref[...] = dw
        dwb_ref[...] = dwb
        do_ref[...] = lax.dot_general(dwb, _rows_joined(wm_ref), NT, preferred_element_type=F32).astype(BF16)

    return _pc(body, name, (T // tm,),
               [_rows(tm, D), _rows(tm, NS * fs), _wspec(w_up), _rows(tm, D), _const((1, D)), _wspec(w_mix)],
               [_rows(tm, D), _rows(tm, D), _rows(tm, NS * ms), _const((1, D)), _const((1, D)), _const((1, D))],
               [_sds((T, D), F32), _sds((T, D), BF16), _sds((T, NS * ms), BF16)] + [_sds((1, D), F32)] * 3,
               sem=("arbitrary",))(dpre, dm, w_up[0], pre_mix, g_mix, w_mix[0])


def attn_bwd(q, k, v, do, sinks):
    T, HD = q.shape
    KVD = k.shape[1]
    NH, NKV = HD // HEAD, KVD // HEAD
    G = NH // NKV
    nb = T // BLK

    def body(s_ref, q_ref, do_ref, kc_ref, kp_ref, vc_ref, vp_ref, dq_ref, dk_ref, dv_ref, ds_ref, ck, cv):
        n = pl.program_id(0)

        @pl.when(n == 0)
        def _():
            ck[...] = jnp.zeros_like(ck)
            cv[...] = jnp.zeros_like(cv)
            ds_ref[...] = jnp.zeros_like(ds_ref)

        @pl.when(n < nb)
        def _():
            valid = _band_mask(n)
            for kh in range(NKV):
                kv = _head(kh)
                k2 = jnp.concatenate([kp_ref[:, kv], kc_ref[:, kv]], axis=0)
                v2 = jnp.concatenate([vp_ref[:, kv], vc_ref[:, kv]], axis=0)
                hs = [kh * G + gq for gq in range(G)]
                qs = [q_ref[:, _head(hh)] for hh in hs]
                dos = [do_ref[:, _head(hh)] for hh in hs]
                sc = [lax.dot_general(qh, k2, NT, preferred_element_type=F32) for qh in qs]
                dp = [lax.dot_general(doh, v2, NT, preferred_element_type=F32) for doh in dos]
                pr = [_softmax_sink(jnp.where(valid, s, NEG), s_ref[0, hh]) for s, hh in zip(sc, hs)]
                delta = [jnp.sum(p * d, axis=-1, keepdims=True) for (p, _), d in zip(pr, dp)]
                dsb = [(p * (d - dl)).astype(BF16) for (p, _), d, dl in zip(pr, dp, delta)]
                pb = [p.astype(BF16) for p, _ in pr]
                for (_, ps), dl, hh in zip(pr, delta, hs):
                    ds_ref[hh:hh + 1, :] += jnp.broadcast_to(-jnp.sum(ps * dl, axis=0, keepdims=True), (1, 128))
                for d, hh in zip(dsb, hs):
                    dq_ref[:, _head(hh)] = jnp.dot(d, k2, preferred_element_type=F32)
                dk2 = lax.dot_general(jnp.concatenate(dsb, axis=0), jnp.concatenate(qs, axis=0), TN,
                                      preferred_element_type=F32)
                dv2 = lax.dot_general(jnp.concatenate(pb, axis=0), jnp.concatenate(dos, axis=0), TN,
                                      preferred_element_type=F32)
                dk_ref[:, kv] = ck[:, kv] + dk2[0:BLK]
                dv_ref[:, kv] = cv[:, kv] + dv2[0:BLK]
                ck[:, kv] = dk2[BLK:2 * BLK]
                cv[:, kv] = dv2[BLK:2 * BLK]

        @pl.when(n == nb)
        def _():
            dk_ref[...] = ck[...]
            dv_ref[...] = cv[...]

    qcur = pl.BlockSpec((BLK, HD), lambda n: (jnp.minimum(n, nb - 1), 0))
    kcur = pl.BlockSpec((BLK, KVD), lambda n: (jnp.minimum(n, nb - 1), 0))
    kprev = pl.BlockSpec((BLK, KVD), lambda n: (jnp.maximum(n - 1, 0), 0))
    return _pc(body, "attn_bwd", (nb + 1,),
               [pl.BlockSpec(memory_space=pltpu.SMEM), qcur, qcur, kcur, kprev, kcur, kprev],
               [qcur, kprev, kprev, _const((NH, 128))],
               [_sds((T, HD), F32), _sds((T, KVD), F32), _sds((T, KVD), F32), _sds((NH, 128), F32)],
               scratch=[pltpu.VMEM((BLK, KVD), F32), pltpu.VMEM((BLK, KVD), F32)],
               sem=("arbitrary",))(sinks, q, do, k, k, v, v)


def qkv_bwd(dq, dk, dv, dpre_mix, w_q, w_k, w_v, cs, alpha):
    T, HD = dq.shape
    KVD = dk.shape[1]
    D = dpre_mix.shape[1]
    ds = D // NS
    tm = _tile(T, 512)
    scale = 1.0 / (HEAD ** 0.5)

    def body(dq_ref, dk_ref, dv_ref, dp_ref, wq_ref, wk_ref, wv_ref, cs_ref, dqb_ref, dkb_ref, dvb_ref, dx_ref):
        for gq, val in enumerate(_rope(dq_ref[...], cs_ref, -1.0)):
            dqb_ref[:, gq * 128:(gq + 1) * 128] = (val * scale).astype(BF16)
        for gq, val in enumerate(_rope(dk_ref[...], cs_ref, -1.0)):
            dkb_ref[:, gq * 128:(gq + 1) * 128] = val.astype(BF16)
        dvb_ref[...] = dv_ref[...].astype(BF16)
        dqb, dkb, dvb = dqb_ref[...], dkb_ref[...], dvb_ref[...]
        dx_ref[...] = (alpha * dp_ref[...]
                       + lax.dot_general(dqb, _rows_joined(wq_ref), NT, preferred_element_type=F32)
                       + lax.dot_general(dkb, _rows_joined(wk_ref), NT, preferred_element_type=F32)
                       + lax.dot_general(dvb, _rows_joined(wv_ref), NT, preferred_element_type=F32))

    cs_spec = pl.BlockSpec((2, tm, 128), lambda i: (0, i, 0))
    return _pc(body, "qkv_bwd", (T // tm,),
               [_rows(tm, HD), _rows(tm, KVD), _rows(tm, KVD), _rows(tm, D), _wspec(w_q), _wspec(w_k), _wspec(w_v), cs_spec],
               [_rows(tm, HD), _rows(tm, KVD), _rows(tm, KVD), _rows(tm, D)],
               [_sds((T, HD), BF16), _sds((T, KVD), BF16), _sds((T, KVD), BF16), _sds((T, D), F32)],
               sem=("parallel",))(dq, dk, dv, dpre_mix, w_q[0], w_k[0], w_v[0], cs)


def conv_mid_bwd(ds, cv, ln_g, ln_b):
    T, C = cv.shape
    tm = _tile(T, 512)

    def body(ds_ref, cv_ref, g_ref, b_ref, dcv_ref, dg_ref, db_ref, dc_ref):
        xhat, _ = _ln_stats(cv_ref[...])
        ln = xhat * g_ref[...] + b_ref[...]
        sg = _sigmoid(ln)
        dl = ds_ref[...].astype(F32) * (sg * (1.0 + ln * (1.0 - sg)))
        dcv, dg, db = _ln_bwd(dl, cv_ref[...], g_ref[...])
        first = pl.program_id(0) == 0
        _acc_rows(dg_ref, dg, first)
        _acc_rows(db_ref, db, first)
        _acc_rows(dc_ref, jnp.sum(dcv, axis=0, keepdims=True), first)
        dcv_ref[...] = dcv

    return _pc(body, "conv_mid_bwd", (T // tm,), [_rows(tm, C), _rows(tm, C), _const((1, C)), _const((1, C))],
               [_rows(tm, C), _const((1, C)), _const((1, C)), _const((1, C))],
               [_sds((T, C), F32)] + [_sds((1, C), F32)] * 3, sem=("arbitrary",))(ds, cv, ln_g, ln_b)


def dwconv_bwd(dcv, h, w_dw, taps):
    T, C = dcv.shape
    tq = _tile(T)
    nh = tq // HALO
    nblk = T // tq
    off = HALO - (taps - 1)

    def body(d_ref, dn_ref, a_ref, g_ref, ap_ref, gp_ref, w_ref, dh_ref, dw_ref, dbi_ref, su, sus, sd, sds, wb):
        i = pl.program_id(0)
        su[HALO:HALO + tq, :] = a_ref[...].astype(F32) * _sigmoid(g_ref[...].astype(F32))
        up = ap_ref[...].astype(F32) * _sigmoid(gp_ref[...].astype(F32))
        su[0:HALO, :] = jnp.where(i > 0, up, 0.0)
        sd[0:tq, :] = d_ref[...]
        sd[tq:tq + HALO, :] = jnp.where(i < nblk - 1, dn_ref[...], 0.0)
        _phases(su, sus)
        _phases(sd, sds)

        @pl.when(i == 0)
        def _():
            dw_ref[...] = jnp.zeros_like(dw_ref)

        for j in range(taps):
            dw_ref[j:j + 1, :] += jnp.sum(d_ref[...] * _tap(su, sus, off + j, tq), axis=0, keepdims=True)
        sa = jnp.zeros((1, C), F32)
        sb = jnp.zeros((1, C), F32)
        _spread(w_ref, wb, taps)
        for r in range(tq // CONV_ROWS):
            rows = slice(r * CONV_ROWS, (r + 1) * CONV_ROWS)
            dus = [wb[0] * _tap(sd, sds, taps - 1 + r * CONV_ROWS + 8 * k, 8) for k in range(CONV_ROWS // 8)]
            for j in range(1, taps):
                wj = wb[j]
                dus = [acc + wj * _tap(sd, sds, taps - 1 - j + r * CONV_ROWS + 8 * k, 8) for k, acc in enumerate(dus)]
            du = jnp.concatenate(dus, axis=0)
            a = a_ref[rows, :].astype(F32)
            sg = _sigmoid(g_ref[rows, :].astype(F32))
            da = du * sg
            dgt = du * a * sg * (1.0 - sg)
            dh_ref[rows, 0:C] = da.astype(BF16)
            dh_ref[rows, C:2 * C] = dgt.astype(BF16)
            sa = sa + jnp.sum(da, axis=0, keepdims=True)
            sb = sb + jnp.sum(dgt, axis=0, keepdims=True)
        first = i == 0
        _acc_rows(dbi_ref.at[:, 0:C], sa, first)
        _acc_rows(dbi_ref.at[:, C:2 * C], sb, first)

    prev = lambda col: pl.BlockSpec((HALO, C), lambda i: (jnp.maximum(i * nh - 1, 0), col))
    nxt = pl.BlockSpec((HALO, C), lambda i: (jnp.minimum((i + 1) * nh, T // HALO - 1), 0))
    cur = lambda col: pl.BlockSpec((tq, C), lambda i: (i, col))
    return _pc(body, "dwconv_bwd", (nblk,),
               [cur(0), nxt, cur(0), cur(1), prev(0), prev(1), _const((HALO, C))],
               [_rows(tq, 2 * C), _const((HALO, C)), _const((1, 2 * C))],
               [_sds((T, 2 * C), BF16), _sds((HALO, C), F32), _sds((1, 2 * C), F32)],
               scratch=[pltpu.VMEM((HALO + tq, C), F32), pltpu.VMEM((7, HALO + tq, C), F32),
                        pltpu.VMEM((HALO + tq, C), F32), pltpu.VMEM((7, HALO + tq, C), F32), pltpu.VMEM((taps, 8, C), F32)],
               sem=("arbitrary",))(dcv, dcv, h, h, h, h, w_dw)


def conv_in_bwd(dh, dpre_mix, w_in, alpha):
    T, D = dpre_mix.shape
    nw = w_in[0].shape[2]
    tm = _tile(T, 512)

    def body(dh_ref, dp_ref, w_ref, dx_ref):
        acc = alpha * dp_ref[...]
        for j in range(NS):
            acc = acc + lax.dot_general(dh_ref[:, j * nw:(j + 1) * nw], w_ref[j], NT, preferred_element_type=F32)
        dx_ref[...] = acc

    return _pc(body, "conv_in_bwd", (T // tm,), [_rows(tm, NS * nw), _rows(tm, D), _wspec(w_in)], _rows(tm, D),
               _sds((T, D), F32), sem=("parallel",))(dh, dpre_mix, w_in[0])


def wgrad(a, b, row_sharded, name, into):
    prev, out_shape, off = into
    layer = None
    if isinstance(a, tuple):
        layer, a = a
    T, Ka = a.shape[-2:]
    Nb = b.shape[1]
    tt = min(1024, T)
    nt = T // tt
    ka, tn = min(Ka, 1024), min(Nb, 1024)
    if row_sharded:
        sr = Ka // NS
        spb = max(ka // sr, 1)
        rb = ka // spb
        assert out_shape[2] == Nb and off % rb == 0
        out_spec = pl.BlockSpec((spb, rb, tn), lambda i, j, t: (i, off // rb, j))
    else:
        sc = Nb // NS
        spb = max(tn // sc, 1)
        rb = ka
        assert out_shape[2] == sc and off % ka == 0
        out_spec = pl.BlockSpec((spb, ka, tn // spb), lambda i, j, t: (j, off // ka + i, 0))

    def body(a_ref, b_ref, *rest):
        o_ref, acc = rest[-2:]
        t = pl.program_id(2)
        av = a_ref[...]
        if av.dtype != BF16:
            av = av.astype(BF16)
        d = lax.dot_general(av, b_ref[...], TN, preferred_element_type=F32)

        @pl.when(t == 0)
        def _():
            acc[...] = d

        @pl.when(t > 0)
        def _():
            acc[...] += d

        @pl.when(t == nt - 1)
        def _():
            for s in range(spb):
                if row_sharded:
                    o_ref[s] = acc[s * rb:(s + 1) * rb, :].astype(BF16)
                else:
                    o_ref[s] = acc[:, s * (tn // spb):(s + 1) * (tn // spb)].astype(BF16)

    a_spec = (pl.BlockSpec((tt, ka), lambda i, j, t: (t, i)) if layer is None
              else pl.BlockSpec((None, tt, ka), lambda i, j, t: (layer, t, i)))
    ins = [a_spec, pl.BlockSpec((tt, tn), lambda i, j, t: (t, j))]
    args = [a, b]
    kw = {}
    if prev is not None:
        ins.append(ANY)
        args.append(prev)
        kw["input_output_aliases"] = {2: 0}
    return _pc(body, name, (Ka // ka, Nb // tn, nt), ins, out_spec, _sds(out_shape, BF16),
               scratch=[pltpu.VMEM((ka, tn), F32)], sem=("parallel", "parallel", "arbitrary"), **kw)(*args)


def _adamw_math(w, g, m, v):
    c1 = 1.0 - ADAM_B1 ** ADAM_STEP
    c2 = 1.0 - ADAM_B2 ** ADAM_STEP
    mn = ADAM_B1 * m + (1.0 - ADAM_B1) * g
    vn = ADAM_B2 * v + (1.0 - ADAM_B2) * (g * g)
    return -ADAM_LR * ((mn / c1) / (jnp.sqrt(vn / c2) + ADAM_EPS) + ADAM_WD * w), mn, vn


def adamw_layer(w, m, v, layer, gbuf, off, prev, name):
    L, R, W = w.shape
    tr = 256
    assert R % tr == 0 and off % tr == 0

    def body(w_ref, g_ref, m_ref, v_ref, *rest):
        go_ref, d_ref, mo_ref, vo_ref = rest[-4:]
        g = g_ref[...]
        go_ref[...] = g
        d_ref[...], mo_ref[...], vo_ref[...] = _adamw_math(w_ref[...], g, m_ref[...], v_ref[...])

    lay = pl.BlockSpec((None, tr, W), lambda r: (layer, r, 0))
    ins = [lay, pl.BlockSpec((tr, W), lambda r: (off // tr + r, 0)), lay, lay]
    args = [w, gbuf, m, v]
    kw = {}
    if prev is not None:
        ins += [ANY] * 4
        args += list(prev)
        kw["input_output_aliases"] = {4 + k: k for k in range(4)}
    return _pc(body, name, (R // tr,), ins, [lay] * 4, [_sds((L, R, W), F32)] * 4, sem=("parallel",), **kw)(*args)


def adamw_many(ws, gs, ms, vs):
    n = len(ws)

    def body(*refs):
        for k in range(n):
            d, mn, vn = _adamw_math(refs[k][...], refs[n + k][...], refs[2 * n + k][...], refs[3 * n + k][...])
            refs[4 * n + k][...] = d
            refs[5 * n + k][...] = mn
            refs[6 * n + k][...] = vn

    outs = pl.pallas_call(body, name="adamw_small", out_shape=[_sds(a.shape, F32) for a in ws] * 3)(*ws, *gs, *ms, *vs)
    return outs[:n], outs[n:2 * n], outs[2 * n:]


def _rope_tables(T):
    pos = jnp.arange(T, dtype=F32)
    inv_freq = ROPE_THETA ** (-jnp.arange(0, ROPE, 2, dtype=F32) / ROPE)
    ang = pos[:, None] * inv_freq[None, :]
    cos, sin = jnp.cos(ang), jnp.sin(ang)
    pad = HEAD - ROPE
    c = jnp.concatenate([cos, cos, jnp.ones((T, pad), F32)], axis=1)
    s = jnp.concatenate([-sin, sin, jnp.zeros((T, pad), F32)], axis=1)
    return jnp.stack([jnp.tile(c, (1, 128 // HEAD)), jnp.tile(s, (1, 128 // HEAD))])


def _local_step(x, p, target, W, small, lay, hook=None):
    if hook is None:
        hook = lambda stage, after, G, sg=None: None
    T, D = x.shape
    depth = small["mix_ln_g"].shape[0]
    alpha = float((2 * depth) ** 0.25)
    taps = small["taps"]
    row = lambda a, i: a[i:i + 1]
    cs = _rope_tables(T)

    h = conv_in_fwd(x, W["conv_w_in"], small["conv_b_in"])
    cv, s = dwconv_fwd(h, small["conv_w_dw"], small["conv_b_dw"], small["conv_ln_g"], small["conv_ln_b"], taps)
    pre_mix0, x1, x1b = mm_res_ln(s, W["conv_w_out"], x, row(small["mix_ln_g"], 0), row(small["mix_ln_b"], 0), alpha,
                                  small["conv_b_out"], "conv_out_fwd")
    hook("weights1", x1b, None)
    r0 = mlp_up_fwd(x1b, W["mlp_w_up0"], "mlp_up_fwd0")
    pre_mlp0, x2, x2b = mm_res_ln(r0, W["mlp_w_down0"], x1, row(small["mlp_ln_g"], 0), row(small["mlp_ln_b"], 0), alpha,
                                  None, "mlp_down_fwd0")
    x3, x3b, pp0, gl0 = ple_fwd(x2, x2b, p, 0, W["ple_w_proj0"], W["ple_w_gate0"], None, "ple_fwd0")

    hook("weights2", x3b, None)
    q, k, v = qkv_fwd(x3b, W["attn_w_q"], W["kv_w_k"], W["kv_w_v"], cs)
    o = attn_fwd(q, k, v, small["attn_sinks"])
    pre_mix1, x4, x4b = mm_res_ln(o, W["attn_w_o"], x3, row(small["mix_ln_g"], 1), row(small["mix_ln_b"], 1), alpha,
                                  None, "attn_out_fwd")
    r1 = mlp_up_fwd(x4b, W["mlp_w_up1"], "mlp_up_fwd1")
    pre_mlp1, x5, x5b = mm_res_ln(r1, W["mlp_w_down1"], x4, row(small["mlp_ln_g"], 1), row(small["mlp_ln_b"], 1), alpha,
                                  None, "mlp_down_fwd1")
    dx6, loss, pp1, gl1 = ple_fwd(x5, x5b, p, 1, W["ple_w_proj1"], W["ple_w_gate1"], target, "ple_fwd1")

    G, sg = {}, {}
    where = {n: (key, off) for key in lay for n, off, _ in lay[key]}
    rows_of = {key: sum(r for _, _, r in lay[key]) for key in lay}

    def wg(name, a, b, row_sharded):
        key, off = where[name]
        shape = (NS, rows_of[key], W[name][0].shape[2])
        G[key] = wgrad(a, b, row_sharded, "wg_" + name, (G.get(key), shape, off))

    dpp1, dgl1, dx5 = ple_bwd(dx6, pp1, gl1, W["ple_w_gate1"], "ple_bwd1")
    wg("ple_w_proj1", (1, p), dpp1, False)
    wg("ple_w_gate1", x5b, dgl1, True)
    dpre_mlp1, dpre_mlp1b, dm1, g_mlp_g1, g_mlp_b1 = mlp_bwd1(dx5, pre_mlp1, row(small["mlp_ln_g"], 1), r1,
                                                              W["mlp_w_down1"], "mlp_bwd1_1")
    wg("mlp_w_down1", r1, dpre_mlp1b, True)
    wg("mlp_w_up1", x4b, dm1, False)
    dpre_mix1, dpre_mix1b, do, g_mix_g1, g_mix_b1, _ = mlp_bwd2(dpre_mlp1, dm1, W["mlp_w_up1"], alpha, pre_mix1,
                                                                row(small["mix_ln_g"], 1), W["attn_w_o"], "mlp_bwd2_1")
    wg("attn_w_o", o, dpre_mix1b, True)
    dq, dk, dv, dsinks = attn_bwd(q, k, v, do, small["attn_sinks"])
    dqb, dkb, dvb, dx3 = qkv_bwd(dq, dk, dv, dpre_mix1,
                                 W["attn_w_q"], W["kv_w_k"], W["kv_w_v"], cs, alpha)
    wg("attn_w_q", x3b, dqb, True)
    wg("kv_w_k", x3b, dkb, True)
    wg("kv_w_v", x3b, dvb, True)
    hook("grads3", None, G)

    dpp0, dgl0, dx2 = ple_bwd(dx3, pp0, gl0, W["ple_w_gate0"], "ple_bwd0")
    wg("ple_w_proj0", (0, p), dpp0, False)
    wg("ple_w_gate0", x2b, dgl0, True)
    dpre_mlp0, dpre_mlp0b, dm0, g_mlp_g0, g_mlp_b0 = mlp_bwd1(dx2, pre_mlp0, row(small["mlp_ln_g"], 0), r0,
                                                              W["mlp_w_down0"], "mlp_bwd1_0")
    wg("mlp_w_down0", r0, dpre_mlp0b, True)
    wg("mlp_w_up0", x1b, dm0, False)
    hook("grads2", None, G)
    dpre_mix0, dpre_mix0b, dsw, g_mix_g0, g_mix_b0, g_b_out = mlp_bwd2(dpre_mlp0, dm0, W["mlp_w_up0"], alpha, pre_mix0,
                                                                      row(small["mix_ln_g"], 0), W["conv_w_out"],
                                                                      "mlp_bwd2_0")
    wg("conv_w_out", s, dpre_mix0b, True)
    hook("grads1", None, G)
    dcv, g_cln_g, g_cln_b, g_b_dw = conv_mid_bwd(dsw, cv, small["conv_ln_g"], small["conv_ln_b"])
    dh, g_w_dw, g_b_in = dwconv_bwd(dcv, h, small["conv_w_dw"], taps)
    wg("conv_w_in", x, dh, False)

    sg["conv_b_in"] = g_b_in
    sg["conv_w_dw"] = g_w_dw
    sg["conv_b_dw"], sg["conv_ln_g"], sg["conv_ln_b"], sg["conv_b_out"] = g_b_dw, g_cln_g, g_cln_b, g_b_out
    sg["mix_ln_g"] = [g_mix_g0, g_mix_g1]
    sg["mix_ln_b"] = [g_mix_b0, g_mix_b1]
    sg["mlp_ln_g"] = [g_mlp_g0, g_mlp_g1]
    sg["mlp_ln_b"] = [g_mlp_b0, g_mlp_b1]
    sg["attn_sinks"] = dsinks[:, 0][None, :]
    sg["loss"] = loss
    hook("grads0", None, G, sg)
    grad_x = conv_in_bwd(dh, dpre_mix0, W["conv_w_in"], alpha)
    return loss, grad_x, G, sg


BUFFERS = (("b0", ("conv_w_in",)), ("a0", ("conv_w_out",)),
           ("a1", ("mlp_w_up0", "mlp_w_down0", "ple_w_gate0")), ("c1", ("ple_w_proj0",)),
           ("a2", ("mlp_w_up1", "mlp_w_down1", "ple_w_gate1", "attn_w_q", "attn_w_o")),
           ("c2", ("kv_w_k", "kv_w_v", "ple_w_proj1")))
GROUPS = (("b0", "a0"), ("a1", "c1"), ("a2", "c2"))
REDUCED = (("b0",), ("a0",), ("a1", "c1"), ("a2", "c2"))
ROW_SHARDED = {"mlp_w_down0", "mlp_w_down1", "ple_w_gate0", "ple_w_gate1", "conv_w_out", "attn_w_q", "attn_w_o", "kv_w_k",
               "kv_w_v"}


def _split_layers(weights):
    out = {"conv_w_in": weights["conv_w_in"][0], "conv_w_out": weights["conv_w_out"][0],
           "attn_w_q": weights["attn_w_q"][0], "attn_w_o": weights["attn_w_o"][0],
           "kv_w_k": weights["kv_w_k"], "kv_w_v": weights["kv_w_v"]}
    for n in ("mlp_w_up", "mlp_w_down", "ple_w_proj", "ple_w_gate"):
        for i in range(weights[n].shape[0]):
            out[n + str(i)] = weights[n][i]
    return out


def _layout(shards):
    lay = {}
    for key, names in BUFFERS:
        off, rows = 0, []
        for n in names:
            rows.append((n, off, shards[n].shape[0]))
            off += shards[n].shape[0]
        lay[key] = rows
    return lay


def _place():
    return lax.axis_index("x"), lax.axis_index("y"), lax.axis_index("c")


def _flip(v, f):
    return (v + f) % 2 if f else v


CHIP_FLIPS = ((1, 0), (0, 1), (1, 1))


HBM = pl.BlockSpec(memory_space=pltpu.HBM)
SEM = pl.BlockSpec(memory_space=pltpu.SEMAPHORE)
EFFECT = pltpu.SideEffectType.DATAFLOW_SIDE_EFFECTING


def _half(ref, rows, c):
    return ref.at[pl.ds(pl.multiple_of(c * (rows // 2), 16), rows // 2), :]


def _gather_copies(refs, shapes, whole, send, recv):
    x, y, c = _place()
    me = 2 * x + y
    na = len(refs)
    cps = []
    for d, (fx, fy) in enumerate(CHIP_FLIPS):
        to = (_flip(x, fx), _flip(y, fy), c)
        for k in range(na):
            mine = refs[k].at[me] if k >= na - whole else _half(refs[k].at[me], shapes[k][1], c)
            cps.append(pltpu.make_async_remote_copy(mine, mine, send.at[d * na + k], recv.at[d * na + k], device_id=to,
                                                    device_id_type=MESH))
    return cps


def gather_start(bufs, whole, after, name):
    na = len(bufs)
    shapes = [b.shape for b in bufs]
    nsem = len(CHIP_FLIPS) * na

    def body(*refs):
        ins = refs[:na]
        send, recv = refs[-(na + 3)], refs[-(na + 2)]
        token = refs[-1]
        for cp in _gather_copies(ins, shapes, whole, send, recv):
            cp.start()
        token[...] = jnp.zeros_like(token)

    args = [pltpu.with_memory_space_constraint(b, pltpu.HBM) for b in bufs]
    ins = [HBM] * na
    if after is not None:
        args.append(after)
        ins.append(ANY)
    return pl.pallas_call(
        body, name=name, in_specs=ins,
        out_specs=[SEM, SEM] + [HBM] * na + [pl.BlockSpec(memory_space=pltpu.VMEM)],
        out_shape=[pltpu.SemaphoreType.DMA((nsem,)), pltpu.SemaphoreType.DMA((nsem,))]
        + [pltpu.HBM(b.shape, b.dtype) for b in bufs] + [_sds((8, 128), F32)],
        input_output_aliases={k: k + 2 for k in range(na)},
        compiler_params=pltpu.CompilerParams(has_side_effects=EFFECT))(*args)


def gather_wait(send, recv, bufs, whole, after, name):
    na = len(bufs)
    shapes = [b.shape for b in bufs]

    def body(*refs):
        ins = refs[:na]
        send_ref, recv_ref = refs[na], refs[na + 1]
        for cp in _gather_copies(ins, shapes, whole, send_ref, recv_ref):
            cp.wait_send()
            cp.wait_recv()

    return pl.pallas_call(
        body, name=name, in_specs=[HBM] * na + [SEM, SEM, ANY], out_specs=[HBM] * na,
        out_shape=[pltpu.HBM(b.shape, b.dtype) for b in bufs], input_output_aliases={k: k for k in range(na)},
        compiler_params=pltpu.CompilerParams(has_side_effects=EFFECT))(*bufs, send, recv, after)


def sibling_forward(bufs, name):
    nb = len(bufs)

    def body(*refs):
        outs = refs[nb:2 * nb]
        send, recv = refs[2 * nb:]
        x, y, c = _place()
        cps = []
        for d, (fx, fy) in enumerate(CHIP_FLIPS):
            frm = 2 * _flip(x, fx) + _flip(y, fy)
            for k in range(nb):
                theirs = _half(outs[k].at[frm], bufs[k].shape[1], c)
                cps.append(pltpu.make_async_remote_copy(theirs, theirs, send.at[d * nb + k], recv.at[d * nb + k],
                                                        device_id=(x, y, 1 - c), device_id_type=MESH))
        for cp in cps:
            cp.start()
        for cp in cps:
            cp.wait()

    nsem = len(CHIP_FLIPS) * nb
    return pl.pallas_call(
        body, name=name, in_specs=[ANY] * nb, out_specs=[ANY] * nb, out_shape=[_sds(b.shape, b.dtype) for b in bufs],
        input_output_aliases={k: k for k in range(nb)},
        scratch_shapes=[pltpu.SemaphoreType.DMA((nsem,)), pltpu.SemaphoreType.DMA((nsem,))])(*bufs)


def pack_rows(pieces, rows, width, name):
    def body(*refs):
        o_ref = refs[-1]
        o_ref[...] = jnp.zeros_like(o_ref)
        for ref, (a, off) in zip(refs[:-1], pieces):
            o_ref[off:off + a.shape[0], 0:a.shape[1]] = ref[...]

    return pl.pallas_call(body, name=name, out_shape=_sds((rows, width), F32))(*[a for a, _ in pieces])


PEER_FLIPS = tuple((fx, fy, fc) for fx in (0, 1) for fy in (0, 1) for fc in (0, 1) if fx or fy or fc)


def _reduce_copies(parts, zones, pack, send, recv):
    x, y, c = _place()
    nb = len(parts)
    na = nb + (1 if pack is not None else 0)
    cps = []
    for f, (fx, fy, fc) in enumerate(PEER_FLIPS):
        tx, ty, tc = _flip(x, fx), _flip(y, fy), _flip(c, fc)
        for k in range(nb):
            hrows = parts[k].shape[1] // 2
            piece = parts[k].at[2 * tx + ty, pl.ds(pl.multiple_of(tc * hrows, 16), hrows), :]
            cps.append(pltpu.make_async_remote_copy(piece, zones[k].at[f], send.at[f * na + k], recv.at[f * na + k],
                                                    device_id=(tx, ty, tc), device_id_type=MESH))
        if pack is not None:
            mine = pack.at[4 * x + 2 * y + c]
            cps.append(pltpu.make_async_remote_copy(mine, mine, send.at[f * na + nb], recv.at[f * na + nb],
                                                    device_id=(tx, ty, tc), device_id_type=MESH))
    return cps


def reduce_begin(parts, pack, name):
    nb = len(parts)
    zones = [lax.empty((len(PEER_FLIPS), g.shape[1] // 2, g.shape[2]), g.dtype) for g in parts]
    arrs = list(parts) + zones + ([pack] if pack is not None else [])
    na = len(arrs)
    nsem = len(PEER_FLIPS) * (nb + (1 if pack is not None else 0))

    def body(*refs):
        ins = refs[:na]
        send, recv = refs[na], refs[na + 1]
        for cp in _reduce_copies(ins[:nb], ins[nb:2 * nb], ins[2 * nb] if pack is not None else None, send, recv):
            cp.start()
        refs[-1][...] = jnp.zeros_like(refs[-1])

    return pl.pallas_call(
        body, name=name, in_specs=[HBM] * na,
        out_specs=[SEM, SEM] + [HBM] * na + [pl.BlockSpec(memory_space=pltpu.VMEM)],
        out_shape=[pltpu.SemaphoreType.DMA((nsem,)), pltpu.SemaphoreType.DMA((nsem,))]
        + [pltpu.HBM(a.shape, a.dtype) for a in arrs] + [_sds((8, 128), F32)],
        input_output_aliases={k: k + 2 for k in range(na)},
        compiler_params=pltpu.CompilerParams(has_side_effects=EFFECT))(
            *[pltpu.with_memory_space_constraint(a, pltpu.HBM) for a in arrs])


def reduce_end(send, recv, parts, zones, pack, after, name):
    nb = len(parts)
    arrs = list(parts) + list(zones) + ([pack] if pack is not None else [])
    na = len(arrs)

    def body(*refs):
        ins = refs[:na]
        for cp in _reduce_copies(ins[:nb], ins[nb:2 * nb], ins[2 * nb] if pack is not None else None, refs[na], refs[na + 1]):
            cp.wait_send()
            cp.wait_recv()

    return pl.pallas_call(
        body, name=name, in_specs=[HBM] * na + [SEM, SEM, ANY], out_specs=[HBM] * na,
        out_shape=[pltpu.HBM(a.shape, a.dtype) for a in arrs], input_output_aliases={k: k for k in range(na)},
        compiler_params=pltpu.CompilerParams(has_side_effects=EFFECT))(*arrs, send, recv, after)


def sibling_share(halves, name):
    nb = len(halves)

    def body(*refs):
        outs = refs[nb:2 * nb]
        send, recv = refs[2 * nb:]
        x, y, c = _place()
        cps = []
        for k in range(nb):
            hrows = halves[k].shape[0] // 2
            mine = outs[k].at[pl.ds(pl.multiple_of(c * hrows, 8), hrows), :]
            cps.append(pltpu.make_async_remote_copy(mine, mine, send.at[k], recv.at[k], device_id=(x, y, 1 - c),
                                                    device_id_type=MESH))
        for cp in cps:
            cp.start()
        for cp in cps:
            cp.wait()

    return pl.pallas_call(
        body, name=name, in_specs=[ANY] * nb, out_specs=[ANY] * nb,
        out_shape=[_sds(h.shape, h.dtype) for h in halves], input_output_aliases={k: k for k in range(nb)},
        scratch_shapes=[pltpu.SemaphoreType.DMA((nb,)), pltpu.SemaphoreType.DMA((nb,))])(*halves)


def _row_tile(rows):
    for cand in (512, 384, 256, 128, 64, 32, 16):
        if rows % cand == 0:
            return cand
    return rows


def piece_sum(g, z, idx, name):
    _, hrows, W = z.shape
    tr = _row_tile(hrows)
    nrb = hrows // tr

    def body(idx_ref, g_ref, z_ref, o_ref):
        acc = g_ref[...].astype(F32)
        for d in range(z.shape[0]):
            acc = acc + z_ref[d].astype(F32)
        o_ref[...] = acc

    gs = pltpu.PrefetchScalarGridSpec(
        num_scalar_prefetch=1, grid=(nrb,),
        in_specs=[pl.BlockSpec((None, tr, W), lambda i, sc: (sc[0], sc[1] * nrb + i, 0)),
                  pl.BlockSpec((z.shape[0], tr, W), lambda i, sc: (0, i, 0))],
        out_specs=pl.BlockSpec((tr, W), lambda i, sc: (sc[1] * nrb + i, 0)))
    return pl.pallas_call(body, name=name, grid_spec=gs, out_shape=_sds((2 * hrows, W), F32),
                          compiler_params=pltpu.CompilerParams(dimension_semantics=("parallel",),
                                                               vmem_limit_bytes=48 * 2 ** 20))(idx, g, z)


def small_sum(packs):
    n, R, W = packs.shape

    def body(p_ref, o_ref):
        acc = p_ref[0]
        for d in range(1, n):
            acc = acc + p_ref[d]
        o_ref[...] = acc

    return pl.pallas_call(body, name="small_sum", out_shape=_sds((R, W), F32))(packs)


WEIGHTS = ["conv_w_in", "conv_b_in", "conv_w_dw", "conv_b_dw", "conv_ln_g", "conv_ln_b", "conv_w_out", "conv_b_out", "kv_w_k",
           "kv_w_v", "attn_w_q", "attn_sinks", "attn_w_o", "mix_ln_g", "mix_ln_b", "mlp_w_up", "mlp_w_down", "mlp_ln_g",
           "mlp_ln_b", "ple_w_proj", "ple_w_gate"]
BIG = ["conv_w_in", "conv_w_out", "kv_w_k", "kv_w_v", "attn_w_q", "attn_w_o", "mlp_w_up", "mlp_w_down", "ple_w_proj",
       "ple_w_gate"]
SMALL = [n for n in WEIGHTS if n not in BIG]


def _step(x, p, target, w, m, v):
    D = x.shape[-1]
    ds = D // NS
    xq, yq, cq = _place()
    chip = 2 * xq + yq
    idx = jnp.stack([chip, cq]).astype(jnp.int32)

    shards = _split_layers(w)
    lay = _layout(shards)
    taps = w["conv_w_dw"].shape[1]
    small_loc = pack_rows([(w["conv_w_dw"][0], 0), (w["conv_b_dw"], HALO), (w["conv_ln_g"], HALO + 1), (w["conv_ln_b"], HALO + 2),
                           (w["conv_b_out"], HALO + 3), (w["conv_b_in"].reshape(2, ds), HALO + 4)], HALO + 8, ds, "pack_small")
    slot = lambda a: lax.dynamic_update_slice(lax.empty((NS,) + a.shape, a.dtype), a[None], (chip, 0, 0))
    started, token = [], None
    for gi, keys in enumerate(GROUPS):
        bufs = [slot(jnp.concatenate([shards[n].astype(BF16) for n, _, _ in lay[key]], axis=0)) for key in keys]
        if gi == 0:
            bufs.append(slot(small_loc))
        send, recv, *thru, token = gather_start(bufs, 1 if gi == 0 else 0, token, "gather_start%d" % gi)
        started.append((send, recv, thru))
    W = {}

    def arrive(gi, after):
        send, recv, thru = started[gi]
        whole = 1 if gi == 0 else 0
        got = gather_wait(send, recv, thru, whole, after, "gather_wait%d" % gi)
        nk = len(GROUPS[gi])
        for key, buf in zip(GROUPS[gi], sibling_forward(got[:nk], "sibling_forward%d" % gi)):
            for n, off, rows in lay[key]:
                W[n] = (buf, off, rows)
        return got[nk:]

    gs, = arrive(0, token)
    across = lambda rows: gs[:, rows, :].transpose(1, 0, 2).reshape(rows.stop - rows.start, D)
    small = {"taps": taps, "conv_w_dw": across(slice(0, HALO)), "conv_b_dw": across(slice(HALO, HALO + 1)),
             "conv_ln_g": across(slice(HALO + 1, HALO + 2)), "conv_ln_b": across(slice(HALO + 2, HALO + 3)),
             "conv_b_out": across(slice(HALO + 3, HALO + 4)), "conv_b_in": gs[:, HALO + 4:HALO + 6, :].reshape(1, 2 * D),
             "attn_sinks": w["attn_sinks"], "mix_ln_g": w["mix_ln_g"], "mix_ln_b": w["mix_ln_b"],
             "mlp_ln_g": w["mlp_ln_g"], "mlp_ln_b": w["mlp_ln_b"]}

    reducing = {}

    def reduce_start(gi, G, pack):
        nk = len(REDUCED[gi])
        send, recv, *thru, token = reduce_begin([G[key] for key in REDUCED[gi]], pack, "reduce_begin%d" % gi)
        reducing[gi] = (send, recv, thru[:nk], thru[nk:2 * nk], thru[2 * nk] if pack is not None else None)
        _FOLLOW.append(token)

    def small_pack(sg):
        pieces = [(sg["conv_b_in"].reshape(2, D), 0), (sg["conv_w_dw"], 2)]
        r0 = 2 + HALO
        for i, n in enumerate(("conv_b_dw", "conv_ln_g", "conv_ln_b", "conv_b_out")):
            pieces.append((sg[n], r0 + i))
        r0 += 4
        for i, n in enumerate(("mix_ln_g", "mix_ln_b", "mlp_ln_g", "mlp_ln_b")):
            pieces += [(sg[n][0], r0 + 2 * i), (sg[n][1], r0 + 2 * i + 1)]
        pieces += [(sg["attn_sinks"], r0 + 8), (sg["loss"][0:1], r0 + 9)]
        mine = pack_rows(pieces, r0 + 10, D, "pack_small_grads")
        return lax.dynamic_update_slice(lax.empty((8,) + mine.shape, F32), mine[None], (4 * xq + 2 * yq + cq, 0, 0))

    def hook(stage, after, G, sg=None):
        if stage == "weights1":
            arrive(1, after)
        elif stage == "weights2":
            arrive(2, after)
        elif stage == "grads0":
            reduce_start(0, G, small_pack(sg))
        elif stage.startswith("grads"):
            reduce_start(int(stage[5:]), G, None)

    loss, grad_x, G, sg = _local_step(x[0], p[:, 0], target[0], W, small, lay, hook)
    _FOLLOW.clear()
    nsink = w["attn_sinks"].shape[1]

    grads, delta, new_m, new_v = {}, {}, {}, {}
    found = {}

    def finish(groups, after, tag):
        keys, halves, tot = [], [], None
        for gi in groups:
            send, recv, parts, zones, pack = reducing[gi]
            done = reduce_end(send, recv, parts, zones, pack, after, "reduce_end%d" % gi)
            nk = len(REDUCED[gi])
            for key, g_, z_ in zip(REDUCED[gi], done[:nk], done[nk:2 * nk]):
                keys.append(key)
                halves.append(piece_sum(g_, z_, idx, "piece_sum_" + key))
            if pack is not None:
                tot = small_sum(done[2 * nk])
        for key, buf in zip(keys, sibling_share(halves, "sibling_share" + tag)):
            for n, off, _ in lay[key]:
                found[n] = (buf, off)
        return tot

    def big_adamw(names):
        for n in names:
            three = lambda a: a.reshape((-1,) + a.shape[-2:])
            w3, m3, v3 = three(w[n]), three(m[n]), three(v[n])
            outs = None
            for i in range(w3.shape[0]):
                buf, off = found[n + str(i)] if n + str(i) in found else found[n]
                outs = adamw_layer(w3, m3, v3, i, buf, off, outs, "adamw_%s%d" % (n, i))
            grads[n], delta[n], new_m[n], new_v[n] = [a.reshape(w[n].shape) for a in outs]

    last = [n for n, _, _ in lay[REDUCED[0][0]]]
    finish(reversed(range(1, len(REDUCED))), grad_x, "1")
    big_adamw([n for n in BIG if n not in last])
    tot = finish([0], new_v["mlp_w_down"], "0")
    big_adamw(last)
    cols = lambda rows: lax.dynamic_slice(rows, (0, chip * ds), (rows.shape[0], ds))
    grads["conv_b_in"] = lax.dynamic_slice(tot[0:2].reshape(1, 2 * D), (0, chip * 2 * ds), (1, 2 * ds))
    grads["conv_w_dw"] = cols(tot[2:2 + taps])[None]
    r0 = 2 + HALO
    for i, n in enumerate(("conv_b_dw", "conv_ln_g", "conv_ln_b", "conv_b_out")):
        grads[n] = cols(tot[r0 + i:r0 + i + 1])
    r0 += 4
    for i, n in enumerate(("mix_ln_g", "mix_ln_b", "mlp_ln_g", "mlp_ln_b")):
        grads[n] = tot[r0 + 2 * i:r0 + 2 * i + 2]
    grads["attn_sinks"] = tot[r0 + 8:r0 + 9, 0:nsink]

    ds_, ms_, vs_ = adamw_many([w[n] for n in SMALL], [grads[n] for n in SMALL], [m[n] for n in SMALL], [v[n] for n in SMALL])
    for n, d_, m_, v_ in zip(SMALL, ds_, ms_, vs_):
        delta[n], new_m[n], new_v[n] = d_, m_, v_

    total = tot[r0 + 9, 0]
    return (total, grad_x[None], *[grads[n] for n in WEIGHTS], *[delta[n] for n in WEIGHTS], *[new_m[n] for n in WEIGHTS],
            *[new_v[n] for n in WEIGHTS])


def kernel(x, p, conv_w_in, conv_b_in, conv_w_dw, conv_b_dw, conv_ln_g, conv_ln_b, conv_w_out, conv_b_out, kv_w_k, kv_w_v, attn_w_q, attn_sinks, attn_w_o, mix_ln_g, mix_ln_b, mlp_w_up, mlp_w_down, mlp_ln_g, mlp_ln_b, ple_w_proj, ple_w_gate, loss_target, m_conv_w_in, m_conv_b_in, m_conv_w_dw, m_conv_b_dw, m_conv_ln_g, m_conv_ln_b, m_conv_w_out, m_conv_b_out, m_kv_w_k, m_kv_w_v, m_attn_w_q, m_attn_sinks, m_attn_w_o, m_mix_ln_g, m_mix_ln_b, m_mlp_w_up, m_mlp_w_down, m_mlp_ln_g, m_mlp_ln_b, m_ple_w_proj, m_ple_w_gate, v_conv_w_in, v_conv_b_in, v_conv_w_dw, v_conv_b_dw, v_conv_ln_g, v_conv_ln_b, v_conv_w_out, v_conv_b_out, v_kv_w_k, v_kv_w_v, v_attn_w_q, v_attn_sinks, v_attn_w_o, v_mix_ln_g, v_mix_ln_b, v_mlp_w_up, v_mlp_w_down, v_mlp_ln_g, v_mlp_ln_b, v_ple_w_proj, v_ple_w_gate):
    w = dict(zip(WEIGHTS, (conv_w_in, conv_b_in, conv_w_dw, conv_b_dw, conv_ln_g, conv_ln_b, conv_w_out, conv_b_out, kv_w_k,
                           kv_w_v, attn_w_q, attn_sinks, attn_w_o, mix_ln_g, mix_ln_b, mlp_w_up, mlp_w_down, mlp_ln_g, mlp_ln_b,
                           ple_w_proj, ple_w_gate)))
    m = dict(zip(WEIGHTS, (m_conv_w_in, m_conv_b_in, m_conv_w_dw, m_conv_b_dw, m_conv_ln_g, m_conv_ln_b, m_conv_w_out,
                           m_conv_b_out, m_kv_w_k, m_kv_w_v, m_attn_w_q, m_attn_sinks, m_attn_w_o, m_mix_ln_g, m_mix_ln_b,
                           m_mlp_w_up, m_mlp_w_down, m_mlp_ln_g, m_mlp_ln_b, m_ple_w_proj, m_ple_w_gate)))
    v = dict(zip(WEIGHTS, (v_conv_w_in, v_conv_b_in, v_conv_w_dw, v_conv_b_dw, v_conv_ln_g, v_conv_ln_b, v_conv_w_out,
                           v_conv_b_out, v_kv_w_k, v_kv_w_v, v_attn_w_q, v_attn_sinks, v_attn_w_o, v_mix_ln_g, v_mix_ln_b,
                           v_mlp_w_up, v_mlp_w_down, v_mlp_ln_g, v_mlp_ln_b, v_ple_w_proj, v_ple_w_gate)))
    return _step(x, p, loss_target, w, m, v)
```

```python
import functools

import jax
import jax.numpy as jnp
from jax import lax
from jax.experimental import pallas as pl
from jax.experimental.pallas import tpu as pltpu

F32 = jnp.float32
BF16 = jnp.bfloat16
NS = 4
HEAD = 64
BLK = 128
ROPE = 16
ROPE_THETA = 500000.0
LN_EPS = 1e-5
NEG = -1e30
HALO = 32
ADAM_LR, ADAM_B1, ADAM_B2, ADAM_EPS, ADAM_WD, ADAM_STEP = 0.001, 0.9, 0.999, 1e-08, 0.01, 10
MESH = pl.DeviceIdType.MESH
ANY = pl.BlockSpec(memory_space=pl.ANY)
NT = (((1,), (1,)), ((), ()))
TN = (((0,), (0,)), ((), ()))


_FOLLOW = []


def _pc(body, name, grid, in_specs, out_specs, out_shape, scratch=(), sem=None, vmem=56, **kw):
    call = lambda fn, ins: pl.pallas_call(
        fn, name=name, grid=grid, in_specs=ins, out_specs=out_specs, out_shape=out_shape,
        scratch_shapes=list(scratch),
        compiler_params=pltpu.CompilerParams(dimension_semantics=sem, vmem_limit_bytes=vmem * 2 ** 20), **kw)
    if not _FOLLOW:
        return call(body, in_specs)
    extra = list(_FOLLOW)
    _FOLLOW.clear()
    n_in = len(in_specs)

    def ordered(*refs):
        return body(*refs[:n_in], *refs[n_in + len(extra):])

    run = call(ordered, list(in_specs) + [ANY] * len(extra))
    return lambda *args: run(*args, *extra)


def _rows(tm, n):
    return pl.BlockSpec((tm, n), lambda i: (i, 0))


def _const(shape):
    return pl.BlockSpec(shape, lambda *_: (0,) * len(shape))


def _wspec(w):
    buf, off, rows = w
    assert off % rows == 0
    return pl.BlockSpec((NS, rows, buf.shape[2]), lambda *_: (0, off // rows, 0))


def _rows_joined(w_ref):
    n, r, c = w_ref.shape
    return w_ref[...].reshape(n * r, c)


def _sds(shape, dtype):
    return jax.ShapeDtypeStruct(shape, dtype)


def _tile(t, rows=256):
    return min(rows, t)


def _sigmoid(x):
    return 0.5 * jnp.tanh(0.5 * x) + 0.5


def _ln_stats(w):
    mu = jnp.mean(w, axis=-1, keepdims=True)
    xc = w - mu
    var = jnp.mean(xc * xc, axis=-1, keepdims=True)
    rstd = lax.rsqrt(var + LN_EPS)
    return xc * rstd, rstd


def _ln_bwd(dy, w, g):
    xhat, rstd = _ln_stats(w)
    dxhat = dy * g
    m1 = jnp.mean(dxhat, axis=-1, keepdims=True)
    m2 = jnp.mean(dxhat * xhat, axis=-1, keepdims=True)
    dw = rstd * (dxhat - m1 - xhat * m2)
    return dw, jnp.sum(dy * xhat, axis=0, keepdims=True), jnp.sum(dy, axis=0, keepdims=True)


def _acc_rows(ref, val, first):
    @pl.when(first)
    def _():
        ref[...] = val

    @pl.when(jnp.logical_not(first))
    def _():
        ref[...] += val


def conv_in_fwd(xb, w_in, b_in):
    T, D = xb.shape
    nw = w_in[0].shape[2]
    tm = _tile(T, 512)

    def body(x_ref, w_ref, b_ref, h_ref):
        x = x_ref[...].astype(BF16)
        for j in range(NS):
            sl = slice(j * nw, (j + 1) * nw)
            h_ref[:, sl] = (jnp.dot(x, w_ref[j], preferred_element_type=F32) + b_ref[:, sl]).astype(BF16)

    return _pc(body, "conv_in_fwd", (T // tm,), [_rows(tm, D), _wspec(w_in), _const((1, NS * nw))],
               _rows(tm, NS * nw), _sds((T, NS * nw), BF16), sem=("parallel",))(xb, w_in[0], b_in)


CONV_ROWS = 16


def _phases(scr, sh):
    n = scr.shape[0] - 8
    for b in range(1, 8):
        sh[b - 1, 0:n, :] = scr[b:b + n, :]


def _spread(w_ref, wb, taps):
    for j in range(taps):
        wb[j] = jnp.broadcast_to(w_ref[j:j + 1, :], wb.shape[1:])


def _tap(scr, sh, o, n):
    b = o % 8
    return scr[o:o + n, :] if b == 0 else sh[b - 1, o - b:o - b + n, :]


def dwconv_fwd(h, w_dw, b_dw, ln_g, ln_b, taps):
    T = h.shape[0]
    C = h.shape[1] // 2
    tq = _tile(T)
    nh = tq // HALO
    off = HALO - (taps - 1)

    def body(a_ref, g_ref, ap_ref, gp_ref, w_ref, bdw_ref, lg_ref, lb_ref, cv_ref, s_ref, scr, sh, wb):
        i = pl.program_id(0)
        scr[HALO:HALO + tq, :] = a_ref[...].astype(F32) * _sigmoid(g_ref[...].astype(F32))
        up = ap_ref[...].astype(F32) * _sigmoid(gp_ref[...].astype(F32))
        scr[0:HALO, :] = jnp.where(i > 0, up, 0.0)
        _phases(scr, sh)
        _spread(w_ref, wb, taps)
        bias = jnp.broadcast_to(bdw_ref[...], (8, C))
        for r in range(tq // CONV_ROWS):
            accs = [bias] * (CONV_ROWS // 8)
            for j in range(taps):
                wj = wb[j]
                accs = [acc + wj * _tap(scr, sh, off + j + r * CONV_ROWS + 8 * k, 8) for k, acc in enumerate(accs)]
            for k, acc in enumerate(accs):
                cv_ref[r * CONV_ROWS + 8 * k:r * CONV_ROWS + 8 * k + 8, :] = acc
        xhat, _ = _ln_stats(cv_ref[...])
        ln = xhat * lg_ref[...] + lb_ref[...]
        s_ref[...] = (ln * _sigmoid(ln)).astype(BF16)

    prev = lambda col: pl.BlockSpec((HALO, C), lambda i: (jnp.maximum(i * nh - 1, 0), col))
    cur = lambda col: pl.BlockSpec((tq, C), lambda i: (i, col))
    return _pc(body, "dwconv_fwd", (T // tq,),
               [cur(0), cur(1), prev(0), prev(1), _const((HALO, C)), _const((1, C)), _const((1, C)), _const((1, C))],
               [_rows(tq, C), _rows(tq, C)], [_sds((T, C), F32), _sds((T, C), BF16)],
               scratch=[pltpu.VMEM((HALO + tq, C), F32), pltpu.VMEM((7, HALO + tq, C), F32), pltpu.VMEM((taps, 8, C), F32)],
               sem=("parallel",))(h, h, h, h, w_dw, b_dw, ln_g, ln_b)


def mm_res_ln(a, w, res, g, b, alpha, bias, name):
    T, K = a.shape
    ks = K // NS
    D = res.shape[1]
    tm = _tile(T, 512)

    def body(*refs):
        a_ref, w_ref, res_ref, g_ref, b_ref = refs[:5]
        n = 5
        if bias is not None:
            bias_ref = refs[5]
            n = 6
        pre_ref, xo_ref, xb_ref = refs[n:n + 3]
        acc = jnp.dot(a_ref[...], _rows_joined(w_ref), preferred_element_type=F32)
        if bias is not None:
            acc = acc + bias_ref[...]
        pre = alpha * res_ref[...] + acc
        xhat, _ = _ln_stats(pre)
        xo = xhat * g_ref[...] + b_ref[...]
        pre_ref[...] = pre
        xo_ref[...] = xo
        xb_ref[...] = xo.astype(BF16)

    ins = [_rows(tm, K), _wspec(w), _rows(tm, D), _const((1, D)), _const((1, D))]
    args = [a, w[0], res, g, b]
    if bias is not None:
        ins.append(_const((1, D)))
        args.append(bias)
    return _pc(body, name, (T // tm,), ins, [_rows(tm, D)] * 3, [_sds((T, D), F32), _sds((T, D), F32), _sds((T, D), BF16)],
               sem=("parallel",))(*args)


def mlp_up_fwd(xb, w_up, name):
    T, D = xb.shape
    fs = w_up[0].shape[2]
    tm = _tile(T, 512)

    def body(x_ref, w_ref, r_ref):
        x = x_ref[...]
        for j in range(NS):
            m = jnp.maximum(jnp.dot(x, w_ref[j], preferred_element_type=F32), 0.0)
            r_ref[:, j * fs:(j + 1) * fs] = (m * m).astype(BF16)

    return _pc(body, name, (T // tm,), [_rows(tm, D), _wspec(w_up)], _rows(tm, NS * fs), _sds((T, NS * fs), BF16),
               sem=("parallel",))(xb, w_up[0])


def ple_fwd(x, xb, p, layer, w_proj, w_gate, target, name):
    T, D = x.shape
    P = p.shape[2]
    ds = D // NS
    tm = _tile(T, 512)
    last = target is not None

    def body(*refs):
        x_ref, xb_ref, p_ref, wp_ref, wg_ref = refs[:5]
        n = 5
        if last:
            t_ref = refs[5]
            n = 6
        o_ref, o2_ref, pp_ref, gl_ref = refs[n:n + 4]
        gl = jnp.dot(xb_ref[...], _rows_joined(wg_ref), preferred_element_type=F32)
        gl_ref[...] = gl.astype(BF16)
        sg = _sigmoid(gl)
        pb = p_ref[...].astype(BF16)
        sq = jnp.zeros((1, 1), F32)
        for j in range(NS):
            sl = slice(j * ds, (j + 1) * ds)
            pp = jnp.dot(pb, wp_ref[j], preferred_element_type=F32)
            pp_ref[:, sl] = pp.astype(BF16)
            out = x_ref[:, sl] + pp * sg[:, sl]
            if last:
                err = out - t_ref[:, sl]
                o_ref[:, sl] = err * (1.0 / D)
                e2 = jnp.sum(err * err, axis=0, keepdims=True)
                sq = sq + jnp.sum(e2, axis=1, keepdims=True)
            else:
                o_ref[:, sl] = out
                o2_ref[:, sl] = out.astype(BF16)
        if last:
            _acc_rows(o2_ref, jnp.broadcast_to(sq * (0.5 / D), (8, 128)), pl.program_id(0) == 0)

    ins = [_rows(tm, D), _rows(tm, D), pl.BlockSpec((None, tm, P), lambda i: (layer, i, 0)), _wspec(w_proj), _wspec(w_gate)]
    args = [x, xb, p, w_proj[0], w_gate[0]]
    if last:
        ins.append(_rows(tm, D))
        args.append(target)
        outs = [_rows(tm, D), _const((8, 128)), _rows(tm, D), _rows(tm, D)]
        shapes = [_sds((T, D), F32), _sds((8, 128), F32), _sds((T, D), BF16), _sds((T, D), BF16)]
    else:
        outs = [_rows(tm, D)] * 4
        shapes = [_sds((T, D), F32), _sds((T, D), BF16), _sds((T, D), BF16), _sds((T, D), BF16)]
    return _pc(body, name, (T // tm,), ins, outs, shapes, sem=("arbitrary",) if last else ("parallel",))(*args)


def _rope(x, cs_ref, sign):
    c = cs_ref[0]
    s = cs_ref[1] * sign
    lane = lax.broadcasted_iota(jnp.int32, c.shape, 1)
    first = (lane % HEAD) < (ROPE // 2)
    outs = []
    for gq in range(x.shape[1] // 128):
        xg = x[:, gq * 128:(gq + 1) * 128]
        sw = jnp.where(first, pltpu.roll(xg, 128 - ROPE // 2, 1), pltpu.roll(xg, ROPE // 2, 1))
        outs.append(xg * c + sw * s)
    return outs


def qkv_fwd(xb, w_q, w_k, w_v, cs):
    T, D = xb.shape
    ds = D // NS
    HD, KVD = w_q[0].shape[2], w_k[0].shape[2]
    tm = _tile(T, 512)
    scale = 1.0 / (HEAD ** 0.5)

    def body(x_ref, wq_ref, wk_ref, wv_ref, cs_ref, q_ref, k_ref, v_ref):
        def proj(w_ref):
            return jnp.dot(x_ref[...], _rows_joined(w_ref), preferred_element_type=F32)

        for gq, val in enumerate(_rope(proj(wq_ref), cs_ref, 1.0)):
            q_ref[:, gq * 128:(gq + 1) * 128] = (val * scale).astype(BF16)
        for gq, val in enumerate(_rope(proj(wk_ref), cs_ref, 1.0)):
            k_ref[:, gq * 128:(gq + 1) * 128] = val.astype(BF16)
        v_ref[...] = proj(wv_ref).astype(BF16)

    cs_spec = pl.BlockSpec((2, tm, 128), lambda i: (0, i, 0))
    return _pc(body, "qkv_fwd", (T // tm,), [_rows(tm, D), _wspec(w_q), _wspec(w_k), _wspec(w_v), cs_spec],
               [_rows(tm, HD), _rows(tm, KVD), _rows(tm, KVD)],
               [_sds((T, HD), BF16), _sds((T, KVD), BF16), _sds((T, KVD), BF16)], sem=("parallel",))(
                   xb, w_q[0], w_k[0], w_v[0], cs)


def _band_mask(n):
    row = lax.broadcasted_iota(jnp.int32, (BLK, 2 * BLK), 0)
    col = lax.broadcasted_iota(jnp.int32, (BLK, 2 * BLK), 1)
    return (col > row) & (col <= row + BLK) & ((col >= BLK) | (n > 0))


def _head(h):
    return slice(h * HEAD, (h + 1) * HEAD)


def _softmax_sink(s, sink):
    m = jnp.maximum(jnp.max(s, axis=-1, keepdims=True), sink)
    e = jnp.exp(s - m)
    es = jnp.exp(sink - m)
    den = jnp.sum(e, axis=-1, keepdims=True) + es
    inv = 1.0 / den
    return e * inv, es * inv


def attn_fwd(q, k, v, sinks):
    T, HD = q.shape
    KVD = k.shape[1]
    NKV = KVD // HEAD
    G = HD // KVD

    def body(s_ref, q_ref, kc_ref, kp_ref, vc_ref, vp_ref, o_ref):
        valid = _band_mask(pl.program_id(0))
        for kh in range(NKV):
            k2 = jnp.concatenate([kp_ref[:, _head(kh)], kc_ref[:, _head(kh)]], axis=0)
            v2 = jnp.concatenate([vp_ref[:, _head(kh)], vc_ref[:, _head(kh)]], axis=0)
            hs = [kh * G + gq for gq in range(G)]
            sc = [lax.dot_general(q_ref[:, _head(hh)], k2, NT, preferred_element_type=F32) for hh in hs]
            pb = [_softmax_sink(jnp.where(valid, s, NEG), s_ref[0, hh])[0].astype(BF16) for s, hh in zip(sc, hs)]
            for p, hh in zip(pb, hs):
                o_ref[:, _head(hh)] = jnp.dot(p, v2, preferred_element_type=F32).astype(BF16)

    cur = lambda n_: pl.BlockSpec((BLK, n_), lambda n: (n, 0))
    prev = lambda n_: pl.BlockSpec((BLK, n_), lambda n: (jnp.maximum(n - 1, 0), 0))
    return _pc(body, "attn_fwd", (T // BLK,),
               [pl.BlockSpec(memory_space=pltpu.SMEM), cur(HD), cur(KVD), prev(KVD), cur(KVD), prev(KVD)],
               cur(HD), _sds((T, HD), BF16), sem=("parallel",))(sinks, q, k, k, v, v)


def ple_bwd(dxo, pp, gl, w_gate, name):
    T, D = dxo.shape
    ds = D // NS
    tm = _tile(T, 512)

    def body(d_ref, pp_ref, gl_ref, wg_ref, dpp_ref, dgl_ref, dx_ref):
        d = d_ref[...]
        sg = _sigmoid(gl_ref[...].astype(F32))
        dpp_ref[...] = (d * sg).astype(BF16)
        dgl = (d * pp_ref[...].astype(F32) * sg * (1.0 - sg)).astype(BF16)
        dgl_ref[...] = dgl
        dx_ref[...] = d + lax.dot_general(dgl, _rows_joined(wg_ref), NT, preferred_element_type=F32)

    return _pc(body, name, (T // tm,), [_rows(tm, D)] * 3 + [_wspec(w_gate)], [_rows(tm, D)] * 3,
               [_sds((T, D), BF16), _sds((T, D), BF16), _sds((T, D), F32)], sem=("parallel",))(dxo, pp, gl, w_gate[0])


def mlp_bwd1(dy, pre, g, r, w_down, name):
    T, D = dy.shape
    fs = w_down[2]
    tm = _tile(T, 512)

    def body(dy_ref, pre_ref, g_ref, r_ref, w_ref, dw_ref, dwb_ref, dm_ref, dg_ref, db_ref):
        dw, dg, db = _ln_bwd(dy_ref[...], pre_ref[...], g_ref[...])
        first = pl.program_id(0) == 0
        _acc_rows(dg_ref, dg, first)
        _acc_rows(db_ref, db, first)
        dwb = dw.astype(BF16)
        dw_ref[...] = dw
        dwb_ref[...] = dwb
        for j in range(NS):
            sl = slice(j * fs, (j + 1) * fs)
            dr = lax.dot_general(dwb, w_ref[j], NT, preferred_element_type=F32)
            dm_ref[:, sl] = (dr * (2.0 * jnp.sqrt(r_ref[:, sl].astype(F32)))).astype(BF16)

    return _pc(body, name, (T // tm,), [_rows(tm, D), _rows(tm, D), _const((1, D)), _rows(tm, NS * fs), _wspec(w_down)],
               [_rows(tm, D), _rows(tm, D), _rows(tm, NS * fs), _const((1, D)), _const((1, D))],
               [_sds((T, D), F32), _sds((T, D), BF16), _sds((T, NS * fs), BF16), _sds((1, D), F32), _sds((1, D), F32)],
               sem=("arbitrary",))(dy, pre, g, r, w_down[0])


def mlp_bwd2(dpre, dm, w_up, alpha, pre_mix, g_mix, w_mix, name):
    T, D = dpre.shape
    fs = w_up[0].shape[2]
    ms = w_mix[2]
    tm = _tile(T, 512)

    def body(dp_ref, dm_ref, wu_ref, pre_ref, g_ref, wm_ref, dw_ref, dwb_ref, do_ref, dg_ref, db_ref, dc_ref):
        dy = alpha * dp_ref[...]
        for j in range(NS):
            dy = dy + lax.dot_general(dm_ref[:, j * fs:(j + 1) * fs], wu_ref[j], NT, preferred_element_type=F32)
        dw, dg, db = _ln_bwd(dy, pre_ref[...], g_ref[...])
        first = pl.program_id(0) == 0
        _acc_rows(dg_ref, dg, first)
        _acc_rows(db_ref, db, first)
        _acc_rows(dc_ref, jnp.sum(dw, axis=0, keepdims=True), first)
        dwb = dw.astype(BF16)
        dw_ref[...] = dw
        dwb_ref[...] = dwb
        do_ref[...] = lax.dot_general(dwb, _rows_joined(wm_ref), NT, preferred_element_type=F32).astype(BF16)

    return _pc(body, name, (T // tm,),
               [_rows(tm, D), _rows(tm, NS * fs), _wspec(w_up), _rows(tm, D), _const((1, D)), _wspec(w_mix)],
               [_rows(tm, D), _rows(tm, D), _rows(tm, NS * ms), _const((1, D)), _const((1, D)), _const((1, D))],
               [_sds((T, D), F32), _sds((T, D), BF16), _sds((T, NS * ms), BF16)] + [_sds((1, D), F32)] * 3,
               sem=("arbitrary",))(dpre, dm, w_up[0], pre_mix, g_mix, w_mix[0])


def attn_bwd(q, k, v, do, sinks):
    T, HD = q.shape
    KVD = k.shape[1]
    NH, NKV = HD // HEAD, KVD // HEAD
    G = NH // NKV
    nb = T // BLK

    def body(s_ref, q_ref, do_ref, kc_ref, kp_ref, vc_ref, vp_ref, dq_ref, dk_ref, dv_ref, ds_ref, ck, cv):
        n = pl.program_id(0)

        @pl.when(n == 0)
        def _():
            ck[...] = jnp.zeros_like(ck)
            cv[...] = jnp.zeros_like(cv)
            ds_ref[...] = jnp.zeros_like(ds_ref)

        @pl.when(n < nb)
        def _():
            valid = _band_mask(n)
            for kh in range(NKV):
                kv = _head(kh)
                k2 = jnp.concatenate([kp_ref[:, kv], kc_ref[:, kv]], axis=0)
                v2 = jnp.concatenate([vp_ref[:, kv], vc_ref[:, kv]], axis=0)
                hs = [kh * G + gq for gq in range(G)]
                qs = [q_ref[:, _head(hh)] for hh in hs]
                dos = [do_ref[:, _head(hh)] for hh in hs]
                sc = [lax.dot_general(qh, k2, NT, preferred_element_type=F32) for qh in qs]
                dp = [lax.dot_general(doh, v2, NT, preferred_element_type=F32) for doh in dos]
                pr = [_softmax_sink(jnp.where(valid, s, NEG), s_ref[0, hh]) for s, hh in zip(sc, hs)]
                delta = [jnp.sum(p * d, axis=-1, keepdims=True) for (p, _), d in zip(pr, dp)]
                dsb = [(p * (d - dl)).astype(BF16) for (p, _), d, dl in zip(pr, dp, delta)]
                pb = [p.astype(BF16) for p, _ in pr]
                for (_, ps), dl, hh in zip(pr, delta, hs):
                    ds_ref[hh:hh + 1, :] += jnp.broadcast_to(-jnp.sum(ps * dl, axis=0, keepdims=True), (1, 128))
                for d, hh in zip(dsb, hs):
                    dq_ref[:, _head(hh)] = jnp.dot(d, k2, preferred_element_type=F32)
                dk2 = lax.dot_general(jnp.concatenate(dsb, axis=0), jnp.concatenate(qs, axis=0), TN,
                                      preferred_element_type=F32)
                dv2 = lax.dot_general(jnp.concatenate(pb, axis=0), jnp.concatenate(dos, axis=0), TN,
                                      preferred_element_type=F32)
                dk_ref[:, kv] = ck[:, kv] + dk2[0:BLK]
                dv_ref[:, kv] = cv[:, kv] + dv2[0:BLK]
                ck[:, kv] = dk2[BLK:2 * BLK]
                cv[:, kv] = dv2[BLK:2 * BLK]

        @pl.when(n == nb)
        def _():
            dk_ref[...] = ck[...]
            dv_ref[...] = cv[...]

    qcur = pl.BlockSpec((BLK, HD), lambda n: (jnp.minimum(n, nb - 1), 0))
    kcur = pl.BlockSpec((BLK, KVD), lambda n: (jnp.minimum(n, nb - 1), 0))
    kprev = pl.BlockSpec((BLK, KVD), lambda n: (jnp.maximum(n - 1, 0), 0))
    return _pc(body, "attn_bwd", (nb + 1,),
               [pl.BlockSpec(memory_space=pltpu.SMEM), qcur, qcur, kcur, kprev, kcur, kprev],
               [qcur, kprev, kprev, _const((NH, 128))],
               [_sds((T, HD), F32), _sds((T, KVD), F32), _sds((T, KVD), F32), _sds((NH, 128), F32)],
               scratch=[pltpu.VMEM((BLK, KVD), F32), pltpu.VMEM((BLK, KVD), F32)],
               sem=("arbitrary",))(sinks, q, do, k, k, v, v)


def qkv_bwd(dq, dk, dv, dpre_mix, w_q, w_k, w_v, cs, alpha):
    T, HD = dq.shape
    KVD = dk.shape[1]
    D = dpre_mix.shape[1]
    ds = D // NS
    tm = _tile(T, 512)
    scale = 1.0 / (HEAD ** 0.5)

    def body(dq_ref, dk_ref, dv_ref, dp_ref, wq_ref, wk_ref, wv_ref, cs_ref, dqb_ref, dkb_ref, dvb_ref, dx_ref):
        for gq, val in enumerate(_rope(dq_ref[...], cs_ref, -1.0)):
            dqb_ref[:, gq * 128:(gq + 1) * 128] = (val * scale).astype(BF16)
        for gq, val in enumerate(_rope(dk_ref[...], cs_ref, -1.0)):
            dkb_ref[:, gq * 128:(gq + 1) * 128] = val.astype(BF16)
        dvb_ref[...] = dv_ref[...].astype(BF16)
        dqb, dkb, dvb = dqb_ref[...], dkb_ref[...], dvb_ref[...]
        dx_ref[...] = (alpha * dp_ref[...]
                       + lax.dot_general(dqb, _rows_joined(wq_ref), NT, preferred_element_type=F32)
                       + lax.dot_general(dkb, _rows_joined(wk_ref), NT, preferred_element_type=F32)
                       + lax.dot_general(dvb, _rows_joined(wv_ref), NT, preferred_element_type=F32))

    cs_spec = pl.BlockSpec((2, tm, 128), lambda i: (0, i, 0))
    return _pc(body, "qkv_bwd", (T // tm,),
               [_rows(tm, HD), _rows(tm, KVD), _rows(tm, KVD), _rows(tm, D), _wspec(w_q), _wspec(w_k), _wspec(w_v), cs_spec],
               [_rows(tm, HD), _rows(tm, KVD), _rows(tm, KVD), _rows(tm, D)],
               [_sds((T, HD), BF16), _sds((T, KVD), BF16), _sds((T, KVD), BF16), _sds((T, D), F32)],
               sem=("parallel",))(dq, dk, dv, dpre_mix, w_q[0], w_k[0], w_v[0], cs)


def conv_mid_bwd(ds, cv, ln_g, ln_b):
    T, C = cv.shape
    tm = _tile(T, 512)

    def body(ds_ref, cv_ref, g_ref, b_ref, dcv_ref, dg_ref, db_ref, dc_ref):
        xhat, _ = _ln_stats(cv_ref[...])
        ln = xhat * g_ref[...] + b_ref[...]
        sg = _sigmoid(ln)
        dl = ds_ref[...].astype(F32) * (sg * (1.0 + ln * (1.0 - sg)))
        dcv, dg, db = _ln_bwd(dl, cv_ref[...], g_ref[...])
        first = pl.program_id(0) == 0
        _acc_rows(dg_ref, dg, first)
        _acc_rows(db_ref, db, first)
        _acc_rows(dc_ref, jnp.sum(dcv, axis=0, keepdims=True), first)
        dcv_ref[...] = dcv

    return _pc(body, "conv_mid_bwd", (T // tm,), [_rows(tm, C), _rows(tm, C), _const((1, C)), _const((1, C))],
               [_rows(tm, C), _const((1, C)), _const((1, C)), _const((1, C))],
               [_sds((T, C), F32)] + [_sds((1, C), F32)] * 3, sem=("arbitrary",))(ds, cv, ln_g, ln_b)


def dwconv_bwd(dcv, h, w_dw, taps):
    T, C = dcv.shape
    tq = _tile(T)
    nh = tq // HALO
    nblk = T // tq
    off = HALO - (taps - 1)

    def body(d_ref, dn_ref, a_ref, g_ref, ap_ref, gp_ref, w_ref, dh_ref, dw_ref, dbi_ref, su, sus, sd, sds, wb):
        i = pl.program_id(0)
        su[HALO:HALO + tq, :] = a_ref[...].astype(F32) * _sigmoid(g_ref[...].astype(F32))
        up = ap_ref[...].astype(F32) * _sigmoid(gp_ref[...].astype(F32))
        su[0:HALO, :] = jnp.where(i > 0, up, 0.0)
        sd[0:tq, :] = d_ref[...]
        sd[tq:tq + HALO, :] = jnp.where(i < nblk - 1, dn_ref[...], 0.0)
        _phases(su, sus)
        _phases(sd, sds)

        @pl.when(i == 0)
        def _():
            dw_ref[...] = jnp.zeros_like(dw_ref)

        for j in range(taps):
            dw_ref[j:j + 1, :] += jnp.sum(d_ref[...] * _tap(su, sus, off + j, tq), axis=0, keepdims=True)
        sa = jnp.zeros((1, C), F32)
        sb = jnp.zeros((1, C), F32)
        _spread(w_ref, wb, taps)
        for r in range(tq // CONV_ROWS):
            rows = slice(r * CONV_ROWS, (r + 1) * CONV_ROWS)
            dus = [wb[0] * _tap(sd, sds, taps - 1 + r * CONV_ROWS + 8 * k, 8) for k in range(CONV_ROWS // 8)]
            for j in range(1, taps):
                wj = wb[j]
                dus = [acc + wj * _tap(sd, sds, taps - 1 - j + r * CONV_ROWS + 8 * k, 8) for k, acc in enumerate(dus)]
            du = jnp.concatenate(dus, axis=0)
            a = a_ref[rows, :].astype(F32)
            sg = _sigmoid(g_ref[rows, :].astype(F32))
            da = du * sg
            dgt = du * a * sg * (1.0 - sg)
            dh_ref[rows, 0:C] = da.astype(BF16)
            dh_ref[rows, C:2 * C] = dgt.astype(BF16)
            sa = sa + jnp.sum(da, axis=0, keepdims=True)
            sb = sb + jnp.sum(dgt, axis=0, keepdims=True)
        first = i == 0
        _acc_rows(dbi_ref.at[:, 0:C], sa, first)
        _acc_rows(dbi_ref.at[:, C:2 * C], sb, first)

    prev = lambda col: pl.BlockSpec((HALO, C), lambda i: (jnp.maximum(i * nh - 1, 0), col))
    nxt = pl.BlockSpec((HALO, C), lambda i: (jnp.minimum((i + 1) * nh, T // HALO - 1), 0))
    cur = lambda col: pl.BlockSpec((tq, C), lambda i: (i, col))
    return _pc(body, "dwconv_bwd", (nblk,),
               [cur(0), nxt, cur(0), cur(1), prev(0), prev(1), _const((HALO, C))],
               [_rows(tq, 2 * C), _const((HALO, C)), _const((1, 2 * C))],
               [_sds((T, 2 * C), BF16), _sds((HALO, C), F32), _sds((1, 2 * C), F32)],
               scratch=[pltpu.VMEM((HALO + tq, C), F32), pltpu.VMEM((7, HALO + tq, C), F32),
                        pltpu.VMEM((HALO + tq, C), F32), pltpu.VMEM((7, HALO + tq, C), F32), pltpu.VMEM((taps, 8, C), F32)],
               sem=("arbitrary",))(dcv, dcv, h, h, h, h, w_dw)


def conv_in_bwd(dh, dpre_mix, w_in, alpha):
    T, D = dpre_mix.shape
    nw = w_in[0].shape[2]
    tm = _tile(T, 512)

    def body(dh_ref, dp_ref, w_ref, dx_ref):
        acc = alpha * dp_ref[...]
        for j in range(NS):
            acc = acc + lax.dot_general(dh_ref[:, j * nw:(j + 1) * nw], w_ref[j], NT, preferred_element_type=F32)
        dx_ref[...] = acc

    return _pc(body, "conv_in_bwd", (T // tm,), [_rows(tm, NS * nw), _rows(tm, D), _wspec(w_in)], _rows(tm, D),
               _sds((T, D), F32), sem=("parallel",))(dh, dpre_mix, w_in[0])


def wgrad(a, b, row_sharded, name, into):
    prev, out_shape, off = into
    layer = None
    if isinstance(a, tuple):
        layer, a = a
    T, Ka = a.shape[-2:]
    Nb = b.shape[1]
    tt = min(2048, T)
    nt = T // tt
    ka, tn = min(Ka, 1024), min(Nb, 1024)
    if row_sharded:
        sr = Ka // NS
        spb = max(ka // sr, 1)
        rb = ka // spb
        assert out_shape[2] == Nb and off % rb == 0
        out_spec = pl.BlockSpec((spb, rb, tn), lambda i, j, t: (i, off // rb, j))
    else:
        sc = Nb // NS
        spb = max(tn // sc, 1)
        rb = ka
        assert out_shape[2] == sc and off % ka == 0
        out_spec = pl.BlockSpec((spb, ka, tn // spb), lambda i, j, t: (j, off // ka + i, 0))

    def body(a_ref, b_ref, *rest):
        o_ref, acc = rest[-2:]
        t = pl.program_id(2)
        av = a_ref[...]
        if av.dtype != BF16:
            av = av.astype(BF16)
        d = lax.dot_general(av, b_ref[...], TN, preferred_element_type=F32)

        @pl.when(t == 0)
        def _():
            acc[...] = d

        @pl.when(t > 0)
        def _():
            acc[...] += d

        @pl.when(t == nt - 1)
        def _():
            for s in range(spb):
                if row_sharded:
                    o_ref[s] = acc[s * rb:(s + 1) * rb, :].astype(BF16)
                else:
                    o_ref[s] = acc[:, s * (tn // spb):(s + 1) * (tn // spb)].astype(BF16)

    a_spec = (pl.BlockSpec((tt, ka), lambda i, j, t: (t, i)) if layer is None
              else pl.BlockSpec((None, tt, ka), lambda i, j, t: (layer, t, i)))
    ins = [a_spec, pl.BlockSpec((tt, tn), lambda i, j, t: (t, j))]
    args = [a, b]
    kw = {}
    if prev is not None:
        ins.append(ANY)
        args.append(prev)
        kw["input_output_aliases"] = {2: 0}
    return _pc(body, name, (Ka // ka, Nb // tn, nt), ins, out_spec, _sds(out_shape, BF16),
               scratch=[pltpu.VMEM((ka, tn), F32)], sem=("parallel", "parallel", "arbitrary"), **kw)(*args)


def _adamw_math(w, g, m, v):
    c1 = 1.0 - ADAM_B1 ** ADAM_STEP
    c2 = 1.0 - ADAM_B2 ** ADAM_STEP
    mn = ADAM_B1 * m + (1.0 - ADAM_B1) * g
    vn = ADAM_B2 * v + (1.0 - ADAM_B2) * (g * g)
    return -ADAM_LR * ((mn / c1) / (jnp.sqrt(vn / c2) + ADAM_EPS) + ADAM_WD * w), mn, vn


def adamw_layer(w, m, v, layer, gbuf, off, prev, name):
    L, R, W = w.shape
    tr = 256
    assert R % tr == 0 and off % tr == 0

    def body(w_ref, g_ref, m_ref, v_ref, *rest):
        go_ref, d_ref, mo_ref, vo_ref = rest[-4:]
        g = g_ref[...]
        go_ref[...] = g
        d_ref[...], mo_ref[...], vo_ref[...] = _adamw_math(w_ref[...], g, m_ref[...], v_ref[...])

    lay = pl.BlockSpec((None, tr, W), lambda r: (layer, r, 0))
    ins = [lay, pl.BlockSpec((tr, W), lambda r: (off // tr + r, 0)), lay, lay]
    args = [w, gbuf, m, v]
    kw = {}
    if prev is not None:
        ins += [ANY] * 4
        args += list(prev)
        kw["input_output_aliases"] = {4 + k: k for k in range(4)}
    return _pc(body, name, (R // tr,), ins, [lay] * 4, [_sds((L, R, W), F32)] * 4, sem=("parallel",), **kw)(*args)


def adamw_many(ws, gs, ms, vs):
    n = len(ws)

    def body(*refs):
        for k in range(n):
            d, mn, vn = _adamw_math(refs[k][...], refs[n + k][...], refs[2 * n + k][...], refs[3 * n + k][...])
            refs[4 * n + k][...] = d
            refs[5 * n + k][...] = mn
            refs[6 * n + k][...] = vn

    outs = pl.pallas_call(body, name="adamw_small", out_shape=[_sds(a.shape, F32) for a in ws] * 3)(*ws, *gs, *ms, *vs)
    return outs[:n], outs[n:2 * n], outs[2 * n:]


def _rope_tables(T):
    pos = jnp.arange(T, dtype=F32)
    inv_freq = ROPE_THETA ** (-jnp.arange(0, ROPE, 2, dtype=F32) / ROPE)
    ang = pos[:, None] * inv_freq[None, :]
    cos, sin = jnp.cos(ang), jnp.sin(ang)
    pad = HEAD - ROPE
    c = jnp.concatenate([cos, cos, jnp.ones((T, pad), F32)], axis=1)
    s = jnp.concatenate([-sin, sin, jnp.zeros((T, pad), F32)], axis=1)
    return jnp.stack([jnp.tile(c, (1, 128 // HEAD)), jnp.tile(s, (1, 128 // HEAD))])


def _local_step(x, p, target, W, small, lay, hook=None):
    if hook is None:
        hook = lambda stage, after, G, sg=None: None
    T, D = x.shape
    depth = small["mix_ln_g"].shape[0]
    alpha = float((2 * depth) ** 0.25)
    taps = small["taps"]
    row = lambda a, i: a[i:i + 1]
    cs = _rope_tables(T)

    h = conv_in_fwd(x, W["conv_w_in"], small["conv_b_in"])
    cv, s = dwconv_fwd(h, small["conv_w_dw"], small["conv_b_dw"], small["conv_ln_g"], small["conv_ln_b"], taps)
    pre_mix0, x1, x1b = mm_res_ln(s, W["conv_w_out"], x, row(small["mix_ln_g"], 0), row(small["mix_ln_b"], 0), alpha,
                                  small["conv_b_out"], "conv_out_fwd")
    hook("weights1", x1b, None)
    r0 = mlp_up_fwd(x1b, W["mlp_w_up0"], "mlp_up_fwd0")
    pre_mlp0, x2, x2b = mm_res_ln(r0, W["mlp_w_down0"], x1, row(small["mlp_ln_g"], 0), row(small["mlp_ln_b"], 0), alpha,
                                  None, "mlp_down_fwd0")
    x3, x3b, pp0, gl0 = ple_fwd(x2, x2b, p, 0, W["ple_w_proj0"], W["ple_w_gate0"], None, "ple_fwd0")

    hook("weights2", x3b, None)
    q, k, v = qkv_fwd(x3b, W["attn_w_q"], W["kv_w_k"], W["kv_w_v"], cs)
    o = attn_fwd(q, k, v, small["attn_sinks"])
    pre_mix1, x4, x4b = mm_res_ln(o, W["attn_w_o"], x3, row(small["mix_ln_g"], 1), row(small["mix_ln_b"], 1), alpha,
                                  None, "attn_out_fwd")
    r1 = mlp_up_fwd(x4b, W["mlp_w_up1"], "mlp_up_fwd1")
    pre_mlp1, x5, x5b = mm_res_ln(r1, W["mlp_w_down1"], x4, row(small["mlp_ln_g"], 1), row(small["mlp_ln_b"], 1), alpha,
                                  None, "mlp_down_fwd1")
    dx6, loss, pp1, gl1 = ple_fwd(x5, x5b, p, 1, W["ple_w_proj1"], W["ple_w_gate1"], target, "ple_fwd1")

    G, sg = {}, {}
    where = {n: (key, off) for key in lay for n, off, _ in lay[key]}
    rows_of = {key: sum(r for _, _, r in lay[key]) for key in lay}

    def wg(name, a, b, row_sharded):
        key, off = where[name]
        shape = (NS, rows_of[key], W[name][0].shape[2])
        G[key] = wgrad(a, b, row_sharded, "wg_" + name, (G.get(key), shape, off))

    dpp1, dgl1, dx5 = ple_bwd(dx6, pp1, gl1, W["ple_w_gate1"], "ple_bwd1")
    wg("ple_w_proj1", (1, p), dpp1, False)
    wg("ple_w_gate1", x5b, dgl1, True)
    dpre_mlp1, dpre_mlp1b, dm1, g_mlp_g1, g_mlp_b1 = mlp_bwd1(dx5, pre_mlp1, row(small["mlp_ln_g"], 1), r1,
                                                              W["mlp_w_down1"], "mlp_bwd1_1")
    wg("mlp_w_down1", r1, dpre_mlp1b, True)
    wg("mlp_w_up1", x4b, dm1, False)
    dpre_mix1, dpre_mix1b, do, g_mix_g1, g_mix_b1, _ = mlp_bwd2(dpre_mlp1, dm1, W["mlp_w_up1"], alpha, pre_mix1,
                                                                row(small["mix_ln_g"], 1), W["attn_w_o"], "mlp_bwd2_1")
    wg("attn_w_o", o, dpre_mix1b, True)
    dq, dk, dv, dsinks = attn_bwd(q, k, v, do, small["attn_sinks"])
    dqb, dkb, dvb, dx3 = qkv_bwd(dq, dk, dv, dpre_mix1,
                                 W["attn_w_q"], W["kv_w_k"], W["kv_w_v"], cs, alpha)
    wg("attn_w_q", x3b, dqb, True)
    wg("kv_w_k", x3b, dkb, True)
    wg("kv_w_v", x3b, dvb, True)
    hook("grads3", None, G)

    dpp0, dgl0, dx2 = ple_bwd(dx3, pp0, gl0, W["ple_w_gate0"], "ple_bwd0")
    wg("ple_w_proj0", (0, p), dpp0, False)
    wg("ple_w_gate0", x2b, dgl0, True)
    dpre_mlp0, dpre_mlp0b, dm0, g_mlp_g0, g_mlp_b0 = mlp_bwd1(dx2, pre_mlp0, row(small["mlp_ln_g"], 0), r0,
                                                              W["mlp_w_down0"], "mlp_bwd1_0")
    wg("mlp_w_down0", r0, dpre_mlp0b, True)
    wg("mlp_w_up0", x1b, dm0, False)
    hook("grads2", None, G)
    dpre_mix0, dpre_mix0b, dsw, g_mix_g0, g_mix_b0, g_b_out = mlp_bwd2(dpre_mlp0, dm0, W["mlp_w_up0"], alpha, pre_mix0,
                                                                      row(small["mix_ln_g"], 0), W["conv_w_out"],
                                                                      "mlp_bwd2_0")
    wg("conv_w_out", s, dpre_mix0b, True)
    hook("grads1", None, G)
    dcv, g_cln_g, g_cln_b, g_b_dw = conv_mid_bwd(dsw, cv, small["conv_ln_g"], small["conv_ln_b"])
    dh, g_w_dw, g_b_in = dwconv_bwd(dcv, h, small["conv_w_dw"], taps)
    wg("conv_w_in", x, dh, False)

    sg["conv_b_in"] = g_b_in
    sg["conv_w_dw"] = g_w_dw
    sg["conv_b_dw"], sg["conv_ln_g"], sg["conv_ln_b"], sg["conv_b_out"] = g_b_dw, g_cln_g, g_cln_b, g_b_out
    sg["mix_ln_g"] = [g_mix_g0, g_mix_g1]
    sg["mix_ln_b"] = [g_mix_b0, g_mix_b1]
    sg["mlp_ln_g"] = [g_mlp_g0, g_mlp_g1]
    sg["mlp_ln_b"] = [g_mlp_b0, g_mlp_b1]
    sg["attn_sinks"] = dsinks[:, 0][None, :]
    sg["loss"] = loss
    hook("grads0", None, G, sg)
    grad_x = conv_in_bwd(dh, dpre_mix0, W["conv_w_in"], alpha)
    return loss, grad_x, G, sg


BUFFERS = (("b0", ("conv_w_in",)), ("a0", ("conv_w_out",)),
           ("a1", ("mlp_w_up0", "mlp_w_down0", "ple_w_gate0")), ("c1", ("ple_w_proj0",)),
           ("a2", ("mlp_w_up1", "mlp_w_down1", "ple_w_gate1", "attn_w_q", "attn_w_o")),
           ("c2", ("kv_w_k", "kv_w_v", "ple_w_proj1")))
GROUPS = (("b0", "a0"), ("a1", "c1"), ("a2", "c2"))
REDUCED = (("b0",), ("a0",), ("a1", "c1"), ("a2", "c2"))
ROW_SHARDED = {"mlp_w_down0", "mlp_w_down1", "ple_w_gate0", "ple_w_gate1", "conv_w_out", "attn_w_q", "attn_w_o", "kv_w_k",
               "kv_w_v"}


def _split_layers(weights):
    out = {"conv_w_in": weights["conv_w_in"][0], "conv_w_out": weights["conv_w_out"][0],
           "attn_w_q": weights["attn_w_q"][0], "attn_w_o": weights["attn_w_o"][0],
           "kv_w_k": weights["kv_w_k"], "kv_w_v": weights["kv_w_v"]}
    for n in ("mlp_w_up", "mlp_w_down", "ple_w_proj", "ple_w_gate"):
        for i in range(weights[n].shape[0]):
            out[n + str(i)] = weights[n][i]
    return out


def _layout(shards):
    lay = {}
    for key, names in BUFFERS:
        off, rows = 0, []
        for n in names:
            rows.append((n, off, shards[n].shape[0]))
            off += shards[n].shape[0]
        lay[key] = rows
    return lay


def _place():
    return lax.axis_index("x"), lax.axis_index("y"), lax.axis_index("c")


def _flip(v, f):
    return (v + f) % 2 if f else v


CHIP_FLIPS = ((1, 0), (0, 1), (1, 1))


HBM = pl.BlockSpec(memory_space=pltpu.HBM)
SEM = pl.BlockSpec(memory_space=pltpu.SEMAPHORE)
EFFECT = pltpu.SideEffectType.DATAFLOW_SIDE_EFFECTING


def _half(ref, rows, c):
    return ref.at[pl.ds(pl.multiple_of(c * (rows // 2), 16), rows // 2), :]


def _gather_copies(refs, shapes, whole, send, recv):
    x, y, c = _place()
    me = 2 * x + y
    na = len(refs)
    cps = []
    for d, (fx, fy) in enumerate(CHIP_FLIPS):
        to = (_flip(x, fx), _flip(y, fy), c)
        for k in range(na):
            mine = refs[k].at[me] if k >= na - whole else _half(refs[k].at[me], shapes[k][1], c)
            cps.append(pltpu.make_async_remote_copy(mine, mine, send.at[d * na + k], recv.at[d * na + k], device_id=to,
                                                    device_id_type=MESH))
    return cps


def gather_start(bufs, whole, after, name):
    na = len(bufs)
    shapes = [b.shape for b in bufs]
    nsem = len(CHIP_FLIPS) * na

    def body(*refs):
        ins = refs[:na]
        send, recv = refs[-(na + 3)], refs[-(na + 2)]
        token = refs[-1]
        for cp in _gather_copies(ins, shapes, whole, send, recv):
            cp.start()
        token[...] = jnp.zeros_like(token)

    args = [pltpu.with_memory_space_constraint(b, pltpu.HBM) for b in bufs]
    ins = [HBM] * na
    if after is not None:
        args.append(after)
        ins.append(ANY)
    return pl.pallas_call(
        body, name=name, in_specs=ins,
        out_specs=[SEM, SEM] + [HBM] * na + [pl.BlockSpec(memory_space=pltpu.VMEM)],
        out_shape=[pltpu.SemaphoreType.DMA((nsem,)), pltpu.SemaphoreType.DMA((nsem,))]
        + [pltpu.HBM(b.shape, b.dtype) for b in bufs] + [_sds((8, 128), F32)],
        input_output_aliases={k: k + 2 for k in range(na)},
        compiler_params=pltpu.CompilerParams(has_side_effects=EFFECT))(*args)


def gather_wait(send, recv, bufs, whole, after, name):
    na = len(bufs)
    shapes = [b.shape for b in bufs]

    def body(*refs):
        ins = refs[:na]
        send_ref, recv_ref = refs[na], refs[na + 1]
        for cp in _gather_copies(ins, shapes, whole, send_ref, recv_ref):
            cp.wait_send()
            cp.wait_recv()

    return pl.pallas_call(
        body, name=name, in_specs=[HBM] * na + [SEM, SEM, ANY], out_specs=[HBM] * na,
        out_shape=[pltpu.HBM(b.shape, b.dtype) for b in bufs], input_output_aliases={k: k for k in range(na)},
        compiler_params=pltpu.CompilerParams(has_side_effects=EFFECT))(*bufs, send, recv, after)


def sibling_forward(bufs, name):
    nb = len(bufs)

    def body(*refs):
        outs = refs[nb:2 * nb]
        send, recv = refs[2 * nb:]
        x, y, c = _place()
        cps = []
        for d, (fx, fy) in enumerate(CHIP_FLIPS):
            frm = 2 * _flip(x, fx) + _flip(y, fy)
            for k in range(nb):
                theirs = _half(outs[k].at[frm], bufs[k].shape[1], c)
                cps.append(pltpu.make_async_remote_copy(theirs, theirs, send.at[d * nb + k], recv.at[d * nb + k],
                                                        device_id=(x, y, 1 - c), device_id_type=MESH))
        for cp in cps:
            cp.start()
        for cp in cps:
            cp.wait()

    nsem = len(CHIP_FLIPS) * nb
    return pl.pallas_call(
        body, name=name, in_specs=[ANY] * nb, out_specs=[ANY] * nb, out_shape=[_sds(b.shape, b.dtype) for b in bufs],
        input_output_aliases={k: k for k in range(nb)},
        scratch_shapes=[pltpu.SemaphoreType.DMA((nsem,)), pltpu.SemaphoreType.DMA((nsem,))])(*bufs)


def pack_rows(pieces, rows, width, name):
    def body(*refs):
        o_ref = refs[-1]
        o_ref[...] = jnp.zeros_like(o_ref)
        for ref, (a, off) in zip(refs[:-1], pieces):
            o_ref[off:off + a.shape[0], 0:a.shape[1]] = ref[...]

    return pl.pallas_call(body, name=name, out_shape=_sds((rows, width), F32))(*[a for a, _ in pieces])


PEER_FLIPS = tuple((fx, fy, fc) for fx in (0, 1) for fy in (0, 1) for fc in (0, 1) if fx or fy or fc)


def _reduce_copies(parts, zones, pack, send, recv):
    x, y, c = _place()
    nb = len(parts)
    na = nb + (1 if pack is not None else 0)
    cps = []
    for f, (fx, fy, fc) in enumerate(PEER_FLIPS):
        tx, ty, tc = _flip(x, fx), _flip(y, fy), _flip(c, fc)
        for k in range(nb):
            hrows = parts[k].shape[1] // 2
            piece = parts[k].at[2 * tx + ty, pl.ds(pl.multiple_of(tc * hrows, 16), hrows), :]
            cps.append(pltpu.make_async_remote_copy(piece, zones[k].at[f], send.at[f * na + k], recv.at[f * na + k],
                                                    device_id=(tx, ty, tc), device_id_type=MESH))
        if pack is not None:
            mine = pack.at[4 * x + 2 * y + c]
            cps.append(pltpu.make_async_remote_copy(mine, mine, send.at[f * na + nb], recv.at[f * na + nb],
                                                    device_id=(tx, ty, tc), device_id_type=MESH))
    return cps


def reduce_begin(parts, pack, name):
    nb = len(parts)
    zones = [lax.empty((len(PEER_FLIPS), g.shape[1] // 2, g.shape[2]), g.dtype) for g in parts]
    arrs = list(parts) + zones + ([pack] if pack is not None else [])
    na = len(arrs)
    nsem = len(PEER_FLIPS) * (nb + (1 if pack is not None else 0))

    def body(*refs):
        ins = refs[:na]
        send, recv = refs[na], refs[na + 1]
        for cp in _reduce_copies(ins[:nb], ins[nb:2 * nb], ins[2 * nb] if pack is not None else None, send, recv):
            cp.start()
        refs[-1][...] = jnp.zeros_like(refs[-1])

    return pl.pallas_call(
        body, name=name, in_specs=[HBM] * na,
        out_specs=[SEM, SEM] + [HBM] * na + [pl.BlockSpec(memory_space=pltpu.VMEM)],
        out_shape=[pltpu.SemaphoreType.DMA((nsem,)), pltpu.SemaphoreType.DMA((nsem,))]
        + [pltpu.HBM(a.shape, a.dtype) for a in arrs] + [_sds((8, 128), F32)],
        input_output_aliases={k: k + 2 for k in range(na)},
        compiler_params=pltpu.CompilerParams(has_side_effects=EFFECT))(
            *[pltpu.with_memory_space_constraint(a, pltpu.HBM) for a in arrs])


def reduce_end(send, recv, parts, zones, pack, after, name):
    nb = len(parts)
    arrs = list(parts) + list(zones) + ([pack] if pack is not None else [])
    na = len(arrs)

    def body(*refs):
        ins = refs[:na]
        for cp in _reduce_copies(ins[:nb], ins[nb:2 * nb], ins[2 * nb] if pack is not None else None, refs[na], refs[na + 1]):
            cp.wait_send()
            cp.wait_recv()

    return pl.pallas_call(
        body, name=name, in_specs=[HBM] * na + [SEM, SEM, ANY], out_specs=[HBM] * na,
        out_shape=[pltpu.HBM(a.shape, a.dtype) for a in arrs], input_output_aliases={k: k for k in range(na)},
        compiler_params=pltpu.CompilerParams(has_side_effects=EFFECT))(*arrs, send, recv, after)


def sibling_share(halves, name):
    nb = len(halves)

    def body(*refs):
        outs = refs[nb:2 * nb]
        send, recv = refs[2 * nb:]
        x, y, c = _place()
        cps = []
        for k in range(nb):
            hrows = halves[k].shape[0] // 2
            mine = outs[k].at[pl.ds(pl.multiple_of(c * hrows, 8), hrows), :]
            cps.append(pltpu.make_async_remote_copy(mine, mine, send.at[k], recv.at[k], device_id=(x, y, 1 - c),
                                                    device_id_type=MESH))
        for cp in cps:
            cp.start()
        for cp in cps:
            cp.wait()

    return pl.pallas_call(
        body, name=name, in_specs=[ANY] * nb, out_specs=[ANY] * nb,
        out_shape=[_sds(h.shape, h.dtype) for h in halves], input_output_aliases={k: k for k in range(nb)},
        scratch_shapes=[pltpu.SemaphoreType.DMA((nb,)), pltpu.SemaphoreType.DMA((nb,))])(*halves)


def _row_tile(rows):
    for cand in (512, 384, 256, 128, 64, 32, 16):
        if rows % cand == 0:
            return cand
    return rows


def piece_sum(g, z, idx, name):
    _, hrows, W = z.shape
    tr = _row_tile(hrows)
    nrb = hrows // tr

    def body(idx_ref, g_ref, z_ref, o_ref):
        acc = g_ref[...].astype(F32)
        for d in range(z.shape[0]):
            acc = acc + z_ref[d].astype(F32)
        o_ref[...] = acc

    gs = pltpu.PrefetchScalarGridSpec(
        num_scalar_prefetch=1, grid=(nrb,),
        in_specs=[pl.BlockSpec((None, tr, W), lambda i, sc: (sc[0], sc[1] * nrb + i, 0)),
                  pl.BlockSpec((z.shape[0], tr, W), lambda i, sc: (0, i, 0))],
        out_specs=pl.BlockSpec((tr, W), lambda i, sc: (sc[1] * nrb + i, 0)))
    return pl.pallas_call(body, name=name, grid_spec=gs, out_shape=_sds((2 * hrows, W), F32),
                          compiler_params=pltpu.CompilerParams(dimension_semantics=("parallel",),
                                                               vmem_limit_bytes=48 * 2 ** 20))(idx, g, z)


def small_sum(packs):
    n, R, W = packs.shape

    def body(p_ref, o_ref):
        acc = p_ref[0]
        for d in range(1, n):
            acc = acc + p_ref[d]
        o_ref[...] = acc

    return pl.pallas_call(body, name="small_sum", out_shape=_sds((R, W), F32))(packs)


WEIGHTS = ["conv_w_in", "conv_b_in", "conv_w_dw", "conv_b_dw", "conv_ln_g", "conv_ln_b", "conv_w_out", "conv_b_out", "kv_w_k",
           "kv_w_v", "attn_w_q", "attn_sinks", "attn_w_o", "mix_ln_g", "mix_ln_b", "mlp_w_up", "mlp_w_down", "mlp_ln_g",
           "mlp_ln_b", "ple_w_proj", "ple_w_gate"]
BIG = ["conv_w_in", "conv_w_out", "kv_w_k", "kv_w_v", "attn_w_q", "attn_w_o", "mlp_w_up", "mlp_w_down", "ple_w_proj",
       "ple_w_gate"]
SMALL = [n for n in WEIGHTS if n not in BIG]


def _step(x, p, target, w, m, v):
    D = x.shape[-1]
    ds = D // NS
    xq, yq, cq = _place()
    chip = 2 * xq + yq
    idx = jnp.stack([chip, cq]).astype(jnp.int32)

    shards = _split_layers(w)
    lay = _layout(shards)
    taps = w["conv_w_dw"].shape[1]
    small_loc = pack_rows([(w["conv_w_dw"][0], 0), (w["conv_b_dw"], HALO), (w["conv_ln_g"], HALO + 1), (w["conv_ln_b"], HALO + 2),
                           (w["conv_b_out"], HALO + 3), (w["conv_b_in"].reshape(2, ds), HALO + 4)], HALO + 8, ds, "pack_small")
    slot = lambda a: lax.dynamic_update_slice(lax.empty((NS,) + a.shape, a.dtype), a[None], (chip, 0, 0))
    started, token = [], None
    for gi, keys in enumerate(GROUPS):
        bufs = [slot(jnp.concatenate([shards[n].astype(BF16) for n, _, _ in lay[key]], axis=0)) for key in keys]
        if gi == 0:
            bufs.append(slot(small_loc))
        send, recv, *thru, token = gather_start(bufs, 1 if gi == 0 else 0, token, "gather_start%d" % gi)
        started.append((send, recv, thru))
    W = {}

    def arrive(gi, after):
        send, recv, thru = started[gi]
        whole = 1 if gi == 0 else 0
        got = gather_wait(send, recv, thru, whole, after, "gather_wait%d" % gi)
        nk = len(GROUPS[gi])
        for key, buf in zip(GROUPS[gi], sibling_forward(got[:nk], "sibling_forward%d" % gi)):
            for n, off, rows in lay[key]:
                W[n] = (buf, off, rows)
        return got[nk:]

    gs, = arrive(0, token)
    across = lambda rows: gs[:, rows, :].transpose(1, 0, 2).reshape(rows.stop - rows.start, D)
    small = {"taps": taps, "conv_w_dw": across(slice(0, HALO)), "conv_b_dw": across(slice(HALO, HALO + 1)),
             "conv_ln_g": across(slice(HALO + 1, HALO + 2)), "conv_ln_b": across(slice(HALO + 2, HALO + 3)),
             "conv_b_out": across(slice(HALO + 3, HALO + 4)), "conv_b_in": gs[:, HALO + 4:HALO + 6, :].reshape(1, 2 * D),
             "attn_sinks": w["attn_sinks"], "mix_ln_g": w["mix_ln_g"], "mix_ln_b": w["mix_ln_b"],
             "mlp_ln_g": w["mlp_ln_g"], "mlp_ln_b": w["mlp_ln_b"]}

    reducing = {}

    def reduce_start(gi, G, pack):
        nk = len(REDUCED[gi])
        send, recv, *thru, token = reduce_begin([G[key] for key in REDUCED[gi]], pack, "reduce_begin%d" % gi)
        reducing[gi] = (send, recv, thru[:nk], thru[nk:2 * nk], thru[2 * nk] if pack is not None else None)
        _FOLLOW.append(token)

    def small_pack(sg):
        pieces = [(sg["conv_b_in"].reshape(2, D), 0), (sg["conv_w_dw"], 2)]
        r0 = 2 + HALO
        for i, n in enumerate(("conv_b_dw", "conv_ln_g", "conv_ln_b", "conv_b_out")):
            pieces.append((sg[n], r0 + i))
        r0 += 4
        for i, n in enumerate(("mix_ln_g", "mix_ln_b", "mlp_ln_g", "mlp_ln_b")):
            pieces += [(sg[n][0], r0 + 2 * i), (sg[n][1], r0 + 2 * i + 1)]
        pieces += [(sg["attn_sinks"], r0 + 8), (sg["loss"][0:1], r0 + 9)]
        mine = pack_rows(pieces, r0 + 10, D, "pack_small_grads")
        return lax.dynamic_update_slice(lax.empty((8,) + mine.shape, F32), mine[None], (4 * xq + 2 * yq + cq, 0, 0))

    def hook(stage, after, G, sg=None):
        if stage == "weights1":
            arrive(1, after)
        elif stage == "weights2":
            arrive(2, after)
        elif stage == "grads0":
            reduce_start(0, G, small_pack(sg))
        elif stage.startswith("grads"):
            reduce_start(int(stage[5:]), G, None)

    loss, grad_x, G, sg = _local_step(x[0], p[:, 0], target[0], W, small, lay, hook)
    _FOLLOW.clear()
    nsink = w["attn_sinks"].shape[1]

    grads, delta, new_m, new_v = {}, {}, {}, {}
    found = {}

    def finish(groups, after, tag):
        keys, halves, tot = [], [], None
        for gi in groups:
            send, recv, parts, zones, pack = reducing[gi]
            done = reduce_end(send, recv, parts, zones, pack, after, "reduce_end%d" % gi)
            nk = len(REDUCED[gi])
            for key, g_, z_ in zip(REDUCED[gi], done[:nk], done[nk:2 * nk]):
                keys.append(key)
                halves.append(piece_sum(g_, z_, idx, "piece_sum_" + key))
            if pack is not None:
                tot = small_sum(done[2 * nk])
        for key, buf in zip(keys, sibling_share(halves, "sibling_share" + tag)):
            for n, off, _ in lay[key]:
                found[n] = (buf, off)
        return tot

    def big_adamw(names):
        for n in names:
            three = lambda a: a.reshape((-1,) + a.shape[-2:])
            w3, m3, v3 = three(w[n]), three(m[n]), three(v[n])
            outs = None
            for i in range(w3.shape[0]):
                buf, off = found[n + str(i)] if n + str(i) in found else found[n]
                outs = adamw_layer(w3, m3, v3, i, buf, off, outs, "adamw_%s%d" % (n, i))
            grads[n], delta[n], new_m[n], new_v[n] = [a.reshape(w[n].shape) for a in outs]

    last = [n for n, _, _ in lay[REDUCED[0][0]]]
    finish(reversed(range(1, len(REDUCED))), grad_x, "1")
    big_adamw([n for n in BIG if n not in last])
    tot = finish([0], new_v["mlp_w_down"], "0")
    big_adamw(last)
    cols = lambda rows: lax.dynamic_slice(rows, (0, chip * ds), (rows.shape[0], ds))
    grads["conv_b_in"] = lax.dynamic_slice(tot[0:2].reshape(1, 2 * D), (0, chip * 2 * ds), (1, 2 * ds))
    grads["conv_w_dw"] = cols(tot[2:2 + taps])[None]
    r0 = 2 + HALO
    for i, n in enumerate(("conv_b_dw", "conv_ln_g", "conv_ln_b", "conv_b_out")):
        grads[n] = cols(tot[r0 + i:r0 + i + 1])
    r0 += 4
    for i, n in enumerate(("mix_ln_g", "mix_ln_b", "mlp_ln_g", "mlp_ln_b")):
        grads[n] = tot[r0 + 2 * i:r0 + 2 * i + 2]
    grads["attn_sinks"] = tot[r0 + 8:r0 + 9, 0:nsink]

    ds_, ms_, vs_ = adamw_many([w[n] for n in SMALL], [grads[n] for n in SMALL], [m[n] for n in SMALL], [v[n] for n in SMALL])
    for n, d_, m_, v_ in zip(SMALL, ds_, ms_, vs_):
        delta[n], new_m[n], new_v[n] = d_, m_, v_

    total = tot[r0 + 9, 0]
    return (total, grad_x[None], *[grads[n] for n in WEIGHTS], *[delta[n] for n in WEIGHTS], *[new_m[n] for n in WEIGHTS],
            *[new_v[n] for n in WEIGHTS])


def kernel(x, p, conv_w_in, conv_b_in, conv_w_dw, conv_b_dw, conv_ln_g, conv_ln_b, conv_w_out, conv_b_out, kv_w_k, kv_w_v, attn_w_q, attn_sinks, attn_w_o, mix_ln_g, mix_ln_b, mlp_w_up, mlp_w_down, mlp_ln_g, mlp_ln_b, ple_w_proj, ple_w_gate, loss_target, m_conv_w_in, m_conv_b_in, m_conv_w_dw, m_conv_b_dw, m_conv_ln_g, m_conv_ln_b, m_conv_w_out, m_conv_b_out, m_kv_w_k, m_kv_w_v, m_attn_w_q, m_attn_sinks, m_attn_w_o, m_mix_ln_g, m_mix_ln_b, m_mlp_w_up, m_mlp_w_down, m_mlp_ln_g, m_mlp_ln_b, m_ple_w_proj, m_ple_w_gate, v_conv_w_in, v_conv_b_in, v_conv_w_dw, v_conv_b_dw, v_conv_ln_g, v_conv_ln_b, v_conv_w_out, v_conv_b_out, v_kv_w_k, v_kv_w_v, v_attn_w_q, v_attn_sinks, v_attn_w_o, v_mix_ln_g, v_mix_ln_b, v_mlp_w_up, v_mlp_w_down, v_mlp_ln_g, v_mlp_ln_b, v_ple_w_proj, v_ple_w_gate):
    w = dict(zip(WEIGHTS, (conv_w_in, conv_b_in, conv_w_dw, conv_b_dw, conv_ln_g, conv_ln_b, conv_w_out, conv_b_out, kv_w_k,
                           kv_w_v, attn_w_q, attn_sinks, attn_w_o, mix_ln_g, mix_ln_b, mlp_w_up, mlp_w_down, mlp_ln_g, mlp_ln_b,
                           ple_w_proj, ple_w_gate)))
    m = dict(zip(WEIGHTS, (m_conv_w_in, m_conv_b_in, m_conv_w_dw, m_conv_b_dw, m_conv_ln_g, m_conv_ln_b, m_conv_w_out,
                           m_conv_b_out, m_kv_w_k, m_kv_w_v, m_attn_w_q, m_attn_sinks, m_attn_w_o, m_mix_ln_g, m_mix_ln_b,
                           m_mlp_w_up, m_mlp_w_down, m_mlp_ln_g, m_mlp_ln_b, m_ple_w_proj, m_ple_w_gate)))
    v = dict(zip(WEIGHTS, (v_conv_w_in, v_conv_b_in, v_conv_w_dw, v_conv_b_dw, v_conv_ln_g, v_conv_ln_b, v_conv_w_out,
                           v_conv_b_out, v_kv_w_k, v_kv_w_v, v_attn_w_q, v_attn_sinks, v_attn_w_o, v_mix_ln_g, v_mix_ln_b,
                           v_mlp_w_up, v_mlp_w_down, v_mlp_ln_g, v_mlp_ln_b, v_ple_w_proj, v_ple_w_gate)))
    return _step(x, p, loss_target, w, m, v)
```

```python
import functools

import jax
import jax.numpy as jnp
from jax import lax
from jax.experimental import pallas as pl
from jax.experimental.pallas import tpu as pltpu

F32 = jnp.float32
BF16 = jnp.bfloat16
NS = 4
HEAD = 64
BLK = 128
ROPE = 16
ROPE_THETA = 500000.0
LN_EPS = 1e-5
NEG = -1e30
HALO = 32
ADAM_LR, ADAM_B1, ADAM_B2, ADAM_EPS, ADAM_WD, ADAM_STEP = 0.001, 0.9, 0.999, 1e-08, 0.01, 10
MESH = pl.DeviceIdType.MESH
ANY = pl.BlockSpec(memory_space=pl.ANY)
NT = (((1,), (1,)), ((), ()))
TN = (((0,), (0,)), ((), ()))


_FOLLOW = []


def _pc(body, name, grid, in_specs, out_specs, out_shape, scratch=(), sem=None, vmem=56, **kw):
    call = lambda fn, ins: pl.pallas_call(
        fn, name=name, grid=grid, in_specs=ins, out_specs=out_specs, out_shape=out_shape,
        scratch_shapes=list(scratch),
        compiler_params=pltpu.CompilerParams(dimension_semantics=sem, vmem_limit_bytes=vmem * 2 ** 20), **kw)
    if not _FOLLOW:
        return call(body, in_specs)
    extra = list(_FOLLOW)
    _FOLLOW.clear()
    n_in = len(in_specs)

    def ordered(*refs):
        return body(*refs[:n_in], *refs[n_in + len(extra):])

    run = call(ordered, list(in_specs) + [ANY] * len(extra))
    return lambda *args: run(*args, *extra)


def _rows(tm, n):
    return pl.BlockSpec((tm, n), lambda i: (i, 0))


def _const(shape):
    return pl.BlockSpec(shape, lambda *_: (0,) * len(shape))


def _wspec(w):
    buf, off, rows = w
    assert off % rows == 0
    return pl.BlockSpec((NS, rows, buf.shape[2]), lambda *_: (0, off // rows, 0))


def _rows_joined(w_ref):
    n, r, c = w_ref.shape
    return w_ref[...].reshape(n * r, c)


def _sds(shape, dtype):
    return jax.ShapeDtypeStruct(shape, dtype)


def _tile(t, rows=256):
    return min(rows, t)


def _sigmoid(x):
    return 0.5 * jnp.tanh(0.5 * x) + 0.5


def _ln_stats(w):
    mu = jnp.mean(w, axis=-1, keepdims=True)
    xc = w - mu
    var = jnp.mean(xc * xc, axis=-1, keepdims=True)
    rstd = lax.rsqrt(var + LN_EPS)
    return xc * rstd, rstd


def _ln_bwd(dy, w, g):
    xhat, rstd = _ln_stats(w)
    dxhat = dy * g
    m1 = jnp.mean(dxhat, axis=-1, keepdims=True)
    m2 = jnp.mean(dxhat * xhat, axis=-1, keepdims=True)
    dw = rstd * (dxhat - m1 - xhat * m2)
    return dw, jnp.sum(dy * xhat, axis=0, keepdims=True), jnp.sum(dy, axis=0, keepdims=True)


def _acc_rows(ref, val, first):
    @pl.when(first)
    def _():
        ref[...] = val

    @pl.when(jnp.logical_not(first))
    def _():
        ref[...] += val


def conv_in_fwd(xb, w_in, b_in):
    T, D = xb.shape
    nw = w_in[0].shape[2]
    tm = _tile(T, 512)

    def body(x_ref, w_ref, b_ref, h_ref):
        x = x_ref[...].astype(BF16)
        for j in range(NS):
            sl = slice(j * nw, (j + 1) * nw)
            h_ref[:, sl] = (jnp.dot(x, w_ref[j], preferred_element_type=F32) + b_ref[:, sl]).astype(BF16)

    return _pc(body, "conv_in_fwd", (T // tm,), [_rows(tm, D), _wspec(w_in), _const((1, NS * nw))],
               _rows(tm, NS * nw), _sds((T, NS * nw), BF16), sem=("parallel",))(xb, w_in[0], b_in)


CONV_ROWS = 16


def _phases(scr, sh):
    n = scr.shape[0] - 8
    for b in range(1, 8):
        sh[b - 1, 0:n, :] = scr[b:b + n, :]


def _spread(w_ref, wb, taps):
    for j in range(taps):
        wb[j] = jnp.broadcast_to(w_ref[j:j + 1, :], wb.shape[1:])


def _tap(scr, sh, o, n):
    b = o % 8
    return scr[o:o + n, :] if b == 0 else sh[b - 1, o - b:o - b + n, :]


def dwconv_fwd(h, w_dw, b_dw, ln_g, ln_b, taps):
    T = h.shape[0]
    C = h.shape[1] // 2
    tq = _tile(T)
    nh = tq // HALO
    off = HALO - (taps - 1)

    def body(a_ref, g_ref, ap_ref, gp_ref, w_ref, bdw_ref, lg_ref, lb_ref, cv_ref, s_ref, scr, sh, wb):
        i = pl.program_id(0)
        scr[HALO:HALO + tq, :] = a_ref[...].astype(F32) * _sigmoid(g_ref[...].astype(F32))
        up = ap_ref[...].astype(F32) * _sigmoid(gp_ref[...].astype(F32))
        scr[0:HALO, :] = jnp.where(i > 0, up, 0.0)
        _phases(scr, sh)
        _spread(w_ref, wb, taps)
        bias = jnp.broadcast_to(bdw_ref[...], (8, C))
        for r in range(tq // CONV_ROWS):
            accs = [bias] * (CONV_ROWS // 8)
            for j in range(taps):
                wj = wb[j]
                accs = [acc + wj * _tap(scr, sh, off + j + r * CONV_ROWS + 8 * k, 8) for k, acc in enumerate(accs)]
            for k, acc in enumerate(accs):
                cv_ref[r * CONV_ROWS + 8 * k:r * CONV_ROWS + 8 * k + 8, :] = acc
        xhat, _ = _ln_stats(cv_ref[...])
        ln = xhat * lg_ref[...] + lb_ref[...]
        s_ref[...] = (ln * _sigmoid(ln)).astype(BF16)

    prev = lambda col: pl.BlockSpec((HALO, C), lambda i: (jnp.maximum(i * nh - 1, 0), col))
    cur = lambda col: pl.BlockSpec((tq, C), lambda i: (i, col))
    return _pc(body, "dwconv_fwd", (T // tq,),
               [cur(0), cur(1), prev(0), prev(1), _const((HALO, C)), _const((1, C)), _const((1, C)), _const((1, C))],
               [_rows(tq, C), _rows(tq, C)], [_sds((T, C), F32), _sds((T, C), BF16)],
               scratch=[pltpu.VMEM((HALO + tq, C), F32), pltpu.VMEM((7, HALO + tq, C), F32), pltpu.VMEM((taps, 8, C), F32)],
               sem=("parallel",))(h, h, h, h, w_dw, b_dw, ln_g, ln_b)


def mm_res_ln(a, w, res, g, b, alpha, bias, name):
    T, K = a.shape
    ks = K // NS
    D = res.shape[1]
    tm = _tile(T, 512)

    def body(*refs):
        a_ref, w_ref, res_ref, g_ref, b_ref = refs[:5]
        n = 5
        if bias is not None:
            bias_ref = refs[5]
            n = 6
        pre_ref, xo_ref, xb_ref = refs[n:n + 3]
        acc = jnp.dot(a_ref[...], _rows_joined(w_ref), preferred_element_type=F32)
        if bias is not None:
            acc = acc + bias_ref[...]
        pre = alpha * res_ref[...] + acc
        xhat, _ = _ln_stats(pre)
        xo = xhat * g_ref[...] + b_ref[...]
        pre_ref[...] = pre
        xo_ref[...] = xo
        xb_ref[...] = xo.astype(BF16)

    ins = [_rows(tm, K), _wspec(w), _rows(tm, D), _const((1, D)), _const((1, D))]
    args = [a, w[0], res, g, b]
    if bias is not None:
        ins.append(_const((1, D)))
        args.append(bias)
    return _pc(body, name, (T // tm,), ins, [_rows(tm, D)] * 3, [_sds((T, D), F32), _sds((T, D), F32), _sds((T, D), BF16)],
               sem=("parallel",))(*args)


def mlp_up_fwd(xb, w_up, name):
    T, D = xb.shape
    fs = w_up[0].shape[2]
    tm = _tile(T, 1024)

    def body(x_ref, w_ref, r_ref):
        x = x_ref[...]
        for j in range(NS):
            m = jnp.maximum(jnp.dot(x, w_ref[j], preferred_element_type=F32), 0.0)
            r_ref[:, j * fs:(j + 1) * fs] = (m * m).astype(BF16)

    return _pc(body, name, (T // tm,), [_rows(tm, D), _wspec(w_up)], _rows(tm, NS * fs), _sds((T, NS * fs), BF16),
               sem=("parallel",))(xb, w_up[0])


def ple_fwd(x, xb, p, layer, w_proj, w_gate, target, name):
    T, D = x.shape
    P = p.shape[2]
    ds = D // NS
    tm = _tile(T, 512)
    last = target is not None

    def body(*refs):
        x_ref, xb_ref, p_ref, wp_ref, wg_ref = refs[:5]
        n = 5
        if last:
            t_ref = refs[5]
            n = 6
        o_ref, o2_ref, pp_ref, gl_ref = refs[n:n + 4]
        gl = jnp.dot(xb_ref[...], _rows_joined(wg_ref), preferred_element_type=F32)
        gl_ref[...] = gl.astype(BF16)
        sg = _sigmoid(gl)
        pb = p_ref[...].astype(BF16)
        sq = jnp.zeros((1, 1), F32)
        for j in range(NS):
            sl = slice(j * ds, (j + 1) * ds)
            pp = jnp.dot(pb, wp_ref[j], preferred_element_type=F32)
            pp_ref[:, sl] = pp.astype(BF16)
            out = x_ref[:, sl] + pp * sg[:, sl]
            if last:
                err = out - t_ref[:, sl]
                o_ref[:, sl] = err * (1.0 / D)
                e2 = jnp.sum(err * err, axis=0, keepdims=True)
                sq = sq + jnp.sum(e2, axis=1, keepdims=True)
            else:
                o_ref[:, sl] = out
                o2_ref[:, sl] = out.astype(BF16)
        if last:
            _acc_rows(o2_ref, jnp.broadcast_to(sq * (0.5 / D), (8, 128)), pl.program_id(0) == 0)

    ins = [_rows(tm, D), _rows(tm, D), pl.BlockSpec((None, tm, P), lambda i: (layer, i, 0)), _wspec(w_proj), _wspec(w_gate)]
    args = [x, xb, p, w_proj[0], w_gate[0]]
    if last:
        ins.append(_rows(tm, D))
        args.append(target)
        outs = [_rows(tm, D), _const((8, 128)), _rows(tm, D), _rows(tm, D)]
        shapes = [_sds((T, D), F32), _sds((8, 128), F32), _sds((T, D), BF16), _sds((T, D), BF16)]
    else:
        outs = [_rows(tm, D)] * 4
        shapes = [_sds((T, D), F32), _sds((T, D), BF16), _sds((T, D), BF16), _sds((T, D), BF16)]
    return _pc(body, name, (T // tm,), ins, outs, shapes, sem=("arbitrary",) if last else ("parallel",))(*args)


def _rope(x, cs_ref, sign):
    c = cs_ref[0]
    s = cs_ref[1] * sign
    lane = lax.broadcasted_iota(jnp.int32, c.shape, 1)
    first = (lane % HEAD) < (ROPE // 2)
    outs = []
    for gq in range(x.shape[1] // 128):
        xg = x[:, gq * 128:(gq + 1) * 128]
        sw = jnp.where(first, pltpu.roll(xg, 128 - ROPE // 2, 1), pltpu.roll(xg, ROPE // 2, 1))
        outs.append(xg * c + sw * s)
    return outs


def qkv_fwd(xb, w_q, w_k, w_v, cs):
    T, D = xb.shape
    ds = D // NS
    HD, KVD = w_q[0].shape[2], w_k[0].shape[2]
    tm = _tile(T, 512)
    scale = 1.0 / (HEAD ** 0.5)

    def body(x_ref, wq_ref, wk_ref, wv_ref, cs_ref, q_ref, k_ref, v_ref):
        def proj(w_ref):
            return jnp.dot(x_ref[...], _rows_joined(w_ref), preferred_element_type=F32)

        for gq, val in enumerate(_rope(proj(wq_ref), cs_ref, 1.0)):
            q_ref[:, gq * 128:(gq + 1) * 128] = (val * scale).astype(BF16)
        for gq, val in enumerate(_rope(proj(wk_ref), cs_ref, 1.0)):
            k_ref[:, gq * 128:(gq + 1) * 128] = val.astype(BF16)
        v_ref[...] = proj(wv_ref).astype(BF16)

    cs_spec = pl.BlockSpec((2, tm, 128), lambda i: (0, i, 0))
    return _pc(body, "qkv_fwd", (T // tm,), [_rows(tm, D), _wspec(w_q), _wspec(w_k), _wspec(w_v), cs_spec],
               [_rows(tm, HD), _rows(tm, KVD), _rows(tm, KVD)],
               [_sds((T, HD), BF16), _sds((T, KVD), BF16), _sds((T, KVD), BF16)], sem=("parallel",))(
                   xb, w_q[0], w_k[0], w_v[0], cs)


def _band_mask(n):
    row = lax.broadcasted_iota(jnp.int32, (BLK, 2 * BLK), 0)
    col = lax.broadcasted_iota(jnp.int32, (BLK, 2 * BLK), 1)
    return (col > row) & (col <= row + BLK) & ((col >= BLK) | (n > 0))


def _head(h):
    return slice(h * HEAD, (h + 1) * HEAD)


def _softmax_sink(s, sink):
    m = jnp.maximum(jnp.max(s, axis=-1, keepdims=True), sink)
    e = jnp.exp(s - m)
    es = jnp.exp(sink - m)
    den = jnp.sum(e, axis=-1, keepdims=True) + es
    inv = 1.0 / den
    return e * inv, es * inv


def attn_fwd(q, k, v, sinks):
    T, HD = q.shape
    KVD = k.shape[1]
    NKV = KVD // HEAD
    G = HD // KVD

    def body(s_ref, q_ref, kc_ref, kp_ref, vc_ref, vp_ref, o_ref):
        valid = _band_mask(pl.program_id(0))
        for kh in range(NKV):
            k2 = jnp.concatenate([kp_ref[:, _head(kh)], kc_ref[:, _head(kh)]], axis=0)
            v2 = jnp.concatenate([vp_ref[:, _head(kh)], vc_ref[:, _head(kh)]], axis=0)
            hs = [kh * G + gq for gq in range(G)]
            sc = [lax.dot_general(q_ref[:, _head(hh)], k2, NT, preferred_element_type=F32) for hh in hs]
            pb = [_softmax_sink(jnp.where(valid, s, NEG), s_ref[0, hh])[0].astype(BF16) for s, hh in zip(sc, hs)]
            for p, hh in zip(pb, hs):
                o_ref[:, _head(hh)] = jnp.dot(p, v2, preferred_element_type=F32).astype(BF16)

    cur = lambda n_: pl.BlockSpec((BLK, n_), lambda n: (n, 0))
    prev = lambda n_: pl.BlockSpec((BLK, n_), lambda n: (jnp.maximum(n - 1, 0), 0))
    return _pc(body, "attn_fwd", (T // BLK,),
               [pl.BlockSpec(memory_space=pltpu.SMEM), cur(HD), cur(KVD), prev(KVD), cur(KVD), prev(KVD)],
               cur(HD), _sds((T, HD), BF16), sem=("parallel",))(sinks, q, k, k, v, v)


def ple_bwd(dxo, pp, gl, w_gate, name):
    T, D = dxo.shape
    ds = D // NS
    tm = _tile(T, 512)

    def body(d_ref, pp_ref, gl_ref, wg_ref, dpp_ref, dgl_ref, dx_ref):
        d = d_ref[...]
        sg = _sigmoid(gl_ref[...].astype(F32))
        dpp_ref[...] = (d * sg).astype(BF16)
        dgl = (d * pp_ref[...].astype(F32) * sg * (1.0 - sg)).astype(BF16)
        dgl_ref[...] = dgl
        dx_ref[...] = d + lax.dot_general(dgl, _rows_joined(wg_ref), NT, preferred_element_type=F32)

    return _pc(body, name, (T // tm,), [_rows(tm, D)] * 3 + [_wspec(w_gate)], [_rows(tm, D)] * 3,
               [_sds((T, D), BF16), _sds((T, D), BF16), _sds((T, D), F32)], sem=("parallel",))(dxo, pp, gl, w_gate[0])


def mlp_bwd1(dy, pre, g, r, w_down, name):
    T, D = dy.shape
    fs = w_down[2]
    tm = _tile(T, 512)

    def body(dy_ref, pre_ref, g_ref, r_ref, w_ref, dw_ref, dwb_ref, dm_ref, dg_ref, db_ref):
        dw, dg, db = _ln_bwd(dy_ref[...], pre_ref[...], g_ref[...])
        first = pl.program_id(0) == 0
        _acc_rows(dg_ref, dg, first)
        _acc_rows(db_ref, db, first)
        dwb = dw.astype(BF16)
        dw_ref[...] = dw
        dwb_ref[...] = dwb
        for j in range(NS):
            sl = slice(j * fs, (j + 1) * fs)
            dr = lax.dot_general(dwb, w_ref[j], NT, preferred_element_type=F32)
            dm_ref[:, sl] = (dr * (2.0 * jnp.sqrt(r_ref[:, sl].astype(F32)))).astype(BF16)

    return _pc(body, name, (T // tm,), [_rows(tm, D), _rows(tm, D), _const((1, D)), _rows(tm, NS * fs), _wspec(w_down)],
               [_rows(tm, D), _rows(tm, D), _rows(tm, NS * fs), _const((1, D)), _const((1, D))],
               [_sds((T, D), F32), _sds((T, D), BF16), _sds((T, NS * fs), BF16), _sds((1, D), F32), _sds((1, D), F32)],
               sem=("arbitrary",))(dy, pre, g, r, w_down[0])


def mlp_bwd2(dpre, dm, w_up, alpha, pre_mix, g_mix, w_mix, name):
    T, D = dpre.shape
    fs = w_up[0].shape[2]
    ms = w_mix[2]
    tm = _tile(T, 512)

    def body(dp_ref, dm_ref, wu_ref, pre_ref, g_ref, wm_ref, dw_ref, dwb_ref, do_ref, dg_ref, db_ref, dc_ref):
        dy = alpha * dp_ref[...]
        for j in range(NS):
            dy = dy + lax.dot_general(dm_ref[:, j * fs:(j + 1) * fs], wu_ref[j], NT, preferred_element_type=F32)
        dw, dg, db = _ln_bwd(dy, pre_ref[...], g_ref[...])
        first = pl.program_id(0) == 0
        _acc_rows(dg_ref, dg, first)
        _acc_rows(db_ref, db, first)
        _acc_rows(dc_ref, jnp.sum(dw, axis=0, keepdims=True), first)
        dwb = dw.astype(BF16)
        dw_ref[...] = dw
        dwb_ref[...] = dwb
        do_ref[...] = lax.dot_general(dwb, _rows_joined(wm_ref), NT, preferred_element_type=F32).astype(BF16)

    return _pc(body, name, (T // tm,),
               [_rows(tm, D), _rows(tm, NS * fs), _wspec(w_up), _rows(tm, D), _const((1, D)), _wspec(w_mix)],
               [_rows(tm, D), _rows(tm, D), _rows(tm, NS * ms), _const((1, D)), _const((1, D)), _const((1, D))],
               [_sds((T, D), F32), _sds((T, D), BF16), _sds((T, NS * ms), BF16)] + [_sds((1, D), F32)] * 3,
               sem=("arbitrary",))(dpre, dm, w_up[0], pre_mix, g_mix, w_mix[0])


def attn_bwd(q, k, v, do, sinks):
    T, HD = q.shape
    KVD = k.shape[1]
    NH, NKV = HD // HEAD, KVD // HEAD
    G = NH // NKV
    nb = T // BLK

    def body(s_ref, q_ref, do_ref, kc_ref, kp_ref, vc_ref, vp_ref, dq_ref, dk_ref, dv_ref, ds_ref, ck, cv):
        n = pl.program_id(0)

        @pl.when(n == 0)
        def _():
            ck[...] = jnp.zeros_like(ck)
            cv[...] = jnp.zeros_like(cv)
            ds_ref[...] = jnp.zeros_like(ds_ref)

        @pl.when(n < nb)
        def _():
            valid = _band_mask(n)
            for kh in range(NKV):
                kv = _head(kh)
                k2 = jnp.concatenate([kp_ref[:, kv], kc_ref[:, kv]], axis=0)
                v2 = jnp.concatenate([vp_ref[:, kv], vc_ref[:, kv]], axis=0)
                hs = [kh * G + gq for gq in range(G)]
                qs = [q_ref[:, _head(hh)] for hh in hs]
                dos = [do_ref[:, _head(hh)] for hh in hs]
                sc = [lax.dot_general(qh, k2, NT, preferred_element_type=F32) for qh in qs]
                dp = [lax.dot_general(doh, v2, NT, preferred_element_type=F32) for doh in dos]
                pr = [_softmax_sink(jnp.where(valid, s, NEG), s_ref[0, hh]) for s, hh in zip(sc, hs)]
                delta = [jnp.sum(p * d, axis=-1, keepdims=True) for (p, _), d in zip(pr, dp)]
                dsb = [(p * (d - dl)).astype(BF16) for (p, _), d, dl in zip(pr, dp, delta)]
                pb = [p.astype(BF16) for p, _ in pr]
                for (_, ps), dl, hh in zip(pr, delta, hs):
                    ds_ref[hh:hh + 1, :] += jnp.broadcast_to(-jnp.sum(ps * dl, axis=0, keepdims=True), (1, 128))
                for d, hh in zip(dsb, hs):
                    dq_ref[:, _head(hh)] = jnp.dot(d, k2, preferred_element_type=F32)
                dk2 = lax.dot_general(jnp.concatenate(dsb, axis=0), jnp.concatenate(qs, axis=0), TN,
                                      preferred_element_type=F32)
                dv2 = lax.dot_general(jnp.concatenate(pb, axis=0), jnp.concatenate(dos, axis=0), TN,
                                      preferred_element_type=F32)
                dk_ref[:, kv] = ck[:, kv] + dk2[0:BLK]
                dv_ref[:, kv] = cv[:, kv] + dv2[0:BLK]
                ck[:, kv] = dk2[BLK:2 * BLK]
                cv[:, kv] = dv2[BLK:2 * BLK]

        @pl.when(n == nb)
        def _():
            dk_ref[...] = ck[...]
            dv_ref[...] = cv[...]

    qcur = pl.BlockSpec((BLK, HD), lambda n: (jnp.minimum(n, nb - 1), 0))
    kcur = pl.BlockSpec((BLK, KVD), lambda n: (jnp.minimum(n, nb - 1), 0))
    kprev = pl.BlockSpec((BLK, KVD), lambda n: (jnp.maximum(n - 1, 0), 0))
    return _pc(body, "attn_bwd", (nb + 1,),
               [pl.BlockSpec(memory_space=pltpu.SMEM), qcur, qcur, kcur, kprev, kcur, kprev],
               [qcur, kprev, kprev, _const((NH, 128))],
               [_sds((T, HD), F32), _sds((T, KVD), F32), _sds((T, KVD), F32), _sds((NH, 128), F32)],
               scratch=[pltpu.VMEM((BLK, KVD), F32), pltpu.VMEM((BLK, KVD), F32)],
               sem=("arbitrary",))(sinks, q, do, k, k, v, v)


def qkv_bwd(dq, dk, dv, dpre_mix, w_q, w_k, w_v, cs, alpha):
    T, HD = dq.shape
    KVD = dk.shape[1]
    D = dpre_mix.shape[1]
    ds = D // NS
    tm = _tile(T, 512)
    scale = 1.0 / (HEAD ** 0.5)

    def body(dq_ref, dk_ref, dv_ref, dp_ref, wq_ref, wk_ref, wv_ref, cs_ref, dqb_ref, dkb_ref, dvb_ref, dx_ref):
        for gq, val in enumerate(_rope(dq_ref[...], cs_ref, -1.0)):
            dqb_ref[:, gq * 128:(gq + 1) * 128] = (val * scale).astype(BF16)
        for gq, val in enumerate(_rope(dk_ref[...], cs_ref, -1.0)):
            dkb_ref[:, gq * 128:(gq + 1) * 128] = val.astype(BF16)
        dvb_ref[...] = dv_ref[...].astype(BF16)
        dqb, dkb, dvb = dqb_ref[...], dkb_ref[...], dvb_ref[...]
        dx_ref[...] = (alpha * dp_ref[...]
                       + lax.dot_general(dqb, _rows_joined(wq_ref), NT, preferred_element_type=F32)
                       + lax.dot_general(dkb, _rows_joined(wk_ref), NT, preferred_element_type=F32)
                       + lax.dot_general(dvb, _rows_joined(wv_ref), NT, preferred_element_type=F32))

    cs_spec = pl.BlockSpec((2, tm, 128), lambda i: (0, i, 0))
    return _pc(body, "qkv_bwd", (T // tm,),
               [_rows(tm, HD), _rows(tm, KVD), _rows(tm, KVD), _rows(tm, D), _wspec(w_q), _wspec(w_k), _wspec(w_v), cs_spec],
               [_rows(tm, HD), _rows(tm, KVD), _rows(tm, KVD), _rows(tm, D)],
               [_sds((T, HD), BF16), _sds((T, KVD), BF16), _sds((T, KVD), BF16), _sds((T, D), F32)],
               sem=("parallel",))(dq, dk, dv, dpre_mix, w_q[0], w_k[0], w_v[0], cs)


def conv_mid_bwd(ds, cv, ln_g, ln_b):
    T, C = cv.shape
    tm = _tile(T, 512)

    def body(ds_ref, cv_ref, g_ref, b_ref, dcv_ref, dg_ref, db_ref, dc_ref):
        xhat, _ = _ln_stats(cv_ref[...])
        ln = xhat * g_ref[...] + b_ref[...]
        sg = _sigmoid(ln)
        dl = ds_ref[...].astype(F32) * (sg * (1.0 + ln * (1.0 - sg)))
        dcv, dg, db = _ln_bwd(dl, cv_ref[...], g_ref[...])
        first = pl.program_id(0) == 0
        _acc_rows(dg_ref, dg, first)
        _acc_rows(db_ref, db, first)
        _acc_rows(dc_ref, jnp.sum(dcv, axis=0, keepdims=True), first)
        dcv_ref[...] = dcv

    return _pc(body, "conv_mid_bwd", (T // tm,), [_rows(tm, C), _rows(tm, C), _const((1, C)), _const((1, C))],
               [_rows(tm, C), _const((1, C)), _const((1, C)), _const((1, C))],
               [_sds((T, C), F32)] + [_sds((1, C), F32)] * 3, sem=("arbitrary",))(ds, cv, ln_g, ln_b)


def dwconv_bwd(dcv, h, w_dw, taps):
    T, C = dcv.shape
    tq = _tile(T)
    nh = tq // HALO
    nblk = T // tq
    off = HALO - (taps - 1)

    def body(d_ref, dn_ref, a_ref, g_ref, ap_ref, gp_ref, w_ref, dh_ref, dw_ref, dbi_ref, su, sus, sd, sds, wb):
        i = pl.program_id(0)
        su[HALO:HALO + tq, :] = a_ref[...].astype(F32) * _sigmoid(g_ref[...].astype(F32))
        up = ap_ref[...].astype(F32) * _sigmoid(gp_ref[...].astype(F32))
        su[0:HALO, :] = jnp.where(i > 0, up, 0.0)
        sd[0:tq, :] = d_ref[...]
        sd[tq:tq + HALO, :] = jnp.where(i < nblk - 1, dn_ref[...], 0.0)
        _phases(su, sus)
        _phases(sd, sds)

        @pl.when(i == 0)
        def _():
            dw_ref[...] = jnp.zeros_like(dw_ref)

        for j in range(taps):
            dw_ref[j:j + 1, :] += jnp.sum(d_ref[...] * _tap(su, sus, off + j, tq), axis=0, keepdims=True)
        sa = jnp.zeros((1, C), F32)
        sb = jnp.zeros((1, C), F32)
        _spread(w_ref, wb, taps)
        for r in range(tq // CONV_ROWS):
            rows = slice(r * CONV_ROWS, (r + 1) * CONV_ROWS)
            dus = [wb[0] * _tap(sd, sds, taps - 1 + r * CONV_ROWS + 8 * k, 8) for k in range(CONV_ROWS // 8)]
            for j in range(1, taps):
                wj = wb[j]
                dus = [acc + wj * _tap(sd, sds, taps - 1 - j + r * CONV_ROWS + 8 * k, 8) for k, acc in enumerate(dus)]
            du = jnp.concatenate(dus, axis=0)
            a = a_ref[rows, :].astype(F32)
            sg = _sigmoid(g_ref[rows, :].astype(F32))
            da = du * sg
            dgt = du * a * sg * (1.0 - sg)
            dh_ref[rows, 0:C] = da.astype(BF16)
            dh_ref[rows, C:2 * C] = dgt.astype(BF16)
            sa = sa + jnp.sum(da, axis=0, keepdims=True)
            sb = sb + jnp.sum(dgt, axis=0, keepdims=True)
        first = i == 0
        _acc_rows(dbi_ref.at[:, 0:C], sa, first)
        _acc_rows(dbi_ref.at[:, C:2 * C], sb, first)

    prev = lambda col: pl.BlockSpec((HALO, C), lambda i: (jnp.maximum(i * nh - 1, 0), col))
    nxt = pl.BlockSpec((HALO, C), lambda i: (jnp.minimum((i + 1) * nh, T // HALO - 1), 0))
    cur = lambda col: pl.BlockSpec((tq, C), lambda i: (i, col))
    return _pc(body, "dwconv_bwd", (nblk,),
               [cur(0), nxt, cur(0), cur(1), prev(0), prev(1), _const((HALO, C))],
               [_rows(tq, 2 * C), _const((HALO, C)), _const((1, 2 * C))],
               [_sds((T, 2 * C), BF16), _sds((HALO, C), F32), _sds((1, 2 * C), F32)],
               scratch=[pltpu.VMEM((HALO + tq, C), F32), pltpu.VMEM((7, HALO + tq, C), F32),
                        pltpu.VMEM((HALO + tq, C), F32), pltpu.VMEM((7, HALO + tq, C), F32), pltpu.VMEM((taps, 8, C), F32)],
               sem=("arbitrary",))(dcv, dcv, h, h, h, h, w_dw)


def conv_in_bwd(dh, dpre_mix, w_in, alpha):
    T, D = dpre_mix.shape
    nw = w_in[0].shape[2]
    tm = _tile(T, 512)

    def body(dh_ref, dp_ref, w_ref, dx_ref):
        acc = alpha * dp_ref[...]
        for j in range(NS):
            acc = acc + lax.dot_general(dh_ref[:, j * nw:(j + 1) * nw], w_ref[j], NT, preferred_element_type=F32)
        dx_ref[...] = acc

    return _pc(body, "conv_in_bwd", (T // tm,), [_rows(tm, NS * nw), _rows(tm, D), _wspec(w_in)], _rows(tm, D),
               _sds((T, D), F32), sem=("parallel",))(dh, dpre_mix, w_in[0])


def wgrad(a, b, row_sharded, name, into):
    prev, out_shape, off = into
    layer = None
    if isinstance(a, tuple):
        layer, a = a
    T, Ka = a.shape[-2:]
    Nb = b.shape[1]
    tt = min(4096, T)
    nt = T // tt
    ka, tn = min(Ka, 1024), min(Nb, 1024)
    if row_sharded:
        sr = Ka // NS
        spb = max(ka // sr, 1)
        rb = ka // spb
        assert out_shape[2] == Nb and off % rb == 0
        out_spec = pl.BlockSpec((spb, rb, tn), lambda i, j, t: (i, off // rb, j))
    else:
        sc = Nb // NS
        spb = max(tn // sc, 1)
        rb = ka
        assert out_shape[2] == sc and off % ka == 0
        out_spec = pl.BlockSpec((spb, ka, tn // spb), lambda i, j, t: (j, off // ka + i, 0))

    def body(a_ref, b_ref, *rest):
        o_ref, acc = rest[-2:]
        t = pl.program_id(2)
        av = a_ref[...]
        if av.dtype != BF16:
            av = av.astype(BF16)
        d = lax.dot_general(av, b_ref[...], TN, preferred_element_type=F32)

        @pl.when(t == 0)
        def _():
            acc[...] = d

        @pl.when(t > 0)
        def _():
            acc[...] += d

        @pl.when(t == nt - 1)
        def _():
            for s in range(spb):
                if row_sharded:
                    o_ref[s] = acc[s * rb:(s + 1) * rb, :].astype(BF16)
                else:
                    o_ref[s] = acc[:, s * (tn // spb):(s + 1) * (tn // spb)].astype(BF16)

    a_spec = (pl.BlockSpec((tt, ka), lambda i, j, t: (t, i)) if layer is None
              else pl.BlockSpec((None, tt, ka), lambda i, j, t: (layer, t, i)))
    ins = [a_spec, pl.BlockSpec((tt, tn), lambda i, j, t: (t, j))]
    args = [a, b]
    kw = {}
    if prev is not None:
        ins.append(ANY)
        args.append(prev)
        kw["input_output_aliases"] = {2: 0}
    return _pc(body, name, (Ka // ka, Nb // tn, nt), ins, out_spec, _sds(out_shape, BF16),
               scratch=[pltpu.VMEM((ka, tn), F32)], sem=("parallel", "parallel", "arbitrary"), **kw)(*args)


def _adamw_math(w, g, m, v):
    c1 = 1.0 - ADAM_B1 ** ADAM_STEP
    c2 = 1.0 - ADAM_B2 ** ADAM_STEP
    mn = ADAM_B1 * m + (1.0 - ADAM_B1) * g
    vn = ADAM_B2 * v + (1.0 - ADAM_B2) * (g * g)
    return -ADAM_LR * ((mn / c1) / (jnp.sqrt(vn / c2) + ADAM_EPS) + ADAM_WD * w), mn, vn


def adamw_layer(w, m, v, layer, gbuf, off, prev, name):
    L, R, W = w.shape
    tr = 256
    assert R % tr == 0 and off % tr == 0

    def body(w_ref, g_ref, m_ref, v_ref, *rest):
        go_ref, d_ref, mo_ref, vo_ref = rest[-4:]
        g = g_ref[...]
        go_ref[...] = g
        d_ref[...], mo_ref[...], vo_ref[...] = _adamw_math(w_ref[...], g, m_ref[...], v_ref[...])

    lay = pl.BlockSpec((None, tr, W), lambda r: (layer, r, 0))
    ins = [lay, pl.BlockSpec((tr, W), lambda r: (off // tr + r, 0)), lay, lay]
    args = [w, gbuf, m, v]
    kw = {}
    if prev is not None:
        ins += [ANY] * 4
        args += list(prev)
        kw["input_output_aliases"] = {4 + k: k for k in range(4)}
    return _pc(body, name, (R // tr,), ins, [lay] * 4, [_sds((L, R, W), F32)] * 4, sem=("parallel",), **kw)(*args)


def adamw_many(ws, gs, ms, vs):
    n = len(ws)

    def body(*refs):
        for k in range(n):
            d, mn, vn = _adamw_math(refs[k][...], refs[n + k][...], refs[2 * n + k][...], refs[3 * n + k][...])
            refs[4 * n + k][...] = d
            refs[5 * n + k][...] = mn
            refs[6 * n + k][...] = vn

    outs = pl.pallas_call(body, name="adamw_small", out_shape=[_sds(a.shape, F32) for a in ws] * 3)(*ws, *gs, *ms, *vs)
    return outs[:n], outs[n:2 * n], outs[2 * n:]


def _rope_tables(T):
    pos = jnp.arange(T, dtype=F32)
    inv_freq = ROPE_THETA ** (-jnp.arange(0, ROPE, 2, dtype=F32) / ROPE)
    ang = pos[:, None] * inv_freq[None, :]
    cos, sin = jnp.cos(ang), jnp.sin(ang)
    pad = HEAD - ROPE
    c = jnp.concatenate([cos, cos, jnp.ones((T, pad), F32)], axis=1)
    s = jnp.concatenate([-sin, sin, jnp.zeros((T, pad), F32)], axis=1)
    return jnp.stack([jnp.tile(c, (1, 128 // HEAD)), jnp.tile(s, (1, 128 // HEAD))])


def _local_step(x, p, target, W, small, lay, hook=None):
    if hook is None:
        hook = lambda stage, after, G, sg=None: None
    T, D = x.shape
    depth = small["mix_ln_g"].shape[0]
    alpha = float((2 * depth) ** 0.25)
    taps = small["taps"]
    row = lambda a, i: a[i:i + 1]
    cs = _rope_tables(T)

    h = conv_in_fwd(x, W["conv_w_in"], small["conv_b_in"])
    cv, s = dwconv_fwd(h, small["conv_w_dw"], small["conv_b_dw"], small["conv_ln_g"], small["conv_ln_b"], taps)
    pre_mix0, x1, x1b = mm_res_ln(s, W["conv_w_out"], x, row(small["mix_ln_g"], 0), row(small["mix_ln_b"], 0), alpha,
                                  small["conv_b_out"], "conv_out_fwd")
    hook("weights1", x1b, None)
    r0 = mlp_up_fwd(x1b, W["mlp_w_up0"], "mlp_up_fwd0")
    pre_mlp0, x2, x2b = mm_res_ln(r0, W["mlp_w_down0"], x1, row(small["mlp_ln_g"], 0), row(small["mlp_ln_b"], 0), alpha,
                                  None, "mlp_down_fwd0")
    x3, x3b, pp0, gl0 = ple_fwd(x2, x2b, p, 0, W["ple_w_proj0"], W["ple_w_gate0"], None, "ple_fwd0")

    hook("weights2", x3b, None)
    q, k, v = qkv_fwd(x3b, W["attn_w_q"], W["kv_w_k"], W["kv_w_v"], cs)
    o = attn_fwd(q, k, v, small["attn_sinks"])
    pre_mix1, x4, x4b = mm_res_ln(o, W["attn_w_o"], x3, row(small["mix_ln_g"], 1), row(small["mix_ln_b"], 1), alpha,
                                  None, "attn_out_fwd")
    r1 = mlp_up_fwd(x4b, W["mlp_w_up1"], "mlp_up_fwd1")
    pre_mlp1, x5, x5b = mm_res_ln(r1, W["mlp_w_down1"], x4, row(small["mlp_ln_g"], 1), row(small["mlp_ln_b"], 1), alpha,
                                  None, "mlp_down_fwd1")
    dx6, loss, pp1, gl1 = ple_fwd(x5, x5b, p, 1, W["ple_w_proj1"], W["ple_w_gate1"], target, "ple_fwd1")

    G, sg = {}, {}
    where = {n: (key, off) for key in lay for n, off, _ in lay[key]}
    rows_of = {key: sum(r for _, _, r in lay[key]) for key in lay}

    def wg(name, a, b, row_sharded):
        key, off = where[name]
        shape = (NS, rows_of[key], W[name][0].shape[2])
        G[key] = wgrad(a, b, row_sharded, "wg_" + name, (G.get(key), shape, off))

    dpp1, dgl1, dx5 = ple_bwd(dx6, pp1, gl1, W["ple_w_gate1"], "ple_bwd1")
    wg("ple_w_proj1", (1, p), dpp1, False)
    wg("ple_w_gate1", x5b, dgl1, True)
    dpre_mlp1, dpre_mlp1b, dm1, g_mlp_g1, g_mlp_b1 = mlp_bwd1(dx5, pre_mlp1, row(small["mlp_ln_g"], 1), r1,
                                                              W["mlp_w_down1"], "mlp_bwd1_1")
    wg("mlp_w_down1", r1, dpre_mlp1b, True)
    wg("mlp_w_up1", x4b, dm1, False)
    dpre_mix1, dpre_mix1b, do, g_mix_g1, g_mix_b1, _ = mlp_bwd2(dpre_mlp1, dm1, W["mlp_w_up1"], alpha, pre_mix1,
                                                                row(small["mix_ln_g"], 1), W["attn_w_o"], "mlp_bwd2_1")
    wg("attn_w_o", o, dpre_mix1b, True)
    dq, dk, dv, dsinks = attn_bwd(q, k, v, do, small["attn_sinks"])
    dqb, dkb, dvb, dx3 = qkv_bwd(dq, dk, dv, dpre_mix1,
                                 W["attn_w_q"], W["kv_w_k"], W["kv_w_v"], cs, alpha)
    wg("attn_w_q", x3b, dqb, True)
    wg("kv_w_k", x3b, dkb, True)
    wg("kv_w_v", x3b, dvb, True)
    hook("grads3", None, G)

    dpp0, dgl0, dx2 = ple_bwd(dx3, pp0, gl0, W["ple_w_gate0"], "ple_bwd0")
    wg("ple_w_proj0", (0, p), dpp0, False)
    wg("ple_w_gate0", x2b, dgl0, True)
    dpre_mlp0, dpre_mlp0b, dm0, g_mlp_g0, g_mlp_b0 = mlp_bwd1(dx2, pre_mlp0, row(small["mlp_ln_g"], 0), r0,
                                                              W["mlp_w_down0"], "mlp_bwd1_0")
    wg("mlp_w_down0", r0, dpre_mlp0b, True)
    wg("mlp_w_up0", x1b, dm0, False)
    hook("grads2", None, G)
    dpre_mix0, dpre_mix0b, dsw, g_mix_g0, g_mix_b0, g_b_out = mlp_bwd2(dpre_mlp0, dm0, W["mlp_w_up0"], alpha, pre_mix0,
                                                                      row(small["mix_ln_g"], 0), W["conv_w_out"],
                                                                      "mlp_bwd2_0")
    wg("conv_w_out", s, dpre_mix0b, True)
    hook("grads1", None, G)
    dcv, g_cln_g, g_cln_b, g_b_dw = conv_mid_bwd(dsw, cv, small["conv_ln_g"], small["conv_ln_b"])
    dh, g_w_dw, g_b_in = dwconv_bwd(dcv, h, small["conv_w_dw"], taps)
    wg("conv_w_in", x, dh, False)

    sg["conv_b_in"] = g_b_in
    sg["conv_w_dw"] = g_w_dw
    sg["conv_b_dw"], sg["conv_ln_g"], sg["conv_ln_b"], sg["conv_b_out"] = g_b_dw, g_cln_g, g_cln_b, g_b_out
    sg["mix_ln_g"] = [g_mix_g0, g_mix_g1]
    sg["mix_ln_b"] = [g_mix_b0, g_mix_b1]
    sg["mlp_ln_g"] = [g_mlp_g0, g_mlp_g1]
    sg["mlp_ln_b"] = [g_mlp_b0, g_mlp_b1]
    sg["attn_sinks"] = dsinks[:, 0][None, :]
    sg["loss"] = loss
    hook("grads0", None, G, sg)
    grad_x = conv_in_bwd(dh, dpre_mix0, W["conv_w_in"], alpha)
    return loss, grad_x, G, sg


BUFFERS = (("b0", ("conv_w_in",)), ("a0", ("conv_w_out",)),
           ("a1", ("mlp_w_up0", "mlp_w_down0", "ple_w_gate0")), ("c1", ("ple_w_proj0",)),
           ("a2", ("mlp_w_up1", "mlp_w_down1", "ple_w_gate1", "attn_w_q", "attn_w_o")),
           ("c2", ("kv_w_k", "kv_w_v", "ple_w_proj1")))
GROUPS = (("b0", "a0"), ("a1", "c1"), ("a2", "c2"))
REDUCED = (("b0",), ("a0",), ("a1", "c1"), ("a2", "c2"))
ROW_SHARDED = {"mlp_w_down0", "mlp_w_down1", "ple_w_gate0", "ple_w_gate1", "conv_w_out", "attn_w_q", "attn_w_o", "kv_w_k",
               "kv_w_v"}


def _split_layers(weights):
    out = {"conv_w_in": weights["conv_w_in"][0], "conv_w_out": weights["conv_w_out"][0],
           "attn_w_q": weights["attn_w_q"][0], "attn_w_o": weights["attn_w_o"][0],
           "kv_w_k": weights["kv_w_k"], "kv_w_v": weights["kv_w_v"]}
    for n in ("mlp_w_up", "mlp_w_down", "ple_w_proj", "ple_w_gate"):
        for i in range(weights[n].shape[0]):
            out[n + str(i)] = weights[n][i]
    return out


def _layout(shards):
    lay = {}
    for key, names in BUFFERS:
        off, rows = 0, []
        for n in names:
            rows.append((n, off, shards[n].shape[0]))
            off += shards[n].shape[0]
        lay[key] = rows
    return lay


def _place():
    return lax.axis_index("x"), lax.axis_index("y"), lax.axis_index("c")


def _flip(v, f):
    return (v + f) % 2 if f else v


CHIP_FLIPS = ((1, 0), (0, 1), (1, 1))


HBM = pl.BlockSpec(memory_space=pltpu.HBM)
SEM = pl.BlockSpec(memory_space=pltpu.SEMAPHORE)
EFFECT = pltpu.SideEffectType.DATAFLOW_SIDE_EFFECTING


def _half(ref, rows, c):
    return ref.at[pl.ds(pl.multiple_of(c * (rows // 2), 16), rows // 2), :]


def _gather_copies(refs, shapes, whole, send, recv):
    x, y, c = _place()
    me = 2 * x + y
    na = len(refs)
    cps = []
    for d, (fx, fy) in enumerate(CHIP_FLIPS):
        to = (_flip(x, fx), _flip(y, fy), c)
        for k in range(na):
            mine = refs[k].at[me] if k >= na - whole else _half(refs[k].at[me], shapes[k][1], c)
            cps.append(pltpu.make_async_remote_copy(mine, mine, send.at[d * na + k], recv.at[d * na + k], device_id=to,
                                                    device_id_type=MESH))
    return cps


def gather_start(bufs, whole, after, name):
    na = len(bufs)
    shapes = [b.shape for b in bufs]
    nsem = len(CHIP_FLIPS) * na

    def body(*refs):
        ins = refs[:na]
        send, recv = refs[-(na + 3)], refs[-(na + 2)]
        token = refs[-1]
        for cp in _gather_copies(ins, shapes, whole, send, recv):
            cp.start()
        token[...] = jnp.zeros_like(token)

    args = [pltpu.with_memory_space_constraint(b, pltpu.HBM) for b in bufs]
    ins = [HBM] * na
    if after is not None:
        args.append(after)
        ins.append(ANY)
    return pl.pallas_call(
        body, name=name, in_specs=ins,
        out_specs=[SEM, SEM] + [HBM] * na + [pl.BlockSpec(memory_space=pltpu.VMEM)],
        out_shape=[pltpu.SemaphoreType.DMA((nsem,)), pltpu.SemaphoreType.DMA((nsem,))]
        + [pltpu.HBM(b.shape, b.dtype) for b in bufs] + [_sds((8, 128), F32)],
        input_output_aliases={k: k + 2 for k in range(na)},
        compiler_params=pltpu.CompilerParams(has_side_effects=EFFECT))(*args)


def gather_wait(send, recv, bufs, whole, after, name):
    na = len(bufs)
    shapes = [b.shape for b in bufs]

    def body(*refs):
        ins = refs[:na]
        send_ref, recv_ref = refs[na], refs[na + 1]
        for cp in _gather_copies(ins, shapes, whole, send_ref, recv_ref):
            cp.wait_send()
            cp.wait_recv()

    return pl.pallas_call(
        body, name=name, in_specs=[HBM] * na + [SEM, SEM, ANY], out_specs=[HBM] * na,
        out_shape=[pltpu.HBM(b.shape, b.dtype) for b in bufs], input_output_aliases={k: k for k in range(na)},
        compiler_params=pltpu.CompilerParams(has_side_effects=EFFECT))(*bufs, send, recv, after)


def sibling_forward(bufs, name):
    nb = len(bufs)

    def body(*refs):
        outs = refs[nb:2 * nb]
        send, recv = refs[2 * nb:]
        x, y, c = _place()
        cps = []
        for d, (fx, fy) in enumerate(CHIP_FLIPS):
            frm = 2 * _flip(x, fx) + _flip(y, fy)
            for k in range(nb):
                theirs = _half(outs[k].at[frm], bufs[k].shape[1], c)
                cps.append(pltpu.make_async_remote_copy(theirs, theirs, send.at[d * nb + k], recv.at[d * nb + k],
                                                        device_id=(x, y, 1 - c), device_id_type=MESH))
        for cp in cps:
            cp.start()
        for cp in cps:
            cp.wait()

    nsem = len(CHIP_FLIPS) * nb
    return pl.pallas_call(
        body, name=name, in_specs=[ANY] * nb, out_specs=[ANY] * nb, out_shape=[_sds(b.shape, b.dtype) for b in bufs],
        input_output_aliases={k: k for k in range(nb)},
        scratch_shapes=[pltpu.SemaphoreType.DMA((nsem,)), pltpu.SemaphoreType.DMA((nsem,))])(*bufs)


def pack_rows(pieces, rows, width, name):
    def body(*refs):
        o_ref = refs[-1]
        o_ref[...] = jnp.zeros_like(o_ref)
        for ref, (a, off) in zip(refs[:-1], pieces):
            o_ref[off:off + a.shape[0], 0:a.shape[1]] = ref[...]

    return pl.pallas_call(body, name=name, out_shape=_sds((rows, width), F32))(*[a for a, _ in pieces])


PEER_FLIPS = tuple((fx, fy, fc) for fx in (0, 1) for fy in (0, 1) for fc in (0, 1) if fx or fy or fc)


def _reduce_copies(parts, zones, pack, send, recv):
    x, y, c = _place()
    nb = len(parts)
    na = nb + (1 if pack is not None else 0)
    cps = []
    for f, (fx, fy, fc) in enumerate(PEER_FLIPS):
        tx, ty, tc = _flip(x, fx), _flip(y, fy), _flip(c, fc)
        for k in range(nb):
            hrows = parts[k].shape[1] // 2
            piece = parts[k].at[2 * tx + ty, pl.ds(pl.multiple_of(tc * hrows, 16), hrows), :]
            cps.append(pltpu.make_async_remote_copy(piece, zones[k].at[f], send.at[f * na + k], recv.at[f * na + k],
                                                    device_id=(tx, ty, tc), device_id_type=MESH))
        if pack is not None:
            mine = pack.at[4 * x + 2 * y + c]
            cps.append(pltpu.make_async_remote_copy(mine, mine, send.at[f * na + nb], recv.at[f * na + nb],
                                                    device_id=(tx, ty, tc), device_id_type=MESH))
    return cps


def reduce_begin(parts, pack, name):
    nb = len(parts)
    zones = [lax.empty((len(PEER_FLIPS), g.shape[1] // 2, g.shape[2]), g.dtype) for g in parts]
    arrs = list(parts) + zones + ([pack] if pack is not None else [])
    na = len(arrs)
    nsem = len(PEER_FLIPS) * (nb + (1 if pack is not None else 0))

    def body(*refs):
        ins = refs[:na]
        send, recv = refs[na], refs[na + 1]
        for cp in _reduce_copies(ins[:nb], ins[nb:2 * nb], ins[2 * nb] if pack is not None else None, send, recv):
            cp.start()
        refs[-1][...] = jnp.zeros_like(refs[-1])

    return pl.pallas_call(
        body, name=name, in_specs=[HBM] * na,
        out_specs=[SEM, SEM] + [HBM] * na + [pl.BlockSpec(memory_space=pltpu.VMEM)],
        out_shape=[pltpu.SemaphoreType.DMA((nsem,)), pltpu.SemaphoreType.DMA((nsem,))]
        + [pltpu.HBM(a.shape, a.dtype) for a in arrs] + [_sds((8, 128), F32)],
        input_output_aliases={k: k + 2 for k in range(na)},
        compiler_params=pltpu.CompilerParams(has_side_effects=EFFECT))(
            *[pltpu.with_memory_space_constraint(a, pltpu.HBM) for a in arrs])


def reduce_end(send, recv, parts, zones, pack, after, name):
    nb = len(parts)
    arrs = list(parts) + list(zones) + ([pack] if pack is not None else [])
    na = len(arrs)

    def body(*refs):
        ins = refs[:na]
        for cp in _reduce_copies(ins[:nb], ins[nb:2 * nb], ins[2 * nb] if pack is not None else None, refs[na], refs[na + 1]):
            cp.wait_send()
            cp.wait_recv()

    return pl.pallas_call(
        body, name=name, in_specs=[HBM] * na + [SEM, SEM, ANY], out_specs=[HBM] * na,
        out_shape=[pltpu.HBM(a.shape, a.dtype) for a in arrs], input_output_aliases={k: k for k in range(na)},
        compiler_params=pltpu.CompilerParams(has_side_effects=EFFECT))(*arrs, send, recv, after)


def sibling_share(halves, name):
    nb = len(halves)

    def body(*refs):
        outs = refs[nb:2 * nb]
        send, recv = refs[2 * nb:]
        x, y, c = _place()
        cps = []
        for k in range(nb):
            hrows = halves[k].shape[0] // 2
            mine = outs[k].at[pl.ds(pl.multiple_of(c * hrows, 8), hrows), :]
            cps.append(pltpu.make_async_remote_copy(mine, mine, send.at[k], recv.at[k], device_id=(x, y, 1 - c),
                                                    device_id_type=MESH))
        for cp in cps:
            cp.start()
        for cp in cps:
            cp.wait()

    return pl.pallas_call(
        body, name=name, in_specs=[ANY] * nb, out_specs=[ANY] * nb,
        out_shape=[_sds(h.shape, h.dtype) for h in halves], input_output_aliases={k: k for k in range(nb)},
        scratch_shapes=[pltpu.SemaphoreType.DMA((nb,)), pltpu.SemaphoreType.DMA((nb,))])(*halves)


def _row_tile(rows):
    for cand in (512, 384, 256, 128, 64, 32, 16):
        if rows % cand == 0:
            return cand
    return rows


def piece_sum(g, z, idx, name):
    _, hrows, W = z.shape
    tr = _row_tile(hrows)
    nrb = hrows // tr

    def body(idx_ref, g_ref, z_ref, o_ref):
        acc = g_ref[...].astype(F32)
        for d in range(z.shape[0]):
            acc = acc + z_ref[d].astype(F32)
        o_ref[...] = acc

    gs = pltpu.PrefetchScalarGridSpec(
        num_scalar_prefetch=1, grid=(nrb,),
        in_specs=[pl.BlockSpec((None, tr, W), lambda i, sc: (sc[0], sc[1] * nrb + i, 0)),
                  pl.BlockSpec((z.shape[0], tr, W), lambda i, sc: (0, i, 0))],
        out_specs=pl.BlockSpec((tr, W), lambda i, sc: (sc[1] * nrb + i, 0)))
    return pl.pallas_call(body, name=name, grid_spec=gs, out_shape=_sds((2 * hrows, W), F32),
                          compiler_params=pltpu.CompilerParams(dimension_semantics=("parallel",),
                                                               vmem_limit_bytes=48 * 2 ** 20))(idx, g, z)


def small_sum(packs):
    n, R, W = packs.shape

    def body(p_ref, o_ref):
        acc = p_ref[0]
        for d in range(1, n):
            acc = acc + p_ref[d]
        o_ref[...] = acc

    return pl.pallas_call(body, name="small_sum", out_shape=_sds((R, W), F32))(packs)


WEIGHTS = ["conv_w_in", "conv_b_in", "conv_w_dw", "conv_b_dw", "conv_ln_g", "conv_ln_b", "conv_w_out", "conv_b_out", "kv_w_k",
           "kv_w_v", "attn_w_q", "attn_sinks", "attn_w_o", "mix_ln_g", "mix_ln_b", "mlp_w_up", "mlp_w_down", "mlp_ln_g",
           "mlp_ln_b", "ple_w_proj", "ple_w_gate"]
BIG = ["conv_w_in", "conv_w_out", "kv_w_k", "kv_w_v", "attn_w_q", "attn_w_o", "mlp_w_up", "mlp_w_down", "ple_w_proj",
       "ple_w_gate"]
SMALL = [n for n in WEIGHTS if n not in BIG]


def _step(x, p, target, w, m, v):
    D = x.shape[-1]
    ds = D // NS
    xq, yq, cq = _place()
    chip = 2 * xq + yq
    idx = jnp.stack([chip, cq]).astype(jnp.int32)

    shards = _split_layers(w)
    lay = _layout(shards)
    taps = w["conv_w_dw"].shape[1]
    small_loc = pack_rows([(w["conv_w_dw"][0], 0), (w["conv_b_dw"], HALO), (w["conv_ln_g"], HALO + 1), (w["conv_ln_b"], HALO + 2),
                           (w["conv_b_out"], HALO + 3), (w["conv_b_in"].reshape(2, ds), HALO + 4)], HALO + 8, ds, "pack_small")
    slot = lambda a: lax.dynamic_update_slice(lax.empty((NS,) + a.shape, a.dtype), a[None], (chip, 0, 0))
    started, token = [], None
    for gi, keys in enumerate(GROUPS):
        bufs = [slot(jnp.concatenate([shards[n].astype(BF16) for n, _, _ in lay[key]], axis=0)) for key in keys]
        if gi == 0:
            bufs.append(slot(small_loc))
        send, recv, *thru, token = gather_start(bufs, 1 if gi == 0 else 0, token, "gather_start%d" % gi)
        started.append((send, recv, thru))
    W = {}

    def arrive(gi, after):
        send, recv, thru = started[gi]
        whole = 1 if gi == 0 else 0
        got = gather_wait(send, recv, thru, whole, after, "gather_wait%d" % gi)
        nk = len(GROUPS[gi])
        for key, buf in zip(GROUPS[gi], sibling_forward(got[:nk], "sibling_forward%d" % gi)):
            for n, off, rows in lay[key]:
                W[n] = (buf, off, rows)
        return got[nk:]

    gs, = arrive(0, token)
    across = lambda rows: gs[:, rows, :].transpose(1, 0, 2).reshape(rows.stop - rows.start, D)
    small = {"taps": taps, "conv_w_dw": across(slice(0, HALO)), "conv_b_dw": across(slice(HALO, HALO + 1)),
             "conv_ln_g": across(slice(HALO + 1, HALO + 2)), "conv_ln_b": across(slice(HALO + 2, HALO + 3)),
             "conv_b_out": across(slice(HALO + 3, HALO + 4)), "conv_b_in": gs[:, HALO + 4:HALO + 6, :].reshape(1, 2 * D),
             "attn_sinks": w["attn_sinks"], "mix_ln_g": w["mix_ln_g"], "mix_ln_b": w["mix_ln_b"],
             "mlp_ln_g": w["mlp_ln_g"], "mlp_ln_b": w["mlp_ln_b"]}

    reducing = {}

    def reduce_start(gi, G, pack):
        nk = len(REDUCED[gi])
        send, recv, *thru, token = reduce_begin([G[key] for key in REDUCED[gi]], pack, "reduce_begin%d" % gi)
        reducing[gi] = (send, recv, thru[:nk], thru[nk:2 * nk], thru[2 * nk] if pack is not None else None)
        _FOLLOW.append(token)

    def small_pack(sg):
        pieces = [(sg["conv_b_in"].reshape(2, D), 0), (sg["conv_w_dw"], 2)]
        r0 = 2 + HALO
        for i, n in enumerate(("conv_b_dw", "conv_ln_g", "conv_ln_b", "conv_b_out")):
            pieces.append((sg[n], r0 + i))
        r0 += 4
        for i, n in enumerate(("mix_ln_g", "mix_ln_b", "mlp_ln_g", "mlp_ln_b")):
            pieces += [(sg[n][0], r0 + 2 * i), (sg[n][1], r0 + 2 * i + 1)]
        pieces += [(sg["attn_sinks"], r0 + 8), (sg["loss"][0:1], r0 + 9)]
        mine = pack_rows(pieces, r0 + 10, D, "pack_small_grads")
        return lax.dynamic_update_slice(lax.empty((8,) + mine.shape, F32), mine[None], (4 * xq + 2 * yq + cq, 0, 0))

    def hook(stage, after, G, sg=None):
        if stage == "weights1":
            arrive(1, after)
        elif stage == "weights2":
            arrive(2, after)
        elif stage == "grads0":
            reduce_start(0, G, small_pack(sg))
        elif stage.startswith("grads"):
            reduce_start(int(stage[5:]), G, None)

    loss, grad_x, G, sg = _local_step(x[0], p[:, 0], target[0], W, small, lay, hook)
    _FOLLOW.clear()
    nsink = w["attn_sinks"].shape[1]

    grads, delta, new_m, new_v = {}, {}, {}, {}
    found = {}

    def finish(groups, after, tag):
        keys, halves, tot = [], [], None
        for gi in groups:
            send, recv, parts, zones, pack = reducing[gi]
            done = reduce_end(send, recv, parts, zones, pack, after, "reduce_end%d" % gi)
            nk = len(REDUCED[gi])
            for key, g_, z_ in zip(REDUCED[gi], done[:nk], done[nk:2 * nk]):
                keys.append(key)
                halves.append(piece_sum(g_, z_, idx, "piece_sum_" + key))
            if pack is not None:
                tot = small_sum(done[2 * nk])
        for key, buf in zip(keys, sibling_share(halves, "sibling_share" + tag)):
            for n, off, _ in lay[key]:
                found[n] = (buf, off)
        return tot

    def big_adamw(names):
        for n in names:
            three = lambda a: a.reshape((-1,) + a.shape[-2:])
            w3, m3, v3 = three(w[n]), three(m[n]), three(v[n])
            outs = None
            for i in range(w3.shape[0]):
                buf, off = found[n + str(i)] if n + str(i) in found else found[n]
                outs = adamw_layer(w3, m3, v3, i, buf, off, outs, "adamw_%s%d" % (n, i))
            grads[n], delta[n], new_m[n], new_v[n] = [a.reshape(w[n].shape) for a in outs]

    last = [n for n, _, _ in lay[REDUCED[0][0]]]
    finish(reversed(range(1, len(REDUCED))), grad_x, "1")
    big_adamw([n for n in BIG if n not in last])
    tot = finish([0], new_v["mlp_w_down"], "0")
    big_adamw(last)
    cols = lambda rows: lax.dynamic_slice(rows, (0, chip * ds), (rows.shape[0], ds))
    grads["conv_b_in"] = lax.dynamic_slice(tot[0:2].reshape(1, 2 * D), (0, chip * 2 * ds), (1, 2 * ds))
    grads["conv_w_dw"] = cols(tot[2:2 + taps])[None]
    r0 = 2 + HALO
    for i, n in enumerate(("conv_b_dw", "conv_ln_g", "conv_ln_b", "conv_b_out")):
        grads[n] = cols(tot[r0 + i:r0 + i + 1])
    r0 += 4
    for i, n in enumerate(("mix_ln_g", "mix_ln_b", "mlp_ln_g", "mlp_ln_b")):
        grads[n] = tot[r0 + 2 * i:r0 + 2 * i + 2]
    grads["attn_sinks"] = tot[r0 + 8:r0 + 9, 0:nsink]

    ds_, ms_, vs_ = adamw_many([w[n] for n in SMALL], [grads[n] for n in SMALL], [m[n] for n in SMALL], [v[n] for n in SMALL])
    for n, d_, m_, v_ in zip(SMALL, ds_, ms_, vs_):
        delta[n], new_m[n], new_v[n] = d_, m_, v_

    total = tot[r0 + 9, 0]
    return (total, grad_x[None], *[grads[n] for n in WEIGHTS], *[delta[n] for n in WEIGHTS], *[new_m[n] for n in WEIGHTS],
            *[new_v[n] for n in WEIGHTS])


def kernel(x, p, conv_w_in, conv_b_in, conv_w_dw, conv_b_dw, conv_ln_g, conv_ln_b, conv_w_out, conv_b_out, kv_w_k, kv_w_v, attn_w_q, attn_sinks, attn_w_o, mix_ln_g, mix_ln_b, mlp_w_up, mlp_w_down, mlp_ln_g, mlp_ln_b, ple_w_proj, ple_w_gate, loss_target, m_conv_w_in, m_conv_b_in, m_conv_w_dw, m_conv_b_dw, m_conv_ln_g, m_conv_ln_b, m_conv_w_out, m_conv_b_out, m_kv_w_k, m_kv_w_v, m_attn_w_q, m_attn_sinks, m_attn_w_o, m_mix_ln_g, m_mix_ln_b, m_mlp_w_up, m_mlp_w_down, m_mlp_ln_g, m_mlp_ln_b, m_ple_w_proj, m_ple_w_gate, v_conv_w_in, v_conv_b_in, v_conv_w_dw, v_conv_b_dw, v_conv_ln_g, v_conv_ln_b, v_conv_w_out, v_conv_b_out, v_kv_w_k, v_kv_w_v, v_attn_w_q, v_attn_sinks, v_attn_w_o, v_mix_ln_g, v_mix_ln_b, v_mlp_w_up, v_mlp_w_down, v_mlp_ln_g, v_mlp_ln_b, v_ple_w_proj, v_ple_w_gate):
    w = dict(zip(WEIGHTS, (conv_w_in, conv_b_in, conv_w_dw, conv_b_dw, conv_ln_g, conv_ln_b, conv_w_out, conv_b_out, kv_w_k,
                           kv_w_v, attn_w_q, attn_sinks, attn_w_o, mix_ln_g, mix_ln_b, mlp_w_up, mlp_w_down, mlp_ln_g, mlp_ln_b,
                           ple_w_proj, ple_w_gate)))
    m = dict(zip(WEIGHTS, (m_conv_w_in, m_conv_b_in, m_conv_w_dw, m_conv_b_dw, m_conv_ln_g, m_conv_ln_b, m_conv_w_out,
                           m_conv_b_out, m_kv_w_k, m_kv_w_v, m_attn_w_q, m_attn_sinks, m_attn_w_o, m_mix_ln_g, m_mix_ln_b,
                           m_mlp_w_up, m_mlp_w_down, m_mlp_ln_g, m_mlp_ln_b, m_ple_w_proj, m_ple_w_gate)))
    v = dict(zip(WEIGHTS, (v_conv_w_in, v_conv_b_in, v_conv_w_dw, v_conv_b_dw, v_conv_ln_g, v_conv_ln_b, v_conv_w_out,
                           v_conv_b_out, v_kv_w_k, v_kv_w_v, v_attn_w_q, v_attn_sinks, v_attn_w_o, v_mix_ln_g, v_mix_ln_b,
                           v_mlp_w_up, v_mlp_w_down, v_mlp_ln_g, v_mlp_ln_b, v_ple_w_proj, v_ple_w_gate)))
    return _step(x, p, loss_target, w, m, v)
```

```python
import functools

import jax
import jax.numpy as jnp
from jax import lax
from jax.experimental import pallas as pl
from jax.experimental.pallas import tpu as pltpu

F32 = jnp.float32
BF16 = jnp.bfloat16
NS = 4
HEAD = 64
BLK = 128
ROPE = 16
ROPE_THETA = 500000.0
LN_EPS = 1e-5
NEG = -1e30
HALO = 32
ADAM_LR, ADAM_B1, ADAM_B2, ADAM_EPS, ADAM_WD, ADAM_STEP = 0.001, 0.9, 0.999, 1e-08, 0.01, 10
MESH = pl.DeviceIdType.MESH
ANY = pl.BlockSpec(memory_space=pl.ANY)
NT = (((1,), (1,)), ((), ()))
TN = (((0,), (0,)), ((), ()))


_FOLLOW = []


def _pc(body, name, grid, in_specs, out_specs, out_shape, scratch=(), sem=None, vmem=56, **kw):
    call = lambda fn, ins: pl.pallas_call(
        fn, name=name, grid=grid, in_specs=ins, out_specs=out_specs, out_shape=out_shape,
        scratch_shapes=list(scratch),
        compiler_params=pltpu.CompilerParams(dimension_semantics=sem, vmem_limit_bytes=vmem * 2 ** 20), **kw)
    if not _FOLLOW:
        return call(body, in_specs)
    extra = list(_FOLLOW)
    _FOLLOW.clear()
    n_in = len(in_specs)

    def ordered(*refs):
        return body(*refs[:n_in], *refs[n_in + len(extra):])

    run = call(ordered, list(in_specs) + [ANY] * len(extra))
    return lambda *args: run(*args, *extra)


def _rows(tm, n):
    return pl.BlockSpec((tm, n), lambda i: (i, 0))


def _const(shape):
    return pl.BlockSpec(shape, lambda *_: (0,) * len(shape))


def _wspec(w):
    buf, off, rows = w
    assert off % rows == 0
    return pl.BlockSpec((NS, rows, buf.shape[2]), lambda *_: (0, off // rows, 0))


def _rows_joined(w_ref):
    n, r, c = w_ref.shape
    return w_ref[...].reshape(n * r, c)


def _sds(shape, dtype):
    return jax.ShapeDtypeStruct(shape, dtype)


def _tile(t, rows=256):
    return min(rows, t)


def _sigmoid(x):
    return 0.5 * jnp.tanh(0.5 * x) + 0.5


def _ln_stats(w):
    mu = jnp.mean(w, axis=-1, keepdims=True)
    xc = w - mu
    var = jnp.mean(xc * xc, axis=-1, keepdims=True)
    rstd = lax.rsqrt(var + LN_EPS)
    return xc * rstd, rstd


def _ln_bwd(dy, w, g):
    xhat, rstd = _ln_stats(w)
    dxhat = dy * g
    m1 = jnp.mean(dxhat, axis=-1, keepdims=True)
    m2 = jnp.mean(dxhat * xhat, axis=-1, keepdims=True)
    dw = rstd * (dxhat - m1 - xhat * m2)
    return dw, jnp.sum(dy * xhat, axis=0, keepdims=True), jnp.sum(dy, axis=0, keepdims=True)


def _acc_rows(ref, val, first):
    @pl.when(first)
    def _():
        ref[...] = val

    @pl.when(jnp.logical_not(first))
    def _():
        ref[...] += val


def conv_in_fwd(xb, w_in, b_in):
    T, D = xb.shape
    nw = w_in[0].shape[2]
    tm = _tile(T, 512)

    def body(x_ref, w_ref, b_ref, h_ref):
        x = x_ref[...].astype(BF16)
        for j in range(NS):
            sl = slice(j * nw, (j + 1) * nw)
            h_ref[:, sl] = (jnp.dot(x, w_ref[j], preferred_element_type=F32) + b_ref[:, sl]).astype(BF16)

    return _pc(body, "conv_in_fwd", (T // tm,), [_rows(tm, D), _wspec(w_in), _const((1, NS * nw))],
               _rows(tm, NS * nw), _sds((T, NS * nw), BF16), sem=("parallel",))(xb, w_in[0], b_in)


CONV_ROWS = 16


def _phases(scr, sh):
    n = scr.shape[0] - 8
    for b in range(1, 8):
        sh[b - 1, 0:n, :] = scr[b:b + n, :]


def _spread(w_ref, wb, taps):
    for j in range(taps):
        wb[j] = jnp.broadcast_to(w_ref[j:j + 1, :], wb.shape[1:])


def _tap(scr, sh, o, n):
    b = o % 8
    return scr[o:o + n, :] if b == 0 else sh[b - 1, o - b:o - b + n, :]


def dwconv_fwd(h, w_dw, b_dw, ln_g, ln_b, taps):
    T = h.shape[0]
    C = h.shape[1] // 2
    tq = _tile(T)
    nh = tq // HALO
    off = HALO - (taps - 1)

    def body(a_ref, g_ref, ap_ref, gp_ref, w_ref, bdw_ref, lg_ref, lb_ref, cv_ref, s_ref, scr, sh, wb):
        i = pl.program_id(0)
        scr[HALO:HALO + tq, :] = a_ref[...].astype(F32) * _sigmoid(g_ref[...].astype(F32))
        up = ap_ref[...].astype(F32) * _sigmoid(gp_ref[...].astype(F32))
        scr[0:HALO, :] = jnp.where(i > 0, up, 0.0)
        _phases(scr, sh)
        _spread(w_ref, wb, taps)
        bias = jnp.broadcast_to(bdw_ref[...], (8, C))
        for r in range(tq // CONV_ROWS):
            accs = [bias] * (CONV_ROWS // 8)
            for j in range(taps):
                wj = wb[j]
                accs = [acc + wj * _tap(scr, sh, off + j + r * CONV_ROWS + 8 * k, 8) for k, acc in enumerate(accs)]
            for k, acc in enumerate(accs):
                cv_ref[r * CONV_ROWS + 8 * k:r * CONV_ROWS + 8 * k + 8, :] = acc
        xhat, _ = _ln_stats(cv_ref[...])
        ln = xhat * lg_ref[...] + lb_ref[...]
        s_ref[...] = (ln * _sigmoid(ln)).astype(BF16)

    prev = lambda col: pl.BlockSpec((HALO, C), lambda i: (jnp.maximum(i * nh - 1, 0), col))
    cur = lambda col: pl.BlockSpec((tq, C), lambda i: (i, col))
    return _pc(body, "dwconv_fwd", (T // tq,),
               [cur(0), cur(1), prev(0), prev(1), _const((HALO, C)), _const((1, C)), _const((1, C)), _const((1, C))],
               [_rows(tq, C), _rows(tq, C)], [_sds((T, C), F32), _sds((T, C), BF16)],
               scratch=[pltpu.VMEM((HALO + tq, C), F32), pltpu.VMEM((7, HALO + tq, C), F32), pltpu.VMEM((taps, 8, C), F32)],
               sem=("parallel",))(h, h, h, h, w_dw, b_dw, ln_g, ln_b)


def mm_res_ln(a, w, res, g, b, alpha, bias, name):
    T, K = a.shape
    ks = K // NS
    D = res.shape[1]
    tm = _tile(T, 512)

    def body(*refs):
        a_ref, w_ref, res_ref, g_ref, b_ref = refs[:5]
        n = 5
        if bias is not None:
            bias_ref = refs[5]
            n = 6
        pre_ref, xo_ref, xb_ref = refs[n:n + 3]
        acc = jnp.dot(a_ref[...], _rows_joined(w_ref), preferred_element_type=F32)
        if bias is not None:
            acc = acc + bias_ref[...]
        pre = alpha * res_ref[...] + acc
        xhat, _ = _ln_stats(pre)
        xo = xhat * g_ref[...] + b_ref[...]
        pre_ref[...] = pre
        xo_ref[...] = xo
        xb_ref[...] = xo.astype(BF16)

    ins = [_rows(tm, K), _wspec(w), _rows(tm, D), _const((1, D)), _const((1, D))]
    args = [a, w[0], res, g, b]
    if bias is not None:
        ins.append(_const((1, D)))
        args.append(bias)
    return _pc(body, name, (T // tm,), ins, [_rows(tm, D)] * 3, [_sds((T, D), F32), _sds((T, D), F32), _sds((T, D), BF16)],
               sem=("parallel",))(*args)


def mlp_up_fwd(xb, w_up, name):
    T, D = xb.shape
    fs = w_up[0].shape[2]
    tm = _tile(T, 512)

    def body(x_ref, w_ref, r_ref):
        x = x_ref[...]
        for j in range(NS):
            m = jnp.maximum(jnp.dot(x, w_ref[j], preferred_element_type=F32), 0.0)
            r_ref[:, j * fs:(j + 1) * fs] = (m * m).astype(BF16)

    return _pc(body, name, (T // tm,), [_rows(tm, D), _wspec(w_up)], _rows(tm, NS * fs), _sds((T, NS * fs), BF16),
               sem=("parallel",))(xb, w_up[0])


def ple_fwd(x, xb, p, layer, w_proj, w_gate, target, name):
    T, D = x.shape
    P = p.shape[2]
    ds = D // NS
    tm = _tile(T, 512)
    last = target is not None

    def body(*refs):
        x_ref, xb_ref, p_ref, wp_ref, wg_ref = refs[:5]
        n = 5
        if last:
            t_ref = refs[5]
            n = 6
        o_ref, o2_ref, pp_ref, gl_ref = refs[n:n + 4]
        gl = jnp.dot(xb_ref[...], _rows_joined(wg_ref), preferred_element_type=F32)
        gl_ref[...] = gl.astype(BF16)
        sg = _sigmoid(gl)
        pb = p_ref[...].astype(BF16)
        sq = jnp.zeros((1, 1), F32)
        for j in range(NS):
            sl = slice(j * ds, (j + 1) * ds)
            pp = jnp.dot(pb, wp_ref[j], preferred_element_type=F32)
            pp_ref[:, sl] = pp.astype(BF16)
            out = x_ref[:, sl] + pp * sg[:, sl]
            if last:
                err = out - t_ref[:, sl]
                o_ref[:, sl] = err * (1.0 / D)
                e2 = jnp.sum(err * err, axis=0, keepdims=True)
                sq = sq + jnp.sum(e2, axis=1, keepdims=True)
            else:
                o_ref[:, sl] = out
                o2_ref[:, sl] = out.astype(BF16)
        if last:
            _acc_rows(o2_ref, jnp.broadcast_to(sq * (0.5 / D), (8, 128)), pl.program_id(0) == 0)

    ins = [_rows(tm, D), _rows(tm, D), pl.BlockSpec((None, tm, P), lambda i: (layer, i, 0)), _wspec(w_proj), _wspec(w_gate)]
    args = [x, xb, p, w_proj[0], w_gate[0]]
    if last:
        ins.append(_rows(tm, D))
        args.append(target)
        outs = [_rows(tm, D), _const((8, 128)), _rows(tm, D), _rows(tm, D)]
        shapes = [_sds((T, D), F32), _sds((8, 128), F32), _sds((T, D), BF16), _sds((T, D), BF16)]
    else:
        outs = [_rows(tm, D)] * 4
        shapes = [_sds((T, D), F32), _sds((T, D), BF16), _sds((T, D), BF16), _sds((T, D), BF16)]
    return _pc(body, name, (T // tm,), ins, outs, shapes, sem=("arbitrary",) if last else ("parallel",))(*args)


def _rope(x, cs_ref, sign):
    c = cs_ref[0]
    s = cs_ref[1] * sign
    lane = lax.broadcasted_iota(jnp.int32, c.shape, 1)
    first = (lane % HEAD) < (ROPE // 2)
    outs = []
    for gq in range(x.shape[1] // 128):
        xg = x[:, gq * 128:(gq + 1) * 128]
        sw = jnp.where(first, pltpu.roll(xg, 128 - ROPE // 2, 1), pltpu.roll(xg, ROPE // 2, 1))
        outs.append(xg * c + sw * s)
    return outs


def qkv_fwd(xb, w_q, w_k, w_v, cs):
    T, D = xb.shape
    ds = D // NS
    HD, KVD = w_q[0].shape[2], w_k[0].shape[2]
    tm = _tile(T, 512)
    scale = 1.0 / (HEAD ** 0.5)

    def body(x_ref, wq_ref, wk_ref, wv_ref, cs_ref, q_ref, k_ref, v_ref):
        def proj(w_ref):
            return jnp.dot(x_ref[...], _rows_joined(w_ref), preferred_element_type=F32)

        for gq, val in enumerate(_rope(proj(wq_ref), cs_ref, 1.0)):
            q_ref[:, gq * 128:(gq + 1) * 128] = (val * scale).astype(BF16)
        for gq, val in enumerate(_rope(proj(wk_ref), cs_ref, 1.0)):
            k_ref[:, gq * 128:(gq + 1) * 128] = val.astype(BF16)
        v_ref[...] = proj(wv_ref).astype(BF16)

    cs_spec = pl.BlockSpec((2, tm, 128), lambda i: (0, i, 0))
    return _pc(body, "qkv_fwd", (T // tm,), [_rows(tm, D), _wspec(w_q), _wspec(w_k), _wspec(w_v), cs_spec],
               [_rows(tm, HD), _rows(tm, KVD), _rows(tm, KVD)],
               [_sds((T, HD), BF16), _sds((T, KVD), BF16), _sds((T, KVD), BF16)], sem=("parallel",))(
                   xb, w_q[0], w_k[0], w_v[0], cs)


def _band_mask(n):
    row = lax.broadcasted_iota(jnp.int32, (BLK, 2 * BLK), 0)
    col = lax.broadcasted_iota(jnp.int32, (BLK, 2 * BLK), 1)
    return (col > row) & (col <= row + BLK) & ((col >= BLK) | (n > 0))


def _head(h):
    return slice(h * HEAD, (h + 1) * HEAD)


def _softmax_sink(s, sink):
    m = jnp.maximum(jnp.max(s, axis=-1, keepdims=True), sink)
    e = jnp.exp(s - m)
    es = jnp.exp(sink - m)
    den = jnp.sum(e, axis=-1, keepdims=True) + es
    inv = 1.0 / den
    return e * inv, es * inv


def attn_fwd(q, k, v, sinks):
    T, HD = q.shape
    KVD = k.shape[1]
    NKV = KVD // HEAD
    G = HD // KVD

    def body(s_ref, q_ref, kc_ref, kp_ref, vc_ref, vp_ref, o_ref):
        valid = _band_mask(pl.program_id(0))
        NH = NKV * G
        k2 = [jnp.concatenate([kp_ref[:, _head(kh)], kc_ref[:, _head(kh)]], axis=0) for kh in range(NKV)]
        v2 = [jnp.concatenate([vp_ref[:, _head(kh)], vc_ref[:, _head(kh)]], axis=0) for kh in range(NKV)]
        sc = [lax.dot_general(q_ref[:, _head(hh)], k2[hh // G], NT, preferred_element_type=F32) for hh in range(NH)]
        pb = [_softmax_sink(jnp.where(valid, s, NEG), s_ref[0, hh])[0].astype(BF16) for hh, s in enumerate(sc)]
        for hh, p in enumerate(pb):
            o_ref[:, _head(hh)] = jnp.dot(p, v2[hh // G], preferred_element_type=F32).astype(BF16)

    cur = lambda n_: pl.BlockSpec((BLK, n_), lambda n: (n, 0))
    prev = lambda n_: pl.BlockSpec((BLK, n_), lambda n: (jnp.maximum(n - 1, 0), 0))
    return _pc(body, "attn_fwd", (T // BLK,),
               [pl.BlockSpec(memory_space=pltpu.SMEM), cur(HD), cur(KVD), prev(KVD), cur(KVD), prev(KVD)],
               cur(HD), _sds((T, HD), BF16), sem=("parallel",))(sinks, q, k, k, v, v)


def ple_bwd(dxo, pp, gl, w_gate, name):
    T, D = dxo.shape
    ds = D // NS
    tm = _tile(T, 512)

    def body(d_ref, pp_ref, gl_ref, wg_ref, dpp_ref, dgl_ref, dx_ref):
        d = d_ref[...]
        sg = _sigmoid(gl_ref[...].astype(F32))
        dpp_ref[...] = (d * sg).astype(BF16)
        dgl = (d * pp_ref[...].astype(F32) * sg * (1.0 - sg)).astype(BF16)
        dgl_ref[...] = dgl
        dx_ref[...] = d + lax.dot_general(dgl, _rows_joined(wg_ref), NT, preferred_element_type=F32)

    return _pc(body, name, (T // tm,), [_rows(tm, D)] * 3 + [_wspec(w_gate)], [_rows(tm, D)] * 3,
               [_sds((T, D), BF16), _sds((T, D), BF16), _sds((T, D), F32)], sem=("parallel",))(dxo, pp, gl, w_gate[0])


def mlp_bwd1(dy, pre, g, r, w_down, name):
    T, D = dy.shape
    fs = w_down[2]
    tm = _tile(T, 512)

    def body(dy_ref, pre_ref, g_ref, r_ref, w_ref, dw_ref, dwb_ref, dm_ref, dg_ref, db_ref):
        dw, dg, db = _ln_bwd(dy_ref[...], pre_ref[...], g_ref[...])
        first = pl.program_id(0) == 0
        _acc_rows(dg_ref, dg, first)
        _acc_rows(db_ref, db, first)
        dwb = dw.astype(BF16)
        dw_ref[...] = dw
        dwb_ref[...] = dwb
        for j in range(NS):
            sl = slice(j * fs, (j + 1) * fs)
            dr = lax.dot_general(dwb, w_ref[j], NT, preferred_element_type=F32)
            dm_ref[:, sl] = (dr * (2.0 * jnp.sqrt(r_ref[:, sl].astype(F32)))).astype(BF16)

    return _pc(body, name, (T // tm,), [_rows(tm, D), _rows(tm, D), _const((1, D)), _rows(tm, NS * fs), _wspec(w_down)],
               [_rows(tm, D), _rows(tm, D), _rows(tm, NS * fs), _const((1, D)), _const((1, D))],
               [_sds((T, D), F32), _sds((T, D), BF16), _sds((T, NS * fs), BF16), _sds((1, D), F32), _sds((1, D), F32)],
               sem=("arbitrary",))(dy, pre, g, r, w_down[0])


def mlp_bwd2(dpre, dm, w_up, alpha, pre_mix, g_mix, w_mix, name):
    T, D = dpre.shape
    fs = w_up[0].shape[2]
    ms = w_mix[2]
    tm = _tile(T, 512)

    def body(dp_ref, dm_ref, wu_ref, pre_ref, g_ref, wm_ref, dw_ref, dwb_ref, do_ref, dg_ref, db_ref, dc_ref):
        dy = alpha * dp_ref[...]
        for j in range(NS):
            dy = dy + lax.dot_general(dm_ref[:, j * fs:(j + 1) * fs], wu_ref[j], NT, preferred_element_type=F32)
        dw, dg, db = _ln_bwd(dy, pre_ref[...], g_ref[...])
        first = pl.program_id(0) == 0
        _acc_rows(dg_ref, dg, first)
        _acc_rows(db_ref, db, first)
        _acc_rows(dc_ref, jnp.sum(dw, axis=0, keepdims=True), first)
        dwb = dw.astype(BF16)
        dw_ref[...] = dw
        dwb_ref[...] = dwb
        do_ref[...] = lax.dot_general(dwb, _rows_joined(wm_ref), NT, preferred_element_type=F32).astype(BF16)

    return _pc(body, name, (T // tm,),
               [_rows(tm, D), _rows(tm, NS * fs), _wspec(w_up), _rows(tm, D), _const((1, D)), _wspec(w_mix)],
               [_rows(tm, D), _rows(tm, D), _rows(tm, NS * ms), _const((1, D)), _const((1, D)), _const((1, D))],
               [_sds((T, D), F32), _sds((T, D), BF16), _sds((T, NS * ms), BF16)] + [_sds((1, D), F32)] * 3,
               sem=("arbitrary",))(dpre, dm, w_up[0], pre_mix, g_mix, w_mix[0])


def attn_bwd(q, k, v, do, sinks):
    T, HD = q.shape
    KVD = k.shape[1]
    NH, NKV = HD // HEAD, KVD // HEAD
    G = NH // NKV
    nb = T // BLK

    def body(s_ref, q_ref, do_ref, kc_ref, kp_ref, vc_ref, vp_ref, dq_ref, dk_ref, dv_ref, ds_ref, ck, cv):
        n = pl.program_id(0)

        @pl.when(n == 0)
        def _():
            ck[...] = jnp.zeros_like(ck)
            cv[...] = jnp.zeros_like(cv)
            ds_ref[...] = jnp.zeros_like(ds_ref)

        @pl.when(n < nb)
        def _():
            valid = _band_mask(n)
            for kh in range(NKV):
                kv = _head(kh)
                k2 = jnp.concatenate([kp_ref[:, kv], kc_ref[:, kv]], axis=0)
                v2 = jnp.concatenate([vp_ref[:, kv], vc_ref[:, kv]], axis=0)
                hs = [kh * G + gq for gq in range(G)]
                qs = [q_ref[:, _head(hh)] for hh in hs]
                dos = [do_ref[:, _head(hh)] for hh in hs]
                sc = [lax.dot_general(qh, k2, NT, preferred_element_type=F32) for qh in qs]
                dp = [lax.dot_general(doh, v2, NT, preferred_element_type=F32) for doh in dos]
                pr = [_softmax_sink(jnp.where(valid, s, NEG), s_ref[0, hh]) for s, hh in zip(sc, hs)]
                delta = [jnp.sum(p * d, axis=-1, keepdims=True) for (p, _), d in zip(pr, dp)]
                dsb = [(p * (d - dl)).astype(BF16) for (p, _), d, dl in zip(pr, dp, delta)]
                pb = [p.astype(BF16) for p, _ in pr]
                for (_, ps), dl, hh in zip(pr, delta, hs):
                    ds_ref[hh:hh + 1, :] += jnp.broadcast_to(-jnp.sum(ps * dl, axis=0, keepdims=True), (1, 128))
                for d, hh in zip(dsb, hs):
                    dq_ref[:, _head(hh)] = jnp.dot(d, k2, preferred_element_type=F32)
                dk2 = lax.dot_general(jnp.concatenate(dsb, axis=0), jnp.concatenate(qs, axis=0), TN,
                                      preferred_element_type=F32)
                dv2 = lax.dot_general(jnp.concatenate(pb, axis=0), jnp.concatenate(dos, axis=0), TN,
                                      preferred_element_type=F32)
                dk_ref[:, kv] = ck[:, kv] + dk2[0:BLK]
                dv_ref[:, kv] = cv[:, kv] + dv2[0:BLK]
                ck[:, kv] = dk2[BLK:2 * BLK]
                cv[:, kv] = dv2[BLK:2 * BLK]

        @pl.when(n == nb)
        def _():
            dk_ref[...] = ck[...]
            dv_ref[...] = cv[...]

    qcur = pl.BlockSpec((BLK, HD), lambda n: (jnp.minimum(n, nb - 1), 0))
    kcur = pl.BlockSpec((BLK, KVD), lambda n: (jnp.minimum(n, nb - 1), 0))
    kprev = pl.BlockSpec((BLK, KVD), lambda n: (jnp.maximum(n - 1, 0), 0))
    return _pc(body, "attn_bwd", (nb + 1,),
               [pl.BlockSpec(memory_space=pltpu.SMEM), qcur, qcur, kcur, kprev, kcur, kprev],
               [qcur, kprev, kprev, _const((NH, 128))],
               [_sds((T, HD), F32), _sds((T, KVD), F32), _sds((T, KVD), F32), _sds((NH, 128), F32)],
               scratch=[pltpu.VMEM((BLK, KVD), F32), pltpu.VMEM((BLK, KVD), F32)],
               sem=("arbitrary",))(sinks, q, do, k, k, v, v)


def qkv_bwd(dq, dk, dv, dpre_mix, w_q, w_k, w_v, cs, alpha):
    T, HD = dq.shape
    KVD = dk.shape[1]
    D = dpre_mix.shape[1]
    ds = D // NS
    tm = _tile(T, 512)
    scale = 1.0 / (HEAD ** 0.5)

    def body(dq_ref, dk_ref, dv_ref, dp_ref, wq_ref, wk_ref, wv_ref, cs_ref, dqb_ref, dkb_ref, dvb_ref, dx_ref):
        for gq, val in enumerate(_rope(dq_ref[...], cs_ref, -1.0)):
            dqb_ref[:, gq * 128:(gq + 1) * 128] = (val * scale).astype(BF16)
        for gq, val in enumerate(_rope(dk_ref[...], cs_ref, -1.0)):
            dkb_ref[:, gq * 128:(gq + 1) * 128] = val.astype(BF16)
        dvb_ref[...] = dv_ref[...].astype(BF16)
        dqb, dkb, dvb = dqb_ref[...], dkb_ref[...], dvb_ref[...]
        dx_ref[...] = (alpha * dp_ref[...]
                       + lax.dot_general(dqb, _rows_joined(wq_ref), NT, preferred_element_type=F32)
                       + lax.dot_general(dkb, _rows_joined(wk_ref), NT, preferred_element_type=F32)
                       + lax.dot_general(dvb, _rows_joined(wv_ref), NT, preferred_element_type=F32))

    cs_spec = pl.BlockSpec((2, tm, 128), lambda i: (0, i, 0))
    return _pc(body, "qkv_bwd", (T // tm,),
               [_rows(tm, HD), _rows(tm, KVD), _rows(tm, KVD), _rows(tm, D), _wspec(w_q), _wspec(w_k), _wspec(w_v), cs_spec],
               [_rows(tm, HD), _rows(tm, KVD), _rows(tm, KVD), _rows(tm, D)],
               [_sds((T, HD), BF16), _sds((T, KVD), BF16), _sds((T, KVD), BF16), _sds((T, D), F32)],
               sem=("parallel",))(dq, dk, dv, dpre_mix, w_q[0], w_k[0], w_v[0], cs)


def conv_mid_bwd(ds, cv, ln_g, ln_b):
    T, C = cv.shape
    tm = _tile(T, 512)

    def body(ds_ref, cv_ref, g_ref, b_ref, dcv_ref, dg_ref, db_ref, dc_ref):
        xhat, _ = _ln_stats(cv_ref[...])
        ln = xhat * g_ref[...] + b_ref[...]
        sg = _sigmoid(ln)
        dl = ds_ref[...].astype(F32) * (sg * (1.0 + ln * (1.0 - sg)))
        dcv, dg, db = _ln_bwd(dl, cv_ref[...], g_ref[...])
        first = pl.program_id(0) == 0
        _acc_rows(dg_ref, dg, first)
        _acc_rows(db_ref, db, first)
        _acc_rows(dc_ref, jnp.sum(dcv, axis=0, keepdims=True), first)
        dcv_ref[...] = dcv

    return _pc(body, "conv_mid_bwd", (T // tm,), [_rows(tm, C), _rows(tm, C), _const((1, C)), _const((1, C))],
               [_rows(tm, C), _const((1, C)), _const((1, C)), _const((1, C))],
               [_sds((T, C), F32)] + [_sds((1, C), F32)] * 3, sem=("arbitrary",))(ds, cv, ln_g, ln_b)


def dwconv_bwd(dcv, h, w_dw, taps):
    T, C = dcv.shape
    tq = _tile(T)
    nh = tq // HALO
    nblk = T // tq
    off = HALO - (taps - 1)

    def body(d_ref, dn_ref, a_ref, g_ref, ap_ref, gp_ref, w_ref, dh_ref, dw_ref, dbi_ref, su, sus, sd, sds, wb):
        i = pl.program_id(0)
        su[HALO:HALO + tq, :] = a_ref[...].astype(F32) * _sigmoid(g_ref[...].astype(F32))
        up = ap_ref[...].astype(F32) * _sigmoid(gp_ref[...].astype(F32))
        su[0:HALO, :] = jnp.where(i > 0, up, 0.0)
        sd[0:tq, :] = d_ref[...]
        sd[tq:tq + HALO, :] = jnp.where(i < nblk - 1, dn_ref[...], 0.0)
        _phases(su, sus)
        _phases(sd, sds)

        @pl.when(i == 0)
        def _():
            dw_ref[...] = jnp.zeros_like(dw_ref)

        for j in range(taps):
            dw_ref[j:j + 1, :] += jnp.sum(d_ref[...] * _tap(su, sus, off + j, tq), axis=0, keepdims=True)
        sa = jnp.zeros((1, C), F32)
        sb = jnp.zeros((1, C), F32)
        _spread(w_ref, wb, taps)
        for r in range(tq // CONV_ROWS):
            rows = slice(r * CONV_ROWS, (r + 1) * CONV_ROWS)
            dus = [wb[0] * _tap(sd, sds, taps - 1 + r * CONV_ROWS + 8 * k, 8) for k in range(CONV_ROWS // 8)]
            for j in range(1, taps):
                wj = wb[j]
                dus = [acc + wj * _tap(sd, sds, taps - 1 - j + r * CONV_ROWS + 8 * k, 8) for k, acc in enumerate(dus)]
            du = jnp.concatenate(dus, axis=0)
            a = a_ref[rows, :].astype(F32)
            sg = _sigmoid(g_ref[rows, :].astype(F32))
            da = du * sg
            dgt = du * a * sg * (1.0 - sg)
            dh_ref[rows, 0:C] = da.astype(BF16)
            dh_ref[rows, C:2 * C] = dgt.astype(BF16)
            sa = sa + jnp.sum(da, axis=0, keepdims=True)
            sb = sb + jnp.sum(dgt, axis=0, keepdims=True)
        first = i == 0
        _acc_rows(dbi_ref.at[:, 0:C], sa, first)
        _acc_rows(dbi_ref.at[:, C:2 * C], sb, first)

    prev = lambda col: pl.BlockSpec((HALO, C), lambda i: (jnp.maximum(i * nh - 1, 0), col))
    nxt = pl.BlockSpec((HALO, C), lambda i: (jnp.minimum((i + 1) * nh, T // HALO - 1), 0))
    cur = lambda col: pl.BlockSpec((tq, C), lambda i: (i, col))
    return _pc(body, "dwconv_bwd", (nblk,),
               [cur(0), nxt, cur(0), cur(1), prev(0), prev(1), _const((HALO, C))],
               [_rows(tq, 2 * C), _const((HALO, C)), _const((1, 2 * C))],
               [_sds((T, 2 * C), BF16), _sds((HALO, C), F32), _sds((1, 2 * C), F32)],
               scratch=[pltpu.VMEM((HALO + tq, C), F32), pltpu.VMEM((7, HALO + tq, C), F32),
                        pltpu.VMEM((HALO + tq, C), F32), pltpu.VMEM((7, HALO + tq, C), F32), pltpu.VMEM((taps, 8, C), F32)],
               sem=("arbitrary",))(dcv, dcv, h, h, h, h, w_dw)


def conv_in_bwd(dh, dpre_mix, w_in, alpha):
    T, D = dpre_mix.shape
    nw = w_in[0].shape[2]
    tm = _tile(T, 512)

    def body(dh_ref, dp_ref, w_ref, dx_ref):
        acc = alpha * dp_ref[...]
        for j in range(NS):
            acc = acc + lax.dot_general(dh_ref[:, j * nw:(j + 1) * nw], w_ref[j], NT, preferred_element_type=F32)
        dx_ref[...] = acc

    return _pc(body, "conv_in_bwd", (T // tm,), [_rows(tm, NS * nw), _rows(tm, D), _wspec(w_in)], _rows(tm, D),
               _sds((T, D), F32), sem=("parallel",))(dh, dpre_mix, w_in[0])


def wgrad(a, b, row_sharded, name, into):
    prev, out_shape, off = into
    layer = None
    if isinstance(a, tuple):
        layer, a = a
    T, Ka = a.shape[-2:]
    Nb = b.shape[1]
    ka, tn = min(Ka, 1024), min(Nb, 1024)
    tt = min(4096 if (Ka // ka) * (Nb // tn) >= 4 else 2048, T)
    nt = T // tt
    if row_sharded:
        sr = Ka // NS
        spb = max(ka // sr, 1)
        rb = ka // spb
        assert out_shape[2] == Nb and off % rb == 0
        out_spec = pl.BlockSpec((spb, rb, tn), lambda i, j, t: (i, off // rb, j))
    else:
        sc = Nb // NS
        spb = max(tn // sc, 1)
        rb = ka
        assert out_shape[2] == sc and off % ka == 0
        out_spec = pl.BlockSpec((spb, ka, tn // spb), lambda i, j, t: (j, off // ka + i, 0))

    def body(a_ref, b_ref, *rest):
        o_ref, acc = rest[-2:]
        t = pl.program_id(2)
        av = a_ref[...]
        if av.dtype != BF16:
            av = av.astype(BF16)
        d = lax.dot_general(av, b_ref[...], TN, preferred_element_type=F32)

        @pl.when(t == 0)
        def _():
            acc[...] = d

        @pl.when(t > 0)
        def _():
            acc[...] += d

        @pl.when(t == nt - 1)
        def _():
            for s in range(spb):
                if row_sharded:
                    o_ref[s] = acc[s * rb:(s + 1) * rb, :].astype(BF16)
                else:
                    o_ref[s] = acc[:, s * (tn // spb):(s + 1) * (tn // spb)].astype(BF16)

    a_spec = (pl.BlockSpec((tt, ka), lambda i, j, t: (t, i)) if layer is None
              else pl.BlockSpec((None, tt, ka), lambda i, j, t: (layer, t, i)))
    ins = [a_spec, pl.BlockSpec((tt, tn), lambda i, j, t: (t, j))]
    args = [a, b]
    kw = {}
    if prev is not None:
        ins.append(ANY)
        args.append(prev)
        kw["input_output_aliases"] = {2: 0}
    return _pc(body, name, (Ka // ka, Nb // tn, nt), ins, out_spec, _sds(out_shape, BF16),
               scratch=[pltpu.VMEM((ka, tn), F32)], sem=("parallel", "parallel", "arbitrary"), **kw)(*args)


def _adamw_math(w, g, m, v):
    c1 = 1.0 - ADAM_B1 ** ADAM_STEP
    c2 = 1.0 - ADAM_B2 ** ADAM_STEP
    mn = ADAM_B1 * m + (1.0 - ADAM_B1) * g
    vn = ADAM_B2 * v + (1.0 - ADAM_B2) * (g * g)
    return -ADAM_LR * ((mn / c1) / (jnp.sqrt(vn / c2) + ADAM_EPS) + ADAM_WD * w), mn, vn


def adamw_layer(w, m, v, layer, gbuf, off, prev, name):
    L, R, W = w.shape
    tr = 256
    assert R % tr == 0 and off % tr == 0

    def body(w_ref, g_ref, m_ref, v_ref, *rest):
        go_ref, d_ref, mo_ref, vo_ref = rest[-4:]
        g = g_ref[...]
        go_ref[...] = g
        d_ref[...], mo_ref[...], vo_ref[...] = _adamw_math(w_ref[...], g, m_ref[...], v_ref[...])

    lay = pl.BlockSpec((None, tr, W), lambda r: (layer, r, 0))
    ins = [lay, pl.BlockSpec((tr, W), lambda r: (off // tr + r, 0)), lay, lay]
    args = [w, gbuf, m, v]
    kw = {}
    if prev is not None:
        ins += [ANY] * 4
        args += list(prev)
        kw["input_output_aliases"] = {4 + k: k for k in range(4)}
    return _pc(body, name, (R // tr,), ins, [lay] * 4, [_sds((L, R, W), F32)] * 4, sem=("parallel",), **kw)(*args)


def adamw_many(ws, gs, ms, vs):
    n = len(ws)

    def body(*refs):
        for k in range(n):
            d, mn, vn = _adamw_math(refs[k][...], refs[n + k][...], refs[2 * n + k][...], refs[3 * n + k][...])
            refs[4 * n + k][...] = d
            refs[5 * n + k][...] = mn
            refs[6 * n + k][...] = vn

    outs = pl.pallas_call(body, name="adamw_small", out_shape=[_sds(a.shape, F32) for a in ws] * 3)(*ws, *gs, *ms, *vs)
    return outs[:n], outs[n:2 * n], outs[2 * n:]


def _rope_tables(T):
    pos = jnp.arange(T, dtype=F32)
    inv_freq = ROPE_THETA ** (-jnp.arange(0, ROPE, 2, dtype=F32) / ROPE)
    ang = pos[:, None] * inv_freq[None, :]
    cos, sin = jnp.cos(ang), jnp.sin(ang)
    pad = HEAD - ROPE
    c = jnp.concatenate([cos, cos, jnp.ones((T, pad), F32)], axis=1)
    s = jnp.concatenate([-sin, sin, jnp.zeros((T, pad), F32)], axis=1)
    return jnp.stack([jnp.tile(c, (1, 128 // HEAD)), jnp.tile(s, (1, 128 // HEAD))])


def _local_step(x, p, target, W, small, lay, hook=None):
    if hook is None:
        hook = lambda stage, after, G, sg=None: None
    T, D = x.shape
    depth = small["mix_ln_g"].shape[0]
    alpha = float((2 * depth) ** 0.25)
    taps = small["taps"]
    row = lambda a, i: a[i:i + 1]
    cs = _rope_tables(T)

    h = conv_in_fwd(x, W["conv_w_in"], small["conv_b_in"])
    cv, s = dwconv_fwd(h, small["conv_w_dw"], small["conv_b_dw"], small["conv_ln_g"], small["conv_ln_b"], taps)
    pre_mix0, x1, x1b = mm_res_ln(s, W["conv_w_out"], x, row(small["mix_ln_g"], 0), row(small["mix_ln_b"], 0), alpha,
                                  small["conv_b_out"], "conv_out_fwd")
    hook("weights1", x1b, None)
    r0 = mlp_up_fwd(x1b, W["mlp_w_up0"], "mlp_up_fwd0")
    pre_mlp0, x2, x2b = mm_res_ln(r0, W["mlp_w_down0"], x1, row(small["mlp_ln_g"], 0), row(small["mlp_ln_b"], 0), alpha,
                                  None, "mlp_down_fwd0")
    x3, x3b, pp0, gl0 = ple_fwd(x2, x2b, p, 0, W["ple_w_proj0"], W["ple_w_gate0"], None, "ple_fwd0")

    hook("weights2", x3b, None)
    q, k, v = qkv_fwd(x3b, W["attn_w_q"], W["kv_w_k"], W["kv_w_v"], cs)
    o = attn_fwd(q, k, v, small["attn_sinks"])
    pre_mix1, x4, x4b = mm_res_ln(o, W["attn_w_o"], x3, row(small["mix_ln_g"], 1), row(small["mix_ln_b"], 1), alpha,
                                  None, "attn_out_fwd")
    r1 = mlp_up_fwd(x4b, W["mlp_w_up1"], "mlp_up_fwd1")
    pre_mlp1, x5, x5b = mm_res_ln(r1, W["mlp_w_down1"], x4, row(small["mlp_ln_g"], 1), row(small["mlp_ln_b"], 1), alpha,
                                  None, "mlp_down_fwd1")
    dx6, loss, pp1, gl1 = ple_fwd(x5, x5b, p, 1, W["ple_w_proj1"], W["ple_w_gate1"], target, "ple_fwd1")

    G, sg = {}, {}
    where = {n: (key, off) for key in lay for n, off, _ in lay[key]}
    rows_of = {key: sum(r for _, _, r in lay[key]) for key in lay}

    def wg(name, a, b, row_sharded):
        key, off = where[name]
        shape = (NS, rows_of[key], W[name][0].shape[2])
        G[key] = wgrad(a, b, row_sharded, "wg_" + name, (G.get(key), shape, off))

    dpp1, dgl1, dx5 = ple_bwd(dx6, pp1, gl1, W["ple_w_gate1"], "ple_bwd1")
    wg("ple_w_proj1", (1, p), dpp1, False)
    wg("ple_w_gate1", x5b, dgl1, True)
    dpre_mlp1, dpre_mlp1b, dm1, g_mlp_g1, g_mlp_b1 = mlp_bwd1(dx5, pre_mlp1, row(small["mlp_ln_g"], 1), r1,
                                                              W["mlp_w_down1"], "mlp_bwd1_1")
    wg("mlp_w_down1", r1, dpre_mlp1b, True)
    wg("mlp_w_up1", x4b, dm1, False)
    dpre_mix1, dpre_mix1b, do, g_mix_g1, g_mix_b1, _ = mlp_bwd2(dpre_mlp1, dm1, W["mlp_w_up1"], alpha, pre_mix1,
                                                                row(small["mix_ln_g"], 1), W["attn_w_o"], "mlp_bwd2_1")
    wg("attn_w_o", o, dpre_mix1b, True)
    dq, dk, dv, dsinks = attn_bwd(q, k, v, do, small["attn_sinks"])
    dqb, dkb, dvb, dx3 = qkv_bwd(dq, dk, dv, dpre_mix1,
                                 W["attn_w_q"], W["kv_w_k"], W["kv_w_v"], cs, alpha)
    wg("attn_w_q", x3b, dqb, True)
    wg("kv_w_k", x3b, dkb, True)
    wg("kv_w_v", x3b, dvb, True)
    hook("grads3", None, G)

    dpp0, dgl0, dx2 = ple_bwd(dx3, pp0, gl0, W["ple_w_gate0"], "ple_bwd0")
    wg("ple_w_proj0", (0, p), dpp0, False)
    wg("ple_w_gate0", x2b, dgl0, True)
    dpre_mlp0, dpre_mlp0b, dm0, g_mlp_g0, g_mlp_b0 = mlp_bwd1(dx2, pre_mlp0, row(small["mlp_ln_g"], 0), r0,
                                                              W["mlp_w_down0"], "mlp_bwd1_0")
    wg("mlp_w_down0", r0, dpre_mlp0b, True)
    wg("mlp_w_up0", x1b, dm0, False)
    hook("grads2", None, G)
    dpre_mix0, dpre_mix0b, dsw, g_mix_g0, g_mix_b0, g_b_out = mlp_bwd2(dpre_mlp0, dm0, W["mlp_w_up0"], alpha, pre_mix0,
                                                                      row(small["mix_ln_g"], 0), W["conv_w_out"],
                                                                      "mlp_bwd2_0")
    wg("conv_w_out", s, dpre_mix0b, True)
    hook("grads1", None, G)
    dcv, g_cln_g, g_cln_b, g_b_dw = conv_mid_bwd(dsw, cv, small["conv_ln_g"], small["conv_ln_b"])
    dh, g_w_dw, g_b_in = dwconv_bwd(dcv, h, small["conv_w_dw"], taps)
    wg("conv_w_in", x, dh, False)

    sg["conv_b_in"] = g_b_in
    sg["conv_w_dw"] = g_w_dw
    sg["conv_b_dw"], sg["conv_ln_g"], sg["conv_ln_b"], sg["conv_b_out"] = g_b_dw, g_cln_g, g_cln_b, g_b_out
    sg["mix_ln_g"] = [g_mix_g0, g_mix_g1]
    sg["mix_ln_b"] = [g_mix_b0, g_mix_b1]
    sg["mlp_ln_g"] = [g_mlp_g0, g_mlp_g1]
    sg["mlp_ln_b"] = [g_mlp_b0, g_mlp_b1]
    sg["attn_sinks"] = dsinks[:, 0][None, :]
    sg["loss"] = loss
    hook("grads0", None, G, sg)
    grad_x = conv_in_bwd(dh, dpre_mix0, W["conv_w_in"], alpha)
    return loss, grad_x, G, sg


BUFFERS = (("b0", ("conv_w_in",)), ("a0", ("conv_w_out",)),
           ("a1", ("mlp_w_up0", "mlp_w_down0", "ple_w_gate0")), ("c1", ("ple_w_proj0",)),
           ("a2", ("mlp_w_up1", "mlp_w_down1", "ple_w_gate1", "attn_w_q", "attn_w_o")),
           ("c2", ("kv_w_k", "kv_w_v", "ple_w_proj1")))
GROUPS = (("b0", "a0"), ("a1", "c1"), ("a2", "c2"))
REDUCED = (("b0",), ("a0",), ("a1", "c1"), ("a2", "c2"))
ROW_SHARDED = {"mlp_w_down0", "mlp_w_down1", "ple_w_gate0", "ple_w_gate1", "conv_w_out", "attn_w_q", "attn_w_o", "kv_w_k",
               "kv_w_v"}


def _split_layers(weights):
    out = {"conv_w_in": weights["conv_w_in"][0], "conv_w_out": weights["conv_w_out"][0],
           "attn_w_q": weights["attn_w_q"][0], "attn_w_o": weights["attn_w_o"][0],
           "kv_w_k": weights["kv_w_k"], "kv_w_v": weights["kv_w_v"]}
    for n in ("mlp_w_up", "mlp_w_down", "ple_w_proj", "ple_w_gate"):
        for i in range(weights[n].shape[0]):
            out[n + str(i)] = weights[n][i]
    return out


def _layout(shards):
    lay = {}
    for key, names in BUFFERS:
        off, rows = 0, []
        for n in names:
            rows.append((n, off, shards[n].shape[0]))
            off += shards[n].shape[0]
        lay[key] = rows
    return lay


def _place():
    return lax.axis_index("x"), lax.axis_index("y"), lax.axis_index("c")


def _flip(v, f):
    return (v + f) % 2 if f else v


CHIP_FLIPS = ((1, 0), (0, 1), (1, 1))


HBM = pl.BlockSpec(memory_space=pltpu.HBM)
SEM = pl.BlockSpec(memory_space=pltpu.SEMAPHORE)
EFFECT = pltpu.SideEffectType.DATAFLOW_SIDE_EFFECTING


def _half(ref, rows, c):
    return ref.at[pl.ds(pl.multiple_of(c * (rows // 2), 16), rows // 2), :]


def _gather_copies(refs, shapes, whole, send, recv):
    x, y, c = _place()
    me = 2 * x + y
    na = len(refs)
    cps = []
    for d, (fx, fy) in enumerate(CHIP_FLIPS):
        to = (_flip(x, fx), _flip(y, fy), c)
        for k in range(na):
            mine = refs[k].at[me] if k >= na - whole else _half(refs[k].at[me], shapes[k][1], c)
            cps.append(pltpu.make_async_remote_copy(mine, mine, send.at[d * na + k], recv.at[d * na + k], device_id=to,
                                                    device_id_type=MESH))
    return cps


def gather_start(bufs, whole, after, name):
    na = len(bufs)
    shapes = [b.shape for b in bufs]
    nsem = len(CHIP_FLIPS) * na

    def body(*refs):
        ins = refs[:na]
        send, recv = refs[-(na + 3)], refs[-(na + 2)]
        token = refs[-1]
        for cp in _gather_copies(ins, shapes, whole, send, recv):
            cp.start()
        token[...] = jnp.zeros_like(token)

    args = [pltpu.with_memory_space_constraint(b, pltpu.HBM) for b in bufs]
    ins = [HBM] * na
    if after is not None:
        args.append(after)
        ins.append(ANY)
    return pl.pallas_call(
        body, name=name, in_specs=ins,
        out_specs=[SEM, SEM] + [HBM] * na + [pl.BlockSpec(memory_space=pltpu.VMEM)],
        out_shape=[pltpu.SemaphoreType.DMA((nsem,)), pltpu.SemaphoreType.DMA((nsem,))]
        + [pltpu.HBM(b.shape, b.dtype) for b in bufs] + [_sds((8, 128), F32)],
        input_output_aliases={k: k + 2 for k in range(na)},
        compiler_params=pltpu.CompilerParams(has_side_effects=EFFECT))(*args)


def gather_wait(send, recv, bufs, whole, after, name):
    na = len(bufs)
    shapes = [b.shape for b in bufs]

    def body(*refs):
        ins = refs[:na]
        send_ref, recv_ref = refs[na], refs[na + 1]
        for cp in _gather_copies(ins, shapes, whole, send_ref, recv_ref):
            cp.wait_send()
            cp.wait_recv()

    return pl.pallas_call(
        body, name=name, in_specs=[HBM] * na + [SEM, SEM, ANY], out_specs=[HBM] * na,
        out_shape=[pltpu.HBM(b.shape, b.dtype) for b in bufs], input_output_aliases={k: k for k in range(na)},
        compiler_params=pltpu.CompilerParams(has_side_effects=EFFECT))(*bufs, send, recv, after)


def sibling_forward(bufs, name):
    nb = len(bufs)

    def body(*refs):
        outs = refs[nb:2 * nb]
        send, recv = refs[2 * nb:]
        x, y, c = _place()
        cps = []
        for d, (fx, fy) in enumerate(CHIP_FLIPS):
            frm = 2 * _flip(x, fx) + _flip(y, fy)
            for k in range(nb):
                theirs = _half(outs[k].at[frm], bufs[k].shape[1], c)
                cps.append(pltpu.make_async_remote_copy(theirs, theirs, send.at[d * nb + k], recv.at[d * nb + k],
                                                        device_id=(x, y, 1 - c), device_id_type=MESH))
        for cp in cps:
            cp.start()
        for cp in cps:
            cp.wait()

    nsem = len(CHIP_FLIPS) * nb
    return pl.pallas_call(
        body, name=name, in_specs=[ANY] * nb, out_specs=[ANY] * nb, out_shape=[_sds(b.shape, b.dtype) for b in bufs],
        input_output_aliases={k: k for k in range(nb)},
        scratch_shapes=[pltpu.SemaphoreType.DMA((nsem,)), pltpu.SemaphoreType.DMA((nsem,))])(*bufs)


def pack_rows(pieces, rows, width, name):
    def body(*refs):
        o_ref = refs[-1]
        o_ref[...] = jnp.zeros_like(o_ref)
        for ref, (a, off) in zip(refs[:-1], pieces):
            o_ref[off:off + a.shape[0], 0:a.shape[1]] = ref[...]

    return pl.pallas_call(body, name=name, out_shape=_sds((rows, width), F32))(*[a for a, _ in pieces])


PEER_FLIPS = tuple((fx, fy, fc) for fx in (0, 1) for fy in (0, 1) for fc in (0, 1) if fx or fy or fc)


def _reduce_copies(parts, zones, pack, send, recv):
    x, y, c = _place()
    nb = len(parts)
    na = nb + (1 if pack is not None else 0)
    cps = []
    for f, (fx, fy, fc) in enumerate(PEER_FLIPS):
        tx, ty, tc = _flip(x, fx), _flip(y, fy), _flip(c, fc)
        for k in range(nb):
            hrows = parts[k].shape[1] // 2
            piece = parts[k].at[2 * tx + ty, pl.ds(pl.multiple_of(tc * hrows, 16), hrows), :]
            cps.append(pltpu.make_async_remote_copy(piece, zones[k].at[f], send.at[f * na + k], recv.at[f * na + k],
                                                    device_id=(tx, ty, tc), device_id_type=MESH))
        if pack is not None:
            mine = pack.at[4 * x + 2 * y + c]
            cps.append(pltpu.make_async_remote_copy(mine, mine, send.at[f * na + nb], recv.at[f * na + nb],
                                                    device_id=(tx, ty, tc), device_id_type=MESH))
    return cps


def reduce_begin(parts, pack, name):
    nb = len(parts)
    zones = [lax.empty((len(PEER_FLIPS), g.shape[1] // 2, g.shape[2]), g.dtype) for g in parts]
    arrs = list(parts) + zones + ([pack] if pack is not None else [])
    na = len(arrs)
    nsem = len(PEER_FLIPS) * (nb + (1 if pack is not None else 0))

    def body(*refs):
        ins = refs[:na]
        send, recv = refs[na], refs[na + 1]
        for cp in _reduce_copies(ins[:nb], ins[nb:2 * nb], ins[2 * nb] if pack is not None else None, send, recv):
            cp.start()
        refs[-1][...] = jnp.zeros_like(refs[-1])

    return pl.pallas_call(
        body, name=name, in_specs=[HBM] * na,
        out_specs=[SEM, SEM] + [HBM] * na + [pl.BlockSpec(memory_space=pltpu.VMEM)],
        out_shape=[pltpu.SemaphoreType.DMA((nsem,)), pltpu.SemaphoreType.DMA((nsem,))]
        + [pltpu.HBM(a.shape, a.dtype) for a in arrs] + [_sds((8, 128), F32)],
        input_output_aliases={k: k + 2 for k in range(na)},
        compiler_params=pltpu.CompilerParams(has_side_effects=EFFECT))(
            *[pltpu.with_memory_space_constraint(a, pltpu.HBM) for a in arrs])


def reduce_end(send, recv, parts, zones, pack, after, name):
    nb = len(parts)
    arrs = list(parts) + list(zones) + ([pack] if pack is not None else [])
    na = len(arrs)

    def body(*refs):
        ins = refs[:na]
        for cp in _reduce_copies(ins[:nb], ins[nb:2 * nb], ins[2 * nb] if pack is not None else None, refs[na], refs[na + 1]):
            cp.wait_send()
            cp.wait_recv()

    return pl.pallas_call(
        body, name=name, in_specs=[HBM] * na + [SEM, SEM, ANY], out_specs=[HBM] * na,
        out_shape=[pltpu.HBM(a.shape, a.dtype) for a in arrs], input_output_aliases={k: k for k in range(na)},
        compiler_params=pltpu.CompilerParams(has_side_effects=EFFECT))(*arrs, send, recv, after)


def sibling_share(halves, name):
    nb = len(halves)

    def body(*refs):
        outs = refs[nb:2 * nb]
        send, recv = refs[2 * nb:]
        x, y, c = _place()
        cps = []
        for k in range(nb):
            hrows = halves[k].shape[0] // 2
            mine = outs[k].at[pl.ds(pl.multiple_of(c * hrows, 8), hrows), :]
            cps.append(pltpu.make_async_remote_copy(mine, mine, send.at[k], recv.at[k], device_id=(x, y, 1 - c),
                                                    device_id_type=MESH))
        for cp in cps:
            cp.start()
        for cp in cps:
            cp.wait()

    return pl.pallas_call(
        body, name=name, in_specs=[ANY] * nb, out_specs=[ANY] * nb,
        out_shape=[_sds(h.shape, h.dtype) for h in halves], input_output_aliases={k: k for k in range(nb)},
        scratch_shapes=[pltpu.SemaphoreType.DMA((nb,)), pltpu.SemaphoreType.DMA((nb,))])(*halves)


def _row_tile(rows):
    for cand in (512, 384, 256, 128, 64, 32, 16):
        if rows % cand == 0:
            return cand
    return rows


def piece_sum(g, z, idx, name):
    _, hrows, W = z.shape
    tr = _row_tile(hrows)
    nrb = hrows // tr

    def body(idx_ref, g_ref, z_ref, o_ref):
        acc = g_ref[...].astype(F32)
        for d in range(z.shape[0]):
            acc = acc + z_ref[d].astype(F32)
        o_ref[...] = acc

    gs = pltpu.PrefetchScalarGridSpec(
        num_scalar_prefetch=1, grid=(nrb,),
        in_specs=[pl.BlockSpec((None, tr, W), lambda i, sc: (sc[0], sc[1] * nrb + i, 0)),
                  pl.BlockSpec((z.shape[0], tr, W), lambda i, sc: (0, i, 0))],
        out_specs=pl.BlockSpec((tr, W), lambda i, sc: (sc[1] * nrb + i, 0)))
    return pl.pallas_call(body, name=name, grid_spec=gs, out_shape=_sds((2 * hrows, W), F32),
                          compiler_params=pltpu.CompilerParams(dimension_semantics=("parallel",),
                                                               vmem_limit_bytes=48 * 2 ** 20))(idx, g, z)


def small_sum(packs):
    n, R, W = packs.shape

    def body(p_ref, o_ref):
        acc = p_ref[0]
        for d in range(1, n):
            acc = acc + p_ref[d]
        o_ref[...] = acc

    return pl.pallas_call(body, name="small_sum", out_shape=_sds((R, W), F32))(packs)


WEIGHTS = ["conv_w_in", "conv_b_in", "conv_w_dw", "conv_b_dw", "conv_ln_g", "conv_ln_b", "conv_w_out", "conv_b_out", "kv_w_k",
           "kv_w_v", "attn_w_q", "attn_sinks", "attn_w_o", "mix_ln_g", "mix_ln_b", "mlp_w_up", "mlp_w_down", "mlp_ln_g",
           "mlp_ln_b", "ple_w_proj", "ple_w_gate"]
BIG = ["conv_w_in", "conv_w_out", "kv_w_k", "kv_w_v", "attn_w_q", "attn_w_o", "mlp_w_up", "mlp_w_down", "ple_w_proj",
       "ple_w_gate"]
SMALL = [n for n in WEIGHTS if n not in BIG]


def _step(x, p, target, w, m, v):
    D = x.shape[-1]
    ds = D // NS
    xq, yq, cq = _place()
    chip = 2 * xq + yq
    idx = jnp.stack([chip, cq]).astype(jnp.int32)

    shards = _split_layers(w)
    lay = _layout(shards)
    taps = w["conv_w_dw"].shape[1]
    small_loc = pack_rows([(w["conv_w_dw"][0], 0), (w["conv_b_dw"], HALO), (w["conv_ln_g"], HALO + 1), (w["conv_ln_b"], HALO + 2),
                           (w["conv_b_out"], HALO + 3), (w["conv_b_in"].reshape(2, ds), HALO + 4)], HALO + 8, ds, "pack_small")
    slot = lambda a: lax.dynamic_update_slice(lax.empty((NS,) + a.shape, a.dtype), a[None], (chip, 0, 0))
    started, token = [], None
    for gi, keys in enumerate(GROUPS):
        bufs = [slot(jnp.concatenate([shards[n].astype(BF16) for n, _, _ in lay[key]], axis=0)) for key in keys]
        if gi == 0:
            bufs.append(slot(small_loc))
        send, recv, *thru, token = gather_start(bufs, 1 if gi == 0 else 0, token, "gather_start%d" % gi)
        started.append((send, recv, thru))
    W = {}

    def arrive(gi, after):
        send, recv, thru = started[gi]
        whole = 1 if gi == 0 else 0
        got = gather_wait(send, recv, thru, whole, after, "gather_wait%d" % gi)
        nk = len(GROUPS[gi])
        for key, buf in zip(GROUPS[gi], sibling_forward(got[:nk], "sibling_forward%d" % gi)):
            for n, off, rows in lay[key]:
                W[n] = (buf, off, rows)
        return got[nk:]

    gs, = arrive(0, token)
    across = lambda rows: gs[:, rows, :].transpose(1, 0, 2).reshape(rows.stop - rows.start, D)
    small = {"taps": taps, "conv_w_dw": across(slice(0, HALO)), "conv_b_dw": across(slice(HALO, HALO + 1)),
             "conv_ln_g": across(slice(HALO + 1, HALO + 2)), "conv_ln_b": across(slice(HALO + 2, HALO + 3)),
             "conv_b_out": across(slice(HALO + 3, HALO + 4)), "conv_b_in": gs[:, HALO + 4:HALO + 6, :].reshape(1, 2 * D),
             "attn_sinks": w["attn_sinks"], "mix_ln_g": w["mix_ln_g"], "mix_ln_b": w["mix_ln_b"],
             "mlp_ln_g": w["mlp_ln_g"], "mlp_ln_b": w["mlp_ln_b"]}

    reducing = {}

    def reduce_start(gi, G, pack):
        nk = len(REDUCED[gi])
        send, recv, *thru, token = reduce_begin([G[key] for key in REDUCED[gi]], pack, "reduce_begin%d" % gi)
        reducing[gi] = (send, recv, thru[:nk], thru[nk:2 * nk], thru[2 * nk] if pack is not None else None)
        _FOLLOW.append(token)

    def small_pack(sg):
        pieces = [(sg["conv_b_in"].reshape(2, D), 0), (sg["conv_w_dw"], 2)]
        r0 = 2 + HALO
        for i, n in enumerate(("conv_b_dw", "conv_ln_g", "conv_ln_b", "conv_b_out")):
            pieces.append((sg[n], r0 + i))
        r0 += 4
        for i, n in enumerate(("mix_ln_g", "mix_ln_b", "mlp_ln_g", "mlp_ln_b")):
            pieces += [(sg[n][0], r0 + 2 * i), (sg[n][1], r0 + 2 * i + 1)]
        pieces += [(sg["attn_sinks"], r0 + 8), (sg["loss"][0:1], r0 + 9)]
        mine = pack_rows(pieces, r0 + 10, D, "pack_small_grads")
        return lax.dynamic_update_slice(lax.empty((8,) + mine.shape, F32), mine[None], (4 * xq + 2 * yq + cq, 0, 0))

    def hook(stage, after, G, sg=None):
        if stage == "weights1":
            arrive(1, after)
        elif stage == "weights2":
            arrive(2, after)
        elif stage == "grads0":
            reduce_start(0, G, small_pack(sg))
        elif stage.startswith("grads"):
            reduce_start(int(stage[5:]), G, None)

    loss, grad_x, G, sg = _local_step(x[0], p[:, 0], target[0], W, small, lay, hook)
    _FOLLOW.clear()
    nsink = w["attn_sinks"].shape[1]

    grads, delta, new_m, new_v = {}, {}, {}, {}
    found = {}

    def finish(groups, after, tag):
        keys, halves, tot = [], [], None
        for gi in groups:
            send, recv, parts, zones, pack = reducing[gi]
            done = reduce_end(send, recv, parts, zones, pack, after, "reduce_end%d" % gi)
            nk = len(REDUCED[gi])
            for key, g_, z_ in zip(REDUCED[gi], done[:nk], done[nk:2 * nk]):
                keys.append(key)
                halves.append(piece_sum(g_, z_, idx, "piece_sum_" + key))
            if pack is not None:
                tot = small_sum(done[2 * nk])
        for key, buf in zip(keys, sibling_share(halves, "sibling_share" + tag)):
            for n, off, _ in lay[key]:
                found[n] = (buf, off)
        return tot

    def big_adamw(names):
        for n in names:
            three = lambda a: a.reshape((-1,) + a.shape[-2:])
            w3, m3, v3 = three(w[n]), three(m[n]), three(v[n])
            outs = None
            for i in range(w3.shape[0]):
                buf, off = found[n + str(i)] if n + str(i) in found else found[n]
                outs = adamw_layer(w3, m3, v3, i, buf, off, outs, "adamw_%s%d" % (n, i))
            grads[n], delta[n], new_m[n], new_v[n] = [a.reshape(w[n].shape) for a in outs]

    last = [n for n, _, _ in lay[REDUCED[0][0]]]
    finish(reversed(range(1, len(REDUCED))), grad_x, "1")
    big_adamw([n for n in BIG if n not in last])
    tot = finish([0], new_v["mlp_w_down"], "0")
    big_adamw(last)
    cols = lambda rows: lax.dynamic_slice(rows, (0, chip * ds), (rows.shape[0], ds))
    grads["conv_b_in"] = lax.dynamic_slice(tot[0:2].reshape(1, 2 * D), (0, chip * 2 * ds), (1, 2 * ds))
    grads["conv_w_dw"] = cols(tot[2:2 + taps])[None]
    r0 = 2 + HALO
    for i, n in enumerate(("conv_b_dw", "conv_ln_g", "conv_ln_b", "conv_b_out")):
        grads[n] = cols(tot[r0 + i:r0 + i + 1])
    r0 += 4
    for i, n in enumerate(("mix_ln_g", "mix_ln_b", "mlp_ln_g", "mlp_ln_b")):
        grads[n] = tot[r0 + 2 * i:r0 + 2 * i + 2]
    grads["attn_sinks"] = tot[r0 + 8:r0 + 9, 0:nsink]

    ds_, ms_, vs_ = adamw_many([w[n] for n in SMALL], [grads[n] for n in SMALL], [m[n] for n in SMALL], [v[n] for n in SMALL])
    for n, d_, m_, v_ in zip(SMALL, ds_, ms_, vs_):
        delta[n], new_m[n], new_v[n] = d_, m_, v_

    total = tot[r0 + 9, 0]
    return (total, grad_x[None], *[grads[n] for n in WEIGHTS], *[delta[n] for n in WEIGHTS], *[new_m[n] for n in WEIGHTS],
            *[new_v[n] for n in WEIGHTS])


def kernel(x, p, conv_w_in, conv_b_in, conv_w_dw, conv_b_dw, conv_ln_g, conv_ln_b, conv_w_out, conv_b_out, kv_w_k, kv_w_v, attn_w_q, attn_sinks, attn_w_o, mix_ln_g, mix_ln_b, mlp_w_up, mlp_w_down, mlp_ln_g, mlp_ln_b, ple_w_proj, ple_w_gate, loss_target, m_conv_w_in, m_conv_b_in, m_conv_w_dw, m_conv_b_dw, m_conv_ln_g, m_conv_ln_b, m_conv_w_out, m_conv_b_out, m_kv_w_k, m_kv_w_v, m_attn_w_q, m_attn_sinks, m_attn_w_o, m_mix_ln_g, m_mix_ln_b, m_mlp_w_up, m_mlp_w_down, m_mlp_ln_g, m_mlp_ln_b, m_ple_w_proj, m_ple_w_gate, v_conv_w_in, v_conv_b_in, v_conv_w_dw, v_conv_b_dw, v_conv_ln_g, v_conv_ln_b, v_conv_w_out, v_conv_b_out, v_kv_w_k, v_kv_w_v, v_attn_w_q, v_attn_sinks, v_attn_w_o, v_mix_ln_g, v_mix_ln_b, v_mlp_w_up, v_mlp_w_down, v_mlp_ln_g, v_mlp_ln_b, v_ple_w_proj, v_ple_w_gate):
    w = dict(zip(WEIGHTS, (conv_w_in, conv_b_in, conv_w_dw, conv_b_dw, conv_ln_g, conv_ln_b, conv_w_out, conv_b_out, kv_w_k,
                           kv_w_v, attn_w_q, attn_sinks, attn_w_o, mix_ln_g, mix_ln_b, mlp_w_up, mlp_w_down, mlp_ln_g, mlp_ln_b,
                           ple_w_proj, ple_w_gate)))
    m = dict(zip(WEIGHTS, (m_conv_w_in, m_conv_b_in, m_conv_w_dw, m_conv_b_dw, m_conv_ln_g, m_conv_ln_b, m_conv_w_out,
                           m_conv_b_out, m_kv_w_k, m_kv_w_v, m_attn_w_q, m_attn_sinks, m_attn_w_o, m_mix_ln_g, m_mix_ln_b,
                           m_mlp_w_up, m_mlp_w_down, m_mlp_ln_g, m_mlp_ln_b, m_ple_w_proj, m_ple_w_gate)))
    v = dict(zip(WEIGHTS, (v_conv_w_in, v_conv_b_in, v_conv_w_dw, v_conv_b_dw, v_conv_ln_g, v_conv_ln_b, v_conv_w_out,
                           v_conv_b_out, v_kv_w_k, v_kv_w_v, v_attn_w_q, v_attn_sinks, v_attn_w_o, v_mix_ln_g, v_mix_ln_b,
                           v_mlp_w_up, v_mlp_w_down, v_mlp_ln_g, v_mlp_ln_b, v_ple_w_proj, v_ple_w_gate)))
    return _step(x, p, loss_target, w, m, v)
```

```python
import functools

import jax
import jax.numpy as jnp
from jax import lax
from jax.experimental import pallas as pl
from jax.experimental.pallas import tpu as pltpu

F32 = jnp.float32
BF16 = jnp.bfloat16
NS = 4
HEAD = 64
BLK = 128
ROPE = 16
ROPE_THETA = 500000.0
LN_EPS = 1e-5
NEG = -1e30
KV_PER_STAGE = 1
HALO = 32
ADAM_LR, ADAM_B1, ADAM_B2, ADAM_EPS, ADAM_WD, ADAM_STEP = 0.001, 0.9, 0.999, 1e-08, 0.01, 10
MESH = pl.DeviceIdType.MESH
ANY = pl.BlockSpec(memory_space=pl.ANY)
NT = (((1,), (1,)), ((), ()))
TN = (((0,), (0,)), ((), ()))


_FOLLOW = []


def _pc(body, name, grid, in_specs, out_specs, out_shape, scratch=(), sem=None, vmem=56, **kw):
    call = lambda fn, ins: pl.pallas_call(
        fn, name=name, grid=grid, in_specs=ins, out_specs=out_specs, out_shape=out_shape,
        scratch_shapes=list(scratch),
        compiler_params=pltpu.CompilerParams(dimension_semantics=sem, vmem_limit_bytes=vmem * 2 ** 20), **kw)
    if not _FOLLOW:
        return call(body, in_specs)
    extra = list(_FOLLOW)
    _FOLLOW.clear()
    n_in = len(in_specs)

    def ordered(*refs):
        return body(*refs[:n_in], *refs[n_in + len(extra):])

    run = call(ordered, list(in_specs) + [ANY] * len(extra))
    return lambda *args: run(*args, *extra)


def _rows(tm, n):
    return pl.BlockSpec((tm, n), lambda i: (i, 0))


def _const(shape):
    return pl.BlockSpec(shape, lambda *_: (0,) * len(shape))


def _wspec(w):
    buf, off, rows = w
    assert off % rows == 0
    return pl.BlockSpec((NS, rows, buf.shape[2]), lambda *_: (0, off // rows, 0))


def _rows_joined(w_ref):
    n, r, c = w_ref.shape
    return w_ref[...].reshape(n * r, c)


def _sds(shape, dtype):
    return jax.ShapeDtypeStruct(shape, dtype)


def _tile(t, rows=256):
    return min(rows, t)


def _sigmoid(x):
    return 0.5 * jnp.tanh(0.5 * x) + 0.5


def _ln_stats(w):
    mu = jnp.mean(w, axis=-1, keepdims=True)
    xc = w - mu
    var = jnp.mean(xc * xc, axis=-1, keepdims=True)
    rstd = lax.rsqrt(var + LN_EPS)
    return xc * rstd, rstd


def _ln_bwd(dy, w, g):
    xhat, rstd = _ln_stats(w)
    dxhat = dy * g
    m1 = jnp.mean(dxhat, axis=-1, keepdims=True)
    m2 = jnp.mean(dxhat * xhat, axis=-1, keepdims=True)
    dw = rstd * (dxhat - m1 - xhat * m2)
    return dw, jnp.sum(dy * xhat, axis=0, keepdims=True), jnp.sum(dy, axis=0, keepdims=True)


def _acc_rows(ref, val, first):
    @pl.when(first)
    def _():
        ref[...] = val

    @pl.when(jnp.logical_not(first))
    def _():
        ref[...] += val


def conv_in_fwd(xb, w_in, b_in):
    T, D = xb.shape
    nw = w_in[0].shape[2]
    tm = _tile(T, 512)

    def body(x_ref, w_ref, b_ref, h_ref):
        x = x_ref[...].astype(BF16)
        for j in range(NS):
            sl = slice(j * nw, (j + 1) * nw)
            h_ref[:, sl] = (jnp.dot(x, w_ref[j], preferred_element_type=F32) + b_ref[:, sl]).astype(BF16)

    return _pc(body, "conv_in_fwd", (T // tm,), [_rows(tm, D), _wspec(w_in), _const((1, NS * nw))],
               _rows(tm, NS * nw), _sds((T, NS * nw), BF16), sem=("parallel",))(xb, w_in[0], b_in)


CONV_ROWS = 16


def _phases(scr, sh):
    n = scr.shape[0] - 8
    for b in range(1, 8):
        sh[b - 1, 0:n, :] = scr[b:b + n, :]


def _spread(w_ref, wb, taps):
    for j in range(taps):
        wb[j] = jnp.broadcast_to(w_ref[j:j + 1, :], wb.shape[1:])


def _tap(scr, sh, o, n):
    b = o % 8
    return scr[o:o + n, :] if b == 0 else sh[b - 1, o - b:o - b + n, :]


def dwconv_fwd(h, w_dw, b_dw, ln_g, ln_b, taps):
    T = h.shape[0]
    C = h.shape[1] // 2
    tq = _tile(T)
    nh = tq // HALO
    off = HALO - (taps - 1)

    def body(a_ref, g_ref, ap_ref, gp_ref, w_ref, bdw_ref, lg_ref, lb_ref, cv_ref, s_ref, scr, sh, wb):
        i = pl.program_id(0)
        scr[HALO:HALO + tq, :] = a_ref[...].astype(F32) * _sigmoid(g_ref[...].astype(F32))
        up = ap_ref[...].astype(F32) * _sigmoid(gp_ref[...].astype(F32))
        scr[0:HALO, :] = jnp.where(i > 0, up, 0.0)
        _phases(scr, sh)
        _spread(w_ref, wb, taps)
        bias = jnp.broadcast_to(bdw_ref[...], (8, C))
        for r in range(tq // CONV_ROWS):
            accs = [bias] * (CONV_ROWS // 8)
            for j in range(taps):
                wj = wb[j]
                accs = [acc + wj * _tap(scr, sh, off + j + r * CONV_ROWS + 8 * k, 8) for k, acc in enumerate(accs)]
            for k, acc in enumerate(accs):
                cv_ref[r * CONV_ROWS + 8 * k:r * CONV_ROWS + 8 * k + 8, :] = acc
        xhat, _ = _ln_stats(cv_ref[...])
        ln = xhat * lg_ref[...] + lb_ref[...]
        s_ref[...] = (ln * _sigmoid(ln)).astype(BF16)

    prev = lambda col: pl.BlockSpec((HALO, C), lambda i: (jnp.maximum(i * nh - 1, 0), col))
    cur = lambda col: pl.BlockSpec((tq, C), lambda i: (i, col))
    return _pc(body, "dwconv_fwd", (T // tq,),
               [cur(0), cur(1), prev(0), prev(1), _const((HALO, C)), _const((1, C)), _const((1, C)), _const((1, C))],
               [_rows(tq, C), _rows(tq, C)], [_sds((T, C), F32), _sds((T, C), BF16)],
               scratch=[pltpu.VMEM((HALO + tq, C), F32), pltpu.VMEM((7, HALO + tq, C), F32), pltpu.VMEM((taps, 8, C), F32)],
               sem=("parallel",))(h, h, h, h, w_dw, b_dw, ln_g, ln_b)


def mm_res_ln(a, w, res, g, b, alpha, bias, name):
    T, K = a.shape
    ks = K // NS
    D = res.shape[1]
    tm = _tile(T, 512)

    def body(*refs):
        a_ref, w_ref, res_ref, g_ref, b_ref = refs[:5]
        n = 5
        if bias is not None:
            bias_ref = refs[5]
            n = 6
        pre_ref, xo_ref, xb_ref = refs[n:n + 3]
        acc = jnp.dot(a_ref[...], _rows_joined(w_ref), preferred_element_type=F32)
        if bias is not None:
            acc = acc + bias_ref[...]
        pre = alpha * res_ref[...] + acc
        xhat, _ = _ln_stats(pre)
        xo = xhat * g_ref[...] + b_ref[...]
        pre_ref[...] = pre
        xo_ref[...] = xo
        xb_ref[...] = xo.astype(BF16)

    ins = [_rows(tm, K), _wspec(w), _rows(tm, D), _const((1, D)), _const((1, D))]
    args = [a, w[0], res, g, b]
    if bias is not None:
        ins.append(_const((1, D)))
        args.append(bias)
    return _pc(body, name, (T // tm,), ins, [_rows(tm, D)] * 3, [_sds((T, D), F32), _sds((T, D), F32), _sds((T, D), BF16)],
               sem=("parallel",))(*args)


def mlp_up_fwd(xb, w_up, name):
    T, D = xb.shape
    fs = w_up[0].shape[2]
    tm = _tile(T, 512)

    def body(x_ref, w_ref, r_ref):
        x = x_ref[...]
        for j in range(NS):
            m = jnp.maximum(jnp.dot(x, w_ref[j], preferred_element_type=F32), 0.0)
            r_ref[:, j * fs:(j + 1) * fs] = (m * m).astype(BF16)

    return _pc(body, name, (T // tm,), [_rows(tm, D), _wspec(w_up)], _rows(tm, NS * fs), _sds((T, NS * fs), BF16),
               sem=("parallel",))(xb, w_up[0])


def ple_fwd(x, xb, p, layer, w_proj, w_gate, target, name):
    T, D = x.shape
    P = p.shape[2]
    ds = D // NS
    tm = _tile(T, 512)
    last = target is not None

    def body(*refs):
        x_ref, xb_ref, p_ref, wp_ref, wg_ref = refs[:5]
        n = 5
        if last:
            t_ref = refs[5]
            n = 6
        o_ref, o2_ref, pp_ref, gl_ref = refs[n:n + 4]
        gl = jnp.dot(xb_ref[...], _rows_joined(wg_ref), preferred_element_type=F32)
        gl_ref[...] = gl.astype(BF16)
        sg = _sigmoid(gl)
        pb = p_ref[...].astype(BF16)
        sq = jnp.zeros((1, 1), F32)
        for j in range(NS):
            sl = slice(j * ds, (j + 1) * ds)
            pp = jnp.dot(pb, wp_ref[j], preferred_element_type=F32)
            pp_ref[:, sl] = pp.astype(BF16)
            out = x_ref[:, sl] + pp * sg[:, sl]
            if last:
                err = out - t_ref[:, sl]
                o_ref[:, sl] = err * (1.0 / D)
                e2 = jnp.sum(err * err, axis=0, keepdims=True)
                sq = sq + jnp.sum(e2, axis=1, keepdims=True)
            else:
                o_ref[:, sl] = out
                o2_ref[:, sl] = out.astype(BF16)
        if last:
            _acc_rows(o2_ref, jnp.broadcast_to(sq * (0.5 / D), (8, 128)), pl.program_id(0) == 0)

    ins = [_rows(tm, D), _rows(tm, D), pl.BlockSpec((None, tm, P), lambda i: (layer, i, 0)), _wspec(w_proj), _wspec(w_gate)]
    args = [x, xb, p, w_proj[0], w_gate[0]]
    if last:
        ins.append(_rows(tm, D))
        args.append(target)
        outs = [_rows(tm, D), _const((8, 128)), _rows(tm, D), _rows(tm, D)]
        shapes = [_sds((T, D), F32), _sds((8, 128), F32), _sds((T, D), BF16), _sds((T, D), BF16)]
    else:
        outs = [_rows(tm, D)] * 4
        shapes = [_sds((T, D), F32), _sds((T, D), BF16), _sds((T, D), BF16), _sds((T, D), BF16)]
    return _pc(body, name, (T // tm,), ins, outs, shapes, sem=("arbitrary",) if last else ("parallel",))(*args)


def _rope(x, cs_ref, sign):
    c = cs_ref[0]
    s = cs_ref[1] * sign
    lane = lax.broadcasted_iota(jnp.int32, c.shape, 1)
    first = (lane % HEAD) < (ROPE // 2)
    outs = []
    for gq in range(x.shape[1] // 128):
        xg = x[:, gq * 128:(gq + 1) * 128]
        sw = jnp.where(first, pltpu.roll(xg, 128 - ROPE // 2, 1), pltpu.roll(xg, ROPE // 2, 1))
        outs.append(xg * c + sw * s)
    return outs


def qkv_fwd(xb, w_q, w_k, w_v, cs):
    T, D = xb.shape
    ds = D // NS
    HD, KVD = w_q[0].shape[2], w_k[0].shape[2]
    tm = _tile(T, 512)
    scale = 1.0 / (HEAD ** 0.5)

    def body(x_ref, wq_ref, wk_ref, wv_ref, cs_ref, q_ref, k_ref, v_ref):
        def proj(w_ref):
            return jnp.dot(x_ref[...], _rows_joined(w_ref), preferred_element_type=F32)

        for gq, val in enumerate(_rope(proj(wq_ref), cs_ref, 1.0)):
            q_ref[:, gq * 128:(gq + 1) * 128] = (val * scale).astype(BF16)
        for gq, val in enumerate(_rope(proj(wk_ref), cs_ref, 1.0)):
            k_ref[:, gq * 128:(gq + 1) * 128] = val.astype(BF16)
        v_ref[...] = proj(wv_ref).astype(BF16)

    cs_spec = pl.BlockSpec((2, tm, 128), lambda i: (0, i, 0))
    return _pc(body, "qkv_fwd", (T // tm,), [_rows(tm, D), _wspec(w_q), _wspec(w_k), _wspec(w_v), cs_spec],
               [_rows(tm, HD), _rows(tm, KVD), _rows(tm, KVD)],
               [_sds((T, HD), BF16), _sds((T, KVD), BF16), _sds((T, KVD), BF16)], sem=("parallel",))(
                   xb, w_q[0], w_k[0], w_v[0], cs)


def _band_mask(n):
    row = lax.broadcasted_iota(jnp.int32, (BLK, 2 * BLK), 0)
    col = lax.broadcasted_iota(jnp.int32, (BLK, 2 * BLK), 1)
    return (col > row) & (col <= row + BLK) & ((col >= BLK) | (n > 0))


def _head(h):
    return slice(h * HEAD, (h + 1) * HEAD)


def _softmax_sink(s, sink):
    m = jnp.maximum(jnp.max(s, axis=-1, keepdims=True), sink)
    e = jnp.exp(s - m)
    es = jnp.exp(sink - m)
    den = jnp.sum(e, axis=-1, keepdims=True) + es
    inv = 1.0 / den
    return e * inv, es * inv


def attn_fwd(q, k, v, sinks):
    T, HD = q.shape
    KVD = k.shape[1]
    NKV = KVD // HEAD
    G = HD // KVD

    def body(s_ref, q_ref, kc_ref, kp_ref, vc_ref, vp_ref, o_ref):
        valid = _band_mask(pl.program_id(0))
        NH = NKV * G
        k2 = [jnp.concatenate([kp_ref[:, _head(kh)], kc_ref[:, _head(kh)]], axis=0) for kh in range(NKV)]
        v2 = [jnp.concatenate([vp_ref[:, _head(kh)], vc_ref[:, _head(kh)]], axis=0) for kh in range(NKV)]
        sc = [lax.dot_general(q_ref[:, _head(hh)], k2[hh // G], NT, preferred_element_type=F32) for hh in range(NH)]
        pb = [_softmax_sink(jnp.where(valid, s, NEG), s_ref[0, hh])[0].astype(BF16) for hh, s in enumerate(sc)]
        for hh, p in enumerate(pb):
            o_ref[:, _head(hh)] = jnp.dot(p, v2[hh // G], preferred_element_type=F32).astype(BF16)

    cur = lambda n_: pl.BlockSpec((BLK, n_), lambda n: (n, 0))
    prev = lambda n_: pl.BlockSpec((BLK, n_), lambda n: (jnp.maximum(n - 1, 0), 0))
    return _pc(body, "attn_fwd", (T // BLK,),
               [pl.BlockSpec(memory_space=pltpu.SMEM), cur(HD), cur(KVD), prev(KVD), cur(KVD), prev(KVD)],
               cur(HD), _sds((T, HD), BF16), sem=("parallel",))(sinks, q, k, k, v, v)


def ple_bwd(dxo, pp, gl, w_gate, name):
    T, D = dxo.shape
    ds = D // NS
    tm = _tile(T, 512)

    def body(d_ref, pp_ref, gl_ref, wg_ref, dpp_ref, dgl_ref, dx_ref):
        d = d_ref[...]
        sg = _sigmoid(gl_ref[...].astype(F32))
        dpp_ref[...] = (d * sg).astype(BF16)
        dgl = (d * pp_ref[...].astype(F32) * sg * (1.0 - sg)).astype(BF16)
        dgl_ref[...] = dgl
        dx_ref[...] = d + lax.dot_general(dgl, _rows_joined(wg_ref), NT, preferred_element_type=F32)

    return _pc(body, name, (T // tm,), [_rows(tm, D)] * 3 + [_wspec(w_gate)], [_rows(tm, D)] * 3,
               [_sds((T, D), BF16), _sds((T, D), BF16), _sds((T, D), F32)], sem=("parallel",))(dxo, pp, gl, w_gate[0])


def mlp_bwd1(dy, pre, g, r, w_down, name):
    T, D = dy.shape
    fs = w_down[2]
    tm = _tile(T, 512)

    def body(dy_ref, pre_ref, g_ref, r_ref, w_ref, dw_ref, dwb_ref, dm_ref, dg_ref, db_ref):
        dw, dg, db = _ln_bwd(dy_ref[...], pre_ref[...], g_ref[...])
        first = pl.program_id(0) == 0
        _acc_rows(dg_ref, dg, first)
        _acc_rows(db_ref, db, first)
        dwb = dw.astype(BF16)
        dw_ref[...] = dw
        dwb_ref[...] = dwb
        for j in range(NS):
            sl = slice(j * fs, (j + 1) * fs)
            dr = lax.dot_general(dwb, w_ref[j], NT, preferred_element_type=F32)
            dm_ref[:, sl] = (dr * (2.0 * jnp.sqrt(r_ref[:, sl].astype(F32)))).astype(BF16)

    return _pc(body, name, (T // tm,), [_rows(tm, D), _rows(tm, D), _const((1, D)), _rows(tm, NS * fs), _wspec(w_down)],
               [_rows(tm, D), _rows(tm, D), _rows(tm, NS * fs), _const((1, D)), _const((1, D))],
               [_sds((T, D), F32), _sds((T, D), BF16), _sds((T, NS * fs), BF16), _sds((1, D), F32), _sds((1, D), F32)],
               sem=("arbitrary",))(dy, pre, g, r, w_down[0])


def mlp_bwd2(dpre, dm, w_up, alpha, pre_mix, g_mix, w_mix, name):
    T, D = dpre.shape
    fs = w_up[0].shape[2]
    ms = w_mix[2]
    tm = _tile(T, 512)

    def body(dp_ref, dm_ref, wu_ref, pre_ref, g_ref, wm_ref, dw_ref, dwb_ref, do_ref, dg_ref, db_ref, dc_ref):
        dy = alpha * dp_ref[...]
        for j in range(NS):
            dy = dy + lax.dot_general(dm_ref[:, j * fs:(j + 1) * fs], wu_ref[j], NT, preferred_element_type=F32)
        dw, dg, db = _ln_bwd(dy, pre_ref[...], g_ref[...])
        first = pl.program_id(0) == 0
        _acc_rows(dg_ref, dg, first)
        _acc_rows(db_ref, db, first)
        _acc_rows(dc_ref, jnp.sum(dw, axis=0, keepdims=True), first)
        dwb = dw.astype(BF16)
        dw_ref[...] = dw
        dwb_ref[...] = dwb
        do_ref[...] = lax.dot_general(dwb, _rows_joined(wm_ref), NT, preferred_element_type=F32).astype(BF16)

    return _pc(body, name, (T // tm,),
               [_rows(tm, D), _rows(tm, NS * fs), _wspec(w_up), _rows(tm, D), _const((1, D)), _wspec(w_mix)],
               [_rows(tm, D), _rows(tm, D), _rows(tm, NS * ms), _const((1, D)), _const((1, D)), _const((1, D))],
               [_sds((T, D), F32), _sds((T, D), BF16), _sds((T, NS * ms), BF16)] + [_sds((1, D), F32)] * 3,
               sem=("arbitrary",))(dpre, dm, w_up[0], pre_mix, g_mix, w_mix[0])


def attn_bwd(q, k, v, do, sinks):
    T, HD = q.shape
    KVD = k.shape[1]
    NH, NKV = HD // HEAD, KVD // HEAD
    G = NH // NKV
    nb = T // BLK

    def body(s_ref, q_ref, do_ref, kc_ref, kp_ref, vc_ref, vp_ref, dq_ref, dk_ref, dv_ref, ds_ref, ck, cv):
        n = pl.program_id(0)

        @pl.when(n == 0)
        def _():
            ck[...] = jnp.zeros_like(ck)
            cv[...] = jnp.zeros_like(cv)
            ds_ref[...] = jnp.zeros_like(ds_ref)

        @pl.when(n < nb)
        def _():
            valid = _band_mask(n)
            for k0 in range(0, NKV, KV_PER_STAGE):
                khs = range(k0, min(k0 + KV_PER_STAGE, NKV))
                k2 = {kh: jnp.concatenate([kp_ref[:, _head(kh)], kc_ref[:, _head(kh)]], axis=0) for kh in khs}
                v2 = {kh: jnp.concatenate([vp_ref[:, _head(kh)], vc_ref[:, _head(kh)]], axis=0) for kh in khs}
                hs = [kh * G + gq for kh in khs for gq in range(G)]
                qs = {hh: q_ref[:, _head(hh)] for hh in hs}
                dos = {hh: do_ref[:, _head(hh)] for hh in hs}
                sc = {hh: lax.dot_general(qs[hh], k2[hh // G], NT, preferred_element_type=F32) for hh in hs}
                pr = {hh: _softmax_sink(jnp.where(valid, sc[hh], NEG), s_ref[0, hh]) for hh in hs}
                dp = {hh: lax.dot_general(dos[hh], v2[hh // G], NT, preferred_element_type=F32) for hh in hs}
                delta = {hh: jnp.sum(pr[hh][0] * dp[hh], axis=-1, keepdims=True) for hh in hs}
                dsb = {hh: (pr[hh][0] * (dp[hh] - delta[hh])).astype(BF16) for hh in hs}
                pb = {hh: pr[hh][0].astype(BF16) for hh in hs}
                for hh in hs:
                    ds_ref[hh:hh + 1, :] += jnp.broadcast_to(-jnp.sum(pr[hh][1] * delta[hh], axis=0, keepdims=True), (1, 128))
                for hh in hs:
                    dq_ref[:, _head(hh)] = jnp.dot(dsb[hh], k2[hh // G], preferred_element_type=F32)
                for kh in khs:
                    kv = _head(kh)
                    grp = [kh * G + gq for gq in range(G)]
                    dk2 = lax.dot_general(jnp.concatenate([dsb[hh] for hh in grp], axis=0),
                                          jnp.concatenate([qs[hh] for hh in grp], axis=0), TN, preferred_element_type=F32)
                    dv2 = lax.dot_general(jnp.concatenate([pb[hh] for hh in grp], axis=0),
                                          jnp.concatenate([dos[hh] for hh in grp], axis=0), TN, preferred_element_type=F32)
                    dk_ref[:, kv] = ck[:, kv] + dk2[0:BLK]
                    dv_ref[:, kv] = cv[:, kv] + dv2[0:BLK]
                    ck[:, kv] = dk2[BLK:2 * BLK]
                    cv[:, kv] = dv2[BLK:2 * BLK]

        @pl.when(n == nb)
        def _():
            dk_ref[...] = ck[...]
            dv_ref[...] = cv[...]

    qcur = pl.BlockSpec((BLK, HD), lambda n: (jnp.minimum(n, nb - 1), 0))
    kcur = pl.BlockSpec((BLK, KVD), lambda n: (jnp.minimum(n, nb - 1), 0))
    kprev = pl.BlockSpec((BLK, KVD), lambda n: (jnp.maximum(n - 1, 0), 0))
    return _pc(body, "attn_bwd", (nb + 1,),
               [pl.BlockSpec(memory_space=pltpu.SMEM), qcur, qcur, kcur, kprev, kcur, kprev],
               [qcur, kprev, kprev, _const((NH, 128))],
               [_sds((T, HD), F32), _sds((T, KVD), F32), _sds((T, KVD), F32), _sds((NH, 128), F32)],
               scratch=[pltpu.VMEM((BLK, KVD), F32), pltpu.VMEM((BLK, KVD), F32)],
               sem=("arbitrary",))(sinks, q, do, k, k, v, v)


def qkv_bwd(dq, dk, dv, dpre_mix, w_q, w_k, w_v, cs, alpha):
    T, HD = dq.shape
    KVD = dk.shape[1]
    D = dpre_mix.shape[1]
    ds = D // NS
    tm = _tile(T, 512)
    scale = 1.0 / (HEAD ** 0.5)

    def body(dq_ref, dk_ref, dv_ref, dp_ref, wq_ref, wk_ref, wv_ref, cs_ref, dqb_ref, dkb_ref, dvb_ref, dx_ref):
        for gq, val in enumerate(_rope(dq_ref[...], cs_ref, -1.0)):
            dqb_ref[:, gq * 128:(gq + 1) * 128] = (val * scale).astype(BF16)
        for gq, val in enumerate(_rope(dk_ref[...], cs_ref, -1.0)):
            dkb_ref[:, gq * 128:(gq + 1) * 128] = val.astype(BF16)
        dvb_ref[...] = dv_ref[...].astype(BF16)
        dqb, dkb, dvb = dqb_ref[...], dkb_ref[...], dvb_ref[...]
        dx_ref[...] = (alpha * dp_ref[...]
                       + lax.dot_general(dqb, _rows_joined(wq_ref), NT, preferred_element_type=F32)
                       + lax.dot_general(dkb, _rows_joined(wk_ref), NT, preferred_element_type=F32)
                       + lax.dot_general(dvb, _rows_joined(wv_ref), NT, preferred_element_type=F32))

    cs_spec = pl.BlockSpec((2, tm, 128), lambda i: (0, i, 0))
    return _pc(body, "qkv_bwd", (T // tm,),
               [_rows(tm, HD), _rows(tm, KVD), _rows(tm, KVD), _rows(tm, D), _wspec(w_q), _wspec(w_k), _wspec(w_v), cs_spec],
               [_rows(tm, HD), _rows(tm, KVD), _rows(tm, KVD), _rows(tm, D)],
               [_sds((T, HD), BF16), _sds((T, KVD), BF16), _sds((T, KVD), BF16), _sds((T, D), F32)],
               sem=("parallel",))(dq, dk, dv, dpre_mix, w_q[0], w_k[0], w_v[0], cs)


def conv_mid_bwd(ds, cv, ln_g, ln_b):
    T, C = cv.shape
    tm = _tile(T, 512)

    def body(ds_ref, cv_ref, g_ref, b_ref, dcv_ref, dg_ref, db_ref, dc_ref):
        xhat, _ = _ln_stats(cv_ref[...])
        ln = xhat * g_ref[...] + b_ref[...]
        sg = _sigmoid(ln)
        dl = ds_ref[...].astype(F32) * (sg * (1.0 + ln * (1.0 - sg)))
        dcv, dg, db = _ln_bwd(dl, cv_ref[...], g_ref[...])
        first = pl.program_id(0) == 0
        _acc_rows(dg_ref, dg, first)
        _acc_rows(db_ref, db, first)
        _acc_rows(dc_ref, jnp.sum(dcv, axis=0, keepdims=True), first)
        dcv_ref[...] = dcv

    return _pc(body, "conv_mid_bwd", (T // tm,), [_rows(tm, C), _rows(tm, C), _const((1, C)), _const((1, C))],
               [_rows(tm, C), _const((1, C)), _const((1, C)), _const((1, C))],
               [_sds((T, C), F32)] + [_sds((1, C), F32)] * 3, sem=("arbitrary",))(ds, cv, ln_g, ln_b)


def dwconv_bwd(dcv, h, w_dw, taps):
    T, C = dcv.shape
    tq = _tile(T)
    nh = tq // HALO
    nblk = T // tq
    off = HALO - (taps - 1)

    def body(d_ref, dn_ref, a_ref, g_ref, ap_ref, gp_ref, w_ref, dh_ref, dw_ref, dbi_ref, su, sus, sd, sds, wb):
        i = pl.program_id(0)
        su[HALO:HALO + tq, :] = a_ref[...].astype(F32) * _sigmoid(g_ref[...].astype(F32))
        up = ap_ref[...].astype(F32) * _sigmoid(gp_ref[...].astype(F32))
        su[0:HALO, :] = jnp.where(i > 0, up, 0.0)
        sd[0:tq, :] = d_ref[...]
        sd[tq:tq + HALO, :] = jnp.where(i < nblk - 1, dn_ref[...], 0.0)
        _phases(su, sus)
        _phases(sd, sds)

        @pl.when(i == 0)
        def _():
            dw_ref[...] = jnp.zeros_like(dw_ref)

        for j in range(taps):
            dw_ref[j:j + 1, :] += jnp.sum(d_ref[...] * _tap(su, sus, off + j, tq), axis=0, keepdims=True)
        sa = jnp.zeros((1, C), F32)
        sb = jnp.zeros((1, C), F32)
        _spread(w_ref, wb, taps)
        for r in range(tq // CONV_ROWS):
            rows = slice(r * CONV_ROWS, (r + 1) * CONV_ROWS)
            dus = [wb[0] * _tap(sd, sds, taps - 1 + r * CONV_ROWS + 8 * k, 8) for k in range(CONV_ROWS // 8)]
            for j in range(1, taps):
                wj = wb[j]
                dus = [acc + wj * _tap(sd, sds, taps - 1 - j + r * CONV_ROWS + 8 * k, 8) for k, acc in enumerate(dus)]
            du = jnp.concatenate(dus, axis=0)
            a = a_ref[rows, :].astype(F32)
            sg = _sigmoid(g_ref[rows, :].astype(F32))
            da = du * sg
            dgt = du * a * sg * (1.0 - sg)
            dh_ref[rows, 0:C] = da.astype(BF16)
            dh_ref[rows, C:2 * C] = dgt.astype(BF16)
            sa = sa + jnp.sum(da, axis=0, keepdims=True)
            sb = sb + jnp.sum(dgt, axis=0, keepdims=True)
        first = i == 0
        _acc_rows(dbi_ref.at[:, 0:C], sa, first)
        _acc_rows(dbi_ref.at[:, C:2 * C], sb, first)

    prev = lambda col: pl.BlockSpec((HALO, C), lambda i: (jnp.maximum(i * nh - 1, 0), col))
    nxt = pl.BlockSpec((HALO, C), lambda i: (jnp.minimum((i + 1) * nh, T // HALO - 1), 0))
    cur = lambda col: pl.BlockSpec((tq, C), lambda i: (i, col))
    return _pc(body, "dwconv_bwd", (nblk,),
               [cur(0), nxt, cur(0), cur(1), prev(0), prev(1), _const((HALO, C))],
               [_rows(tq, 2 * C), _const((HALO, C)), _const((1, 2 * C))],
               [_sds((T, 2 * C), BF16), _sds((HALO, C), F32), _sds((1, 2 * C), F32)],
               scratch=[pltpu.VMEM((HALO + tq, C), F32), pltpu.VMEM((7, HALO + tq, C), F32),
                        pltpu.VMEM((HALO + tq, C), F32), pltpu.VMEM((7, HALO + tq, C), F32), pltpu.VMEM((taps, 8, C), F32)],
               sem=("arbitrary",))(dcv, dcv, h, h, h, h, w_dw)


def conv_in_bwd(dh, dpre_mix, w_in, alpha):
    T, D = dpre_mix.shape
    nw = w_in[0].shape[2]
    tm = _tile(T, 512)

    def body(dh_ref, dp_ref, w_ref, dx_ref):
        acc = alpha * dp_ref[...]
        for j in range(NS):
            acc = acc + lax.dot_general(dh_ref[:, j * nw:(j + 1) * nw], w_ref[j], NT, preferred_element_type=F32)
        dx_ref[...] = acc

    return _pc(body, "conv_in_bwd", (T // tm,), [_rows(tm, NS * nw), _rows(tm, D), _wspec(w_in)], _rows(tm, D),
               _sds((T, D), F32), sem=("parallel",))(dh, dpre_mix, w_in[0])


def wgrad(a, b, row_sharded, name, into):
    prev, out_shape, off = into
    layer = None
    if isinstance(a, tuple):
        layer, a = a
    T, Ka = a.shape[-2:]
    Nb = b.shape[1]
    ka, tn = min(Ka, 1024), min(Nb, 1024)
    tt = min(4096 if (Ka // ka) * (Nb // tn) >= 4 else 2048, T)
    nt = T // tt
    if row_sharded:
        sr = Ka // NS
        spb = max(ka // sr, 1)
        rb = ka // spb
        assert out_shape[2] == Nb and off % rb == 0
        out_spec = pl.BlockSpec((spb, rb, tn), lambda i, j, t: (i, off // rb, j))
    else:
        sc = Nb // NS
        spb = max(tn // sc, 1)
        rb = ka
        assert out_shape[2] == sc and off % ka == 0
        out_spec = pl.BlockSpec((spb, ka, tn // spb), lambda i, j, t: (j, off // ka + i, 0))

    def body(a_ref, b_ref, *rest):
        o_ref, acc = rest[-2:]
        t = pl.program_id(2)
        av = a_ref[...]
        if av.dtype != BF16:
            av = av.astype(BF16)
        d = lax.dot_general(av, b_ref[...], TN, preferred_element_type=F32)

        @pl.when(t == 0)
        def _():
            acc[...] = d

        @pl.when(t > 0)
        def _():
            acc[...] += d

        @pl.when(t == nt - 1)
        def _():
            for s in range(spb):
                if row_sharded:
                    o_ref[s] = acc[s * rb:(s + 1) * rb, :].astype(BF16)
                else:
                    o_ref[s] = acc[:, s * (tn // spb):(s + 1) * (tn // spb)].astype(BF16)

    a_spec = (pl.BlockSpec((tt, ka), lambda i, j, t: (t, i)) if layer is None
              else pl.BlockSpec((None, tt, ka), lambda i, j, t: (layer, t, i)))
    ins = [a_spec, pl.BlockSpec((tt, tn), lambda i, j, t: (t, j))]
    args = [a, b]
    kw = {}
    if prev is not None:
        ins.append(ANY)
        args.append(prev)
        kw["input_output_aliases"] = {2: 0}
    return _pc(body, name, (Ka // ka, Nb // tn, nt), ins, out_spec, _sds(out_shape, BF16),
               scratch=[pltpu.VMEM((ka, tn), F32)], sem=("parallel", "parallel", "arbitrary"), **kw)(*args)


def _adamw_math(w, g, m, v):
    c1 = 1.0 - ADAM_B1 ** ADAM_STEP
    c2 = 1.0 - ADAM_B2 ** ADAM_STEP
    mn = ADAM_B1 * m + (1.0 - ADAM_B1) * g
    vn = ADAM_B2 * v + (1.0 - ADAM_B2) * (g * g)
    return -ADAM_LR * ((mn / c1) / (jnp.sqrt(vn / c2) + ADAM_EPS) + ADAM_WD * w), mn, vn


def adamw_layer(w, m, v, layer, gbuf, off, prev, name):
    L, R, W = w.shape
    tr = 256
    assert R % tr == 0 and off % tr == 0

    def body(w_ref, g_ref, m_ref, v_ref, *rest):
        go_ref, d_ref, mo_ref, vo_ref = rest[-4:]
        g = g_ref[...]
        go_ref[...] = g
        d_ref[...], mo_ref[...], vo_ref[...] = _adamw_math(w_ref[...], g, m_ref[...], v_ref[...])

    lay = pl.BlockSpec((None, tr, W), lambda r: (layer, r, 0))
    ins = [lay, pl.BlockSpec((tr, W), lambda r: (off // tr + r, 0)), lay, lay]
    args = [w, gbuf, m, v]
    kw = {}
    if prev is not None:
        ins += [ANY] * 4
        args += list(prev)
        kw["input_output_aliases"] = {4 + k: k for k in range(4)}
    return _pc(body, name, (R // tr,), ins, [lay] * 4, [_sds((L, R, W), F32)] * 4, sem=("parallel",), **kw)(*args)


def adamw_many(ws, gs, ms, vs):
    n = len(ws)

    def body(*refs):
        for k in range(n):
            d, mn, vn = _adamw_math(refs[k][...], refs[n + k][...], refs[2 * n + k][...], refs[3 * n + k][...])
            refs[4 * n + k][...] = d
            refs[5 * n + k][...] = mn
            refs[6 * n + k][...] = vn

    outs = pl.pallas_call(body, name="adamw_small", out_shape=[_sds(a.shape, F32) for a in ws] * 3)(*ws, *gs, *ms, *vs)
    return outs[:n], outs[n:2 * n], outs[2 * n:]


def _rope_tables(T):
    pos = jnp.arange(T, dtype=F32)
    inv_freq = ROPE_THETA ** (-jnp.arange(0, ROPE, 2, dtype=F32) / ROPE)
    ang = pos[:, None] * inv_freq[None, :]
    cos, sin = jnp.cos(ang), jnp.sin(ang)
    pad = HEAD - ROPE
    c = jnp.concatenate([cos, cos, jnp.ones((T, pad), F32)], axis=1)
    s = jnp.concatenate([-sin, sin, jnp.zeros((T, pad), F32)], axis=1)
    return jnp.stack([jnp.tile(c, (1, 128 // HEAD)), jnp.tile(s, (1, 128 // HEAD))])


def _local_step(x, p, target, W, small, lay, hook=None):
    if hook is None:
        hook = lambda stage, after, G, sg=None: None
    T, D = x.shape
    depth = small["mix_ln_g"].shape[0]
    alpha = float((2 * depth) ** 0.25)
    taps = small["taps"]
    row = lambda a, i: a[i:i + 1]
    cs = _rope_tables(T)

    h = conv_in_fwd(x, W["conv_w_in"], small["conv_b_in"])
    cv, s = dwconv_fwd(h, small["conv_w_dw"], small["conv_b_dw"], small["conv_ln_g"], small["conv_ln_b"], taps)
    hook("weights1", s, None)
    pre_mix0, x1, x1b = mm_res_ln(s, W["conv_w_out"], x, row(small["mix_ln_g"], 0), row(small["mix_ln_b"], 0), alpha,
                                  small["conv_b_out"], "conv_out_fwd")
    r0 = mlp_up_fwd(x1b, W["mlp_w_up0"], "mlp_up_fwd0")
    pre_mlp0, x2, x2b = mm_res_ln(r0, W["mlp_w_down0"], x1, row(small["mlp_ln_g"], 0), row(small["mlp_ln_b"], 0), alpha,
                                  None, "mlp_down_fwd0")
    x3, x3b, pp0, gl0 = ple_fwd(x2, x2b, p, 0, W["ple_w_proj0"], W["ple_w_gate0"], None, "ple_fwd0")

    hook("weights2", x3b, None)
    q, k, v = qkv_fwd(x3b, W["attn_w_q"], W["kv_w_k"], W["kv_w_v"], cs)
    o = attn_fwd(q, k, v, small["attn_sinks"])
    pre_mix1, x4, x4b = mm_res_ln(o, W["attn_w_o"], x3, row(small["mix_ln_g"], 1), row(small["mix_ln_b"], 1), alpha,
                                  None, "attn_out_fwd")
    r1 = mlp_up_fwd(x4b, W["mlp_w_up1"], "mlp_up_fwd1")
    pre_mlp1, x5, x5b = mm_res_ln(r1, W["mlp_w_down1"], x4, row(small["mlp_ln_g"], 1), row(small["mlp_ln_b"], 1), alpha,
                                  None, "mlp_down_fwd1")
    dx6, loss, pp1, gl1 = ple_fwd(x5, x5b, p, 1, W["ple_w_proj1"], W["ple_w_gate1"], target, "ple_fwd1")

    G, sg = {}, {}
    where = {n: (key, off) for key in lay for n, off, _ in lay[key]}
    rows_of = {key: sum(r for _, _, r in lay[key]) for key in lay}

    def wg(name, a, b, row_sharded):
        key, off = where[name]
        shape = (NS, rows_of[key], W[name][0].shape[2])
        G[key] = wgrad(a, b, row_sharded, "wg_" + name, (G.get(key), shape, off))

    dpp1, dgl1, dx5 = ple_bwd(dx6, pp1, gl1, W["ple_w_gate1"], "ple_bwd1")
    wg("ple_w_proj1", (1, p), dpp1, False)
    wg("ple_w_gate1", x5b, dgl1, True)
    dpre_mlp1, dpre_mlp1b, dm1, g_mlp_g1, g_mlp_b1 = mlp_bwd1(dx5, pre_mlp1, row(small["mlp_ln_g"], 1), r1,
                                                              W["mlp_w_down1"], "mlp_bwd1_1")
    wg("mlp_w_down1", r1, dpre_mlp1b, True)
    wg("mlp_w_up1", x4b, dm1, False)
    dpre_mix1, dpre_mix1b, do, g_mix_g1, g_mix_b1, _ = mlp_bwd2(dpre_mlp1, dm1, W["mlp_w_up1"], alpha, pre_mix1,
                                                                row(small["mix_ln_g"], 1), W["attn_w_o"], "mlp_bwd2_1")
    wg("attn_w_o", o, dpre_mix1b, True)
    dq, dk, dv, dsinks = attn_bwd(q, k, v, do, small["attn_sinks"])
    dqb, dkb, dvb, dx3 = qkv_bwd(dq, dk, dv, dpre_mix1,
                                 W["attn_w_q"], W["kv_w_k"], W["kv_w_v"], cs, alpha)
    wg("attn_w_q", x3b, dqb, True)
    wg("kv_w_k", x3b, dkb, True)
    wg("kv_w_v", x3b, dvb, True)
    hook("grads3", None, G)

    dpp0, dgl0, dx2 = ple_bwd(dx3, pp0, gl0, W["ple_w_gate0"], "ple_bwd0")
    wg("ple_w_proj0", (0, p), dpp0, False)
    wg("ple_w_gate0", x2b, dgl0, True)
    dpre_mlp0, dpre_mlp0b, dm0, g_mlp_g0, g_mlp_b0 = mlp_bwd1(dx2, pre_mlp0, row(small["mlp_ln_g"], 0), r0,
                                                              W["mlp_w_down0"], "mlp_bwd1_0")
    wg("mlp_w_down0", r0, dpre_mlp0b, True)
    wg("mlp_w_up0", x1b, dm0, False)
    hook("grads2", None, G)
    dpre_mix0, dpre_mix0b, dsw, g_mix_g0, g_mix_b0, g_b_out = mlp_bwd2(dpre_mlp0, dm0, W["mlp_w_up0"], alpha, pre_mix0,
                                                                      row(small["mix_ln_g"], 0), W["conv_w_out"],
                                                                      "mlp_bwd2_0")
    wg("conv_w_out", s, dpre_mix0b, True)
    hook("grads1", None, G)
    dcv, g_cln_g, g_cln_b, g_b_dw = conv_mid_bwd(dsw, cv, small["conv_ln_g"], small["conv_ln_b"])
    dh, g_w_dw, g_b_in = dwconv_bwd(dcv, h, small["conv_w_dw"], taps)
    wg("conv_w_in", x, dh, False)

    sg["conv_b_in"] = g_b_in
    sg["conv_w_dw"] = g_w_dw
    sg["conv_b_dw"], sg["conv_ln_g"], sg["conv_ln_b"], sg["conv_b_out"] = g_b_dw, g_cln_g, g_cln_b, g_b_out
    sg["mix_ln_g"] = [g_mix_g0, g_mix_g1]
    sg["mix_ln_b"] = [g_mix_b0, g_mix_b1]
    sg["mlp_ln_g"] = [g_mlp_g0, g_mlp_g1]
    sg["mlp_ln_b"] = [g_mlp_b0, g_mlp_b1]
    sg["attn_sinks"] = dsinks[:, 0][None, :]
    sg["loss"] = loss
    hook("grads0", None, G, sg)
    grad_x = conv_in_bwd(dh, dpre_mix0, W["conv_w_in"], alpha)
    return loss, grad_x, G, sg


BUFFERS = (("b0", ("conv_w_in",)), ("a0", ("conv_w_out",)),
           ("a1", ("mlp_w_up0", "mlp_w_down0", "ple_w_gate0")), ("c1", ("ple_w_proj0",)),
           ("a2", ("mlp_w_up1", "mlp_w_down1", "ple_w_gate1", "attn_w_q", "attn_w_o")),
           ("c2", ("kv_w_k", "kv_w_v", "ple_w_proj1")))
GROUPS = (("b0",), ("a0", "a1", "c1"), ("a2", "c2"))
REDUCED = (("b0",), ("a0",), ("a1", "c1"), ("a2", "c2"))
ROW_SHARDED = {"mlp_w_down0", "mlp_w_down1", "ple_w_gate0", "ple_w_gate1", "conv_w_out", "attn_w_q", "attn_w_o", "kv_w_k",
               "kv_w_v"}


def _split_layers(weights):
    out = {"conv_w_in": weights["conv_w_in"][0], "conv_w_out": weights["conv_w_out"][0],
           "attn_w_q": weights["attn_w_q"][0], "attn_w_o": weights["attn_w_o"][0],
           "kv_w_k": weights["kv_w_k"], "kv_w_v": weights["kv_w_v"]}
    for n in ("mlp_w_up", "mlp_w_down", "ple_w_proj", "ple_w_gate"):
        for i in range(weights[n].shape[0]):
            out[n + str(i)] = weights[n][i]
    return out


def _layout(shards):
    lay = {}
    for key, names in BUFFERS:
        off, rows = 0, []
        for n in names:
            rows.append((n, off, shards[n].shape[0]))
            off += shards[n].shape[0]
        lay[key] = rows
    return lay


def _place():
    return lax.axis_index("x"), lax.axis_index("y"), lax.axis_index("c")


def _flip(v, f):
    return (v + f) % 2 if f else v


CHIP_FLIPS = ((1, 0), (0, 1), (1, 1))


HBM = pl.BlockSpec(memory_space=pltpu.HBM)
SEM = pl.BlockSpec(memory_space=pltpu.SEMAPHORE)
EFFECT = pltpu.SideEffectType.DATAFLOW_SIDE_EFFECTING


def _half(ref, rows, c):
    return ref.at[pl.ds(pl.multiple_of(c * (rows // 2), 16), rows // 2), :]


def _gather_copies(refs, shapes, whole, send, recv):
    x, y, c = _place()
    me = 2 * x + y
    na = len(refs)
    cps = []
    for d, (fx, fy) in enumerate(CHIP_FLIPS):
        to = (_flip(x, fx), _flip(y, fy), c)
        for k in range(na):
            mine = refs[k].at[me] if k >= na - whole else _half(refs[k].at[me], shapes[k][1], c)
            cps.append(pltpu.make_async_remote_copy(mine, mine, send.at[d * na + k], recv.at[d * na + k], device_id=to,
                                                    device_id_type=MESH))
    return cps


def gather_start(bufs, whole, after, name):
    na = len(bufs)
    shapes = [b.shape for b in bufs]
    nsem = len(CHIP_FLIPS) * na

    def body(*refs):
        ins = refs[:na]
        send, recv = refs[-(na + 3)], refs[-(na + 2)]
        token = refs[-1]
        for cp in _gather_copies(ins, shapes, whole, send, recv):
            cp.start()
        token[...] = jnp.zeros_like(token)

    args = [pltpu.with_memory_space_constraint(b, pltpu.HBM) for b in bufs]
    ins = [HBM] * na
    if after is not None:
        args.append(after)
        ins.append(ANY)
    return pl.pallas_call(
        body, name=name, in_specs=ins,
        out_specs=[SEM, SEM] + [HBM] * na + [pl.BlockSpec(memory_space=pltpu.VMEM)],
        out_shape=[pltpu.SemaphoreType.DMA((nsem,)), pltpu.SemaphoreType.DMA((nsem,))]
        + [pltpu.HBM(b.shape, b.dtype) for b in bufs] + [_sds((8, 128), F32)],
        input_output_aliases={k: k + 2 for k in range(na)},
        compiler_params=pltpu.CompilerParams(has_side_effects=EFFECT))(*args)


def gather_wait(send, recv, bufs, whole, after, name):
    na = len(bufs)
    shapes = [b.shape for b in bufs]

    def body(*refs):
        ins = refs[:na]
        send_ref, recv_ref = refs[na], refs[na + 1]
        for cp in _gather_copies(ins, shapes, whole, send_ref, recv_ref):
            cp.wait_send()
            cp.wait_recv()

    return pl.pallas_call(
        body, name=name, in_specs=[HBM] * na + [SEM, SEM, ANY], out_specs=[HBM] * na,
        out_shape=[pltpu.HBM(b.shape, b.dtype) for b in bufs], input_output_aliases={k: k for k in range(na)},
        compiler_params=pltpu.CompilerParams(has_side_effects=EFFECT))(*bufs, send, recv, after)


def sibling_forward(bufs, name):
    nb = len(bufs)

    def body(*refs):
        outs = refs[nb:2 * nb]
        send, recv = refs[2 * nb:]
        x, y, c = _place()
        cps = []
        for d, (fx, fy) in enumerate(CHIP_FLIPS):
            frm = 2 * _flip(x, fx) + _flip(y, fy)
            for k in range(nb):
                theirs = _half(outs[k].at[frm], bufs[k].shape[1], c)
                cps.append(pltpu.make_async_remote_copy(theirs, theirs, send.at[d * nb + k], recv.at[d * nb + k],
                                                        device_id=(x, y, 1 - c), device_id_type=MESH))
        for cp in cps:
            cp.start()
        for cp in cps:
            cp.wait()

    nsem = len(CHIP_FLIPS) * nb
    return pl.pallas_call(
        body, name=name, in_specs=[ANY] * nb, out_specs=[ANY] * nb, out_shape=[_sds(b.shape, b.dtype) for b in bufs],
        input_output_aliases={k: k for k in range(nb)},
        scratch_shapes=[pltpu.SemaphoreType.DMA((nsem,)), pltpu.SemaphoreType.DMA((nsem,))])(*bufs)


def pack_rows(pieces, rows, width, name):
    def body(*refs):
        o_ref = refs[-1]
        o_ref[...] = jnp.zeros_like(o_ref)
        for ref, (a, off) in zip(refs[:-1], pieces):
            o_ref[off:off + a.shape[0], 0:a.shape[1]] = ref[...]

    return pl.pallas_call(body, name=name, out_shape=_sds((rows, width), F32))(*[a for a, _ in pieces])


PEER_FLIPS = tuple((fx, fy, fc) for fx in (0, 1) for fy in (0, 1) for fc in (0, 1) if fx or fy or fc)


def _reduce_copies(parts, zones, pack, send, recv):
    x, y, c = _place()
    nb = len(parts)
    na = nb + (1 if pack is not None else 0)
    cps = []
    for f, (fx, fy, fc) in enumerate(PEER_FLIPS):
        tx, ty, tc = _flip(x, fx), _flip(y, fy), _flip(c, fc)
        for k in range(nb):
            hrows = parts[k].shape[1] // 2
            piece = parts[k].at[2 * tx + ty, pl.ds(pl.multiple_of(tc * hrows, 16), hrows), :]
            cps.append(pltpu.make_async_remote_copy(piece, zones[k].at[f], send.at[f * na + k], recv.at[f * na + k],
                                                    device_id=(tx, ty, tc), device_id_type=MESH))
        if pack is not None:
            mine = pack.at[4 * x + 2 * y + c]
            cps.append(pltpu.make_async_remote_copy(mine, mine, send.at[f * na + nb], recv.at[f * na + nb],
                                                    device_id=(tx, ty, tc), device_id_type=MESH))
    return cps


def reduce_begin(parts, pack, name):
    nb = len(parts)
    zones = [lax.empty((len(PEER_FLIPS), g.shape[1] // 2, g.shape[2]), g.dtype) for g in parts]
    arrs = list(parts) + zones + ([pack] if pack is not None else [])
    na = len(arrs)
    nsem = len(PEER_FLIPS) * (nb + (1 if pack is not None else 0))

    def body(*refs):
        ins = refs[:na]
        send, recv = refs[na], refs[na + 1]
        for cp in _reduce_copies(ins[:nb], ins[nb:2 * nb], ins[2 * nb] if pack is not None else None, send, recv):
            cp.start()
        refs[-1][...] = jnp.zeros_like(refs[-1])

    return pl.pallas_call(
        body, name=name, in_specs=[HBM] * na,
        out_specs=[SEM, SEM] + [HBM] * na + [pl.BlockSpec(memory_space=pltpu.VMEM)],
        out_shape=[pltpu.SemaphoreType.DMA((nsem,)), pltpu.SemaphoreType.DMA((nsem,))]
        + [pltpu.HBM(a.shape, a.dtype) for a in arrs] + [_sds((8, 128), F32)],
        input_output_aliases={k: k + 2 for k in range(na)},
        compiler_params=pltpu.CompilerParams(has_side_effects=EFFECT))(
            *[pltpu.with_memory_space_constraint(a, pltpu.HBM) for a in arrs])


def reduce_end(send, recv, parts, zones, pack, after, name):
    nb = len(parts)
    arrs = list(parts) + list(zones) + ([pack] if pack is not None else [])
    na = len(arrs)

    def body(*refs):
        ins = refs[:na]
        for cp in _reduce_copies(ins[:nb], ins[nb:2 * nb], ins[2 * nb] if pack is not None else None, refs[na], refs[na + 1]):
            cp.wait_send()
            cp.wait_recv()

    return pl.pallas_call(
        body, name=name, in_specs=[HBM] * na + [SEM, SEM, ANY], out_specs=[HBM] * na,
        out_shape=[pltpu.HBM(a.shape, a.dtype) for a in arrs], input_output_aliases={k: k for k in range(na)},
        compiler_params=pltpu.CompilerParams(has_side_effects=EFFECT))(*arrs, send, recv, after)


def sibling_share(halves, name):
    nb = len(halves)

    def body(*refs):
        outs = refs[nb:2 * nb]
        send, recv = refs[2 * nb:]
        x, y, c = _place()
        cps = []
        for k in range(nb):
            hrows = halves[k].shape[0] // 2
            mine = outs[k].at[pl.ds(pl.multiple_of(c * hrows, 8), hrows), :]
            cps.append(pltpu.make_async_remote_copy(mine, mine, send.at[k], recv.at[k], device_id=(x, y, 1 - c),
                                                    device_id_type=MESH))
        for cp in cps:
            cp.start()
        for cp in cps:
            cp.wait()

    return pl.pallas_call(
        body, name=name, in_specs=[ANY] * nb, out_specs=[ANY] * nb,
        out_shape=[_sds(h.shape, h.dtype) for h in halves], input_output_aliases={k: k for k in range(nb)},
        scratch_shapes=[pltpu.SemaphoreType.DMA((nb,)), pltpu.SemaphoreType.DMA((nb,))])(*halves)


def _row_tile(rows):
    for cand in (512, 384, 256, 128, 64, 32, 16):
        if rows % cand == 0:
            return cand
    return rows


def piece_sum(g, z, idx, name):
    _, hrows, W = z.shape
    tr = _row_tile(hrows)
    nrb = hrows // tr

    def body(idx_ref, g_ref, z_ref, o_ref):
        acc = g_ref[...].astype(F32)
        for d in range(z.shape[0]):
            acc = acc + z_ref[d].astype(F32)
        o_ref[...] = acc

    gs = pltpu.PrefetchScalarGridSpec(
        num_scalar_prefetch=1, grid=(nrb,),
        in_specs=[pl.BlockSpec((None, tr, W), lambda i, sc: (sc[0], sc[1] * nrb + i, 0)),
                  pl.BlockSpec((z.shape[0], tr, W), lambda i, sc: (0, i, 0))],
        out_specs=pl.BlockSpec((tr, W), lambda i, sc: (sc[1] * nrb + i, 0)))
    return pl.pallas_call(body, name=name, grid_spec=gs, out_shape=_sds((2 * hrows, W), F32),
                          compiler_params=pltpu.CompilerParams(dimension_semantics=("parallel",),
                                                               vmem_limit_bytes=48 * 2 ** 20))(idx, g, z)


def small_sum(packs):
    n, R, W = packs.shape

    def body(p_ref, o_ref):
        acc = p_ref[0]
        for d in range(1, n):
            acc = acc + p_ref[d]
        o_ref[...] = acc

    return pl.pallas_call(body, name="small_sum", out_shape=_sds((R, W), F32))(packs)


WEIGHTS = ["conv_w_in", "conv_b_in", "conv_w_dw", "conv_b_dw", "conv_ln_g", "conv_ln_b", "conv_w_out", "conv_b_out", "kv_w_k",
           "kv_w_v", "attn_w_q", "attn_sinks", "attn_w_o", "mix_ln_g", "mix_ln_b", "mlp_w_up", "mlp_w_down", "mlp_ln_g",
           "mlp_ln_b", "ple_w_proj", "ple_w_gate"]
BIG = ["conv_w_in", "conv_w_out", "kv_w_k", "kv_w_v", "attn_w_q", "attn_w_o", "mlp_w_up", "mlp_w_down", "ple_w_proj",
       "ple_w_gate"]
SMALL = [n for n in WEIGHTS if n not in BIG]


def _step(x, p, target, w, m, v):
    D = x.shape[-1]
    ds = D // NS
    xq, yq, cq = _place()
    chip = 2 * xq + yq
    idx = jnp.stack([chip, cq]).astype(jnp.int32)

    shards = _split_layers(w)
    lay = _layout(shards)
    taps = w["conv_w_dw"].shape[1]
    small_loc = pack_rows([(w["conv_w_dw"][0], 0), (w["conv_b_dw"], HALO), (w["conv_ln_g"], HALO + 1), (w["conv_ln_b"], HALO + 2),
                           (w["conv_b_out"], HALO + 3), (w["conv_b_in"].reshape(2, ds), HALO + 4)], HALO + 8, ds, "pack_small")
    slot = lambda a: lax.dynamic_update_slice(lax.empty((NS,) + a.shape, a.dtype), a[None], (chip, 0, 0))
    started, token = [], None
    for gi, keys in enumerate(GROUPS):
        bufs = [slot(jnp.concatenate([shards[n].astype(BF16) for n, _, _ in lay[key]], axis=0)) for key in keys]
        if gi == 0:
            bufs.append(slot(small_loc))
        send, recv, *thru, token = gather_start(bufs, 1 if gi == 0 else 0, token, "gather_start%d" % gi)
        started.append((send, recv, thru))
    W = {}

    def arrive(gi, after):
        send, recv, thru = started[gi]
        whole = 1 if gi == 0 else 0
        got = gather_wait(send, recv, thru, whole, after, "gather_wait%d" % gi)
        nk = len(GROUPS[gi])
        for key, buf in zip(GROUPS[gi], sibling_forward(got[:nk], "sibling_forward%d" % gi)):
            for n, off, rows in lay[key]:
                W[n] = (buf, off, rows)
        return got[nk:]

    gs, = arrive(0, token)
    across = lambda rows: gs[:, rows, :].transpose(1, 0, 2).reshape(rows.stop - rows.start, D)
    small = {"taps": taps, "conv_w_dw": across(slice(0, HALO)), "conv_b_dw": across(slice(HALO, HALO + 1)),
             "conv_ln_g": across(slice(HALO + 1, HALO + 2)), "conv_ln_b": across(slice(HALO + 2, HALO + 3)),
             "conv_b_out": across(slice(HALO + 3, HALO + 4)), "conv_b_in": gs[:, HALO + 4:HALO + 6, :].reshape(1, 2 * D),
             "attn_sinks": w["attn_sinks"], "mix_ln_g": w["mix_ln_g"], "mix_ln_b": w["mix_ln_b"],
             "mlp_ln_g": w["mlp_ln_g"], "mlp_ln_b": w["mlp_ln_b"]}

    reducing = {}

    def reduce_start(gi, G, pack):
        nk = len(REDUCED[gi])
        send, recv, *thru, token = reduce_begin([G[key] for key in REDUCED[gi]], pack, "reduce_begin%d" % gi)
        reducing[gi] = (send, recv, thru[:nk], thru[nk:2 * nk], thru[2 * nk] if pack is not None else None)
        _FOLLOW.append(token)

    def small_pack(sg):
        pieces = [(sg["conv_b_in"].reshape(2, D), 0), (sg["conv_w_dw"], 2)]
        r0 = 2 + HALO
        for i, n in enumerate(("conv_b_dw", "conv_ln_g", "conv_ln_b", "conv_b_out")):
            pieces.append((sg[n], r0 + i))
        r0 += 4
        for i, n in enumerate(("mix_ln_g", "mix_ln_b", "mlp_ln_g", "mlp_ln_b")):
            pieces += [(sg[n][0], r0 + 2 * i), (sg[n][1], r0 + 2 * i + 1)]
        pieces += [(sg["attn_sinks"], r0 + 8), (sg["loss"][0:1], r0 + 9)]
        mine = pack_rows(pieces, r0 + 10, D, "pack_small_grads")
        return lax.dynamic_update_slice(lax.empty((8,) + mine.shape, F32), mine[None], (4 * xq + 2 * yq + cq, 0, 0))

    def hook(stage, after, G, sg=None):
        if stage == "weights1":
            arrive(1, after)
        elif stage == "weights2":
            arrive(2, after)
        elif stage == "grads0":
            reduce_start(0, G, small_pack(sg))
        elif stage.startswith("grads"):
            reduce_start(int(stage[5:]), G, None)

    loss, grad_x, G, sg = _local_step(x[0], p[:, 0], target[0], W, small, lay, hook)
    _FOLLOW.clear()
    nsink = w["attn_sinks"].shape[1]

    grads, delta, new_m, new_v = {}, {}, {}, {}
    found = {}

    def finish(groups, after, tag):
        keys, halves, tot = [], [], None
        for gi in groups:
            send, recv, parts, zones, pack = reducing[gi]
            done = reduce_end(send, recv, parts, zones, pack, after, "reduce_end%d" % gi)
            nk = len(REDUCED[gi])
            for key, g_, z_ in zip(REDUCED[gi], done[:nk], done[nk:2 * nk]):
                keys.append(key)
                halves.append(piece_sum(g_, z_, idx, "piece_sum_" + key))
            if pack is not None:
                tot = small_sum(done[2 * nk])
        for key, buf in zip(keys, sibling_share(halves, "sibling_share" + tag)):
            for n, off, _ in lay[key]:
                found[n] = (buf, off)
        return tot

    def big_adamw(names):
        for n in names:
            three = lambda a: a.reshape((-1,) + a.shape[-2:])
            w3, m3, v3 = three(w[n]), three(m[n]), three(v[n])
            outs = None
            for i in range(w3.shape[0]):
                buf, off = found[n + str(i)] if n + str(i) in found else found[n]
                outs = adamw_layer(w3, m3, v3, i, buf, off, outs, "adamw_%s%d" % (n, i))
            grads[n], delta[n], new_m[n], new_v[n] = [a.reshape(w[n].shape) for a in outs]

    last = [n for n, _, _ in lay[REDUCED[0][0]]]
    finish(reversed(range(1, len(REDUCED))), grad_x, "1")
    big_adamw([n for n in BIG if n not in last])
    tot = finish([0], new_v["mlp_w_down"], "0")
    big_adamw(last)
    cols = lambda rows: lax.dynamic_slice(rows, (0, chip * ds), (rows.shape[0], ds))
    grads["conv_b_in"] = lax.dynamic_slice(tot[0:2].reshape(1, 2 * D), (0, chip * 2 * ds), (1, 2 * ds))
    grads["conv_w_dw"] = cols(tot[2:2 + taps])[None]
    r0 = 2 + HALO
    for i, n in enumerate(("conv_b_dw", "conv_ln_g", "conv_ln_b", "conv_b_out")):
        grads[n] = cols(tot[r0 + i:r0 + i + 1])
    r0 += 4
    for i, n in enumerate(("mix_ln_g", "mix_ln_b", "mlp_ln_g", "mlp_ln_b")):
        grads[n] = tot[r0 + 2 * i:r0 + 2 * i + 2]
    grads["attn_sinks"] = tot[r0 + 8:r0 + 9, 0:nsink]

    ds_, ms_, vs_ = adamw_many([w[n] for n in SMALL], [grads[n] for n in SMALL], [m[n] for n in SMALL], [v[n] for n in SMALL])
    for n, d_, m_, v_ in zip(SMALL, ds_, ms_, vs_):
        delta[n], new_m[n], new_v[n] = d_, m_, v_

    total = tot[r0 + 9, 0]
    return (total, grad_x[None], *[grads[n] for n in WEIGHTS], *[delta[n] for n in WEIGHTS], *[new_m[n] for n in WEIGHTS],
            *[new_v[n] for n in WEIGHTS])


def kernel(x, p, conv_w_in, conv_b_in, conv_w_dw, conv_b_dw, conv_ln_g, conv_ln_b, conv_w_out, conv_b_out, kv_w_k, kv_w_v, attn_w_q, attn_sinks, attn_w_o, mix_ln_g, mix_ln_b, mlp_w_up, mlp_w_down, mlp_ln_g, mlp_ln_b, ple_w_proj, ple_w_gate, loss_target, m_conv_w_in, m_conv_b_in, m_conv_w_dw, m_conv_b_dw, m_conv_ln_g, m_conv_ln_b, m_conv_w_out, m_conv_b_out, m_kv_w_k, m_kv_w_v, m_attn_w_q, m_attn_sinks, m_attn_w_o, m_mix_ln_g, m_mix_ln_b, m_mlp_w_up, m_mlp_w_down, m_mlp_ln_g, m_mlp_ln_b, m_ple_w_proj, m_ple_w_gate, v_conv_w_in, v_conv_b_in, v_conv_w_dw, v_conv_b_dw, v_conv_ln_g, v_conv_ln_b, v_conv_w_out, v_conv_b_out, v_kv_w_k, v_kv_w_v, v_attn_w_q, v_attn_sinks, v_attn_w_o, v_mix_ln_g, v_mix_ln_b, v_mlp_w_up, v_mlp_w_down, v_mlp_ln_g, v_mlp_ln_b, v_ple_w_proj, v_ple_w_gate):
    w = dict(zip(WEIGHTS, (conv_w_in, conv_b_in, conv_w_dw, conv_b_dw, conv_ln_g, conv_ln_b, conv_w_out, conv_b_out, kv_w_k,
                           kv_w_v, attn_w_q, attn_sinks, attn_w_o, mix_ln_g, mix_ln_b, mlp_w_up, mlp_w_down, mlp_ln_g, mlp_ln_b,
                           ple_w_proj, ple_w_gate)))
    m = dict(zip(WEIGHTS, (m_conv_w_in, m_conv_b_in, m_conv_w_dw, m_conv_b_dw, m_conv_ln_g, m_conv_ln_b, m_conv_w_out,
                           m_conv_b_out, m_kv_w_k, m_kv_w_v, m_attn_w_q, m_attn_sinks, m_attn_w_o, m_mix_ln_g, m_mix_ln_b,
                           m_mlp_w_up, m_mlp_w_down, m_mlp_ln_g, m_mlp_ln_b, m_ple_w_proj, m_ple_w_gate)))
    v = dict(zip(WEIGHTS, (v_conv_w_in, v_conv_b_in, v_conv_w_dw, v_conv_b_dw, v_conv_ln_g, v_conv_ln_b, v_conv_w_out,
                           v_conv_b_out, v_kv_w_k, v_kv_w_v, v_attn_w_q, v_attn_sinks, v_attn_w_o, v_mix_ln_g, v_mix_ln_b,
                           v_mlp_w_up, v_mlp_w_down, v_mlp_ln_g, v_mlp_ln_b, v_ple_w_proj, v_ple_w_gate)))
    return _step(x, p, loss_target, w, m, v)
```

```python
import jax
import jax.numpy as jnp
from jax import lax
from jax.experimental import pallas as pl
from jax.experimental.pallas import tpu as pltpu

F32 = jnp.float32
BF16 = jnp.bfloat16
NS = 4
HEAD = 64
BLK = 128
ROPE = 16
ROPE_THETA = 500000.0
LN_EPS = 1e-5
NEG = -1e30
KV_PER_STAGE = 1
HALO = 32
ADAM_LR, ADAM_B1, ADAM_B2, ADAM_EPS, ADAM_WD, ADAM_STEP = 0.001, 0.9, 0.999, 1e-08, 0.01, 10
MESH = pl.DeviceIdType.MESH
ANY = pl.BlockSpec(memory_space=pl.ANY)
NT = (((1,), (1,)), ((), ()))
TN = (((0,), (0,)), ((), ()))


_FOLLOW = []


def _pc(body, name, grid, in_specs, out_specs, out_shape, scratch=(), sem=None, vmem=56, **kw):
    call = lambda fn, ins: pl.pallas_call(
        fn, name=name, grid=grid, in_specs=ins, out_specs=out_specs, out_shape=out_shape,
        scratch_shapes=list(scratch),
        compiler_params=pltpu.CompilerParams(dimension_semantics=sem, vmem_limit_bytes=vmem * 2 ** 20), **kw)
    if not _FOLLOW:
        return call(body, in_specs)
    extra = list(_FOLLOW)
    _FOLLOW.clear()
    n_in = len(in_specs)

    def ordered(*refs):
        return body(*refs[:n_in], *refs[n_in + len(extra):])

    run = call(ordered, list(in_specs) + [ANY] * len(extra))
    return lambda *args: run(*args, *extra)


def _rows(tm, n):
    return pl.BlockSpec((tm, n), lambda i: (i, 0))


def _const(shape):
    return pl.BlockSpec(shape, lambda *_: (0,) * len(shape))


def _wspec(w):
    buf, off, rows = w
    assert off % rows == 0
    return pl.BlockSpec((NS, rows, buf.shape[2]), lambda *_: (0, off // rows, 0))


def _rows_joined(w_ref):
    n, r, c = w_ref.shape
    return w_ref[...].reshape(n * r, c)


def _sds(shape, dtype):
    return jax.ShapeDtypeStruct(shape, dtype)


def _tile(t, rows=256):
    return min(rows, t)


def _sigmoid(x):
    return 0.5 * jnp.tanh(0.5 * x) + 0.5


def _ln_stats(w):
    mu = jnp.mean(w, axis=-1, keepdims=True)
    xc = w - mu
    var = jnp.mean(xc * xc, axis=-1, keepdims=True)
    rstd = lax.rsqrt(var + LN_EPS)
    return xc * rstd, rstd


def _ln_bwd(dy, w, g):
    xhat, rstd = _ln_stats(w)
    dxhat = dy * g
    m1 = jnp.mean(dxhat, axis=-1, keepdims=True)
    m2 = jnp.mean(dxhat * xhat, axis=-1, keepdims=True)
    dw = rstd * (dxhat - m1 - xhat * m2)
    return dw, jnp.sum(dy * xhat, axis=0, keepdims=True), jnp.sum(dy, axis=0, keepdims=True)


def _acc_rows(ref, val, first):
    @pl.when(first)
    def _():
        ref[...] = val

    @pl.when(jnp.logical_not(first))
    def _():
        ref[...] += val


def conv_in_fwd(xb, w_in, b_in):
    T, D = xb.shape
    nw = w_in[0].shape[2]
    tm = _tile(T, 512)

    def body(x_ref, w_ref, b_ref, h_ref):
        x = x_ref[...].astype(BF16)
        for j in range(NS):
            sl = slice(j * nw, (j + 1) * nw)
            h_ref[:, sl] = (jnp.dot(x, w_ref[j], preferred_element_type=F32) + b_ref[:, sl]).astype(BF16)

    return _pc(body, "conv_in_fwd", (T // tm,), [_rows(tm, D), _wspec(w_in), _const((1, NS * nw))],
               _rows(tm, NS * nw), _sds((T, NS * nw), BF16), sem=("parallel",))(xb, w_in[0], b_in)


CONV_ROWS = 16


def _phases(scr, sh):
    n = scr.shape[0] - 8
    for b in range(1, 8):
        sh[b - 1, 0:n, :] = scr[b:b + n, :]


def _spread(w_ref, wb, taps):
    for j in range(taps):
        wb[j] = jnp.broadcast_to(w_ref[j:j + 1, :], wb.shape[1:])


def _tap(scr, sh, o, n):
    b = o % 8
    return scr[o:o + n, :] if b == 0 else sh[b - 1, o - b:o - b + n, :]


def dwconv_fwd(h, w_dw, b_dw, ln_g, ln_b, taps):
    T = h.shape[0]
    C = h.shape[1] // 2
    tq = _tile(T)
    nh = tq // HALO
    off = HALO - (taps - 1)

    def body(a_ref, g_ref, ap_ref, gp_ref, w_ref, bdw_ref, lg_ref, lb_ref, cv_ref, s_ref, scr, sh, wb):
        i = pl.program_id(0)
        scr[HALO:HALO + tq, :] = a_ref[...].astype(F32) * _sigmoid(g_ref[...].astype(F32))
        up = ap_ref[...].astype(F32) * _sigmoid(gp_ref[...].astype(F32))
        scr[0:HALO, :] = jnp.where(i > 0, up, 0.0)
        _phases(scr, sh)
        _spread(w_ref, wb, taps)
        bias = jnp.broadcast_to(bdw_ref[...], (8, C))
        for r in range(tq // CONV_ROWS):
            accs = [bias] * (CONV_ROWS // 8)
            for j in range(taps):
                wj = wb[j]
                accs = [acc + wj * _tap(scr, sh, off + j + r * CONV_ROWS + 8 * k, 8) for k, acc in enumerate(accs)]
            for k, acc in enumerate(accs):
                cv_ref[r * CONV_ROWS + 8 * k:r * CONV_ROWS + 8 * k + 8, :] = acc
        xhat, _ = _ln_stats(cv_ref[...])
        ln = xhat * lg_ref[...] + lb_ref[...]
        s_ref[...] = (ln * _sigmoid(ln)).astype(BF16)

    prev = lambda col: pl.BlockSpec((HALO, C), lambda i: (jnp.maximum(i * nh - 1, 0), col))
    cur = lambda col: pl.BlockSpec((tq, C), lambda i: (i, col))
    return _pc(body, "dwconv_fwd", (T // tq,),
               [cur(0), cur(1), prev(0), prev(1), _const((HALO, C)), _const((1, C)), _const((1, C)), _const((1, C))],
               [_rows(tq, C), _rows(tq, C)], [_sds((T, C), F32), _sds((T, C), BF16)],
               scratch=[pltpu.VMEM((HALO + tq, C), F32), pltpu.VMEM((7, HALO + tq, C), F32), pltpu.VMEM((taps, 8, C), F32)],
               sem=("parallel",))(h, h, h, h, w_dw, b_dw, ln_g, ln_b)


def mm_res_ln(a, w, res, g, b, alpha, bias, name):
    T, K = a.shape
    D = res.shape[1]
    tm = _tile(T, 512)

    def body(*refs):
        a_ref, w_ref, res_ref, g_ref, b_ref = refs[:5]
        n = 5
        if bias is not None:
            bias_ref = refs[5]
            n = 6
        pre_ref, xo_ref, xb_ref = refs[n:n + 3]
        acc = jnp.dot(a_ref[...], _rows_joined(w_ref), preferred_element_type=F32)
        if bias is not None:
            acc = acc + bias_ref[...]
        pre = alpha * res_ref[...] + acc
        xhat, _ = _ln_stats(pre)
        xo = xhat * g_ref[...] + b_ref[...]
        pre_ref[...] = pre
        xo_ref[...] = xo
        xb_ref[...] = xo.astype(BF16)

    ins = [_rows(tm, K), _wspec(w), _rows(tm, D), _const((1, D)), _const((1, D))]
    args = [a, w[0], res, g, b]
    if bias is not None:
        ins.append(_const((1, D)))
        args.append(bias)
    return _pc(body, name, (T // tm,), ins, [_rows(tm, D)] * 3, [_sds((T, D), F32), _sds((T, D), F32), _sds((T, D), BF16)],
               sem=("parallel",))(*args)


def mlp_up_fwd(xb, w_up, name):
    T, D = xb.shape
    fs = w_up[0].shape[2]
    tm = _tile(T, 512)

    def body(x_ref, w_ref, r_ref, t_ref):
        x = x_ref[...]
        for j in range(NS):
            sl = slice(j * fs, (j + 1) * fs)
            m = jnp.maximum(jnp.dot(x, w_ref[j], preferred_element_type=F32), 0.0)
            r_ref[:, sl] = (m * m).astype(BF16)
            t_ref[:, sl] = (2.0 * m).astype(BF16)

    return _pc(body, name, (T // tm,), [_rows(tm, D), _wspec(w_up)], [_rows(tm, NS * fs)] * 2,
               [_sds((T, NS * fs), BF16)] * 2, sem=("parallel",))(xb, w_up[0])


def ple_fwd(x, xb, p, layer, w_proj, w_gate, target, name):
    T, D = x.shape
    P = p.shape[2]
    ds = D // NS
    tm = _tile(T, 512)
    last = target is not None

    def body(*refs):
        x_ref, xb_ref, p_ref, wp_ref, wg_ref = refs[:5]
        n = 5
        if last:
            t_ref = refs[5]
            n = 6
        o_ref, o2_ref, pp_ref, gl_ref = refs[n:n + 4]
        gl = jnp.dot(xb_ref[...], _rows_joined(wg_ref), preferred_element_type=F32)
        gl_ref[...] = gl.astype(BF16)
        sg = _sigmoid(gl)
        pb = p_ref[...].astype(BF16)
        sq = jnp.zeros((1, 1), F32)
        for j in range(NS):
            sl = slice(j * ds, (j + 1) * ds)
            pp = jnp.dot(pb, wp_ref[j], preferred_element_type=F32)
            pp_ref[:, sl] = pp.astype(BF16)
            out = x_ref[:, sl] + pp * sg[:, sl]
            if last:
                err = out - t_ref[:, sl]
                o_ref[:, sl] = err * (1.0 / D)
                e2 = jnp.sum(err * err, axis=0, keepdims=True)
                sq = sq + jnp.sum(e2, axis=1, keepdims=True)
            else:
                o_ref[:, sl] = out
                o2_ref[:, sl] = out.astype(BF16)
        if last:
            _acc_rows(o2_ref, jnp.broadcast_to(sq * (0.5 / D), (8, 128)), pl.program_id(0) == 0)

    ins = [_rows(tm, D), _rows(tm, D), pl.BlockSpec((None, tm, P), lambda i: (layer, i, 0)), _wspec(w_proj), _wspec(w_gate)]
    args = [x, xb, p, w_proj[0], w_gate[0]]
    if last:
        ins.append(_rows(tm, D))
        args.append(target)
        outs = [_rows(tm, D), _const((8, 128)), _rows(tm, D), _rows(tm, D)]
        shapes = [_sds((T, D), F32), _sds((8, 128), F32), _sds((T, D), BF16), _sds((T, D), BF16)]
    else:
        outs = [_rows(tm, D)] * 4
        shapes = [_sds((T, D), F32), _sds((T, D), BF16), _sds((T, D), BF16), _sds((T, D), BF16)]
    return _pc(body, name, (T // tm,), ins, outs, shapes, sem=("arbitrary",) if last else ("parallel",))(*args)


def _rope(x, cs_ref, sign):
    c = cs_ref[0]
    s = cs_ref[1] * sign
    lane = lax.broadcasted_iota(jnp.int32, c.shape, 1)
    first = (lane % HEAD) < (ROPE // 2)
    outs = []
    for gq in range(x.shape[1] // 128):
        xg = x[:, gq * 128:(gq + 1) * 128]
        sw = jnp.where(first, pltpu.roll(xg, 128 - ROPE // 2, 1), pltpu.roll(xg, ROPE // 2, 1))
        outs.append(xg * c + sw * s)
    return outs


def qkv_fwd(xb, w_q, w_k, w_v, cs):
    T, D = xb.shape
    HD, KVD = w_q[0].shape[2], w_k[0].shape[2]
    tm = _tile(T, 512)
    scale = 1.0 / (HEAD ** 0.5)

    def body(x_ref, wq_ref, wk_ref, wv_ref, cs_ref, q_ref, k_ref, v_ref):
        def proj(w_ref):
            return jnp.dot(x_ref[...], _rows_joined(w_ref), preferred_element_type=F32)

        for gq, val in enumerate(_rope(proj(wq_ref), cs_ref, 1.0)):
            q_ref[:, gq * 128:(gq + 1) * 128] = (val * scale).astype(BF16)
        for gq, val in enumerate(_rope(proj(wk_ref), cs_ref, 1.0)):
            k_ref[:, gq * 128:(gq + 1) * 128] = val.astype(BF16)
        v_ref[...] = proj(wv_ref).astype(BF16)

    cs_spec = pl.BlockSpec((2, tm, 128), lambda i: (0, i, 0))
    return _pc(body, "qkv_fwd", (T // tm,), [_rows(tm, D), _wspec(w_q), _wspec(w_k), _wspec(w_v), cs_spec],
               [_rows(tm, HD), _rows(tm, KVD), _rows(tm, KVD)],
               [_sds((T, HD), BF16), _sds((T, KVD), BF16), _sds((T, KVD), BF16)], sem=("parallel",))(
                   xb, w_q[0], w_k[0], w_v[0], cs)


def _band_mask(n):
    row = lax.broadcasted_iota(jnp.int32, (BLK, 2 * BLK), 0)
    col = lax.broadcasted_iota(jnp.int32, (BLK, 2 * BLK), 1)
    return (col > row) & (col <= row + BLK) & ((col >= BLK) | (n > 0))


def _head(h):
    return slice(h * HEAD, (h + 1) * HEAD)


def _softmax_sink(s, sink):
    m = jnp.maximum(jnp.max(s, axis=-1, keepdims=True), sink)
    e = jnp.exp(s - m)
    es = jnp.exp(sink - m)
    den = jnp.sum(e, axis=-1, keepdims=True) + es
    inv = 1.0 / den
    return e * inv, es * inv


def attn_fwd(q, k, v, sinks):
    T, HD = q.shape
    KVD = k.shape[1]
    NKV = KVD // HEAD
    G = HD // KVD

    def body(s_ref, q_ref, kc_ref, kp_ref, vc_ref, vp_ref, o_ref):
        valid = _band_mask(pl.program_id(0))
        NH = NKV * G
        k2 = [jnp.concatenate([kp_ref[:, _head(kh)], kc_ref[:, _head(kh)]], axis=0) for kh in range(NKV)]
        v2 = [jnp.concatenate([vp_ref[:, _head(kh)], vc_ref[:, _head(kh)]], axis=0) for kh in range(NKV)]
        sc = [lax.dot_general(q_ref[:, _head(hh)], k2[hh // G], NT, preferred_element_type=F32) for hh in range(NH)]
        pb = [_softmax_sink(jnp.where(valid, s, NEG), s_ref[0, hh])[0].astype(BF16) for hh, s in enumerate(sc)]
        for hh, p in enumerate(pb):
            o_ref[:, _head(hh)] = jnp.dot(p, v2[hh // G], preferred_element_type=F32).astype(BF16)

    cur = lambda n_: pl.BlockSpec((BLK, n_), lambda n: (n, 0))
    prev = lambda n_: pl.BlockSpec((BLK, n_), lambda n: (jnp.maximum(n - 1, 0), 0))
    return _pc(body, "attn_fwd", (T // BLK,),
               [pl.BlockSpec(memory_space=pltpu.SMEM), cur(HD), cur(KVD), prev(KVD), cur(KVD), prev(KVD)],
               cur(HD), _sds((T, HD), BF16), sem=("parallel",))(sinks, q, k, k, v, v)


def ple_bwd(dxo, pp, gl, w_gate, name):
    T, D = dxo.shape
    tm = _tile(T, 512)

    def body(d_ref, pp_ref, gl_ref, wg_ref, dpp_ref, dgl_ref, dx_ref):
        d = d_ref[...]
        sg = _sigmoid(gl_ref[...].astype(F32))
        dpp_ref[...] = (d * sg).astype(BF16)
        dgl = (d * pp_ref[...].astype(F32) * sg * (1.0 - sg)).astype(BF16)
        dgl_ref[...] = dgl
        dx_ref[...] = d + lax.dot_general(dgl, _rows_joined(wg_ref), NT, preferred_element_type=F32)

    return _pc(body, name, (T // tm,), [_rows(tm, D)] * 3 + [_wspec(w_gate)], [_rows(tm, D)] * 3,
               [_sds((T, D), BF16), _sds((T, D), BF16), _sds((T, D), F32)], sem=("parallel",))(dxo, pp, gl, w_gate[0])


def mlp_bwd1(dy, pre, g, t, w_down, name):
    T, D = dy.shape
    fs = w_down[2]
    tm = _tile(T, 512)

    def body(dy_ref, pre_ref, g_ref, t_ref, w_ref, dw_ref, dwb_ref, dm_ref, dg_ref, db_ref):
        dw, dg, db = _ln_bwd(dy_ref[...], pre_ref[...], g_ref[...])
        first = pl.program_id(0) == 0
        _acc_rows(dg_ref, dg, first)
        _acc_rows(db_ref, db, first)
        dwb = dw.astype(BF16)
        dw_ref[...] = dw
        dwb_ref[...] = dwb
        for j in range(NS):
            sl = slice(j * fs, (j + 1) * fs)
            dr = lax.dot_general(dwb, w_ref[j], NT, preferred_element_type=F32)
            dm_ref[:, sl] = (dr * t_ref[:, sl].astype(F32)).astype(BF16)

    return _pc(body, name, (T // tm,), [_rows(tm, D), _rows(tm, D), _const((1, D)), _rows(tm, NS * fs), _wspec(w_down)],
               [_rows(tm, D), _rows(tm, D), _rows(tm, NS * fs), _const((1, D)), _const((1, D))],
               [_sds((T, D), F32), _sds((T, D), BF16), _sds((T, NS * fs), BF16), _sds((1, D), F32), _sds((1, D), F32)],
               sem=("arbitrary",))(dy, pre, g, t, w_down[0])


def mlp_bwd2(dpre, dm, w_up, alpha, pre_mix, g_mix, w_mix, name):
    T, D = dpre.shape
    fs = w_up[0].shape[2]
    ms = w_mix[2]
    tm = _tile(T, 512)

    def body(dp_ref, dm_ref, wu_ref, pre_ref, g_ref, wm_ref, dw_ref, dwb_ref, do_ref, dg_ref, db_ref, dc_ref):
        dy = alpha * dp_ref[...]
        for j in range(NS):
            dy = dy + lax.dot_general(dm_ref[:, j * fs:(j + 1) * fs], wu_ref[j], NT, preferred_element_type=F32)
        dw, dg, db = _ln_bwd(dy, pre_ref[...], g_ref[...])
        first = pl.program_id(0) == 0
        _acc_rows(dg_ref, dg, first)
        _acc_rows(db_ref, db, first)
        _acc_rows(dc_ref, jnp.sum(dw, axis=0, keepdims=True), first)
        dwb = dw.astype(BF16)
        dw_ref[...] = dw
        dwb_ref[...] = dwb
        do_ref[...] = lax.dot_general(dwb, _rows_joined(wm_ref), NT, preferred_element_type=F32).astype(BF16)

    return _pc(body, name, (T // tm,),
               [_rows(tm, D), _rows(tm, NS * fs), _wspec(w_up), _rows(tm, D), _const((1, D)), _wspec(w_mix)],
               [_rows(tm, D), _rows(tm, D), _rows(tm, NS * ms), _const((1, D)), _const((1, D)), _const((1, D))],
               [_sds((T, D), F32), _sds((T, D), BF16), _sds((T, NS * ms), BF16)] + [_sds((1, D), F32)] * 3,
               sem=("arbitrary",))(dpre, dm, w_up[0], pre_mix, g_mix, w_mix[0])


def attn_bwd(q, k, v, do, sinks):
    T, HD = q.shape
    KVD = k.shape[1]
    NH, NKV = HD // HEAD, KVD // HEAD
    G = NH // NKV
    nb = T // BLK

    def body(s_ref, q_ref, do_ref, kc_ref, kp_ref, vc_ref, vp_ref, dq_ref, dk_ref, dv_ref, ds_ref, ck, cv):
        n = pl.program_id(0)

        @pl.when(n == 0)
        def _():
            ck[...] = jnp.zeros_like(ck)
            cv[...] = jnp.zeros_like(cv)
            ds_ref[...] = jnp.zeros_like(ds_ref)

        @pl.when(n < nb)
        def _():
            valid = _band_mask(n)
            for k0 in range(0, NKV, KV_PER_STAGE):
                khs = range(k0, min(k0 + KV_PER_STAGE, NKV))
                k2 = {kh: jnp.concatenate([kp_ref[:, _head(kh)], kc_ref[:, _head(kh)]], axis=0) for kh in khs}
                v2 = {kh: jnp.concatenate([vp_ref[:, _head(kh)], vc_ref[:, _head(kh)]], axis=0) for kh in khs}
                hs = [kh * G + gq for kh in khs for gq in range(G)]
                qs = {hh: q_ref[:, _head(hh)] for hh in hs}
                dos = {hh: do_ref[:, _head(hh)] for hh in hs}
                sc = {hh: lax.dot_general(qs[hh], k2[hh // G], NT, preferred_element_type=F32) for hh in hs}
                pr = {hh: _softmax_sink(jnp.where(valid, sc[hh], NEG), s_ref[0, hh]) for hh in hs}
                dp = {hh: lax.dot_general(dos[hh], v2[hh // G], NT, preferred_element_type=F32) for hh in hs}
                delta = {hh: jnp.sum(pr[hh][0] * dp[hh], axis=-1, keepdims=True) for hh in hs}
                dsb = {hh: (pr[hh][0] * (dp[hh] - delta[hh])).astype(BF16) for hh in hs}
                pb = {hh: pr[hh][0].astype(BF16) for hh in hs}
                for hh in hs:
                    ds_ref[hh:hh + 1, :] += jnp.broadcast_to(-jnp.sum(pr[hh][1] * delta[hh], axis=0, keepdims=True), (1, 128))
                for hh in hs:
                    dq_ref[:, _head(hh)] = jnp.dot(dsb[hh], k2[hh // G], preferred_element_type=F32)
                for kh in khs:
                    kv = _head(kh)
                    grp = [kh * G + gq for gq in range(G)]
                    dk2 = lax.dot_general(jnp.concatenate([dsb[hh] for hh in grp], axis=0),
                                          jnp.concatenate([qs[hh] for hh in grp], axis=0), TN, preferred_element_type=F32)
                    dv2 = lax.dot_general(jnp.concatenate([pb[hh] for hh in grp], axis=0),
                                          jnp.concatenate([dos[hh] for hh in grp], axis=0), TN, preferred_element_type=F32)
                    dk_ref[:, kv] = ck[:, kv] + dk2[0:BLK]
                    dv_ref[:, kv] = cv[:, kv] + dv2[0:BLK]
                    ck[:, kv] = dk2[BLK:2 * BLK]
                    cv[:, kv] = dv2[BLK:2 * BLK]

        @pl.when(n == nb)
        def _():
            dk_ref[...] = ck[...]
            dv_ref[...] = cv[...]

    qcur = pl.BlockSpec((BLK, HD), lambda n: (jnp.minimum(n, nb - 1), 0))
    kcur = pl.BlockSpec((BLK, KVD), lambda n: (jnp.minimum(n, nb - 1), 0))
    kprev = pl.BlockSpec((BLK, KVD), lambda n: (jnp.maximum(n - 1, 0), 0))
    return _pc(body, "attn_bwd", (nb + 1,),
               [pl.BlockSpec(memory_space=pltpu.SMEM), qcur, qcur, kcur, kprev, kcur, kprev],
               [qcur, kprev, kprev, _const((NH, 128))],
               [_sds((T, HD), F32), _sds((T, KVD), F32), _sds((T, KVD), F32), _sds((NH, 128), F32)],
               scratch=[pltpu.VMEM((BLK, KVD), F32), pltpu.VMEM((BLK, KVD), F32)],
               sem=("arbitrary",))(sinks, q, do, k, k, v, v)


def qkv_bwd(dq, dk, dv, dpre_mix, w_q, w_k, w_v, cs, alpha):
    T, HD = dq.shape
    KVD = dk.shape[1]
    D = dpre_mix.shape[1]
    tm = _tile(T, 512)
    scale = 1.0 / (HEAD ** 0.5)

    def body(dq_ref, dk_ref, dv_ref, dp_ref, wq_ref, wk_ref, wv_ref, cs_ref, dqb_ref, dkb_ref, dvb_ref, dx_ref):
        for gq, val in enumerate(_rope(dq_ref[...], cs_ref, -1.0)):
            dqb_ref[:, gq * 128:(gq + 1) * 128] = (val * scale).astype(BF16)
        for gq, val in enumerate(_rope(dk_ref[...], cs_ref, -1.0)):
            dkb_ref[:, gq * 128:(gq + 1) * 128] = val.astype(BF16)
        dvb_ref[...] = dv_ref[...].astype(BF16)
        dqb, dkb, dvb = dqb_ref[...], dkb_ref[...], dvb_ref[...]
        dx_ref[...] = (alpha * dp_ref[...]
                       + lax.dot_general(dqb, _rows_joined(wq_ref), NT, preferred_element_type=F32)
                       + lax.dot_general(dkb, _rows_joined(wk_ref), NT, preferred_element_type=F32)
                       + lax.dot_general(dvb, _rows_joined(wv_ref), NT, preferred_element_type=F32))

    cs_spec = pl.BlockSpec((2, tm, 128), lambda i: (0, i, 0))
    return _pc(body, "qkv_bwd", (T // tm,),
               [_rows(tm, HD), _rows(tm, KVD), _rows(tm, KVD), _rows(tm, D), _wspec(w_q), _wspec(w_k), _wspec(w_v), cs_spec],
               [_rows(tm, HD), _rows(tm, KVD), _rows(tm, KVD), _rows(tm, D)],
               [_sds((T, HD), BF16), _sds((T, KVD), BF16), _sds((T, KVD), BF16), _sds((T, D), F32)],
               sem=("parallel",))(dq, dk, dv, dpre_mix, w_q[0], w_k[0], w_v[0], cs)


def conv_mid_bwd(ds, cv, ln_g, ln_b):
    T, C = cv.shape
    tm = _tile(T, 512)

    def body(ds_ref, cv_ref, g_ref, b_ref, dcv_ref, dg_ref, db_ref, dc_ref):
        xhat, _ = _ln_stats(cv_ref[...])
        ln = xhat * g_ref[...] + b_ref[...]
        sg = _sigmoid(ln)
        dl = ds_ref[...].astype(F32) * (sg * (1.0 + ln * (1.0 - sg)))
        dcv, dg, db = _ln_bwd(dl, cv_ref[...], g_ref[...])
        first = pl.program_id(0) == 0
        _acc_rows(dg_ref, dg, first)
        _acc_rows(db_ref, db, first)
        _acc_rows(dc_ref, jnp.sum(dcv, axis=0, keepdims=True), first)
        dcv_ref[...] = dcv

    return _pc(body, "conv_mid_bwd", (T // tm,), [_rows(tm, C), _rows(tm, C), _const((1, C)), _const((1, C))],
               [_rows(tm, C), _const((1, C)), _const((1, C)), _const((1, C))],
               [_sds((T, C), F32)] + [_sds((1, C), F32)] * 3, sem=("arbitrary",))(ds, cv, ln_g, ln_b)


def dwconv_bwd(dcv, h, w_dw, taps):
    T, C = dcv.shape
    tq = _tile(T)
    nh = tq // HALO
    nblk = T // tq
    off = HALO - (taps - 1)

    def body(d_ref, dn_ref, a_ref, g_ref, ap_ref, gp_ref, w_ref, dh_ref, dw_ref, dbi_ref, su, sus, sd, sds, wb):
        i = pl.program_id(0)
        su[HALO:HALO + tq, :] = a_ref[...].astype(F32) * _sigmoid(g_ref[...].astype(F32))
        up = ap_ref[...].astype(F32) * _sigmoid(gp_ref[...].astype(F32))
        su[0:HALO, :] = jnp.where(i > 0, up, 0.0)
        sd[0:tq, :] = d_ref[...]
        sd[tq:tq + HALO, :] = jnp.where(i < nblk - 1, dn_ref[...], 0.0)
        _phases(su, sus)
        _phases(sd, sds)

        @pl.when(i == 0)
        def _():
            dw_ref[...] = jnp.zeros_like(dw_ref)

        for j in range(taps):
            dw_ref[j:j + 1, :] += jnp.sum(d_ref[...] * _tap(su, sus, off + j, tq), axis=0, keepdims=True)
        sa = jnp.zeros((1, C), F32)
        sb = jnp.zeros((1, C), F32)
        _spread(w_ref, wb, taps)
        for r in range(tq // CONV_ROWS):
            rows = slice(r * CONV_ROWS, (r + 1) * CONV_ROWS)
            dus = [wb[0] * _tap(sd, sds, taps - 1 + r * CONV_ROWS + 8 * k, 8) for k in range(CONV_ROWS // 8)]
            for j in range(1, taps):
                wj = wb[j]
                dus = [acc + wj * _tap(sd, sds, taps - 1 - j + r * CONV_ROWS + 8 * k, 8) for k, acc in enumerate(dus)]
            du = jnp.concatenate(dus, axis=0)
            a = a_ref[rows, :].astype(F32)
            sg = _sigmoid(g_ref[rows, :].astype(F32))
            da = du * sg
            dgt = du * a * sg * (1.0 - sg)
            dh_ref[rows, 0:C] = da.astype(BF16)
            dh_ref[rows, C:2 * C] = dgt.astype(BF16)
            sa = sa + jnp.sum(da, axis=0, keepdims=True)
            sb = sb + jnp.sum(dgt, axis=0, keepdims=True)
        first = i == 0
        _acc_rows(dbi_ref.at[:, 0:C], sa, first)
        _acc_rows(dbi_ref.at[:, C:2 * C], sb, first)

    prev = lambda col: pl.BlockSpec((HALO, C), lambda i: (jnp.maximum(i * nh - 1, 0), col))
    nxt = pl.BlockSpec((HALO, C), lambda i: (jnp.minimum((i + 1) * nh, T // HALO - 1), 0))
    cur = lambda col: pl.BlockSpec((tq, C), lambda i: (i, col))
    return _pc(body, "dwconv_bwd", (nblk,),
               [cur(0), nxt, cur(0), cur(1), prev(0), prev(1), _const((HALO, C))],
               [_rows(tq, 2 * C), _const((HALO, C)), _const((1, 2 * C))],
               [_sds((T, 2 * C), BF16), _sds((HALO, C), F32), _sds((1, 2 * C), F32)],
               scratch=[pltpu.VMEM((HALO + tq, C), F32), pltpu.VMEM((7, HALO + tq, C), F32),
                        pltpu.VMEM((HALO + tq, C), F32), pltpu.VMEM((7, HALO + tq, C), F32), pltpu.VMEM((taps, 8, C), F32)],
               sem=("arbitrary",))(dcv, dcv, h, h, h, h, w_dw)


def conv_in_bwd(dh, dpre_mix, w_in, alpha):
    T, D = dpre_mix.shape
    nw = w_in[0].shape[2]
    tm = _tile(T, 512)

    def body(dh_ref, dp_ref, w_ref, dx_ref):
        acc = alpha * dp_ref[...]
        for j in range(NS):
            acc = acc + lax.dot_general(dh_ref[:, j * nw:(j + 1) * nw], w_ref[j], NT, preferred_element_type=F32)
        dx_ref[...] = acc

    return _pc(body, "conv_in_bwd", (T // tm,), [_rows(tm, NS * nw), _rows(tm, D), _wspec(w_in)], _rows(tm, D),
               _sds((T, D), F32), sem=("parallel",))(dh, dpre_mix, w_in[0])


def wgrad(a, b, row_sharded, name, into):
    prev, out_shape, off = into
    layer = None
    if isinstance(a, tuple):
        layer, a = a
    T, Ka = a.shape[-2:]
    Nb = b.shape[1]
    ka, tn = min(Ka, 1024), min(Nb, 1024)
    tt = min(4096 if (Ka // ka) * (Nb // tn) >= 4 else 2048, T)
    nt = T // tt
    if row_sharded:
        sr = Ka // NS
        spb = max(ka // sr, 1)
        rb = ka // spb
        assert out_shape[2] == Nb and off % rb == 0
        out_spec = pl.BlockSpec((spb, rb, tn), lambda i, j, t: (i, off // rb, j))
    else:
        sc = Nb // NS
        spb = max(tn // sc, 1)
        rb = ka
        assert out_shape[2] == sc and off % ka == 0
        out_spec = pl.BlockSpec((spb, ka, tn // spb), lambda i, j, t: (j, off // ka + i, 0))

    def body(a_ref, b_ref, *rest):
        o_ref, acc = rest[-2:]
        t = pl.program_id(2)
        av = a_ref[...]
        if av.dtype != BF16:
            av = av.astype(BF16)
        d = lax.dot_general(av, b_ref[...], TN, preferred_element_type=F32)

        @pl.when(t == 0)
        def _():
            acc[...] = d

        @pl.when(t > 0)
        def _():
            acc[...] += d

        @pl.when(t == nt - 1)
        def _():
            for s in range(spb):
                if row_sharded:
                    o_ref[s] = acc[s * rb:(s + 1) * rb, :].astype(BF16)
                else:
                    o_ref[s] = acc[:, s * (tn // spb):(s + 1) * (tn // spb)].astype(BF16)

    a_spec = (pl.BlockSpec((tt, ka), lambda i, j, t: (t, i)) if layer is None
              else pl.BlockSpec((None, tt, ka), lambda i, j, t: (layer, t, i)))
    ins = [a_spec, pl.BlockSpec((tt, tn), lambda i, j, t: (t, j))]
    args = [a, b]
    kw = {}
    if prev is not None:
        ins.append(ANY)
        args.append(prev)
        kw["input_output_aliases"] = {2: 0}
    return _pc(body, name, (Ka // ka, Nb // tn, nt), ins, out_spec, _sds(out_shape, BF16),
               scratch=[pltpu.VMEM((ka, tn), F32)], sem=("parallel", "parallel", "arbitrary"), **kw)(*args)


def _adamw_math(w, g, m, v):
    c1 = 1.0 - ADAM_B1 ** ADAM_STEP
    c2 = 1.0 - ADAM_B2 ** ADAM_STEP
    mn = ADAM_B1 * m + (1.0 - ADAM_B1) * g
    vn = ADAM_B2 * v + (1.0 - ADAM_B2) * (g * g)
    return -ADAM_LR * ((mn / c1) / (jnp.sqrt(vn / c2) + ADAM_EPS) + ADAM_WD * w), mn, vn


def adamw_layer(w, m, v, layer, gbuf, off, prev, name):
    L, R, W = w.shape
    tr = 256
    assert R % tr == 0 and off % tr == 0

    def body(w_ref, g_ref, m_ref, v_ref, *rest):
        go_ref, d_ref, mo_ref, vo_ref = rest[-4:]
        g = g_ref[...]
        go_ref[...] = g
        d_ref[...], mo_ref[...], vo_ref[...] = _adamw_math(w_ref[...], g, m_ref[...], v_ref[...])

    lay = pl.BlockSpec((None, tr, W), lambda r: (layer, r, 0))
    ins = [lay, pl.BlockSpec((tr, W), lambda r: (off // tr + r, 0)), lay, lay]
    args = [w, gbuf, m, v]
    kw = {}
    if prev is not None:
        ins += [ANY] * 4
        args += list(prev)
        kw["input_output_aliases"] = {4 + k: k for k in range(4)}
    return _pc(body, name, (R // tr,), ins, [lay] * 4, [_sds((L, R, W), F32)] * 4, sem=("parallel",), **kw)(*args)


def adamw_many(ws, gs, ms, vs):
    n = len(ws)

    def body(*refs):
        for k in range(n):
            d, mn, vn = _adamw_math(refs[k][...], refs[n + k][...], refs[2 * n + k][...], refs[3 * n + k][...])
            refs[4 * n + k][...] = d
            refs[5 * n + k][...] = mn
            refs[6 * n + k][...] = vn

    outs = pl.pallas_call(body, name="adamw_small", out_shape=[_sds(a.shape, F32) for a in ws] * 3)(*ws, *gs, *ms, *vs)
    return outs[:n], outs[n:2 * n], outs[2 * n:]


def _rope_tables(T):
    pos = jnp.arange(T, dtype=F32)
    inv_freq = ROPE_THETA ** (-jnp.arange(0, ROPE, 2, dtype=F32) / ROPE)
    ang = pos[:, None] * inv_freq[None, :]
    cos, sin = jnp.cos(ang), jnp.sin(ang)
    pad = HEAD - ROPE
    c = jnp.concatenate([cos, cos, jnp.ones((T, pad), F32)], axis=1)
    s = jnp.concatenate([-sin, sin, jnp.zeros((T, pad), F32)], axis=1)
    return jnp.stack([jnp.tile(c, (1, 128 // HEAD)), jnp.tile(s, (1, 128 // HEAD))])


def _local_step(x, p, target, W, small, lay, hook=None):
    if hook is None:
        hook = lambda stage, after, G, sg=None: None
    T, D = x.shape
    depth = small["mix_ln_g"].shape[0]
    alpha = float((2 * depth) ** 0.25)
    taps = small["taps"]
    row = lambda a, i: a[i:i + 1]
    cs = _rope_tables(T)

    h = conv_in_fwd(x, W["conv_w_in"], small["conv_b_in"])
    cv, s = dwconv_fwd(h, small["conv_w_dw"], small["conv_b_dw"], small["conv_ln_g"], small["conv_ln_b"], taps)
    hook("weights1", s, None)
    pre_mix0, x1, x1b = mm_res_ln(s, W["conv_w_out"], x, row(small["mix_ln_g"], 0), row(small["mix_ln_b"], 0), alpha,
                                  small["conv_b_out"], "conv_out_fwd")
    r0, t0 = mlp_up_fwd(x1b, W["mlp_w_up0"], "mlp_up_fwd0")
    pre_mlp0, x2, x2b = mm_res_ln(r0, W["mlp_w_down0"], x1, row(small["mlp_ln_g"], 0), row(small["mlp_ln_b"], 0), alpha,
                                  None, "mlp_down_fwd0")
    x3, x3b, pp0, gl0 = ple_fwd(x2, x2b, p, 0, W["ple_w_proj0"], W["ple_w_gate0"], None, "ple_fwd0")

    hook("weights2", x3b, None)
    q, k, v = qkv_fwd(x3b, W["attn_w_q"], W["kv_w_k"], W["kv_w_v"], cs)
    o = attn_fwd(q, k, v, small["attn_sinks"])
    pre_mix1, x4, x4b = mm_res_ln(o, W["attn_w_o"], x3, row(small["mix_ln_g"], 1), row(small["mix_ln_b"], 1), alpha,
                                  None, "attn_out_fwd")
    r1, t1 = mlp_up_fwd(x4b, W["mlp_w_up1"], "mlp_up_fwd1")
    pre_mlp1, x5, x5b = mm_res_ln(r1, W["mlp_w_down1"], x4, row(small["mlp_ln_g"], 1), row(small["mlp_ln_b"], 1), alpha,
                                  None, "mlp_down_fwd1")
    dx6, loss, pp1, gl1 = ple_fwd(x5, x5b, p, 1, W["ple_w_proj1"], W["ple_w_gate1"], target, "ple_fwd1")

    G, sg = {}, {}
    where = {n: (key, off) for key in lay for n, off, _ in lay[key]}
    rows_of = {key: sum(r for _, _, r in lay[key]) for key in lay}

    def wg(name, a, b, row_sharded):
        key, off = where[name]
        shape = (NS, rows_of[key], W[name][0].shape[2])
        G[key] = wgrad(a, b, row_sharded, "wg_" + name, (G.get(key), shape, off))

    dpp1, dgl1, dx5 = ple_bwd(dx6, pp1, gl1, W["ple_w_gate1"], "ple_bwd1")
    wg("ple_w_proj1", (1, p), dpp1, False)
    wg("ple_w_gate1", x5b, dgl1, True)
    dpre_mlp1, dpre_mlp1b, dm1, g_mlp_g1, g_mlp_b1 = mlp_bwd1(dx5, pre_mlp1, row(small["mlp_ln_g"], 1), t1,
                                                              W["mlp_w_down1"], "mlp_bwd1_1")
    wg("mlp_w_down1", r1, dpre_mlp1b, True)
    wg("mlp_w_up1", x4b, dm1, False)
    dpre_mix1, dpre_mix1b, do, g_mix_g1, g_mix_b1, _ = mlp_bwd2(dpre_mlp1, dm1, W["mlp_w_up1"], alpha, pre_mix1,
                                                                row(small["mix_ln_g"], 1), W["attn_w_o"], "mlp_bwd2_1")
    wg("attn_w_o", o, dpre_mix1b, True)
    dq, dk, dv, dsinks = attn_bwd(q, k, v, do, small["attn_sinks"])
    dqb, dkb, dvb, dx3 = qkv_bwd(dq, dk, dv, dpre_mix1,
                                 W["attn_w_q"], W["kv_w_k"], W["kv_w_v"], cs, alpha)
    wg("attn_w_q", x3b, dqb, True)
    wg("kv_w_k", x3b, dkb, True)
    wg("kv_w_v", x3b, dvb, True)
    hook("grads3", None, G)

    dpp0, dgl0, dx2 = ple_bwd(dx3, pp0, gl0, W["ple_w_gate0"], "ple_bwd0")
    wg("ple_w_proj0", (0, p), dpp0, False)
    wg("ple_w_gate0", x2b, dgl0, True)
    dpre_mlp0, dpre_mlp0b, dm0, g_mlp_g0, g_mlp_b0 = mlp_bwd1(dx2, pre_mlp0, row(small["mlp_ln_g"], 0), t0,
                                                              W["mlp_w_down0"], "mlp_bwd1_0")
    wg("mlp_w_down0", r0, dpre_mlp0b, True)
    wg("mlp_w_up0", x1b, dm0, False)
    hook("grads2", None, G)
    dpre_mix0, dpre_mix0b, dsw, g_mix_g0, g_mix_b0, g_b_out = mlp_bwd2(dpre_mlp0, dm0, W["mlp_w_up0"], alpha, pre_mix0,
                                                                      row(small["mix_ln_g"], 0), W["conv_w_out"],
                                                                      "mlp_bwd2_0")
    wg("conv_w_out", s, dpre_mix0b, True)
    hook("grads1", None, G)
    dcv, g_cln_g, g_cln_b, g_b_dw = conv_mid_bwd(dsw, cv, small["conv_ln_g"], small["conv_ln_b"])
    dh, g_w_dw, g_b_in = dwconv_bwd(dcv, h, small["conv_w_dw"], taps)
    wg("conv_w_in", x, dh, False)

    sg["conv_b_in"] = g_b_in
    sg["conv_w_dw"] = g_w_dw
    sg["conv_b_dw"], sg["conv_ln_g"], sg["conv_ln_b"], sg["conv_b_out"] = g_b_dw, g_cln_g, g_cln_b, g_b_out
    sg["mix_ln_g"] = [g_mix_g0, g_mix_g1]
    sg["mix_ln_b"] = [g_mix_b0, g_mix_b1]
    sg["mlp_ln_g"] = [g_mlp_g0, g_mlp_g1]
    sg["mlp_ln_b"] = [g_mlp_b0, g_mlp_b1]
    sg["attn_sinks"] = dsinks[:, 0][None, :]
    sg["loss"] = loss
    hook("grads0", None, G, sg)
    grad_x = conv_in_bwd(dh, dpre_mix0, W["conv_w_in"], alpha)
    return loss, grad_x, G, sg


BUFFERS = (("b0", ("conv_w_in",)), ("a0", ("conv_w_out",)),
           ("a1", ("mlp_w_up0", "mlp_w_down0", "ple_w_gate0")), ("c1", ("ple_w_proj0",)),
           ("a2", ("mlp_w_up1", "mlp_w_down1", "ple_w_gate1", "attn_w_q", "attn_w_o")),
           ("c2", ("kv_w_k", "kv_w_v", "ple_w_proj1")))
GROUPS = (("b0",), ("a0", "a1", "c1"), ("a2", "c2"))
REDUCED = (("b0",), ("a0",), ("a1", "c1"), ("a2", "c2"))
ROW_SHARDED = {"mlp_w_down0", "mlp_w_down1", "ple_w_gate0", "ple_w_gate1", "conv_w_out", "attn_w_q", "attn_w_o", "kv_w_k",
               "kv_w_v"}


def _split_layers(weights):
    out = {"conv_w_in": weights["conv_w_in"][0], "conv_w_out": weights["conv_w_out"][0],
           "attn_w_q": weights["attn_w_q"][0], "attn_w_o": weights["attn_w_o"][0],
           "kv_w_k": weights["kv_w_k"], "kv_w_v": weights["kv_w_v"]}
    for n in ("mlp_w_up", "mlp_w_down", "ple_w_proj", "ple_w_gate"):
        for i in range(weights[n].shape[0]):
            out[n + str(i)] = weights[n][i]
    return out


def _layout(shards):
    lay = {}
    for key, names in BUFFERS:
        off, rows = 0, []
        for n in names:
            rows.append((n, off, shards[n].shape[0]))
            off += shards[n].shape[0]
        lay[key] = rows
    return lay


def _place():
    return lax.axis_index("x"), lax.axis_index("y"), lax.axis_index("c")


def _flip(v, f):
    return (v + f) % 2 if f else v


CHIP_FLIPS = ((1, 0), (0, 1), (1, 1))


HBM = pl.BlockSpec(memory_space=pltpu.HBM)
SEM = pl.BlockSpec(memory_space=pltpu.SEMAPHORE)
EFFECT = pltpu.SideEffectType.DATAFLOW_SIDE_EFFECTING


def _half(ref, rows, c):
    return ref.at[pl.ds(pl.multiple_of(c * (rows // 2), 16), rows // 2), :]


def _gather_copies(refs, shapes, whole, send, recv):
    x, y, c = _place()
    me = 2 * x + y
    na = len(refs)
    cps = []
    for d, (fx, fy) in enumerate(CHIP_FLIPS):
        to = (_flip(x, fx), _flip(y, fy), c)
        for k in range(na):
            mine = refs[k].at[me] if k >= na - whole else _half(refs[k].at[me], shapes[k][1], c)
            cps.append(pltpu.make_async_remote_copy(mine, mine, send.at[d * na + k], recv.at[d * na + k], device_id=to,
                                                    device_id_type=MESH))
    return cps


def gather_start(bufs, whole, after, name):
    na = len(bufs)
    shapes = [b.shape for b in bufs]
    nsem = len(CHIP_FLIPS) * na

    def body(*refs):
        ins = refs[:na]
        send, recv = refs[-(na + 3)], refs[-(na + 2)]
        token = refs[-1]
        for cp in _gather_copies(ins, shapes, whole, send, recv):
            cp.start()
        token[...] = jnp.zeros_like(token)

    args = [pltpu.with_memory_space_constraint(b, pltpu.HBM) for b in bufs]
    ins = [HBM] * na
    if after is not None:
        args.append(after)
        ins.append(ANY)
    return pl.pallas_call(
        body, name=name, in_specs=ins,
        out_specs=[SEM, SEM] + [HBM] * na + [pl.BlockSpec(memory_space=pltpu.VMEM)],
        out_shape=[pltpu.SemaphoreType.DMA((nsem,)), pltpu.SemaphoreType.DMA((nsem,))]
        + [pltpu.HBM(b.shape, b.dtype) for b in bufs] + [_sds((8, 128), F32)],
        input_output_aliases={k: k + 2 for k in range(na)},
        compiler_params=pltpu.CompilerParams(has_side_effects=EFFECT))(*args)


def gather_wait(send, recv, bufs, whole, after, name):
    na = len(bufs)
    shapes = [b.shape for b in bufs]

    def body(*refs):
        ins = refs[:na]
        send_ref, recv_ref = refs[na], refs[na + 1]
        for cp in _gather_copies(ins, shapes, whole, send_ref, recv_ref):
            cp.wait_send()
            cp.wait_recv()

    return pl.pallas_call(
        body, name=name, in_specs=[HBM] * na + [SEM, SEM, ANY], out_specs=[HBM] * na,
        out_shape=[pltpu.HBM(b.shape, b.dtype) for b in bufs], input_output_aliases={k: k for k in range(na)},
        compiler_params=pltpu.CompilerParams(has_side_effects=EFFECT))(*bufs, send, recv, after)


def sibling_forward(bufs, name):
    nb = len(bufs)

    def body(*refs):
        outs = refs[nb:2 * nb]
        send, recv = refs[2 * nb:]
        x, y, c = _place()
        cps = []
        for d, (fx, fy) in enumerate(CHIP_FLIPS):
            frm = 2 * _flip(x, fx) + _flip(y, fy)
            for k in range(nb):
                theirs = _half(outs[k].at[frm], bufs[k].shape[1], c)
                cps.append(pltpu.make_async_remote_copy(theirs, theirs, send.at[d * nb + k], recv.at[d * nb + k],
                                                        device_id=(x, y, 1 - c), device_id_type=MESH))
        for cp in cps:
            cp.start()
        for cp in cps:
            cp.wait()

    nsem = len(CHIP_FLIPS) * nb
    return pl.pallas_call(
        body, name=name, in_specs=[ANY] * nb, out_specs=[ANY] * nb, out_shape=[_sds(b.shape, b.dtype) for b in bufs],
        input_output_aliases={k: k for k in range(nb)},
        scratch_shapes=[pltpu.SemaphoreType.DMA((nsem,)), pltpu.SemaphoreType.DMA((nsem,))])(*bufs)


def pack_rows(pieces, rows, width, name):
    def body(*refs):
        o_ref = refs[-1]
        o_ref[...] = jnp.zeros_like(o_ref)
        for ref, (a, off) in zip(refs[:-1], pieces):
            o_ref[off:off + a.shape[0], 0:a.shape[1]] = ref[...]

    return pl.pallas_call(body, name=name, out_shape=_sds((rows, width), F32))(*[a for a, _ in pieces])


PEER_FLIPS = tuple((fx, fy, fc) for fx in (0, 1) for fy in (0, 1) for fc in (0, 1) if fx or fy or fc)


def _reduce_copies(parts, zones, pack, send, recv):
    x, y, c = _place()
    nb = len(parts)
    na = nb + (1 if pack is not None else 0)
    cps = []
    for f, (fx, fy, fc) in enumerate(PEER_FLIPS):
        tx, ty, tc = _flip(x, fx), _flip(y, fy), _flip(c, fc)
        for k in range(nb):
            hrows = parts[k].shape[1] // 2
            piece = parts[k].at[2 * tx + ty, pl.ds(pl.multiple_of(tc * hrows, 16), hrows), :]
            cps.append(pltpu.make_async_remote_copy(piece, zones[k].at[f], send.at[f * na + k], recv.at[f * na + k],
                                                    device_id=(tx, ty, tc), device_id_type=MESH))
        if pack is not None:
            mine = pack.at[4 * x + 2 * y + c]
            cps.append(pltpu.make_async_remote_copy(mine, mine, send.at[f * na + nb], recv.at[f * na + nb],
                                                    device_id=(tx, ty, tc), device_id_type=MESH))
    return cps


def reduce_begin(parts, pack, name):
    nb = len(parts)
    zones = [lax.empty((len(PEER_FLIPS), g.shape[1] // 2, g.shape[2]), g.dtype) for g in parts]
    arrs = list(parts) + zones + ([pack] if pack is not None else [])
    na = len(arrs)
    nsem = len(PEER_FLIPS) * (nb + (1 if pack is not None else 0))

    def body(*refs):
        ins = refs[:na]
        send, recv = refs[na], refs[na + 1]
        for cp in _reduce_copies(ins[:nb], ins[nb:2 * nb], ins[2 * nb] if pack is not None else None, send, recv):
            cp.start()
        refs[-1][...] = jnp.zeros_like(refs[-1])

    return pl.pallas_call(
        body, name=name, in_specs=[HBM] * na,
        out_specs=[SEM, SEM] + [HBM] * na + [pl.BlockSpec(memory_space=pltpu.VMEM)],
        out_shape=[pltpu.SemaphoreType.DMA((nsem,)), pltpu.SemaphoreType.DMA((nsem,))]
        + [pltpu.HBM(a.shape, a.dtype) for a in arrs] + [_sds((8, 128), F32)],
        input_output_aliases={k: k + 2 for k in range(na)},
        compiler_params=pltpu.CompilerParams(has_side_effects=EFFECT))(
            *[pltpu.with_memory_space_constraint(a, pltpu.HBM) for a in arrs])


def reduce_end(send, recv, parts, zones, pack, after, name):
    nb = len(parts)
    arrs = list(parts) + list(zones) + ([pack] if pack is not None else [])
    na = len(arrs)

    def body(*refs):
        ins = refs[:na]
        for cp in _reduce_copies(ins[:nb], ins[nb:2 * nb], ins[2 * nb] if pack is not None else None, refs[na], refs[na + 1]):
            cp.wait_send()
            cp.wait_recv()

    return pl.pallas_call(
        body, name=name, in_specs=[HBM] * na + [SEM, SEM, ANY], out_specs=[HBM] * na,
        out_shape=[pltpu.HBM(a.shape, a.dtype) for a in arrs], input_output_aliases={k: k for k in range(na)},
        compiler_params=pltpu.CompilerParams(has_side_effects=EFFECT))(*arrs, send, recv, after)


def sibling_share(halves, name):
    nb = len(halves)

    def body(*refs):
        outs = refs[nb:2 * nb]
        send, recv = refs[2 * nb:]
        x, y, c = _place()
        cps = []
        for k in range(nb):
            hrows = halves[k].shape[0] // 2
            mine = outs[k].at[pl.ds(pl.multiple_of(c * hrows, 8), hrows), :]
            cps.append(pltpu.make_async_remote_copy(mine, mine, send.at[k], recv.at[k], device_id=(x, y, 1 - c),
                                                    device_id_type=MESH))
        for cp in cps:
            cp.start()
        for cp in cps:
            cp.wait()

    return pl.pallas_call(
        body, name=name, in_specs=[ANY] * nb, out_specs=[ANY] * nb,
        out_shape=[_sds(h.shape, h.dtype) for h in halves], input_output_aliases={k: k for k in range(nb)},
        scratch_shapes=[pltpu.SemaphoreType.DMA((nb,)), pltpu.SemaphoreType.DMA((nb,))])(*halves)


def _row_tile(rows):
    for cand in (512, 384, 256, 128, 64, 32, 16):
        if rows % cand == 0:
            return cand
    return rows


def piece_sum(g, z, idx, name):
    _, hrows, W = z.shape
    tr = _row_tile(hrows)
    nrb = hrows // tr

    def body(idx_ref, g_ref, z_ref, o_ref):
        acc = g_ref[...].astype(F32)
        for d in range(z.shape[0]):
            acc = acc + z_ref[d].astype(F32)
        o_ref[...] = acc

    gs = pltpu.PrefetchScalarGridSpec(
        num_scalar_prefetch=1, grid=(nrb,),
        in_specs=[pl.BlockSpec((None, tr, W), lambda i, sc: (sc[0], sc[1] * nrb + i, 0)),
                  pl.BlockSpec((z.shape[0], tr, W), lambda i, sc: (0, i, 0))],
        out_specs=pl.BlockSpec((tr, W), lambda i, sc: (sc[1] * nrb + i, 0)))
    return pl.pallas_call(body, name=name, grid_spec=gs, out_shape=_sds((2 * hrows, W), F32),
                          compiler_params=pltpu.CompilerParams(dimension_semantics=("parallel",),
                                                               vmem_limit_bytes=48 * 2 ** 20))(idx, g, z)


def small_sum(packs):
    n, R, W = packs.shape

    def body(p_ref, o_ref):
        acc = p_ref[0]
        for d in range(1, n):
            acc = acc + p_ref[d]
        o_ref[...] = acc

    return pl.pallas_call(body, name="small_sum", out_shape=_sds((R, W), F32))(packs)


WEIGHTS = ["conv_w_in", "conv_b_in", "conv_w_dw", "conv_b_dw", "conv_ln_g", "conv_ln_b", "conv_w_out", "conv_b_out", "kv_w_k",
           "kv_w_v", "attn_w_q", "attn_sinks", "attn_w_o", "mix_ln_g", "mix_ln_b", "mlp_w_up", "mlp_w_down", "mlp_ln_g",
           "mlp_ln_b", "ple_w_proj", "ple_w_gate"]
BIG = ["conv_w_in", "conv_w_out", "kv_w_k", "kv_w_v", "attn_w_q", "attn_w_o", "mlp_w_up", "mlp_w_down", "ple_w_proj",
       "ple_w_gate"]
SMALL = [n for n in WEIGHTS if n not in BIG]


def _step(x, p, target, w, m, v):
    D = x.shape[-1]
    ds = D // NS
    xq, yq, cq = _place()
    chip = 2 * xq + yq
    idx = jnp.stack([chip, cq]).astype(jnp.int32)

    shards = _split_layers(w)
    lay = _layout(shards)
    taps = w["conv_w_dw"].shape[1]
    small_loc = pack_rows([(w["conv_w_dw"][0], 0), (w["conv_b_dw"], HALO), (w["conv_ln_g"], HALO + 1), (w["conv_ln_b"], HALO + 2),
                           (w["conv_b_out"], HALO + 3), (w["conv_b_in"].reshape(2, ds), HALO + 4)], HALO + 8, ds, "pack_small")
    slot = lambda a: lax.dynamic_update_slice(lax.empty((NS,) + a.shape, a.dtype), a[None], (chip, 0, 0))
    started, token = [], None
    for gi, keys in enumerate(GROUPS):
        bufs = [slot(jnp.concatenate([shards[n].astype(BF16) for n, _, _ in lay[key]], axis=0)) for key in keys]
        if gi == 0:
            bufs.append(slot(small_loc))
        send, recv, *thru, token = gather_start(bufs, 1 if gi == 0 else 0, token, "gather_start%d" % gi)
        started.append((send, recv, thru))
    W = {}

    def arrive(gi, after):
        send, recv, thru = started[gi]
        whole = 1 if gi == 0 else 0
        got = gather_wait(send, recv, thru, whole, after, "gather_wait%d" % gi)
        nk = len(GROUPS[gi])
        for key, buf in zip(GROUPS[gi], sibling_forward(got[:nk], "sibling_forward%d" % gi)):
            for n, off, rows in lay[key]:
                W[n] = (buf, off, rows)
        return got[nk:]

    gs, = arrive(0, token)
    across = lambda rows: gs[:, rows, :].transpose(1, 0, 2).reshape(rows.stop - rows.start, D)
    small = {"taps": taps, "conv_w_dw": across(slice(0, HALO)), "conv_b_dw": across(slice(HALO, HALO + 1)),
             "conv_ln_g": across(slice(HALO + 1, HALO + 2)), "conv_ln_b": across(slice(HALO + 2, HALO + 3)),
             "conv_b_out": across(slice(HALO + 3, HALO + 4)), "conv_b_in": gs[:, HALO + 4:HALO + 6, :].reshape(1, 2 * D),
             "attn_sinks": w["attn_sinks"], "mix_ln_g": w["mix_ln_g"], "mix_ln_b": w["mix_ln_b"],
             "mlp_ln_g": w["mlp_ln_g"], "mlp_ln_b": w["mlp_ln_b"]}

    reducing = {}

    def reduce_start(gi, G, pack):
        nk = len(REDUCED[gi])
        send, recv, *thru, token = reduce_begin([G[key] for key in REDUCED[gi]], pack, "reduce_begin%d" % gi)
        reducing[gi] = (send, recv, thru[:nk], thru[nk:2 * nk], thru[2 * nk] if pack is not None else None)
        _FOLLOW.append(token)

    def small_pack(sg):
        pieces = [(sg["conv_b_in"].reshape(2, D), 0), (sg["conv_w_dw"], 2)]
        r0 = 2 + HALO
        for i, n in enumerate(("conv_b_dw", "conv_ln_g", "conv_ln_b", "conv_b_out")):
            pieces.append((sg[n], r0 + i))
        r0 += 4
        for i, n in enumerate(("mix_ln_g", "mix_ln_b", "mlp_ln_g", "mlp_ln_b")):
            pieces += [(sg[n][0], r0 + 2 * i), (sg[n][1], r0 + 2 * i + 1)]
        pieces += [(sg["attn_sinks"], r0 + 8), (sg["loss"][0:1], r0 + 9)]
        mine = pack_rows(pieces, r0 + 10, D, "pack_small_grads")
        return lax.dynamic_update_slice(lax.empty((8,) + mine.shape, F32), mine[None], (4 * xq + 2 * yq + cq, 0, 0))

    def hook(stage, after, G, sg=None):
        if stage == "weights1":
            arrive(1, after)
        elif stage == "weights2":
            arrive(2, after)
        elif stage == "grads0":
            reduce_start(0, G, small_pack(sg))
        elif stage.startswith("grads"):
            reduce_start(int(stage[5:]), G, None)

    _, grad_x, G, sg = _local_step(x[0], p[:, 0], target[0], W, small, lay, hook)
    _FOLLOW.clear()
    nsink = w["attn_sinks"].shape[1]

    grads, delta, new_m, new_v = {}, {}, {}, {}
    found = {}

    def finish(groups, after, tag):
        keys, halves, tot = [], [], None
        for gi in groups:
            send, recv, parts, zones, pack = reducing[gi]
            done = reduce_end(send, recv, parts, zones, pack, after, "reduce_end%d" % gi)
            nk = len(REDUCED[gi])
            for key, g_, z_ in zip(REDUCED[gi], done[:nk], done[nk:2 * nk]):
                keys.append(key)
                halves.append(piece_sum(g_, z_, idx, "piece_sum_" + key))
            if pack is not None:
                tot = small_sum(done[2 * nk])
        for key, buf in zip(keys, sibling_share(halves, "sibling_share" + tag)):
            for n, off, _ in lay[key]:
                found[n] = (buf, off)
        return tot

    def big_adamw(names):
        for n in names:
            three = lambda a: a.reshape((-1,) + a.shape[-2:])
            w3, m3, v3 = three(w[n]), three(m[n]), three(v[n])
            outs = None
            for i in range(w3.shape[0]):
                buf, off = found[n + str(i)] if n + str(i) in found else found[n]
                outs = adamw_layer(w3, m3, v3, i, buf, off, outs, "adamw_%s%d" % (n, i))
            grads[n], delta[n], new_m[n], new_v[n] = [a.reshape(w[n].shape) for a in outs]

    last = [n for n, _, _ in lay[REDUCED[0][0]]]
    finish(reversed(range(1, len(REDUCED))), grad_x, "1")
    big_adamw([n for n in BIG if n not in last])
    tot = finish([0], new_v["mlp_w_down"], "0")
    big_adamw(last)
    cols = lambda rows: lax.dynamic_slice(rows, (0, chip * ds), (rows.shape[0], ds))
    grads["conv_b_in"] = lax.dynamic_slice(tot[0:2].reshape(1, 2 * D), (0, chip * 2 * ds), (1, 2 * ds))
    grads["conv_w_dw"] = cols(tot[2:2 + taps])[None]
    r0 = 2 + HALO
    for i, n in enumerate(("conv_b_dw", "conv_ln_g", "conv_ln_b", "conv_b_out")):
        grads[n] = cols(tot[r0 + i:r0 + i + 1])
    r0 += 4
    for i, n in enumerate(("mix_ln_g", "mix_ln_b", "mlp_ln_g", "mlp_ln_b")):
        grads[n] = tot[r0 + 2 * i:r0 + 2 * i + 2]
    grads["attn_sinks"] = tot[r0 + 8:r0 + 9, 0:nsink]

    ds_, ms_, vs_ = adamw_many([w[n] for n in SMALL], [grads[n] for n in SMALL], [m[n] for n in SMALL], [v[n] for n in SMALL])
    for n, d_, m_, v_ in zip(SMALL, ds_, ms_, vs_):
        delta[n], new_m[n], new_v[n] = d_, m_, v_

    total = tot[r0 + 9, 0]
    return (total, grad_x[None], *[grads[n] for n in WEIGHTS], *[delta[n] for n in WEIGHTS], *[new_m[n] for n in WEIGHTS],
            *[new_v[n] for n in WEIGHTS])


def kernel(x, p, conv_w_in, conv_b_in, conv_w_dw, conv_b_dw, conv_ln_g, conv_ln_b, conv_w_out, conv_b_out, kv_w_k, kv_w_v, attn_w_q, attn_sinks, attn_w_o, mix_ln_g, mix_ln_b, mlp_w_up, mlp_w_down, mlp_ln_g, mlp_ln_b, ple_w_proj, ple_w_gate, loss_target, m_conv_w_in, m_conv_b_in, m_conv_w_dw, m_conv_b_dw, m_conv_ln_g, m_conv_ln_b, m_conv_w_out, m_conv_b_out, m_kv_w_k, m_kv_w_v, m_attn_w_q, m_attn_sinks, m_attn_w_o, m_mix_ln_g, m_mix_ln_b, m_mlp_w_up, m_mlp_w_down, m_mlp_ln_g, m_mlp_ln_b, m_ple_w_proj, m_ple_w_gate, v_conv_w_in, v_conv_b_in, v_conv_w_dw, v_conv_b_dw, v_conv_ln_g, v_conv_ln_b, v_conv_w_out, v_conv_b_out, v_kv_w_k, v_kv_w_v, v_attn_w_q, v_attn_sinks, v_attn_w_o, v_mix_ln_g, v_mix_ln_b, v_mlp_w_up, v_mlp_w_down, v_mlp_ln_g, v_mlp_ln_b, v_ple_w_proj, v_ple_w_gate):
    w = dict(zip(WEIGHTS, (conv_w_in, conv_b_in, conv_w_dw, conv_b_dw, conv_ln_g, conv_ln_b, conv_w_out, conv_b_out, kv_w_k,
                           kv_w_v, attn_w_q, attn_sinks, attn_w_o, mix_ln_g, mix_ln_b, mlp_w_up, mlp_w_down, mlp_ln_g, mlp_ln_b,
                           ple_w_proj, ple_w_gate)))
    m = dict(zip(WEIGHTS, (m_conv_w_in, m_conv_b_in, m_conv_w_dw, m_conv_b_dw, m_conv_ln_g, m_conv_ln_b, m_conv_w_out,
                           m_conv_b_out, m_kv_w_k, m_kv_w_v, m_attn_w_q, m_attn_sinks, m_attn_w_o, m_mix_ln_g, m_mix_ln_b,
                           m_mlp_w_up, m_mlp_w_down, m_mlp_ln_g, m_mlp_ln_b, m_ple_w_proj, m_ple_w_gate)))
    v = dict(zip(WEIGHTS, (v_conv_w_in, v_conv_b_in, v_conv_w_dw, v_conv_b_dw, v_conv_ln_g, v_conv_ln_b, v_conv_w_out,
                           v_conv_b_out, v_kv_w_k, v_kv_w_v, v_attn_w_q, v_attn_sinks, v_attn_w_o, v_mix_ln_g, v_mix_ln_b,
                           v_mlp_w_up, v_mlp_w_down, v_mlp_ln_g, v_mlp_ln_b, v_ple_w_proj, v_ple_w_gate)))
    return _step(x, p, loss_target, w, m, v)
```

```python
import jax
import jax.numpy as jnp
from jax import lax
from jax.experimental import pallas as pl
from jax.experimental.pallas import tpu as pltpu

F32 = jnp.float32
BF16 = jnp.bfloat16
NS = 4
HEAD = 64
BLK = 128
ROPE = 16
ROPE_THETA = 500000.0
LN_EPS = 1e-5
NEG = -1e30
KV_PER_STAGE = 1
HALO = 32
ADAM_LR, ADAM_B1, ADAM_B2, ADAM_EPS, ADAM_WD, ADAM_STEP = 0.001, 0.9, 0.999, 1e-08, 0.01, 10
MESH = pl.DeviceIdType.MESH
ANY = pl.BlockSpec(memory_space=pl.ANY)
NT = (((1,), (1,)), ((), ()))
TN = (((0,), (0,)), ((), ()))


_FOLLOW = []


def _pc(body, name, grid, in_specs, out_specs, out_shape, scratch=(), sem=None, vmem=56, **kw):
    call = lambda fn, ins: pl.pallas_call(
        fn, name=name, grid=grid, in_specs=ins, out_specs=out_specs, out_shape=out_shape,
        scratch_shapes=list(scratch),
        compiler_params=pltpu.CompilerParams(dimension_semantics=sem, vmem_limit_bytes=vmem * 2 ** 20), **kw)
    if not _FOLLOW:
        return call(body, in_specs)
    extra = list(_FOLLOW)
    _FOLLOW.clear()
    n_in = len(in_specs)

    def ordered(*refs):
        return body(*refs[:n_in], *refs[n_in + len(extra):])

    run = call(ordered, list(in_specs) + [ANY] * len(extra))
    return lambda *args: run(*args, *extra)


def _rows(tm, n):
    return pl.BlockSpec((tm, n), lambda i: (i, 0))


def _const(shape):
    return pl.BlockSpec(shape, lambda *_: (0,) * len(shape))


def _wspec(w):
    buf, off, rows = w
    assert off % rows == 0
    return pl.BlockSpec((NS, rows, buf.shape[2]), lambda *_: (0, off // rows, 0))


def _rows_joined(w_ref):
    n, r, c = w_ref.shape
    return w_ref[...].reshape(n * r, c)


def _sds(shape, dtype):
    return jax.ShapeDtypeStruct(shape, dtype)


def _tile(t, rows=256):
    return min(rows, t)


def _sigmoid(x):
    return 0.5 * jnp.tanh(0.5 * x) + 0.5


def _ln_stats(w):
    mu = jnp.mean(w, axis=-1, keepdims=True)
    xc = w - mu
    var = jnp.mean(xc * xc, axis=-1, keepdims=True)
    rstd = lax.rsqrt(var + LN_EPS)
    return xc * rstd, rstd, mu


def _ln_bwd(dy, w, g, stats=None):
    if stats is None:
        xhat, rstd, _ = _ln_stats(w)
    else:
        mu, rstd = stats
        xhat = (w - mu) * rstd
    dxhat = dy * g
    m1 = jnp.mean(dxhat, axis=-1, keepdims=True)
    m2 = jnp.mean(dxhat * xhat, axis=-1, keepdims=True)
    dw = rstd * (dxhat - m1 - xhat * m2)
    return dw, jnp.sum(dy * xhat, axis=0, keepdims=True), jnp.sum(dy, axis=0, keepdims=True)


def _acc_rows(ref, val, first):
    @pl.when(first)
    def _():
        ref[...] = val

    @pl.when(jnp.logical_not(first))
    def _():
        ref[...] += val


def conv_in_fwd(xb, w_in, b_in):
    T, D = xb.shape
    nw = w_in[0].shape[2]
    tm = _tile(T, 512)

    def body(x_ref, w_ref, b_ref, h_ref):
        x = x_ref[...].astype(BF16)
        for j in range(NS):
            sl = slice(j * nw, (j + 1) * nw)
            h_ref[:, sl] = (jnp.dot(x, w_ref[j], preferred_element_type=F32) + b_ref[:, sl]).astype(BF16)

    return _pc(body, "conv_in_fwd", (T // tm,), [_rows(tm, D), _wspec(w_in), _const((1, NS * nw))],
               _rows(tm, NS * nw), _sds((T, NS * nw), BF16), sem=("parallel",))(xb, w_in[0], b_in)


CONV_ROWS = 16


def _phases(scr, sh):
    n = scr.shape[0] - 8
    for b in range(1, 8):
        sh[b - 1, 0:n, :] = scr[b:b + n, :]


def _spread(w_ref, wb, taps):
    for j in range(taps):
        wb[j] = jnp.broadcast_to(w_ref[j:j + 1, :], wb.shape[1:])


def _tap(scr, sh, o, n):
    b = o % 8
    return scr[o:o + n, :] if b == 0 else sh[b - 1, o - b:o - b + n, :]


def dwconv_fwd(h, w_dw, b_dw, ln_g, ln_b, taps):
    T = h.shape[0]
    C = h.shape[1] // 2
    tq = _tile(T)
    nh = tq // HALO
    off = HALO - (taps - 1)

    def body(a_ref, g_ref, ap_ref, gp_ref, w_ref, bdw_ref, lg_ref, lb_ref, cv_ref, s_ref, scr, sh, wb):
        i = pl.program_id(0)
        scr[HALO:HALO + tq, :] = a_ref[...].astype(F32) * _sigmoid(g_ref[...].astype(F32))
        up = ap_ref[...].astype(F32) * _sigmoid(gp_ref[...].astype(F32))
        scr[0:HALO, :] = jnp.where(i > 0, up, 0.0)
        _phases(scr, sh)
        _spread(w_ref, wb, taps)
        bias = jnp.broadcast_to(bdw_ref[...], (8, C))
        for r in range(tq // CONV_ROWS):
            accs = [bias] * (CONV_ROWS // 8)
            for j in range(taps):
                wj = wb[j]
                accs = [acc + wj * _tap(scr, sh, off + j + r * CONV_ROWS + 8 * k, 8) for k, acc in enumerate(accs)]
            for k, acc in enumerate(accs):
                cv_ref[r * CONV_ROWS + 8 * k:r * CONV_ROWS + 8 * k + 8, :] = acc
        xhat, _, _ = _ln_stats(cv_ref[...])
        ln = xhat * lg_ref[...] + lb_ref[...]
        s_ref[...] = (ln * _sigmoid(ln)).astype(BF16)

    prev = lambda col: pl.BlockSpec((HALO, C), lambda i: (jnp.maximum(i * nh - 1, 0), col))
    cur = lambda col: pl.BlockSpec((tq, C), lambda i: (i, col))
    return _pc(body, "dwconv_fwd", (T // tq,),
               [cur(0), cur(1), prev(0), prev(1), _const((HALO, C)), _const((1, C)), _const((1, C)), _const((1, C))],
               [_rows(tq, C), _rows(tq, C)], [_sds((T, C), F32), _sds((T, C), BF16)],
               scratch=[pltpu.VMEM((HALO + tq, C), F32), pltpu.VMEM((7, HALO + tq, C), F32), pltpu.VMEM((taps, 8, C), F32)],
               sem=("parallel",))(h, h, h, h, w_dw, b_dw, ln_g, ln_b)


def mm_res_ln(a, w, res, g, b, alpha, bias, name):
    T, K = a.shape
    D = res.shape[1]
    tm = _tile(T, 512)

    def body(*refs):
        a_ref, w_ref, res_ref, g_ref, b_ref = refs[:5]
        n = 5
        if bias is not None:
            bias_ref = refs[5]
            n = 6
        pre_ref, mu_ref, rs_ref, xo_ref, xb_ref = refs[n:n + 5]
        acc = jnp.dot(a_ref[...], _rows_joined(w_ref), preferred_element_type=F32)
        if bias is not None:
            acc = acc + bias_ref[...]
        pre = alpha * res_ref[...] + acc
        xhat, rstd, mu = _ln_stats(pre)
        xo = xhat * g_ref[...] + b_ref[...]
        pre_ref[...] = pre
        mu_ref[...] = mu
        rs_ref[...] = rstd
        xo_ref[...] = xo
        xb_ref[...] = xo.astype(BF16)

    ins = [_rows(tm, K), _wspec(w), _rows(tm, D), _const((1, D)), _const((1, D))]
    args = [a, w[0], res, g, b]
    if bias is not None:
        ins.append(_const((1, D)))
        args.append(bias)
    pre, mu, rstd, xo, xb = _pc(
        body, name, (T // tm,), ins, [_rows(tm, D), _rows(tm, 1), _rows(tm, 1), _rows(tm, D), _rows(tm, D)],
        [_sds((T, D), F32), _sds((T, 1), F32), _sds((T, 1), F32), _sds((T, D), F32), _sds((T, D), BF16)],
        sem=("parallel",))(*args)
    return (pre, mu, rstd), xo, xb


def mlp_up_fwd(xb, w_up, name):
    T, D = xb.shape
    fs = w_up[0].shape[2]
    tm = _tile(T, 512)

    def body(x_ref, w_ref, r_ref, t_ref):
        x = x_ref[...]
        for j in range(NS):
            sl = slice(j * fs, (j + 1) * fs)
            m = jnp.maximum(jnp.dot(x, w_ref[j], preferred_element_type=F32), 0.0)
            r_ref[:, sl] = (m * m).astype(BF16)
            t_ref[:, sl] = (2.0 * m).astype(BF16)

    return _pc(body, name, (T // tm,), [_rows(tm, D), _wspec(w_up)], [_rows(tm, NS * fs)] * 2,
               [_sds((T, NS * fs), BF16)] * 2, sem=("parallel",))(xb, w_up[0])


def ple_fwd(x, xb, p, layer, w_proj, w_gate, target, name):
    T, D = x.shape
    P = p.shape[2]
    ds = D // NS
    tm = _tile(T, 512)
    last = target is not None

    def body(*refs):
        x_ref, xb_ref, p_ref, wp_ref, wg_ref = refs[:5]
        n = 5
        if last:
            t_ref = refs[5]
            n = 6
        o_ref, o2_ref, pp_ref, gl_ref = refs[n:n + 4]
        gl = jnp.dot(xb_ref[...], _rows_joined(wg_ref), preferred_element_type=F32)
        gl_ref[...] = gl.astype(BF16)
        sg = _sigmoid(gl)
        pb = p_ref[...].astype(BF16)
        sq = jnp.zeros((1, 1), F32)
        for j in range(NS):
            sl = slice(j * ds, (j + 1) * ds)
            pp = jnp.dot(pb, wp_ref[j], preferred_element_type=F32)
            pp_ref[:, sl] = pp.astype(BF16)
            out = x_ref[:, sl] + pp * sg[:, sl]
            if last:
                err = out - t_ref[:, sl]
                o_ref[:, sl] = err * (1.0 / D)
                e2 = jnp.sum(err * err, axis=0, keepdims=True)
                sq = sq + jnp.sum(e2, axis=1, keepdims=True)
            else:
                o_ref[:, sl] = out
                o2_ref[:, sl] = out.astype(BF16)
        if last:
            _acc_rows(o2_ref, jnp.broadcast_to(sq * (0.5 / D), (8, 128)), pl.program_id(0) == 0)

    ins = [_rows(tm, D), _rows(tm, D), pl.BlockSpec((None, tm, P), lambda i: (layer, i, 0)), _wspec(w_proj), _wspec(w_gate)]
    args = [x, xb, p, w_proj[0], w_gate[0]]
    if last:
        ins.append(_rows(tm, D))
        args.append(target)
        outs = [_rows(tm, D), _const((8, 128)), _rows(tm, D), _rows(tm, D)]
        shapes = [_sds((T, D), F32), _sds((8, 128), F32), _sds((T, D), BF16), _sds((T, D), BF16)]
    else:
        outs = [_rows(tm, D)] * 4
        shapes = [_sds((T, D), F32), _sds((T, D), BF16), _sds((T, D), BF16), _sds((T, D), BF16)]
    return _pc(body, name, (T // tm,), ins, outs, shapes, sem=("arbitrary",) if last else ("parallel",))(*args)


def _rope(x, cs_ref, sign):
    c = cs_ref[0]
    s = cs_ref[1] * sign
    lane = lax.broadcasted_iota(jnp.int32, c.shape, 1)
    first = (lane % HEAD) < (ROPE // 2)
    outs = []
    for gq in range(x.shape[1] // 128):
        xg = x[:, gq * 128:(gq + 1) * 128]
        sw = jnp.where(first, pltpu.roll(xg, 128 - ROPE // 2, 1), pltpu.roll(xg, ROPE // 2, 1))
        outs.append(xg * c + sw * s)
    return outs


def qkv_fwd(xb, w_q, w_k, w_v, cs):
    T, D = xb.shape
    HD, KVD = w_q[0].shape[2], w_k[0].shape[2]
    tm = _tile(T, 512)
    scale = 1.0 / (HEAD ** 0.5)

    def body(x_ref, wq_ref, wk_ref, wv_ref, cs_ref, q_ref, k_ref, v_ref):
        def proj(w_ref):
            return jnp.dot(x_ref[...], _rows_joined(w_ref), preferred_element_type=F32)

        for gq, val in enumerate(_rope(proj(wq_ref), cs_ref, 1.0)):
            q_ref[:, gq * 128:(gq + 1) * 128] = (val * scale).astype(BF16)
        for gq, val in enumerate(_rope(proj(wk_ref), cs_ref, 1.0)):
            k_ref[:, gq * 128:(gq + 1) * 128] = val.astype(BF16)
        v_ref[...] = proj(wv_ref).astype(BF16)

    cs_spec = pl.BlockSpec((2, tm, 128), lambda i: (0, i, 0))
    return _pc(body, "qkv_fwd", (T // tm,), [_rows(tm, D), _wspec(w_q), _wspec(w_k), _wspec(w_v), cs_spec],
               [_rows(tm, HD), _rows(tm, KVD), _rows(tm, KVD)],
               [_sds((T, HD), BF16), _sds((T, KVD), BF16), _sds((T, KVD), BF16)], sem=("parallel",))(
                   xb, w_q[0], w_k[0], w_v[0], cs)


def _band_mask(n):
    row = lax.broadcasted_iota(jnp.int32, (BLK, 2 * BLK), 0)
    col = lax.broadcasted_iota(jnp.int32, (BLK, 2 * BLK), 1)
    return (col > row) & (col <= row + BLK) & ((col >= BLK) | (n > 0))


def _head(h):
    return slice(h * HEAD, (h + 1) * HEAD)


def _softmax_sink(s, sink):
    m = jnp.maximum(jnp.max(s, axis=-1, keepdims=True), sink)
    e = jnp.exp(s - m)
    es = jnp.exp(sink - m)
    den = jnp.sum(e, axis=-1, keepdims=True) + es
    inv = 1.0 / den
    return e * inv, es * inv


def attn_fwd(q, k, v, sinks):
    T, HD = q.shape
    KVD = k.shape[1]
    NKV = KVD // HEAD
    G = HD // KVD

    def body(s_ref, q_ref, kc_ref, kp_ref, vc_ref, vp_ref, o_ref):
        valid = _band_mask(pl.program_id(0))
        NH = NKV * G
        k2 = [jnp.concatenate([kp_ref[:, _head(kh)], kc_ref[:, _head(kh)]], axis=0) for kh in range(NKV)]
        v2 = [jnp.concatenate([vp_ref[:, _head(kh)], vc_ref[:, _head(kh)]], axis=0) for kh in range(NKV)]
        sc = [lax.dot_general(q_ref[:, _head(hh)], k2[hh // G], NT, preferred_element_type=F32) for hh in range(NH)]
        pb = [_softmax_sink(jnp.where(valid, s, NEG), s_ref[0, hh])[0].astype(BF16) for hh, s in enumerate(sc)]
        for hh, p in enumerate(pb):
            o_ref[:, _head(hh)] = jnp.dot(p, v2[hh // G], preferred_element_type=F32).astype(BF16)

    cur = lambda n_: pl.BlockSpec((BLK, n_), lambda n: (n, 0))
    prev = lambda n_: pl.BlockSpec((BLK, n_), lambda n: (jnp.maximum(n - 1, 0), 0))
    return _pc(body, "attn_fwd", (T // BLK,),
               [pl.BlockSpec(memory_space=pltpu.SMEM), cur(HD), cur(KVD), prev(KVD), cur(KVD), prev(KVD)],
               cur(HD), _sds((T, HD), BF16), sem=("parallel",))(sinks, q, k, k, v, v)


def ple_bwd(dxo, pp, gl, w_gate, name):
    T, D = dxo.shape
    tm = _tile(T, 512)

    def body(d_ref, pp_ref, gl_ref, wg_ref, dpp_ref, dgl_ref, dx_ref):
        d = d_ref[...]
        sg = _sigmoid(gl_ref[...].astype(F32))
        dpp_ref[...] = (d * sg).astype(BF16)
        dgl = (d * pp_ref[...].astype(F32) * sg * (1.0 - sg)).astype(BF16)
        dgl_ref[...] = dgl
        dx_ref[...] = d + lax.dot_general(dgl, _rows_joined(wg_ref), NT, preferred_element_type=F32)

    return _pc(body, name, (T // tm,), [_rows(tm, D)] * 3 + [_wspec(w_gate)], [_rows(tm, D)] * 3,
               [_sds((T, D), BF16), _sds((T, D), BF16), _sds((T, D), F32)], sem=("parallel",))(dxo, pp, gl, w_gate[0])


def mlp_bwd1(dy, pre, g, t, w_down, name):
    T, D = dy.shape
    fs = w_down[2]
    tm = _tile(T, 512)

    def body(dy_ref, pre_ref, mu_ref, rs_ref, g_ref, t_ref, w_ref, dw_ref, dwb_ref, dm_ref, dg_ref, db_ref):
        dw, dg, db = _ln_bwd(dy_ref[...], pre_ref[...], g_ref[...], (mu_ref[...], rs_ref[...]))
        first = pl.program_id(0) == 0
        _acc_rows(dg_ref, dg, first)
        _acc_rows(db_ref, db, first)
        dwb = dw.astype(BF16)
        dw_ref[...] = dw
        dwb_ref[...] = dwb
        for j in range(NS):
            sl = slice(j * fs, (j + 1) * fs)
            dr = lax.dot_general(dwb, w_ref[j], NT, preferred_element_type=F32)
            dm_ref[:, sl] = (dr * t_ref[:, sl].astype(F32)).astype(BF16)

    return _pc(body, name, (T // tm,),
               [_rows(tm, D), _rows(tm, D), _rows(tm, 1), _rows(tm, 1), _const((1, D)), _rows(tm, NS * fs), _wspec(w_down)],
               [_rows(tm, D), _rows(tm, D), _rows(tm, NS * fs), _const((1, D)), _const((1, D))],
               [_sds((T, D), F32), _sds((T, D), BF16), _sds((T, NS * fs), BF16), _sds((1, D), F32), _sds((1, D), F32)],
               sem=("arbitrary",))(dy, *pre, g, t, w_down[0])


def mlp_bwd2(dpre, dm, w_up, alpha, pre_mix, g_mix, w_mix, name):
    T, D = dpre.shape
    fs = w_up[0].shape[2]
    ms = w_mix[2]
    tm = _tile(T, 512)

    def body(dp_ref, dm_ref, wu_ref, pre_ref, mu_ref, rs_ref, g_ref, wm_ref, dw_ref, dwb_ref, do_ref, dg_ref, db_ref, dc_ref):
        dy = alpha * dp_ref[...]
        for j in range(NS):
            dy = dy + lax.dot_general(dm_ref[:, j * fs:(j + 1) * fs], wu_ref[j], NT, preferred_element_type=F32)
        dw, dg, db = _ln_bwd(dy, pre_ref[...], g_ref[...], (mu_ref[...], rs_ref[...]))
        first = pl.program_id(0) == 0
        _acc_rows(dg_ref, dg, first)
        _acc_rows(db_ref, db, first)
        _acc_rows(dc_ref, jnp.sum(dw, axis=0, keepdims=True), first)
        dwb = dw.astype(BF16)
        dw_ref[...] = dw
        dwb_ref[...] = dwb
        do_ref[...] = lax.dot_general(dwb, _rows_joined(wm_ref), NT, preferred_element_type=F32).astype(BF16)

    return _pc(body, name, (T // tm,),
               [_rows(tm, D), _rows(tm, NS * fs), _wspec(w_up), _rows(tm, D), _rows(tm, 1), _rows(tm, 1), _const((1, D)),
                _wspec(w_mix)],
               [_rows(tm, D), _rows(tm, D), _rows(tm, NS * ms), _const((1, D)), _const((1, D)), _const((1, D))],
               [_sds((T, D), F32), _sds((T, D), BF16), _sds((T, NS * ms), BF16)] + [_sds((1, D), F32)] * 3,
               sem=("arbitrary",))(dpre, dm, w_up[0], *pre_mix, g_mix, w_mix[0])


def attn_bwd(q, k, v, do, sinks):
    T, HD = q.shape
    KVD = k.shape[1]
    NH, NKV = HD // HEAD, KVD // HEAD
    G = NH // NKV
    nb = T // BLK

    def body(s_ref, q_ref, do_ref, kc_ref, kp_ref, vc_ref, vp_ref, dq_ref, dk_ref, dv_ref, ds_ref, ck, cv):
        n = pl.program_id(0)

        @pl.when(n == 0)
        def _():
            ck[...] = jnp.zeros_like(ck)
            cv[...] = jnp.zeros_like(cv)
            ds_ref[...] = jnp.zeros_like(ds_ref)

        @pl.when(n < nb)
        def _():
            valid = _band_mask(n)
            for k0 in range(0, NKV, KV_PER_STAGE):
                khs = range(k0, min(k0 + KV_PER_STAGE, NKV))
                k2 = {kh: jnp.concatenate([kp_ref[:, _head(kh)], kc_ref[:, _head(kh)]], axis=0) for kh in khs}
                v2 = {kh: jnp.concatenate([vp_ref[:, _head(kh)], vc_ref[:, _head(kh)]], axis=0) for kh in khs}
                hs = [kh * G + gq for kh in khs for gq in range(G)]
                qs = {hh: q_ref[:, _head(hh)] for hh in hs}
                dos = {hh: do_ref[:, _head(hh)] for hh in hs}
                sc = {hh: lax.dot_general(qs[hh], k2[hh // G], NT, preferred_element_type=F32) for hh in hs}
                pr = {hh: _softmax_sink(jnp.where(valid, sc[hh], NEG), s_ref[0, hh]) for hh in hs}
                dp = {hh: lax.dot_general(dos[hh], v2[hh // G], NT, preferred_element_type=F32) for hh in hs}
                delta = {hh: jnp.sum(pr[hh][0] * dp[hh], axis=-1, keepdims=True) for hh in hs}
                dsb = {hh: (pr[hh][0] * (dp[hh] - delta[hh])).astype(BF16) for hh in hs}
                pb = {hh: pr[hh][0].astype(BF16) for hh in hs}
                for hh in hs:
                    ds_ref[hh:hh + 1, :] += jnp.broadcast_to(-jnp.sum(pr[hh][1] * delta[hh], axis=0, keepdims=True), (1, 128))
                for hh in hs:
                    dq_ref[:, _head(hh)] = jnp.dot(dsb[hh], k2[hh // G], preferred_element_type=F32)
                for kh in khs:
                    kv = _head(kh)
                    grp = [kh * G + gq for gq in range(G)]
                    dk2 = lax.dot_general(jnp.concatenate([dsb[hh] for hh in grp], axis=0),
                                          jnp.concatenate([qs[hh] for hh in grp], axis=0), TN, preferred_element_type=F32)
                    dv2 = lax.dot_general(jnp.concatenate([pb[hh] for hh in grp], axis=0),
                                          jnp.concatenate([dos[hh] for hh in grp], axis=0), TN, preferred_element_type=F32)
                    dk_ref[:, kv] = ck[:, kv] + dk2[0:BLK]
                    dv_ref[:, kv] = cv[:, kv] + dv2[0:BLK]
                    ck[:, kv] = dk2[BLK:2 * BLK]
                    cv[:, kv] = dv2[BLK:2 * BLK]

        @pl.when(n == nb)
        def _():
            dk_ref[...] = ck[...]
            dv_ref[...] = cv[...]

    qcur = pl.BlockSpec((BLK, HD), lambda n: (jnp.minimum(n, nb - 1), 0))
    kcur = pl.BlockSpec((BLK, KVD), lambda n: (jnp.minimum(n, nb - 1), 0))
    kprev = pl.BlockSpec((BLK, KVD), lambda n: (jnp.maximum(n - 1, 0), 0))
    return _pc(body, "attn_bwd", (nb + 1,),
               [pl.BlockSpec(memory_space=pltpu.SMEM), qcur, qcur, kcur, kprev, kcur, kprev],
               [qcur, kprev, kprev, _const((NH, 128))],
               [_sds((T, HD), F32), _sds((T, KVD), F32), _sds((T, KVD), F32), _sds((NH, 128), F32)],
               scratch=[pltpu.VMEM((BLK, KVD), F32), pltpu.VMEM((BLK, KVD), F32)],
               sem=("arbitrary",))(sinks, q, do, k, k, v, v)


def qkv_bwd(dq, dk, dv, dpre_mix, w_q, w_k, w_v, cs, alpha):
    T, HD = dq.shape
    KVD = dk.shape[1]
    D = dpre_mix.shape[1]
    tm = _tile(T, 512)
    scale = 1.0 / (HEAD ** 0.5)

    def body(dq_ref, dk_ref, dv_ref, dp_ref, wq_ref, wk_ref, wv_ref, cs_ref, dqb_ref, dkb_ref, dvb_ref, dx_ref):
        for gq, val in enumerate(_rope(dq_ref[...], cs_ref, -1.0)):
            dqb_ref[:, gq * 128:(gq + 1) * 128] = (val * scale).astype(BF16)
        for gq, val in enumerate(_rope(dk_ref[...], cs_ref, -1.0)):
            dkb_ref[:, gq * 128:(gq + 1) * 128] = val.astype(BF16)
        dvb_ref[...] = dv_ref[...].astype(BF16)
        dqb, dkb, dvb = dqb_ref[...], dkb_ref[...], dvb_ref[...]
        dx_ref[...] = (alpha * dp_ref[...]
                       + lax.dot_general(dqb, _rows_joined(wq_ref), NT, preferred_element_type=F32)
                       + lax.dot_general(dkb, _rows_joined(wk_ref), NT, preferred_element_type=F32)
                       + lax.dot_general(dvb, _rows_joined(wv_ref), NT, preferred_element_type=F32))

    cs_spec = pl.BlockSpec((2, tm, 128), lambda i: (0, i, 0))
    return _pc(body, "qkv_bwd", (T // tm,),
               [_rows(tm, HD), _rows(tm, KVD), _rows(tm, KVD), _rows(tm, D), _wspec(w_q), _wspec(w_k), _wspec(w_v), cs_spec],
               [_rows(tm, HD), _rows(tm, KVD), _rows(tm, KVD), _rows(tm, D)],
               [_sds((T, HD), BF16), _sds((T, KVD), BF16), _sds((T, KVD), BF16), _sds((T, D), F32)],
               sem=("parallel",))(dq, dk, dv, dpre_mix, w_q[0], w_k[0], w_v[0], cs)


def conv_mid_bwd(ds, cv, ln_g, ln_b):
    T, C = cv.shape
    tm = _tile(T, 512)

    def body(ds_ref, cv_ref, g_ref, b_ref, dcv_ref, dg_ref, db_ref, dc_ref):
        xhat, _, _ = _ln_stats(cv_ref[...])
        ln = xhat * g_ref[...] + b_ref[...]
        sg = _sigmoid(ln)
        dl = ds_ref[...].astype(F32) * (sg * (1.0 + ln * (1.0 - sg)))
        dcv, dg, db = _ln_bwd(dl, cv_ref[...], g_ref[...])
        first = pl.program_id(0) == 0
        _acc_rows(dg_ref, dg, first)
        _acc_rows(db_ref, db, first)
        _acc_rows(dc_ref, jnp.sum(dcv, axis=0, keepdims=True), first)
        dcv_ref[...] = dcv

    return _pc(body, "conv_mid_bwd", (T // tm,), [_rows(tm, C), _rows(tm, C), _const((1, C)), _const((1, C))],
               [_rows(tm, C), _const((1, C)), _const((1, C)), _const((1, C))],
               [_sds((T, C), F32)] + [_sds((1, C), F32)] * 3, sem=("arbitrary",))(ds, cv, ln_g, ln_b)


def dwconv_bwd(dcv, h, w_dw, taps):
    T, C = dcv.shape
    tq = _tile(T)
    nh = tq // HALO
    nblk = T // tq
    off = HALO - (taps - 1)

    def body(d_ref, dn_ref, a_ref, g_ref, ap_ref, gp_ref, w_ref, dh_ref, dw_ref, dbi_ref, su, sus, sd, sds, wb):
        i = pl.program_id(0)
        su[HALO:HALO + tq, :] = a_ref[...].astype(F32) * _sigmoid(g_ref[...].astype(F32))
        up = ap_ref[...].astype(F32) * _sigmoid(gp_ref[...].astype(F32))
        su[0:HALO, :] = jnp.where(i > 0, up, 0.0)
        sd[0:tq, :] = d_ref[...]
        sd[tq:tq + HALO, :] = jnp.where(i < nblk - 1, dn_ref[...], 0.0)
        _phases(su, sus)
        _phases(sd, sds)

        @pl.when(i == 0)
        def _():
            dw_ref[...] = jnp.zeros_like(dw_ref)

        for j in range(taps):
            dw_ref[j:j + 1, :] += jnp.sum(d_ref[...] * _tap(su, sus, off + j, tq), axis=0, keepdims=True)
        sa = jnp.zeros((1, C), F32)
        sb = jnp.zeros((1, C), F32)
        _spread(w_ref, wb, taps)
        for r in range(tq // CONV_ROWS):
            rows = slice(r * CONV_ROWS, (r + 1) * CONV_ROWS)
            dus = [wb[0] * _tap(sd, sds, taps - 1 + r * CONV_ROWS + 8 * k, 8) for k in range(CONV_ROWS // 8)]
            for j in range(1, taps):
                wj = wb[j]
                dus = [acc + wj * _tap(sd, sds, taps - 1 - j + r * CONV_ROWS + 8 * k, 8) for k, acc in enumerate(dus)]
            du = jnp.concatenate(dus, axis=0)
            a = a_ref[rows, :].astype(F32)
            sg = _sigmoid(g_ref[rows, :].astype(F32))
            da = du * sg
            dgt = du * a * sg * (1.0 - sg)
            dh_ref[rows, 0:C] = da.astype(BF16)
            dh_ref[rows, C:2 * C] = dgt.astype(BF16)
            sa = sa + jnp.sum(da, axis=0, keepdims=True)
            sb = sb + jnp.sum(dgt, axis=0, keepdims=True)
        first = i == 0
        _acc_rows(dbi_ref.at[:, 0:C], sa, first)
        _acc_rows(dbi_ref.at[:, C:2 * C], sb, first)

    prev = lambda col: pl.BlockSpec((HALO, C), lambda i: (jnp.maximum(i * nh - 1, 0), col))
    nxt = pl.BlockSpec((HALO, C), lambda i: (jnp.minimum((i + 1) * nh, T // HALO - 1), 0))
    cur = lambda col: pl.BlockSpec((tq, C), lambda i: (i, col))
    return _pc(body, "dwconv_bwd", (nblk,),
               [cur(0), nxt, cur(0), cur(1), prev(0), prev(1), _const((HALO, C))],
               [_rows(tq, 2 * C), _const((HALO, C)), _const((1, 2 * C))],
               [_sds((T, 2 * C), BF16), _sds((HALO, C), F32), _sds((1, 2 * C), F32)],
               scratch=[pltpu.VMEM((HALO + tq, C), F32), pltpu.VMEM((7, HALO + tq, C), F32),
                        pltpu.VMEM((HALO + tq, C), F32), pltpu.VMEM((7, HALO + tq, C), F32), pltpu.VMEM((taps, 8, C), F32)],
               sem=("arbitrary",))(dcv, dcv, h, h, h, h, w_dw)


def conv_in_bwd(dh, dpre_mix, w_in, alpha):
    T, D = dpre_mix.shape
    nw = w_in[0].shape[2]
    tm = _tile(T, 512)

    def body(dh_ref, dp_ref, w_ref, dx_ref):
        acc = alpha * dp_ref[...]
        for j in range(NS):
            acc = acc + lax.dot_general(dh_ref[:, j * nw:(j + 1) * nw], w_ref[j], NT, preferred_element_type=F32)
        dx_ref[...] = acc

    return _pc(body, "conv_in_bwd", (T // tm,), [_rows(tm, NS * nw), _rows(tm, D), _wspec(w_in)], _rows(tm, D),
               _sds((T, D), F32), sem=("parallel",))(dh, dpre_mix, w_in[0])


def wgrad(a, b, row_sharded, name, into):
    prev, out_shape, off = into
    layer = None
    if isinstance(a, tuple):
        layer, a = a
    T, Ka = a.shape[-2:]
    Nb = b.shape[1]
    ka, tn = min(Ka, 1024), min(Nb, 1024)
    tt = min(4096 if (Ka // ka) * (Nb // tn) >= 4 else 2048, T)
    nt = T // tt
    if row_sharded:
        sr = Ka // NS
        spb = max(ka // sr, 1)
        rb = ka // spb
        assert out_shape[2] == Nb and off % rb == 0
        out_spec = pl.BlockSpec((spb, rb, tn), lambda i, j, t: (i, off // rb, j))
    else:
        sc = Nb // NS
        spb = max(tn // sc, 1)
        rb = ka
        assert out_shape[2] == sc and off % ka == 0
        out_spec = pl.BlockSpec((spb, ka, tn // spb), lambda i, j, t: (j, off // ka + i, 0))

    def body(a_ref, b_ref, *rest):
        o_ref, acc = rest[-2:]
        t = pl.program_id(2)
        av = a_ref[...]
        if av.dtype != BF16:
            av = av.astype(BF16)
        d = lax.dot_general(av, b_ref[...], TN, preferred_element_type=F32)

        @pl.when(t == 0)
        def _():
            acc[...] = d

        @pl.when(t > 0)
        def _():
            acc[...] += d

        @pl.when(t == nt - 1)
        def _():
            for s in range(spb):
                if row_sharded:
                    o_ref[s] = acc[s * rb:(s + 1) * rb, :].astype(BF16)
                else:
                    o_ref[s] = acc[:, s * (tn // spb):(s + 1) * (tn // spb)].astype(BF16)

    a_spec = (pl.BlockSpec((tt, ka), lambda i, j, t: (t, i)) if layer is None
              else pl.BlockSpec((None, tt, ka), lambda i, j, t: (layer, t, i)))
    ins = [a_spec, pl.BlockSpec((tt, tn), lambda i, j, t: (t, j))]
    args = [a, b]
    kw = {}
    if prev is not None:
        ins.append(ANY)
        args.append(prev)
        kw["input_output_aliases"] = {2: 0}
    return _pc(body, name, (Ka // ka, Nb // tn, nt), ins, out_spec, _sds(out_shape, BF16),
               scratch=[pltpu.VMEM((ka, tn), F32)], sem=("parallel", "parallel", "arbitrary"), **kw)(*args)


def _adamw_math(w, g, m, v):
    c1 = 1.0 - ADAM_B1 ** ADAM_STEP
    c2 = 1.0 - ADAM_B2 ** ADAM_STEP
    mn = ADAM_B1 * m + (1.0 - ADAM_B1) * g
    vn = ADAM_B2 * v + (1.0 - ADAM_B2) * (g * g)
    return -ADAM_LR * ((mn / c1) / (jnp.sqrt(vn / c2) + ADAM_EPS) + ADAM_WD * w), mn, vn


def adamw_layer(w, m, v, layer, gbuf, off, prev, name):
    L, R, W = w.shape
    tr = 256
    assert R % tr == 0 and off % tr == 0

    def body(w_ref, g_ref, m_ref, v_ref, *rest):
        go_ref, d_ref, mo_ref, vo_ref = rest[-4:]
        g = g_ref[...]
        go_ref[...] = g
        d_ref[...], mo_ref[...], vo_ref[...] = _adamw_math(w_ref[...], g, m_ref[...], v_ref[...])

    lay = pl.BlockSpec((None, tr, W), lambda r: (layer, r, 0))
    ins = [lay, pl.BlockSpec((tr, W), lambda r: (off // tr + r, 0)), lay, lay]
    args = [w, gbuf, m, v]
    kw = {}
    if prev is not None:
        ins += [ANY] * 4
        args += list(prev)
        kw["input_output_aliases"] = {4 + k: k for k in range(4)}
    return _pc(body, name, (R // tr,), ins, [lay] * 4, [_sds((L, R, W), F32)] * 4, sem=("parallel",), **kw)(*args)


def adamw_many(ws, gs, ms, vs):
    n = len(ws)

    def body(*refs):
        for k in range(n):
            d, mn, vn = _adamw_math(refs[k][...], refs[n + k][...], refs[2 * n + k][...], refs[3 * n + k][...])
            refs[4 * n + k][...] = d
            refs[5 * n + k][...] = mn
            refs[6 * n + k][...] = vn

    outs = pl.pallas_call(body, name="adamw_small", out_shape=[_sds(a.shape, F32) for a in ws] * 3)(*ws, *gs, *ms, *vs)
    return outs[:n], outs[n:2 * n], outs[2 * n:]


def _rope_tables(T):
    pos = jnp.arange(T, dtype=F32)
    inv_freq = ROPE_THETA ** (-jnp.arange(0, ROPE, 2, dtype=F32) / ROPE)
    ang = pos[:, None] * inv_freq[None, :]
    cos, sin = jnp.cos(ang), jnp.sin(ang)
    pad = HEAD - ROPE
    c = jnp.concatenate([cos, cos, jnp.ones((T, pad), F32)], axis=1)
    s = jnp.concatenate([-sin, sin, jnp.zeros((T, pad), F32)], axis=1)
    return jnp.stack([jnp.tile(c, (1, 128 // HEAD)), jnp.tile(s, (1, 128 // HEAD))])


def _local_step(x, p, target, W, small, lay, hook=None):
    if hook is None:
        hook = lambda stage, after, G, sg=None: None
    T, D = x.shape
    depth = small["mix_ln_g"].shape[0]
    alpha = float((2 * depth) ** 0.25)
    taps = small["taps"]
    row = lambda a, i: a[i:i + 1]
    cs = _rope_tables(T)

    h = conv_in_fwd(x, W["conv_w_in"], small["conv_b_in"])
    cv, s = dwconv_fwd(h, small["conv_w_dw"], small["conv_b_dw"], small["conv_ln_g"], small["conv_ln_b"], taps)
    hook("weights1", s, None)
    pre_mix0, x1, x1b = mm_res_ln(s, W["conv_w_out"], x, row(small["mix_ln_g"], 0), row(small["mix_ln_b"], 0), alpha,
                                  small["conv_b_out"], "conv_out_fwd")
    r0, t0 = mlp_up_fwd(x1b, W["mlp_w_up0"], "mlp_up_fwd0")
    pre_mlp0, x2, x2b = mm_res_ln(r0, W["mlp_w_down0"], x1, row(small["mlp_ln_g"], 0), row(small["mlp_ln_b"], 0), alpha,
                                  None, "mlp_down_fwd0")
    x3, x3b, pp0, gl0 = ple_fwd(x2, x2b, p, 0, W["ple_w_proj0"], W["ple_w_gate0"], None, "ple_fwd0")

    hook("weights2", x3b, None)
    q, k, v = qkv_fwd(x3b, W["attn_w_q"], W["kv_w_k"], W["kv_w_v"], cs)
    o = attn_fwd(q, k, v, small["attn_sinks"])
    pre_mix1, x4, x4b = mm_res_ln(o, W["attn_w_o"], x3, row(small["mix_ln_g"], 1), row(small["mix_ln_b"], 1), alpha,
                                  None, "attn_out_fwd")
    r1, t1 = mlp_up_fwd(x4b, W["mlp_w_up1"], "mlp_up_fwd1")
    pre_mlp1, x5, x5b = mm_res_ln(r1, W["mlp_w_down1"], x4, row(small["mlp_ln_g"], 1), row(small["mlp_ln_b"], 1), alpha,
                                  None, "mlp_down_fwd1")
    dx6, loss, pp1, gl1 = ple_fwd(x5, x5b, p, 1, W["ple_w_proj1"], W["ple_w_gate1"], target, "ple_fwd1")

    G, sg = {}, {}
    where = {n: (key, off) for key in lay for n, off, _ in lay[key]}
    rows_of = {key: sum(r for _, _, r in lay[key]) for key in lay}

    def wg(name, a, b, row_sharded):
        key, off = where[name]
        shape = (NS, rows_of[key], W[name][0].shape[2])
        G[key] = wgrad(a, b, row_sharded, "wg_" + name, (G.get(key), shape, off))

    dpp1, dgl1, dx5 = ple_bwd(dx6, pp1, gl1, W["ple_w_gate1"], "ple_bwd1")
    wg("ple_w_proj1", (1, p), dpp1, False)
    wg("ple_w_gate1", x5b, dgl1, True)
    dpre_mlp1, dpre_mlp1b, dm1, g_mlp_g1, g_mlp_b1 = mlp_bwd1(dx5, pre_mlp1, row(small["mlp_ln_g"], 1), t1,
                                                              W["mlp_w_down1"], "mlp_bwd1_1")
    wg("mlp_w_down1", r1, dpre_mlp1b, True)
    wg("mlp_w_up1", x4b, dm1, False)
    dpre_mix1, dpre_mix1b, do, g_mix_g1, g_mix_b1, _ = mlp_bwd2(dpre_mlp1, dm1, W["mlp_w_up1"], alpha, pre_mix1,
                                                                row(small["mix_ln_g"], 1), W["attn_w_o"], "mlp_bwd2_1")
    wg("attn_w_o", o, dpre_mix1b, True)
    dq, dk, dv, dsinks = attn_bwd(q, k, v, do, small["attn_sinks"])
    dqb, dkb, dvb, dx3 = qkv_bwd(dq, dk, dv, dpre_mix1,
                                 W["attn_w_q"], W["kv_w_k"], W["kv_w_v"], cs, alpha)
    wg("attn_w_q", x3b, dqb, True)
    wg("kv_w_k", x3b, dkb, True)
    wg("kv_w_v", x3b, dvb, True)
    hook("grads3", None, G)

    dpp0, dgl0, dx2 = ple_bwd(dx3, pp0, gl0, W["ple_w_gate0"], "ple_bwd0")
    wg("ple_w_proj0", (0, p), dpp0, False)
    wg("ple_w_gate0", x2b, dgl0, True)
    dpre_mlp0, dpre_mlp0b, dm0, g_mlp_g0, g_mlp_b0 = mlp_bwd1(dx2, pre_mlp0, row(small["mlp_ln_g"], 0), t0,
                                                              W["mlp_w_down0"], "mlp_bwd1_0")
    wg("mlp_w_down0", r0, dpre_mlp0b, True)
    wg("mlp_w_up0", x1b, dm0, False)
    hook("grads2", None, G)
    dpre_mix0, dpre_mix0b, dsw, g_mix_g0, g_mix_b0, g_b_out = mlp_bwd2(dpre_mlp0, dm0, W["mlp_w_up0"], alpha, pre_mix0,
                                                                      row(small["mix_ln_g"], 0), W["conv_w_out"],
                                                                      "mlp_bwd2_0")
    wg("conv_w_out", s, dpre_mix0b, True)
    hook("grads1", None, G)
    dcv, g_cln_g, g_cln_b, g_b_dw = conv_mid_bwd(dsw, cv, small["conv_ln_g"], small["conv_ln_b"])
    dh, g_w_dw, g_b_in = dwconv_bwd(dcv, h, small["conv_w_dw"], taps)
    wg("conv_w_in", x, dh, False)

    sg["conv_b_in"] = g_b_in
    sg["conv_w_dw"] = g_w_dw
    sg["conv_b_dw"], sg["conv_ln_g"], sg["conv_ln_b"], sg["conv_b_out"] = g_b_dw, g_cln_g, g_cln_b, g_b_out
    sg["mix_ln_g"] = [g_mix_g0, g_mix_g1]
    sg["mix_ln_b"] = [g_mix_b0, g_mix_b1]
    sg["mlp_ln_g"] = [g_mlp_g0, g_mlp_g1]
    sg["mlp_ln_b"] = [g_mlp_b0, g_mlp_b1]
    sg["attn_sinks"] = dsinks[:, 0][None, :]
    sg["loss"] = loss
    hook("grads0", None, G, sg)
    grad_x = conv_in_bwd(dh, dpre_mix0, W["conv_w_in"], alpha)
    return loss, grad_x, G, sg


BUFFERS = (("b0", ("conv_w_in",)), ("a0", ("conv_w_out",)),
           ("a1", ("mlp_w_up0", "mlp_w_down0", "ple_w_gate0")), ("c1", ("ple_w_proj0",)),
           ("a2", ("mlp_w_up1", "mlp_w_down1", "ple_w_gate1", "attn_w_q", "attn_w_o")),
           ("c2", ("kv_w_k", "kv_w_v", "ple_w_proj1")))
GROUPS = (("b0",), ("a0", "a1", "c1"), ("a2", "c2"))
REDUCED = (("b0",), ("a0",), ("a1", "c1"), ("a2", "c2"))
ROW_SHARDED = {"mlp_w_down0", "mlp_w_down1", "ple_w_gate0", "ple_w_gate1", "conv_w_out", "attn_w_q", "attn_w_o", "kv_w_k",
               "kv_w_v"}


def _split_layers(weights):
    out = {"conv_w_in": weights["conv_w_in"][0], "conv_w_out": weights["conv_w_out"][0],
           "attn_w_q": weights["attn_w_q"][0], "attn_w_o": weights["attn_w_o"][0],
           "kv_w_k": weights["kv_w_k"], "kv_w_v": weights["kv_w_v"]}
    for n in ("mlp_w_up", "mlp_w_down", "ple_w_proj", "ple_w_gate"):
        for i in range(weights[n].shape[0]):
            out[n + str(i)] = weights[n][i]
    return out


def _layout(shards):
    lay = {}
    for key, names in BUFFERS:
        off, rows = 0, []
        for n in names:
            rows.append((n, off, shards[n].shape[0]))
            off += shards[n].shape[0]
        lay[key] = rows
    return lay


def _place():
    return lax.axis_index("x"), lax.axis_index("y"), lax.axis_index("c")


def _flip(v, f):
    return (v + f) % 2 if f else v


CHIP_FLIPS = ((1, 0), (0, 1), (1, 1))


HBM = pl.BlockSpec(memory_space=pltpu.HBM)
SEM = pl.BlockSpec(memory_space=pltpu.SEMAPHORE)
EFFECT = pltpu.SideEffectType.DATAFLOW_SIDE_EFFECTING


def _half(ref, rows, c):
    return ref.at[pl.ds(pl.multiple_of(c * (rows // 2), 16), rows // 2), :]


def _gather_copies(refs, shapes, whole, send, recv):
    x, y, c = _place()
    me = 2 * x + y
    na = len(refs)
    cps = []
    for d, (fx, fy) in enumerate(CHIP_FLIPS):
        to = (_flip(x, fx), _flip(y, fy), c)
        for k in range(na):
            mine = refs[k].at[me] if k >= na - whole else _half(refs[k].at[me], shapes[k][1], c)
            cps.append(pltpu.make_async_remote_copy(mine, mine, send.at[d * na + k], recv.at[d * na + k], device_id=to,
                                                    device_id_type=MESH))
    return cps


def gather_start(bufs, whole, after, name):
    na = len(bufs)
    shapes = [b.shape for b in bufs]
    nsem = len(CHIP_FLIPS) * na

    def body(*refs):
        ins = refs[:na]
        send, recv = refs[-(na + 3)], refs[-(na + 2)]
        token = refs[-1]
        for cp in _gather_copies(ins, shapes, whole, send, recv):
            cp.start()
        token[...] = jnp.zeros_like(token)

    args = [pltpu.with_memory_space_constraint(b, pltpu.HBM) for b in bufs]
    ins = [HBM] * na
    if after is not None:
        args.append(after)
        ins.append(ANY)
    return pl.pallas_call(
        body, name=name, in_specs=ins,
        out_specs=[SEM, SEM] + [HBM] * na + [pl.BlockSpec(memory_space=pltpu.VMEM)],
        out_shape=[pltpu.SemaphoreType.DMA((nsem,)), pltpu.SemaphoreType.DMA((nsem,))]
        + [pltpu.HBM(b.shape, b.dtype) for b in bufs] + [_sds((8, 128), F32)],
        input_output_aliases={k: k + 2 for k in range(na)},
        compiler_params=pltpu.CompilerParams(has_side_effects=EFFECT))(*args)


def gather_wait(send, recv, bufs, whole, after, name):
    na = len(bufs)
    shapes = [b.shape for b in bufs]

    def body(*refs):
        ins = refs[:na]
        send_ref, recv_ref = refs[na], refs[na + 1]
        for cp in _gather_copies(ins, shapes, whole, send_ref, recv_ref):
            cp.wait_send()
            cp.wait_recv()

    return pl.pallas_call(
        body, name=name, in_specs=[HBM] * na + [SEM, SEM, ANY], out_specs=[HBM] * na,
        out_shape=[pltpu.HBM(b.shape, b.dtype) for b in bufs], input_output_aliases={k: k for k in range(na)},
        compiler_params=pltpu.CompilerParams(has_side_effects=EFFECT))(*bufs, send, recv, after)


def sibling_forward(bufs, name):
    nb = len(bufs)

    def body(*refs):
        outs = refs[nb:2 * nb]
        send, recv = refs[2 * nb:]
        x, y, c = _place()
        cps = []
        for d, (fx, fy) in enumerate(CHIP_FLIPS):
            frm = 2 * _flip(x, fx) + _flip(y, fy)
            for k in range(nb):
                theirs = _half(outs[k].at[frm], bufs[k].shape[1], c)
                cps.append(pltpu.make_async_remote_copy(theirs, theirs, send.at[d * nb + k], recv.at[d * nb + k],
                                                        device_id=(x, y, 1 - c), device_id_type=MESH))
        for cp in cps:
            cp.start()
        for cp in cps:
            cp.wait()

    nsem = len(CHIP_FLIPS) * nb
    return pl.pallas_call(
        body, name=name, in_specs=[ANY] * nb, out_specs=[ANY] * nb, out_shape=[_sds(b.shape, b.dtype) for b in bufs],
        input_output_aliases={k: k for k in range(nb)},
        scratch_shapes=[pltpu.SemaphoreType.DMA((nsem,)), pltpu.SemaphoreType.DMA((nsem,))])(*bufs)


def pack_rows(pieces, rows, width, name):
    def body(*refs):
        o_ref = refs[-1]
        o_ref[...] = jnp.zeros_like(o_ref)
        for ref, (a, off) in zip(refs[:-1], pieces):
            o_ref[off:off + a.shape[0], 0:a.shape[1]] = ref[...]

    return pl.pallas_call(body, name=name, out_shape=_sds((rows, width), F32))(*[a for a, _ in pieces])


PEER_FLIPS = tuple((fx, fy, fc) for fx in (0, 1) for fy in (0, 1) for fc in (0, 1) if fx or fy or fc)


def _reduce_copies(parts, zones, pack, send, recv):
    x, y, c = _place()
    nb = len(parts)
    na = nb + (1 if pack is not None else 0)
    cps = []
    for f, (fx, fy, fc) in enumerate(PEER_FLIPS):
        tx, ty, tc = _flip(x, fx), _flip(y, fy), _flip(c, fc)
        for k in range(nb):
            hrows = parts[k].shape[1] // 2
            piece = parts[k].at[2 * tx + ty, pl.ds(pl.multiple_of(tc * hrows, 16), hrows), :]
            cps.append(pltpu.make_async_remote_copy(piece, zones[k].at[f], send.at[f * na + k], recv.at[f * na + k],
                                                    device_id=(tx, ty, tc), device_id_type=MESH))
        if pack is not None:
            mine = pack.at[4 * x + 2 * y + c]
            cps.append(pltpu.make_async_remote_copy(mine, mine, send.at[f * na + nb], recv.at[f * na + nb],
                                                    device_id=(tx, ty, tc), device_id_type=MESH))
    return cps


def reduce_begin(parts, pack, name):
    nb = len(parts)
    zones = [lax.empty((len(PEER_FLIPS), g.shape[1] // 2, g.shape[2]), g.dtype) for g in parts]
    arrs = list(parts) + zones + ([pack] if pack is not None else [])
    na = len(arrs)
    nsem = len(PEER_FLIPS) * (nb + (1 if pack is not None else 0))

    def body(*refs):
        ins = refs[:na]
        send, recv = refs[na], refs[na + 1]
        for cp in _reduce_copies(ins[:nb], ins[nb:2 * nb], ins[2 * nb] if pack is not None else None, send, recv):
            cp.start()
        refs[-1][...] = jnp.zeros_like(refs[-1])

    return pl.pallas_call(
        body, name=name, in_specs=[HBM] * na,
        out_specs=[SEM, SEM] + [HBM] * na + [pl.BlockSpec(memory_space=pltpu.VMEM)],
        out_shape=[pltpu.SemaphoreType.DMA((nsem,)), pltpu.SemaphoreType.DMA((nsem,))]
        + [pltpu.HBM(a.shape, a.dtype) for a in arrs] + [_sds((8, 128), F32)],
        input_output_aliases={k: k + 2 for k in range(na)},
        compiler_params=pltpu.CompilerParams(has_side_effects=EFFECT))(
            *[pltpu.with_memory_space_constraint(a, pltpu.HBM) for a in arrs])


def reduce_end(send, recv, parts, zones, pack, after, name):
    nb = len(parts)
    arrs = list(parts) + list(zones) + ([pack] if pack is not None else [])
    na = len(arrs)

    def body(*refs):
        ins = refs[:na]
        for cp in _reduce_copies(ins[:nb], ins[nb:2 * nb], ins[2 * nb] if pack is not None else None, refs[na], refs[na + 1]):
            cp.wait_send()
            cp.wait_recv()

    return pl.pallas_call(
        body, name=name, in_specs=[HBM] * na + [SEM, SEM, ANY], out_specs=[HBM] * na,
        out_shape=[pltpu.HBM(a.shape, a.dtype) for a in arrs], input_output_aliases={k: k for k in range(na)},
        compiler_params=pltpu.CompilerParams(has_side_effects=EFFECT))(*arrs, send, recv, after)


def sibling_share(halves, name):
    nb = len(halves)

    def body(*refs):
        outs = refs[nb:2 * nb]
        send, recv = refs[2 * nb:]
        x, y, c = _place()
        cps = []
        for k in range(nb):
            hrows = halves[k].shape[0] // 2
            mine = outs[k].at[pl.ds(pl.multiple_of(c * hrows, 8), hrows), :]
            cps.append(pltpu.make_async_remote_copy(mine, mine, send.at[k], recv.at[k], device_id=(x, y, 1 - c),
                                                    device_id_type=MESH))
        for cp in cps:
            cp.start()
        for cp in cps:
            cp.wait()

    return pl.pallas_call(
        body, name=name, in_specs=[ANY] * nb, out_specs=[ANY] * nb,
        out_shape=[_sds(h.shape, h.dtype) for h in halves], input_output_aliases={k: k for k in range(nb)},
        scratch_shapes=[pltpu.SemaphoreType.DMA((nb,)), pltpu.SemaphoreType.DMA((nb,))])(*halves)


def _row_tile(rows):
    for cand in (512, 384, 256, 128, 64, 32, 16):
        if rows % cand == 0:
            return cand
    return rows


def piece_sum(g, z, idx, name):
    _, hrows, W = z.shape
    tr = _row_tile(hrows)
    nrb = hrows // tr

    def body(idx_ref, g_ref, z_ref, o_ref):
        acc = g_ref[...].astype(F32)
        for d in range(z.shape[0]):
            acc = acc + z_ref[d].astype(F32)
        o_ref[...] = acc

    gs = pltpu.PrefetchScalarGridSpec(
        num_scalar_prefetch=1, grid=(nrb,),
        in_specs=[pl.BlockSpec((None, tr, W), lambda i, sc: (sc[0], sc[1] * nrb + i, 0)),
                  pl.BlockSpec((z.shape[0], tr, W), lambda i, sc: (0, i, 0))],
        out_specs=pl.BlockSpec((tr, W), lambda i, sc: (sc[1] * nrb + i, 0)))
    return pl.pallas_call(body, name=name, grid_spec=gs, out_shape=_sds((2 * hrows, W), F32),
                          compiler_params=pltpu.CompilerParams(dimension_semantics=("parallel",),
                                                               vmem_limit_bytes=48 * 2 ** 20))(idx, g, z)


def small_sum(packs):
    n, R, W = packs.shape

    def body(p_ref, o_ref):
        acc = p_ref[0]
        for d in range(1, n):
            acc = acc + p_ref[d]
        o_ref[...] = acc

    return pl.pallas_call(body, name="small_sum", out_shape=_sds((R, W), F32))(packs)


WEIGHTS = ["conv_w_in", "conv_b_in", "conv_w_dw", "conv_b_dw", "conv_ln_g", "conv_ln_b", "conv_w_out", "conv_b_out", "kv_w_k",
           "kv_w_v", "attn_w_q", "attn_sinks", "attn_w_o", "mix_ln_g", "mix_ln_b", "mlp_w_up", "mlp_w_down", "mlp_ln_g",
           "mlp_ln_b", "ple_w_proj", "ple_w_gate"]
BIG = ["conv_w_in", "conv_w_out", "kv_w_k", "kv_w_v", "attn_w_q", "attn_w_o", "mlp_w_up", "mlp_w_down", "ple_w_proj",
       "ple_w_gate"]
SMALL = [n for n in WEIGHTS if n not in BIG]


def _step(x, p, target, w, m, v):
    D = x.shape[-1]
    ds = D // NS
    xq, yq, cq = _place()
    chip = 2 * xq + yq
    idx = jnp.stack([chip, cq]).astype(jnp.int32)

    shards = _split_layers(w)
    lay = _layout(shards)
    taps = w["conv_w_dw"].shape[1]
    small_loc = pack_rows([(w["conv_w_dw"][0], 0), (w["conv_b_dw"], HALO), (w["conv_ln_g"], HALO + 1), (w["conv_ln_b"], HALO + 2),
                           (w["conv_b_out"], HALO + 3), (w["conv_b_in"].reshape(2, ds), HALO + 4)], HALO + 8, ds, "pack_small")
    slot = lambda a: lax.dynamic_update_slice(lax.empty((NS,) + a.shape, a.dtype), a[None], (chip, 0, 0))
    started, token = [], None
    for gi, keys in enumerate(GROUPS):
        bufs = [slot(jnp.concatenate([shards[n].astype(BF16) for n, _, _ in lay[key]], axis=0)) for key in keys]
        if gi == 0:
            bufs.append(slot(small_loc))
        send, recv, *thru, token = gather_start(bufs, 1 if gi == 0 else 0, token, "gather_start%d" % gi)
        started.append((send, recv, thru))
    W = {}

    def arrive(gi, after):
        send, recv, thru = started[gi]
        whole = 1 if gi == 0 else 0
        got = gather_wait(send, recv, thru, whole, after, "gather_wait%d" % gi)
        nk = len(GROUPS[gi])
        for key, buf in zip(GROUPS[gi], sibling_forward(got[:nk], "sibling_forward%d" % gi)):
            for n, off, rows in lay[key]:
                W[n] = (buf, off, rows)
        return got[nk:]

    gs, = arrive(0, token)
    across = lambda rows: gs[:, rows, :].transpose(1, 0, 2).reshape(rows.stop - rows.start, D)
    small = {"taps": taps, "conv_w_dw": across(slice(0, HALO)), "conv_b_dw": across(slice(HALO, HALO + 1)),
             "conv_ln_g": across(slice(HALO + 1, HALO + 2)), "conv_ln_b": across(slice(HALO + 2, HALO + 3)),
             "conv_b_out": across(slice(HALO + 3, HALO + 4)), "conv_b_in": gs[:, HALO + 4:HALO + 6, :].reshape(1, 2 * D),
             "attn_sinks": w["attn_sinks"], "mix_ln_g": w["mix_ln_g"], "mix_ln_b": w["mix_ln_b"],
             "mlp_ln_g": w["mlp_ln_g"], "mlp_ln_b": w["mlp_ln_b"]}

    reducing = {}

    def reduce_start(gi, G, pack):
        nk = len(REDUCED[gi])
        send, recv, *thru, token = reduce_begin([G[key] for key in REDUCED[gi]], pack, "reduce_begin%d" % gi)
        reducing[gi] = (send, recv, thru[:nk], thru[nk:2 * nk], thru[2 * nk] if pack is not None else None)
        _FOLLOW.append(token)

    def small_pack(sg):
        pieces = [(sg["conv_b_in"].reshape(2, D), 0), (sg["conv_w_dw"], 2)]
        r0 = 2 + HALO
        for i, n in enumerate(("conv_b_dw", "conv_ln_g", "conv_ln_b", "conv_b_out")):
            pieces.append((sg[n], r0 + i))
        r0 += 4
        for i, n in enumerate(("mix_ln_g", "mix_ln_b", "mlp_ln_g", "mlp_ln_b")):
            pieces += [(sg[n][0], r0 + 2 * i), (sg[n][1], r0 + 2 * i + 1)]
        pieces += [(sg["attn_sinks"], r0 + 8), (sg["loss"][0:1], r0 + 9)]
        mine = pack_rows(pieces, r0 + 10, D, "pack_small_grads")
        return lax.dynamic_update_slice(lax.empty((8,) + mine.shape, F32), mine[None], (4 * xq + 2 * yq + cq, 0, 0))

    def hook(stage, after, G, sg=None):
        if stage == "weights1":
            arrive(1, after)
        elif stage == "weights2":
            arrive(2, after)
        elif stage == "grads0":
            reduce_start(0, G, small_pack(sg))
        elif stage.startswith("grads"):
            reduce_start(int(stage[5:]), G, None)

    _, grad_x, G, sg = _local_step(x[0], p[:, 0], target[0], W, small, lay, hook)
    _FOLLOW.clear()
    nsink = w["attn_sinks"].shape[1]

    grads, delta, new_m, new_v = {}, {}, {}, {}
    found = {}

    def finish(groups, after, tag):
        keys, halves, tot = [], [], None
        for gi in groups:
            send, recv, parts, zones, pack = reducing[gi]
            done = reduce_end(send, recv, parts, zones, pack, after, "reduce_end%d" % gi)
            nk = len(REDUCED[gi])
            for key, g_, z_ in zip(REDUCED[gi], done[:nk], done[nk:2 * nk]):
                keys.append(key)
                halves.append(piece_sum(g_, z_, idx, "piece_sum_" + key))
            if pack is not None:
                tot = small_sum(done[2 * nk])
        for key, buf in zip(keys, sibling_share(halves, "sibling_share" + tag)):
            for n, off, _ in lay[key]:
                found[n] = (buf, off)
        return tot

    def big_adamw(names):
        for n in names:
            three = lambda a: a.reshape((-1,) + a.shape[-2:])
            w3, m3, v3 = three(w[n]), three(m[n]), three(v[n])
            outs = None
            for i in range(w3.shape[0]):
                buf, off = found[n + str(i)] if n + str(i) in found else found[n]
                outs = adamw_layer(w3, m3, v3, i, buf, off, outs, "adamw_%s%d" % (n, i))
            grads[n], delta[n], new_m[n], new_v[n] = [a.reshape(w[n].shape) for a in outs]

    last = [n for n, _, _ in lay[REDUCED[0][0]]]
    finish(reversed(range(1, len(REDUCED))), grad_x, "1")
    big_adamw([n for n in BIG if n not in last])
    tot = finish([0], new_v["mlp_w_down"], "0")
    big_adamw(last)
    cols = lambda rows: lax.dynamic_slice(rows, (0, chip * ds), (rows.shape[0], ds))
    grads["conv_b_in"] = lax.dynamic_slice(tot[0:2].reshape(1, 2 * D), (0, chip * 2 * ds), (1, 2 * ds))
    grads["conv_w_dw"] = cols(tot[2:2 + taps])[None]
    r0 = 2 + HALO
    for i, n in enumerate(("conv_b_dw", "conv_ln_g", "conv_ln_b", "conv_b_out")):
        grads[n] = cols(tot[r0 + i:r0 + i + 1])
    r0 += 4
    for i, n in enumerate(("mix_ln_g", "mix_ln_b", "mlp_ln_g", "mlp_ln_b")):
        grads[n] = tot[r0 + 2 * i:r0 + 2 * i + 2]
    grads["attn_sinks"] = tot[r0 + 8:r0 + 9, 0:nsink]

    ds_, ms_, vs_ = adamw_many([w[n] for n in SMALL], [grads[n] for n in SMALL], [m[n] for n in SMALL], [v[n] for n in SMALL])
    for n, d_, m_, v_ in zip(SMALL, ds_, ms_, vs_):
        delta[n], new_m[n], new_v[n] = d_, m_, v_

    total = tot[r0 + 9, 0]
    return (total, grad_x[None], *[grads[n] for n in WEIGHTS], *[delta[n] for n in WEIGHTS], *[new_m[n] for n in WEIGHTS],
            *[new_v[n] for n in WEIGHTS])


def kernel(x, p, conv_w_in, conv_b_in, conv_w_dw, conv_b_dw, conv_ln_g, conv_ln_b, conv_w_out, conv_b_out, kv_w_k, kv_w_v, attn_w_q, attn_sinks, attn_w_o, mix_ln_g, mix_ln_b, mlp_w_up, mlp_w_down, mlp_ln_g, mlp_ln_b, ple_w_proj, ple_w_gate, loss_target, m_conv_w_in, m_conv_b_in, m_conv_w_dw, m_conv_b_dw, m_conv_ln_g, m_conv_ln_b, m_conv_w_out, m_conv_b_out, m_kv_w_k, m_kv_w_v, m_attn_w_q, m_attn_sinks, m_attn_w_o, m_mix_ln_g, m_mix_ln_b, m_mlp_w_up, m_mlp_w_down, m_mlp_ln_g, m_mlp_ln_b, m_ple_w_proj, m_ple_w_gate, v_conv_w_in, v_conv_b_in, v_conv_w_dw, v_conv_b_dw, v_conv_ln_g, v_conv_ln_b, v_conv_w_out, v_conv_b_out, v_kv_w_k, v_kv_w_v, v_attn_w_q, v_attn_sinks, v_attn_w_o, v_mix_ln_g, v_mix_ln_b, v_mlp_w_up, v_mlp_w_down, v_mlp_ln_g, v_mlp_ln_b, v_ple_w_proj, v_ple_w_gate):
    w = dict(zip(WEIGHTS, (conv_w_in, conv_b_in, conv_w_dw, conv_b_dw, conv_ln_g, conv_ln_b, conv_w_out, conv_b_out, kv_w_k,
                           kv_w_v, attn_w_q, attn_sinks, attn_w_o, mix_ln_g, mix_ln_b, mlp_w_up, mlp_w_down, mlp_ln_g, mlp_ln_b,
                           ple_w_proj, ple_w_gate)))
    m = dict(zip(WEIGHTS, (m_conv_w_in, m_conv_b_in, m_conv_w_dw, m_conv_b_dw, m_conv_ln_g, m_conv_ln_b, m_conv_w_out,
                           m_conv_b_out, m_kv_w_k, m_kv_w_v, m_attn_w_q, m_attn_sinks, m_attn_w_o, m_mix_ln_g, m_mix_ln_b,
                           m_mlp_w_up, m_mlp_w_down, m_mlp_ln_g, m_mlp_ln_b, m_ple_w_proj, m_ple_w_gate)))
    v = dict(zip(WEIGHTS, (v_conv_w_in, v_conv_b_in, v_conv_w_dw, v_conv_b_dw, v_conv_ln_g, v_conv_ln_b, v_conv_w_out,
                           v_conv_b_out, v_kv_w_k, v_kv_w_v, v_attn_w_q, v_attn_sinks, v_attn_w_o, v_mix_ln_g, v_mix_ln_b,
                           v_mlp_w_up, v_mlp_w_down, v_mlp_ln_g, v_mlp_ln_b, v_ple_w_proj, v_ple_w_gate)))
    return _step(x, p, loss_target, w, m, v)
```

```python
import jax
import jax.numpy as jnp
from jax import lax
from jax.experimental import pallas as pl
from jax.experimental.pallas import tpu as pltpu

F32 = jnp.float32
BF16 = jnp.bfloat16
NS = 4
HEAD = 64
BLK = 128
ROPE = 16
ROPE_THETA = 500000.0
LN_EPS = 1e-5
NEG = -1e30
VMEM_LIMIT_MIB = 56
TILE_ROWS = 256
TILE_ROWS_WIDE = 512
BLOCK_DIM = 1024
KV_PER_STAGE = 1
HALO = 32
ADAM_LR, ADAM_B1, ADAM_B2, ADAM_EPS, ADAM_WD, ADAM_STEP = 0.001, 0.9, 0.999, 1e-08, 0.01, 10
MESH = pl.DeviceIdType.MESH
ANY = pl.BlockSpec(memory_space=pl.ANY)
NT = (((1,), (1,)), ((), ()))
TN = (((0,), (0,)), ((), ()))


_FOLLOW = []


def _pc(body, name, grid, in_specs, out_specs, out_shape, scratch=(), sem=None, vmem=VMEM_LIMIT_MIB, **kw):
    call = lambda fn, ins: pl.pallas_call(
        fn, name=name, grid=grid, in_specs=ins, out_specs=out_specs, out_shape=out_shape,
        scratch_shapes=list(scratch),
        compiler_params=pltpu.CompilerParams(dimension_semantics=sem, vmem_limit_bytes=vmem * 2 ** 20), **kw)
    if not _FOLLOW:
        return call(body, in_specs)
    extra = list(_FOLLOW)
    _FOLLOW.clear()
    n_in = len(in_specs)

    def ordered(*refs):
        return body(*refs[:n_in], *refs[n_in + len(extra):])

    run = call(ordered, list(in_specs) + [ANY] * len(extra))
    return lambda *args: run(*args, *extra)


def _rows(tm, n):
    return pl.BlockSpec((tm, n), lambda i: (i, 0))


def _const(shape):
    return pl.BlockSpec(shape, lambda *_: (0,) * len(shape))


def _wspec(w):
    buf, off, rows = w
    assert off % rows == 0
    return pl.BlockSpec((NS, rows, buf.shape[2]), lambda *_: (0, off // rows, 0))


def _rows_joined(w_ref):
    n, r, c = w_ref.shape
    return w_ref[...].reshape(n * r, c)


def _sds(shape, dtype):
    return jax.ShapeDtypeStruct(shape, dtype)


def _tile(t, rows=TILE_ROWS):
    return min(rows, t)


def _sigmoid(x):
    return 0.5 * jnp.tanh(0.5 * x) + 0.5


def _ln_stats(w):
    mu = jnp.mean(w, axis=-1, keepdims=True)
    xc = w - mu
    var = jnp.mean(xc * xc, axis=-1, keepdims=True)
    rstd = lax.rsqrt(var + LN_EPS)
    return xc * rstd, rstd, mu


def _ln_bwd(dy, w, g, stats=None):
    if stats is None:
        xhat, rstd, _ = _ln_stats(w)
    else:
        mu, rstd = stats
        xhat = (w - mu) * rstd
    dxhat = dy * g
    m1 = jnp.mean(dxhat, axis=-1, keepdims=True)
    m2 = jnp.mean(dxhat * xhat, axis=-1, keepdims=True)
    dw = rstd * (dxhat - m1 - xhat * m2)
    return dw, jnp.sum(dy * xhat, axis=0, keepdims=True), jnp.sum(dy, axis=0, keepdims=True)


def _acc_rows(ref, val, first):
    @pl.when(first)
    def _():
        ref[...] = val

    @pl.when(jnp.logical_not(first))
    def _():
        ref[...] += val


def conv_in_fwd(xb, w_in, b_in):
    T, D = xb.shape
    nw = w_in[0].shape[2]
    tm = _tile(T, TILE_ROWS_WIDE)

    def body(x_ref, w_ref, b_ref, h_ref):
        x = x_ref[...].astype(BF16)
        for j in range(NS):
            sl = slice(j * nw, (j + 1) * nw)
            h_ref[:, sl] = (jnp.dot(x, w_ref[j], preferred_element_type=F32) + b_ref[:, sl]).astype(BF16)

    return _pc(body, "conv_in_fwd", (T // tm,), [_rows(tm, D), _wspec(w_in), _const((1, NS * nw))],
               _rows(tm, NS * nw), _sds((T, NS * nw), BF16), sem=("parallel",))(xb, w_in[0], b_in)


CONV_ROWS = 16


def _phases(scr, sh):
    n = scr.shape[0] - 8
    for b in range(1, 8):
        sh[b - 1, 0:n, :] = scr[b:b + n, :]


def _spread(w_ref, wb, taps):
    for j in range(taps):
        wb[j] = jnp.broadcast_to(w_ref[j:j + 1, :], wb.shape[1:])


def _tap(scr, sh, o, n):
    b = o % 8
    return scr[o:o + n, :] if b == 0 else sh[b - 1, o - b:o - b + n, :]


def dwconv_fwd(h, w_dw, b_dw, ln_g, ln_b, taps):
    T = h.shape[0]
    C = h.shape[1] // 2
    tq = _tile(T)
    nh = tq // HALO
    off = HALO - (taps - 1)

    def body(a_ref, g_ref, ap_ref, gp_ref, w_ref, bdw_ref, lg_ref, lb_ref, cv_ref, s_ref, scr, sh, wb):
        i = pl.program_id(0)
        scr[HALO:HALO + tq, :] = a_ref[...].astype(F32) * _sigmoid(g_ref[...].astype(F32))
        up = ap_ref[...].astype(F32) * _sigmoid(gp_ref[...].astype(F32))
        scr[0:HALO, :] = jnp.where(i > 0, up, 0.0)
        _phases(scr, sh)
        _spread(w_ref, wb, taps)
        bias = jnp.broadcast_to(bdw_ref[...], (8, C))
        for r in range(tq // CONV_ROWS):
            accs = [bias] * (CONV_ROWS // 8)
            for j in range(taps):
                wj = wb[j]
                accs = [acc + wj * _tap(scr, sh, off + j + r * CONV_ROWS + 8 * k, 8) for k, acc in enumerate(accs)]
            for k, acc in enumerate(accs):
                cv_ref[r * CONV_ROWS + 8 * k:r * CONV_ROWS + 8 * k + 8, :] = acc
        xhat, _, _ = _ln_stats(cv_ref[...])
        ln = xhat * lg_ref[...] + lb_ref[...]
        s_ref[...] = (ln * _sigmoid(ln)).astype(BF16)

    prev = lambda col: pl.BlockSpec((HALO, C), lambda i: (jnp.maximum(i * nh - 1, 0), col))
    cur = lambda col: pl.BlockSpec((tq, C), lambda i: (i, col))
    return _pc(body, "dwconv_fwd", (T // tq,),
               [cur(0), cur(1), prev(0), prev(1), _const((HALO, C)), _const((1, C)), _const((1, C)), _const((1, C))],
               [_rows(tq, C), _rows(tq, C)], [_sds((T, C), F32), _sds((T, C), BF16)],
               scratch=[pltpu.VMEM((HALO + tq, C), F32), pltpu.VMEM((7, HALO + tq, C), F32), pltpu.VMEM((taps, 8, C), F32)],
               sem=("parallel",))(h, h, h, h, w_dw, b_dw, ln_g, ln_b)


def mm_res_ln(a, w, res, g, b, alpha, bias, name):
    T, K = a.shape
    D = res.shape[1]
    tm = _tile(T, TILE_ROWS_WIDE)

    def body(*refs):
        a_ref, w_ref, res_ref, g_ref, b_ref = refs[:5]
        n = 5
        if bias is not None:
            bias_ref = refs[5]
            n = 6
        pre_ref, mu_ref, rs_ref, xo_ref, xb_ref = refs[n:n + 5]
        acc = jnp.dot(a_ref[...], _rows_joined(w_ref), preferred_element_type=F32)
        if bias is not None:
            acc = acc + bias_ref[...]
        pre = alpha * res_ref[...] + acc
        xhat, rstd, mu = _ln_stats(pre)
        xo = xhat * g_ref[...] + b_ref[...]
        pre_ref[...] = pre
        mu_ref[...] = mu
        rs_ref[...] = rstd
        xo_ref[...] = xo
        xb_ref[...] = xo.astype(BF16)

    ins = [_rows(tm, K), _wspec(w), _rows(tm, D), _const((1, D)), _const((1, D))]
    args = [a, w[0], res, g, b]
    if bias is not None:
        ins.append(_const((1, D)))
        args.append(bias)
    pre, mu, rstd, xo, xb = _pc(
        body, name, (T // tm,), ins, [_rows(tm, D), _rows(tm, 1), _rows(tm, 1), _rows(tm, D), _rows(tm, D)],
        [_sds((T, D), F32), _sds((T, 1), F32), _sds((T, 1), F32), _sds((T, D), F32), _sds((T, D), BF16)],
        sem=("parallel",))(*args)
    return (pre, mu, rstd), xo, xb


def mlp_up_fwd(xb, w_up, name):
    T, D = xb.shape
    fs = w_up[0].shape[2]
    tm = _tile(T, TILE_ROWS_WIDE)

    def body(x_ref, w_ref, r_ref, t_ref):
        x = x_ref[...]
        for j in range(NS):
            sl = slice(j * fs, (j + 1) * fs)
            m = jnp.maximum(jnp.dot(x, w_ref[j], preferred_element_type=F32), 0.0)
            r_ref[:, sl] = (m * m).astype(BF16)
            t_ref[:, sl] = (2.0 * m).astype(BF16)

    return _pc(body, name, (T // tm,), [_rows(tm, D), _wspec(w_up)], [_rows(tm, NS * fs)] * 2,
               [_sds((T, NS * fs), BF16)] * 2, sem=("parallel",))(xb, w_up[0])


def ple_fwd(x, xb, p, layer, w_proj, w_gate, target, name):
    T, D = x.shape
    P = p.shape[2]
    ds = D // NS
    tm = _tile(T, TILE_ROWS_WIDE)
    last = target is not None

    def body(*refs):
        x_ref, xb_ref, p_ref, wp_ref, wg_ref = refs[:5]
        n = 5
        if last:
            t_ref = refs[5]
            n = 6
        o_ref, o2_ref, pp_ref, gl_ref = refs[n:n + 4]
        gl = jnp.dot(xb_ref[...], _rows_joined(wg_ref), preferred_element_type=F32)
        gl_ref[...] = gl.astype(BF16)
        sg = _sigmoid(gl)
        pb = p_ref[...].astype(BF16)
        sq = jnp.zeros((1, 1), F32)
        for j in range(NS):
            sl = slice(j * ds, (j + 1) * ds)
            pp = jnp.dot(pb, wp_ref[j], preferred_element_type=F32)
            pp_ref[:, sl] = pp.astype(BF16)
            out = x_ref[:, sl] + pp * sg[:, sl]
            if last:
                err = out - t_ref[:, sl]
                o_ref[:, sl] = err * (1.0 / D)
                e2 = jnp.sum(err * err, axis=0, keepdims=True)
                sq = sq + jnp.sum(e2, axis=1, keepdims=True)
            else:
                o_ref[:, sl] = out
                o2_ref[:, sl] = out.astype(BF16)
        if last:
            _acc_rows(o2_ref, jnp.broadcast_to(sq * (0.5 / D), (8, 128)), pl.program_id(0) == 0)

    ins = [_rows(tm, D), _rows(tm, D), pl.BlockSpec((None, tm, P), lambda i: (layer, i, 0)), _wspec(w_proj), _wspec(w_gate)]
    args = [x, xb, p, w_proj[0], w_gate[0]]
    if last:
        ins.append(_rows(tm, D))
        args.append(target)
        outs = [_rows(tm, D), _const((8, 128)), _rows(tm, D), _rows(tm, D)]
        shapes = [_sds((T, D), F32), _sds((8, 128), F32), _sds((T, D), BF16), _sds((T, D), BF16)]
    else:
        outs = [_rows(tm, D)] * 4
        shapes = [_sds((T, D), F32), _sds((T, D), BF16), _sds((T, D), BF16), _sds((T, D), BF16)]
    return _pc(body, name, (T // tm,), ins, outs, shapes, sem=("arbitrary",) if last else ("parallel",))(*args)


def _rope(x, cs_ref, sign):
    c = cs_ref[0]
    s = cs_ref[1] * sign
    lane = lax.broadcasted_iota(jnp.int32, c.shape, 1)
    first = (lane % HEAD) < (ROPE // 2)
    outs = []
    for gq in range(x.shape[1] // 128):
        xg = x[:, gq * 128:(gq + 1) * 128]
        sw = jnp.where(first, pltpu.roll(xg, 128 - ROPE // 2, 1), pltpu.roll(xg, ROPE // 2, 1))
        outs.append(xg * c + sw * s)
    return outs


def qkv_fwd(xb, w_q, w_k, w_v, cs):
    T, D = xb.shape
    HD, KVD = w_q[0].shape[2], w_k[0].shape[2]
    tm = _tile(T, TILE_ROWS_WIDE)
    scale = 1.0 / (HEAD ** 0.5)

    def body(x_ref, wq_ref, wk_ref, wv_ref, cs_ref, q_ref, k_ref, v_ref):
        def proj(w_ref):
            return jnp.dot(x_ref[...], _rows_joined(w_ref), preferred_element_type=F32)

        for gq, val in enumerate(_rope(proj(wq_ref), cs_ref, 1.0)):
            q_ref[:, gq * 128:(gq + 1) * 128] = (val * scale).astype(BF16)
        for gq, val in enumerate(_rope(proj(wk_ref), cs_ref, 1.0)):
            k_ref[:, gq * 128:(gq + 1) * 128] = val.astype(BF16)
        v_ref[...] = proj(wv_ref).astype(BF16)

    cs_spec = pl.BlockSpec((2, tm, 128), lambda i: (0, i, 0))
    return _pc(body, "qkv_fwd", (T // tm,), [_rows(tm, D), _wspec(w_q), _wspec(w_k), _wspec(w_v), cs_spec],
               [_rows(tm, HD), _rows(tm, KVD), _rows(tm, KVD)],
               [_sds((T, HD), BF16), _sds((T, KVD), BF16), _sds((T, KVD), BF16)], sem=("parallel",))(
                   xb, w_q[0], w_k[0], w_v[0], cs)


def _band_mask(n):
    row = lax.broadcasted_iota(jnp.int32, (BLK, 2 * BLK), 0)
    col = lax.broadcasted_iota(jnp.int32, (BLK, 2 * BLK), 1)
    return (col > row) & (col <= row + BLK) & ((col >= BLK) | (n > 0))


def _head(h):
    return slice(h * HEAD, (h + 1) * HEAD)


def _softmax_sink(s, sink):
    m = jnp.maximum(jnp.max(s, axis=-1, keepdims=True), sink)
    e = jnp.exp(s - m)
    es = jnp.exp(sink - m)
    den = jnp.sum(e, axis=-1, keepdims=True) + es
    inv = 1.0 / den
    return e * inv, es * inv


def attn_fwd(q, k, v, sinks):
    T, HD = q.shape
    KVD = k.shape[1]
    NKV = KVD // HEAD
    G = HD // KVD

    def body(s_ref, q_ref, kc_ref, kp_ref, vc_ref, vp_ref, o_ref):
        valid = _band_mask(pl.program_id(0))
        NH = NKV * G
        k2 = [jnp.concatenate([kp_ref[:, _head(kh)], kc_ref[:, _head(kh)]], axis=0) for kh in range(NKV)]
        v2 = [jnp.concatenate([vp_ref[:, _head(kh)], vc_ref[:, _head(kh)]], axis=0) for kh in range(NKV)]
        sc = [lax.dot_general(q_ref[:, _head(hh)], k2[hh // G], NT, preferred_element_type=F32) for hh in range(NH)]
        pb = [_softmax_sink(jnp.where(valid, s, NEG), s_ref[0, hh])[0].astype(BF16) for hh, s in enumerate(sc)]
        for hh, p in enumerate(pb):
            o_ref[:, _head(hh)] = jnp.dot(p, v2[hh // G], preferred_element_type=F32).astype(BF16)

    cur = lambda n_: pl.BlockSpec((BLK, n_), lambda n: (n, 0))
    prev = lambda n_: pl.BlockSpec((BLK, n_), lambda n: (jnp.maximum(n - 1, 0), 0))
    return _pc(body, "attn_fwd", (T // BLK,),
               [pl.BlockSpec(memory_space=pltpu.SMEM), cur(HD), cur(KVD), prev(KVD), cur(KVD), prev(KVD)],
               cur(HD), _sds((T, HD), BF16), sem=("parallel",))(sinks, q, k, k, v, v)


def ple_bwd(dxo, pp, gl, w_gate, name):
    T, D = dxo.shape
    tm = _tile(T, TILE_ROWS_WIDE)

    def body(d_ref, pp_ref, gl_ref, wg_ref, dpp_ref, dgl_ref, dx_ref):
        d = d_ref[...]
        sg = _sigmoid(gl_ref[...].astype(F32))
        dpp_ref[...] = (d * sg).astype(BF16)
        dgl = (d * pp_ref[...].astype(F32) * sg * (1.0 - sg)).astype(BF16)
        dgl_ref[...] = dgl
        dx_ref[...] = d + lax.dot_general(dgl, _rows_joined(wg_ref), NT, preferred_element_type=F32)

    return _pc(body, name, (T // tm,), [_rows(tm, D)] * 3 + [_wspec(w_gate)], [_rows(tm, D)] * 3,
               [_sds((T, D), BF16), _sds((T, D), BF16), _sds((T, D), F32)], sem=("parallel",))(dxo, pp, gl, w_gate[0])


def mlp_bwd1(dy, pre, g, t, w_down, name):
    T, D = dy.shape
    fs = w_down[2]
    tm = _tile(T, TILE_ROWS_WIDE)

    def body(dy_ref, pre_ref, mu_ref, rs_ref, g_ref, t_ref, w_ref, dw_ref, dwb_ref, dm_ref, dg_ref, db_ref):
        dw, dg, db = _ln_bwd(dy_ref[...], pre_ref[...], g_ref[...], (mu_ref[...], rs_ref[...]))
        first = pl.program_id(0) == 0
        _acc_rows(dg_ref, dg, first)
        _acc_rows(db_ref, db, first)
        dwb = dw.astype(BF16)
        dw_ref[...] = dw
        dwb_ref[...] = dwb
        for j in range(NS):
            sl = slice(j * fs, (j + 1) * fs)
            dr = lax.dot_general(dwb, w_ref[j], NT, preferred_element_type=F32)
            dm_ref[:, sl] = (dr * t_ref[:, sl].astype(F32)).astype(BF16)

    return _pc(body, name, (T // tm,),
               [_rows(tm, D), _rows(tm, D), _rows(tm, 1), _rows(tm, 1), _const((1, D)), _rows(tm, NS * fs), _wspec(w_down)],
               [_rows(tm, D), _rows(tm, D), _rows(tm, NS * fs), _const((1, D)), _const((1, D))],
               [_sds((T, D), F32), _sds((T, D), BF16), _sds((T, NS * fs), BF16), _sds((1, D), F32), _sds((1, D), F32)],
               sem=("arbitrary",))(dy, *pre, g, t, w_down[0])


def mlp_bwd2(dpre, dm, w_up, alpha, pre_mix, g_mix, w_mix, name):
    T, D = dpre.shape
    fs = w_up[0].shape[2]
    ms = w_mix[2]
    tm = _tile(T, TILE_ROWS_WIDE)

    def body(dp_ref, dm_ref, wu_ref, pre_ref, mu_ref, rs_ref, g_ref, wm_ref, dw_ref, dwb_ref, do_ref, dg_ref, db_ref, dc_ref):
        dy = alpha * dp_ref[...]
        for j in range(NS):
            dy = dy + lax.dot_general(dm_ref[:, j * fs:(j + 1) * fs], wu_ref[j], NT, preferred_element_type=F32)
        dw, dg, db = _ln_bwd(dy, pre_ref[...], g_ref[...], (mu_ref[...], rs_ref[...]))
        first = pl.program_id(0) == 0
        _acc_rows(dg_ref, dg, first)
        _acc_rows(db_ref, db, first)
        _acc_rows(dc_ref, jnp.sum(dw, axis=0, keepdims=True), first)
        dwb = dw.astype(BF16)
        dw_ref[...] = dw
        dwb_ref[...] = dwb
        do_ref[...] = lax.dot_general(dwb, _rows_joined(wm_ref), NT, preferred_element_type=F32).astype(BF16)

    return _pc(body, name, (T // tm,),
               [_rows(tm, D), _rows(tm, NS * fs), _wspec(w_up), _rows(tm, D), _rows(tm, 1), _rows(tm, 1), _const((1, D)),
                _wspec(w_mix)],
               [_rows(tm, D), _rows(tm, D), _rows(tm, NS * ms), _const((1, D)), _const((1, D)), _const((1, D))],
               [_sds((T, D), F32), _sds((T, D), BF16), _sds((T, NS * ms), BF16)] + [_sds((1, D), F32)] * 3,
               sem=("arbitrary",))(dpre, dm, w_up[0], *pre_mix, g_mix, w_mix[0])


def attn_bwd(q, k, v, do, sinks):
    T, HD = q.shape
    KVD = k.shape[1]
    NH, NKV = HD // HEAD, KVD // HEAD
    G = NH // NKV
    nb = T // BLK

    def body(s_ref, q_ref, do_ref, kc_ref, kp_ref, vc_ref, vp_ref, dq_ref, dk_ref, dv_ref, ds_ref, ck, cv):
        n = pl.program_id(0)

        @pl.when(n == 0)
        def _():
            ck[...] = jnp.zeros_like(ck)
            cv[...] = jnp.zeros_like(cv)
            ds_ref[...] = jnp.zeros_like(ds_ref)

        @pl.when(n < nb)
        def _():
            valid = _band_mask(n)
            for k0 in range(0, NKV, KV_PER_STAGE):
                khs = range(k0, min(k0 + KV_PER_STAGE, NKV))
                k2 = {kh: jnp.concatenate([kp_ref[:, _head(kh)], kc_ref[:, _head(kh)]], axis=0) for kh in khs}
                v2 = {kh: jnp.concatenate([vp_ref[:, _head(kh)], vc_ref[:, _head(kh)]], axis=0) for kh in khs}
                hs = [kh * G + gq for kh in khs for gq in range(G)]
                qs = {hh: q_ref[:, _head(hh)] for hh in hs}
                dos = {hh: do_ref[:, _head(hh)] for hh in hs}
                sc = {hh: lax.dot_general(qs[hh], k2[hh // G], NT, preferred_element_type=F32) for hh in hs}
                pr = {hh: _softmax_sink(jnp.where(valid, sc[hh], NEG), s_ref[0, hh]) for hh in hs}
                dp = {hh: lax.dot_general(dos[hh], v2[hh // G], NT, preferred_element_type=F32) for hh in hs}
                delta = {hh: jnp.sum(pr[hh][0] * dp[hh], axis=-1, keepdims=True) for hh in hs}
                dsb = {hh: (pr[hh][0] * (dp[hh] - delta[hh])).astype(BF16) for hh in hs}
                pb = {hh: pr[hh][0].astype(BF16) for hh in hs}
                for hh in hs:
                    ds_ref[hh:hh + 1, :] += jnp.broadcast_to(-jnp.sum(pr[hh][1] * delta[hh], axis=0, keepdims=True), (1, 128))
                for hh in hs:
                    dq_ref[:, _head(hh)] = jnp.dot(dsb[hh], k2[hh // G], preferred_element_type=F32)
                for kh in khs:
                    kv = _head(kh)
                    grp = [kh * G + gq for gq in range(G)]
                    dk2 = lax.dot_general(jnp.concatenate([dsb[hh] for hh in grp], axis=0),
                                          jnp.concatenate([qs[hh] for hh in grp], axis=0), TN, preferred_element_type=F32)
                    dv2 = lax.dot_general(jnp.concatenate([pb[hh] for hh in grp], axis=0),
                                          jnp.concatenate([dos[hh] for hh in grp], axis=0), TN, preferred_element_type=F32)
                    dk_ref[:, kv] = ck[:, kv] + dk2[0:BLK]
                    dv_ref[:, kv] = cv[:, kv] + dv2[0:BLK]
                    ck[:, kv] = dk2[BLK:2 * BLK]
                    cv[:, kv] = dv2[BLK:2 * BLK]

        @pl.when(n == nb)
        def _():
            dk_ref[...] = ck[...]
            dv_ref[...] = cv[...]

    qcur = pl.BlockSpec((BLK, HD), lambda n: (jnp.minimum(n, nb - 1), 0))
    kcur = pl.BlockSpec((BLK, KVD), lambda n: (jnp.minimum(n, nb - 1), 0))
    kprev = pl.BlockSpec((BLK, KVD), lambda n: (jnp.maximum(n - 1, 0), 0))
    return _pc(body, "attn_bwd", (nb + 1,),
               [pl.BlockSpec(memory_space=pltpu.SMEM), qcur, qcur, kcur, kprev, kcur, kprev],
               [qcur, kprev, kprev, _const((NH, 128))],
               [_sds((T, HD), F32), _sds((T, KVD), F32), _sds((T, KVD), F32), _sds((NH, 128), F32)],
               scratch=[pltpu.VMEM((BLK, KVD), F32), pltpu.VMEM((BLK, KVD), F32)],
               sem=("arbitrary",))(sinks, q, do, k, k, v, v)


def qkv_bwd(dq, dk, dv, dpre_mix, w_q, w_k, w_v, cs, alpha):
    T, HD = dq.shape
    KVD = dk.shape[1]
    D = dpre_mix.shape[1]
    tm = _tile(T, TILE_ROWS_WIDE)
    scale = 1.0 / (HEAD ** 0.5)

    def body(dq_ref, dk_ref, dv_ref, dp_ref, wq_ref, wk_ref, wv_ref, cs_ref, dqb_ref, dkb_ref, dvb_ref, dx_ref):
        for gq, val in enumerate(_rope(dq_ref[...], cs_ref, -1.0)):
            dqb_ref[:, gq * 128:(gq + 1) * 128] = (val * scale).astype(BF16)
        for gq, val in enumerate(_rope(dk_ref[...], cs_ref, -1.0)):
            dkb_ref[:, gq * 128:(gq + 1) * 128] = val.astype(BF16)
        dvb_ref[...] = dv_ref[...].astype(BF16)
        dqb, dkb, dvb = dqb_ref[...], dkb_ref[...], dvb_ref[...]
        dx_ref[...] = (alpha * dp_ref[...]
                       + lax.dot_general(dqb, _rows_joined(wq_ref), NT, preferred_element_type=F32)
                       + lax.dot_general(dkb, _rows_joined(wk_ref), NT, preferred_element_type=F32)
                       + lax.dot_general(dvb, _rows_joined(wv_ref), NT, preferred_element_type=F32))

    cs_spec = pl.BlockSpec((2, tm, 128), lambda i: (0, i, 0))
    return _pc(body, "qkv_bwd", (T // tm,),
               [_rows(tm, HD), _rows(tm, KVD), _rows(tm, KVD), _rows(tm, D), _wspec(w_q), _wspec(w_k), _wspec(w_v), cs_spec],
               [_rows(tm, HD), _rows(tm, KVD), _rows(tm, KVD), _rows(tm, D)],
               [_sds((T, HD), BF16), _sds((T, KVD), BF16), _sds((T, KVD), BF16), _sds((T, D), F32)],
               sem=("parallel",))(dq, dk, dv, dpre_mix, w_q[0], w_k[0], w_v[0], cs)


def conv_mid_bwd(ds, cv, ln_g, ln_b):
    T, C = cv.shape
    tm = _tile(T, TILE_ROWS_WIDE)

    def body(ds_ref, cv_ref, g_ref, b_ref, dcv_ref, dg_ref, db_ref, dc_ref):
        xhat, _, _ = _ln_stats(cv_ref[...])
        ln = xhat * g_ref[...] + b_ref[...]
        sg = _sigmoid(ln)
        dl = ds_ref[...].astype(F32) * (sg * (1.0 + ln * (1.0 - sg)))
        dcv, dg, db = _ln_bwd(dl, cv_ref[...], g_ref[...])
        first = pl.program_id(0) == 0
        _acc_rows(dg_ref, dg, first)
        _acc_rows(db_ref, db, first)
        _acc_rows(dc_ref, jnp.sum(dcv, axis=0, keepdims=True), first)
        dcv_ref[...] = dcv

    return _pc(body, "conv_mid_bwd", (T // tm,), [_rows(tm, C), _rows(tm, C), _const((1, C)), _const((1, C))],
               [_rows(tm, C), _const((1, C)), _const((1, C)), _const((1, C))],
               [_sds((T, C), F32)] + [_sds((1, C), F32)] * 3, sem=("arbitrary",))(ds, cv, ln_g, ln_b)


def dwconv_bwd(dcv, h, w_dw, taps):
    T, C = dcv.shape
    tq = _tile(T)
    nh = tq // HALO
    nblk = T // tq
    off = HALO - (taps - 1)

    def body(d_ref, dn_ref, a_ref, g_ref, ap_ref, gp_ref, w_ref, dh_ref, dw_ref, dbi_ref, su, sus, sd, sds, wb):
        i = pl.program_id(0)
        su[HALO:HALO + tq, :] = a_ref[...].astype(F32) * _sigmoid(g_ref[...].astype(F32))
        up = ap_ref[...].astype(F32) * _sigmoid(gp_ref[...].astype(F32))
        su[0:HALO, :] = jnp.where(i > 0, up, 0.0)
        sd[0:tq, :] = d_ref[...]
        sd[tq:tq + HALO, :] = jnp.where(i < nblk - 1, dn_ref[...], 0.0)
        _phases(su, sus)
        _phases(sd, sds)

        @pl.when(i == 0)
        def _():
            dw_ref[...] = jnp.zeros_like(dw_ref)

        for j in range(taps):
            dw_ref[j:j + 1, :] += jnp.sum(d_ref[...] * _tap(su, sus, off + j, tq), axis=0, keepdims=True)
        sa = jnp.zeros((1, C), F32)
        sb = jnp.zeros((1, C), F32)
        _spread(w_ref, wb, taps)
        for r in range(tq // CONV_ROWS):
            rows = slice(r * CONV_ROWS, (r + 1) * CONV_ROWS)
            dus = [wb[0] * _tap(sd, sds, taps - 1 + r * CONV_ROWS + 8 * k, 8) for k in range(CONV_ROWS // 8)]
            for j in range(1, taps):
                wj = wb[j]
                dus = [acc + wj * _tap(sd, sds, taps - 1 - j + r * CONV_ROWS + 8 * k, 8) for k, acc in enumerate(dus)]
            du = jnp.concatenate(dus, axis=0)
            a = a_ref[rows, :].astype(F32)
            sg = _sigmoid(g_ref[rows, :].astype(F32))
            da = du * sg
            dgt = du * a * sg * (1.0 - sg)
            dh_ref[rows, 0:C] = da.astype(BF16)
            dh_ref[rows, C:2 * C] = dgt.astype(BF16)
            sa = sa + jnp.sum(da, axis=0, keepdims=True)
            sb = sb + jnp.sum(dgt, axis=0, keepdims=True)
        first = i == 0
        _acc_rows(dbi_ref.at[:, 0:C], sa, first)
        _acc_rows(dbi_ref.at[:, C:2 * C], sb, first)

    prev = lambda col: pl.BlockSpec((HALO, C), lambda i: (jnp.maximum(i * nh - 1, 0), col))
    nxt = pl.BlockSpec((HALO, C), lambda i: (jnp.minimum((i + 1) * nh, T // HALO - 1), 0))
    cur = lambda col: pl.BlockSpec((tq, C), lambda i: (i, col))
    return _pc(body, "dwconv_bwd", (nblk,),
               [cur(0), nxt, cur(0), cur(1), prev(0), prev(1), _const((HALO, C))],
               [_rows(tq, 2 * C), _const((HALO, C)), _const((1, 2 * C))],
               [_sds((T, 2 * C), BF16), _sds((HALO, C), F32), _sds((1, 2 * C), F32)],
               scratch=[pltpu.VMEM((HALO + tq, C), F32), pltpu.VMEM((7, HALO + tq, C), F32),
                        pltpu.VMEM((HALO + tq, C), F32), pltpu.VMEM((7, HALO + tq, C), F32), pltpu.VMEM((taps, 8, C), F32)],
               sem=("arbitrary",))(dcv, dcv, h, h, h, h, w_dw)


def conv_in_bwd(dh, dpre_mix, w_in, alpha):
    T, D = dpre_mix.shape
    nw = w_in[0].shape[2]
    tm = _tile(T, TILE_ROWS_WIDE)

    def body(dh_ref, dp_ref, w_ref, dx_ref):
        acc = alpha * dp_ref[...]
        for j in range(NS):
            acc = acc + lax.dot_general(dh_ref[:, j * nw:(j + 1) * nw], w_ref[j], NT, preferred_element_type=F32)
        dx_ref[...] = acc

    return _pc(body, "conv_in_bwd", (T // tm,), [_rows(tm, NS * nw), _rows(tm, D), _wspec(w_in)], _rows(tm, D),
               _sds((T, D), F32), sem=("parallel",))(dh, dpre_mix, w_in[0])


def wgrad(a, b, row_sharded, name, into):
    prev, out_shape, off = into
    layer = None
    if isinstance(a, tuple):
        layer, a = a
    T, Ka = a.shape[-2:]
    Nb = b.shape[1]
    ka, tn = min(Ka, BLOCK_DIM), min(Nb, BLOCK_DIM)
    tt = T if (Ka // ka) * (Nb // tn) >= 4 else T // 2
    nt = T // tt
    if row_sharded:
        sr = Ka // NS
        spb = max(ka // sr, 1)
        rb = ka // spb
        assert out_shape[2] == Nb and off % rb == 0
        out_spec = pl.BlockSpec((spb, rb, tn), lambda i, j, t: (i, off // rb, j))
    else:
        sc = Nb // NS
        spb = max(tn // sc, 1)
        rb = ka
        assert out_shape[2] == sc and off % ka == 0
        out_spec = pl.BlockSpec((spb, ka, tn // spb), lambda i, j, t: (j, off // ka + i, 0))

    def body(a_ref, b_ref, *rest):
        o_ref, acc = rest[-2:]
        t = pl.program_id(2)
        av = a_ref[...]
        if av.dtype != BF16:
            av = av.astype(BF16)
        d = lax.dot_general(av, b_ref[...], TN, preferred_element_type=F32)

        @pl.when(t == 0)
        def _():
            acc[...] = d

        @pl.when(t > 0)
        def _():
            acc[...] += d

        @pl.when(t == nt - 1)
        def _():
            for s in range(spb):
                if row_sharded:
                    o_ref[s] = acc[s * rb:(s + 1) * rb, :].astype(BF16)
                else:
                    o_ref[s] = acc[:, s * (tn // spb):(s + 1) * (tn // spb)].astype(BF16)

    a_spec = (pl.BlockSpec((tt, ka), lambda i, j, t: (t, i)) if layer is None
              else pl.BlockSpec((None, tt, ka), lambda i, j, t: (layer, t, i)))
    ins = [a_spec, pl.BlockSpec((tt, tn), lambda i, j, t: (t, j))]
    args = [a, b]
    kw = {}
    if prev is not None:
        ins.append(ANY)
        args.append(prev)
        kw["input_output_aliases"] = {2: 0}
    return _pc(body, name, (Ka // ka, Nb // tn, nt), ins, out_spec, _sds(out_shape, BF16),
               scratch=[pltpu.VMEM((ka, tn), F32)], sem=("parallel", "parallel", "arbitrary"), **kw)(*args)


def _adamw_math(w, g, m, v):
    c1 = 1.0 - ADAM_B1 ** ADAM_STEP
    c2 = 1.0 - ADAM_B2 ** ADAM_STEP
    mn = ADAM_B1 * m + (1.0 - ADAM_B1) * g
    vn = ADAM_B2 * v + (1.0 - ADAM_B2) * (g * g)
    return -ADAM_LR * ((mn / c1) / (jnp.sqrt(vn / c2) + ADAM_EPS) + ADAM_WD * w), mn, vn


def adamw_layer(w, m, v, layer, gbuf, off, prev, name):
    L, R, W = w.shape
    tr = TILE_ROWS
    assert R % tr == 0 and off % tr == 0

    def body(w_ref, g_ref, m_ref, v_ref, *rest):
        go_ref, d_ref, mo_ref, vo_ref = rest[-4:]
        g = g_ref[...]
        go_ref[...] = g
        d_ref[...], mo_ref[...], vo_ref[...] = _adamw_math(w_ref[...], g, m_ref[...], v_ref[...])

    lay = pl.BlockSpec((None, tr, W), lambda r: (layer, r, 0))
    ins = [lay, pl.BlockSpec((tr, W), lambda r: (off // tr + r, 0)), lay, lay]
    args = [w, gbuf, m, v]
    kw = {}
    if prev is not None:
        ins += [ANY] * 4
        args += list(prev)
        kw["input_output_aliases"] = {4 + k: k for k in range(4)}
    return _pc(body, name, (R // tr,), ins, [lay] * 4, [_sds((L, R, W), F32)] * 4, sem=("parallel",), **kw)(*args)


def adamw_many(ws, gs, ms, vs):
    n = len(ws)

    def body(*refs):
        for k in range(n):
            d, mn, vn = _adamw_math(refs[k][...], refs[n + k][...], refs[2 * n + k][...], refs[3 * n + k][...])
            refs[4 * n + k][...] = d
            refs[5 * n + k][...] = mn
            refs[6 * n + k][...] = vn

    outs = pl.pallas_call(body, name="adamw_small", out_shape=[_sds(a.shape, F32) for a in ws] * 3)(*ws, *gs, *ms, *vs)
    return outs[:n], outs[n:2 * n], outs[2 * n:]


def _rope_tables(T):
    pos = jnp.arange(T, dtype=F32)
    inv_freq = ROPE_THETA ** (-jnp.arange(0, ROPE, 2, dtype=F32) / ROPE)
    ang = pos[:, None] * inv_freq[None, :]
    cos, sin = jnp.cos(ang), jnp.sin(ang)
    pad = HEAD - ROPE
    c = jnp.concatenate([cos, cos, jnp.ones((T, pad), F32)], axis=1)
    s = jnp.concatenate([-sin, sin, jnp.zeros((T, pad), F32)], axis=1)
    return jnp.stack([jnp.tile(c, (1, 128 // HEAD)), jnp.tile(s, (1, 128 // HEAD))])


def _local_step(x, p, target, W, small, lay, hook=None):
    if hook is None:
        hook = lambda stage, after, G, sg=None: None
    T, D = x.shape
    depth = small["mix_ln_g"].shape[0]
    alpha = float((2 * depth) ** 0.25)
    taps = small["taps"]
    row = lambda a, i: a[i:i + 1]
    cs = _rope_tables(T)

    h = conv_in_fwd(x, W["conv_w_in"], small["conv_b_in"])
    cv, s = dwconv_fwd(h, small["conv_w_dw"], small["conv_b_dw"], small["conv_ln_g"], small["conv_ln_b"], taps)
    hook("weights1", s, None)
    pre_mix0, x1, x1b = mm_res_ln(s, W["conv_w_out"], x, row(small["mix_ln_g"], 0), row(small["mix_ln_b"], 0), alpha,
                                  small["conv_b_out"], "conv_out_fwd")
    r0, t0 = mlp_up_fwd(x1b, W["mlp_w_up0"], "mlp_up_fwd0")
    pre_mlp0, x2, x2b = mm_res_ln(r0, W["mlp_w_down0"], x1, row(small["mlp_ln_g"], 0), row(small["mlp_ln_b"], 0), alpha,
                                  None, "mlp_down_fwd0")
    x3, x3b, pp0, gl0 = ple_fwd(x2, x2b, p, 0, W["ple_w_proj0"], W["ple_w_gate0"], None, "ple_fwd0")

    hook("weights2", x3b, None)
    q, k, v = qkv_fwd(x3b, W["attn_w_q"], W["kv_w_k"], W["kv_w_v"], cs)
    o = attn_fwd(q, k, v, small["attn_sinks"])
    pre_mix1, x4, x4b = mm_res_ln(o, W["attn_w_o"], x3, row(small["mix_ln_g"], 1), row(small["mix_ln_b"], 1), alpha,
                                  None, "attn_out_fwd")
    r1, t1 = mlp_up_fwd(x4b, W["mlp_w_up1"], "mlp_up_fwd1")
    pre_mlp1, x5, x5b = mm_res_ln(r1, W["mlp_w_down1"], x4, row(small["mlp_ln_g"], 1), row(small["mlp_ln_b"], 1), alpha,
                                  None, "mlp_down_fwd1")
    dx6, loss, pp1, gl1 = ple_fwd(x5, x5b, p, 1, W["ple_w_proj1"], W["ple_w_gate1"], target, "ple_fwd1")

    G, sg = {}, {}
    where = {n: (key, off) for key in lay for n, off, _ in lay[key]}
    rows_of = {key: sum(r for _, _, r in lay[key]) for key in lay}

    def wg(name, a, b, row_sharded):
        key, off = where[name]
        shape = (NS, rows_of[key], W[name][0].shape[2])
        G[key] = wgrad(a, b, row_sharded, "wg_" + name, (G.get(key), shape, off))

    dpp1, dgl1, dx5 = ple_bwd(dx6, pp1, gl1, W["ple_w_gate1"], "ple_bwd1")
    wg("ple_w_proj1", (1, p), dpp1, False)
    wg("ple_w_gate1", x5b, dgl1, True)
    dpre_mlp1, dpre_mlp1b, dm1, g_mlp_g1, g_mlp_b1 = mlp_bwd1(dx5, pre_mlp1, row(small["mlp_ln_g"], 1), t1,
                                                              W["mlp_w_down1"], "mlp_bwd1_1")
    wg("mlp_w_down1", r1, dpre_mlp1b, True)
    wg("mlp_w_up1", x4b, dm1, False)
    dpre_mix1, dpre_mix1b, do, g_mix_g1, g_mix_b1, _ = mlp_bwd2(dpre_mlp1, dm1, W["mlp_w_up1"], alpha, pre_mix1,
                                                                row(small["mix_ln_g"], 1), W["attn_w_o"], "mlp_bwd2_1")
    wg("attn_w_o", o, dpre_mix1b, True)
    dq, dk, dv, dsinks = attn_bwd(q, k, v, do, small["attn_sinks"])
    dqb, dkb, dvb, dx3 = qkv_bwd(dq, dk, dv, dpre_mix1,
                                 W["attn_w_q"], W["kv_w_k"], W["kv_w_v"], cs, alpha)
    wg("attn_w_q", x3b, dqb, True)
    wg("kv_w_k", x3b, dkb, True)
    wg("kv_w_v", x3b, dvb, True)
    hook("grads3", None, G)

    dpp0, dgl0, dx2 = ple_bwd(dx3, pp0, gl0, W["ple_w_gate0"], "ple_bwd0")
    wg("ple_w_proj0", (0, p), dpp0, False)
    wg("ple_w_gate0", x2b, dgl0, True)
    dpre_mlp0, dpre_mlp0b, dm0, g_mlp_g0, g_mlp_b0 = mlp_bwd1(dx2, pre_mlp0, row(small["mlp_ln_g"], 0), t0,
                                                              W["mlp_w_down0"], "mlp_bwd1_0")
    wg("mlp_w_down0", r0, dpre_mlp0b, True)
    wg("mlp_w_up0", x1b, dm0, False)
    hook("grads2", None, G)
    dpre_mix0, dpre_mix0b, dsw, g_mix_g0, g_mix_b0, g_b_out = mlp_bwd2(dpre_mlp0, dm0, W["mlp_w_up0"], alpha, pre_mix0,
                                                                      row(small["mix_ln_g"], 0), W["conv_w_out"],
                                                                      "mlp_bwd2_0")
    wg("conv_w_out", s, dpre_mix0b, True)
    hook("grads1", None, G)
    dcv, g_cln_g, g_cln_b, g_b_dw = conv_mid_bwd(dsw, cv, small["conv_ln_g"], small["conv_ln_b"])
    dh, g_w_dw, g_b_in = dwconv_bwd(dcv, h, small["conv_w_dw"], taps)
    wg("conv_w_in", x, dh, False)

    sg["conv_b_in"] = g_b_in
    sg["conv_w_dw"] = g_w_dw
    sg["conv_b_dw"], sg["conv_ln_g"], sg["conv_ln_b"], sg["conv_b_out"] = g_b_dw, g_cln_g, g_cln_b, g_b_out
    sg["mix_ln_g"] = [g_mix_g0, g_mix_g1]
    sg["mix_ln_b"] = [g_mix_b0, g_mix_b1]
    sg["mlp_ln_g"] = [g_mlp_g0, g_mlp_g1]
    sg["mlp_ln_b"] = [g_mlp_b0, g_mlp_b1]
    sg["attn_sinks"] = dsinks[:, 0][None, :]
    sg["loss"] = loss
    hook("grads0", None, G, sg)
    grad_x = conv_in_bwd(dh, dpre_mix0, W["conv_w_in"], alpha)
    return loss, grad_x, G, sg


BUFFERS = (("b0", ("conv_w_in",)), ("a0", ("conv_w_out",)),
           ("a1", ("mlp_w_up0", "mlp_w_down0", "ple_w_gate0")), ("c1", ("ple_w_proj0",)),
           ("a2", ("mlp_w_up1", "mlp_w_down1", "ple_w_gate1", "attn_w_q", "attn_w_o")),
           ("c2", ("kv_w_k", "kv_w_v", "ple_w_proj1")))
GROUPS = (("b0",), ("a0", "a1", "c1"), ("a2", "c2"))
REDUCED = (("b0",), ("a0",), ("a1", "c1"), ("a2", "c2"))
ROW_SHARDED = {"mlp_w_down0", "mlp_w_down1", "ple_w_gate0", "ple_w_gate1", "conv_w_out", "attn_w_q", "attn_w_o", "kv_w_k",
               "kv_w_v"}


def _split_layers(weights):
    out = {"conv_w_in": weights["conv_w_in"][0], "conv_w_out": weights["conv_w_out"][0],
           "attn_w_q": weights["attn_w_q"][0], "attn_w_o": weights["attn_w_o"][0],
           "kv_w_k": weights["kv_w_k"], "kv_w_v": weights["kv_w_v"]}
    for n in ("mlp_w_up", "mlp_w_down", "ple_w_proj", "ple_w_gate"):
        for i in range(weights[n].shape[0]):
            out[n + str(i)] = weights[n][i]
    return out


def _layout(shards):
    lay = {}
    for key, names in BUFFERS:
        off, rows = 0, []
        for n in names:
            rows.append((n, off, shards[n].shape[0]))
            off += shards[n].shape[0]
        lay[key] = rows
    return lay


def _place():
    return lax.axis_index("x"), lax.axis_index("y"), lax.axis_index("c")


def _flip(v, f):
    return (v + f) % 2 if f else v


CHIP_FLIPS = ((1, 0), (0, 1), (1, 1))


HBM = pl.BlockSpec(memory_space=pltpu.HBM)
SEM = pl.BlockSpec(memory_space=pltpu.SEMAPHORE)
EFFECT = pltpu.SideEffectType.DATAFLOW_SIDE_EFFECTING


def _half(ref, rows, c):
    return ref.at[pl.ds(pl.multiple_of(c * (rows // 2), 16), rows // 2), :]


def _gather_copies(refs, shapes, whole, send, recv):
    x, y, c = _place()
    me = 2 * x + y
    na = len(refs)
    cps = []
    for d, (fx, fy) in enumerate(CHIP_FLIPS):
        to = (_flip(x, fx), _flip(y, fy), c)
        for k in range(na):
            mine = refs[k].at[me] if k >= na - whole else _half(refs[k].at[me], shapes[k][1], c)
            cps.append(pltpu.make_async_remote_copy(mine, mine, send.at[d * na + k], recv.at[d * na + k], device_id=to,
                                                    device_id_type=MESH))
    return cps


def gather_start(bufs, whole, after, name):
    na = len(bufs)
    shapes = [b.shape for b in bufs]
    nsem = len(CHIP_FLIPS) * na

    def body(*refs):
        ins = refs[:na]
        send, recv = refs[-(na + 3)], refs[-(na + 2)]
        token = refs[-1]
        for cp in _gather_copies(ins, shapes, whole, send, recv):
            cp.start()
        token[...] = jnp.zeros_like(token)

    args = [pltpu.with_memory_space_constraint(b, pltpu.HBM) for b in bufs]
    ins = [HBM] * na
    if after is not None:
        args.append(after)
        ins.append(ANY)
    return pl.pallas_call(
        body, name=name, in_specs=ins,
        out_specs=[SEM, SEM] + [HBM] * na + [pl.BlockSpec(memory_space=pltpu.VMEM)],
        out_shape=[pltpu.SemaphoreType.DMA((nsem,)), pltpu.SemaphoreType.DMA((nsem,))]
        + [pltpu.HBM(b.shape, b.dtype) for b in bufs] + [_sds((8, 128), F32)],
        input_output_aliases={k: k + 2 for k in range(na)},
        compiler_params=pltpu.CompilerParams(has_side_effects=EFFECT))(*args)


def gather_wait(send, recv, bufs, whole, after, name):
    na = len(bufs)
    shapes = [b.shape for b in bufs]

    def body(*refs):
        ins = refs[:na]
        send_ref, recv_ref = refs[na], refs[na + 1]
        for cp in _gather_copies(ins, shapes, whole, send_ref, recv_ref):
            cp.wait_send()
            cp.wait_recv()

    return pl.pallas_call(
        body, name=name, in_specs=[HBM] * na + [SEM, SEM, ANY], out_specs=[HBM] * na,
        out_shape=[pltpu.HBM(b.shape, b.dtype) for b in bufs], input_output_aliases={k: k for k in range(na)},
        compiler_params=pltpu.CompilerParams(has_side_effects=EFFECT))(*bufs, send, recv, after)


def sibling_forward(bufs, name):
    nb = len(bufs)

    def body(*refs):
        outs = refs[nb:2 * nb]
        send, recv = refs[2 * nb:]
        x, y, c = _place()
        cps = []
        for d, (fx, fy) in enumerate(CHIP_FLIPS):
            frm = 2 * _flip(x, fx) + _flip(y, fy)
            for k in range(nb):
                theirs = _half(outs[k].at[frm], bufs[k].shape[1], c)
                cps.append(pltpu.make_async_remote_copy(theirs, theirs, send.at[d * nb + k], recv.at[d * nb + k],
                                                        device_id=(x, y, 1 - c), device_id_type=MESH))
        for cp in cps:
            cp.start()
        for cp in cps:
            cp.wait()

    nsem = len(CHIP_FLIPS) * nb
    return pl.pallas_call(
        body, name=name, in_specs=[ANY] * nb, out_specs=[ANY] * nb, out_shape=[_sds(b.shape, b.dtype) for b in bufs],
        input_output_aliases={k: k for k in range(nb)},
        scratch_shapes=[pltpu.SemaphoreType.DMA((nsem,)), pltpu.SemaphoreType.DMA((nsem,))])(*bufs)


def pack_rows(pieces, rows, width, name):
    def body(*refs):
        o_ref = refs[-1]
        o_ref[...] = jnp.zeros_like(o_ref)
        for ref, (a, off) in zip(refs[:-1], pieces):
            o_ref[off:off + a.shape[0], 0:a.shape[1]] = ref[...]

    return pl.pallas_call(body, name=name, out_shape=_sds((rows, width), F32))(*[a for a, _ in pieces])


PEER_FLIPS = tuple((fx, fy, fc) for fx in (0, 1) for fy in (0, 1) for fc in (0, 1) if fx or fy or fc)


def _reduce_copies(parts, zones, pack, send, recv):
    x, y, c = _place()
    nb = len(parts)
    na = nb + (1 if pack is not None else 0)
    cps = []
    for f, (fx, fy, fc) in enumerate(PEER_FLIPS):
        tx, ty, tc = _flip(x, fx), _flip(y, fy), _flip(c, fc)
        for k in range(nb):
            hrows = parts[k].shape[1] // 2
            piece = parts[k].at[2 * tx + ty, pl.ds(pl.multiple_of(tc * hrows, 16), hrows), :]
            cps.append(pltpu.make_async_remote_copy(piece, zones[k].at[f], send.at[f * na + k], recv.at[f * na + k],
                                                    device_id=(tx, ty, tc), device_id_type=MESH))
        if pack is not None:
            mine = pack.at[4 * x + 2 * y + c]
            cps.append(pltpu.make_async_remote_copy(mine, mine, send.at[f * na + nb], recv.at[f * na + nb],
                                                    device_id=(tx, ty, tc), device_id_type=MESH))
    return cps


def reduce_begin(parts, pack, name):
    nb = len(parts)
    zones = [lax.empty((len(PEER_FLIPS), g.shape[1] // 2, g.shape[2]), g.dtype) for g in parts]
    arrs = list(parts) + zones + ([pack] if pack is not None else [])
    na = len(arrs)
    nsem = len(PEER_FLIPS) * (nb + (1 if pack is not None else 0))

    def body(*refs):
        ins = refs[:na]
        send, recv = refs[na], refs[na + 1]
        for cp in _reduce_copies(ins[:nb], ins[nb:2 * nb], ins[2 * nb] if pack is not None else None, send, recv):
            cp.start()
        refs[-1][...] = jnp.zeros_like(refs[-1])

    return pl.pallas_call(
        body, name=name, in_specs=[HBM] * na,
        out_specs=[SEM, SEM] + [HBM] * na + [pl.BlockSpec(memory_space=pltpu.VMEM)],
        out_shape=[pltpu.SemaphoreType.DMA((nsem,)), pltpu.SemaphoreType.DMA((nsem,))]
        + [pltpu.HBM(a.shape, a.dtype) for a in arrs] + [_sds((8, 128), F32)],
        input_output_aliases={k: k + 2 for k in range(na)},
        compiler_params=pltpu.CompilerParams(has_side_effects=EFFECT))(
            *[pltpu.with_memory_space_constraint(a, pltpu.HBM) for a in arrs])


def reduce_end(send, recv, parts, zones, pack, after, name):
    nb = len(parts)
    arrs = list(parts) + list(zones) + ([pack] if pack is not None else [])
    na = len(arrs)

    def body(*refs):
        ins = refs[:na]
        for cp in _reduce_copies(ins[:nb], ins[nb:2 * nb], ins[2 * nb] if pack is not None else None, refs[na], refs[na + 1]):
            cp.wait_send()
            cp.wait_recv()

    return pl.pallas_call(
        body, name=name, in_specs=[HBM] * na + [SEM, SEM, ANY], out_specs=[HBM] * na,
        out_shape=[pltpu.HBM(a.shape, a.dtype) for a in arrs], input_output_aliases={k: k for k in range(na)},
        compiler_params=pltpu.CompilerParams(has_side_effects=EFFECT))(*arrs, send, recv, after)


def sibling_share(halves, name):
    nb = len(halves)

    def body(*refs):
        outs = refs[nb:2 * nb]
        send, recv = refs[2 * nb:]
        x, y, c = _place()
        cps = []
        for k in range(nb):
            hrows = halves[k].shape[0] // 2
            mine = outs[k].at[pl.ds(pl.multiple_of(c * hrows, 8), hrows), :]
            cps.append(pltpu.make_async_remote_copy(mine, mine, send.at[k], recv.at[k], device_id=(x, y, 1 - c),
                                                    device_id_type=MESH))
        for cp in cps:
            cp.start()
        for cp in cps:
            cp.wait()

    return pl.pallas_call(
        body, name=name, in_specs=[ANY] * nb, out_specs=[ANY] * nb,
        out_shape=[_sds(h.shape, h.dtype) for h in halves], input_output_aliases={k: k for k in range(nb)},
        scratch_shapes=[pltpu.SemaphoreType.DMA((nb,)), pltpu.SemaphoreType.DMA((nb,))])(*halves)


def _row_tile(rows):
    for cand in (512, 384, 256, 128, 64, 32, 16):
        if rows % cand == 0:
            return cand
    return rows


def piece_sum(g, z, idx, name):
    _, hrows, W = z.shape
    tr = _row_tile(hrows)
    nrb = hrows // tr

    def body(idx_ref, g_ref, z_ref, o_ref):
        acc = g_ref[...].astype(F32)
        for d in range(z.shape[0]):
            acc = acc + z_ref[d].astype(F32)
        o_ref[...] = acc

    gs = pltpu.PrefetchScalarGridSpec(
        num_scalar_prefetch=1, grid=(nrb,),
        in_specs=[pl.BlockSpec((None, tr, W), lambda i, sc: (sc[0], sc[1] * nrb + i, 0)),
                  pl.BlockSpec((z.shape[0], tr, W), lambda i, sc: (0, i, 0))],
        out_specs=pl.BlockSpec((tr, W), lambda i, sc: (sc[1] * nrb + i, 0)))
    return pl.pallas_call(body, name=name, grid_spec=gs, out_shape=_sds((2 * hrows, W), F32),
                          compiler_params=pltpu.CompilerParams(dimension_semantics=("parallel",),
                                                               vmem_limit_bytes=VMEM_LIMIT_MIB * 2 ** 20))(idx, g, z)


def small_sum(packs):
    n, R, W = packs.shape

    def body(p_ref, o_ref):
        acc = p_ref[0]
        for d in range(1, n):
            acc = acc + p_ref[d]
        o_ref[...] = acc

    return pl.pallas_call(body, name="small_sum", out_shape=_sds((R, W), F32))(packs)


WEIGHTS = ["conv_w_in", "conv_b_in", "conv_w_dw", "conv_b_dw", "conv_ln_g", "conv_ln_b", "conv_w_out", "conv_b_out", "kv_w_k",
           "kv_w_v", "attn_w_q", "attn_sinks", "attn_w_o", "mix_ln_g", "mix_ln_b", "mlp_w_up", "mlp_w_down", "mlp_ln_g",
           "mlp_ln_b", "ple_w_proj", "ple_w_gate"]
BIG = ["conv_w_in", "conv_w_out", "kv_w_k", "kv_w_v", "attn_w_q", "attn_w_o", "mlp_w_up", "mlp_w_down", "ple_w_proj",
       "ple_w_gate"]
SMALL = [n for n in WEIGHTS if n not in BIG]


def _step(x, p, target, w, m, v):
    D = x.shape[-1]
    ds = D // NS
    xq, yq, cq = _place()
    chip = 2 * xq + yq
    idx = jnp.stack([chip, cq]).astype(jnp.int32)

    shards = _split_layers(w)
    lay = _layout(shards)
    taps = w["conv_w_dw"].shape[1]
    small_loc = pack_rows([(w["conv_w_dw"][0], 0), (w["conv_b_dw"], HALO), (w["conv_ln_g"], HALO + 1), (w["conv_ln_b"], HALO + 2),
                           (w["conv_b_out"], HALO + 3), (w["conv_b_in"].reshape(2, ds), HALO + 4)], HALO + 8, ds, "pack_small")
    slot = lambda a: lax.dynamic_update_slice(lax.empty((NS,) + a.shape, a.dtype), a[None], (chip, 0, 0))
    started, token = [], None
    for gi, keys in enumerate(GROUPS):
        bufs = [slot(jnp.concatenate([shards[n].astype(BF16) for n, _, _ in lay[key]], axis=0)) for key in keys]
        if gi == 0:
            bufs.append(slot(small_loc))
        send, recv, *thru, token = gather_start(bufs, 1 if gi == 0 else 0, token, "gather_start%d" % gi)
        started.append((send, recv, thru))
    W = {}

    def arrive(gi, after):
        send, recv, thru = started[gi]
        whole = 1 if gi == 0 else 0
        got = gather_wait(send, recv, thru, whole, after, "gather_wait%d" % gi)
        nk = len(GROUPS[gi])
        for key, buf in zip(GROUPS[gi], sibling_forward(got[:nk], "sibling_forward%d" % gi)):
            for n, off, rows in lay[key]:
                W[n] = (buf, off, rows)
        return got[nk:]

    gs, = arrive(0, token)
    across = lambda rows: gs[:, rows, :].transpose(1, 0, 2).reshape(rows.stop - rows.start, D)
    small = {"taps": taps, "conv_w_dw": across(slice(0, HALO)), "conv_b_dw": across(slice(HALO, HALO + 1)),
             "conv_ln_g": across(slice(HALO + 1, HALO + 2)), "conv_ln_b": across(slice(HALO + 2, HALO + 3)),
             "conv_b_out": across(slice(HALO + 3, HALO + 4)), "conv_b_in": gs[:, HALO + 4:HALO + 6, :].reshape(1, 2 * D),
             "attn_sinks": w["attn_sinks"], "mix_ln_g": w["mix_ln_g"], "mix_ln_b": w["mix_ln_b"],
             "mlp_ln_g": w["mlp_ln_g"], "mlp_ln_b": w["mlp_ln_b"]}

    reducing = {}

    def reduce_start(gi, G, pack):
        nk = len(REDUCED[gi])
        send, recv, *thru, token = reduce_begin([G[key] for key in REDUCED[gi]], pack, "reduce_begin%d" % gi)
        reducing[gi] = (send, recv, thru[:nk], thru[nk:2 * nk], thru[2 * nk] if pack is not None else None)
        _FOLLOW.append(token)

    def small_pack(sg):
        pieces = [(sg["conv_b_in"].reshape(2, D), 0), (sg["conv_w_dw"], 2)]
        r0 = 2 + HALO
        for i, n in enumerate(("conv_b_dw", "conv_ln_g", "conv_ln_b", "conv_b_out")):
            pieces.append((sg[n], r0 + i))
        r0 += 4
        for i, n in enumerate(("mix_ln_g", "mix_ln_b", "mlp_ln_g", "mlp_ln_b")):
            pieces += [(sg[n][0], r0 + 2 * i), (sg[n][1], r0 + 2 * i + 1)]
        pieces += [(sg["attn_sinks"], r0 + 8), (sg["loss"][0:1], r0 + 9)]
        mine = pack_rows(pieces, r0 + 10, D, "pack_small_grads")
        return lax.dynamic_update_slice(lax.empty((8,) + mine.shape, F32), mine[None], (4 * xq + 2 * yq + cq, 0, 0))

    def hook(stage, after, G, sg=None):
        if stage == "weights1":
            arrive(1, after)
        elif stage == "weights2":
            arrive(2, after)
        elif stage == "grads0":
            reduce_start(0, G, small_pack(sg))
        elif stage.startswith("grads"):
            reduce_start(int(stage[5:]), G, None)

    _, grad_x, G, sg = _local_step(x[0], p[:, 0], target[0], W, small, lay, hook)
    _FOLLOW.clear()
    nsink = w["attn_sinks"].shape[1]

    grads, delta, new_m, new_v = {}, {}, {}, {}
    found = {}

    def finish(groups, after, tag):
        keys, halves, tot = [], [], None
        for gi in groups:
            send, recv, parts, zones, pack = reducing[gi]
            done = reduce_end(send, recv, parts, zones, pack, after, "reduce_end%d" % gi)
            nk = len(REDUCED[gi])
            for key, g_, z_ in zip(REDUCED[gi], done[:nk], done[nk:2 * nk]):
                keys.append(key)
                halves.append(piece_sum(g_, z_, idx, "piece_sum_" + key))
            if pack is not None:
                tot = small_sum(done[2 * nk])
        for key, buf in zip(keys, sibling_share(halves, "sibling_share" + tag)):
            for n, off, _ in lay[key]:
                found[n] = (buf, off)
        return tot

    def big_adamw(names):
        for n in names:
            three = lambda a: a.reshape((-1,) + a.shape[-2:])
            w3, m3, v3 = three(w[n]), three(m[n]), three(v[n])
            outs = None
            for i in range(w3.shape[0]):
                buf, off = found[n + str(i)] if n + str(i) in found else found[n]
                outs = adamw_layer(w3, m3, v3, i, buf, off, outs, "adamw_%s%d" % (n, i))
            grads[n], delta[n], new_m[n], new_v[n] = [a.reshape(w[n].shape) for a in outs]

    last = [n for n, _, _ in lay[REDUCED[0][0]]]
    finish(reversed(range(1, len(REDUCED))), grad_x, "1")
    big_adamw([n for n in BIG if n not in last])
    tot = finish([0], new_v["mlp_w_down"], "0")
    big_adamw(last)
    cols = lambda rows: lax.dynamic_slice(rows, (0, chip * ds), (rows.shape[0], ds))
    grads["conv_b_in"] = lax.dynamic_slice(tot[0:2].reshape(1, 2 * D), (0, chip * 2 * ds), (1, 2 * ds))
    grads["conv_w_dw"] = cols(tot[2:2 + taps])[None]
    r0 = 2 + HALO
    for i, n in enumerate(("conv_b_dw", "conv_ln_g", "conv_ln_b", "conv_b_out")):
        grads[n] = cols(tot[r0 + i:r0 + i + 1])
    r0 += 4
    for i, n in enumerate(("mix_ln_g", "mix_ln_b", "mlp_ln_g", "mlp_ln_b")):
        grads[n] = tot[r0 + 2 * i:r0 + 2 * i + 2]
    grads["attn_sinks"] = tot[r0 + 8:r0 + 9, 0:nsink]

    ds_, ms_, vs_ = adamw_many([w[n] for n in SMALL], [grads[n] for n in SMALL], [m[n] for n in SMALL], [v[n] for n in SMALL])
    for n, d_, m_, v_ in zip(SMALL, ds_, ms_, vs_):
        delta[n], new_m[n], new_v[n] = d_, m_, v_

    total = tot[r0 + 9, 0]
    return (total, grad_x[None], *[grads[n] for n in WEIGHTS], *[delta[n] for n in WEIGHTS], *[new_m[n] for n in WEIGHTS],
            *[new_v[n] for n in WEIGHTS])


def kernel(x, p, conv_w_in, conv_b_in, conv_w_dw, conv_b_dw, conv_ln_g, conv_ln_b, conv_w_out, conv_b_out, kv_w_k, kv_w_v, attn_w_q, attn_sinks, attn_w_o, mix_ln_g, mix_ln_b, mlp_w_up, mlp_w_down, mlp_ln_g, mlp_ln_b, ple_w_proj, ple_w_gate, loss_target, m_conv_w_in, m_conv_b_in, m_conv_w_dw, m_conv_b_dw, m_conv_ln_g, m_conv_ln_b, m_conv_w_out, m_conv_b_out, m_kv_w_k, m_kv_w_v, m_attn_w_q, m_attn_sinks, m_attn_w_o, m_mix_ln_g, m_mix_ln_b, m_mlp_w_up, m_mlp_w_down, m_mlp_ln_g, m_mlp_ln_b, m_ple_w_proj, m_ple_w_gate, v_conv_w_in, v_conv_b_in, v_conv_w_dw, v_conv_b_dw, v_conv_ln_g, v_conv_ln_b, v_conv_w_out, v_conv_b_out, v_kv_w_k, v_kv_w_v, v_attn_w_q, v_attn_sinks, v_attn_w_o, v_mix_ln_g, v_mix_ln_b, v_mlp_w_up, v_mlp_w_down, v_mlp_ln_g, v_mlp_ln_b, v_ple_w_proj, v_ple_w_gate):
    w = dict(zip(WEIGHTS, (conv_w_in, conv_b_in, conv_w_dw, conv_b_dw, conv_ln_g, conv_ln_b, conv_w_out, conv_b_out, kv_w_k,
                           kv_w_v, attn_w_q, attn_sinks, attn_w_o, mix_ln_g, mix_ln_b, mlp_w_up, mlp_w_down, mlp_ln_g, mlp_ln_b,
                           ple_w_proj, ple_w_gate)))
    m = dict(zip(WEIGHTS, (m_conv_w_in, m_conv_b_in, m_conv_w_dw, m_conv_b_dw, m_conv_ln_g, m_conv_ln_b, m_conv_w_out,
                           m_conv_b_out, m_kv_w_k, m_kv_w_v, m_attn_w_q, m_attn_sinks, m_attn_w_o, m_mix_ln_g, m_mix_ln_b,
                           m_mlp_w_up, m_mlp_w_down, m_mlp_ln_g, m_mlp_ln_b, m_ple_w_proj, m_ple_w_gate)))
    v = dict(zip(WEIGHTS, (v_conv_w_in, v_conv_b_in, v_conv_w_dw, v_conv_b_dw, v_conv_ln_g, v_conv_ln_b, v_conv_w_out,
                           v_conv_b_out, v_kv_w_k, v_kv_w_v, v_attn_w_q, v_attn_sinks, v_attn_w_o, v_mix_ln_g, v_mix_ln_b,
                           v_mlp_w_up, v_mlp_w_down, v_mlp_ln_g, v_mlp_ln_b, v_ple_w_proj, v_ple_w_gate)))
    return _step(x, p, loss_target, w, m, v)
```

```python
import jax
import jax.numpy as jnp
from jax import lax
from jax.experimental import pallas as pl
from jax.experimental.pallas import tpu as pltpu

F32 = jnp.float32
BF16 = jnp.bfloat16
NS = 4
HEAD = 64
BLK = 128
ROPE = 16
ROPE_THETA = 500000.0
LN_EPS = 1e-5
NEG = -1e30
VMEM_LIMIT_MIB = 56
TILE_ROWS = 256
TILE_ROWS_WIDE = 512
BLOCK_DIM = 1024
KV_PER_STAGE = 1
HALO = 32
ADAM_LR, ADAM_B1, ADAM_B2, ADAM_EPS, ADAM_WD, ADAM_STEP = 0.001, 0.9, 0.999, 1e-08, 0.01, 10
MESH = pl.DeviceIdType.MESH
ANY = pl.BlockSpec(memory_space=pl.ANY)
NT = (((1,), (1,)), ((), ()))
TN = (((0,), (0,)), ((), ()))


_FOLLOW = []


def _pc(body, name, grid, in_specs, out_specs, out_shape, scratch=(), sem=None, vmem=VMEM_LIMIT_MIB, **kw):
    call = lambda fn, ins: pl.pallas_call(
        fn, name=name, grid=grid, in_specs=ins, out_specs=out_specs, out_shape=out_shape,
        scratch_shapes=list(scratch),
        compiler_params=pltpu.CompilerParams(dimension_semantics=sem, vmem_limit_bytes=vmem * 2 ** 20), **kw)
    if not _FOLLOW:
        return call(body, in_specs)
    extra = list(_FOLLOW)
    _FOLLOW.clear()
    n_in = len(in_specs)

    def ordered(*refs):
        return body(*refs[:n_in], *refs[n_in + len(extra):])

    run = call(ordered, list(in_specs) + [ANY] * len(extra))
    return lambda *args: run(*args, *extra)


def _rows(tm, n):
    return pl.BlockSpec((tm, n), lambda i: (i, 0))


def _const(shape):
    return pl.BlockSpec(shape, lambda *_: (0,) * len(shape))


def _wspec(w):
    buf, off, rows = w
    assert off % rows == 0
    return pl.BlockSpec((NS, rows, buf.shape[2]), lambda *_: (0, off // rows, 0))


def _rows_joined(w_ref):
    n, r, c = w_ref.shape
    return w_ref[...].reshape(n * r, c)


def _sds(shape, dtype):
    return jax.ShapeDtypeStruct(shape, dtype)


def _tile(t, rows=TILE_ROWS):
    return min(rows, t)


def _sigmoid(x):
    return 0.5 * jnp.tanh(0.5 * x) + 0.5


def _ln_stats(w):
    mu = jnp.mean(w, axis=-1, keepdims=True)
    xc = w - mu
    var = jnp.mean(xc * xc, axis=-1, keepdims=True)
    rstd = lax.rsqrt(var + LN_EPS)
    return xc * rstd, rstd, mu


def _ln_bwd(dy, w, g, stats=None):
    if stats is None:
        xhat, rstd, _ = _ln_stats(w)
    else:
        mu, rstd = stats
        xhat = (w - mu) * rstd
    dxhat = dy * g
    m1 = jnp.mean(dxhat, axis=-1, keepdims=True)
    m2 = jnp.mean(dxhat * xhat, axis=-1, keepdims=True)
    dw = rstd * (dxhat - m1 - xhat * m2)
    return dw, jnp.sum(dy * xhat, axis=0, keepdims=True), jnp.sum(dy, axis=0, keepdims=True)


def _acc_rows(ref, val, first):
    @pl.when(first)
    def _():
        ref[...] = val

    @pl.when(jnp.logical_not(first))
    def _():
        ref[...] += val


def conv_in_fwd(xb, w_in, b_in):
    T, D = xb.shape
    nw = w_in[0].shape[2]
    tm = _tile(T, TILE_ROWS_WIDE)

    def body(x_ref, w_ref, b_ref, h_ref):
        x = x_ref[...].astype(BF16)
        for j in range(NS):
            sl = slice(j * nw, (j + 1) * nw)
            h_ref[:, sl] = (jnp.dot(x, w_ref[j], preferred_element_type=F32) + b_ref[:, sl]).astype(BF16)

    return _pc(body, "conv_in_fwd", (T // tm,), [_rows(tm, D), _wspec(w_in), _const((1, NS * nw))],
               _rows(tm, NS * nw), _sds((T, NS * nw), BF16), sem=("parallel",))(xb, w_in[0], b_in)


CONV_ROWS = 16


def _phases(scr, sh):
    n = scr.shape[0] - 8
    for b in range(1, 8):
        sh[b - 1, 0:n, :] = scr[b:b + n, :]


def _spread(w_ref, wb, taps):
    for j in range(taps):
        wb[j] = jnp.broadcast_to(w_ref[j:j + 1, :], wb.shape[1:])


def _tap(scr, sh, o, n):
    b = o % 8
    return scr[o:o + n, :] if b == 0 else sh[b - 1, o - b:o - b + n, :]


def dwconv_fwd(h, w_dw, b_dw, ln_g, ln_b, taps):
    T = h.shape[0]
    C = h.shape[1] // 2
    tq = _tile(T)
    nh = tq // HALO
    off = HALO - (taps - 1)

    def body(a_ref, g_ref, ap_ref, gp_ref, w_ref, bdw_ref, lg_ref, lb_ref, cv_ref, s_ref, scr, sh, wb):
        i = pl.program_id(0)
        scr[HALO:HALO + tq, :] = a_ref[...].astype(F32) * _sigmoid(g_ref[...].astype(F32))
        up = ap_ref[...].astype(F32) * _sigmoid(gp_ref[...].astype(F32))
        scr[0:HALO, :] = jnp.where(i > 0, up, 0.0)
        _phases(scr, sh)
        _spread(w_ref, wb, taps)
        bias = jnp.broadcast_to(bdw_ref[...], (8, C))
        for r in range(tq // CONV_ROWS):
            accs = [bias] * (CONV_ROWS // 8)
            for j in range(taps):
                wj = wb[j]
                accs = [acc + wj * _tap(scr, sh, off + j + r * CONV_ROWS + 8 * k, 8) for k, acc in enumerate(accs)]
            for k, acc in enumerate(accs):
                cv_ref[r * CONV_ROWS + 8 * k:r * CONV_ROWS + 8 * k + 8, :] = acc
        xhat, _, _ = _ln_stats(cv_ref[...])
        ln = xhat * lg_ref[...] + lb_ref[...]
        s_ref[...] = (ln * _sigmoid(ln)).astype(BF16)

    prev = lambda col: pl.BlockSpec((HALO, C), lambda i: (jnp.maximum(i * nh - 1, 0), col))
    cur = lambda col: pl.BlockSpec((tq, C), lambda i: (i, col))
    return _pc(body, "dwconv_fwd", (T // tq,),
               [cur(0), cur(1), prev(0), prev(1), _const((HALO, C)), _const((1, C)), _const((1, C)), _const((1, C))],
               [_rows(tq, C), _rows(tq, C)], [_sds((T, C), F32), _sds((T, C), BF16)],
               scratch=[pltpu.VMEM((HALO + tq, C), F32), pltpu.VMEM((7, HALO + tq, C), F32), pltpu.VMEM((taps, 8, C), F32)],
               sem=("parallel",))(h, h, h, h, w_dw, b_dw, ln_g, ln_b)


def mm_res_ln(a, w, res, g, b, alpha, bias, name):
    T, K = a.shape
    D = res.shape[1]
    tm = _tile(T, TILE_ROWS_WIDE)

    def body(*refs):
        a_ref, w_ref, res_ref, g_ref, b_ref = refs[:5]
        n = 5
        if bias is not None:
            bias_ref = refs[5]
            n = 6
        pre_ref, mu_ref, rs_ref, xo_ref, xb_ref = refs[n:n + 5]
        acc = jnp.dot(a_ref[...], _rows_joined(w_ref), preferred_element_type=F32)
        if bias is not None:
            acc = acc + bias_ref[...]
        pre = alpha * res_ref[...] + acc
        xhat, rstd, mu = _ln_stats(pre)
        xo = xhat * g_ref[...] + b_ref[...]
        pre_ref[...] = pre
        mu_ref[...] = mu
        rs_ref[...] = rstd
        xo_ref[...] = xo
        xb_ref[...] = xo.astype(BF16)

    ins = [_rows(tm, K), _wspec(w), _rows(tm, D), _const((1, D)), _const((1, D))]
    args = [a, w[0], res, g, b]
    if bias is not None:
        ins.append(_const((1, D)))
        args.append(bias)
    pre, mu, rstd, xo, xb = _pc(
        body, name, (T // tm,), ins, [_rows(tm, D), _rows(tm, 1), _rows(tm, 1), _rows(tm, D), _rows(tm, D)],
        [_sds((T, D), F32), _sds((T, 1), F32), _sds((T, 1), F32), _sds((T, D), F32), _sds((T, D), BF16)],
        sem=("parallel",))(*args)
    return (pre, mu, rstd), xo, xb


def mlp_up_fwd(xb, w_up, name):
    T, D = xb.shape
    fs = w_up[0].shape[2]
    tm = _tile(T, TILE_ROWS_WIDE)

    def body(x_ref, w_ref, r_ref, t_ref):
        x = x_ref[...]
        for j in range(NS):
            sl = slice(j * fs, (j + 1) * fs)
            m = jnp.maximum(jnp.dot(x, w_ref[j], preferred_element_type=F32), 0.0)
            r_ref[:, sl] = (m * m).astype(BF16)
            t_ref[:, sl] = (2.0 * m).astype(BF16)

    return _pc(body, name, (T // tm,), [_rows(tm, D), _wspec(w_up)], [_rows(tm, NS * fs)] * 2,
               [_sds((T, NS * fs), BF16)] * 2, sem=("parallel",))(xb, w_up[0])


def ple_fwd(x, xb, p, layer, w_proj, w_gate, target, name):
    T, D = x.shape
    P = p.shape[2]
    ds = D // NS
    tm = _tile(T, TILE_ROWS_WIDE)
    last = target is not None

    def body(*refs):
        x_ref, xb_ref, p_ref, wp_ref, wg_ref = refs[:5]
        n = 5
        if last:
            t_ref = refs[5]
            n = 6
        o_ref, o2_ref, pp_ref, gl_ref = refs[n:n + 4]
        gl = jnp.dot(xb_ref[...], _rows_joined(wg_ref), preferred_element_type=F32)
        gl_ref[...] = gl.astype(BF16)
        sg = _sigmoid(gl)
        pb = p_ref[...].astype(BF16)
        sq = jnp.zeros((1, 1), F32)
        for j in range(NS):
            sl = slice(j * ds, (j + 1) * ds)
            pp = jnp.dot(pb, wp_ref[j], preferred_element_type=F32)
            pp_ref[:, sl] = pp.astype(BF16)
            out = x_ref[:, sl] + pp * sg[:, sl]
            if last:
                err = out - t_ref[:, sl]
                o_ref[:, sl] = err * (1.0 / D)
                e2 = jnp.sum(err * err, axis=0, keepdims=True)
                sq = sq + jnp.sum(e2, axis=1, keepdims=True)
            else:
                o_ref[:, sl] = out
                o2_ref[:, sl] = out.astype(BF16)
        if last:
            _acc_rows(o2_ref, jnp.broadcast_to(sq * (0.5 / D), (8, 128)), pl.program_id(0) == 0)

    ins = [_rows(tm, D), _rows(tm, D), pl.BlockSpec((None, tm, P), lambda i: (layer, i, 0)), _wspec(w_proj), _wspec(w_gate)]
    args = [x, xb, p, w_proj[0], w_gate[0]]
    if last:
        ins.append(_rows(tm, D))
        args.append(target)
        outs = [_rows(tm, D), _const((8, 128)), _rows(tm, D), _rows(tm, D)]
        shapes = [_sds((T, D), F32), _sds((8, 128), F32), _sds((T, D), BF16), _sds((T, D), BF16)]
    else:
        outs = [_rows(tm, D)] * 4
        shapes = [_sds((T, D), F32), _sds((T, D), BF16), _sds((T, D), BF16), _sds((T, D), BF16)]
    return _pc(body, name, (T // tm,), ins, outs, shapes, sem=("arbitrary",) if last else ("parallel",))(*args)


def _rope(x, cs_ref, sign):
    c = cs_ref[0]
    s = cs_ref[1] * sign
    lane = lax.broadcasted_iota(jnp.int32, c.shape, 1)
    first = (lane % HEAD) < (ROPE // 2)
    outs = []
    for gq in range(x.shape[1] // 128):
        xg = x[:, gq * 128:(gq + 1) * 128]
        sw = jnp.where(first, pltpu.roll(xg, 128 - ROPE // 2, 1), pltpu.roll(xg, ROPE // 2, 1))
        outs.append(xg * c + sw * s)
    return outs


def qkv_fwd(xb, w_q, w_k, w_v, cs):
    T, D = xb.shape
    HD, KVD = w_q[0].shape[2], w_k[0].shape[2]
    tm = _tile(T, TILE_ROWS_WIDE)
    scale = 1.0 / (HEAD ** 0.5)

    def body(x_ref, wq_ref, wk_ref, wv_ref, cs_ref, q_ref, k_ref, v_ref):
        def proj(w_ref):
            return jnp.dot(x_ref[...], _rows_joined(w_ref), preferred_element_type=F32)

        for gq, val in enumerate(_rope(proj(wq_ref), cs_ref, 1.0)):
            q_ref[:, gq * 128:(gq + 1) * 128] = (val * scale).astype(BF16)
        for gq, val in enumerate(_rope(proj(wk_ref), cs_ref, 1.0)):
            k_ref[:, gq * 128:(gq + 1) * 128] = val.astype(BF16)
        v_ref[...] = proj(wv_ref).astype(BF16)

    cs_spec = pl.BlockSpec((2, tm, 128), lambda i: (0, i, 0))
    return _pc(body, "qkv_fwd", (T // tm,), [_rows(tm, D), _wspec(w_q), _wspec(w_k), _wspec(w_v), cs_spec],
               [_rows(tm, HD), _rows(tm, KVD), _rows(tm, KVD)],
               [_sds((T, HD), BF16), _sds((T, KVD), BF16), _sds((T, KVD), BF16)], sem=("parallel",))(
                   xb, w_q[0], w_k[0], w_v[0], cs)


def _band_mask(n):
    row = lax.broadcasted_iota(jnp.int32, (BLK, 2 * BLK), 0)
    col = lax.broadcasted_iota(jnp.int32, (BLK, 2 * BLK), 1)
    return (col > row) & (col <= row + BLK) & ((col >= BLK) | (n > 0))


def _head(h):
    return slice(h * HEAD, (h + 1) * HEAD)


def _softmax_sink(s, sink):
    m = jnp.maximum(jnp.max(s, axis=-1, keepdims=True), sink)
    e = jnp.exp(s - m)
    es = jnp.exp(sink - m)
    den = jnp.sum(e, axis=-1, keepdims=True) + es
    inv = 1.0 / den
    return e * inv, es * inv


def attn_fwd(q, k, v, sinks):
    T, HD = q.shape
    KVD = k.shape[1]
    NKV = KVD // HEAD
    G = HD // KVD

    def body(s_ref, q_ref, kc_ref, kp_ref, vc_ref, vp_ref, o_ref):
        valid = _band_mask(pl.program_id(0))
        NH = NKV * G
        k2 = [jnp.concatenate([kp_ref[:, _head(kh)], kc_ref[:, _head(kh)]], axis=0) for kh in range(NKV)]
        v2 = [jnp.concatenate([vp_ref[:, _head(kh)], vc_ref[:, _head(kh)]], axis=0) for kh in range(NKV)]
        sc = [lax.dot_general(q_ref[:, _head(hh)], k2[hh // G], NT, preferred_element_type=F32) for hh in range(NH)]
        pb = [_softmax_sink(jnp.where(valid, s, NEG), s_ref[0, hh])[0].astype(BF16) for hh, s in enumerate(sc)]
        for hh, p in enumerate(pb):
            o_ref[:, _head(hh)] = jnp.dot(p, v2[hh // G], preferred_element_type=F32).astype(BF16)

    cur = lambda n_: pl.BlockSpec((BLK, n_), lambda n: (n, 0))
    prev = lambda n_: pl.BlockSpec((BLK, n_), lambda n: (jnp.maximum(n - 1, 0), 0))
    return _pc(body, "attn_fwd", (T // BLK,),
               [pl.BlockSpec(memory_space=pltpu.SMEM), cur(HD), cur(KVD), prev(KVD), cur(KVD), prev(KVD)],
               cur(HD), _sds((T, HD), BF16), sem=("parallel",))(sinks, q, k, k, v, v)


def ple_bwd(dxo, pp, gl, w_gate, name):
    T, D = dxo.shape
    tm = _tile(T, TILE_ROWS_WIDE)

    def body(d_ref, pp_ref, gl_ref, wg_ref, dpp_ref, dgl_ref, dx_ref):
        d = d_ref[...]
        sg = _sigmoid(gl_ref[...].astype(F32))
        dpp_ref[...] = (d * sg).astype(BF16)
        dgl = (d * pp_ref[...].astype(F32) * sg * (1.0 - sg)).astype(BF16)
        dgl_ref[...] = dgl
        dx_ref[...] = d + lax.dot_general(dgl, _rows_joined(wg_ref), NT, preferred_element_type=F32)

    return _pc(body, name, (T // tm,), [_rows(tm, D)] * 3 + [_wspec(w_gate)], [_rows(tm, D)] * 3,
               [_sds((T, D), BF16), _sds((T, D), BF16), _sds((T, D), F32)], sem=("parallel",))(dxo, pp, gl, w_gate[0])


def mlp_bwd1(dy, pre, g, t, w_down, name):
    T, D = dy.shape
    fs = w_down[2]
    tm = _tile(T, TILE_ROWS_WIDE)

    def body(dy_ref, pre_ref, mu_ref, rs_ref, g_ref, t_ref, w_ref, dw_ref, dwb_ref, dm_ref, dg_ref, db_ref):
        dw, dg, db = _ln_bwd(dy_ref[...], pre_ref[...], g_ref[...], (mu_ref[...], rs_ref[...]))
        first = pl.program_id(0) == 0
        _acc_rows(dg_ref, dg, first)
        _acc_rows(db_ref, db, first)
        dwb = dw.astype(BF16)
        dw_ref[...] = dw
        dwb_ref[...] = dwb
        for j in range(NS):
            sl = slice(j * fs, (j + 1) * fs)
            dr = lax.dot_general(dwb, w_ref[j], NT, preferred_element_type=F32)
            dm_ref[:, sl] = (dr * t_ref[:, sl].astype(F32)).astype(BF16)

    return _pc(body, name, (T // tm,),
               [_rows(tm, D), _rows(tm, D), _rows(tm, 1), _rows(tm, 1), _const((1, D)), _rows(tm, NS * fs), _wspec(w_down)],
               [_rows(tm, D), _rows(tm, D), _rows(tm, NS * fs), _const((1, D)), _const((1, D))],
               [_sds((T, D), F32), _sds((T, D), BF16), _sds((T, NS * fs), BF16), _sds((1, D), F32), _sds((1, D), F32)],
               sem=("arbitrary",))(dy, *pre, g, t, w_down[0])


def mlp_bwd2(dpre, dm, w_up, alpha, pre_mix, g_mix, w_mix, name):
    T, D = dpre.shape
    fs = w_up[0].shape[2]
    ms = w_mix[2]
    tm = _tile(T, TILE_ROWS_WIDE)

    def body(dp_ref, dm_ref, wu_ref, pre_ref, mu_ref, rs_ref, g_ref, wm_ref, dw_ref, dwb_ref, do_ref, dg_ref, db_ref, dc_ref):
        dy = alpha * dp_ref[...]
        for j in range(NS):
            dy = dy + lax.dot_general(dm_ref[:, j * fs:(j + 1) * fs], wu_ref[j], NT, preferred_element_type=F32)
        dw, dg, db = _ln_bwd(dy, pre_ref[...], g_ref[...], (mu_ref[...], rs_ref[...]))
        first = pl.program_id(0) == 0
        _acc_rows(dg_ref, dg, first)
        _acc_rows(db_ref, db, first)
        _acc_rows(dc_ref, jnp.sum(dw, axis=0, keepdims=True), first)
        dwb = dw.astype(BF16)
        dw_ref[...] = dw
        dwb_ref[...] = dwb
        do_ref[...] = lax.dot_general(dwb, _rows_joined(wm_ref), NT, preferred_element_type=F32).astype(BF16)

    return _pc(body, name, (T // tm,),
               [_rows(tm, D), _rows(tm, NS * fs), _wspec(w_up), _rows(tm, D), _rows(tm, 1), _rows(tm, 1), _const((1, D)),
                _wspec(w_mix)],
               [_rows(tm, D), _rows(tm, D), _rows(tm, NS * ms), _const((1, D)), _const((1, D)), _const((1, D))],
               [_sds((T, D), F32), _sds((T, D), BF16), _sds((T, NS * ms), BF16)] + [_sds((1, D), F32)] * 3,
               sem=("arbitrary",))(dpre, dm, w_up[0], *pre_mix, g_mix, w_mix[0])


def attn_bwd(q, k, v, do, sinks):
    T, HD = q.shape
    KVD = k.shape[1]
    NH, NKV = HD // HEAD, KVD // HEAD
    G = NH // NKV
    nb = T // BLK

    def body(s_ref, q_ref, do_ref, kc_ref, kp_ref, vc_ref, vp_ref, dq_ref, dk_ref, dv_ref, ds_ref, ck, cv):
        n = pl.program_id(0)

        @pl.when(n == 0)
        def _():
            ck[...] = jnp.zeros_like(ck)
            cv[...] = jnp.zeros_like(cv)
            ds_ref[...] = jnp.zeros_like(ds_ref)

        @pl.when(n < nb)
        def _():
            valid = _band_mask(n)
            for k0 in range(0, NKV, KV_PER_STAGE):
                khs = range(k0, min(k0 + KV_PER_STAGE, NKV))
                k2 = {kh: jnp.concatenate([kp_ref[:, _head(kh)], kc_ref[:, _head(kh)]], axis=0) for kh in khs}
                v2 = {kh: jnp.concatenate([vp_ref[:, _head(kh)], vc_ref[:, _head(kh)]], axis=0) for kh in khs}
                hs = [kh * G + gq for kh in khs for gq in range(G)]
                qs = {hh: q_ref[:, _head(hh)] for hh in hs}
                dos = {hh: do_ref[:, _head(hh)] for hh in hs}
                sc = {hh: lax.dot_general(qs[hh], k2[hh // G], NT, preferred_element_type=F32) for hh in hs}
                pr = {hh: _softmax_sink(jnp.where(valid, sc[hh], NEG), s_ref[0, hh]) for hh in hs}
                dp = {hh: lax.dot_general(dos[hh], v2[hh // G], NT, preferred_element_type=F32) for hh in hs}
                delta = {hh: jnp.sum(pr[hh][0] * dp[hh], axis=-1, keepdims=True) for hh in hs}
                dsb = {hh: (pr[hh][0] * (dp[hh] - delta[hh])).astype(BF16) for hh in hs}
                pb = {hh: pr[hh][0].astype(BF16) for hh in hs}
                for hh in hs:
                    ds_ref[hh:hh + 1, :] += jnp.broadcast_to(-jnp.sum(pr[hh][1] * delta[hh], axis=0, keepdims=True), (1, 128))
                for hh in hs:
                    dq_ref[:, _head(hh)] = jnp.dot(dsb[hh], k2[hh // G], preferred_element_type=F32)
                for kh in khs:
                    kv = _head(kh)
                    grp = [kh * G + gq for gq in range(G)]
                    dk2 = lax.dot_general(jnp.concatenate([dsb[hh] for hh in grp], axis=0),
                                          jnp.concatenate([qs[hh] for hh in grp], axis=0), TN, preferred_element_type=F32)
                    dv2 = lax.dot_general(jnp.concatenate([pb[hh] for hh in grp], axis=0),
                                          jnp.concatenate([dos[hh] for hh in grp], axis=0), TN, preferred_element_type=F32)
                    dk_ref[:, kv] = ck[:, kv] + dk2[0:BLK]
                    dv_ref[:, kv] = cv[:, kv] + dv2[0:BLK]
                    ck[:, kv] = dk2[BLK:2 * BLK]
                    cv[:, kv] = dv2[BLK:2 * BLK]

        @pl.when(n == nb)
        def _():
            dk_ref[...] = ck[...]
            dv_ref[...] = cv[...]

    qcur = pl.BlockSpec((BLK, HD), lambda n: (jnp.minimum(n, nb - 1), 0))
    kcur = pl.BlockSpec((BLK, KVD), lambda n: (jnp.minimum(n, nb - 1), 0))
    kprev = pl.BlockSpec((BLK, KVD), lambda n: (jnp.maximum(n - 1, 0), 0))
    return _pc(body, "attn_bwd", (nb + 1,),
               [pl.BlockSpec(memory_space=pltpu.SMEM), qcur, qcur, kcur, kprev, kcur, kprev],
               [qcur, kprev, kprev, _const((NH, 128))],
               [_sds((T, HD), F32), _sds((T, KVD), F32), _sds((T, KVD), F32), _sds((NH, 128), F32)],
               scratch=[pltpu.VMEM((BLK, KVD), F32), pltpu.VMEM((BLK, KVD), F32)],
               sem=("arbitrary",))(sinks, q, do, k, k, v, v)


def qkv_bwd(dq, dk, dv, dpre_mix, w_q, w_k, w_v, cs, alpha):
    T, HD = dq.shape
    KVD = dk.shape[1]
    D = dpre_mix.shape[1]
    tm = _tile(T, TILE_ROWS_WIDE)
    scale = 1.0 / (HEAD ** 0.5)

    def body(dq_ref, dk_ref, dv_ref, dp_ref, wq_ref, wk_ref, wv_ref, cs_ref, dqb_ref, dkb_ref, dvb_ref, dx_ref):
        for gq, val in enumerate(_rope(dq_ref[...], cs_ref, -1.0)):
            dqb_ref[:, gq * 128:(gq + 1) * 128] = (val * scale).astype(BF16)
        for gq, val in enumerate(_rope(dk_ref[...], cs_ref, -1.0)):
            dkb_ref[:, gq * 128:(gq + 1) * 128] = val.astype(BF16)
        dvb_ref[...] = dv_ref[...].astype(BF16)
        dqb, dkb, dvb = dqb_ref[...], dkb_ref[...], dvb_ref[...]
        dx_ref[...] = (alpha * dp_ref[...]
                       + lax.dot_general(dqb, _rows_joined(wq_ref), NT, preferred_element_type=F32)
                       + lax.dot_general(dkb, _rows_joined(wk_ref), NT, preferred_element_type=F32)
                       + lax.dot_general(dvb, _rows_joined(wv_ref), NT, preferred_element_type=F32))

    cs_spec = pl.BlockSpec((2, tm, 128), lambda i: (0, i, 0))
    return _pc(body, "qkv_bwd", (T // tm,),
               [_rows(tm, HD), _rows(tm, KVD), _rows(tm, KVD), _rows(tm, D), _wspec(w_q), _wspec(w_k), _wspec(w_v), cs_spec],
               [_rows(tm, HD), _rows(tm, KVD), _rows(tm, KVD), _rows(tm, D)],
               [_sds((T, HD), BF16), _sds((T, KVD), BF16), _sds((T, KVD), BF16), _sds((T, D), F32)],
               sem=("parallel",))(dq, dk, dv, dpre_mix, w_q[0], w_k[0], w_v[0], cs)


def conv_mid_bwd(ds, cv, ln_g, ln_b):
    T, C = cv.shape
    tm = _tile(T, TILE_ROWS_WIDE)

    def body(ds_ref, cv_ref, g_ref, b_ref, dcv_ref, dg_ref, db_ref, dc_ref):
        xhat, _, _ = _ln_stats(cv_ref[...])
        ln = xhat * g_ref[...] + b_ref[...]
        sg = _sigmoid(ln)
        dl = ds_ref[...].astype(F32) * (sg * (1.0 + ln * (1.0 - sg)))
        dcv, dg, db = _ln_bwd(dl, cv_ref[...], g_ref[...])
        first = pl.program_id(0) == 0
        _acc_rows(dg_ref, dg, first)
        _acc_rows(db_ref, db, first)
        _acc_rows(dc_ref, jnp.sum(dcv, axis=0, keepdims=True), first)
        dcv_ref[...] = dcv

    return _pc(body, "conv_mid_bwd", (T // tm,), [_rows(tm, C), _rows(tm, C), _const((1, C)), _const((1, C))],
               [_rows(tm, C), _const((1, C)), _const((1, C)), _const((1, C))],
               [_sds((T, C), F32)] + [_sds((1, C), F32)] * 3, sem=("arbitrary",))(ds, cv, ln_g, ln_b)


def dwconv_bwd(dcv, h, w_dw, taps):
    T, C = dcv.shape
    tq = _tile(T)
    nh = tq // HALO
    nblk = T // tq
    off = HALO - (taps - 1)

    def body(d_ref, dn_ref, a_ref, g_ref, ap_ref, gp_ref, w_ref, dh_ref, dw_ref, dbi_ref, su, sus, sd, sds, wb):
        i = pl.program_id(0)
        su[HALO:HALO + tq, :] = a_ref[...].astype(F32) * _sigmoid(g_ref[...].astype(F32))
        up = ap_ref[...].astype(F32) * _sigmoid(gp_ref[...].astype(F32))
        su[0:HALO, :] = jnp.where(i > 0, up, 0.0)
        sd[0:tq, :] = d_ref[...]
        sd[tq:tq + HALO, :] = jnp.where(i < nblk - 1, dn_ref[...], 0.0)
        _phases(su, sus)
        _phases(sd, sds)

        @pl.when(i == 0)
        def _():
            dw_ref[...] = jnp.zeros_like(dw_ref)

        for j in range(taps):
            dw_ref[j:j + 1, :] += jnp.sum(d_ref[...] * _tap(su, sus, off + j, tq), axis=0, keepdims=True)
        sa = jnp.zeros((1, C), F32)
        sb = jnp.zeros((1, C), F32)
        _spread(w_ref, wb, taps)
        for r in range(tq // CONV_ROWS):
            rows = slice(r * CONV_ROWS, (r + 1) * CONV_ROWS)
            dus = [wb[0] * _tap(sd, sds, taps - 1 + r * CONV_ROWS + 8 * k, 8) for k in range(CONV_ROWS // 8)]
            for j in range(1, taps):
                wj = wb[j]
                dus = [acc + wj * _tap(sd, sds, taps - 1 - j + r * CONV_ROWS + 8 * k, 8) for k, acc in enumerate(dus)]
            du = jnp.concatenate(dus, axis=0)
            a = a_ref[rows, :].astype(F32)
            sg = _sigmoid(g_ref[rows, :].astype(F32))
            da = du * sg
            dgt = du * a * sg * (1.0 - sg)
            dh_ref[rows, 0:C] = da.astype(BF16)
            dh_ref[rows, C:2 * C] = dgt.astype(BF16)
            sa = sa + jnp.sum(da, axis=0, keepdims=True)
            sb = sb + jnp.sum(dgt, axis=0, keepdims=True)
        first = i == 0
        _acc_rows(dbi_ref.at[:, 0:C], sa, first)
        _acc_rows(dbi_ref.at[:, C:2 * C], sb, first)

    prev = lambda col: pl.BlockSpec((HALO, C), lambda i: (jnp.maximum(i * nh - 1, 0), col))
    nxt = pl.BlockSpec((HALO, C), lambda i: (jnp.minimum((i + 1) * nh, T // HALO - 1), 0))
    cur = lambda col: pl.BlockSpec((tq, C), lambda i: (i, col))
    return _pc(body, "dwconv_bwd", (nblk,),
               [cur(0), nxt, cur(0), cur(1), prev(0), prev(1), _const((HALO, C))],
               [_rows(tq, 2 * C), _const((HALO, C)), _const((1, 2 * C))],
               [_sds((T, 2 * C), BF16), _sds((HALO, C), F32), _sds((1, 2 * C), F32)],
               scratch=[pltpu.VMEM((HALO + tq, C), F32), pltpu.VMEM((7, HALO + tq, C), F32),
                        pltpu.VMEM((HALO + tq, C), F32), pltpu.VMEM((7, HALO + tq, C), F32), pltpu.VMEM((taps, 8, C), F32)],
               sem=("arbitrary",))(dcv, dcv, h, h, h, h, w_dw)


def conv_in_bwd(dh, dpre_mix, w_in, alpha):
    T, D = dpre_mix.shape
    nw = w_in[0].shape[2]
    tm = _tile(T, TILE_ROWS_WIDE)

    def body(dh_ref, dp_ref, w_ref, dx_ref):
        acc = alpha * dp_ref[...]
        for j in range(NS):
            acc = acc + lax.dot_general(dh_ref[:, j * nw:(j + 1) * nw], w_ref[j], NT, preferred_element_type=F32)
        dx_ref[...] = acc

    return _pc(body, "conv_in_bwd", (T // tm,), [_rows(tm, NS * nw), _rows(tm, D), _wspec(w_in)], _rows(tm, D),
               _sds((T, D), F32), sem=("parallel",))(dh, dpre_mix, w_in[0])


def wgrad(a, b, row_sharded, name, into):
    prev, out_shape, off = into
    layer = None
    if isinstance(a, tuple):
        layer, a = a
    T, Ka = a.shape[-2:]
    Nb = b.shape[1]
    ka, tn = min(Ka, BLOCK_DIM), min(Nb, BLOCK_DIM)
    tt = T if (Ka // ka) * (Nb // tn) >= 4 else T // 2
    nt = T // tt
    if row_sharded:
        sr = Ka // NS
        spb = max(ka // sr, 1)
        rb = ka // spb
        assert out_shape[2] == Nb and off % rb == 0
        out_spec = pl.BlockSpec((spb, rb, tn), lambda i, j, t: (i, off // rb, j))
    else:
        sc = Nb // NS
        spb = max(tn // sc, 1)
        rb = ka
        assert out_shape[2] == sc and off % ka == 0
        out_spec = pl.BlockSpec((spb, ka, tn // spb), lambda i, j, t: (j, off // ka + i, 0))

    def body(a_ref, b_ref, *rest):
        o_ref, acc = rest[-2:]
        t = pl.program_id(2)
        av = a_ref[...]
        if av.dtype != BF16:
            av = av.astype(BF16)
        d = lax.dot_general(av, b_ref[...], TN, preferred_element_type=F32)

        @pl.when(t == 0)
        def _():
            acc[...] = d

        @pl.when(t > 0)
        def _():
            acc[...] += d

        @pl.when(t == nt - 1)
        def _():
            for s in range(spb):
                if row_sharded:
                    o_ref[s] = acc[s * rb:(s + 1) * rb, :].astype(BF16)
                else:
                    o_ref[s] = acc[:, s * (tn // spb):(s + 1) * (tn // spb)].astype(BF16)

    a_spec = (pl.BlockSpec((tt, ka), lambda i, j, t: (t, i)) if layer is None
              else pl.BlockSpec((None, tt, ka), lambda i, j, t: (layer, t, i)))
    ins = [a_spec, pl.BlockSpec((tt, tn), lambda i, j, t: (t, j))]
    args = [a, b]
    kw = {}
    if prev is not None:
        ins.append(ANY)
        args.append(prev)
        kw["input_output_aliases"] = {2: 0}
    return _pc(body, name, (Ka // ka, Nb // tn, nt), ins, out_spec, _sds(out_shape, BF16),
               scratch=[pltpu.VMEM((ka, tn), F32)], sem=("parallel", "parallel", "arbitrary"), **kw)(*args)


def _adamw_math(w, g, m, v):
    c1 = 1.0 - ADAM_B1 ** ADAM_STEP
    c2 = 1.0 - ADAM_B2 ** ADAM_STEP
    mn = ADAM_B1 * m + (1.0 - ADAM_B1) * g
    vn = ADAM_B2 * v + (1.0 - ADAM_B2) * (g * g)
    return -ADAM_LR * ((mn / c1) / (jnp.sqrt(vn / c2) + ADAM_EPS) + ADAM_WD * w), mn, vn


def adamw_layer(w, m, v, layer, gbuf, off, prev, name):
    L, R, W = w.shape
    tr = TILE_ROWS
    assert R % tr == 0 and off % tr == 0

    def body(w_ref, g_ref, m_ref, v_ref, *rest):
        go_ref, d_ref, mo_ref, vo_ref = rest[-4:]
        g = g_ref[...]
        go_ref[...] = g
        d_ref[...], mo_ref[...], vo_ref[...] = _adamw_math(w_ref[...], g, m_ref[...], v_ref[...])

    lay = pl.BlockSpec((None, tr, W), lambda r: (layer, r, 0))
    ins = [lay, pl.BlockSpec((tr, W), lambda r: (off // tr + r, 0)), lay, lay]
    args = [w, gbuf, m, v]
    kw = {}
    if prev is not None:
        ins += [ANY] * 4
        args += list(prev)
        kw["input_output_aliases"] = {4 + k: k for k in range(4)}
    return _pc(body, name, (R // tr,), ins, [lay] * 4, [_sds((L, R, W), F32)] * 4, sem=("parallel",), **kw)(*args)


def adamw_many(ws, gs, ms, vs):
    n = len(ws)

    def body(*refs):
        for k in range(n):
            d, mn, vn = _adamw_math(refs[k][...], refs[n + k][...], refs[2 * n + k][...], refs[3 * n + k][...])
            refs[4 * n + k][...] = d
            refs[5 * n + k][...] = mn
            refs[6 * n + k][...] = vn

    outs = pl.pallas_call(body, name="adamw_small", out_shape=[_sds(a.shape, F32) for a in ws] * 3)(*ws, *gs, *ms, *vs)
    return outs[:n], outs[n:2 * n], outs[2 * n:]


def _rope_tables(T):
    pos = jnp.arange(T, dtype=F32)
    inv_freq = ROPE_THETA ** (-jnp.arange(0, ROPE, 2, dtype=F32) / ROPE)
    ang = pos[:, None] * inv_freq[None, :]
    cos, sin = jnp.cos(ang), jnp.sin(ang)
    pad = HEAD - ROPE
    c = jnp.concatenate([cos, cos, jnp.ones((T, pad), F32)], axis=1)
    s = jnp.concatenate([-sin, sin, jnp.zeros((T, pad), F32)], axis=1)
    return jnp.stack([jnp.tile(c, (1, 128 // HEAD)), jnp.tile(s, (1, 128 // HEAD))])


def _local_step(x, p, target, W, small, lay, hook=None):
    if hook is None:
        hook = lambda stage, after, G, sg=None: None
    T, D = x.shape
    depth = small["mix_ln_g"].shape[0]
    alpha = float((2 * depth) ** 0.25)
    taps = small["taps"]
    row = lambda a, i: a[i:i + 1]
    cs = _rope_tables(T)

    h = conv_in_fwd(x, W["conv_w_in"], small["conv_b_in"])
    cv, s = dwconv_fwd(h, small["conv_w_dw"], small["conv_b_dw"], small["conv_ln_g"], small["conv_ln_b"], taps)
    hook("weights1", s, None)
    pre_mix0, x1, x1b = mm_res_ln(s, W["conv_w_out"], x, row(small["mix_ln_g"], 0), row(small["mix_ln_b"], 0), alpha,
                                  small["conv_b_out"], "conv_out_fwd")
    hook("weights1b", x1b, None)
    r0, t0 = mlp_up_fwd(x1b, W["mlp_w_up0"], "mlp_up_fwd0")
    pre_mlp0, x2, x2b = mm_res_ln(r0, W["mlp_w_down0"], x1, row(small["mlp_ln_g"], 0), row(small["mlp_ln_b"], 0), alpha,
                                  None, "mlp_down_fwd0")
    x3, x3b, pp0, gl0 = ple_fwd(x2, x2b, p, 0, W["ple_w_proj0"], W["ple_w_gate0"], None, "ple_fwd0")

    hook("weights2", x3b, None)
    q, k, v = qkv_fwd(x3b, W["attn_w_q"], W["kv_w_k"], W["kv_w_v"], cs)
    o = attn_fwd(q, k, v, small["attn_sinks"])
    pre_mix1, x4, x4b = mm_res_ln(o, W["attn_w_o"], x3, row(small["mix_ln_g"], 1), row(small["mix_ln_b"], 1), alpha,
                                  None, "attn_out_fwd")
    r1, t1 = mlp_up_fwd(x4b, W["mlp_w_up1"], "mlp_up_fwd1")
    pre_mlp1, x5, x5b = mm_res_ln(r1, W["mlp_w_down1"], x4, row(small["mlp_ln_g"], 1), row(small["mlp_ln_b"], 1), alpha,
                                  None, "mlp_down_fwd1")
    dx6, loss, pp1, gl1 = ple_fwd(x5, x5b, p, 1, W["ple_w_proj1"], W["ple_w_gate1"], target, "ple_fwd1")

    G, sg = {}, {}
    where = {n: (key, off) for key in lay for n, off, _ in lay[key]}
    rows_of = {key: sum(r for _, _, r in lay[key]) for key in lay}

    def wg(name, a, b, row_sharded):
        key, off = where[name]
        shape = (NS, rows_of[key], W[name][0].shape[2])
        G[key] = wgrad(a, b, row_sharded, "wg_" + name, (G.get(key), shape, off))

    dpp1, dgl1, dx5 = ple_bwd(dx6, pp1, gl1, W["ple_w_gate1"], "ple_bwd1")
    wg("ple_w_proj1", (1, p), dpp1, False)
    wg("ple_w_gate1", x5b, dgl1, True)
    dpre_mlp1, dpre_mlp1b, dm1, g_mlp_g1, g_mlp_b1 = mlp_bwd1(dx5, pre_mlp1, row(small["mlp_ln_g"], 1), t1,
                                                              W["mlp_w_down1"], "mlp_bwd1_1")
    wg("mlp_w_down1", r1, dpre_mlp1b, True)
    wg("mlp_w_up1", x4b, dm1, False)
    dpre_mix1, dpre_mix1b, do, g_mix_g1, g_mix_b1, _ = mlp_bwd2(dpre_mlp1, dm1, W["mlp_w_up1"], alpha, pre_mix1,
                                                                row(small["mix_ln_g"], 1), W["attn_w_o"], "mlp_bwd2_1")
    wg("attn_w_o", o, dpre_mix1b, True)
    dq, dk, dv, dsinks = attn_bwd(q, k, v, do, small["attn_sinks"])
    dqb, dkb, dvb, dx3 = qkv_bwd(dq, dk, dv, dpre_mix1,
                                 W["attn_w_q"], W["kv_w_k"], W["kv_w_v"], cs, alpha)
    wg("attn_w_q", x3b, dqb, True)
    wg("kv_w_k", x3b, dkb, True)
    wg("kv_w_v", x3b, dvb, True)
    hook("grads3", None, G)

    dpp0, dgl0, dx2 = ple_bwd(dx3, pp0, gl0, W["ple_w_gate0"], "ple_bwd0")
    wg("ple_w_proj0", (0, p), dpp0, False)
    wg("ple_w_gate0", x2b, dgl0, True)
    dpre_mlp0, dpre_mlp0b, dm0, g_mlp_g0, g_mlp_b0 = mlp_bwd1(dx2, pre_mlp0, row(small["mlp_ln_g"], 0), t0,
                                                              W["mlp_w_down0"], "mlp_bwd1_0")
    wg("mlp_w_down0", r0, dpre_mlp0b, True)
    wg("mlp_w_up0", x1b, dm0, False)
    hook("grads2", None, G)
    dpre_mix0, dpre_mix0b, dsw, g_mix_g0, g_mix_b0, g_b_out = mlp_bwd2(dpre_mlp0, dm0, W["mlp_w_up0"], alpha, pre_mix0,
                                                                      row(small["mix_ln_g"], 0), W["conv_w_out"],
                                                                      "mlp_bwd2_0")
    wg("conv_w_out", s, dpre_mix0b, True)
    hook("grads1", None, G)
    dcv, g_cln_g, g_cln_b, g_b_dw = conv_mid_bwd(dsw, cv, small["conv_ln_g"], small["conv_ln_b"])
    dh, g_w_dw, g_b_in = dwconv_bwd(dcv, h, small["conv_w_dw"], taps)
    wg("conv_w_in", x, dh, False)

    sg["conv_b_in"] = g_b_in
    sg["conv_w_dw"] = g_w_dw
    sg["conv_b_dw"], sg["conv_ln_g"], sg["conv_ln_b"], sg["conv_b_out"] = g_b_dw, g_cln_g, g_cln_b, g_b_out
    sg["mix_ln_g"] = [g_mix_g0, g_mix_g1]
    sg["mix_ln_b"] = [g_mix_b0, g_mix_b1]
    sg["mlp_ln_g"] = [g_mlp_g0, g_mlp_g1]
    sg["mlp_ln_b"] = [g_mlp_b0, g_mlp_b1]
    sg["attn_sinks"] = dsinks[:, 0][None, :]
    sg["loss"] = loss
    hook("grads0", None, G, sg)
    grad_x = conv_in_bwd(dh, dpre_mix0, W["conv_w_in"], alpha)
    return loss, grad_x, G, sg


BUFFERS = (("b0", ("conv_w_in",)), ("a0", ("conv_w_out",)),
           ("a1", ("mlp_w_up0", "mlp_w_down0", "ple_w_gate0")), ("c1", ("ple_w_proj0",)),
           ("a2", ("mlp_w_up1", "mlp_w_down1", "ple_w_gate1", "attn_w_q", "attn_w_o")),
           ("c2", ("kv_w_k", "kv_w_v", "ple_w_proj1")))
GROUPS = (("b0",), ("a0", "a1", "c1"), ("a2", "c2"))
REDUCED = (("b0",), ("a0",), ("a1", "c1"), ("a2", "c2"))
ROW_SHARDED = {"mlp_w_down0", "mlp_w_down1", "ple_w_gate0", "ple_w_gate1", "conv_w_out", "attn_w_q", "attn_w_o", "kv_w_k",
               "kv_w_v"}


def _split_layers(weights):
    out = {"conv_w_in": weights["conv_w_in"][0], "conv_w_out": weights["conv_w_out"][0],
           "attn_w_q": weights["attn_w_q"][0], "attn_w_o": weights["attn_w_o"][0],
           "kv_w_k": weights["kv_w_k"], "kv_w_v": weights["kv_w_v"]}
    for n in ("mlp_w_up", "mlp_w_down", "ple_w_proj", "ple_w_gate"):
        for i in range(weights[n].shape[0]):
            out[n + str(i)] = weights[n][i]
    return out


def _layout(shards):
    lay = {}
    for key, names in BUFFERS:
        off, rows = 0, []
        for n in names:
            rows.append((n, off, shards[n].shape[0]))
            off += shards[n].shape[0]
        lay[key] = rows
    return lay


def _place():
    return lax.axis_index("x"), lax.axis_index("y"), lax.axis_index("c")


def _flip(v, f):
    return (v + f) % 2 if f else v


CHIP_FLIPS = ((1, 0), (0, 1), (1, 1))


HBM = pl.BlockSpec(memory_space=pltpu.HBM)
SEM = pl.BlockSpec(memory_space=pltpu.SEMAPHORE)
EFFECT = pltpu.SideEffectType.DATAFLOW_SIDE_EFFECTING


def _half(ref, rows, c):
    return ref.at[pl.ds(pl.multiple_of(c * (rows // 2), 16), rows // 2), :]


def _gather_copies(refs, shapes, whole, send, recv):
    x, y, c = _place()
    me = 2 * x + y
    na = len(refs)
    cps = []
    for d, (fx, fy) in enumerate(CHIP_FLIPS):
        to = (_flip(x, fx), _flip(y, fy), c)
        for k in range(na):
            mine = refs[k].at[me] if k >= na - whole else _half(refs[k].at[me], shapes[k][1], c)
            cps.append(pltpu.make_async_remote_copy(mine, mine, send.at[d * na + k], recv.at[d * na + k], device_id=to,
                                                    device_id_type=MESH))
    return cps


def _forward_copies(refs, shapes, send, recv):
    x, y, c = _place()
    na = len(refs)
    cps = []
    for d, (fx, fy) in enumerate(CHIP_FLIPS):
        frm = 2 * _flip(x, fx) + _flip(y, fy)
        for k in range(na):
            theirs = _half(refs[k].at[frm], shapes[k][1], c)
            cps.append(pltpu.make_async_remote_copy(theirs, theirs, send.at[d * na + k], recv.at[d * na + k],
                                                    device_id=(x, y, 1 - c), device_id_type=MESH))
    return cps


def _split_copies(refs, shapes, whole, send, recv):
    if whole is None:
        return _forward_copies(refs, shapes, send, recv)
    return _gather_copies(refs, shapes, whole, send, recv)


def gather_start(bufs, whole, after, name):
    na = len(bufs)
    shapes = [b.shape for b in bufs]
    nsem = len(CHIP_FLIPS) * na

    def body(*refs):
        ins = refs[:na]
        send, recv = refs[-(na + 3)], refs[-(na + 2)]
        token = refs[-1]
        for cp in _split_copies(ins, shapes, whole, send, recv):
            cp.start()
        token[...] = jnp.zeros_like(token)

    args = [pltpu.with_memory_space_constraint(b, pltpu.HBM) for b in bufs]
    ins = [HBM] * na
    if after is not None:
        args.append(after)
        ins.append(ANY)
    return pl.pallas_call(
        body, name=name, in_specs=ins,
        out_specs=[SEM, SEM] + [HBM] * na + [pl.BlockSpec(memory_space=pltpu.VMEM)],
        out_shape=[pltpu.SemaphoreType.DMA((nsem,)), pltpu.SemaphoreType.DMA((nsem,))]
        + [pltpu.HBM(b.shape, b.dtype) for b in bufs] + [_sds((8, 128), F32)],
        input_output_aliases={k: k + 2 for k in range(na)},
        compiler_params=pltpu.CompilerParams(has_side_effects=EFFECT))(*args)


def gather_wait(send, recv, bufs, whole, after, name):
    na = len(bufs)
    shapes = [b.shape for b in bufs]

    def body(*refs):
        ins = refs[:na]
        send_ref, recv_ref = refs[na], refs[na + 1]
        for cp in _split_copies(ins, shapes, whole, send_ref, recv_ref):
            cp.wait_send()
            cp.wait_recv()

    return pl.pallas_call(
        body, name=name, in_specs=[HBM] * na + [SEM, SEM, ANY], out_specs=[HBM] * na,
        out_shape=[pltpu.HBM(b.shape, b.dtype) for b in bufs], input_output_aliases={k: k for k in range(na)},
        compiler_params=pltpu.CompilerParams(has_side_effects=EFFECT))(*bufs, send, recv, after)


def sibling_forward(bufs, name):
    nb = len(bufs)

    def body(*refs):
        outs = refs[nb:2 * nb]
        send, recv = refs[2 * nb:]
        x, y, c = _place()
        cps = []
        for d, (fx, fy) in enumerate(CHIP_FLIPS):
            frm = 2 * _flip(x, fx) + _flip(y, fy)
            for k in range(nb):
                theirs = _half(outs[k].at[frm], bufs[k].shape[1], c)
                cps.append(pltpu.make_async_remote_copy(theirs, theirs, send.at[d * nb + k], recv.at[d * nb + k],
                                                        device_id=(x, y, 1 - c), device_id_type=MESH))
        for cp in cps:
            cp.start()
        for cp in cps:
            cp.wait()

    nsem = len(CHIP_FLIPS) * nb
    return pl.pallas_call(
        body, name=name, in_specs=[ANY] * nb, out_specs=[ANY] * nb, out_shape=[_sds(b.shape, b.dtype) for b in bufs],
        input_output_aliases={k: k for k in range(nb)},
        scratch_shapes=[pltpu.SemaphoreType.DMA((nsem,)), pltpu.SemaphoreType.DMA((nsem,))])(*bufs)


def pack_rows(pieces, rows, width, name):
    def body(*refs):
        o_ref = refs[-1]
        o_ref[...] = jnp.zeros_like(o_ref)
        for ref, (a, off) in zip(refs[:-1], pieces):
            o_ref[off:off + a.shape[0], 0:a.shape[1]] = ref[...]

    return pl.pallas_call(body, name=name, out_shape=_sds((rows, width), F32))(*[a for a, _ in pieces])


PEER_FLIPS = tuple((fx, fy, fc) for fx in (0, 1) for fy in (0, 1) for fc in (0, 1) if fx or fy or fc)


def _reduce_copies(parts, zones, pack, send, recv):
    x, y, c = _place()
    nb = len(parts)
    na = nb + (1 if pack is not None else 0)
    cps = []
    for f, (fx, fy, fc) in enumerate(PEER_FLIPS):
        tx, ty, tc = _flip(x, fx), _flip(y, fy), _flip(c, fc)
        for k in range(nb):
            hrows = parts[k].shape[1] // 2
            piece = parts[k].at[2 * tx + ty, pl.ds(pl.multiple_of(tc * hrows, 16), hrows), :]
            cps.append(pltpu.make_async_remote_copy(piece, zones[k].at[f], send.at[f * na + k], recv.at[f * na + k],
                                                    device_id=(tx, ty, tc), device_id_type=MESH))
        if pack is not None:
            mine = pack.at[4 * x + 2 * y + c]
            cps.append(pltpu.make_async_remote_copy(mine, mine, send.at[f * na + nb], recv.at[f * na + nb],
                                                    device_id=(tx, ty, tc), device_id_type=MESH))
    return cps


def reduce_begin(parts, pack, name):
    nb = len(parts)
    zones = [lax.empty((len(PEER_FLIPS), g.shape[1] // 2, g.shape[2]), g.dtype) for g in parts]
    arrs = list(parts) + zones + ([pack] if pack is not None else [])
    na = len(arrs)
    nsem = len(PEER_FLIPS) * (nb + (1 if pack is not None else 0))

    def body(*refs):
        ins = refs[:na]
        send, recv = refs[na], refs[na + 1]
        for cp in _reduce_copies(ins[:nb], ins[nb:2 * nb], ins[2 * nb] if pack is not None else None, send, recv):
            cp.start()
        refs[-1][...] = jnp.zeros_like(refs[-1])

    return pl.pallas_call(
        body, name=name, in_specs=[HBM] * na,
        out_specs=[SEM, SEM] + [HBM] * na + [pl.BlockSpec(memory_space=pltpu.VMEM)],
        out_shape=[pltpu.SemaphoreType.DMA((nsem,)), pltpu.SemaphoreType.DMA((nsem,))]
        + [pltpu.HBM(a.shape, a.dtype) for a in arrs] + [_sds((8, 128), F32)],
        input_output_aliases={k: k + 2 for k in range(na)},
        compiler_params=pltpu.CompilerParams(has_side_effects=EFFECT))(
            *[pltpu.with_memory_space_constraint(a, pltpu.HBM) for a in arrs])


def reduce_end(send, recv, parts, zones, pack, after, name):
    nb = len(parts)
    arrs = list(parts) + list(zones) + ([pack] if pack is not None else [])
    na = len(arrs)

    def body(*refs):
        ins = refs[:na]
        for cp in _reduce_copies(ins[:nb], ins[nb:2 * nb], ins[2 * nb] if pack is not None else None, refs[na], refs[na + 1]):
            cp.wait_send()
            cp.wait_recv()

    return pl.pallas_call(
        body, name=name, in_specs=[HBM] * na + [SEM, SEM, ANY], out_specs=[HBM] * na,
        out_shape=[pltpu.HBM(a.shape, a.dtype) for a in arrs], input_output_aliases={k: k for k in range(na)},
        compiler_params=pltpu.CompilerParams(has_side_effects=EFFECT))(*arrs, send, recv, after)


def sibling_share(halves, name):
    nb = len(halves)

    def body(*refs):
        outs = refs[nb:2 * nb]
        send, recv = refs[2 * nb:]
        x, y, c = _place()
        cps = []
        for k in range(nb):
            hrows = halves[k].shape[0] // 2
            mine = outs[k].at[pl.ds(pl.multiple_of(c * hrows, 8), hrows), :]
            cps.append(pltpu.make_async_remote_copy(mine, mine, send.at[k], recv.at[k], device_id=(x, y, 1 - c),
                                                    device_id_type=MESH))
        for cp in cps:
            cp.start()
        for cp in cps:
            cp.wait()

    return pl.pallas_call(
        body, name=name, in_specs=[ANY] * nb, out_specs=[ANY] * nb,
        out_shape=[_sds(h.shape, h.dtype) for h in halves], input_output_aliases={k: k for k in range(nb)},
        scratch_shapes=[pltpu.SemaphoreType.DMA((nb,)), pltpu.SemaphoreType.DMA((nb,))])(*halves)


def _row_tile(rows):
    for cand in (512, 384, 256, 128, 64, 32, 16):
        if rows % cand == 0:
            return cand
    return rows


def piece_sum(g, z, idx, name):
    _, hrows, W = z.shape
    tr = _row_tile(hrows)
    nrb = hrows // tr

    def body(idx_ref, g_ref, z_ref, o_ref):
        acc = g_ref[...].astype(F32)
        for d in range(z.shape[0]):
            acc = acc + z_ref[d].astype(F32)
        o_ref[...] = acc

    gs = pltpu.PrefetchScalarGridSpec(
        num_scalar_prefetch=1, grid=(nrb,),
        in_specs=[pl.BlockSpec((None, tr, W), lambda i, sc: (sc[0], sc[1] * nrb + i, 0)),
                  pl.BlockSpec((z.shape[0], tr, W), lambda i, sc: (0, i, 0))],
        out_specs=pl.BlockSpec((tr, W), lambda i, sc: (sc[1] * nrb + i, 0)))
    return pl.pallas_call(body, name=name, grid_spec=gs, out_shape=_sds((2 * hrows, W), F32),
                          compiler_params=pltpu.CompilerParams(dimension_semantics=("parallel",),
                                                               vmem_limit_bytes=VMEM_LIMIT_MIB * 2 ** 20))(idx, g, z)


def small_sum(packs):
    n, R, W = packs.shape

    def body(p_ref, o_ref):
        acc = p_ref[0]
        for d in range(1, n):
            acc = acc + p_ref[d]
        o_ref[...] = acc

    return pl.pallas_call(body, name="small_sum", out_shape=_sds((R, W), F32))(packs)


WEIGHTS = ["conv_w_in", "conv_b_in", "conv_w_dw", "conv_b_dw", "conv_ln_g", "conv_ln_b", "conv_w_out", "conv_b_out", "kv_w_k",
           "kv_w_v", "attn_w_q", "attn_sinks", "attn_w_o", "mix_ln_g", "mix_ln_b", "mlp_w_up", "mlp_w_down", "mlp_ln_g",
           "mlp_ln_b", "ple_w_proj", "ple_w_gate"]
BIG = ["conv_w_in", "conv_w_out", "kv_w_k", "kv_w_v", "attn_w_q", "attn_w_o", "mlp_w_up", "mlp_w_down", "ple_w_proj",
       "ple_w_gate"]
SMALL = [n for n in WEIGHTS if n not in BIG]


def _step(x, p, target, w, m, v):
    D = x.shape[-1]
    ds = D // NS
    xq, yq, cq = _place()
    chip = 2 * xq + yq
    idx = jnp.stack([chip, cq]).astype(jnp.int32)

    shards = _split_layers(w)
    lay = _layout(shards)
    taps = w["conv_w_dw"].shape[1]
    small_loc = pack_rows([(w["conv_w_dw"][0], 0), (w["conv_b_dw"], HALO), (w["conv_ln_g"], HALO + 1), (w["conv_ln_b"], HALO + 2),
                           (w["conv_b_out"], HALO + 3), (w["conv_b_in"].reshape(2, ds), HALO + 4)], HALO + 8, ds, "pack_small")
    slot = lambda a: lax.dynamic_update_slice(lax.empty((NS,) + a.shape, a.dtype), a[None], (chip, 0, 0))
    started, token = [], None
    for gi, keys in enumerate(GROUPS):
        bufs = [slot(jnp.concatenate([shards[n].astype(BF16) for n, _, _ in lay[key]], axis=0)) for key in keys]
        if gi == 0:
            bufs.append(slot(small_loc))
        send, recv, *thru, token = gather_start(bufs, 1 if gi == 0 else 0, token, "gather_start%d" % gi)
        started.append((send, recv, thru))
    W = {}

    def arrive(gi, after):
        send, recv, thru = started[gi]
        whole = 1 if gi == 0 else 0
        got = gather_wait(send, recv, thru, whole, after, "gather_wait%d" % gi)
        nk = len(GROUPS[gi])
        now = 1 if gi == 1 else nk
        have(GROUPS[gi][:now], sibling_forward(got[:now], "sibling_forward%d" % gi))
        if now < nk:
            fsend, frecv, *fthru, ftoken = gather_start(got[now:nk], None, None, "forward_start%d" % gi)
            handing[gi] = (fsend, frecv, fthru, GROUPS[gi][now:])
            _FOLLOW.append(ftoken)
        return got[nk:]

    def have(keys, bufs):
        for key, buf in zip(keys, bufs):
            for n, off, rows in lay[key]:
                W[n] = (buf, off, rows)

    def handed(gi, after):
        fsend, frecv, fthru, keys = handing[gi]
        have(keys, gather_wait(fsend, frecv, fthru, None, after, "forward_wait%d" % gi))

    handing = {}

    gs, = arrive(0, token)
    across = lambda rows: gs[:, rows, :].transpose(1, 0, 2).reshape(rows.stop - rows.start, D)
    small = {"taps": taps, "conv_w_dw": across(slice(0, HALO)), "conv_b_dw": across(slice(HALO, HALO + 1)),
             "conv_ln_g": across(slice(HALO + 1, HALO + 2)), "conv_ln_b": across(slice(HALO + 2, HALO + 3)),
             "conv_b_out": across(slice(HALO + 3, HALO + 4)), "conv_b_in": gs[:, HALO + 4:HALO + 6, :].reshape(1, 2 * D),
             "attn_sinks": w["attn_sinks"], "mix_ln_g": w["mix_ln_g"], "mix_ln_b": w["mix_ln_b"],
             "mlp_ln_g": w["mlp_ln_g"], "mlp_ln_b": w["mlp_ln_b"]}

    reducing = {}

    def reduce_start(gi, G, pack):
        nk = len(REDUCED[gi])
        send, recv, *thru, token = reduce_begin([G[key] for key in REDUCED[gi]], pack, "reduce_begin%d" % gi)
        reducing[gi] = (send, recv, thru[:nk], thru[nk:2 * nk], thru[2 * nk] if pack is not None else None)
        _FOLLOW.append(token)

    def small_pack(sg):
        pieces = [(sg["conv_b_in"].reshape(2, D), 0), (sg["conv_w_dw"], 2)]
        r0 = 2 + HALO
        for i, n in enumerate(("conv_b_dw", "conv_ln_g", "conv_ln_b", "conv_b_out")):
            pieces.append((sg[n], r0 + i))
        r0 += 4
        for i, n in enumerate(("mix_ln_g", "mix_ln_b", "mlp_ln_g", "mlp_ln_b")):
            pieces += [(sg[n][0], r0 + 2 * i), (sg[n][1], r0 + 2 * i + 1)]
        pieces += [(sg["attn_sinks"], r0 + 8), (sg["loss"][0:1], r0 + 9)]
        mine = pack_rows(pieces, r0 + 10, D, "pack_small_grads")
        return lax.dynamic_update_slice(lax.empty((8,) + mine.shape, F32), mine[None], (4 * xq + 2 * yq + cq, 0, 0))

    def hook(stage, after, G, sg=None):
        if stage == "weights1":
            arrive(1, after)
        elif stage == "weights1b":
            handed(1, after)
        elif stage == "weights2":
            arrive(2, after)
        elif stage == "grads0":
            reduce_start(0, G, small_pack(sg))
        elif stage.startswith("grads"):
            reduce_start(int(stage[5:]), G, None)

    _, grad_x, G, sg = _local_step(x[0], p[:, 0], target[0], W, small, lay, hook)
    _FOLLOW.clear()
    nsink = w["attn_sinks"].shape[1]

    grads, delta, new_m, new_v = {}, {}, {}, {}
    found = {}

    def finish(groups, after, tag):
        keys, halves, tot = [], [], None
        for gi in groups:
            send, recv, parts, zones, pack = reducing[gi]
            done = reduce_end(send, recv, parts, zones, pack, after, "reduce_end%d" % gi)
            nk = len(REDUCED[gi])
            for key, g_, z_ in zip(REDUCED[gi], done[:nk], done[nk:2 * nk]):
                keys.append(key)
                halves.append(piece_sum(g_, z_, idx, "piece_sum_" + key))
            if pack is not None:
                tot = small_sum(done[2 * nk])
        for key, buf in zip(keys, sibling_share(halves, "sibling_share" + tag)):
            for n, off, _ in lay[key]:
                found[n] = (buf, off)
        return tot

    def big_adamw(names):
        for n in names:
            three = lambda a: a.reshape((-1,) + a.shape[-2:])
            w3, m3, v3 = three(w[n]), three(m[n]), three(v[n])
            outs = None
            for i in range(w3.shape[0]):
                buf, off = found[n + str(i)] if n + str(i) in found else found[n]
                outs = adamw_layer(w3, m3, v3, i, buf, off, outs, "adamw_%s%d" % (n, i))
            grads[n], delta[n], new_m[n], new_v[n] = [a.reshape(w[n].shape) for a in outs]

    last = [n for n, _, _ in lay[REDUCED[0][0]]]
    finish(reversed(range(1, len(REDUCED))), grad_x, "1")
    big_adamw([n for n in BIG if n not in last])
    tot = finish([0], new_v["mlp_w_down"], "0")
    big_adamw(last)
    cols = lambda rows: lax.dynamic_slice(rows, (0, chip * ds), (rows.shape[0], ds))
    grads["conv_b_in"] = lax.dynamic_slice(tot[0:2].reshape(1, 2 * D), (0, chip * 2 * ds), (1, 2 * ds))
    grads["conv_w_dw"] = cols(tot[2:2 + taps])[None]
    r0 = 2 + HALO
    for i, n in enumerate(("conv_b_dw", "conv_ln_g", "conv_ln_b", "conv_b_out")):
        grads[n] = cols(tot[r0 + i:r0 + i + 1])
    r0 += 4
    for i, n in enumerate(("mix_ln_g", "mix_ln_b", "mlp_ln_g", "mlp_ln_b")):
        grads[n] = tot[r0 + 2 * i:r0 + 2 * i + 2]
    grads["attn_sinks"] = tot[r0 + 8:r0 + 9, 0:nsink]

    ds_, ms_, vs_ = adamw_many([w[n] for n in SMALL], [grads[n] for n in SMALL], [m[n] for n in SMALL], [v[n] for n in SMALL])
    for n, d_, m_, v_ in zip(SMALL, ds_, ms_, vs_):
        delta[n], new_m[n], new_v[n] = d_, m_, v_

    total = tot[r0 + 9, 0]
    return (total, grad_x[None], *[grads[n] for n in WEIGHTS], *[delta[n] for n in WEIGHTS], *[new_m[n] for n in WEIGHTS],
            *[new_v[n] for n in WEIGHTS])


def kernel(x, p, conv_w_in, conv_b_in, conv_w_dw, conv_b_dw, conv_ln_g, conv_ln_b, conv_w_out, conv_b_out, kv_w_k, kv_w_v, attn_w_q, attn_sinks, attn_w_o, mix_ln_g, mix_ln_b, mlp_w_up, mlp_w_down, mlp_ln_g, mlp_ln_b, ple_w_proj, ple_w_gate, loss_target, m_conv_w_in, m_conv_b_in, m_conv_w_dw, m_conv_b_dw, m_conv_ln_g, m_conv_ln_b, m_conv_w_out, m_conv_b_out, m_kv_w_k, m_kv_w_v, m_attn_w_q, m_attn_sinks, m_attn_w_o, m_mix_ln_g, m_mix_ln_b, m_mlp_w_up, m_mlp_w_down, m_mlp_ln_g, m_mlp_ln_b, m_ple_w_proj, m_ple_w_gate, v_conv_w_in, v_conv_b_in, v_conv_w_dw, v_conv_b_dw, v_conv_ln_g, v_conv_ln_b, v_conv_w_out, v_conv_b_out, v_kv_w_k, v_kv_w_v, v_attn_w_q, v_attn_sinks, v_attn_w_o, v_mix_ln_g, v_mix_ln_b, v_mlp_w_up, v_mlp_w_down, v_mlp_ln_g, v_mlp_ln_b, v_ple_w_proj, v_ple_w_gate):
    w = dict(zip(WEIGHTS, (conv_w_in, conv_b_in, conv_w_dw, conv_b_dw, conv_ln_g, conv_ln_b, conv_w_out, conv_b_out, kv_w_k,
                           kv_w_v, attn_w_q, attn_sinks, attn_w_o, mix_ln_g, mix_ln_b, mlp_w_up, mlp_w_down, mlp_ln_g, mlp_ln_b,
                           ple_w_proj, ple_w_gate)))
    m = dict(zip(WEIGHTS, (m_conv_w_in, m_conv_b_in, m_conv_w_dw, m_conv_b_dw, m_conv_ln_g, m_conv_ln_b, m_conv_w_out,
                           m_conv_b_out, m_kv_w_k, m_kv_w_v, m_attn_w_q, m_attn_sinks, m_attn_w_o, m_mix_ln_g, m_mix_ln_b,
                           m_mlp_w_up, m_mlp_w_down, m_mlp_ln_g, m_mlp_ln_b, m_ple_w_proj, m_ple_w_gate)))
    v = dict(zip(WEIGHTS, (v_conv_w_in, v_conv_b_in, v_conv_w_dw, v_conv_b_dw, v_conv_ln_g, v_conv_ln_b, v_conv_w_out,
                           v_conv_b_out, v_kv_w_k, v_kv_w_v, v_attn_w_q, v_attn_sinks, v_attn_w_o, v_mix_ln_g, v_mix_ln_b,
                           v_mlp_w_up, v_mlp_w_down, v_mlp_ln_g, v_mlp_ln_b, v_ple_w_proj, v_ple_w_gate)))
    return _step(x, p, loss_target, w, m, v)
```

```python
import jax
import jax.numpy as jnp
from jax import lax
from jax.experimental import pallas as pl
from jax.experimental.pallas import tpu as pltpu

F32 = jnp.float32
BF16 = jnp.bfloat16
NS = 4
HEAD = 64
BLK = 128
ROPE = 16
ROPE_THETA = 500000.0
LN_EPS = 1e-5
NEG = -1e30
VMEM_LIMIT_MIB = 56
TILE_ROWS = 256
TILE_ROWS_WIDE = 512
BLOCK_DIM = 1024
KV_PER_STAGE = 1
HALO = 32
ADAM_LR, ADAM_B1, ADAM_B2, ADAM_EPS, ADAM_WD, ADAM_STEP = 0.001, 0.9, 0.999, 1e-08, 0.01, 10
MESH = pl.DeviceIdType.MESH
ANY = pl.BlockSpec(memory_space=pl.ANY)
NT = (((1,), (1,)), ((), ()))
TN = (((0,), (0,)), ((), ()))


_FOLLOW = []


def _pc(body, name, grid, in_specs, out_specs, out_shape, scratch=(), sem=None, vmem=VMEM_LIMIT_MIB, **kw):
    call = lambda fn, ins: pl.pallas_call(
        fn, name=name, grid=grid, in_specs=ins, out_specs=out_specs, out_shape=out_shape,
        scratch_shapes=list(scratch),
        compiler_params=pltpu.CompilerParams(dimension_semantics=sem, vmem_limit_bytes=vmem * 2 ** 20), **kw)
    if not _FOLLOW:
        return call(body, in_specs)
    extra = list(_FOLLOW)
    _FOLLOW.clear()
    n_in = len(in_specs)

    def ordered(*refs):
        return body(*refs[:n_in], *refs[n_in + len(extra):])

    run = call(ordered, list(in_specs) + [ANY] * len(extra))
    return lambda *args: run(*args, *extra)


def _rows(tm, n):
    return pl.BlockSpec((tm, n), lambda i: (i, 0))


def _const(shape):
    return pl.BlockSpec(shape, lambda *_: (0,) * len(shape))


def _wspec(w):
    buf, off, rows = w
    assert off % rows == 0
    return pl.BlockSpec((NS, rows, buf.shape[2]), lambda *_: (0, off // rows, 0))


def _rows_joined(w_ref):
    n, r, c = w_ref.shape
    return w_ref[...].reshape(n * r, c)


def _sds(shape, dtype):
    return jax.ShapeDtypeStruct(shape, dtype)


def _tile(t, rows=TILE_ROWS):
    return min(rows, t)


def _sigmoid(x):
    return 0.5 * jnp.tanh(0.5 * x) + 0.5


def _ln_stats(w):
    mu = jnp.mean(w, axis=-1, keepdims=True)
    xc = w - mu
    var = jnp.mean(xc * xc, axis=-1, keepdims=True)
    rstd = lax.rsqrt(var + LN_EPS)
    return xc * rstd, rstd, mu


def _ln_bwd(dy, w, g, stats=None):
    if stats is None:
        xhat, rstd, _ = _ln_stats(w)
    else:
        mu, rstd = stats
        xhat = (w - mu) * rstd
    dxhat = dy * g
    m1 = jnp.mean(dxhat, axis=-1, keepdims=True)
    m2 = jnp.mean(dxhat * xhat, axis=-1, keepdims=True)
    dw = rstd * (dxhat - m1 - xhat * m2)
    return dw, jnp.sum(dy * xhat, axis=0, keepdims=True), jnp.sum(dy, axis=0, keepdims=True)


def _acc_rows(ref, val, first):
    @pl.when(first)
    def _():
        ref[...] = val

    @pl.when(jnp.logical_not(first))
    def _():
        ref[...] += val


def conv_in_fwd(xb, w_in, b_in):
    T, D = xb.shape
    nw = w_in[0].shape[2]
    tm = _tile(T, TILE_ROWS_WIDE)

    def body(x_ref, w_ref, b_ref, h_ref):
        x = x_ref[...].astype(BF16)
        for j in range(NS):
            sl = slice(j * nw, (j + 1) * nw)
            h_ref[:, sl] = (jnp.dot(x, w_ref[j], preferred_element_type=F32) + b_ref[:, sl]).astype(BF16)

    return _pc(body, "conv_in_fwd", (T // tm,), [_rows(tm, D), _wspec(w_in), _const((1, NS * nw))],
               _rows(tm, NS * nw), _sds((T, NS * nw), BF16), sem=("parallel",))(xb, w_in[0], b_in)


CONV_ROWS = 16


def _phases(scr, sh):
    n = scr.shape[0] - 8
    for b in range(1, 8):
        sh[b - 1, 0:n, :] = scr[b:b + n, :]


def _spread(w_ref, wb, taps):
    for j in range(taps):
        wb[j] = jnp.broadcast_to(w_ref[j:j + 1, :], wb.shape[1:])


def _tap(scr, sh, o, n):
    b = o % 8
    return scr[o:o + n, :] if b == 0 else sh[b - 1, o - b:o - b + n, :]


def dwconv_fwd(h, w_dw, b_dw, ln_g, ln_b, taps):
    T = h.shape[0]
    C = h.shape[1] // 2
    tq = _tile(T)
    nh = tq // HALO
    off = HALO - (taps - 1)

    def body(a_ref, g_ref, ap_ref, gp_ref, w_ref, bdw_ref, lg_ref, lb_ref, cv_ref, s_ref, scr, sh, wb):
        i = pl.program_id(0)
        scr[HALO:HALO + tq, :] = a_ref[...].astype(F32) * _sigmoid(g_ref[...].astype(F32))
        up = ap_ref[...].astype(F32) * _sigmoid(gp_ref[...].astype(F32))
        scr[0:HALO, :] = jnp.where(i > 0, up, 0.0)
        _phases(scr, sh)
        _spread(w_ref, wb, taps)
        bias = jnp.broadcast_to(bdw_ref[...], (8, C))
        for r in range(tq // CONV_ROWS):
            accs = [bias] * (CONV_ROWS // 8)
            for j in range(taps):
                wj = wb[j]
                accs = [acc + wj * _tap(scr, sh, off + j + r * CONV_ROWS + 8 * k, 8) for k, acc in enumerate(accs)]
            for k, acc in enumerate(accs):
                cv_ref[r * CONV_ROWS + 8 * k:r * CONV_ROWS + 8 * k + 8, :] = acc
        xhat, _, _ = _ln_stats(cv_ref[...])
        ln = xhat * lg_ref[...] + lb_ref[...]
        s_ref[...] = (ln * _sigmoid(ln)).astype(BF16)

    prev = lambda col: pl.BlockSpec((HALO, C), lambda i: (jnp.maximum(i * nh - 1, 0), col))
    cur = lambda col: pl.BlockSpec((tq, C), lambda i: (i, col))
    return _pc(body, "dwconv_fwd", (T // tq,),
               [cur(0), cur(1), prev(0), prev(1), _const((HALO, C)), _const((1, C)), _const((1, C)), _const((1, C))],
               [_rows(tq, C), _rows(tq, C)], [_sds((T, C), F32), _sds((T, C), BF16)],
               scratch=[pltpu.VMEM((HALO + tq, C), F32), pltpu.VMEM((7, HALO + tq, C), F32), pltpu.VMEM((taps, 8, C), F32)],
               sem=("parallel",))(h, h, h, h, w_dw, b_dw, ln_g, ln_b)


def mm_res_ln(a, w, res, g, b, alpha, bias, name):
    T, K = a.shape
    D = res.shape[1]
    tm = _tile(T, TILE_ROWS_WIDE)

    def body(*refs):
        a_ref, w_ref, res_ref, g_ref, b_ref = refs[:5]
        n = 5
        if bias is not None:
            bias_ref = refs[5]
            n = 6
        pre_ref, mu_ref, rs_ref, xo_ref, xb_ref = refs[n:n + 5]
        acc = jnp.dot(a_ref[...], _rows_joined(w_ref), preferred_element_type=F32)
        if bias is not None:
            acc = acc + bias_ref[...]
        pre = alpha * res_ref[...] + acc
        xhat, rstd, mu = _ln_stats(pre)
        xo = xhat * g_ref[...] + b_ref[...]
        pre_ref[...] = pre
        mu_ref[...] = mu
        rs_ref[...] = rstd
        xo_ref[...] = xo
        xb_ref[...] = xo.astype(BF16)

    ins = [_rows(tm, K), _wspec(w), _rows(tm, D), _const((1, D)), _const((1, D))]
    args = [a, w[0], res, g, b]
    if bias is not None:
        ins.append(_const((1, D)))
        args.append(bias)
    pre, mu, rstd, xo, xb = _pc(
        body, name, (T // tm,), ins, [_rows(tm, D), _rows(tm, 1), _rows(tm, 1), _rows(tm, D), _rows(tm, D)],
        [_sds((T, D), F32), _sds((T, 1), F32), _sds((T, 1), F32), _sds((T, D), F32), _sds((T, D), BF16)],
        sem=("parallel",))(*args)
    return (pre, mu, rstd), xo, xb


def mlp_up_fwd(xb, w_up, name):
    T, D = xb.shape
    fs = w_up[0].shape[2]
    tm = _tile(T, TILE_ROWS_WIDE)

    def body(x_ref, w_ref, r_ref, t_ref):
        x = x_ref[...]
        for j in range(NS):
            sl = slice(j * fs, (j + 1) * fs)
            m = jnp.maximum(jnp.dot(x, w_ref[j], preferred_element_type=F32), 0.0)
            r_ref[:, sl] = (m * m).astype(BF16)
            t_ref[:, sl] = (2.0 * m).astype(BF16)

    return _pc(body, name, (T // tm,), [_rows(tm, D), _wspec(w_up)], [_rows(tm, NS * fs)] * 2,
               [_sds((T, NS * fs), BF16)] * 2, sem=("parallel",))(xb, w_up[0])


def ple_fwd(x, xb, p, layer, w_proj, w_gate, target, name):
    T, D = x.shape
    P = p.shape[2]
    ds = D // NS
    tm = _tile(T, TILE_ROWS_WIDE)
    last = target is not None

    def body(*refs):
        x_ref, xb_ref, p_ref, wp_ref, wg_ref = refs[:5]
        n = 5
        if last:
            t_ref = refs[5]
            n = 6
        o_ref, o2_ref, pp_ref, gl_ref = refs[n:n + 4]
        gl = jnp.dot(xb_ref[...], _rows_joined(wg_ref), preferred_element_type=F32)
        gl_ref[...] = gl.astype(BF16)
        sg = _sigmoid(gl)
        pb = p_ref[...].astype(BF16)
        sq = jnp.zeros((1, 1), F32)
        for j in range(NS):
            sl = slice(j * ds, (j + 1) * ds)
            pp = jnp.dot(pb, wp_ref[j], preferred_element_type=F32)
            pp_ref[:, sl] = pp.astype(BF16)
            out = x_ref[:, sl] + pp * sg[:, sl]
            if last:
                err = out - t_ref[:, sl]
                o_ref[:, sl] = err * (1.0 / D)
                e2 = jnp.sum(err * err, axis=0, keepdims=True)
                sq = sq + jnp.sum(e2, axis=1, keepdims=True)
            else:
                o_ref[:, sl] = out
                o2_ref[:, sl] = out.astype(BF16)
        if last:
            _acc_rows(o2_ref, jnp.broadcast_to(sq * (0.5 / D), (8, 128)), pl.program_id(0) == 0)

    ins = [_rows(tm, D), _rows(tm, D), pl.BlockSpec((None, tm, P), lambda i: (layer, i, 0)), _wspec(w_proj), _wspec(w_gate)]
    args = [x, xb, p, w_proj[0], w_gate[0]]
    if last:
        ins.append(_rows(tm, D))
        args.append(target)
        outs = [_rows(tm, D), _const((8, 128)), _rows(tm, D), _rows(tm, D)]
        shapes = [_sds((T, D), F32), _sds((8, 128), F32), _sds((T, D), BF16), _sds((T, D), BF16)]
    else:
        outs = [_rows(tm, D)] * 4
        shapes = [_sds((T, D), F32), _sds((T, D), BF16), _sds((T, D), BF16), _sds((T, D), BF16)]
    return _pc(body, name, (T // tm,), ins, outs, shapes, sem=("arbitrary",) if last else ("parallel",))(*args)


def _rope(x, cs_ref, sign):
    c = cs_ref[0]
    s = cs_ref[1] * sign
    lane = lax.broadcasted_iota(jnp.int32, c.shape, 1)
    first = (lane % HEAD) < (ROPE // 2)
    outs = []
    for gq in range(x.shape[1] // 128):
        xg = x[:, gq * 128:(gq + 1) * 128]
        sw = jnp.where(first, pltpu.roll(xg, 128 - ROPE // 2, 1), pltpu.roll(xg, ROPE // 2, 1))
        outs.append(xg * c + sw * s)
    return outs


def qkv_fwd(xb, w_q, w_k, w_v, cs):
    T, D = xb.shape
    HD, KVD = w_q[0].shape[2], w_k[0].shape[2]
    tm = _tile(T, TILE_ROWS_WIDE)
    scale = 1.0 / (HEAD ** 0.5)

    def body(x_ref, wq_ref, wk_ref, wv_ref, cs_ref, q_ref, k_ref, v_ref):
        def proj(w_ref):
            return jnp.dot(x_ref[...], _rows_joined(w_ref), preferred_element_type=F32)

        for gq, val in enumerate(_rope(proj(wq_ref), cs_ref, 1.0)):
            q_ref[:, gq * 128:(gq + 1) * 128] = (val * scale).astype(BF16)
        for gq, val in enumerate(_rope(proj(wk_ref), cs_ref, 1.0)):
            k_ref[:, gq * 128:(gq + 1) * 128] = val.astype(BF16)
        v_ref[...] = proj(wv_ref).astype(BF16)

    cs_spec = pl.BlockSpec((2, tm, 128), lambda i: (0, i, 0))
    return _pc(body, "qkv_fwd", (T // tm,), [_rows(tm, D), _wspec(w_q), _wspec(w_k), _wspec(w_v), cs_spec],
               [_rows(tm, HD), _rows(tm, KVD), _rows(tm, KVD)],
               [_sds((T, HD), BF16), _sds((T, KVD), BF16), _sds((T, KVD), BF16)], sem=("parallel",))(
                   xb, w_q[0], w_k[0], w_v[0], cs)


def _band_mask(n):
    row = lax.broadcasted_iota(jnp.int32, (BLK, 2 * BLK), 0)
    col = lax.broadcasted_iota(jnp.int32, (BLK, 2 * BLK), 1)
    return (col > row) & (col <= row + BLK) & ((col >= BLK) | (n > 0))


def _head(h):
    return slice(h * HEAD, (h + 1) * HEAD)


def _softmax_sink(s, sink):
    m = jnp.maximum(jnp.max(s, axis=-1, keepdims=True), sink)
    e = jnp.exp(s - m)
    es = jnp.exp(sink - m)
    den = jnp.sum(e, axis=-1, keepdims=True) + es
    inv = 1.0 / den
    return e * inv, es * inv


def attn_fwd(q, k, v, sinks):
    T, HD = q.shape
    KVD = k.shape[1]
    NKV = KVD // HEAD
    G = HD // KVD

    def body(s_ref, q_ref, kc_ref, kp_ref, vc_ref, vp_ref, o_ref):
        valid = _band_mask(pl.program_id(0))
        NH = NKV * G
        k2 = [jnp.concatenate([kp_ref[:, _head(kh)], kc_ref[:, _head(kh)]], axis=0) for kh in range(NKV)]
        v2 = [jnp.concatenate([vp_ref[:, _head(kh)], vc_ref[:, _head(kh)]], axis=0) for kh in range(NKV)]
        sc = [lax.dot_general(q_ref[:, _head(hh)], k2[hh // G], NT, preferred_element_type=F32) for hh in range(NH)]
        pb = [_softmax_sink(jnp.where(valid, s, NEG), s_ref[0, hh])[0].astype(BF16) for hh, s in enumerate(sc)]
        for hh, p in enumerate(pb):
            o_ref[:, _head(hh)] = jnp.dot(p, v2[hh // G], preferred_element_type=F32).astype(BF16)

    cur = lambda n_: pl.BlockSpec((BLK, n_), lambda n: (n, 0))
    prev = lambda n_: pl.BlockSpec((BLK, n_), lambda n: (jnp.maximum(n - 1, 0), 0))
    return _pc(body, "attn_fwd", (T // BLK,),
               [pl.BlockSpec(memory_space=pltpu.SMEM), cur(HD), cur(KVD), prev(KVD), cur(KVD), prev(KVD)],
               cur(HD), _sds((T, HD), BF16), sem=("parallel",))(sinks, q, k, k, v, v)


def ple_bwd(dxo, pp, gl, w_gate, name):
    T, D = dxo.shape
    tm = _tile(T, TILE_ROWS_WIDE)

    def body(d_ref, pp_ref, gl_ref, wg_ref, dpp_ref, dgl_ref, dx_ref):
        d = d_ref[...]
        sg = _sigmoid(gl_ref[...].astype(F32))
        dpp_ref[...] = (d * sg).astype(BF16)
        dgl = (d * pp_ref[...].astype(F32) * sg * (1.0 - sg)).astype(BF16)
        dgl_ref[...] = dgl
        dx_ref[...] = d + lax.dot_general(dgl, _rows_joined(wg_ref), NT, preferred_element_type=F32)

    return _pc(body, name, (T // tm,), [_rows(tm, D)] * 3 + [_wspec(w_gate)], [_rows(tm, D)] * 3,
               [_sds((T, D), BF16), _sds((T, D), BF16), _sds((T, D), F32)], sem=("parallel",))(dxo, pp, gl, w_gate[0])


def mlp_bwd1(dy, pre, g, t, w_down, name):
    T, D = dy.shape
    fs = w_down[2]
    tm = _tile(T, TILE_ROWS_WIDE)

    def body(dy_ref, pre_ref, mu_ref, rs_ref, g_ref, t_ref, w_ref, dw_ref, dwb_ref, dm_ref, dg_ref, db_ref):
        dw, dg, db = _ln_bwd(dy_ref[...], pre_ref[...], g_ref[...], (mu_ref[...], rs_ref[...]))
        first = pl.program_id(0) == 0
        _acc_rows(dg_ref, dg, first)
        _acc_rows(db_ref, db, first)
        dwb = dw.astype(BF16)
        dw_ref[...] = dw
        dwb_ref[...] = dwb
        for j in range(NS):
            sl = slice(j * fs, (j + 1) * fs)
            dr = lax.dot_general(dwb, w_ref[j], NT, preferred_element_type=F32)
            dm_ref[:, sl] = (dr * t_ref[:, sl].astype(F32)).astype(BF16)

    return _pc(body, name, (T // tm,),
               [_rows(tm, D), _rows(tm, D), _rows(tm, 1), _rows(tm, 1), _const((1, D)), _rows(tm, NS * fs), _wspec(w_down)],
               [_rows(tm, D), _rows(tm, D), _rows(tm, NS * fs), _const((1, D)), _const((1, D))],
               [_sds((T, D), F32), _sds((T, D), BF16), _sds((T, NS * fs), BF16), _sds((1, D), F32), _sds((1, D), F32)],
               sem=("arbitrary",))(dy, *pre, g, t, w_down[0])


def mlp_bwd2(dpre, dm, w_up, alpha, pre_mix, g_mix, w_mix, name):
    T, D = dpre.shape
    fs = w_up[0].shape[2]
    ms = w_mix[2]
    tm = _tile(T, TILE_ROWS_WIDE)

    def body(dp_ref, dm_ref, wu_ref, pre_ref, mu_ref, rs_ref, g_ref, wm_ref, dw_ref, dwb_ref, do_ref, dg_ref, db_ref, dc_ref):
        dy = alpha * dp_ref[...]
        for j in range(NS):
            dy = dy + lax.dot_general(dm_ref[:, j * fs:(j + 1) * fs], wu_ref[j], NT, preferred_element_type=F32)
        dw, dg, db = _ln_bwd(dy, pre_ref[...], g_ref[...], (mu_ref[...], rs_ref[...]))
        first = pl.program_id(0) == 0
        _acc_rows(dg_ref, dg, first)
        _acc_rows(db_ref, db, first)
        _acc_rows(dc_ref, jnp.sum(dw, axis=0, keepdims=True), first)
        dwb = dw.astype(BF16)
        dw_ref[...] = dw
        dwb_ref[...] = dwb
        do_ref[...] = lax.dot_general(dwb, _rows_joined(wm_ref), NT, preferred_element_type=F32).astype(BF16)

    return _pc(body, name, (T // tm,),
               [_rows(tm, D), _rows(tm, NS * fs), _wspec(w_up), _rows(tm, D), _rows(tm, 1), _rows(tm, 1), _const((1, D)),
                _wspec(w_mix)],
               [_rows(tm, D), _rows(tm, D), _rows(tm, NS * ms), _const((1, D)), _const((1, D)), _const((1, D))],
               [_sds((T, D), F32), _sds((T, D), BF16), _sds((T, NS * ms), BF16)] + [_sds((1, D), F32)] * 3,
               sem=("arbitrary",))(dpre, dm, w_up[0], *pre_mix, g_mix, w_mix[0])


def attn_bwd(q, k, v, do, sinks):
    T, HD = q.shape
    KVD = k.shape[1]
    NH, NKV = HD // HEAD, KVD // HEAD
    G = NH // NKV
    nb = T // BLK

    def body(s_ref, q_ref, do_ref, kc_ref, kp_ref, vc_ref, vp_ref, dq_ref, dk_ref, dv_ref, ds_ref, ck, cv):
        n = pl.program_id(0)

        @pl.when(n == 0)
        def _():
            ck[...] = jnp.zeros_like(ck)
            cv[...] = jnp.zeros_like(cv)
            ds_ref[...] = jnp.zeros_like(ds_ref)

        @pl.when(n < nb)
        def _():
            valid = _band_mask(n)
            for k0 in range(0, NKV, KV_PER_STAGE):
                khs = range(k0, min(k0 + KV_PER_STAGE, NKV))
                k2 = {kh: jnp.concatenate([kp_ref[:, _head(kh)], kc_ref[:, _head(kh)]], axis=0) for kh in khs}
                v2 = {kh: jnp.concatenate([vp_ref[:, _head(kh)], vc_ref[:, _head(kh)]], axis=0) for kh in khs}
                hs = [kh * G + gq for kh in khs for gq in range(G)]
                qs = {hh: q_ref[:, _head(hh)] for hh in hs}
                dos = {hh: do_ref[:, _head(hh)] for hh in hs}
                sc = {hh: lax.dot_general(qs[hh], k2[hh // G], NT, preferred_element_type=F32) for hh in hs}
                pr = {hh: _softmax_sink(jnp.where(valid, sc[hh], NEG), s_ref[0, hh]) for hh in hs}
                dp = {hh: lax.dot_general(dos[hh], v2[hh // G], NT, preferred_element_type=F32) for hh in hs}
                delta = {hh: jnp.sum(pr[hh][0] * dp[hh], axis=-1, keepdims=True) for hh in hs}
                dsb = {hh: (pr[hh][0] * (dp[hh] - delta[hh])).astype(BF16) for hh in hs}
                pb = {hh: pr[hh][0].astype(BF16) for hh in hs}
                for hh in hs:
                    ds_ref[hh:hh + 1, :] += jnp.broadcast_to(-jnp.sum(pr[hh][1] * delta[hh], axis=0, keepdims=True), (1, 128))
                for hh in hs:
                    dq_ref[:, _head(hh)] = jnp.dot(dsb[hh], k2[hh // G], preferred_element_type=F32)
                for kh in khs:
                    kv = _head(kh)
                    grp = [kh * G + gq for gq in range(G)]
                    dk2 = lax.dot_general(jnp.concatenate([dsb[hh] for hh in grp], axis=0),
                                          jnp.concatenate([qs[hh] for hh in grp], axis=0), TN, preferred_element_type=F32)
                    dv2 = lax.dot_general(jnp.concatenate([pb[hh] for hh in grp], axis=0),
                                          jnp.concatenate([dos[hh] for hh in grp], axis=0), TN, preferred_element_type=F32)
                    dk_ref[:, kv] = ck[:, kv] + dk2[0:BLK]
                    dv_ref[:, kv] = cv[:, kv] + dv2[0:BLK]
                    ck[:, kv] = dk2[BLK:2 * BLK]
                    cv[:, kv] = dv2[BLK:2 * BLK]

        @pl.when(n == nb)
        def _():
            dk_ref[...] = ck[...]
            dv_ref[...] = cv[...]

    qcur = pl.BlockSpec((BLK, HD), lambda n: (jnp.minimum(n, nb - 1), 0))
    kcur = pl.BlockSpec((BLK, KVD), lambda n: (jnp.minimum(n, nb - 1), 0))
    kprev = pl.BlockSpec((BLK, KVD), lambda n: (jnp.maximum(n - 1, 0), 0))
    return _pc(body, "attn_bwd", (nb + 1,),
               [pl.BlockSpec(memory_space=pltpu.SMEM), qcur, qcur, kcur, kprev, kcur, kprev],
               [qcur, kprev, kprev, _const((NH, 128))],
               [_sds((T, HD), F32), _sds((T, KVD), F32), _sds((T, KVD), F32), _sds((NH, 128), F32)],
               scratch=[pltpu.VMEM((BLK, KVD), F32), pltpu.VMEM((BLK, KVD), F32)],
               sem=("arbitrary",))(sinks, q, do, k, k, v, v)


def qkv_bwd(dq, dk, dv, dpre_mix, w_q, w_k, w_v, cs, alpha):
    T, HD = dq.shape
    KVD = dk.shape[1]
    D = dpre_mix.shape[1]
    tm = _tile(T, TILE_ROWS_WIDE)
    scale = 1.0 / (HEAD ** 0.5)

    def body(dq_ref, dk_ref, dv_ref, dp_ref, wq_ref, wk_ref, wv_ref, cs_ref, dqb_ref, dkb_ref, dvb_ref, dx_ref):
        for gq, val in enumerate(_rope(dq_ref[...], cs_ref, -1.0)):
            dqb_ref[:, gq * 128:(gq + 1) * 128] = (val * scale).astype(BF16)
        for gq, val in enumerate(_rope(dk_ref[...], cs_ref, -1.0)):
            dkb_ref[:, gq * 128:(gq + 1) * 128] = val.astype(BF16)
        dvb_ref[...] = dv_ref[...].astype(BF16)
        dqb, dkb, dvb = dqb_ref[...], dkb_ref[...], dvb_ref[...]
        dx_ref[...] = (alpha * dp_ref[...]
                       + lax.dot_general(dqb, _rows_joined(wq_ref), NT, preferred_element_type=F32)
                       + lax.dot_general(dkb, _rows_joined(wk_ref), NT, preferred_element_type=F32)
                       + lax.dot_general(dvb, _rows_joined(wv_ref), NT, preferred_element_type=F32))

    cs_spec = pl.BlockSpec((2, tm, 128), lambda i: (0, i, 0))
    return _pc(body, "qkv_bwd", (T // tm,),
               [_rows(tm, HD), _rows(tm, KVD), _rows(tm, KVD), _rows(tm, D), _wspec(w_q), _wspec(w_k), _wspec(w_v), cs_spec],
               [_rows(tm, HD), _rows(tm, KVD), _rows(tm, KVD), _rows(tm, D)],
               [_sds((T, HD), BF16), _sds((T, KVD), BF16), _sds((T, KVD), BF16), _sds((T, D), F32)],
               sem=("parallel",))(dq, dk, dv, dpre_mix, w_q[0], w_k[0], w_v[0], cs)


def conv_mid_bwd(ds, cv, ln_g, ln_b):
    T, C = cv.shape
    tm = _tile(T, TILE_ROWS_WIDE)

    def body(ds_ref, cv_ref, g_ref, b_ref, dcv_ref, dg_ref, db_ref, dc_ref):
        xhat, _, _ = _ln_stats(cv_ref[...])
        ln = xhat * g_ref[...] + b_ref[...]
        sg = _sigmoid(ln)
        dl = ds_ref[...].astype(F32) * (sg * (1.0 + ln * (1.0 - sg)))
        dcv, dg, db = _ln_bwd(dl, cv_ref[...], g_ref[...])
        first = pl.program_id(0) == 0
        _acc_rows(dg_ref, dg, first)
        _acc_rows(db_ref, db, first)
        _acc_rows(dc_ref, jnp.sum(dcv, axis=0, keepdims=True), first)
        dcv_ref[...] = dcv

    return _pc(body, "conv_mid_bwd", (T // tm,), [_rows(tm, C), _rows(tm, C), _const((1, C)), _const((1, C))],
               [_rows(tm, C), _const((1, C)), _const((1, C)), _const((1, C))],
               [_sds((T, C), F32)] + [_sds((1, C), F32)] * 3, sem=("arbitrary",))(ds, cv, ln_g, ln_b)


def dwconv_bwd(dcv, h, w_dw, taps):
    T, C = dcv.shape
    tq = _tile(T)
    nh = tq // HALO
    nblk = T // tq
    off = HALO - (taps - 1)

    def body(d_ref, dn_ref, a_ref, g_ref, ap_ref, gp_ref, w_ref, dh_ref, dw_ref, dbi_ref, su, sus, sd, sds, wb):
        i = pl.program_id(0)
        su[HALO:HALO + tq, :] = a_ref[...].astype(F32) * _sigmoid(g_ref[...].astype(F32))
        up = ap_ref[...].astype(F32) * _sigmoid(gp_ref[...].astype(F32))
        su[0:HALO, :] = jnp.where(i > 0, up, 0.0)
        sd[0:tq, :] = d_ref[...]
        sd[tq:tq + HALO, :] = jnp.where(i < nblk - 1, dn_ref[...], 0.0)
        _phases(su, sus)
        _phases(sd, sds)

        @pl.when(i == 0)
        def _():
            dw_ref[...] = jnp.zeros_like(dw_ref)

        for j in range(taps):
            dw_ref[j:j + 1, :] += jnp.sum(d_ref[...] * _tap(su, sus, off + j, tq), axis=0, keepdims=True)
        sa = jnp.zeros((1, C), F32)
        sb = jnp.zeros((1, C), F32)
        _spread(w_ref, wb, taps)
        for r in range(tq // CONV_ROWS):
            rows = slice(r * CONV_ROWS, (r + 1) * CONV_ROWS)
            dus = [wb[0] * _tap(sd, sds, taps - 1 + r * CONV_ROWS + 8 * k, 8) for k in range(CONV_ROWS // 8)]
            for j in range(1, taps):
                wj = wb[j]
                dus = [acc + wj * _tap(sd, sds, taps - 1 - j + r * CONV_ROWS + 8 * k, 8) for k, acc in enumerate(dus)]
            du = jnp.concatenate(dus, axis=0)
            a = a_ref[rows, :].astype(F32)
            sg = _sigmoid(g_ref[rows, :].astype(F32))
            da = du * sg
            dgt = du * a * sg * (1.0 - sg)
            dh_ref[rows, 0:C] = da.astype(BF16)
            dh_ref[rows, C:2 * C] = dgt.astype(BF16)
            sa = sa + jnp.sum(da, axis=0, keepdims=True)
            sb = sb + jnp.sum(dgt, axis=0, keepdims=True)
        first = i == 0
        _acc_rows(dbi_ref.at[:, 0:C], sa, first)
        _acc_rows(dbi_ref.at[:, C:2 * C], sb, first)

    prev = lambda col: pl.BlockSpec((HALO, C), lambda i: (jnp.maximum(i * nh - 1, 0), col))
    nxt = pl.BlockSpec((HALO, C), lambda i: (jnp.minimum((i + 1) * nh, T // HALO - 1), 0))
    cur = lambda col: pl.BlockSpec((tq, C), lambda i: (i, col))
    return _pc(body, "dwconv_bwd", (nblk,),
               [cur(0), nxt, cur(0), cur(1), prev(0), prev(1), _const((HALO, C))],
               [_rows(tq, 2 * C), _const((HALO, C)), _const((1, 2 * C))],
               [_sds((T, 2 * C), BF16), _sds((HALO, C), F32), _sds((1, 2 * C), F32)],
               scratch=[pltpu.VMEM((HALO + tq, C), F32), pltpu.VMEM((7, HALO + tq, C), F32),
                        pltpu.VMEM((HALO + tq, C), F32), pltpu.VMEM((7, HALO + tq, C), F32), pltpu.VMEM((taps, 8, C), F32)],
               sem=("arbitrary",))(dcv, dcv, h, h, h, h, w_dw)


def conv_in_bwd(dh, dpre_mix, w_in, alpha):
    T, D = dpre_mix.shape
    nw = w_in[0].shape[2]
    tm = _tile(T, TILE_ROWS_WIDE)

    def body(dh_ref, dp_ref, w_ref, dx_ref):
        acc = alpha * dp_ref[...]
        for j in range(NS):
            acc = acc + lax.dot_general(dh_ref[:, j * nw:(j + 1) * nw], w_ref[j], NT, preferred_element_type=F32)
        dx_ref[...] = acc

    return _pc(body, "conv_in_bwd", (T // tm,), [_rows(tm, NS * nw), _rows(tm, D), _wspec(w_in)], _rows(tm, D),
               _sds((T, D), F32), sem=("parallel",))(dh, dpre_mix, w_in[0])


def wgrad(a, b, row_sharded, name, into):
    prev, out_shape, off = into
    layer = None
    if isinstance(a, tuple):
        layer, a = a
    T, Ka = a.shape[-2:]
    Nb = b.shape[1]
    ka, tn = min(Ka, BLOCK_DIM), min(Nb, BLOCK_DIM)
    tt = T if (Ka // ka) * (Nb // tn) >= 4 else T // 2
    nt = T // tt
    if row_sharded:
        sr = Ka // NS
        spb = max(ka // sr, 1)
        rb = ka // spb
        assert out_shape[2] == Nb and off % rb == 0
        out_spec = pl.BlockSpec((spb, rb, tn), lambda i, j, t: (i, off // rb, j))
    else:
        sc = Nb // NS
        spb = max(tn // sc, 1)
        rb = ka
        assert out_shape[2] == sc and off % ka == 0
        out_spec = pl.BlockSpec((spb, ka, tn // spb), lambda i, j, t: (j, off // ka + i, 0))

    def body(a_ref, b_ref, *rest):
        o_ref, acc = rest[-2:]
        t = pl.program_id(2)
        av = a_ref[...]
        if av.dtype != BF16:
            av = av.astype(BF16)
        d = lax.dot_general(av, b_ref[...], TN, preferred_element_type=F32)

        @pl.when(t == 0)
        def _():
            acc[...] = d

        @pl.when(t > 0)
        def _():
            acc[...] += d

        @pl.when(t == nt - 1)
        def _():
            for s in range(spb):
                if row_sharded:
                    o_ref[s] = acc[s * rb:(s + 1) * rb, :].astype(BF16)
                else:
                    o_ref[s] = acc[:, s * (tn // spb):(s + 1) * (tn // spb)].astype(BF16)

    a_spec = (pl.BlockSpec((tt, ka), lambda i, j, t: (t, i)) if layer is None
              else pl.BlockSpec((None, tt, ka), lambda i, j, t: (layer, t, i)))
    ins = [a_spec, pl.BlockSpec((tt, tn), lambda i, j, t: (t, j))]
    args = [a, b]
    kw = {}
    if prev is not None:
        ins.append(ANY)
        args.append(prev)
        kw["input_output_aliases"] = {2: 0}
    return _pc(body, name, (Ka // ka, Nb // tn, nt), ins, out_spec, _sds(out_shape, BF16),
               scratch=[pltpu.VMEM((ka, tn), F32)], sem=("parallel", "parallel", "arbitrary"), **kw)(*args)


def _adamw_math(w, g, m, v):
    c1 = 1.0 - ADAM_B1 ** ADAM_STEP
    c2 = 1.0 - ADAM_B2 ** ADAM_STEP
    mn = ADAM_B1 * m + (1.0 - ADAM_B1) * g
    vn = ADAM_B2 * v + (1.0 - ADAM_B2) * (g * g)
    return -ADAM_LR * ((mn / c1) / (jnp.sqrt(vn / c2) + ADAM_EPS) + ADAM_WD * w), mn, vn


def adamw_layer(w, m, v, layer, gbuf, off, prev, name):
    L, R, W = w.shape
    tr = TILE_ROWS
    assert R % tr == 0 and off % tr == 0

    def body(w_ref, g_ref, m_ref, v_ref, *rest):
        go_ref, d_ref, mo_ref, vo_ref = rest[-4:]
        g = g_ref[...]
        go_ref[...] = g
        d_ref[...], mo_ref[...], vo_ref[...] = _adamw_math(w_ref[...], g, m_ref[...], v_ref[...])

    lay = pl.BlockSpec((None, tr, W), lambda r: (layer, r, 0))
    ins = [lay, pl.BlockSpec((tr, W), lambda r: (off // tr + r, 0)), lay, lay]
    args = [w, gbuf, m, v]
    kw = {}
    if prev is not None:
        ins += [ANY] * 4
        args += list(prev)
        kw["input_output_aliases"] = {4 + k: k for k in range(4)}
    return _pc(body, name, (R // tr,), ins, [lay] * 4, [_sds((L, R, W), F32)] * 4, sem=("parallel",), **kw)(*args)


def adamw_many(ws, gs, ms, vs):
    n = len(ws)

    def body(*refs):
        for k in range(n):
            d, mn, vn = _adamw_math(refs[k][...], refs[n + k][...], refs[2 * n + k][...], refs[3 * n + k][...])
            refs[4 * n + k][...] = d
            refs[5 * n + k][...] = mn
            refs[6 * n + k][...] = vn

    outs = pl.pallas_call(body, name="adamw_small", out_shape=[_sds(a.shape, F32) for a in ws] * 3)(*ws, *gs, *ms, *vs)
    return outs[:n], outs[n:2 * n], outs[2 * n:]


def _rope_tables(T):
    pos = jnp.arange(T, dtype=F32)
    inv_freq = ROPE_THETA ** (-jnp.arange(0, ROPE, 2, dtype=F32) / ROPE)
    ang = pos[:, None] * inv_freq[None, :]
    cos, sin = jnp.cos(ang), jnp.sin(ang)
    pad = HEAD - ROPE
    c = jnp.concatenate([cos, cos, jnp.ones((T, pad), F32)], axis=1)
    s = jnp.concatenate([-sin, sin, jnp.zeros((T, pad), F32)], axis=1)
    return jnp.stack([jnp.tile(c, (1, 128 // HEAD)), jnp.tile(s, (1, 128 // HEAD))])


def _local_step(x, p, target, W, small, lay, hook=None):
    if hook is None:
        hook = lambda stage, after, G, sg=None: None
    T, D = x.shape
    depth = small["mix_ln_g"].shape[0]
    alpha = float((2 * depth) ** 0.25)
    taps = small["taps"]
    row = lambda a, i: a[i:i + 1]
    cs = _rope_tables(T)

    h = conv_in_fwd(x, W["conv_w_in"], small["conv_b_in"])
    cv, s = dwconv_fwd(h, small["conv_w_dw"], small["conv_b_dw"], small["conv_ln_g"], small["conv_ln_b"], taps)
    hook("weights1", s, None)
    pre_mix0, x1, x1b = mm_res_ln(s, W["conv_w_out"], x, row(small["mix_ln_g"], 0), row(small["mix_ln_b"], 0), alpha,
                                  small["conv_b_out"], "conv_out_fwd")
    r0, t0 = mlp_up_fwd(x1b, W["mlp_w_up0"], "mlp_up_fwd0")
    pre_mlp0, x2, x2b = mm_res_ln(r0, W["mlp_w_down0"], x1, row(small["mlp_ln_g"], 0), row(small["mlp_ln_b"], 0), alpha,
                                  None, "mlp_down_fwd0")
    x3, x3b, pp0, gl0 = ple_fwd(x2, x2b, p, 0, W["ple_w_proj0"], W["ple_w_gate0"], None, "ple_fwd0")

    hook("weights2", x3b, None)
    q, k, v = qkv_fwd(x3b, W["attn_w_q"], W["kv_w_k"], W["kv_w_v"], cs)
    o = attn_fwd(q, k, v, small["attn_sinks"])
    pre_mix1, x4, x4b = mm_res_ln(o, W["attn_w_o"], x3, row(small["mix_ln_g"], 1), row(small["mix_ln_b"], 1), alpha,
                                  None, "attn_out_fwd")
    r1, t1 = mlp_up_fwd(x4b, W["mlp_w_up1"], "mlp_up_fwd1")
    pre_mlp1, x5, x5b = mm_res_ln(r1, W["mlp_w_down1"], x4, row(small["mlp_ln_g"], 1), row(small["mlp_ln_b"], 1), alpha,
                                  None, "mlp_down_fwd1")
    dx6, loss, pp1, gl1 = ple_fwd(x5, x5b, p, 1, W["ple_w_proj1"], W["ple_w_gate1"], target, "ple_fwd1")

    G, sg = {}, {}
    where = {n: (key, off) for key in lay for n, off, _ in lay[key]}
    rows_of = {key: sum(r for _, _, r in lay[key]) for key in lay}

    def wg(name, a, b, row_sharded):
        key, off = where[name]
        shape = (NS, rows_of[key], W[name][0].shape[2])
        G[key] = wgrad(a, b, row_sharded, "wg_" + name, (G.get(key), shape, off))

    dpp1, dgl1, dx5 = ple_bwd(dx6, pp1, gl1, W["ple_w_gate1"], "ple_bwd1")
    wg("ple_w_proj1", (1, p), dpp1, False)
    wg("ple_w_gate1", x5b, dgl1, True)
    dpre_mlp1, dpre_mlp1b, dm1, g_mlp_g1, g_mlp_b1 = mlp_bwd1(dx5, pre_mlp1, row(small["mlp_ln_g"], 1), t1,
                                                              W["mlp_w_down1"], "mlp_bwd1_1")
    wg("mlp_w_down1", r1, dpre_mlp1b, True)
    wg("mlp_w_up1", x4b, dm1, False)
    dpre_mix1, dpre_mix1b, do, g_mix_g1, g_mix_b1, _ = mlp_bwd2(dpre_mlp1, dm1, W["mlp_w_up1"], alpha, pre_mix1,
                                                                row(small["mix_ln_g"], 1), W["attn_w_o"], "mlp_bwd2_1")
    wg("attn_w_o", o, dpre_mix1b, True)
    dq, dk, dv, dsinks = attn_bwd(q, k, v, do, small["attn_sinks"])
    dqb, dkb, dvb, dx3 = qkv_bwd(dq, dk, dv, dpre_mix1,
                                 W["attn_w_q"], W["kv_w_k"], W["kv_w_v"], cs, alpha)
    wg("attn_w_q", x3b, dqb, True)
    wg("kv_w_k", x3b, dkb, True)
    wg("kv_w_v", x3b, dvb, True)
    hook("grads3", None, G)

    dpp0, dgl0, dx2 = ple_bwd(dx3, pp0, gl0, W["ple_w_gate0"], "ple_bwd0")
    wg("ple_w_proj0", (0, p), dpp0, False)
    wg("ple_w_gate0", x2b, dgl0, True)
    dpre_mlp0, dpre_mlp0b, dm0, g_mlp_g0, g_mlp_b0 = mlp_bwd1(dx2, pre_mlp0, row(small["mlp_ln_g"], 0), t0,
                                                              W["mlp_w_down0"], "mlp_bwd1_0")
    wg("mlp_w_down0", r0, dpre_mlp0b, True)
    wg("mlp_w_up0", x1b, dm0, False)
    hook("grads2", None, G)
    dpre_mix0, dpre_mix0b, dsw, g_mix_g0, g_mix_b0, g_b_out = mlp_bwd2(dpre_mlp0, dm0, W["mlp_w_up0"], alpha, pre_mix0,
                                                                      row(small["mix_ln_g"], 0), W["conv_w_out"],
                                                                      "mlp_bwd2_0")
    wg("conv_w_out", s, dpre_mix0b, True)
    hook("grads1", None, G)
    dcv, g_cln_g, g_cln_b, g_b_dw = conv_mid_bwd(dsw, cv, small["conv_ln_g"], small["conv_ln_b"])
    dh, g_w_dw, g_b_in = dwconv_bwd(dcv, h, small["conv_w_dw"], taps)
    wg("conv_w_in", x, dh, False)

    sg["conv_b_in"] = g_b_in
    sg["conv_w_dw"] = g_w_dw
    sg["conv_b_dw"], sg["conv_ln_g"], sg["conv_ln_b"], sg["conv_b_out"] = g_b_dw, g_cln_g, g_cln_b, g_b_out
    sg["mix_ln_g"] = [g_mix_g0, g_mix_g1]
    sg["mix_ln_b"] = [g_mix_b0, g_mix_b1]
    sg["mlp_ln_g"] = [g_mlp_g0, g_mlp_g1]
    sg["mlp_ln_b"] = [g_mlp_b0, g_mlp_b1]
    sg["attn_sinks"] = dsinks[:, 0][None, :]
    sg["loss"] = loss
    hook("grads0", None, G, sg)
    grad_x = conv_in_bwd(dh, dpre_mix0, W["conv_w_in"], alpha)
    return loss, grad_x, G, sg


BUFFERS = (("b0", ("conv_w_in",)), ("a0", ("conv_w_out",)),
           ("a1", ("mlp_w_up0", "mlp_w_down0", "ple_w_gate0")), ("c1", ("ple_w_proj0",)),
           ("a2", ("mlp_w_up1", "mlp_w_down1", "ple_w_gate1", "attn_w_q", "attn_w_o")),
           ("c2", ("kv_w_k", "kv_w_v", "ple_w_proj1")))
GROUPS = (("b0",), ("a0", "a1", "c1"), ("a2", "c2"))
REDUCED = (("b0",), ("a0",), ("a1", "c1"), ("a2", "c2"))
ROW_SHARDED = {"mlp_w_down0", "mlp_w_down1", "ple_w_gate0", "ple_w_gate1", "conv_w_out", "attn_w_q", "attn_w_o", "kv_w_k",
               "kv_w_v"}


def _split_layers(weights):
    out = {"conv_w_in": weights["conv_w_in"][0], "conv_w_out": weights["conv_w_out"][0],
           "attn_w_q": weights["attn_w_q"][0], "attn_w_o": weights["attn_w_o"][0],
           "kv_w_k": weights["kv_w_k"], "kv_w_v": weights["kv_w_v"]}
    for n in ("mlp_w_up", "mlp_w_down", "ple_w_proj", "ple_w_gate"):
        for i in range(weights[n].shape[0]):
            out[n + str(i)] = weights[n][i]
    return out


def _layout(shards):
    lay = {}
    for key, names in BUFFERS:
        off, rows = 0, []
        for n in names:
            rows.append((n, off, shards[n].shape[0]))
            off += shards[n].shape[0]
        lay[key] = rows
    return lay


def _place():
    return lax.axis_index("x"), lax.axis_index("y"), lax.axis_index("c")


def _flip(v, f):
    return (v + f) % 2 if f else v


CHIP_FLIPS = ((1, 0), (0, 1), (1, 1))


HBM = pl.BlockSpec(memory_space=pltpu.HBM)
SEM = pl.BlockSpec(memory_space=pltpu.SEMAPHORE)
EFFECT = pltpu.SideEffectType.DATAFLOW_SIDE_EFFECTING


def _half(ref, rows, c):
    return ref.at[pl.ds(pl.multiple_of(c * (rows // 2), 16), rows // 2), :]


def _gather_copies(refs, shapes, whole, send, recv):
    x, y, c = _place()
    me = 2 * x + y
    na = len(refs)
    cps = []
    for d, (fx, fy) in enumerate(CHIP_FLIPS):
        to = (_flip(x, fx), _flip(y, fy), c)
        for k in range(na):
            mine = refs[k].at[me] if k >= na - whole else _half(refs[k].at[me], shapes[k][1], c)
            cps.append(pltpu.make_async_remote_copy(mine, mine, send.at[d * na + k], recv.at[d * na + k], device_id=to,
                                                    device_id_type=MESH))
    return cps


def gather_start(bufs, whole, after, name):
    na = len(bufs)
    shapes = [b.shape for b in bufs]
    nsem = len(CHIP_FLIPS) * na

    def body(*refs):
        ins = refs[:na]
        send, recv = refs[-(na + 3)], refs[-(na + 2)]
        token = refs[-1]
        for cp in _gather_copies(ins, shapes, whole, send, recv):
            cp.start()
        token[...] = jnp.zeros_like(token)

    args = [pltpu.with_memory_space_constraint(b, pltpu.HBM) for b in bufs]
    ins = [HBM] * na
    if after is not None:
        args.append(after)
        ins.append(ANY)
    return pl.pallas_call(
        body, name=name, in_specs=ins,
        out_specs=[SEM, SEM] + [HBM] * na + [pl.BlockSpec(memory_space=pltpu.VMEM)],
        out_shape=[pltpu.SemaphoreType.DMA((nsem,)), pltpu.SemaphoreType.DMA((nsem,))]
        + [pltpu.HBM(b.shape, b.dtype) for b in bufs] + [_sds((8, 128), F32)],
        input_output_aliases={k: k + 2 for k in range(na)},
        compiler_params=pltpu.CompilerParams(has_side_effects=EFFECT))(*args)


def gather_wait(send, recv, bufs, whole, after, name):
    na = len(bufs)
    shapes = [b.shape for b in bufs]

    def body(*refs):
        ins = refs[:na]
        send_ref, recv_ref = refs[na], refs[na + 1]
        for cp in _gather_copies(ins, shapes, whole, send_ref, recv_ref):
            cp.wait_send()
            cp.wait_recv()

    return pl.pallas_call(
        body, name=name, in_specs=[HBM] * na + [SEM, SEM, ANY], out_specs=[HBM] * na,
        out_shape=[pltpu.HBM(b.shape, b.dtype) for b in bufs], input_output_aliases={k: k for k in range(na)},
        compiler_params=pltpu.CompilerParams(has_side_effects=EFFECT))(*bufs, send, recv, after)


def sibling_forward(bufs, name):
    nb = len(bufs)

    def body(*refs):
        outs = refs[nb:2 * nb]
        send, recv = refs[2 * nb:]
        x, y, c = _place()
        cps = []
        for d, (fx, fy) in enumerate(CHIP_FLIPS):
            frm = 2 * _flip(x, fx) + _flip(y, fy)
            for k in range(nb):
                theirs = _half(outs[k].at[frm], bufs[k].shape[1], c)
                cps.append(pltpu.make_async_remote_copy(theirs, theirs, send.at[d * nb + k], recv.at[d * nb + k],
                                                        device_id=(x, y, 1 - c), device_id_type=MESH))
        for cp in cps:
            cp.start()
        for cp in cps:
            cp.wait()

    nsem = len(CHIP_FLIPS) * nb
    return pl.pallas_call(
        body, name=name, in_specs=[ANY] * nb, out_specs=[ANY] * nb, out_shape=[_sds(b.shape, b.dtype) for b in bufs],
        input_output_aliases={k: k for k in range(nb)},
        scratch_shapes=[pltpu.SemaphoreType.DMA((nsem,)), pltpu.SemaphoreType.DMA((nsem,))])(*bufs)


def pack_rows(pieces, rows, width, name):
    def body(*refs):
        o_ref = refs[-1]
        o_ref[...] = jnp.zeros_like(o_ref)
        for ref, (a, off) in zip(refs[:-1], pieces):
            o_ref[off:off + a.shape[0], 0:a.shape[1]] = ref[...]

    return pl.pallas_call(body, name=name, out_shape=_sds((rows, width), F32))(*[a for a, _ in pieces])


PEER_FLIPS = tuple((fx, fy, fc) for fx in (0, 1) for fy in (0, 1) for fc in (0, 1) if fx or fy or fc)


def _reduce_copies(parts, zones, pack, send, recv):
    x, y, c = _place()
    nb = len(parts)
    na = nb + (1 if pack is not None else 0)
    cps = []
    for f, (fx, fy, fc) in enumerate(PEER_FLIPS):
        tx, ty, tc = _flip(x, fx), _flip(y, fy), _flip(c, fc)
        for k in range(nb):
            hrows = parts[k].shape[1] // 2
            piece = parts[k].at[2 * tx + ty, pl.ds(pl.multiple_of(tc * hrows, 16), hrows), :]
            cps.append(pltpu.make_async_remote_copy(piece, zones[k].at[f], send.at[f * na + k], recv.at[f * na + k],
                                                    device_id=(tx, ty, tc), device_id_type=MESH))
        if pack is not None:
            mine = pack.at[4 * x + 2 * y + c]
            cps.append(pltpu.make_async_remote_copy(mine, mine, send.at[f * na + nb], recv.at[f * na + nb],
                                                    device_id=(tx, ty, tc), device_id_type=MESH))
    return cps


def reduce_begin(parts, pack, name):
    nb = len(parts)
    zones = [lax.empty((len(PEER_FLIPS), g.shape[1] // 2, g.shape[2]), g.dtype) for g in parts]
    arrs = list(parts) + zones + ([pack] if pack is not None else [])
    na = len(arrs)
    nsem = len(PEER_FLIPS) * (nb + (1 if pack is not None else 0))

    def body(*refs):
        ins = refs[:na]
        send, recv = refs[na], refs[na + 1]
        for cp in _reduce_copies(ins[:nb], ins[nb:2 * nb], ins[2 * nb] if pack is not None else None, send, recv):
            cp.start()
        refs[-1][...] = jnp.zeros_like(refs[-1])

    return pl.pallas_call(
        body, name=name, in_specs=[HBM] * na,
        out_specs=[SEM, SEM] + [HBM] * na + [pl.BlockSpec(memory_space=pltpu.VMEM)],
        out_shape=[pltpu.SemaphoreType.DMA((nsem,)), pltpu.SemaphoreType.DMA((nsem,))]
        + [pltpu.HBM(a.shape, a.dtype) for a in arrs] + [_sds((8, 128), F32)],
        input_output_aliases={k: k + 2 for k in range(na)},
        compiler_params=pltpu.CompilerParams(has_side_effects=EFFECT))(
            *[pltpu.with_memory_space_constraint(a, pltpu.HBM) for a in arrs])


def reduce_end(send, recv, parts, zones, pack, after, name):
    nb = len(parts)
    arrs = list(parts) + list(zones) + ([pack] if pack is not None else [])
    na = len(arrs)

    def body(*refs):
        ins = refs[:na]
        for cp in _reduce_copies(ins[:nb], ins[nb:2 * nb], ins[2 * nb] if pack is not None else None, refs[na], refs[na + 1]):
            cp.wait_send()
            cp.wait_recv()

    return pl.pallas_call(
        body, name=name, in_specs=[HBM] * na + [SEM, SEM, ANY], out_specs=[HBM] * na,
        out_shape=[pltpu.HBM(a.shape, a.dtype) for a in arrs], input_output_aliases={k: k for k in range(na)},
        compiler_params=pltpu.CompilerParams(has_side_effects=EFFECT))(*arrs, send, recv, after)


def sibling_share(halves, name):
    nb = len(halves)

    def body(*refs):
        outs = refs[nb:2 * nb]
        send, recv = refs[2 * nb:]
        x, y, c = _place()
        cps = []
        for k in range(nb):
            hrows = halves[k].shape[0] // 2
            mine = outs[k].at[pl.ds(pl.multiple_of(c * hrows, 8), hrows), :]
            cps.append(pltpu.make_async_remote_copy(mine, mine, send.at[k], recv.at[k], device_id=(x, y, 1 - c),
                                                    device_id_type=MESH))
        for cp in cps:
            cp.start()
        for cp in cps:
            cp.wait()

    return pl.pallas_call(
        body, name=name, in_specs=[ANY] * nb, out_specs=[ANY] * nb,
        out_shape=[_sds(h.shape, h.dtype) for h in halves], input_output_aliases={k: k for k in range(nb)},
        scratch_shapes=[pltpu.SemaphoreType.DMA((nb,)), pltpu.SemaphoreType.DMA((nb,))])(*halves)


def _row_tile(rows):
    for cand in (512, 384, 256, 128, 64, 32, 16):
        if rows % cand == 0:
            return cand
    return rows


def piece_sum(g, z, idx, name):
    _, hrows, W = z.shape
    tr = _row_tile(hrows)
    nrb = hrows // tr

    def body(idx_ref, g_ref, z_ref, o_ref):
        acc = g_ref[...].astype(F32)
        for d in range(z.shape[0]):
            acc = acc + z_ref[d].astype(F32)
        o_ref[...] = acc

    gs = pltpu.PrefetchScalarGridSpec(
        num_scalar_prefetch=1, grid=(nrb,),
        in_specs=[pl.BlockSpec((None, tr, W), lambda i, sc: (sc[0], sc[1] * nrb + i, 0)),
                  pl.BlockSpec((z.shape[0], tr, W), lambda i, sc: (0, i, 0))],
        out_specs=pl.BlockSpec((tr, W), lambda i, sc: (sc[1] * nrb + i, 0)))
    return pl.pallas_call(body, name=name, grid_spec=gs, out_shape=_sds((2 * hrows, W), F32),
                          compiler_params=pltpu.CompilerParams(dimension_semantics=("parallel",),
                                                               vmem_limit_bytes=VMEM_LIMIT_MIB * 2 ** 20))(idx, g, z)


def small_sum(packs):
    n, R, W = packs.shape

    def body(p_ref, o_ref):
        acc = p_ref[0]
        for d in range(1, n):
            acc = acc + p_ref[d]
        o_ref[...] = acc

    return pl.pallas_call(body, name="small_sum", out_shape=_sds((R, W), F32))(packs)


WEIGHTS = ["conv_w_in", "conv_b_in", "conv_w_dw", "conv_b_dw", "conv_ln_g", "conv_ln_b", "conv_w_out", "conv_b_out", "kv_w_k",
           "kv_w_v", "attn_w_q", "attn_sinks", "attn_w_o", "mix_ln_g", "mix_ln_b", "mlp_w_up", "mlp_w_down", "mlp_ln_g",
           "mlp_ln_b", "ple_w_proj", "ple_w_gate"]
BIG = ["conv_w_in", "conv_w_out", "kv_w_k", "kv_w_v", "attn_w_q", "attn_w_o", "mlp_w_up", "mlp_w_down", "ple_w_proj",
       "ple_w_gate"]
SMALL = [n for n in WEIGHTS if n not in BIG]


def _step(x, p, target, w, m, v):
    D = x.shape[-1]
    ds = D // NS
    xq, yq, cq = _place()
    chip = 2 * xq + yq
    idx = jnp.stack([chip, cq]).astype(jnp.int32)

    shards = _split_layers(w)
    lay = _layout(shards)
    taps = w["conv_w_dw"].shape[1]
    small_loc = pack_rows([(w["conv_w_dw"][0], 0), (w["conv_b_dw"], HALO), (w["conv_ln_g"], HALO + 1), (w["conv_ln_b"], HALO + 2),
                           (w["conv_b_out"], HALO + 3), (w["conv_b_in"].reshape(2, ds), HALO + 4)], HALO + 8, ds, "pack_small")
    slot = lambda a: lax.dynamic_update_slice(lax.empty((NS,) + a.shape, a.dtype), a[None], (chip, 0, 0))
    started, token = [], None
    for gi, keys in enumerate(GROUPS):
        bufs = [slot(jnp.concatenate([shards[n].astype(BF16) for n, _, _ in lay[key]], axis=0)) for key in keys]
        if gi == 0:
            bufs.append(slot(small_loc))
        send, recv, *thru, token = gather_start(bufs, len(bufs) if gi == 0 else 0, token, "gather_start%d" % gi)
        started.append((send, recv, thru))
    W = {}

    def arrive(gi, after):
        send, recv, thru = started[gi]
        whole = len(thru) if gi == 0 else 0
        got = gather_wait(send, recv, thru, whole, after, "gather_wait%d" % gi)
        nk = len(GROUPS[gi])
        for key, buf in zip(GROUPS[gi], got[:nk] if gi == 0 else sibling_forward(got[:nk], "sibling_forward%d" % gi)):
            for n, off, rows in lay[key]:
                W[n] = (buf, off, rows)
        return got[nk:]

    gs, = arrive(0, token)
    across = lambda rows: gs[:, rows, :].transpose(1, 0, 2).reshape(rows.stop - rows.start, D)
    small = {"taps": taps, "conv_w_dw": across(slice(0, HALO)), "conv_b_dw": across(slice(HALO, HALO + 1)),
             "conv_ln_g": across(slice(HALO + 1, HALO + 2)), "conv_ln_b": across(slice(HALO + 2, HALO + 3)),
             "conv_b_out": across(slice(HALO + 3, HALO + 4)), "conv_b_in": gs[:, HALO + 4:HALO + 6, :].reshape(1, 2 * D),
             "attn_sinks": w["attn_sinks"], "mix_ln_g": w["mix_ln_g"], "mix_ln_b": w["mix_ln_b"],
             "mlp_ln_g": w["mlp_ln_g"], "mlp_ln_b": w["mlp_ln_b"]}

    reducing = {}

    def reduce_start(gi, G, pack):
        nk = len(REDUCED[gi])
        send, recv, *thru, token = reduce_begin([G[key] for key in REDUCED[gi]], pack, "reduce_begin%d" % gi)
        reducing[gi] = (send, recv, thru[:nk], thru[nk:2 * nk], thru[2 * nk] if pack is not None else None)
        _FOLLOW.append(token)

    def small_pack(sg):
        pieces = [(sg["conv_b_in"].reshape(2, D), 0), (sg["conv_w_dw"], 2)]
        r0 = 2 + HALO
        for i, n in enumerate(("conv_b_dw", "conv_ln_g", "conv_ln_b", "conv_b_out")):
            pieces.append((sg[n], r0 + i))
        r0 += 4
        for i, n in enumerate(("mix_ln_g", "mix_ln_b", "mlp_ln_g", "mlp_ln_b")):
            pieces += [(sg[n][0], r0 + 2 * i), (sg[n][1], r0 + 2 * i + 1)]
        pieces += [(sg["attn_sinks"], r0 + 8), (sg["loss"][0:1], r0 + 9)]
        mine = pack_rows(pieces, r0 + 10, D, "pack_small_grads")
        return lax.dynamic_update_slice(lax.empty((8,) + mine.shape, F32), mine[None], (4 * xq + 2 * yq + cq, 0, 0))

    def hook(stage, after, G, sg=None):
        if stage == "weights1":
            arrive(1, after)
        elif stage == "weights2":
            arrive(2, after)
        elif stage == "grads0":
            reduce_start(0, G, small_pack(sg))
        elif stage.startswith("grads"):
            reduce_start(int(stage[5:]), G, None)

    _, grad_x, G, sg = _local_step(x[0], p[:, 0], target[0], W, small, lay, hook)
    _FOLLOW.clear()
    nsink = w["attn_sinks"].shape[1]

    grads, delta, new_m, new_v = {}, {}, {}, {}
    found = {}

    def finish(groups, after, tag):
        keys, halves, tot = [], [], None
        for gi in groups:
            send, recv, parts, zones, pack = reducing[gi]
            done = reduce_end(send, recv, parts, zones, pack, after, "reduce_end%d" % gi)
            nk = len(REDUCED[gi])
            for key, g_, z_ in zip(REDUCED[gi], done[:nk], done[nk:2 * nk]):
                keys.append(key)
                halves.append(piece_sum(g_, z_, idx, "piece_sum_" + key))
            if pack is not None:
                tot = small_sum(done[2 * nk])
        for key, buf in zip(keys, sibling_share(halves, "sibling_share" + tag)):
            for n, off, _ in lay[key]:
                found[n] = (buf, off)
        return tot

    def big_adamw(names):
        for n in names:
            three = lambda a: a.reshape((-1,) + a.shape[-2:])
            w3, m3, v3 = three(w[n]), three(m[n]), three(v[n])
            outs = None
            for i in range(w3.shape[0]):
                buf, off = found[n + str(i)] if n + str(i) in found else found[n]
                outs = adamw_layer(w3, m3, v3, i, buf, off, outs, "adamw_%s%d" % (n, i))
            grads[n], delta[n], new_m[n], new_v[n] = [a.reshape(w[n].shape) for a in outs]

    last = [n for n, _, _ in lay[REDUCED[0][0]]]
    finish(reversed(range(1, len(REDUCED))), grad_x, "1")
    big_adamw([n for n in BIG if n not in last])
    tot = finish([0], new_v["mlp_w_down"], "0")
    big_adamw(last)
    cols = lambda rows: lax.dynamic_slice(rows, (0, chip * ds), (rows.shape[0], ds))
    grads["conv_b_in"] = lax.dynamic_slice(tot[0:2].reshape(1, 2 * D), (0, chip * 2 * ds), (1, 2 * ds))
    grads["conv_w_dw"] = cols(tot[2:2 + taps])[None]
    r0 = 2 + HALO
    for i, n in enumerate(("conv_b_dw", "conv_ln_g", "conv_ln_b", "conv_b_out")):
        grads[n] = cols(tot[r0 + i:r0 + i + 1])
    r0 += 4
    for i, n in enumerate(("mix_ln_g", "mix_ln_b", "mlp_ln_g", "mlp_ln_b")):
        grads[n] = tot[r0 + 2 * i:r0 + 2 * i + 2]
    grads["attn_sinks"] = tot[r0 + 8:r0 + 9, 0:nsink]

    ds_, ms_, vs_ = adamw_many([w[n] for n in SMALL], [grads[n] for n in SMALL], [m[n] for n in SMALL], [v[n] for n in SMALL])
    for n, d_, m_, v_ in zip(SMALL, ds_, ms_, vs_):
        delta[n], new_m[n], new_v[n] = d_, m_, v_

    total = tot[r0 + 9, 0]
    return (total, grad_x[None], *[grads[n] for n in WEIGHTS], *[delta[n] for n in WEIGHTS], *[new_m[n] for n in WEIGHTS],
            *[new_v[n] for n in WEIGHTS])


def kernel(x, p, conv_w_in, conv_b_in, conv_w_dw, conv_b_dw, conv_ln_g, conv_ln_b, conv_w_out, conv_b_out, kv_w_k, kv_w_v, attn_w_q, attn_sinks, attn_w_o, mix_ln_g, mix_ln_b, mlp_w_up, mlp_w_down, mlp_ln_g, mlp_ln_b, ple_w_proj, ple_w_gate, loss_target, m_conv_w_in, m_conv_b_in, m_conv_w_dw, m_conv_b_dw, m_conv_ln_g, m_conv_ln_b, m_conv_w_out, m_conv_b_out, m_kv_w_k, m_kv_w_v, m_attn_w_q, m_attn_sinks, m_attn_w_o, m_mix_ln_g, m_mix_ln_b, m_mlp_w_up, m_mlp_w_down, m_mlp_ln_g, m_mlp_ln_b, m_ple_w_proj, m_ple_w_gate, v_conv_w_in, v_conv_b_in, v_conv_w_dw, v_conv_b_dw, v_conv_ln_g, v_conv_ln_b, v_conv_w_out, v_conv_b_out, v_kv_w_k, v_kv_w_v, v_attn_w_q, v_attn_sinks, v_attn_w_o, v_mix_ln_g, v_mix_ln_b, v_mlp_w_up, v_mlp_w_down, v_mlp_ln_g, v_mlp_ln_b, v_ple_w_proj, v_ple_w_gate):
    w = dict(zip(WEIGHTS, (conv_w_in, conv_b_in, conv_w_dw, conv_b_dw, conv_ln_g, conv_ln_b, conv_w_out, conv_b_out, kv_w_k,
                           kv_w_v, attn_w_q, attn_sinks, attn_w_o, mix_ln_g, mix_ln_b, mlp_w_up, mlp_w_down, mlp_ln_g, mlp_ln_b,
                           ple_w_proj, ple_w_gate)))
    m = dict(zip(WEIGHTS, (m_conv_w_in, m_conv_b_in, m_conv_w_dw, m_conv_b_dw, m_conv_ln_g, m_conv_ln_b, m_conv_w_out,
                           m_conv_b_out, m_kv_w_k, m_kv_w_v, m_attn_w_q, m_attn_sinks, m_attn_w_o, m_mix_ln_g, m_mix_ln_b,
                           m_mlp_w_up, m_mlp_w_down, m_mlp_ln_g, m_mlp_ln_b, m_ple_w_proj, m_ple_w_gate)))
    v = dict(zip(WEIGHTS, (v_conv_w_in, v_conv_b_in, v_conv_w_dw, v_conv_b_dw, v_conv_ln_g, v_conv_ln_b, v_conv_w_out,
                           v_conv_b_out, v_kv_w_k, v_kv_w_v, v_attn_w_q, v_attn_sinks, v_attn_w_o, v_mix_ln_g, v_mix_ln_b,
                           v_mlp_w_up, v_mlp_w_down, v_mlp_ln_g, v_mlp_ln_b, v_ple_w_proj, v_ple_w_gate)))
    return _step(x, p, loss_target, w, m, v)
```
